```python
import math
import jax, jax.numpy as jnp
from jax import lax
import numpy as np

D_MODEL = 1024
BATCH = 8
SEQ = 4096
DEPTH = 2

CTX_LEN = 256
GRID_W = 64

SSD_D_INNER = 2 * D_MODEL
SSD_HEADDIM = 64
SSD_HEADS = SSD_D_INNER // SSD_HEADDIM
SSD_GROUPS = 8
SSD_HPG = SSD_HEADS // SSD_GROUPS
SSD_STATE = 128
SSD_CONV = 5
SSD_CHUNK = 128
SSD_GN = SSD_GROUPS * SSD_STATE
XBC_WIDTH = SSD_D_INNER + 2 * SSD_GN

POOL_WIDTH = D_MODEL
POOL_WINDOWS = (2, 4, 8, 16)
POOL_GROUP = POOL_WIDTH // len(POOL_WINDOWS)

FFN_HIDDEN = -(-(8 * D_MODEL) // (3 * 256)) * 256

IN_COLS = SSD_D_INNER + XBC_WIDTH + 2 * SSD_HEADS + POOL_WIDTH + 2 * D_MODEL
IN_SPLITS = (SSD_D_INNER,
             SSD_D_INNER + XBC_WIDTH,
             SSD_D_INNER + XBC_WIDTH + 2 * SSD_HEADS,
             SSD_D_INNER + XBC_WIDTH + 2 * SSD_HEADS + POOL_WIDTH)
EPS = 1e-6

kernel_name = "hybrid_ssd_pool_dit_block"


def rmsnorm(x, g):
    xf = x.astype(jnp.float32)
    y = xf * lax.rsqrt(jnp.mean(xf * xf, axis=-1, keepdims=True) + EPS)
    return (y * g).astype(x.dtype)


def modulate(h, shift, scale):
    return h * (1.0 + scale) + shift


def adaln(cond, w, b):
    m = jax.nn.silu(cond) @ w + b
    return jnp.split(m[:, None, :], 6, axis=-1)


def dwconv_centred(u, w, b):
    k = w.shape[0]
    pad = k // 2
    L = u.shape[1]
    up = jnp.pad(u, ((0, 0), (pad, pad), (0, 0)))
    out = b
    for i in range(k):
        out = out + w[i] * up[:, i:i + L]
    return out


def pool_minus_self(u):
    L = u.shape[-2]
    uf = u.astype(jnp.float32)
    cs = jnp.cumsum(uf, axis=-2)
    cs = jnp.concatenate([jnp.zeros_like(cs[..., :1, :]), cs], axis=-2)
    t = jnp.arange(L)
    outs = []
    for gi, k in enumerate(POOL_WINDOWS):
        lo = jnp.clip(t - k // 2, 0, L)
        hi = jnp.clip(t + k // 2, 0, L)
        seg = cs[..., gi * POOL_GROUP:(gi + 1) * POOL_GROUP]
        s = jnp.take(seg, hi, axis=-2) - jnp.take(seg, lo, axis=-2)
        outs.append(s / (hi - lo).astype(jnp.float32)[:, None])
    mean = jnp.concatenate(outs, axis=-1)
    return (mean - uf).astype(u.dtype)


def ssd_scan(xh, dt, a_log, bm, cm, init_state, return_y):
    Bsz, L, H, P = xh.shape
    G, N = bm.shape[2], bm.shape[3]
    R = H // G
    Q = SSD_CHUNK
    nc = L // Q
    A = -jnp.exp(a_log.astype(jnp.float32))
    acum = jnp.cumsum((dt * A).reshape(Bsz, nc, Q, H), axis=2)
    xc = (xh.astype(jnp.float32) * dt[..., None]).reshape(Bsz, nc, Q, G, R, P)
    bc = bm.astype(jnp.float32).reshape(Bsz, nc, Q, G, N)
    cc = cm.astype(jnp.float32).reshape(Bsz, nc, Q, G, N)
    a_last = acum[:, :, -1]
    decay_to_end = jnp.exp(a_last[:, :, None, :] - acum).reshape(Bsz, nc, Q, G, R)
    chunk_states = jnp.einsum('bcjgn,bcjgr,bcjgrp->bcgrpn', bc, decay_to_end, xc)

    def step(state, inp):
        cs, al = inp
        return state * jnp.exp(al)[..., None, None] + cs, state

    final, entering = lax.scan(
        step, init_state.reshape(Bsz, G, R, P, N),
        (jnp.moveaxis(chunk_states, 1, 0), jnp.moveaxis(a_last.reshape(Bsz, nc, G, R), 1, 0)))
    final = final.reshape(Bsz, H, P, N)
    if not return_y:
        return None, final
    entering = jnp.moveaxis(entering, 0, 1)
    mask = jnp.tril(jnp.ones((Q, Q), dtype=bool))[None, None, :, :, None]
    diff = acum[:, :, :, None, :] - acum[:, :, None, :, :]
    decay = jnp.exp(jnp.where(mask, diff, -jnp.inf)).reshape(Bsz, nc, Q, Q, G, R)
    cb = jnp.einsum('bcign,bcjgn->bcijg', cc, bc)
    y_intra = jnp.einsum('bcijgr,bcjgrp->bcigrp', cb[..., None] * decay, xc)
    y_inter = (jnp.einsum('bcign,bcgrpn->bcigrp', cc, entering)
               * jnp.exp(acum).reshape(Bsz, nc, Q, G, R)[..., None])
    y = (y_intra + y_inter).reshape(Bsz, L, H, P)
    return y.astype(xh.dtype), final


def ssd_bidir(xs, dt, bm, cm, a_log, d_skip, init_states, return_y):
    ys = []
    finals = []
    for d in range(2):
        flip = (lambda a: jnp.flip(a, axis=1)) if d == 1 else (lambda a: a)
        y, s = ssd_scan(flip(xs), flip(dt[:, :, d]), a_log[d], flip(bm), flip(cm), init_states[d], return_y)
        finals.append(s)
        if return_y:
            ys.append(flip(y) + d_skip[d][:, None] * xs)
    y = (ys[0] + ys[1]) if return_y else None
    return y, finals


def prepare(h, w_in, conv_w, conv_b, dt_bias):
    Bsz, L = h.shape[0], h.shape[1]
    proj = h @ w_in
    z, xbc, dt_raw, u_pool, gate_logits = jnp.split(proj, IN_SPLITS, axis=-1)
    xbc = jax.nn.silu(dwconv_centred(xbc, conv_w, conv_b))
    xs, bm, cm = jnp.split(xbc, (SSD_D_INNER, SSD_D_INNER + SSD_GN), axis=-1)
    xs = xs.reshape(Bsz, L, SSD_HEADS, SSD_HEADDIM)
    bm = bm.reshape(Bsz, L, SSD_GROUPS, SSD_STATE)
    cm = cm.reshape(Bsz, L, SSD_GROUPS, SSD_STATE)
    dt = jax.nn.softplus((dt_raw.reshape(Bsz, L, 2, SSD_HEADS) + dt_bias).astype(jnp.float32))
    return z, xs, bm, cm, dt, u_pool, gate_logits


def mixer_out(z, y, u_pool, gate_logits, ssd_norm_w, w_ssd_out, pool_w, pool_scale, w_pool_out, w_out, on_grid):
    Bsz, L = z.shape[0], z.shape[1]
    yz = y.reshape(Bsz, L, SSD_D_INNER) * jax.nn.silu(z)
    yn = rmsnorm(yz.reshape(Bsz, L, SSD_GROUPS, SSD_D_INNER // SSD_GROUPS), 1.0).reshape(Bsz, L, SSD_D_INNER)
    o_ssd = (yn * ssd_norm_w) @ w_ssd_out
    if on_grid:
        rows = L // GRID_W
        pm = pool_minus_self(u_pool.reshape(Bsz, rows, GRID_W, POOL_WIDTH)).reshape(Bsz, L, POOL_WIDTH)
    else:
        pm = pool_minus_self(u_pool)
    pm = jnp.einsum('blgi,gio->blgo', pm.reshape(Bsz, L, len(POOL_WINDOWS), POOL_GROUP), pool_w)
    o_pool = (pm.reshape(Bsz, L, POOL_WIDTH) * pool_scale) @ w_pool_out
    g_ssd, g_pool = jnp.split(jax.nn.sigmoid(gate_logits), 2, axis=-1)
    return (g_ssd * o_ssd + g_pool * o_pool) @ w_out


def swiglu(h, w_gate_up, w_down):
    a, b = jnp.split(h @ w_gate_up, 2, axis=-1)
    return (jax.nn.silu(a) * b) @ w_down


def _fwd_setup_inputs(seed: int = 0) -> dict:
    key = jax.random.key(seed)
    ks = jax.random.split(key, 24)
    f32 = jnp.float32

    def nrm(k, shape, scale):
        return jax.random.normal(k, shape, f32) * scale

    H = SSD_HEADS
    dt0 = jnp.exp(jax.random.uniform(ks[10], (DEPTH, 2, H), f32, minval=math.log(1e-3), maxval=math.log(1e-1)))
    return {
        "x": nrm(ks[0], (BATCH, SEQ, D_MODEL), 1.0),
        "c": nrm(ks[1], (BATCH, D_MODEL), 1.0),
        "ctx": nrm(ks[2], (BATCH, CTX_LEN, D_MODEL), 1.0),
        "c_ctx": nrm(ks[3], (D_MODEL,), 1.0),
        "w_ada": nrm(ks[4], (DEPTH, D_MODEL, 6 * D_MODEL), 0.5 * D_MODEL ** -0.5),
        "b_ada": nrm(ks[5], (DEPTH, 6 * D_MODEL), 0.02),
        "g_mix": 1.0 + nrm(ks[6], (DEPTH, D_MODEL), 0.05),
        "w_in": nrm(ks[7], (DEPTH, D_MODEL, IN_COLS), D_MODEL ** -0.5),
        "conv_w": nrm(ks[8], (DEPTH, SSD_CONV, XBC_WIDTH), SSD_CONV ** -0.5),
        "conv_b": nrm(ks[9], (DEPTH, XBC_WIDTH), 0.02),
        "dt_bias": dt0 + jnp.log(-jnp.expm1(-dt0)),
        "a_log": jnp.log(jax.random.uniform(ks[11], (DEPTH, 2, H), f32, minval=1.0, maxval=16.0)),
        "d_skip": 1.0 + nrm(ks[12], (DEPTH, 2, H), 0.05),
        "ssd_norm_w": 1.0 + nrm(ks[13], (DEPTH, SSD_D_INNER), 0.05),
        "w_ssd_out": nrm(ks[14], (DEPTH, SSD_D_INNER, D_MODEL), SSD_D_INNER ** -0.5),
        "pool_w": nrm(ks[15], (DEPTH, len(POOL_WINDOWS), POOL_GROUP, POOL_GROUP), POOL_GROUP ** -0.5),
        "pool_scale": 1.0 + nrm(ks[16], (DEPTH, POOL_WIDTH), 0.1),
        "w_pool_out": nrm(ks[17], (DEPTH, POOL_WIDTH, D_MODEL), POOL_WIDTH ** -0.5),
        "w_out": nrm(ks[18], (DEPTH, D_MODEL, D_MODEL), D_MODEL ** -0.5),
        "g_ffn": 1.0 + nrm(ks[19], (DEPTH, D_MODEL), 0.05),
        "w_gate_up": nrm(ks[20], (DEPTH, D_MODEL, 2 * FFN_HIDDEN), D_MODEL ** -0.5),
        "w_down": nrm(ks[21], (DEPTH, FFN_HIDDEN, D_MODEL), FFN_HIDDEN ** -0.5),
        "g_final": 1.0 + nrm(ks[22], (D_MODEL,), 0.05),
    }


def _fwd_reference(x, c, ctx, c_ctx, w_ada, b_ada, g_mix, w_in, conv_w, conv_b, dt_bias, a_log, d_skip,
              ssd_norm_w, w_ssd_out, pool_w, pool_scale, w_pool_out, w_out, g_ffn, w_gate_up, w_down,
              g_final):
    cx = ctx
    for l in range(DEPTH):
        last = l == DEPTH - 1
        sh1, sc1, ga1, sh2, sc2, ga2 = adaln(c, w_ada[l], b_ada[l])
        csh1, csc1, cga1, csh2, csc2, cga2 = adaln(c_ctx[None, :], w_ada[l], b_ada[l])

        h_lat = modulate(rmsnorm(x, g_mix[l]), sh1, sc1)
        h_ctx = modulate(rmsnorm(cx, g_mix[l]), csh1, csc1)
        zc, xsc, bmc, cmc, dtc, upc, glc = prepare(h_ctx, w_in[l], conv_w[l], conv_b[l], dt_bias[l])
        zl, xsl, bml, cml, dtl, upl, gll = prepare(h_lat, w_in[l], conv_w[l], conv_b[l], dt_bias[l])
        zero = jnp.zeros((xsc.shape[0], SSD_HEADS, SSD_HEADDIM, SSD_STATE), jnp.float32)
        y_ctx, ctx_states = ssd_bidir(xsc, dtc, bmc, cmc, a_log[l], d_skip[l], (zero, zero), not last)
        y_lat, _ = ssd_bidir(xsl, dtl, bml, cml, a_log[l], d_skip[l], ctx_states, True)
        x = x + ga1 * mixer_out(zl, y_lat, upl, gll, ssd_norm_w[l], w_ssd_out[l], pool_w[l], pool_scale[l],
                                w_pool_out[l], w_out[l], True)
        x = x + ga2 * swiglu(modulate(rmsnorm(x, g_ffn[l]), sh2, sc2), w_gate_up[l], w_down[l])

        if not last:
            cx = cx + cga1 * mixer_out(zc, y_ctx, upc, glc, ssd_norm_w[l], w_ssd_out[l], pool_w[l],
                                       pool_scale[l], w_pool_out[l], w_out[l], False)
            cx = cx + cga2 * swiglu(modulate(rmsnorm(cx, g_ffn[l]), csh2, csc2), w_gate_up[l], w_down[l])
    return rmsnorm(x, g_final)


import jax as _jax
import jax.numpy as _jnp

TWIN_FORMAT = 'train_step'
FWD_PARAMS = ['x', 'c', 'ctx', 'c_ctx', 'w_ada', 'b_ada', 'g_mix', 'w_in', 'conv_w', 'conv_b', 'dt_bias', 'a_log', 'd_skip', 'ssd_norm_w', 'w_ssd_out', 'pool_w', 'pool_scale', 'w_pool_out', 'w_out', 'g_ffn', 'w_gate_up', 'w_down', 'g_final']
TWIN_WEIGHTS = ['c_ctx', 'w_ada', 'b_ada', 'g_mix', 'w_in', 'conv_w', 'conv_b', 'dt_bias', 'a_log', 'd_skip', 'ssd_norm_w', 'w_ssd_out', 'pool_w', 'pool_scale', 'w_pool_out', 'w_out', 'g_ffn', 'w_gate_up', 'w_down', 'g_final']
TWIN_DIFF_INPUT = 'x'
TWIN_INPUTS = ['x', 'c', 'ctx', 'c_ctx', 'w_ada', 'b_ada', 'g_mix', 'w_in', 'conv_w', 'conv_b', 'dt_bias', 'a_log', 'd_skip', 'ssd_norm_w', 'w_ssd_out', 'pool_w', 'pool_scale', 'w_pool_out', 'w_out', 'g_ffn', 'w_gate_up', 'w_down', 'g_final', 'loss_target', 'm_c_ctx', 'm_w_ada', 'm_b_ada', 'm_g_mix', 'm_w_in', 'm_conv_w', 'm_conv_b', 'm_dt_bias', 'm_a_log', 'm_d_skip', 'm_ssd_norm_w', 'm_w_ssd_out', 'm_pool_w', 'm_pool_scale', 'm_w_pool_out', 'm_w_out', 'm_g_ffn', 'm_w_gate_up', 'm_w_down', 'm_g_final', 'v_c_ctx', 'v_w_ada', 'v_b_ada', 'v_g_mix', 'v_w_in', 'v_conv_w', 'v_conv_b', 'v_dt_bias', 'v_a_log', 'v_d_skip', 'v_ssd_norm_w', 'v_w_ssd_out', 'v_pool_w', 'v_pool_scale', 'v_w_pool_out', 'v_w_out', 'v_g_ffn', 'v_w_gate_up', 'v_w_down', 'v_g_final']
TWIN_OUTPUTS = ['loss', 'grad_x', 'grad_c_ctx', 'grad_w_ada', 'grad_b_ada', 'grad_g_mix', 'grad_w_in', 'grad_conv_w', 'grad_conv_b', 'grad_dt_bias', 'grad_a_log', 'grad_d_skip', 'grad_ssd_norm_w', 'grad_w_ssd_out', 'grad_pool_w', 'grad_pool_scale', 'grad_w_pool_out', 'grad_w_out', 'grad_g_ffn', 'grad_w_gate_up', 'grad_w_down', 'grad_g_final', 'delta_c_ctx', 'delta_w_ada', 'delta_b_ada', 'delta_g_mix', 'delta_w_in', 'delta_conv_w', 'delta_conv_b', 'delta_dt_bias', 'delta_a_log', 'delta_d_skip', 'delta_ssd_norm_w', 'delta_w_ssd_out', 'delta_pool_w', 'delta_pool_scale', 'delta_w_pool_out', 'delta_w_out', 'delta_g_ffn', 'delta_w_gate_up', 'delta_w_down', 'delta_g_final', 'new_m_c_ctx', 'new_m_w_ada', 'new_m_b_ada', 'new_m_g_mix', 'new_m_w_in', 'new_m_conv_w', 'new_m_conv_b', 'new_m_dt_bias', 'new_m_a_log', 'new_m_d_skip', 'new_m_ssd_norm_w', 'new_m_w_ssd_out', 'new_m_pool_w', 'new_m_pool_scale', 'new_m_w_pool_out', 'new_m_w_out', 'new_m_g_ffn', 'new_m_w_gate_up', 'new_m_w_down', 'new_m_g_final', 'new_v_c_ctx', 'new_v_w_ada', 'new_v_b_ada', 'new_v_g_mix', 'new_v_w_in', 'new_v_conv_w', 'new_v_conv_b', 'new_v_dt_bias', 'new_v_a_log', 'new_v_d_skip', 'new_v_ssd_norm_w', 'new_v_w_ssd_out', 'new_v_pool_w', 'new_v_pool_scale', 'new_v_w_pool_out', 'new_v_w_out', 'new_v_g_ffn', 'new_v_w_gate_up', 'new_v_w_down', 'new_v_g_final']
TWIN_LEAF_KINDS = {'loss': 'loss', 'grad_x': 'grad_x', 'grad_c_ctx': 'grad_w', 'grad_w_ada': 'grad_w', 'grad_b_ada': 'grad_w', 'grad_g_mix': 'grad_w', 'grad_w_in': 'grad_w', 'grad_conv_w': 'grad_w', 'grad_conv_b': 'grad_w', 'grad_dt_bias': 'grad_w', 'grad_a_log': 'grad_w', 'grad_d_skip': 'grad_w', 'grad_ssd_norm_w': 'grad_w', 'grad_w_ssd_out': 'grad_w', 'grad_pool_w': 'grad_w', 'grad_pool_scale': 'grad_w', 'grad_w_pool_out': 'grad_w', 'grad_w_out': 'grad_w', 'grad_g_ffn': 'grad_w', 'grad_w_gate_up': 'grad_w', 'grad_w_down': 'grad_w', 'grad_g_final': 'grad_w', 'delta_c_ctx': 'delta_w', 'delta_w_ada': 'delta_w', 'delta_b_ada': 'delta_w', 'delta_g_mix': 'delta_w', 'delta_w_in': 'delta_w', 'delta_conv_w': 'delta_w', 'delta_conv_b': 'delta_w', 'delta_dt_bias': 'delta_w', 'delta_a_log': 'delta_w', 'delta_d_skip': 'delta_w', 'delta_ssd_norm_w': 'delta_w', 'delta_w_ssd_out': 'delta_w', 'delta_pool_w': 'delta_w', 'delta_pool_scale': 'delta_w', 'delta_w_pool_out': 'delta_w', 'delta_w_out': 'delta_w', 'delta_g_ffn': 'delta_w', 'delta_w_gate_up': 'delta_w', 'delta_w_down': 'delta_w', 'delta_g_final': 'delta_w', 'new_m_c_ctx': 'new_m', 'new_m_w_ada': 'new_m', 'new_m_b_ada': 'new_m', 'new_m_g_mix': 'new_m', 'new_m_w_in': 'new_m', 'new_m_conv_w': 'new_m', 'new_m_conv_b': 'new_m', 'new_m_dt_bias': 'new_m', 'new_m_a_log': 'new_m', 'new_m_d_skip': 'new_m', 'new_m_ssd_norm_w': 'new_m', 'new_m_w_ssd_out': 'new_m', 'new_m_pool_w': 'new_m', 'new_m_pool_scale': 'new_m', 'new_m_w_pool_out': 'new_m', 'new_m_w_out': 'new_m', 'new_m_g_ffn': 'new_m', 'new_m_w_gate_up': 'new_m', 'new_m_w_down': 'new_m', 'new_m_g_final': 'new_m', 'new_v_c_ctx': 'new_v', 'new_v_w_ada': 'new_v', 'new_v_b_ada': 'new_v', 'new_v_g_mix': 'new_v', 'new_v_w_in': 'new_v', 'new_v_conv_w': 'new_v', 'new_v_conv_b': 'new_v', 'new_v_dt_bias': 'new_v', 'new_v_a_log': 'new_v', 'new_v_d_skip': 'new_v', 'new_v_ssd_norm_w': 'new_v', 'new_v_w_ssd_out': 'new_v', 'new_v_pool_w': 'new_v', 'new_v_pool_scale': 'new_v', 'new_v_w_pool_out': 'new_v', 'new_v_w_out': 'new_v', 'new_v_g_ffn': 'new_v', 'new_v_w_gate_up': 'new_v', 'new_v_w_down': 'new_v', 'new_v_g_final': 'new_v'}


def _forward(args):
    return _fwd_reference(*[args[k] for k in FWD_PARAMS])


def _output_shape():
    def fwd():
        inp = _fwd_setup_inputs(0)
        return _fwd_reference(*[inp[k] for k in FWD_PARAMS])
    out = _jax.eval_shape(fwd)
    return out.shape, out.dtype

N_MICROBATCH = 1
ADAM_LR = 0.001
ADAM_B1 = 0.9
ADAM_B2 = 0.999
ADAM_EPS = 1e-08
ADAM_WD = 0.01
ADAM_STEP = 10
PER_EXAMPLE_BATCH_AXIS = {'x': 0, 'c': 0, 'ctx': 0, 'loss_target': 0}
SHARED_INPUTS = []
_WEIGHT_DTYPES = {'c_ctx': _jnp.float32, 'w_ada': _jnp.float32, 'b_ada': _jnp.float32, 'g_mix': _jnp.float32, 'w_in': _jnp.float32, 'conv_w': _jnp.float32, 'conv_b': _jnp.float32, 'dt_bias': _jnp.float32, 'a_log': _jnp.float32, 'd_skip': _jnp.float32, 'ssd_norm_w': _jnp.float32, 'w_ssd_out': _jnp.float32, 'pool_w': _jnp.float32, 'pool_scale': _jnp.float32, 'w_pool_out': _jnp.float32, 'w_out': _jnp.float32, 'g_ffn': _jnp.float32, 'w_gate_up': _jnp.float32, 'w_down': _jnp.float32, 'g_final': _jnp.float32}
MOMENT_SCALE = {'c_ctx': 2.345721e-03, 'w_ada': 6.468017e-02, 'b_ada': 1.242078e-01, 'g_mix': 5.387687e-02, 'w_in': 1.763887e-02, 'conv_w': 1.540575e-02, 'conv_b': 2.072593e-02, 'dt_bias': 2.771930e-02, 'a_log': 5.129992e-02, 'd_skip': 4.444446e-02, 'ssd_norm_w': 2.230890e-02, 'w_ssd_out': 2.936299e-02, 'pool_w': 2.695959e-02, 'pool_scale': 2.665600e-02, 'w_pool_out': 2.714249e-02, 'w_out': 4.009206e-02, 'g_ffn': 5.219234e-02, 'w_gate_up': 2.277357e-02, 'w_down': 3.709451e-02, 'g_final': 3.207735e+01}


def _to_microbatches(a, axis):
    t = _jnp.moveaxis(a, axis, 0)
    t = t.reshape((N_MICROBATCH, t.shape[0] // N_MICROBATCH) + t.shape[1:])
    return _jnp.moveaxis(t, 1, axis + 1)


def setup_inputs(seed: int = 0) -> dict:
    inp = _fwd_setup_inputs(seed)
    key = _jax.random.fold_in(_jax.random.key(seed), 7919)
    shape, _ = _output_shape()
    out = dict(inp)
    out["loss_target"] = _jax.random.normal(_jax.random.fold_in(key, 0), shape, _jnp.float32)
    for i, name in enumerate(TWIN_WEIGHTS):
        w = inp[name].astype(_jnp.float32)
        if MOMENT_SCALE is None:
            s = _jnp.sqrt(_jnp.mean(_jnp.square(w)) + 1e-30)
        else:
            s = MOMENT_SCALE[name]
        km, kv = _jax.random.split(_jax.random.fold_in(key, i + 1))
        out[name] = w
        out["m_" + name] = s * _jax.random.normal(km, w.shape, _jnp.float32)
        out["v_" + name] = (s * s) * _jax.random.uniform(kv, w.shape, _jnp.float32, 0.5, 1.5)
    if N_MICROBATCH > 1:
        for name, axis in PER_EXAMPLE_BATCH_AXIS.items():
            out[name] = _to_microbatches(out[name], axis)
    return {'x': out['x'], 'c': out['c'], 'ctx': out['ctx'], 'c_ctx': out['c_ctx'], 'w_ada': out['w_ada'], 'b_ada': out['b_ada'], 'g_mix': out['g_mix'], 'w_in': out['w_in'], 'conv_w': out['conv_w'], 'conv_b': out['conv_b'], 'dt_bias': out['dt_bias'], 'a_log': out['a_log'], 'd_skip': out['d_skip'], 'ssd_norm_w': out['ssd_norm_w'], 'w_ssd_out': out['w_ssd_out'], 'pool_w': out['pool_w'], 'pool_scale': out['pool_scale'], 'w_pool_out': out['w_pool_out'], 'w_out': out['w_out'], 'g_ffn': out['g_ffn'], 'w_gate_up': out['w_gate_up'], 'w_down': out['w_down'], 'g_final': out['g_final'], 'loss_target': out['loss_target'], 'm_c_ctx': out['m_c_ctx'], 'm_w_ada': out['m_w_ada'], 'm_b_ada': out['m_b_ada'], 'm_g_mix': out['m_g_mix'], 'm_w_in': out['m_w_in'], 'm_conv_w': out['m_conv_w'], 'm_conv_b': out['m_conv_b'], 'm_dt_bias': out['m_dt_bias'], 'm_a_log': out['m_a_log'], 'm_d_skip': out['m_d_skip'], 'm_ssd_norm_w': out['m_ssd_norm_w'], 'm_w_ssd_out': out['m_w_ssd_out'], 'm_pool_w': out['m_pool_w'], 'm_pool_scale': out['m_pool_scale'], 'm_w_pool_out': out['m_w_pool_out'], 'm_w_out': out['m_w_out'], 'm_g_ffn': out['m_g_ffn'], 'm_w_gate_up': out['m_w_gate_up'], 'm_w_down': out['m_w_down'], 'm_g_final': out['m_g_final'], 'v_c_ctx': out['v_c_ctx'], 'v_w_ada': out['v_w_ada'], 'v_b_ada': out['v_b_ada'], 'v_g_mix': out['v_g_mix'], 'v_w_in': out['v_w_in'], 'v_conv_w': out['v_conv_w'], 'v_conv_b': out['v_conv_b'], 'v_dt_bias': out['v_dt_bias'], 'v_a_log': out['v_a_log'], 'v_d_skip': out['v_d_skip'], 'v_ssd_norm_w': out['v_ssd_norm_w'], 'v_w_ssd_out': out['v_w_ssd_out'], 'v_pool_w': out['v_pool_w'], 'v_pool_scale': out['v_pool_scale'], 'v_w_pool_out': out['v_w_pool_out'], 'v_w_out': out['v_w_out'], 'v_g_ffn': out['v_g_ffn'], 'v_w_gate_up': out['v_w_gate_up'], 'v_w_down': out['v_w_down'], 'v_g_final': out['v_g_final']}


def _loss(weights, diff, rest, loss_target):
    with _jax.named_scope("forward"):
        args = {**rest, TWIN_DIFF_INPUT: diff, **{k: w.astype(_WEIGHT_DTYPES[k]) for k, w in weights.items()}}
        y = _forward(args)
    with _jax.named_scope("loss_head"):
        err = _jnp.square(y.astype(_jnp.float32) - loss_target)
        return 0.5 * _jnp.sum(_jnp.mean(err, axis=-1)) if err.ndim else 0.5 * err


def _adamw(w, g, m, v):
    m = ADAM_B1 * m + (1.0 - ADAM_B1) * g
    v = ADAM_B2 * v + (1.0 - ADAM_B2) * _jnp.square(g)
    m_hat = m / (1.0 - ADAM_B1 ** ADAM_STEP)
    v_hat = v / (1.0 - ADAM_B2 ** ADAM_STEP)
    delta = -ADAM_LR * (m_hat / (_jnp.sqrt(v_hat) + ADAM_EPS) + ADAM_WD * w)
    return delta, m, v


def reference(x, c, ctx, c_ctx, w_ada, b_ada, g_mix, w_in, conv_w, conv_b, dt_bias, a_log, d_skip, ssd_norm_w, w_ssd_out, pool_w, pool_scale, w_pool_out, w_out, g_ffn, w_gate_up, w_down, g_final, loss_target, m_c_ctx, m_w_ada, m_b_ada, m_g_mix, m_w_in, m_conv_w, m_conv_b, m_dt_bias, m_a_log, m_d_skip, m_ssd_norm_w, m_w_ssd_out, m_pool_w, m_pool_scale, m_w_pool_out, m_w_out, m_g_ffn, m_w_gate_up, m_w_down, m_g_final, v_c_ctx, v_w_ada, v_b_ada, v_g_mix, v_w_in, v_conv_w, v_conv_b, v_dt_bias, v_a_log, v_d_skip, v_ssd_norm_w, v_w_ssd_out, v_pool_w, v_pool_scale, v_w_pool_out, v_w_out, v_g_ffn, v_w_gate_up, v_w_down, v_g_final):
    given = dict(x=x, c=c, ctx=ctx, c_ctx=c_ctx, w_ada=w_ada, b_ada=b_ada, g_mix=g_mix, w_in=w_in, conv_w=conv_w, conv_b=conv_b, dt_bias=dt_bias, a_log=a_log, d_skip=d_skip, ssd_norm_w=ssd_norm_w, w_ssd_out=w_ssd_out, pool_w=pool_w, pool_scale=pool_scale, w_pool_out=w_pool_out, w_out=w_out, g_ffn=g_ffn, w_gate_up=w_gate_up, w_down=w_down, g_final=g_final, loss_target=loss_target, m_c_ctx=m_c_ctx, m_w_ada=m_w_ada, m_b_ada=m_b_ada, m_g_mix=m_g_mix, m_w_in=m_w_in, m_conv_w=m_conv_w, m_conv_b=m_conv_b, m_dt_bias=m_dt_bias, m_a_log=m_a_log, m_d_skip=m_d_skip, m_ssd_norm_w=m_ssd_norm_w, m_w_ssd_out=m_w_ssd_out, m_pool_w=m_pool_w, m_pool_scale=m_pool_scale, m_w_pool_out=m_w_pool_out, m_w_out=m_w_out, m_g_ffn=m_g_ffn, m_w_gate_up=m_w_gate_up, m_w_down=m_w_down, m_g_final=m_g_final, v_c_ctx=v_c_ctx, v_w_ada=v_w_ada, v_b_ada=v_b_ada, v_g_mix=v_g_mix, v_w_in=v_w_in, v_conv_w=v_conv_w, v_conv_b=v_conv_b, v_dt_bias=v_dt_bias, v_a_log=v_a_log, v_d_skip=v_d_skip, v_ssd_norm_w=v_ssd_norm_w, v_w_ssd_out=v_w_ssd_out, v_pool_w=v_pool_w, v_pool_scale=v_pool_scale, v_w_pool_out=v_w_pool_out, v_w_out=v_w_out, v_g_ffn=v_g_ffn, v_w_gate_up=v_w_gate_up, v_w_down=v_w_down, v_g_final=v_g_final)
    weights = {n: given[n] for n in TWIN_WEIGHTS}
    shared = {n: given[n] for n in SHARED_INPUTS}
    per_example = {n: given[n] for n in ['x', 'c', 'ctx']}
    grad_fn = _jax.value_and_grad(_loss, argnums=(0, 1))

    def one_microbatch(ex, loss_target):
        ex = dict(ex)
        diff = ex.pop(TWIN_DIFF_INPUT)
        return grad_fn(weights, diff, {**shared, **ex}, loss_target)

    if N_MICROBATCH == 1:
        loss, (grad_w, grad_x) = one_microbatch(per_example, given["loss_target"])
    else:
        def body(carry, xs):
            loss_sum, grad_sum = carry
            l_k, (gw_k, gx_k) = one_microbatch(xs[0], xs[1])
            with _jax.named_scope("update"):
                return (loss_sum + l_k, _jax.tree.map(_jnp.add, grad_sum, gw_k)), gx_k

        init = (_jnp.zeros((), _jnp.float32), _jax.tree.map(_jnp.zeros_like, weights))
        (loss, grad_w), grad_x = _jax.lax.scan(body, init, (per_example, given["loss_target"]))
    with _jax.named_scope("update"):
        delta_w, new_m, new_v = {}, {}, {}
        for n in TWIN_WEIGHTS:
            delta_w[n], new_m[n], new_v[n] = _adamw(weights[n], grad_w[n], given["m_" + n], given["v_" + n])
    return (loss, grad_x, *[grad_w[n] for n in TWIN_WEIGHTS], *[delta_w[n] for n in TWIN_WEIGHTS],
            *[new_m[n] for n in TWIN_WEIGHTS], *[new_v[n] for n in TWIN_WEIGHTS])
```

```python
import functools

import jax
import jax.numpy as jnp
from jax import lax
from jax.experimental import pallas as pl
from jax.experimental.pallas import tpu as pltpu

F32 = jnp.float32
BF16 = jnp.bfloat16
N_DEV = 8
EPS = 1e-6
GRID_W = 64
POOL_WINDOWS = (2, 4, 8, 16)
HEADDIM = 64
STATE = 128
CHUNK = 128
HPG = 4
CONV_K = 5
ADAM_LR, ADAM_B1, ADAM_B2, ADAM_EPS, ADAM_WD, ADAM_STEP = 0.001, 0.9, 0.999, 1e-08, 0.01, 10
NEG_BIG = -1e30
MESH_AXES = ("x", "y", "c")
ANY = pl.BlockSpec(memory_space=pl.ANY)


def _tile(n, cands):
    for t in cands:
        if n % t == 0:
            return t
    return n


def _round_up(n, m):
    return -(-n // m) * m


def _silu(x):
    return x * jax.nn.sigmoid(x)


def _dsilu(x):
    s = jax.nn.sigmoid(x)
    return s * (1.0 + x * (1.0 - s))


def _cparams(sem):
    return pltpu.CompilerParams(dimension_semantics=sem, vmem_limit_bytes=56 * 1024 * 1024)


def _mm(a, b, mode, out_dtype, name):
    if mode == "tn":
        K, M = a.shape
        N = b.shape[1]
        tm = _tile(M, (512, 256, 128))
        tn = _tile(N, (1024, 512, 256, 128))
        tk = _tile(K, (1088, 544, 512, 256, 128))
        a_spec = pl.BlockSpec((tk, tm), lambda i, j, k: (k, i))
        b_spec = pl.BlockSpec((tk, tn), lambda i, j, k: (k, j))
        dims = (((0,), (0,)), ((), ()))
    else:
        M, K = a.shape
        N = b.shape[0] if mode == "nt" else b.shape[1]
        tm = _tile(M, (1088, 544, 512, 256, 128))
        tn = _tile(N, (512, 256, 128))
        tk = K if K <= 2048 else _tile(K, (1024, 512, 256, 128))
        a_spec = pl.BlockSpec((tm, tk), lambda i, j, k: (i, k))
        if mode == "nt":
            b_spec = pl.BlockSpec((tn, tk), lambda i, j, k: (j, k))
            dims = (((1,), (1,)), ((), ()))
        else:
            b_spec = pl.BlockSpec((tk, tn), lambda i, j, k: (k, j))
            dims = (((1,), (0,)), ((), ()))
    nk = K // tk

    def body(a_ref, b_ref, o_ref, acc):
        k = pl.program_id(2)

        @pl.when(k == 0)
        def _():
            acc[...] = jnp.zeros_like(acc)

        acc[...] += lax.dot_general(a_ref[...].astype(BF16), b_ref[...].astype(BF16), dims,
                                    preferred_element_type=F32)

        @pl.when(k == nk - 1)
        def _():
            o_ref[...] = acc[...].astype(o_ref.dtype)

    return pl.pallas_call(
        body, name=name, grid=(M // tm, N // tn, nk),
        in_specs=[a_spec, b_spec], out_specs=pl.BlockSpec((tm, tn), lambda i, j, k: (i, j)),
        out_shape=jax.ShapeDtypeStruct((M, N), out_dtype),
        scratch_shapes=[pltpu.VMEM((tm, tn), F32)],
        compiler_params=_cparams(("parallel", "parallel", "arbitrary")),
    )(a, b)


def _ada_fwd(cc8, w_adaT, b_ada, name):
    D = cc8.shape[1]
    N = w_adaT.shape[0]
    tn = _tile(N, (512, 256, 128))

    def body(c_ref, w_ref, b_ref, o_ref):
        a = _silu(c_ref[...]).astype(BF16)
        o_ref[...] = lax.dot_general(a, w_ref[...], (((1,), (1,)), ((), ())),
                                     preferred_element_type=F32) + b_ref[...]

    return pl.pallas_call(
        body, name=name, grid=(N // tn,),
        in_specs=[pl.BlockSpec((8, D), lambda j: (0, 0)), pl.BlockSpec((tn, D), lambda j: (j, 0)),
                  pl.BlockSpec((1, tn), lambda j: (0, j))],
        out_specs=pl.BlockSpec((8, tn), lambda j: (0, j)),
        out_shape=jax.ShapeDtypeStruct((8, N), F32),
        compiler_params=_cparams(("parallel",)),
    )(cc8, w_adaT, b_ada)


def _ada_bwd_small(cc8, dsil, dm6, name):
    D = cc8.shape[1]
    N = dm6.shape[1]

    def body(c_ref, ds_ref, dm_ref, sil_ref, dc_ref, db_ref):
        c = c_ref[...]
        sil_ref[...] = _silu(c).astype(BF16)
        dc_ref[...] = ds_ref[...] * _dsilu(c)
        dm = dm_ref[...]
        row = lax.broadcasted_iota(jnp.int32, dm.shape, 0)
        db_ref[...] = jnp.where(row == 0, jnp.sum(dm, axis=0, keepdims=True), 0.0)

    return pl.pallas_call(
        body, name=name, grid=(1,),
        in_specs=[pl.BlockSpec((8, D), lambda i: (0, 0)), pl.BlockSpec((8, D), lambda i: (0, 0)),
                  pl.BlockSpec((8, N), lambda i: (0, 0))],
        out_specs=[pl.BlockSpec((8, D), lambda i: (0, 0)), pl.BlockSpec((8, D), lambda i: (0, 0)),
                   pl.BlockSpec((8, N), lambda i: (0, 0))],
        out_shape=[jax.ShapeDtypeStruct((8, D), BF16), jax.ShapeDtypeStruct((8, D), F32),
                   jax.ShapeDtypeStruct((8, N), F32)],
        compiler_params=_cparams(("arbitrary",)),
    )(cc8, dsil, dm6)


def _seg_pick(m_ref, is_ctx):
    return jnp.where(is_ctx, m_ref[1:2, :], m_ref[0:1, :])


def _norm_mod(x, g, m6, sh_idx, sc_idx, n_ctx_tiles, tr, name, resid=None):
    T, D = x.shape
    row = pl.BlockSpec((tr, D), lambda i: (i, 0))
    vec = pl.BlockSpec((1, D), lambda i: (0, 0))

    def mcol(idx):
        return pl.BlockSpec((8, D), lambda i: (0, idx))

    def body(*refs):
        if resid is None:
            x_ref, g_ref, sh_ref, sc_ref, h_ref = refs
            xv = x_ref[...]
        else:
            x_ref, f_ref, ga_ref, g_ref, sh_ref, sc_ref, xo_ref, h_ref = refs
        is_ctx = pl.program_id(0) < n_ctx_tiles
        if resid is not None:
            xv = x_ref[...] + _seg_pick(ga_ref, is_ctx) * f_ref[...]
            xo_ref[...] = xv
        rstd = lax.rsqrt(jnp.mean(xv * xv, axis=-1, keepdims=True) + EPS)
        hn = xv * rstd * g_ref[...]
        h_ref[...] = (hn * (1.0 + _seg_pick(sc_ref, is_ctx)) + _seg_pick(sh_ref, is_ctx)).astype(BF16)

    if resid is None:
        ins, in_specs = [x, g, m6, m6], [row, vec, mcol(sh_idx), mcol(sc_idx)]
        out_specs, out_shape = row, jax.ShapeDtypeStruct((T, D), BF16)
    else:
        f, ga_idx = resid
        ins = [x, f, m6, g, m6, m6]
        in_specs = [row, row, mcol(ga_idx), vec, mcol(sh_idx), mcol(sc_idx)]
        out_specs = [row, row]
        out_shape = [jax.ShapeDtypeStruct((T, D), F32), jax.ShapeDtypeStruct((T, D), BF16)]
    return pl.pallas_call(body, name=name, grid=(T // tr,), in_specs=in_specs, out_specs=out_specs,
                          out_shape=out_shape, compiler_params=_cparams(("parallel",)))(*ins)


def _resid(x, f, m6, ga_idx, n_ctx_tiles, tr, name):
    T, D = x.shape
    row = pl.BlockSpec((tr, D), lambda i: (i, 0))

    def body(x_ref, f_ref, ga_ref, o_ref):
        is_ctx = pl.program_id(0) < n_ctx_tiles
        o_ref[...] = x_ref[...] + _seg_pick(ga_ref, is_ctx) * f_ref[...]

    return pl.pallas_call(body, name=name, grid=(T // tr,),
                          in_specs=[row, row, pl.BlockSpec((8, D), lambda i: (0, ga_idx))], out_specs=row,
                          out_shape=jax.ShapeDtypeStruct((T, D), F32),
                          compiler_params=_cparams(("parallel",)))(x, f, m6)


def _resid_bwd(dx, f, m6, ga_idx, n_ctx_tiles, tr, name):
    T, D = dx.shape
    row = pl.BlockSpec((tr, D), lambda i: (i, 0))
    acc = pl.BlockSpec((8, D), lambda i: (0, 0))

    def body(dx_ref, f_ref, ga_ref, df_ref, dga_ref):
        i = pl.program_id(0)
        is_ctx = i < n_ctx_tiles

        @pl.when(i == 0)
        def _():
            dga_ref[...] = jnp.zeros_like(dga_ref)

        dxv = dx_ref[...]
        df_ref[...] = (_seg_pick(ga_ref, is_ctx) * dxv).astype(BF16)
        s = jnp.sum(dxv * f_ref[...], axis=0, keepdims=True)
        r = lax.broadcasted_iota(jnp.int32, (8, D), 0)
        dga_ref[...] += jnp.where(r == jnp.where(is_ctx, 1, 0), s, 0.0)

    return pl.pallas_call(body, name=name, grid=(T // tr,),
                          in_specs=[row, row, pl.BlockSpec((8, D), lambda i: (0, ga_idx))],
                          out_specs=[row, acc],
                          out_shape=[jax.ShapeDtypeStruct((T, D), BF16), jax.ShapeDtypeStruct((8, D), F32)],
                          compiler_params=_cparams(("arbitrary",)))(dx, f, m6)


def _norm_mod_bwd(x, dh, dxres, g, m6, sc_idx, n_ctx_tiles, tr, name):
    T, D = x.shape
    row = pl.BlockSpec((tr, D), lambda i: (i, 0))
    acc = pl.BlockSpec((8, D), lambda i: (0, 0))

    def body(x_ref, dh_ref, dr_ref, g_ref, sc_ref, dx_ref, st_ref):
        i = pl.program_id(0)
        is_ctx = i < n_ctx_tiles

        @pl.when(i == 0)
        def _():
            st_ref[...] = jnp.zeros_like(st_ref)

        xv, dh_v, gv = x_ref[...], dh_ref[...], g_ref[...]
        sc1 = 1.0 + _seg_pick(sc_ref, is_ctx)
        rstd = lax.rsqrt(jnp.mean(xv * xv, axis=-1, keepdims=True) + EPS)
        xhat = xv * rstd
        dxhat = dh_v * sc1 * gv
        dx_ref[...] = dr_ref[...] + rstd * (dxhat - xhat * jnp.mean(dxhat * xhat, axis=-1, keepdims=True))
        dsh = jnp.sum(dh_v, axis=0, keepdims=True)
        dsc = jnp.sum(dh_v * xhat * gv, axis=0, keepdims=True)
        dg = jnp.sum(dh_v * sc1 * xhat, axis=0, keepdims=True)
        r = lax.broadcasted_iota(jnp.int32, (8, D), 0)
        seg = jnp.where(is_ctx, 1, 0)
        st_ref[...] += (jnp.where(r == seg, dsh, 0.0) + jnp.where(r == 2 + seg, dsc, 0.0)
                        + jnp.where(r == 4, dg, 0.0))

    return pl.pallas_call(body, name=name, grid=(T // tr,),
                          in_specs=[row, row, row, pl.BlockSpec((1, D), lambda i: (0, 0)),
                                    pl.BlockSpec((8, D), lambda i: (0, sc_idx))],
                          out_specs=[row, acc],
                          out_shape=[jax.ShapeDtypeStruct((T, D), F32), jax.ShapeDtypeStruct((8, D), F32)],
                          compiler_params=_cparams(("arbitrary",)))(x, dh, dxres, g, m6)


def _loss_head(x, tgt, g, n_ctx_tiles, tr, name):
    T, D = x.shape
    row = pl.BlockSpec((tr, D), lambda i: (i, 0))

    def body(x_ref, t_ref, g_ref, l_ref, dx_ref, dg_ref):
        i = pl.program_id(0)

        @pl.when(i == 0)
        def _():
            l_ref[...] = jnp.zeros_like(l_ref)
            dg_ref[...] = jnp.zeros_like(dg_ref)

        @pl.when(i < n_ctx_tiles)
        def _():
            dx_ref[...] = jnp.zeros_like(dx_ref)

        @pl.when(i >= n_ctx_tiles)
        def _():
            xv, gv = x_ref[...], g_ref[...]
            rstd = lax.rsqrt(jnp.mean(xv * xv, axis=-1, keepdims=True) + EPS)
            xhat = xv * rstd
            e = xhat * gv - t_ref[...]
            l_ref[...] += 0.5 * jnp.sum(jnp.mean(e * e, axis=-1, keepdims=True), axis=0, keepdims=True)
            dy = e * (1.0 / D)
            dxhat = dy * gv
            dx_ref[...] = rstd * (dxhat - xhat * jnp.mean(dxhat * xhat, axis=-1, keepdims=True))
            r = lax.broadcasted_iota(jnp.int32, (8, D), 0)
            dg_ref[...] += jnp.where(r == 0, jnp.sum(dy * xhat, axis=0, keepdims=True), 0.0)

    return pl.pallas_call(
        body, name=name, grid=(T // tr,),
        in_specs=[row, pl.BlockSpec((tr, D), lambda i: (jnp.maximum(i - n_ctx_tiles, 0), 0)),
                  pl.BlockSpec((1, D), lambda i: (0, 0))],
        out_specs=[pl.BlockSpec((8, 128), lambda i: (0, 0)), row, pl.BlockSpec((8, D), lambda i: (0, 0))],
        out_shape=[jax.ShapeDtypeStruct((8, 128), F32), jax.ShapeDtypeStruct((T, D), F32),
                   jax.ShapeDtypeStruct((8, D), F32)],
        compiler_params=_cparams(("arbitrary",)))(x, tgt, g)


def _seq_masks(T, n_ctx, width):
    row = lax.broadcasted_iota(jnp.int32, (T, width), 0)
    in_ctx = row < n_ctx
    return jnp.where(in_ctx, row, row - n_ctx), jnp.where(in_ctx, n_ctx, T - n_ctx)


def _shift_rows(u, off, t_loc, seg_len):
    T = u.shape[0]
    if off == 0:
        return u
    v = pltpu.roll(u, (-off) % T, 0)
    ok = (t_loc + off >= 0) & (t_loc + off < seg_len)
    return jnp.where(ok, v, 0.0)


def _conv_fwd(proj, col0_blk, ncol, conv_w8, conv_b, n_ctx, tc, name):
    T = proj.shape[0]

    def body(u_ref, w_ref, b_ref, o_ref):
        u = u_ref[...]
        t_loc, seg_len = _seq_masks(T, n_ctx, tc)
        acc = jnp.broadcast_to(b_ref[...], u.shape)
        for i in range(CONV_K):
            acc = acc + w_ref[i:i + 1, :] * _shift_rows(u, i - CONV_K // 2, t_loc, seg_len)
        o_ref[...] = _silu(acc)

    return pl.pallas_call(
        body, name=name, grid=(ncol // tc,),
        in_specs=[pl.BlockSpec((T, tc), lambda j: (0, col0_blk + j)), pl.BlockSpec((8, tc), lambda j: (0, j)),
                  pl.BlockSpec((1, tc), lambda j: (0, j))],
        out_specs=pl.BlockSpec((T, tc), lambda j: (0, j)),
        out_shape=jax.ShapeDtypeStruct((T, ncol), F32),
        compiler_params=_cparams(("parallel",)))(proj, conv_w8, conv_b)


def _conv_bwd(proj, col0_blk, dact, conv_w8, conv_b, n_ctx, tc, name):
    T, ncol = dact.shape

    def body(u_ref, d_ref, w_ref, b_ref, du_ref, dw_ref):
        u = u_ref[...]
        t_loc, seg_len = _seq_masks(T, n_ctx, tc)
        pre = jnp.broadcast_to(b_ref[...], u.shape)
        shifted = []
        for i in range(CONV_K):
            s = _shift_rows(u, i - CONV_K // 2, t_loc, seg_len)
            shifted.append(s)
            pre = pre + w_ref[i:i + 1, :] * s
        dpre = d_ref[...] * _dsilu(pre)
        du = jnp.zeros_like(u)
        r = lax.broadcasted_iota(jnp.int32, (8, tc), 0)
        dw = jnp.where(r == CONV_K, jnp.sum(dpre, axis=0, keepdims=True), 0.0)
        for i in range(CONV_K):
            du = du + w_ref[i:i + 1, :] * _shift_rows(dpre, -(i - CONV_K // 2), t_loc, seg_len)
            dw = dw + jnp.where(r == i, jnp.sum(dpre * shifted[i], axis=0, keepdims=True), 0.0)
        du_ref[...] = du.astype(BF16)
        dw_ref[...] = dw

    return pl.pallas_call(
        body, name=name, grid=(ncol // tc,),
        in_specs=[pl.BlockSpec((T, tc), lambda j: (0, col0_blk + j)), pl.BlockSpec((T, tc), lambda j: (0, j)),
                  pl.BlockSpec((8, tc), lambda j: (0, j)), pl.BlockSpec((1, tc), lambda j: (0, j))],
        out_specs=[pl.BlockSpec((T, tc), lambda j: (0, j)), pl.BlockSpec((8, tc), lambda j: (0, j))],
        out_shape=[jax.ShapeDtypeStruct((T, ncol), BF16), jax.ShapeDtypeStruct((8, ncol), F32)],
        compiler_params=_cparams(("parallel",)))(proj, dact, conv_w8, conv_b)


def _pool_core(u, half, t_loc, seg_len, transpose):
    tr = u.shape[0]

    def shift(v, s):
        w = pltpu.roll(v, s % tr, 0)
        ok = (t_loc - s >= 0) & (t_loc - s < seg_len)
        return jnp.where(ok, w, 0.0)

    cnt = (jnp.minimum(t_loc, half) + jnp.minimum(seg_len - t_loc, half)).astype(F32)
    q = u / cnt if transpose else u
    back, ahead, h = q, q, 1
    while h < half:
        back = back + shift(back, h)
        ahead = ahead + shift(ahead, -h)
        h *= 2
    if transpose:
        tot = back + shift(ahead, -1)
        return tot - u
    tot = shift(back, 1) + ahead
    return tot / cnt - u


def _pool_apply(src, col0_blk, out_dtype, n_ctx, tr, pg, transpose, name):
    T = src.shape[0]
    n_ctx_tiles = n_ctx // tr

    def body(u_ref, o_ref):
        i, gi = pl.program_id(0), pl.program_id(1)
        row = lax.broadcasted_iota(jnp.int32, (tr, pg), 0)
        seg_len = jnp.where(i < n_ctx_tiles, tr, GRID_W)
        t_loc = row & (seg_len - 1)
        u = u_ref[...].astype(F32)
        for k_idx, k in enumerate(POOL_WINDOWS):
            @pl.when(gi == k_idx)
            def _(k=k):
                o_ref[...] = _pool_core(u, k // 2, t_loc, seg_len, transpose).astype(o_ref.dtype)

    return pl.pallas_call(
        body, name=name, grid=(T // tr, len(POOL_WINDOWS)),
        in_specs=[pl.BlockSpec((tr, pg), lambda i, gi: (i, col0_blk + gi))],
        out_specs=pl.BlockSpec((tr, pg), lambda i, gi: (i, gi)),
        out_shape=jax.ShapeDtypeStruct((T, pg * len(POOL_WINDOWS)), out_dtype),
        compiler_params=_cparams(("parallel", "parallel")))(src)


def _pool_mix_fwd(pm, pool_w, pool_scale, tr, name):
    T, W = pm.shape
    ng, pg = pool_w.shape[0], pool_w.shape[1]

    def body(p_ref, w_ref, s_ref, o_ref):
        o_ref[...] = (jnp.dot(p_ref[...], w_ref[...], preferred_element_type=F32) * s_ref[...]).astype(BF16)

    return pl.pallas_call(
        body, name=name, grid=(T // tr, ng),
        in_specs=[pl.BlockSpec((tr, pg), lambda i, g: (i, g)), pl.BlockSpec((None, pg, pg), lambda i, g: (g, 0, 0)),
                  pl.BlockSpec((1, pg), lambda i, g: (0, g))],
        out_specs=pl.BlockSpec((tr, pg), lambda i, g: (i, g)),
        out_shape=jax.ShapeDtypeStruct((T, W), BF16),
        compiler_params=_cparams(("parallel", "parallel")))(pm, pool_w, pool_scale)


def _pool_mix_bwd(pm, dpms, pool_w, pool_scale, tr, name):
    T, W = pm.shape
    ng, pg = pool_w.shape[0], pool_w.shape[1]

    def body(p_ref, d_ref, w_ref, s_ref, dp_ref, dw_ref, ds_ref):
        i = pl.program_id(1)

        @pl.when(i == 0)
        def _():
            dw_ref[...] = jnp.zeros_like(dw_ref)
            ds_ref[...] = jnp.zeros_like(ds_ref)

        p, w = p_ref[...], w_ref[...]
        d = d_ref[...].astype(F32)
        pmix = jnp.dot(p, w, preferred_element_type=F32)
        r = lax.broadcasted_iota(jnp.int32, (8, pg), 0)
        ds_ref[...] += jnp.where(r == 0, jnp.sum(d * pmix, axis=0, keepdims=True), 0.0)
        dmix = (d * s_ref[...]).astype(BF16)
        dp_ref[...] = lax.dot_general(dmix, w, (((1,), (1,)), ((), ())), preferred_element_type=F32)
        dw_ref[...] += lax.dot_general(p, dmix, (((0,), (0,)), ((), ())), preferred_element_type=F32)

    return pl.pallas_call(
        body, name=name, grid=(ng, T // tr),
        in_specs=[pl.BlockSpec((tr, pg), lambda g, i: (i, g)), pl.BlockSpec((tr, pg), lambda g, i: (i, g)),
                  pl.BlockSpec((None, pg, pg), lambda g, i: (g, 0, 0)), pl.BlockSpec((1, pg), lambda g, i: (0, g))],
        out_specs=[pl.BlockSpec((tr, pg), lambda g, i: (i, g)), pl.BlockSpec((None, pg, pg), lambda g, i: (g, 0, 0)),
                   pl.BlockSpec((8, pg), lambda g, i: (0, g))],
        out_shape=[jax.ShapeDtypeStruct((T, W), F32), jax.ShapeDtypeStruct((ng, pg, pg), F32),
                   jax.ShapeDtypeStruct((8, W), F32)],
        compiler_params=_cparams(("parallel", "arbitrary")))(pm, dpms, pool_w, pool_scale)


def _dt_prep(proj, dt_blk, bias, a_log, tr, name):
    T = proj.shape[0]
    row = pl.BlockSpec((tr, 128), lambda i: (i, 0))

    def body(r_ref, b_ref, al_ref, dt_ref, a_ref):
        xv = r_ref[...] + b_ref[...]
        dt = jnp.maximum(xv, 0.0) + jnp.log(1.0 + jnp.exp(-jnp.abs(xv)))
        dt_ref[...] = dt
        a_ref[...] = -jnp.exp(al_ref[...]) * dt

    return pl.pallas_call(
        body, name=name, grid=(T // tr,),
        in_specs=[pl.BlockSpec((tr, 128), lambda i: (i, dt_blk)), pl.BlockSpec((1, 128), lambda i: (0, 0)),
                  pl.BlockSpec((1, 128), lambda i: (0, 0))],
        out_specs=[row, row],
        out_shape=[jax.ShapeDtypeStruct((T, 128), F32), jax.ShapeDtypeStruct((T, 128), F32)],
        compiler_params=_cparams(("parallel",)))(proj, bias, a_log)


def _dt_bwd(proj, dt_blk, bias, a_log, ddt, da, tr, name):
    T = proj.shape[0]
    row = pl.BlockSpec((tr, 128), lambda i: (i, 0))
    vec = pl.BlockSpec((1, 128), lambda i: (0, 0))

    def body(r_ref, b_ref, al_ref, ddt_ref, da_ref, o_ref, st_ref):
        @pl.when(pl.program_id(0) == 0)
        def _():
            st_ref[...] = jnp.zeros_like(st_ref)

        xv = r_ref[...] + b_ref[...]
        dt = jnp.maximum(xv, 0.0) + jnp.log(1.0 + jnp.exp(-jnp.abs(xv)))
        a_neg = -jnp.exp(al_ref[...])
        dav = da_ref[...]
        draw = (ddt_ref[...] + dav * a_neg) * jax.nn.sigmoid(xv)
        o_ref[...] = draw.astype(BF16)
        r = lax.broadcasted_iota(jnp.int32, (8, 128), 0)
        st_ref[...] += (jnp.where(r == 0, jnp.sum(draw, axis=0, keepdims=True), 0.0)
                        + jnp.where(r == 1, jnp.sum(dav * dt, axis=0, keepdims=True) * a_neg, 0.0))

    return pl.pallas_call(
        body, name=name, grid=(T // tr,),
        in_specs=[pl.BlockSpec((tr, 128), lambda i: (i, dt_blk)), vec, vec, row, row],
        out_specs=[row, pl.BlockSpec((8, 128), lambda i: (0, 0))],
        out_shape=[jax.ShapeDtypeStruct((T, 128), BF16), jax.ShapeDtypeStruct((8, 128), F32)],
        compiler_params=_cparams(("arbitrary",)))(proj, bias, a_log, ddt, da)


def _scan_chunk(d, pos, nc_ctx, nc):
    rev = jnp.where(pos < nc_ctx, nc_ctx - 1 - pos, nc - 1 - (pos - nc_ctx))
    return jnp.where(d == 0, pos, rev)


def _chunk_decay(a_ref, aT_ref, d):
    Q = CHUNK
    ii = lax.broadcasted_iota(jnp.int32, (Q, Q), 0)
    jj = lax.broadcasted_iota(jnp.int32, (Q, Q), 1)
    sgn = jnp.where(d == 0, 1, -1)
    mask = (ii - jj) * sgn >= 0
    maskT = (jj - ii) * sgn >= 0
    return mask, maskT, mask.astype(F32), maskT.astype(F32)


def _ssd_fwd(xbc, dt4, a4, aT4, dsk, n_ctx, name):
    T = xbc.shape[0]
    G = dt4.shape[1]
    R, P, N, Q = HPG, HEADDIM, STATE, CHUNK
    H = G * R
    nc, nc_ctx = T // Q, n_ctx // Q
    xw = R * P
    b_blk0 = (H * P) // N
    c_blk0 = b_blk0 + G

    def ch(d, g, s):
        return _scan_chunk(d, s, nc_ctx, nc)

    def body(x_ref, b_ref, c_ref, dt_ref, a_ref, aT_ref, dsk_ref, y_ref, st_ref, S):
        d, s = pl.program_id(0), pl.program_id(2)

        @pl.when(s == 0)
        def _():
            S[...] = jnp.zeros_like(S)

        mask, maskT, maskf, maskTf = _chunk_decay(a_ref, aT_ref, d)
        a, aT = a_ref[...], aT_ref[...]
        Bm, Cm = b_ref[...].astype(BF16), c_ref[...].astype(BF16)
        Gm = lax.dot_general(Cm, Bm, (((1,), (1,)), ((), ())), preferred_element_type=F32)
        tot = jnp.sum(a, axis=0, keepdims=True)
        for r in range(R):
            lc = jnp.sum(maskf * aT[r:r + 1, :], axis=1, keepdims=True)
            lr = jnp.sum(maskTf * a[:, r:r + 1], axis=0, keepdims=True)
            ldec = jnp.exp(jnp.where(mask, lc - lr, NEG_BIG))
            xr = x_ref[:, r * P:(r + 1) * P]
            xc = xr * dt_ref[:, r:r + 1]
            s_in = S[r]
            y = jnp.dot((Gm * ldec).astype(BF16), xc.astype(BF16), preferred_element_type=F32)
            y = y + lax.dot_general(Cm, s_in.astype(BF16), (((1,), (1,)), ((), ())),
                                    preferred_element_type=F32) * jnp.exp(lc)
            y_ref[:, r * P:(r + 1) * P] = y + dsk_ref[0:1, r:r + 1] * xr
            st_ref[r] = s_in
            tot_r = tot[:, r:r + 1]
            xd = (xc * jnp.exp(tot_r - lc)).astype(BF16)
            S[r] = jnp.exp(tot_r) * s_in + lax.dot_general(xd, Bm, (((0,), (0,)), ((), ())),
                                                           preferred_element_type=F32)

    return pl.pallas_call(
        body, name=name, grid=(2, G, nc),
        in_specs=[pl.BlockSpec((Q, xw), lambda d, g, s: (ch(d, g, s), g)),
                  pl.BlockSpec((Q, N), lambda d, g, s: (ch(d, g, s), b_blk0 + g)),
                  pl.BlockSpec((Q, N), lambda d, g, s: (ch(d, g, s), c_blk0 + g)),
                  pl.BlockSpec((None, None, Q, R), lambda d, g, s: (d, g, ch(d, g, s), 0)),
                  pl.BlockSpec((None, None, Q, R), lambda d, g, s: (d, g, ch(d, g, s), 0)),
                  pl.BlockSpec((None, None, R, Q), lambda d, g, s: (d, g, 0, ch(d, g, s))),
                  pl.BlockSpec((None, None, 8, 128), lambda d, g, s: (d, g, 0, 0))],
        out_specs=[pl.BlockSpec((None, Q, xw), lambda d, g, s: (d, ch(d, g, s), g)),
                   pl.BlockSpec((None, None, R, P, N), lambda d, g, s: (d, ch(d, g, s), g, 0, 0))],
        out_shape=[jax.ShapeDtypeStruct((2, T, H * P), F32), jax.ShapeDtypeStruct((2, nc, H, P, N), F32)],
        scratch_shapes=[pltpu.VMEM((R, P, N), F32)],
        compiler_params=_cparams(("parallel", "parallel", "arbitrary")))(xbc, xbc, xbc, dt4, a4, aT4, dsk)


def _ssd_bwd(xbc, dy, states, dt4, a4, aT4, n_ctx, name):
    T = xbc.shape[0]
    G = dt4.shape[1]
    R, P, N, Q = HPG, HEADDIM, STATE, CHUNK
    H = G * R
    nc, nc_ctx = T // Q, n_ctx // Q
    xw = R * P
    b_blk0 = (H * P) // N
    c_blk0 = b_blk0 + G

    def ch(d, g, s):
        return _scan_chunk(d, nc - 1 - s, nc_ctx, nc)

    def body(x_ref, b_ref, c_ref, dy_ref, st_ref, dt_ref, a_ref, aT_ref,
             dx_ref, db_ref, dc_ref, ddt_ref, da_ref, daT_ref, dS):
        d, s = pl.program_id(0), pl.program_id(2)

        @pl.when(s == 0)
        def _():
            dS[...] = jnp.zeros_like(dS)

        mask, maskT, maskf, maskTf = _chunk_decay(a_ref, aT_ref, d)
        a, aT = a_ref[...], aT_ref[...]
        Bm, Cm = b_ref[...].astype(BF16), c_ref[...].astype(BF16)
        Gm = lax.dot_general(Cm, Bm, (((1,), (1,)), ((), ())), preferred_element_type=F32)
        tot = jnp.sum(a, axis=0, keepdims=True)
        ri = lax.broadcasted_iota(jnp.int32, (Q, 1), 0)
        is_last = ri == jnp.where(d == 0, Q - 1, 0)
        dG = jnp.zeros((Q, Q), F32)
        dB = jnp.zeros((Q, N), F32)
        dC = jnp.zeros((Q, N), F32)
        for r in range(R):
            lc = jnp.sum(maskf * aT[r:r + 1, :], axis=1, keepdims=True)
            lr = jnp.sum(maskTf * a[:, r:r + 1], axis=0, keepdims=True)
            ldec = jnp.exp(jnp.where(mask, lc - lr, NEG_BIG))
            tot_r = tot[:, r:r + 1]
            e_in = jnp.exp(lc)
            dte = jnp.exp(tot_r - lc)
            e_all = jnp.exp(tot_r)
            xr = x_ref[:, r * P:(r + 1) * P]
            dtr = dt_ref[:, r:r + 1]
            xc = xr * dtr
            xc_b = xc.astype(BF16)
            dyr = dy_ref[:, r * P:(r + 1) * P]
            dy_b = dyr.astype(BF16)
            s_in = st_ref[r]
            s_in_b = s_in.astype(BF16)
            ds_out = dS[r]
            ds_out_b = ds_out.astype(BF16)
            Mm = Gm * ldec
            dM = lax.dot_general(dy_b, xc_b, (((1,), (1,)), ((), ())), preferred_element_type=F32)
            Wm = dM * Mm
            y_int = lax.dot_general(Cm, s_in_b, (((1,), (1,)), ((), ())), preferred_element_type=F32) * e_in
            b_ds = lax.dot_general(Bm, ds_out_b, (((1,), (1,)), ((), ())), preferred_element_type=F32)
            xj = jnp.sum(xc * b_ds, axis=1, keepdims=True)
            dlam_c = (jnp.sum(Wm, axis=1, keepdims=True) + jnp.sum(dyr * y_int, axis=1, keepdims=True)
                      - dte * xj)
            last = jnp.sum(jnp.sum(ds_out * s_in, axis=1, keepdims=True), axis=0, keepdims=True) * e_all \
                + jnp.sum(dte * xj, axis=0, keepdims=True)
            dlam_c = dlam_c + jnp.where(is_last, last, 0.0)
            dlam_r = -jnp.sum(Wm, axis=0, keepdims=True)
            daT_ref[r:r + 1, :] = jnp.sum(maskf * dlam_c, axis=0, keepdims=True)
            da_ref[:, r:r + 1] = jnp.sum(maskTf * dlam_r, axis=1, keepdims=True)
            dxc = lax.dot_general(Mm.astype(BF16), dy_b, (((0,), (0,)), ((), ())),
                                  preferred_element_type=F32) + dte * b_ds
            dx_ref[:, r * P:(r + 1) * P] = dxc * dtr
            ddt_ref[:, r:r + 1] = jnp.sum(dxc * xr, axis=1, keepdims=True)
            dG = dG + dM * ldec
            edy_b = (e_in * dyr).astype(BF16)
            dC = dC + jnp.dot(edy_b, s_in_b, preferred_element_type=F32)
            dB = dB + jnp.dot((dte * xc).astype(BF16), ds_out_b, preferred_element_type=F32)
            dS[r] = e_all * ds_out + lax.dot_general(edy_b, Cm, (((0,), (0,)), ((), ())),
                                                     preferred_element_type=F32)
        dG_b = dG.astype(BF16)
        dc_ref[...] = dC + jnp.dot(dG_b, Bm, preferred_element_type=F32)
        db_ref[...] = dB + lax.dot_general(dG_b, Cm, (((0,), (0,)), ((), ())), preferred_element_type=F32)

    tq = pl.BlockSpec((None, None, Q, R), lambda d, g, s: (d, g, ch(d, g, s), 0))
    return pl.pallas_call(
        body, name=name, grid=(2, G, nc),
        in_specs=[pl.BlockSpec((Q, xw), lambda d, g, s: (ch(d, g, s), g)),
                  pl.BlockSpec((Q, N), lambda d, g, s: (ch(d, g, s), b_blk0 + g)),
                  pl.BlockSpec((Q, N), lambda d, g, s: (ch(d, g, s), c_blk0 + g)),
                  pl.BlockSpec((Q, xw), lambda d, g, s: (ch(d, g, s), g)),
                  pl.BlockSpec((None, None, R, P, N), lambda d, g, s: (d, ch(d, g, s), g, 0, 0)),
                  tq, tq, pl.BlockSpec((None, None, R, Q), lambda d, g, s: (d, g, 0, ch(d, g, s)))],
        out_specs=[pl.BlockSpec((None, Q, xw), lambda d, g, s: (d, ch(d, g, s), g)),
                   pl.BlockSpec((None, Q, N), lambda d, g, s: (d, ch(d, g, s), g)),
                   pl.BlockSpec((None, Q, N), lambda d, g, s: (d, ch(d, g, s), g)),
                   tq, tq, pl.BlockSpec((None, None, R, Q), lambda d, g, s: (d, g, 0, ch(d, g, s)))],
        out_shape=[jax.ShapeDtypeStruct((2, T, H * P), F32), jax.ShapeDtypeStruct((2, T, G * N), F32),
                   jax.ShapeDtypeStruct((2, T, G * N), F32), jax.ShapeDtypeStruct((2, G, T, R), F32),
                   jax.ShapeDtypeStruct((2, G, T, R), F32), jax.ShapeDtypeStruct((2, G, R, T), F32)],
        scratch_shapes=[pltpu.VMEM((R, P, N), F32)],
        compiler_params=_cparams(("parallel", "parallel", "arbitrary")))(
            xbc, xbc, xbc, dy, states, dt4, a4, aT4)


def _ssd_combine(dxs2, db2, dc2, dy, xbc, dskv, tr, name):
    T, HP = dy.shape
    GN = db2.shape[2]
    ncol = HP + 2 * GN
    tc = _tile(GN, (512, 256, 128))
    nx, nb = HP // tc, GN // tc

    def body(dx_ref, db_ref, dc_ref, dy_ref, x_ref, k_ref, o_ref, dk_ref):
        i, j = pl.program_id(1), pl.program_id(0)

        @pl.when((i == 0) & (j < nx))
        def _():
            dk_ref[...] = jnp.zeros_like(dk_ref)

        @pl.when(j < nx)
        def _():
            dyv = dy_ref[...]
            o_ref[...] = dx_ref[0] + dx_ref[1] + (k_ref[0:1, :] + k_ref[1:2, :]) * dyv
            r = lax.broadcasted_iota(jnp.int32, (8, tc), 0)
            dk_ref[...] += jnp.where(r < 2, jnp.sum(dyv * x_ref[...], axis=0, keepdims=True), 0.0)

        @pl.when((j >= nx) & (j < nx + nb))
        def _():
            o_ref[...] = db_ref[0] + db_ref[1]

        @pl.when(j >= nx + nb)
        def _():
            o_ref[...] = dc_ref[0] + dc_ref[1]

    def cl(j, lo, n):
        return jnp.clip(j - lo, 0, n - 1)

    return pl.pallas_call(
        body, name=name, grid=(ncol // tc, T // tr),
        in_specs=[pl.BlockSpec((2, tr, tc), lambda j, i: (0, i, cl(j, 0, nx))),
                  pl.BlockSpec((2, tr, tc), lambda j, i: (0, i, cl(j, nx, nb))),
                  pl.BlockSpec((2, tr, tc), lambda j, i: (0, i, cl(j, nx + nb, nb))),
                  pl.BlockSpec((tr, tc), lambda j, i: (i, cl(j, 0, nx))),
                  pl.BlockSpec((tr, tc), lambda j, i: (i, cl(j, 0, nx))),
                  pl.BlockSpec((8, tc), lambda j, i: (0, cl(j, 0, nx)))],
        out_specs=[pl.BlockSpec((tr, tc), lambda j, i: (i, j)), pl.BlockSpec((8, tc), lambda j, i: (0, cl(j, 0, nx)))],
        out_shape=[jax.ShapeDtypeStruct((T, ncol), F32), jax.ShapeDtypeStruct((8, HP), F32)],
        compiler_params=_cparams(("arbitrary", "arbitrary")))(dxs2, db2, dc2, dy, xbc, dskv)


def _gnorm_fwd(y2, proj, w, gs, tr, name):
    T, HP = y2.shape[1], y2.shape[2]

    def body(y_ref, z_ref, w_ref, o_ref):
        yz = (y_ref[0] + y_ref[1]) * _silu(z_ref[...])
        for g in range(HP // gs):
            v = yz[:, g * gs:(g + 1) * gs]
            rstd = lax.rsqrt(jnp.mean(v * v, axis=-1, keepdims=True) + EPS)
            o_ref[:, g * gs:(g + 1) * gs] = (v * rstd * w_ref[:, g * gs:(g + 1) * gs]).astype(BF16)

    return pl.pallas_call(
        body, name=name, grid=(T // tr,),
        in_specs=[pl.BlockSpec((2, tr, HP), lambda i: (0, i, 0)), pl.BlockSpec((tr, HP), lambda i: (i, 0)),
                  pl.BlockSpec((1, HP), lambda i: (0, 0))],
        out_specs=pl.BlockSpec((tr, HP), lambda i: (i, 0)),
        out_shape=jax.ShapeDtypeStruct((T, HP), BF16),
        compiler_params=_cparams(("parallel",)))(y2, proj, w)


def _gnorm_bwd(y2, proj, w, dyn, gs, tr, name):
    T, HP = y2.shape[1], y2.shape[2]
    row = pl.BlockSpec((tr, HP), lambda i: (i, 0))

    def body(y_ref, z_ref, w_ref, d_ref, dy_ref, dz_ref, dw_ref):
        @pl.when(pl.program_id(0) == 0)
        def _():
            dw_ref[...] = jnp.zeros_like(dw_ref)

        yv = y_ref[0] + y_ref[1]
        zv = z_ref[...]
        sz = _silu(zv)
        yz = yv * sz
        dv = d_ref[...]
        r = lax.broadcasted_iota(jnp.int32, (8, gs), 0)
        for g in range(HP // gs):
            sl = slice(g * gs, (g + 1) * gs)
            v = yz[:, sl]
            rstd = lax.rsqrt(jnp.mean(v * v, axis=-1, keepdims=True) + EPS)
            xhat = v * rstd
            dyn_g = dv[:, sl]
            dhat = dyn_g * w_ref[:, sl]
            dyz = rstd * (dhat - xhat * jnp.mean(dhat * xhat, axis=-1, keepdims=True))
            dy_ref[:, sl] = dyz * sz[:, sl]
            dz_ref[:, sl] = (dyz * yv[:, sl] * _dsilu(zv[:, sl])).astype(BF16)
            dw_ref[:, sl] += jnp.where(r == 0, jnp.sum(dyn_g * xhat, axis=0, keepdims=True), 0.0)

    return pl.pallas_call(
        body, name=name, grid=(T // tr,),
        in_specs=[pl.BlockSpec((2, tr, HP), lambda i: (0, i, 0)), row, pl.BlockSpec((1, HP), lambda i: (0, 0)), row],
        out_specs=[row, row, pl.BlockSpec((8, HP), lambda i: (0, 0))],
        out_shape=[jax.ShapeDtypeStruct((T, HP), F32), jax.ShapeDtypeStruct((T, HP), BF16),
                   jax.ShapeDtypeStruct((8, HP), F32)],
        compiler_params=_cparams(("arbitrary",)))(y2, proj, w, dyn)


def _merge_fwd(proj, g1_blk, o_ssd, o_pool, tr, name):
    T, D = o_ssd.shape
    row = pl.BlockSpec((tr, D), lambda i: (i, 0))

    def body(g1_ref, g2_ref, a_ref, b_ref, o_ref):
        o_ref[...] = (jax.nn.sigmoid(g1_ref[...]) * a_ref[...]
                      + jax.nn.sigmoid(g2_ref[...]) * b_ref[...]).astype(BF16)

    return pl.pallas_call(
        body, name=name, grid=(T // tr,),
        in_specs=[pl.BlockSpec((tr, D), lambda i: (i, g1_blk)), pl.BlockSpec((tr, D), lambda i: (i, g1_blk + 1)),
                  row, row],
        out_specs=row, out_shape=jax.ShapeDtypeStruct((T, D), BF16),
        compiler_params=_cparams(("parallel",)))(proj, proj, o_ssd, o_pool)


def _merge_bwd(proj, g1_blk, o_ssd, o_pool, dmg, tr, name):
    T, D = o_ssd.shape
    row = pl.BlockSpec((tr, D), lambda i: (i, 0))

    def body(g1_ref, g2_ref, a_ref, b_ref, d_ref, da_ref, db_ref, dg_ref):
        s1, s2 = jax.nn.sigmoid(g1_ref[...]), jax.nn.sigmoid(g2_ref[...])
        dv = d_ref[...]
        da_ref[...] = (s1 * dv).astype(BF16)
        db_ref[...] = (s2 * dv).astype(BF16)
        dg_ref[:, :D] = (dv * a_ref[...] * s1 * (1.0 - s1)).astype(BF16)
        dg_ref[:, D:] = (dv * b_ref[...] * s2 * (1.0 - s2)).astype(BF16)

    return pl.pallas_call(
        body, name=name, grid=(T // tr,),
        in_specs=[pl.BlockSpec((tr, D), lambda i: (i, g1_blk)), pl.BlockSpec((tr, D), lambda i: (i, g1_blk + 1)),
                  row, row, row],
        out_specs=[row, row, pl.BlockSpec((tr, 2 * D), lambda i: (i, 0))],
        out_shape=[jax.ShapeDtypeStruct((T, D), BF16), jax.ShapeDtypeStruct((T, D), BF16),
                   jax.ShapeDtypeStruct((T, 2 * D), BF16)],
        compiler_params=_cparams(("parallel",)))(proj, proj, o_ssd, o_pool, dmg)


def _swiglu_fwd(gu, tr, name):
    T, F2 = gu.shape
    F = F2 // 2
    tc = _tile(F, (1408, 768, 512, 256, 128))
    nb = F // tc

    def body(a_ref, b_ref, o_ref):
        o_ref[...] = (_silu(a_ref[...]) * b_ref[...]).astype(BF16)

    return pl.pallas_call(
        body, name=name, grid=(T // tr, nb),
        in_specs=[pl.BlockSpec((tr, tc), lambda i, j: (i, j)), pl.BlockSpec((tr, tc), lambda i, j: (i, nb + j))],
        out_specs=pl.BlockSpec((tr, tc), lambda i, j: (i, j)),
        out_shape=jax.ShapeDtypeStruct((T, F), BF16),
        compiler_params=_cparams(("parallel", "parallel")))(gu, gu)


def _swiglu_bwd(gu, dact, tr, name):
    T, F2 = gu.shape
    F = F2 // 2
    tc = _tile(F, (1408, 768, 512, 256, 128))
    nb = F // tc

    def body(a_ref, b_ref, d_ref, o_ref):
        is_a = pl.program_id(1) < nb
        av, bv, dv = a_ref[...], b_ref[...], d_ref[...]
        o_ref[...] = jnp.where(is_a, dv * bv * _dsilu(av), dv * _silu(av)).astype(BF16)

    return pl.pallas_call(
        body, name=name, grid=(T // tr, 2 * nb),
        in_specs=[pl.BlockSpec((tr, tc), lambda i, j: (i, j % nb)), pl.BlockSpec((tr, tc), lambda i, j: (i, nb + j % nb)),
                  pl.BlockSpec((tr, tc), lambda i, j: (i, j % nb))],
        out_specs=pl.BlockSpec((tr, tc), lambda i, j: (i, j)),
        out_shape=jax.ShapeDtypeStruct((T, F2), BF16),
        compiler_params=_cparams(("parallel", "parallel")))(gu, gu, dact)


def _adamw(w, g, m, v, name):
    Rr, C = w.shape
    tr = _tile(Rr, (256, 128, 64, 32, 16, 8))
    row = pl.BlockSpec((tr, C), lambda i: (i, 0))

    def body(w_ref, g_ref, m_ref, v_ref, d_ref, mo_ref, vo_ref):
        gv = g_ref[...]
        mn = ADAM_B1 * m_ref[...] + (1.0 - ADAM_B1) * gv
        vn = ADAM_B2 * v_ref[...] + (1.0 - ADAM_B2) * (gv * gv)
        m_hat = mn / (1.0 - ADAM_B1 ** ADAM_STEP)
        v_hat = vn / (1.0 - ADAM_B2 ** ADAM_STEP)
        d_ref[...] = -ADAM_LR * (m_hat / (jnp.sqrt(v_hat) + ADAM_EPS) + ADAM_WD * w_ref[...])
        mo_ref[...] = mn
        vo_ref[...] = vn

    sds = jax.ShapeDtypeStruct((Rr, C), F32)
    return pl.pallas_call(body, name=name, grid=(Rr // tr,), in_specs=[row] * 4, out_specs=[row] * 3,
                          out_shape=[sds] * 3, compiler_params=_cparams(("parallel",)))(w, g, m, v)


def _sum_slots(x, name, extra=None):
    n, Rr, C = x.shape
    tr = _tile(Rr, (512, 256, 128, 64, 32, 16, 8))
    row = pl.BlockSpec((tr, C), lambda i: (i, 0))

    def body(*refs):
        if extra is None:
            x_ref, o_ref = refs
            acc = x_ref[0]
            lo = 1
        else:
            e_ref, x_ref, o_ref = refs
            acc = e_ref[...]
            lo = 0
        for k in range(lo, n):
            acc = acc + x_ref[k]
        o_ref[...] = acc

    ins = [x] if extra is None else [extra, x]
    specs = [pl.BlockSpec((n, tr, C), lambda i: (0, i, 0))]
    if extra is not None:
        specs = [row] + specs
    return pl.pallas_call(body, name=name, grid=(Rr // tr,), in_specs=specs, out_specs=row,
                          out_shape=jax.ShapeDtypeStruct((Rr, C), x.dtype),
                          compiler_params=_cparams(("parallel",)))(*ins)


def _place():
    return lax.axis_index("x"), lax.axis_index("y"), lax.axis_index("c")


def _all_gather(x, name):
    Rr, C = x.shape

    def body(x_ref, out_ref, send_sems, recv_sems, local_sem):
        mx, my, mc = _place()
        me, sibling = (mx, my, mc), (mx, my, 1 - mc)
        chips = [(1 - mx, my), (mx, 1 - my), (1 - mx, 1 - my)]

        def slot(px, py, pc):
            return out_ref.at[4 * px + 2 * py + pc]

        def copy(k, block, to, src=None):
            return pltpu.make_async_remote_copy(
                src_ref=slot(*block) if src is None else src, dst_ref=slot(*block),
                send_sem=send_sems.at[k], recv_sem=recv_sems.at[k],
                device_id=to, device_id_type=pl.DeviceIdType.MESH)

        mine = pltpu.make_async_copy(x_ref, slot(*me), local_sem)
        mine.start()
        first = [copy(0, me, sibling, src=x_ref)]
        first += [copy(1 + j, me, (*chip, mc), src=x_ref) for j, chip in enumerate(chips)]
        for cp in first:
            cp.start()
        passed = [copy(4 + j, (*chip, mc), sibling) for j, chip in enumerate(chips)]
        for j, chip in enumerate(chips):
            copy(1 + j, (*chip, mc), me).wait_recv()
            passed[j].start()
        copy(0, sibling, me).wait_recv()
        for j, chip in enumerate(chips):
            copy(4 + j, (*chip, 1 - mc), me).wait_recv()
        for cp in first + passed:
            cp.wait_send()
        mine.wait()

    return pl.pallas_call(
        body, name=name, in_specs=[ANY], out_specs=ANY,
        out_shape=jax.ShapeDtypeStruct((N_DEV, Rr, C), x.dtype),
        scratch_shapes=[pltpu.SemaphoreType.DMA((7,)), pltpu.SemaphoreType.DMA((7,)), pltpu.SemaphoreType.DMA],
    )(x)


def _pair_exchange(buf, name):
    _, n, Rr, C = buf.shape

    def body(b_ref, mine_ref, got_ref, send_sem, recv_sem, local_sem):
        mx, my, mc = _place()
        cp = pltpu.make_async_remote_copy(
            src_ref=b_ref.at[1 - mc], dst_ref=got_ref, send_sem=send_sem, recv_sem=recv_sem,
            device_id=(mx, my, 1 - mc), device_id_type=pl.DeviceIdType.MESH)
        cp.start()
        own = pltpu.make_async_copy(b_ref.at[mc], mine_ref, local_sem)
        own.start()
        own.wait()
        cp.wait()

    sds = jax.ShapeDtypeStruct((n, Rr, C), buf.dtype)
    return pl.pallas_call(
        body, name=name, in_specs=[ANY], out_specs=[ANY, ANY], out_shape=[sds, sds],
        scratch_shapes=[pltpu.SemaphoreType.DMA, pltpu.SemaphoreType.DMA, pltpu.SemaphoreType.DMA],
    )(buf)


def _chip_exchange(red, name):
    n, Rr, C = red.shape

    def body(r_ref, out_ref, send_sems, recv_sems, local_sem):
        mx, my, mc = _place()
        chips = [(1 - mx, my), (mx, 1 - my), (1 - mx, 1 - my)]
        mine = pltpu.make_async_copy(r_ref.at[2 * mx + my], out_ref.at[2 * mx + my], local_sem)
        mine.start()
        sends = []
        for k, (px, py) in enumerate(chips):
            cp = pltpu.make_async_remote_copy(
                src_ref=r_ref.at[2 * px + py], dst_ref=out_ref.at[2 * mx + my],
                send_sem=send_sems.at[k], recv_sem=recv_sems.at[k],
                device_id=(px, py, mc), device_id_type=pl.DeviceIdType.MESH)
            cp.start()
            sends.append(cp)
        for k, (px, py) in enumerate(chips):
            pltpu.make_async_remote_copy(
                src_ref=r_ref.at[2 * px + py], dst_ref=out_ref.at[2 * px + py],
                send_sem=send_sems.at[k], recv_sem=recv_sems.at[k],
                device_id=(px, py, mc), device_id_type=pl.DeviceIdType.MESH).wait_recv()
        for cp in sends:
            cp.wait_send()
        mine.wait()

    return pl.pallas_call(
        body, name=name, in_specs=[ANY], out_specs=ANY,
        out_shape=jax.ShapeDtypeStruct((n, Rr, C), red.dtype),
        scratch_shapes=[pltpu.SemaphoreType.DMA((3,)), pltpu.SemaphoreType.DMA((3,)), pltpu.SemaphoreType.DMA],
    )(red)


def _pad_rows(a, rows):
    return jnp.pad(a, ((0, rows - a.shape[0]), (0, 0)))


class _Layout:
    def __init__(self, D, shards):
        self.D = D
        self.pieces = []
        off = 0
        for name, layer, rows in shards:
            pr = _round_up(rows, 16)
            self.pieces.append((name, layer, rows, pr, off))
            off += pr
        self.rows = off

    def find(self, name, layer):
        for p in self.pieces:
            if p[0] == name and p[1] == layer:
                return p
        raise KeyError(name)


def kernel(x, c, ctx, c_ctx, w_ada, b_ada, g_mix, w_in, conv_w, conv_b, dt_bias, a_log, d_skip, ssd_norm_w, w_ssd_out, pool_w, pool_scale, w_pool_out, w_out, g_ffn, w_gate_up, w_down, g_final, loss_target, m_c_ctx, m_w_ada, m_b_ada, m_g_mix, m_w_in, m_conv_w, m_conv_b, m_dt_bias, m_a_log, m_d_skip, m_ssd_norm_w, m_w_ssd_out, m_pool_w, m_pool_scale, m_w_pool_out, m_w_out, m_g_ffn, m_w_gate_up, m_w_down, m_g_final, v_c_ctx, v_w_ada, v_b_ada, v_g_mix, v_w_in, v_conv_w, v_conv_b, v_dt_bias, v_a_log, v_d_skip, v_ssd_norm_w, v_w_ssd_out, v_pool_w, v_pool_scale, v_w_pool_out, v_w_out, v_g_ffn, v_w_gate_up, v_w_down, v_g_final):
    weights = dict(c_ctx=c_ctx, w_ada=w_ada, b_ada=b_ada, g_mix=g_mix, w_in=w_in, conv_w=conv_w, conv_b=conv_b,
                   dt_bias=dt_bias, a_log=a_log, d_skip=d_skip, ssd_norm_w=ssd_norm_w, w_ssd_out=w_ssd_out,
                   pool_w=pool_w, pool_scale=pool_scale, w_pool_out=w_pool_out, w_out=w_out, g_ffn=g_ffn,
                   w_gate_up=w_gate_up, w_down=w_down, g_final=g_final)
    moms_m = dict(c_ctx=m_c_ctx, w_ada=m_w_ada, b_ada=m_b_ada, g_mix=m_g_mix, w_in=m_w_in, conv_w=m_conv_w,
                  conv_b=m_conv_b, dt_bias=m_dt_bias, a_log=m_a_log, d_skip=m_d_skip, ssd_norm_w=m_ssd_norm_w,
                  w_ssd_out=m_w_ssd_out, pool_w=m_pool_w, pool_scale=m_pool_scale, w_pool_out=m_w_pool_out,
                  w_out=m_w_out, g_ffn=m_g_ffn, w_gate_up=m_w_gate_up, w_down=m_w_down, g_final=m_g_final)
    moms_v = dict(c_ctx=v_c_ctx, w_ada=v_w_ada, b_ada=v_b_ada, g_mix=v_g_mix, w_in=v_w_in, conv_w=v_conv_w,
                  conv_b=v_conv_b, dt_bias=v_dt_bias, a_log=v_a_log, d_skip=v_d_skip, ssd_norm_w=v_ssd_norm_w,
                  w_ssd_out=v_w_ssd_out, pool_w=v_pool_w, pool_scale=v_pool_scale, w_pool_out=v_w_pool_out,
                  w_out=v_w_out, g_ffn=v_g_ffn, w_gate_up=v_w_gate_up, w_down=v_w_down, g_final=v_g_final)
    order = ["c_ctx", "w_ada", "b_ada", "g_mix", "w_in", "conv_w", "conv_b", "dt_bias", "a_log", "d_skip",
             "ssd_norm_w", "w_ssd_out", "pool_w", "pool_scale", "w_pool_out", "w_out", "g_ffn", "w_gate_up",
             "w_down", "g_final"]
    big = ["w_ada", "w_in", "conv_w", "w_ssd_out", "pool_w", "w_pool_out", "w_out", "w_gate_up", "w_down"]
    small = [n for n in order if n not in big]

    depth = w_in.shape[0]
    L, D = x.shape[1], x.shape[2]
    n_ctx = ctx.shape[1]
    T = n_ctx + L
    in_cols = w_in.shape[2] * N_DEV
    xbc_w = conv_w.shape[2] * N_DEV
    dinner = ssd_norm_w.shape[1]
    H = dt_bias.shape[2]
    G = H // HPG
    GN = G * STATE
    assert xbc_w == dinner + 2 * GN and dinner == H * HEADDIM
    assert in_cols == dinner + xbc_w + 2 * H + D + 2 * D
    assert dinner == 2 * D and GN == D and 2 * H <= 128
    F = w_down.shape[1] * N_DEV
    pg = pool_w.shape[3]
    tr = n_ctx
    assert L % tr == 0 and tr % GRID_W == 0 and tr % CHUNK == 0 and L % CHUNK == 0
    n_ctx_tiles = 1
    NP = _round_up(9 * D + 128, 512)
    gs = dinner // G
    off_xbc, off_dt, off_pool = dinner, dinner + xbc_w, dinner + xbc_w + 2 * H
    off_gate = off_pool + D

    def shard_rows(name, l):
        w = weights[name][l]
        if name in ("w_ada", "w_in", "w_gate_up"):
            return w.T
        if name == "conv_w":
            w8 = _pad_rows(w, 8)
            hi = w8.astype(BF16)
            lo = (w8 - hi.astype(F32)).astype(BF16)
            return jnp.concatenate([hi, lo], axis=0).reshape(-1, D)
        if name == "pool_w":
            return w.reshape(-1, D)
        return w

    shards = [(n, l, shard_rows(n, l)) for l in range(depth) for n in big]
    lay = _Layout(D, [(n, l, a.shape[0]) for n, l, a in shards])
    packed = jnp.concatenate([_pad_rows(a.astype(BF16), lay.find(n, l)[3]) for n, l, a in shards], axis=0)
    gathered = _all_gather(packed, "gather_weights")

    def full(name, l):
        _, _, rows, _, off = lay.find(name, l)
        return gathered[:, off:off + rows, :]

    def w_inT_new(l):
        w = full("w_in", l).reshape(in_cols, D)
        parts = [w[:off_xbc], w[off_xbc:off_dt], w[off_pool:off_gate], w[off_gate:], w[off_dt:off_pool]]
        return _pad_rows(jnp.concatenate(parts, axis=0), NP)

    xs0 = jnp.concatenate([ctx[0], x[0]], axis=0)
    cc8 = _pad_rows(jnp.concatenate([c, c_ctx[None, :]], axis=0), 8)
    tgt = loss_target[0]

    def vec(a):
        return a.reshape(1, -1)

    def pad128(a):
        return jnp.pad(a.reshape(1, -1), ((0, 0), (0, 128 - 2 * H)))

    def to4(arr):
        return arr[:, :2 * H].reshape(T, 2, G, HPG).transpose(1, 2, 0, 3)

    def from4(arr):
        return jnp.pad(arr.transpose(2, 0, 1, 3).reshape(T, 2 * H), ((0, 0), (0, 128 - 2 * H)))

    dt_blk = (9 * D) // 128
    conv_tc = 128
    saved = []
    xcur = xs0
    for l in range(depth):
        W = dict(
            adaT=full("w_ada", l).reshape(6 * D, D), inT=w_inT_new(l),
            ssd=full("w_ssd_out", l).reshape(dinner, D), po=full("w_pool_out", l).reshape(D, D),
            out=full("w_out", l).reshape(D, D), guT=full("w_gate_up", l).reshape(2 * F, D),
            down=full("w_down", l).reshape(F, D),
            pool=full("pool_w", l).reshape(N_DEV, len(POOL_WINDOWS), pg // N_DEV, pg).transpose(1, 0, 2, 3)
            .reshape(len(POOL_WINDOWS), pg, pg),
        )
        cw = full("conv_w", l).reshape(N_DEV, 16, xbc_w // N_DEV).astype(F32)
        W["conv8"] = (cw[:, :8] + cw[:, 8:]).transpose(1, 0, 2).reshape(8, xbc_w)
        m6 = _ada_fwd(cc8, W["adaT"], vec(b_ada[l]), "ada_fwd")
        h = _norm_mod(xcur, vec(g_mix[l]), m6, 0, 1, n_ctx_tiles, tr, "norm_mod")
        proj = _mm(h, W["inT"], "nt", F32, "mm_in")
        dt, a_dt = _dt_prep(proj, dt_blk, pad128(dt_bias[l]), pad128(a_log[l]), tr, "dt_prep")
        dt4, a4 = to4(dt), to4(a_dt)
        aT4 = a4.transpose(0, 1, 3, 2)
        dsk = jnp.zeros((2, G, 8, 128), F32).at[:, :, 0, :HPG].set(d_skip[l].reshape(2, G, HPG))
        xbc = _conv_fwd(proj, dinner // conv_tc, xbc_w, W["conv8"], vec(conv_b[l]), n_ctx, conv_tc, "conv_fwd")
        y2, states = _ssd_fwd(xbc, dt4, a4, aT4, dsk, n_ctx, "ssd_fwd")
        yn = _gnorm_fwd(y2, proj, vec(ssd_norm_w[l]), gs, tr, "gnorm_fwd")
        pm = _pool_apply(proj, (6 * D) // pg, BF16, n_ctx, tr, pg, False, "pool_fwd")
        pms = _pool_mix_fwd(pm, W["pool"], vec(pool_scale[l]), tr, "pool_mix_fwd")
        o_ssd = _mm(yn, W["ssd"], "nn", F32, "mm_ssd_out")
        o_pool = _mm(pms, W["po"], "nn", F32, "mm_pool_out")
        mg = _merge_fwd(proj, 7, o_ssd, o_pool, tr, "merge_fwd")
        mo = _mm(mg, W["out"], "nn", F32, "mm_out")
        x1, h2 = _norm_mod(xcur, vec(g_ffn[l]), m6, 3, 4, n_ctx_tiles, tr, "resid_norm_mod", resid=(mo, 2))
        gu = _mm(h2, W["guT"], "nt", F32, "mm_gate_up")
        act = _swiglu_fwd(gu, tr, "swiglu_fwd")
        f = _mm(act, W["down"], "nn", F32, "mm_down")
        saved.append(dict(W=W, m6=m6, x0=xcur, h=h, proj=proj, dt4=dt4, a4=a4, aT4=aT4, xbc=xbc, y2=y2,
                          states=states, yn=yn, pm=pm, pms=pms, o_ssd=o_ssd, o_pool=o_pool, mg=mg, mo=mo, x1=x1,
                          h2=h2, gu=gu, act=act, f=f))
        xcur = _resid(x1, f, m6, 5, n_ctx_tiles, tr, "resid")

    loss_blk, dx, dgf = _loss_head(xcur, tgt, vec(g_final), n_ctx_tiles, tr, "loss_head")
    loss = lax.psum(loss_blk[0, 0], MESH_AXES)

    big_rows = {}
    small_g = {n: [None] * depth for n in small}
    d_c_ctx = jnp.zeros((D,), F32)
    for l in reversed(range(depth)):
        S = saved[l]
        W, m6, proj = S["W"], S["m6"], S["proj"]
        df, dga2 = _resid_bwd(dx, S["f"], m6, 5, n_ctx_tiles, tr, "resid_bwd")
        dact = _mm(df, W["down"], "nt", F32, "mm_down_dx")
        g_down = _mm(S["act"], df, "tn", F32, "mm_down_dw")
        dgu = _swiglu_bwd(S["gu"], dact, tr, "swiglu_bwd")
        dh2 = _mm(dgu, W["guT"], "nn", F32, "mm_gate_up_dx")
        g_guT = _mm(dgu, S["h2"], "tn", F32, "mm_gate_up_dw")
        dx1, st2 = _norm_mod_bwd(S["x1"], dh2, dx, vec(g_ffn[l]), m6, 4, n_ctx_tiles, tr, "norm_mod_bwd")
        dmo, dga1 = _resid_bwd(dx1, S["mo"], m6, 2, n_ctx_tiles, tr, "resid_bwd")
        dmg = _mm(dmo, W["out"], "nt", F32, "mm_out_dx")
        g_out = _mm(S["mg"], dmo, "tn", F32, "mm_out_dw")
        do_ssd, do_pool, dgl = _merge_bwd(proj, 7, S["o_ssd"], S["o_pool"], dmg, tr, "merge_bwd")
        dyn = _mm(do_ssd, W["ssd"], "nt", F32, "mm_ssd_out_dx")
        g_ssd = _mm(S["yn"], do_ssd, "tn", F32, "mm_ssd_out_dw")
        dpms = _mm(do_pool, W["po"], "nt", F32, "mm_pool_out_dx")
        g_po = _mm(S["pms"], do_pool, "tn", F32, "mm_pool_out_dw")
        dpm, g_pool, dps = _pool_mix_bwd(S["pm"], dpms, W["pool"], vec(pool_scale[l]), tr, "pool_mix_bwd")
        dup = _pool_apply(dpm, 0, BF16, n_ctx, tr, pg, True, "pool_bwd")
        dy, dz, dnw = _gnorm_bwd(S["y2"], proj, vec(ssd_norm_w[l]), dyn, gs, tr, "gnorm_bwd")
        dxs2, db2, dc2, ddt4, da4, daT4 = _ssd_bwd(S["xbc"], dy, S["states"], S["dt4"], S["a4"], S["aT4"], n_ctx,
                                                   "ssd_bwd")
        dskv = _pad_rows(jnp.repeat(d_skip[l], HEADDIM, axis=1), 8)
        dxbc_act, ddsk = _ssd_combine(dxs2, db2, dc2, dy, S["xbc"], dskv, tr, "ssd_combine")
        dxbc, dconv = _conv_bwd(proj, dinner // conv_tc, dxbc_act, W["conv8"], vec(conv_b[l]), n_ctx, conv_tc,
                                "conv_bwd")
        ddt_raw, dtst = _dt_bwd(proj, dt_blk, pad128(dt_bias[l]), pad128(a_log[l]), from4(ddt4),
                                from4(da4 + daT4.transpose(0, 1, 3, 2)), tr, "dt_bwd")
        dproj = jnp.concatenate([dz, dxbc, dup, dgl, ddt_raw, jnp.zeros((T, NP - 9 * D - 128), BF16)], axis=1)
        dh = _mm(dproj, W["inT"], "nn", F32, "mm_in_dx")
        g_inT_new = _mm(dproj, S["h"], "tn", F32, "mm_in_dw")
        dx0, st1 = _norm_mod_bwd(S["x0"], dh, dx1, vec(g_mix[l]), m6, 1, n_ctx_tiles, tr, "norm_mod_bwd")
        dm6 = _pad_rows(jnp.concatenate([st1[0:2], st1[2:4], dga1[0:2], st2[0:2], st2[2:4], dga2[0:2]], axis=1), 8)
        dsil = _mm(dm6, W["adaT"], "nn", F32, "mm_ada_dx")
        sil_b, dcc, dbada = _ada_bwd_small(cc8, dsil, dm6, "ada_bwd_small")
        g_adaT = _mm(dm6, sil_b, "tn", F32, "mm_ada_dw")
        d_c_ctx = d_c_ctx + dcc[1]
        dx = dx0

        g_inT = jnp.concatenate([g_inT_new[:6 * D], g_inT_new[9 * D:9 * D + 2 * H], g_inT_new[6 * D:9 * D]], axis=0)
        big_rows[("w_ada", l)] = g_adaT.reshape(N_DEV, -1, D)
        big_rows[("w_in", l)] = g_inT.reshape(N_DEV, -1, D)
        big_rows[("conv_w", l)] = jnp.pad(
            dconv[:CONV_K].reshape(CONV_K, N_DEV, xbc_w // N_DEV).transpose(1, 0, 2),
            ((0, 0), (0, 16 - CONV_K), (0, 0))).reshape(N_DEV, -1, D)
        big_rows[("w_ssd_out", l)] = g_ssd.reshape(N_DEV, -1, D)
        big_rows[("pool_w", l)] = g_pool.reshape(len(POOL_WINDOWS), N_DEV, pg // N_DEV, pg).transpose(1, 0, 2, 3) \
            .reshape(N_DEV, -1, D)
        big_rows[("w_pool_out", l)] = g_po.reshape(N_DEV, -1, D)
        big_rows[("w_out", l)] = g_out.reshape(N_DEV, -1, D)
        big_rows[("w_gate_up", l)] = g_guT.reshape(N_DEV, -1, D)
        big_rows[("w_down", l)] = g_down.reshape(N_DEV, -1, D)
        small_g["b_ada"][l] = dbada[0]
        small_g["g_mix"][l] = st1[4]
        small_g["conv_b"][l] = dconv[CONV_K]
        small_g["dt_bias"][l] = dtst[0, :2 * H].reshape(2, H)
        small_g["a_log"][l] = dtst[1, :2 * H].reshape(2, H)
        dsk_h = ddsk[0].reshape(H, HEADDIM).sum(axis=-1)
        small_g["d_skip"][l] = jnp.stack([dsk_h, dsk_h])
        small_g["ssd_norm_w"][l] = dnw[0]
        small_g["pool_scale"][l] = dps[0]
        small_g["g_ffn"][l] = st2[4]
    grad_x = dx[n_ctx:][None]

    gbuf = jnp.concatenate(
        [jnp.pad(big_rows[(n, l)], ((0, 0), (0, pr - rows), (0, 0))) for n, l, rows, pr, _ in lay.pieces], axis=1)
    gbuf = gbuf.reshape(2, 2, 2, lay.rows, D).transpose(2, 0, 1, 3, 4).reshape(2, 4, lay.rows, D)
    mine, got = _pair_exchange(gbuf, "rs_pair_exchange")
    red = _sum_slots(got.reshape(1, 4 * lay.rows, D), "rs_pair_add", extra=mine.reshape(4 * lay.rows, D))
    slots = _chip_exchange(red.reshape(4, lay.rows, D), "rs_chip_exchange")
    g_local = _sum_slots(slots, "rs_chip_add")

    def local_grad(name):
        outs = []
        for l in range(depth):
            _, _, rows, _, off = lay.find(name, l)
            piece = g_local[off:off + rows]
            if name in ("w_ada", "w_in", "w_gate_up"):
                piece = piece.T
            elif name == "conv_w":
                piece = piece.reshape(16, -1)[:CONV_K]
            elif name == "pool_w":
                piece = piece.reshape(weights[name].shape[1:])
            outs.append(piece)
        return jnp.stack(outs)

    grads = {n: local_grad(n) for n in big}

    small_full = {"c_ctx": d_c_ctx, "g_final": dgf[0]}
    for n in small:
        if n not in small_full:
            small_full[n] = jnp.stack(small_g[n])

    def pack_small(tree):
        flat = jnp.concatenate([tree[n].reshape(-1).astype(F32) for n in small])
        rows = _round_up(-(-flat.shape[0] // D), 8)
        return jnp.pad(flat, (0, rows * D - flat.shape[0])).reshape(rows, D)

    def unpack_small(buf):
        flat, out, off = buf.reshape(-1), {}, 0
        for n in small:
            sz = weights[n].size
            out[n] = flat[off:off + sz].reshape(weights[n].shape)
            off += sz
        return out

    g_small = _sum_slots(_all_gather(pack_small(small_full), "gather_small_grads"), "sum_small_grads")
    grads.update(unpack_small(g_small))

    delta, new_m, new_v = {}, {}, {}
    for n in big:
        shp = weights[n].shape
        d_, m_, v_ = _adamw(weights[n].reshape(-1, shp[-1]), grads[n].reshape(-1, shp[-1]),
                            moms_m[n].reshape(-1, shp[-1]), moms_v[n].reshape(-1, shp[-1]), "adamw_" + n)
        delta[n], new_m[n], new_v[n] = d_.reshape(shp), m_.reshape(shp), v_.reshape(shp)
    d_, m_, v_ = _adamw(pack_small(weights), g_small, pack_small(moms_m), pack_small(moms_v), "adamw_small")
    delta.update(unpack_small(d_))
    new_m.update(unpack_small(m_))
    new_v.update(unpack_small(v_))

    return (loss, grad_x, *[grads[n] for n in order], *[delta[n] for n in order],
            *[new_m[n] for n in order], *[new_v[n] for n in order])
```

```python
import functools

import jax
import jax.numpy as jnp
from jax import lax
from jax.experimental import pallas as pl
from jax.experimental.pallas import tpu as pltpu

F32 = jnp.float32
BF16 = jnp.bfloat16
N_DEV = 8
EPS = 1e-6
GRID_W = 64
POOL_WINDOWS = (2, 4, 8, 16)
HEADDIM = 64
STATE = 128
CHUNK = 128
HPG = 4
CONV_K = 5
ADAM_LR, ADAM_B1, ADAM_B2, ADAM_EPS, ADAM_WD, ADAM_STEP = 0.001, 0.9, 0.999, 1e-08, 0.01, 10
NEG_BIG = -1e30
MESH_AXES = ("x", "y", "c")
ANY = pl.BlockSpec(memory_space=pl.ANY)


def _tile(n, cands):
    for t in cands:
        if n % t == 0:
            return t
    return n


def _round_up(n, m):
    return -(-n // m) * m


def _silu(x):
    return x * jax.nn.sigmoid(x)


def _dsilu(x):
    s = jax.nn.sigmoid(x)
    return s * (1.0 + x * (1.0 - s))


def _cparams(sem):
    return pltpu.CompilerParams(dimension_semantics=sem, vmem_limit_bytes=56 * 1024 * 1024)


def _mm(a, b, mode, out_dtype, name):
    if mode == "tn":
        K, M = a.shape
        N = b.shape[1]
        tm = _tile(M, (512, 256, 128))
        tn = _tile(N, (1024, 512, 256, 128))
        tk = _tile(K, (1088, 544, 512, 256, 128))
        a_spec = pl.BlockSpec((tk, tm), lambda i, j, k: (k, i))
        b_spec = pl.BlockSpec((tk, tn), lambda i, j, k: (k, j))
        dims = (((0,), (0,)), ((), ()))
    else:
        M, K = a.shape
        N = b.shape[0] if mode == "nt" else b.shape[1]
        tm = _tile(M, (1088, 544, 512, 256, 128))
        tn = _tile(N, (512, 256, 128))
        tk = K if K <= 2048 else _tile(K, (1024, 512, 256, 128))
        a_spec = pl.BlockSpec((tm, tk), lambda i, j, k: (i, k))
        if mode == "nt":
            b_spec = pl.BlockSpec((tn, tk), lambda i, j, k: (j, k))
            dims = (((1,), (1,)), ((), ()))
        else:
            b_spec = pl.BlockSpec((tk, tn), lambda i, j, k: (k, j))
            dims = (((1,), (0,)), ((), ()))
    nk = K // tk

    def body(a_ref, b_ref, o_ref, acc):
        k = pl.program_id(2)

        @pl.when(k == 0)
        def _():
            acc[...] = jnp.zeros_like(acc)

        acc[...] += lax.dot_general(a_ref[...].astype(BF16), b_ref[...].astype(BF16), dims,
                                    preferred_element_type=F32)

        @pl.when(k == nk - 1)
        def _():
            o_ref[...] = acc[...].astype(o_ref.dtype)

    return pl.pallas_call(
        body, name=name, grid=(M // tm, N // tn, nk),
        in_specs=[a_spec, b_spec], out_specs=pl.BlockSpec((tm, tn), lambda i, j, k: (i, j)),
        out_shape=jax.ShapeDtypeStruct((M, N), out_dtype),
        scratch_shapes=[pltpu.VMEM((tm, tn), F32)],
        compiler_params=_cparams(("parallel", "parallel", "arbitrary")),
    )(a, b)


def _ada_fwd(cc8, w_adaT, b_ada, name):
    D = cc8.shape[1]
    N = w_adaT.shape[0]
    tn = _tile(N, (512, 256, 128))

    def body(c_ref, w_ref, b_ref, o_ref):
        a = _silu(c_ref[...]).astype(BF16)
        o_ref[...] = lax.dot_general(a, w_ref[...], (((1,), (1,)), ((), ())),
                                     preferred_element_type=F32) + b_ref[...]

    return pl.pallas_call(
        body, name=name, grid=(N // tn,),
        in_specs=[pl.BlockSpec((8, D), lambda j: (0, 0)), pl.BlockSpec((tn, D), lambda j: (j, 0)),
                  pl.BlockSpec((1, tn), lambda j: (0, j))],
        out_specs=pl.BlockSpec((8, tn), lambda j: (0, j)),
        out_shape=jax.ShapeDtypeStruct((8, N), F32),
        compiler_params=_cparams(("parallel",)),
    )(cc8, w_adaT, b_ada)


def _ada_bwd_small(cc8, dsil, dm6, name):
    D = cc8.shape[1]
    N = dm6.shape[1]

    def body(c_ref, ds_ref, dm_ref, sil_ref, dc_ref, db_ref):
        c = c_ref[...]
        sil_ref[...] = _silu(c).astype(BF16)
        dc_ref[...] = ds_ref[...] * _dsilu(c)
        dm = dm_ref[...]
        row = lax.broadcasted_iota(jnp.int32, dm.shape, 0)
        db_ref[...] = jnp.where(row == 0, jnp.sum(dm, axis=0, keepdims=True), 0.0)

    return pl.pallas_call(
        body, name=name, grid=(1,),
        in_specs=[pl.BlockSpec((8, D), lambda i: (0, 0)), pl.BlockSpec((8, D), lambda i: (0, 0)),
                  pl.BlockSpec((8, N), lambda i: (0, 0))],
        out_specs=[pl.BlockSpec((8, D), lambda i: (0, 0)), pl.BlockSpec((8, D), lambda i: (0, 0)),
                   pl.BlockSpec((8, N), lambda i: (0, 0))],
        out_shape=[jax.ShapeDtypeStruct((8, D), BF16), jax.ShapeDtypeStruct((8, D), F32),
                   jax.ShapeDtypeStruct((8, N), F32)],
        compiler_params=_cparams(("arbitrary",)),
    )(cc8, dsil, dm6)


def _seg_pick(m_ref, is_ctx):
    return jnp.where(is_ctx, m_ref[1:2, :], m_ref[0:1, :])


def _norm_mod(x, g, m6, sh_idx, sc_idx, n_ctx_tiles, tr, name, resid=None):
    T, D = x.shape
    row = pl.BlockSpec((tr, D), lambda i: (i, 0))
    vec = pl.BlockSpec((1, D), lambda i: (0, 0))

    def mcol(idx):
        return pl.BlockSpec((8, D), lambda i: (0, idx))

    def body(*refs):
        if resid is None:
            x_ref, g_ref, sh_ref, sc_ref, h_ref = refs
            xv = x_ref[...]
        else:
            x_ref, f_ref, ga_ref, g_ref, sh_ref, sc_ref, xo_ref, h_ref = refs
        is_ctx = pl.program_id(0) < n_ctx_tiles
        if resid is not None:
            xv = x_ref[...] + _seg_pick(ga_ref, is_ctx) * f_ref[...]
            xo_ref[...] = xv
        rstd = lax.rsqrt(jnp.mean(xv * xv, axis=-1, keepdims=True) + EPS)
        hn = xv * rstd * g_ref[...]
        h_ref[...] = (hn * (1.0 + _seg_pick(sc_ref, is_ctx)) + _seg_pick(sh_ref, is_ctx)).astype(BF16)

    if resid is None:
        ins, in_specs = [x, g, m6, m6], [row, vec, mcol(sh_idx), mcol(sc_idx)]
        out_specs, out_shape = row, jax.ShapeDtypeStruct((T, D), BF16)
    else:
        f, ga_idx = resid
        ins = [x, f, m6, g, m6, m6]
        in_specs = [row, row, mcol(ga_idx), vec, mcol(sh_idx), mcol(sc_idx)]
        out_specs = [row, row]
        out_shape = [jax.ShapeDtypeStruct((T, D), F32), jax.ShapeDtypeStruct((T, D), BF16)]
    return pl.pallas_call(body, name=name, grid=(T // tr,), in_specs=in_specs, out_specs=out_specs,
                          out_shape=out_shape, compiler_params=_cparams(("parallel",)))(*ins)


def _resid(x, f, m6, ga_idx, n_ctx_tiles, tr, name):
    T, D = x.shape
    row = pl.BlockSpec((tr, D), lambda i: (i, 0))

    def body(x_ref, f_ref, ga_ref, o_ref):
        is_ctx = pl.program_id(0) < n_ctx_tiles
        o_ref[...] = x_ref[...] + _seg_pick(ga_ref, is_ctx) * f_ref[...]

    return pl.pallas_call(body, name=name, grid=(T // tr,),
                          in_specs=[row, row, pl.BlockSpec((8, D), lambda i: (0, ga_idx))], out_specs=row,
                          out_shape=jax.ShapeDtypeStruct((T, D), F32),
                          compiler_params=_cparams(("parallel",)))(x, f, m6)


def _resid_bwd(dx, f, m6, ga_idx, n_ctx_tiles, tr, name):
    T, D = dx.shape
    row = pl.BlockSpec((tr, D), lambda i: (i, 0))
    acc = pl.BlockSpec((8, D), lambda i: (0, 0))

    def body(dx_ref, f_ref, ga_ref, df_ref, dga_ref):
        i = pl.program_id(0)
        is_ctx = i < n_ctx_tiles

        @pl.when(i == 0)
        def _():
            dga_ref[...] = jnp.zeros_like(dga_ref)

        dxv = dx_ref[...]
        df_ref[...] = (_seg_pick(ga_ref, is_ctx) * dxv).astype(BF16)
        s = jnp.sum(dxv * f_ref[...], axis=0, keepdims=True)
        r = lax.broadcasted_iota(jnp.int32, (8, D), 0)
        dga_ref[...] += jnp.where(r == jnp.where(is_ctx, 1, 0), s, 0.0)

    return pl.pallas_call(body, name=name, grid=(T // tr,),
                          in_specs=[row, row, pl.BlockSpec((8, D), lambda i: (0, ga_idx))],
                          out_specs=[row, acc],
                          out_shape=[jax.ShapeDtypeStruct((T, D), BF16), jax.ShapeDtypeStruct((8, D), F32)],
                          compiler_params=_cparams(("arbitrary",)))(dx, f, m6)


def _norm_mod_bwd(x, dh, dxres, g, m6, sc_idx, n_ctx_tiles, tr, name):
    T, D = x.shape
    row = pl.BlockSpec((tr, D), lambda i: (i, 0))
    acc = pl.BlockSpec((8, D), lambda i: (0, 0))

    def body(x_ref, dh_ref, dr_ref, g_ref, sc_ref, dx_ref, st_ref):
        i = pl.program_id(0)
        is_ctx = i < n_ctx_tiles

        @pl.when(i == 0)
        def _():
            st_ref[...] = jnp.zeros_like(st_ref)

        xv, dh_v, gv = x_ref[...], dh_ref[...], g_ref[...]
        sc1 = 1.0 + _seg_pick(sc_ref, is_ctx)
        rstd = lax.rsqrt(jnp.mean(xv * xv, axis=-1, keepdims=True) + EPS)
        xhat = xv * rstd
        dxhat = dh_v * sc1 * gv
        dx_ref[...] = dr_ref[...] + rstd * (dxhat - xhat * jnp.mean(dxhat * xhat, axis=-1, keepdims=True))
        dsh = jnp.sum(dh_v, axis=0, keepdims=True)
        dsc = jnp.sum(dh_v * xhat * gv, axis=0, keepdims=True)
        dg = jnp.sum(dh_v * sc1 * xhat, axis=0, keepdims=True)
        r = lax.broadcasted_iota(jnp.int32, (8, D), 0)
        seg = jnp.where(is_ctx, 1, 0)
        st_ref[...] += (jnp.where(r == seg, dsh, 0.0) + jnp.where(r == 2 + seg, dsc, 0.0)
                        + jnp.where(r == 4, dg, 0.0))

    return pl.pallas_call(body, name=name, grid=(T // tr,),
                          in_specs=[row, row, row, pl.BlockSpec((1, D), lambda i: (0, 0)),
                                    pl.BlockSpec((8, D), lambda i: (0, sc_idx))],
                          out_specs=[row, acc],
                          out_shape=[jax.ShapeDtypeStruct((T, D), F32), jax.ShapeDtypeStruct((8, D), F32)],
                          compiler_params=_cparams(("arbitrary",)))(x, dh, dxres, g, m6)


def _loss_head(x, tgt, g, n_ctx_tiles, tr, name):
    T, D = x.shape
    row = pl.BlockSpec((tr, D), lambda i: (i, 0))

    def body(x_ref, t_ref, g_ref, l_ref, dx_ref, dg_ref):
        i = pl.program_id(0)

        @pl.when(i == 0)
        def _():
            l_ref[...] = jnp.zeros_like(l_ref)
            dg_ref[...] = jnp.zeros_like(dg_ref)

        @pl.when(i < n_ctx_tiles)
        def _():
            dx_ref[...] = jnp.zeros_like(dx_ref)

        @pl.when(i >= n_ctx_tiles)
        def _():
            xv, gv = x_ref[...], g_ref[...]
            rstd = lax.rsqrt(jnp.mean(xv * xv, axis=-1, keepdims=True) + EPS)
            xhat = xv * rstd
            e = xhat * gv - t_ref[...]
            l_ref[...] += 0.5 * jnp.sum(jnp.mean(e * e, axis=-1, keepdims=True), axis=0, keepdims=True)
            dy = e * (1.0 / D)
            dxhat = dy * gv
            dx_ref[...] = rstd * (dxhat - xhat * jnp.mean(dxhat * xhat, axis=-1, keepdims=True))
            r = lax.broadcasted_iota(jnp.int32, (8, D), 0)
            dg_ref[...] += jnp.where(r == 0, jnp.sum(dy * xhat, axis=0, keepdims=True), 0.0)

    return pl.pallas_call(
        body, name=name, grid=(T // tr,),
        in_specs=[row, pl.BlockSpec((tr, D), lambda i: (jnp.maximum(i - n_ctx_tiles, 0), 0)),
                  pl.BlockSpec((1, D), lambda i: (0, 0))],
        out_specs=[pl.BlockSpec((8, 128), lambda i: (0, 0)), row, pl.BlockSpec((8, D), lambda i: (0, 0))],
        out_shape=[jax.ShapeDtypeStruct((8, 128), F32), jax.ShapeDtypeStruct((T, D), F32),
                   jax.ShapeDtypeStruct((8, D), F32)],
        compiler_params=_cparams(("arbitrary",)))(x, tgt, g)


def _seq_masks(T, n_ctx, width):
    row = lax.broadcasted_iota(jnp.int32, (T, width), 0)
    in_ctx = row < n_ctx
    return jnp.where(in_ctx, row, row - n_ctx), jnp.where(in_ctx, n_ctx, T - n_ctx)


def _shift_rows(u, off, t_loc, seg_len):
    T = u.shape[0]
    if off == 0:
        return u
    v = pltpu.roll(u, (-off) % T, 0)
    ok = (t_loc + off >= 0) & (t_loc + off < seg_len)
    return jnp.where(ok, v, 0.0)


def _conv_fwd(proj, col0_blk, ncol, conv_w8, conv_b, n_ctx, tc, name):
    T = proj.shape[0]

    def body(u_ref, w_ref, b_ref, o_ref):
        u = u_ref[...]
        t_loc, seg_len = _seq_masks(T, n_ctx, tc)
        acc = jnp.broadcast_to(b_ref[...], u.shape)
        for i in range(CONV_K):
            acc = acc + w_ref[i:i + 1, :] * _shift_rows(u, i - CONV_K // 2, t_loc, seg_len)
        o_ref[...] = _silu(acc)

    return pl.pallas_call(
        body, name=name, grid=(ncol // tc,),
        in_specs=[pl.BlockSpec((T, tc), lambda j: (0, col0_blk + j)), pl.BlockSpec((8, tc), lambda j: (0, j)),
                  pl.BlockSpec((1, tc), lambda j: (0, j))],
        out_specs=pl.BlockSpec((T, tc), lambda j: (0, j)),
        out_shape=jax.ShapeDtypeStruct((T, ncol), F32),
        compiler_params=_cparams(("parallel",)))(proj, conv_w8, conv_b)


def _conv_bwd(proj, col0_blk, dact, conv_w8, conv_b, n_ctx, tc, name):
    T, ncol = dact.shape

    def body(u_ref, d_ref, w_ref, b_ref, du_ref, dw_ref):
        u = u_ref[...]
        t_loc, seg_len = _seq_masks(T, n_ctx, tc)
        pre = jnp.broadcast_to(b_ref[...], u.shape)
        shifted = []
        for i in range(CONV_K):
            s = _shift_rows(u, i - CONV_K // 2, t_loc, seg_len)
            shifted.append(s)
            pre = pre + w_ref[i:i + 1, :] * s
        dpre = d_ref[...] * _dsilu(pre)
        du = jnp.zeros_like(u)
        r = lax.broadcasted_iota(jnp.int32, (8, tc), 0)
        dw = jnp.where(r == CONV_K, jnp.sum(dpre, axis=0, keepdims=True), 0.0)
        for i in range(CONV_K):
            du = du + w_ref[i:i + 1, :] * _shift_rows(dpre, -(i - CONV_K // 2), t_loc, seg_len)
            dw = dw + jnp.where(r == i, jnp.sum(dpre * shifted[i], axis=0, keepdims=True), 0.0)
        du_ref[...] = du.astype(BF16)
        dw_ref[...] = dw

    return pl.pallas_call(
        body, name=name, grid=(ncol // tc,),
        in_specs=[pl.BlockSpec((T, tc), lambda j: (0, col0_blk + j)), pl.BlockSpec((T, tc), lambda j: (0, j)),
                  pl.BlockSpec((8, tc), lambda j: (0, j)), pl.BlockSpec((1, tc), lambda j: (0, j))],
        out_specs=[pl.BlockSpec((T, tc), lambda j: (0, j)), pl.BlockSpec((8, tc), lambda j: (0, j))],
        out_shape=[jax.ShapeDtypeStruct((T, ncol), BF16), jax.ShapeDtypeStruct((8, ncol), F32)],
        compiler_params=_cparams(("parallel",)))(proj, dact, conv_w8, conv_b)


def _pool_core(u, half, t_loc, seg_len, transpose):
    tr = u.shape[0]

    def shift(v, s):
        w = pltpu.roll(v, s % tr, 0)
        ok = (t_loc - s >= 0) & (t_loc - s < seg_len)
        return jnp.where(ok, w, 0.0)

    cnt = (jnp.minimum(t_loc, half) + jnp.minimum(seg_len - t_loc, half)).astype(F32)
    q = u / cnt if transpose else u
    back, ahead, h = q, q, 1
    while h < half:
        back = back + shift(back, h)
        ahead = ahead + shift(ahead, -h)
        h *= 2
    if transpose:
        tot = back + shift(ahead, -1)
        return tot - u
    tot = shift(back, 1) + ahead
    return tot / cnt - u


def _pool_apply(src, col0_blk, out_dtype, n_ctx, tr, pg, transpose, name):
    T = src.shape[0]
    n_ctx_tiles = n_ctx // tr

    def body(u_ref, o_ref):
        i, gi = pl.program_id(0), pl.program_id(1)
        row = lax.broadcasted_iota(jnp.int32, (tr, pg), 0)
        seg_len = jnp.where(i < n_ctx_tiles, tr, GRID_W)
        t_loc = row & (seg_len - 1)
        u = u_ref[...].astype(F32)
        for k_idx, k in enumerate(POOL_WINDOWS):
            @pl.when(gi == k_idx)
            def _(k=k):
                o_ref[...] = _pool_core(u, k // 2, t_loc, seg_len, transpose).astype(o_ref.dtype)

    return pl.pallas_call(
        body, name=name, grid=(T // tr, len(POOL_WINDOWS)),
        in_specs=[pl.BlockSpec((tr, pg), lambda i, gi: (i, col0_blk + gi))],
        out_specs=pl.BlockSpec((tr, pg), lambda i, gi: (i, gi)),
        out_shape=jax.ShapeDtypeStruct((T, pg * len(POOL_WINDOWS)), out_dtype),
        compiler_params=_cparams(("parallel", "parallel")))(src)


def _pool_mix_fwd(pm, pool_w, pool_scale, tr, name):
    T, W = pm.shape
    ng, pg = pool_w.shape[0], pool_w.shape[1]

    def body(p_ref, w_ref, s_ref, o_ref):
        o_ref[...] = (jnp.dot(p_ref[...], w_ref[...], preferred_element_type=F32) * s_ref[...]).astype(BF16)

    return pl.pallas_call(
        body, name=name, grid=(T // tr, ng),
        in_specs=[pl.BlockSpec((tr, pg), lambda i, g: (i, g)), pl.BlockSpec((None, pg, pg), lambda i, g: (g, 0, 0)),
                  pl.BlockSpec((1, pg), lambda i, g: (0, g))],
        out_specs=pl.BlockSpec((tr, pg), lambda i, g: (i, g)),
        out_shape=jax.ShapeDtypeStruct((T, W), BF16),
        compiler_params=_cparams(("parallel", "parallel")))(pm, pool_w, pool_scale)


def _pool_mix_bwd(pm, dpms, pool_w, pool_scale, tr, name):
    T, W = pm.shape
    ng, pg = pool_w.shape[0], pool_w.shape[1]

    def body(p_ref, d_ref, w_ref, s_ref, dp_ref, dw_ref, ds_ref):
        i = pl.program_id(1)

        @pl.when(i == 0)
        def _():
            dw_ref[...] = jnp.zeros_like(dw_ref)
            ds_ref[...] = jnp.zeros_like(ds_ref)

        p, w = p_ref[...], w_ref[...]
        d = d_ref[...].astype(F32)
        pmix = jnp.dot(p, w, preferred_element_type=F32)
        r = lax.broadcasted_iota(jnp.int32, (8, pg), 0)
        ds_ref[...] += jnp.where(r == 0, jnp.sum(d * pmix, axis=0, keepdims=True), 0.0)
        dmix = (d * s_ref[...]).astype(BF16)
        dp_ref[...] = lax.dot_general(dmix, w, (((1,), (1,)), ((), ())), preferred_element_type=F32)
        dw_ref[...] += lax.dot_general(p, dmix, (((0,), (0,)), ((), ())), preferred_element_type=F32)

    return pl.pallas_call(
        body, name=name, grid=(ng, T // tr),
        in_specs=[pl.BlockSpec((tr, pg), lambda g, i: (i, g)), pl.BlockSpec((tr, pg), lambda g, i: (i, g)),
                  pl.BlockSpec((None, pg, pg), lambda g, i: (g, 0, 0)), pl.BlockSpec((1, pg), lambda g, i: (0, g))],
        out_specs=[pl.BlockSpec((tr, pg), lambda g, i: (i, g)), pl.BlockSpec((None, pg, pg), lambda g, i: (g, 0, 0)),
                   pl.BlockSpec((8, pg), lambda g, i: (0, g))],
        out_shape=[jax.ShapeDtypeStruct((T, W), F32), jax.ShapeDtypeStruct((ng, pg, pg), F32),
                   jax.ShapeDtypeStruct((8, W), F32)],
        compiler_params=_cparams(("parallel", "arbitrary")))(pm, dpms, pool_w, pool_scale)


def _chunk_cumsum(v, upper):
    Q = v.shape[0]
    ii = lax.broadcasted_iota(jnp.int32, (Q, Q), 0)
    jj = lax.broadcasted_iota(jnp.int32, (Q, Q), 1)
    tri = ((jj >= ii) if upper else (jj <= ii)).astype(BF16)
    h1 = v.astype(BF16)
    r1 = v - h1.astype(F32)
    h2 = r1.astype(BF16)
    h3 = (r1 - h2.astype(F32)).astype(BF16)
    return (jnp.dot(tri, h1, preferred_element_type=F32) + jnp.dot(tri, h2, preferred_element_type=F32)
            + jnp.dot(tri, h3, preferred_element_type=F32))


def _dt_prep(proj, dt_blk, bias, a_log, n_heads, tr, name):
    T = proj.shape[0]
    row = pl.BlockSpec((tr, 128), lambda i: (i, 0))

    def body(r_ref, b_ref, al_ref, dt_ref, lam_ref):
        xv = r_ref[...] + b_ref[...]
        dt = jnp.maximum(xv, 0.0) + jnp.log(1.0 + jnp.exp(-jnp.abs(xv)))
        dt_ref[...] = dt
        a = -jnp.exp(al_ref[...]) * dt
        col = lax.broadcasted_iota(jnp.int32, (CHUNK, 128), 1)
        for k in range(tr // CHUNK):
            ak = a[k * CHUNK:(k + 1) * CHUNK]
            lam_ref[k * CHUNK:(k + 1) * CHUNK, :] = jnp.where(col < n_heads, _chunk_cumsum(ak, False),
                                                              _chunk_cumsum(ak, True))

    return pl.pallas_call(
        body, name=name, grid=(T // tr,),
        in_specs=[pl.BlockSpec((tr, 128), lambda i: (i, dt_blk)), pl.BlockSpec((1, 128), lambda i: (0, 0)),
                  pl.BlockSpec((1, 128), lambda i: (0, 0))],
        out_specs=[row, row],
        out_shape=[jax.ShapeDtypeStruct((T, 128), F32), jax.ShapeDtypeStruct((T, 128), F32)],
        compiler_params=_cparams(("parallel",)))(proj, bias, a_log)


def _dt_bwd(proj, dt_blk, bias, a_log, ddt, dlam, n_heads, tr, name):
    T = proj.shape[0]
    row = pl.BlockSpec((tr, 128), lambda i: (i, 0))
    vec = pl.BlockSpec((1, 128), lambda i: (0, 0))

    def body(r_ref, b_ref, al_ref, ddt_ref, dl_ref, o_ref, st_ref):
        @pl.when(pl.program_id(0) == 0)
        def _():
            st_ref[...] = jnp.zeros_like(st_ref)

        xv = r_ref[...] + b_ref[...]
        dt = jnp.maximum(xv, 0.0) + jnp.log(1.0 + jnp.exp(-jnp.abs(xv)))
        a_neg = -jnp.exp(al_ref[...])
        col = lax.broadcasted_iota(jnp.int32, (CHUNK, 128), 1)
        dl = dl_ref[...]
        parts = []
        for k in range(tr // CHUNK):
            dk = dl[k * CHUNK:(k + 1) * CHUNK]
            parts.append(jnp.where(col < n_heads, _chunk_cumsum(dk, True), _chunk_cumsum(dk, False)))
        dav = jnp.concatenate(parts, axis=0)
        draw = (ddt_ref[...] + dav * a_neg) * jax.nn.sigmoid(xv)
        o_ref[...] = draw.astype(BF16)
        r = lax.broadcasted_iota(jnp.int32, (8, 128), 0)
        st_ref[...] += (jnp.where(r == 0, jnp.sum(draw, axis=0, keepdims=True), 0.0)
                        + jnp.where(r == 1, jnp.sum(dav * dt, axis=0, keepdims=True) * a_neg, 0.0))

    return pl.pallas_call(
        body, name=name, grid=(T // tr,),
        in_specs=[pl.BlockSpec((tr, 128), lambda i: (i, dt_blk)), vec, vec, row, row],
        out_specs=[row, pl.BlockSpec((8, 128), lambda i: (0, 0))],
        out_shape=[jax.ShapeDtypeStruct((T, 128), BF16), jax.ShapeDtypeStruct((8, 128), F32)],
        compiler_params=_cparams(("arbitrary",)))(proj, bias, a_log, ddt, dlam)


def _scan_chunk(d, pos, nc_ctx, nc):
    rev = jnp.where(pos < nc_ctx, nc_ctx - 1 - pos, nc - 1 - (pos - nc_ctx))
    return jnp.where(d == 0, pos, rev)


def _chunk_mask(d):
    ii = lax.broadcasted_iota(jnp.int32, (CHUNK, CHUNK), 0)
    jj = lax.broadcasted_iota(jnp.int32, (CHUNK, CHUNK), 1)
    return (ii - jj) * jnp.where(d == 0, 1, -1) >= 0


def _ssd_specs(T, G, n_ctx, gpb, chunk_of):
    R, P, N, Q = HPG, HEADDIM, STATE, CHUNK
    H = G * R
    nc, nc_ctx = T // Q, n_ctx // Q
    xw, bw = gpb * R * P, gpb * N
    b_blk0 = (H * P) // bw
    c_blk0 = b_blk0 + G // gpb

    def ch(d, s):
        return chunk_of(d, s, nc_ctx, nc)

    return dict(
        x=pl.BlockSpec((Q, xw), lambda d, g, s: (ch(d, s), g)),
        b=pl.BlockSpec((Q, bw), lambda d, g, s: (ch(d, s), b_blk0 + g)),
        c=pl.BlockSpec((Q, bw), lambda d, g, s: (ch(d, s), c_blk0 + g)),
        col=pl.BlockSpec((None, gpb, Q, R), lambda d, g, s: (d, g, ch(d, s), 0)),
        row=pl.BlockSpec((None, gpb, R, Q), lambda d, g, s: (d, g, 0, ch(d, s))),
        dsk=pl.BlockSpec((None, gpb, 8, 128), lambda d, g, s: (d, g, 0, 0)),
        xd=pl.BlockSpec((None, Q, xw), lambda d, g, s: (d, ch(d, s), g)),
        bd=pl.BlockSpec((None, Q, bw), lambda d, g, s: (d, ch(d, s), g)),
        st=pl.BlockSpec((None, None, gpb * R, P, N), lambda d, g, s: (d, ch(d, s), g, 0, 0)),
    )


def _ssd_fwd(xbc, dt4, lam4, lamT4, dsk, n_ctx, gpb, name):
    T = xbc.shape[0]
    G = dt4.shape[1]
    R, P, N, Q = HPG, HEADDIM, STATE, CHUNK
    H = G * R
    nc = T // Q
    sp = _ssd_specs(T, G, n_ctx, gpb, _scan_chunk)

    def body(x_ref, b_ref, c_ref, dt_ref, lam_ref, lamT_ref, dsk_ref, y_ref, st_ref, S):
        d, s = pl.program_id(0), pl.program_id(2)

        @pl.when(s == 0)
        def _():
            S[...] = jnp.zeros_like(S)

        mask = _chunk_mask(d)
        for gg in range(gpb):
            Bm = b_ref[:, gg * N:(gg + 1) * N].astype(BF16)
            Cm = c_ref[:, gg * N:(gg + 1) * N].astype(BF16)
            Gm = lax.dot_general(Cm, Bm, (((1,), (1,)), ((), ())), preferred_element_type=F32)
            for r in range(R):
                hr = gg * R + r
                lc = lam_ref[gg, :, r:r + 1]
                lr = lamT_ref[gg, r:r + 1, :]
                tot = jnp.where(d == 0, lc[Q - 1:Q], lc[0:1])
                ldec = jnp.exp(jnp.where(mask, lc - lr, NEG_BIG))
                xr = x_ref[:, hr * P:(hr + 1) * P]
                xc = xr * dt_ref[gg, :, r:r + 1]
                s_in = S[hr]
                y = jnp.dot((Gm * ldec).astype(BF16), xc.astype(BF16), preferred_element_type=F32)
                y = y + lax.dot_general(Cm, s_in.astype(BF16), (((1,), (1,)), ((), ())),
                                        preferred_element_type=F32) * jnp.exp(lc)
                y_ref[:, hr * P:(hr + 1) * P] = y + dsk_ref[gg, 0:1, r:r + 1] * xr
                st_ref[hr] = s_in
                xd = (xc * jnp.exp(tot - lc)).astype(BF16)
                S[hr] = jnp.exp(tot) * s_in + lax.dot_general(xd, Bm, (((0,), (0,)), ((), ())),
                                                              preferred_element_type=F32)

    return pl.pallas_call(
        body, name=name, grid=(2, G // gpb, nc),
        in_specs=[sp["x"], sp["b"], sp["c"], sp["col"], sp["col"], sp["row"], sp["dsk"]],
        out_specs=[sp["xd"], sp["st"]],
        out_shape=[jax.ShapeDtypeStruct((2, T, H * P), F32), jax.ShapeDtypeStruct((2, nc, H, P, N), F32)],
        scratch_shapes=[pltpu.VMEM((gpb * R, P, N), F32)],
        compiler_params=_cparams(("parallel", "parallel", "arbitrary")))(xbc, xbc, xbc, dt4, lam4, lamT4, dsk)


def _ssd_bwd(xbc, dy, states, dt4, lam4, lamT4, n_ctx, gpb, name):
    T = xbc.shape[0]
    G = dt4.shape[1]
    R, P, N, Q = HPG, HEADDIM, STATE, CHUNK
    H = G * R
    nc = T // Q
    sp = _ssd_specs(T, G, n_ctx, gpb, lambda d, s, nc_ctx, n: _scan_chunk(d, n - 1 - s, nc_ctx, n))

    def body(x_ref, b_ref, c_ref, dy_ref, st_ref, dt_ref, lam_ref, lamT_ref,
             dx_ref, db_ref, dc_ref, ddt_ref, dlc_ref, dlr_ref, dS):
        d, s = pl.program_id(0), pl.program_id(2)

        @pl.when(s == 0)
        def _():
            dS[...] = jnp.zeros_like(dS)

        mask = _chunk_mask(d)
        ri = lax.broadcasted_iota(jnp.int32, (Q, 1), 0)
        is_last = ri == jnp.where(d == 0, Q - 1, 0)
        for gg in range(gpb):
            Bm = b_ref[:, gg * N:(gg + 1) * N].astype(BF16)
            Cm = c_ref[:, gg * N:(gg + 1) * N].astype(BF16)
            Gm = lax.dot_general(Cm, Bm, (((1,), (1,)), ((), ())), preferred_element_type=F32)
            dG = jnp.zeros((Q, Q), F32)
            dB = jnp.zeros((Q, N), F32)
            dC = jnp.zeros((Q, N), F32)
            for r in range(R):
                hr = gg * R + r
                lc = lam_ref[gg, :, r:r + 1]
                lr = lamT_ref[gg, r:r + 1, :]
                tot = jnp.where(d == 0, lc[Q - 1:Q], lc[0:1])
                ldec = jnp.exp(jnp.where(mask, lc - lr, NEG_BIG))
                e_in = jnp.exp(lc)
                dte = jnp.exp(tot - lc)
                e_all = jnp.exp(tot)
                xr = x_ref[:, hr * P:(hr + 1) * P]
                dtr = dt_ref[gg, :, r:r + 1]
                xc = xr * dtr
                xc_b = xc.astype(BF16)
                dyr = dy_ref[:, hr * P:(hr + 1) * P]
                dy_b = dyr.astype(BF16)
                s_in = st_ref[hr]
                s_in_b = s_in.astype(BF16)
                ds_out = dS[hr]
                ds_out_b = ds_out.astype(BF16)
                Mm = Gm * ldec
                dM = lax.dot_general(dy_b, xc_b, (((1,), (1,)), ((), ())), preferred_element_type=F32)
                Wm = dM * Mm
                y_int = lax.dot_general(Cm, s_in_b, (((1,), (1,)), ((), ())), preferred_element_type=F32) * e_in
                b_ds = lax.dot_general(Bm, ds_out_b, (((1,), (1,)), ((), ())), preferred_element_type=F32)
                xj = jnp.sum(xc * b_ds, axis=1, keepdims=True)
                dlam_c = (jnp.sum(Wm, axis=1, keepdims=True) + jnp.sum(dyr * y_int, axis=1, keepdims=True)
                          - dte * xj)
                last = jnp.sum(jnp.sum(ds_out * s_in, axis=1, keepdims=True), axis=0, keepdims=True) * e_all \
                    + jnp.sum(dte * xj, axis=0, keepdims=True)
                dlc_ref[gg, :, r:r + 1] = dlam_c + jnp.where(is_last, last, 0.0)
                dlr_ref[gg, r:r + 1, :] = -jnp.sum(Wm, axis=0, keepdims=True)
                dxc = lax.dot_general(Mm.astype(BF16), dy_b, (((0,), (0,)), ((), ())),
                                      preferred_element_type=F32) + dte * b_ds
                dx_ref[:, hr * P:(hr + 1) * P] = dxc * dtr
                ddt_ref[gg, :, r:r + 1] = jnp.sum(dxc * xr, axis=1, keepdims=True)
                dG = dG + dM * ldec
                edy_b = (e_in * dyr).astype(BF16)
                dC = dC + jnp.dot(edy_b, s_in_b, preferred_element_type=F32)
                dB = dB + jnp.dot((dte * xc).astype(BF16), ds_out_b, preferred_element_type=F32)
                dS[hr] = e_all * ds_out + lax.dot_general(edy_b, Cm, (((0,), (0,)), ((), ())),
                                                          preferred_element_type=F32)
            dG_b = dG.astype(BF16)
            dc_ref[:, gg * N:(gg + 1) * N] = dC + jnp.dot(dG_b, Bm, preferred_element_type=F32)
            db_ref[:, gg * N:(gg + 1) * N] = dB + lax.dot_general(dG_b, Cm, (((0,), (0,)), ((), ())),
                                                                  preferred_element_type=F32)

    return pl.pallas_call(
        body, name=name, grid=(2, G // gpb, nc),
        in_specs=[sp["x"], sp["b"], sp["c"], sp["x"], sp["st"], sp["col"], sp["col"], sp["row"]],
        out_specs=[sp["xd"], sp["bd"], sp["bd"], sp["col"], sp["col"], sp["row"]],
        out_shape=[jax.ShapeDtypeStruct((2, T, H * P), F32), jax.ShapeDtypeStruct((2, T, G * N), F32),
                   jax.ShapeDtypeStruct((2, T, G * N), F32), jax.ShapeDtypeStruct((2, G, T, R), F32),
                   jax.ShapeDtypeStruct((2, G, T, R), F32), jax.ShapeDtypeStruct((2, G, R, T), F32)],
        scratch_shapes=[pltpu.VMEM((gpb * R, P, N), F32)],
        compiler_params=_cparams(("parallel", "parallel", "arbitrary")))(
            xbc, xbc, xbc, dy, states, dt4, lam4, lamT4)


def _ssd_combine(dxs2, db2, dc2, dy, xbc, dskv, tr, name):
    T, HP = dy.shape
    GN = db2.shape[2]
    ncol = HP + 2 * GN
    tc = _tile(GN, (512, 256, 128))
    nx, nb = HP // tc, GN // tc

    def body(dx_ref, db_ref, dc_ref, dy_ref, x_ref, k_ref, o_ref, dk_ref):
        i, j = pl.program_id(1), pl.program_id(0)

        @pl.when((i == 0) & (j < nx))
        def _():
            dk_ref[...] = jnp.zeros_like(dk_ref)

        @pl.when(j < nx)
        def _():
            dyv = dy_ref[...]
            o_ref[...] = dx_ref[0] + dx_ref[1] + (k_ref[0:1, :] + k_ref[1:2, :]) * dyv
            r = lax.broadcasted_iota(jnp.int32, (8, tc), 0)
            dk_ref[...] += jnp.where(r < 2, jnp.sum(dyv * x_ref[...], axis=0, keepdims=True), 0.0)

        @pl.when((j >= nx) & (j < nx + nb))
        def _():
            o_ref[...] = db_ref[0] + db_ref[1]

        @pl.when(j >= nx + nb)
        def _():
            o_ref[...] = dc_ref[0] + dc_ref[1]

    def cl(j, lo, n):
        return jnp.clip(j - lo, 0, n - 1)

    return pl.pallas_call(
        body, name=name, grid=(ncol // tc, T // tr),
        in_specs=[pl.BlockSpec((2, tr, tc), lambda j, i: (0, i, cl(j, 0, nx))),
                  pl.BlockSpec((2, tr, tc), lambda j, i: (0, i, cl(j, nx, nb))),
                  pl.BlockSpec((2, tr, tc), lambda j, i: (0, i, cl(j, nx + nb, nb))),
                  pl.BlockSpec((tr, tc), lambda j, i: (i, cl(j, 0, nx))),
                  pl.BlockSpec((tr, tc), lambda j, i: (i, cl(j, 0, nx))),
                  pl.BlockSpec((8, tc), lambda j, i: (0, cl(j, 0, nx)))],
        out_specs=[pl.BlockSpec((tr, tc), lambda j, i: (i, j)), pl.BlockSpec((8, tc), lambda j, i: (0, cl(j, 0, nx)))],
        out_shape=[jax.ShapeDtypeStruct((T, ncol), F32), jax.ShapeDtypeStruct((8, HP), F32)],
        compiler_params=_cparams(("arbitrary", "arbitrary")))(dxs2, db2, dc2, dy, xbc, dskv)


def _gnorm_fwd(y2, proj, w, gs, tr, name):
    T, HP = y2.shape[1], y2.shape[2]

    def body(y_ref, z_ref, w_ref, o_ref):
        yz = (y_ref[0] + y_ref[1]) * _silu(z_ref[...])
        for g in range(HP // gs):
            v = yz[:, g * gs:(g + 1) * gs]
            rstd = lax.rsqrt(jnp.mean(v * v, axis=-1, keepdims=True) + EPS)
            o_ref[:, g * gs:(g + 1) * gs] = (v * rstd * w_ref[:, g * gs:(g + 1) * gs]).astype(BF16)

    return pl.pallas_call(
        body, name=name, grid=(T // tr,),
        in_specs=[pl.BlockSpec((2, tr, HP), lambda i: (0, i, 0)), pl.BlockSpec((tr, HP), lambda i: (i, 0)),
                  pl.BlockSpec((1, HP), lambda i: (0, 0))],
        out_specs=pl.BlockSpec((tr, HP), lambda i: (i, 0)),
        out_shape=jax.ShapeDtypeStruct((T, HP), BF16),
        compiler_params=_cparams(("parallel",)))(y2, proj, w)


def _gnorm_bwd(y2, proj, w, dyn, gs, tr, name):
    T, HP = y2.shape[1], y2.shape[2]
    row = pl.BlockSpec((tr, HP), lambda i: (i, 0))

    def body(y_ref, z_ref, w_ref, d_ref, dy_ref, dz_ref, dw_ref):
        @pl.when(pl.program_id(0) == 0)
        def _():
            dw_ref[...] = jnp.zeros_like(dw_ref)

        yv = y_ref[0] + y_ref[1]
        zv = z_ref[...]
        sz = _silu(zv)
        yz = yv * sz
        dv = d_ref[...]
        r = lax.broadcasted_iota(jnp.int32, (8, gs), 0)
        for g in range(HP // gs):
            sl = slice(g * gs, (g + 1) * gs)
            v = yz[:, sl]
            rstd = lax.rsqrt(jnp.mean(v * v, axis=-1, keepdims=True) + EPS)
            xhat = v * rstd
            dyn_g = dv[:, sl]
            dhat = dyn_g * w_ref[:, sl]
            dyz = rstd * (dhat - xhat * jnp.mean(dhat * xhat, axis=-1, keepdims=True))
            dy_ref[:, sl] = dyz * sz[:, sl]
            dz_ref[:, sl] = (dyz * yv[:, sl] * _dsilu(zv[:, sl])).astype(BF16)
            dw_ref[:, sl] += jnp.where(r == 0, jnp.sum(dyn_g * xhat, axis=0, keepdims=True), 0.0)

    return pl.pallas_call(
        body, name=name, grid=(T // tr,),
        in_specs=[pl.BlockSpec((2, tr, HP), lambda i: (0, i, 0)), row, pl.BlockSpec((1, HP), lambda i: (0, 0)), row],
        out_specs=[row, row, pl.BlockSpec((8, HP), lambda i: (0, 0))],
        out_shape=[jax.ShapeDtypeStruct((T, HP), F32), jax.ShapeDtypeStruct((T, HP), BF16),
                   jax.ShapeDtypeStruct((8, HP), F32)],
        compiler_params=_cparams(("arbitrary",)))(y2, proj, w, dyn)


def _merge_fwd(proj, g1_blk, o_ssd, o_pool, tr, name):
    T, D = o_ssd.shape
    row = pl.BlockSpec((tr, D), lambda i: (i, 0))

    def body(g1_ref, g2_ref, a_ref, b_ref, o_ref):
        o_ref[...] = (jax.nn.sigmoid(g1_ref[...]) * a_ref[...]
                      + jax.nn.sigmoid(g2_ref[...]) * b_ref[...]).astype(BF16)

    return pl.pallas_call(
        body, name=name, grid=(T // tr,),
        in_specs=[pl.BlockSpec((tr, D), lambda i: (i, g1_blk)), pl.BlockSpec((tr, D), lambda i: (i, g1_blk + 1)),
                  row, row],
        out_specs=row, out_shape=jax.ShapeDtypeStruct((T, D), BF16),
        compiler_params=_cparams(("parallel",)))(proj, proj, o_ssd, o_pool)


def _merge_bwd(proj, g1_blk, o_ssd, o_pool, dmg, tr, name):
    T, D = o_ssd.shape
    row = pl.BlockSpec((tr, D), lambda i: (i, 0))

    def body(g1_ref, g2_ref, a_ref, b_ref, d_ref, da_ref, db_ref, dg_ref):
        s1, s2 = jax.nn.sigmoid(g1_ref[...]), jax.nn.sigmoid(g2_ref[...])
        dv = d_ref[...]
        da_ref[...] = (s1 * dv).astype(BF16)
        db_ref[...] = (s2 * dv).astype(BF16)
        dg_ref[:, :D] = (dv * a_ref[...] * s1 * (1.0 - s1)).astype(BF16)
        dg_ref[:, D:] = (dv * b_ref[...] * s2 * (1.0 - s2)).astype(BF16)

    return pl.pallas_call(
        body, name=name, grid=(T // tr,),
        in_specs=[pl.BlockSpec((tr, D), lambda i: (i, g1_blk)), pl.BlockSpec((tr, D), lambda i: (i, g1_blk + 1)),
                  row, row, row],
        out_specs=[row, row, pl.BlockSpec((tr, 2 * D), lambda i: (i, 0))],
        out_shape=[jax.ShapeDtypeStruct((T, D), BF16), jax.ShapeDtypeStruct((T, D), BF16),
                   jax.ShapeDtypeStruct((T, 2 * D), BF16)],
        compiler_params=_cparams(("parallel",)))(proj, proj, o_ssd, o_pool, dmg)


def _swiglu_fwd(gu, tr, name):
    T, F2 = gu.shape
    F = F2 // 2
    tc = _tile(F, (1408, 768, 512, 256, 128))
    nb = F // tc

    def body(a_ref, b_ref, o_ref):
        o_ref[...] = (_silu(a_ref[...]) * b_ref[...]).astype(BF16)

    return pl.pallas_call(
        body, name=name, grid=(T // tr, nb),
        in_specs=[pl.BlockSpec((tr, tc), lambda i, j: (i, j)), pl.BlockSpec((tr, tc), lambda i, j: (i, nb + j))],
        out_specs=pl.BlockSpec((tr, tc), lambda i, j: (i, j)),
        out_shape=jax.ShapeDtypeStruct((T, F), BF16),
        compiler_params=_cparams(("parallel", "parallel")))(gu, gu)


def _swiglu_bwd(gu, dact, tr, name):
    T, F2 = gu.shape
    F = F2 // 2
    tc = _tile(F, (1408, 768, 512, 256, 128))
    nb = F // tc

    def body(a_ref, b_ref, d_ref, o_ref):
        is_a = pl.program_id(1) < nb
        av, bv, dv = a_ref[...], b_ref[...], d_ref[...]
        o_ref[...] = jnp.where(is_a, dv * bv * _dsilu(av), dv * _silu(av)).astype(BF16)

    return pl.pallas_call(
        body, name=name, grid=(T // tr, 2 * nb),
        in_specs=[pl.BlockSpec((tr, tc), lambda i, j: (i, j % nb)), pl.BlockSpec((tr, tc), lambda i, j: (i, nb + j % nb)),
                  pl.BlockSpec((tr, tc), lambda i, j: (i, j % nb))],
        out_specs=pl.BlockSpec((tr, tc), lambda i, j: (i, j)),
        out_shape=jax.ShapeDtypeStruct((T, F2), BF16),
        compiler_params=_cparams(("parallel", "parallel")))(gu, gu, dact)


def _adamw(w, g, m, v, name):
    Rr, C = w.shape
    tr = _tile(Rr, (256, 128, 64, 32, 16, 8))
    row = pl.BlockSpec((tr, C), lambda i: (i, 0))

    def body(w_ref, g_ref, m_ref, v_ref, d_ref, mo_ref, vo_ref):
        gv = g_ref[...]
        mn = ADAM_B1 * m_ref[...] + (1.0 - ADAM_B1) * gv
        vn = ADAM_B2 * v_ref[...] + (1.0 - ADAM_B2) * (gv * gv)
        m_hat = mn / (1.0 - ADAM_B1 ** ADAM_STEP)
        v_hat = vn / (1.0 - ADAM_B2 ** ADAM_STEP)
        d_ref[...] = -ADAM_LR * (m_hat / (jnp.sqrt(v_hat) + ADAM_EPS) + ADAM_WD * w_ref[...])
        mo_ref[...] = mn
        vo_ref[...] = vn

    sds = jax.ShapeDtypeStruct((Rr, C), F32)
    return pl.pallas_call(body, name=name, grid=(Rr // tr,), in_specs=[row] * 4, out_specs=[row] * 3,
                          out_shape=[sds] * 3, compiler_params=_cparams(("parallel",)))(w, g, m, v)


def _sum_slots(x, name):
    n, Rr, C = x.shape
    tr = _tile(Rr, (512, 256, 128, 64, 32, 16, 8))

    def body(x_ref, o_ref):
        acc = x_ref[0].astype(F32)
        for k in range(1, n):
            acc = acc + x_ref[k].astype(F32)
        o_ref[...] = acc

    return pl.pallas_call(body, name=name, grid=(Rr // tr,),
                          in_specs=[pl.BlockSpec((n, tr, C), lambda i: (0, i, 0))],
                          out_specs=pl.BlockSpec((tr, C), lambda i: (i, 0)),
                          out_shape=jax.ShapeDtypeStruct((Rr, C), F32),
                          compiler_params=_cparams(("parallel",)))(x)


def _place():
    return lax.axis_index("x"), lax.axis_index("y"), lax.axis_index("c")


def _all_gather(x, name):
    Rr, C = x.shape

    def body(x_ref, out_ref, send_sems, recv_sems, local_sem):
        mx, my, mc = _place()
        me, sibling = (mx, my, mc), (mx, my, 1 - mc)
        chips = [(1 - mx, my), (mx, 1 - my), (1 - mx, 1 - my)]

        def slot(px, py, pc):
            return out_ref.at[4 * px + 2 * py + pc]

        def copy(k, block, to, src=None):
            return pltpu.make_async_remote_copy(
                src_ref=slot(*block) if src is None else src, dst_ref=slot(*block),
                send_sem=send_sems.at[k], recv_sem=recv_sems.at[k],
                device_id=to, device_id_type=pl.DeviceIdType.MESH)

        mine = pltpu.make_async_copy(x_ref, slot(*me), local_sem)
        mine.start()
        first = [copy(0, me, sibling, src=x_ref)]
        first += [copy(1 + j, me, (*chip, mc), src=x_ref) for j, chip in enumerate(chips)]
        for cp in first:
            cp.start()
        passed = [copy(4 + j, (*chip, mc), sibling) for j, chip in enumerate(chips)]
        for j, chip in enumerate(chips):
            copy(1 + j, (*chip, mc), me).wait_recv()
            passed[j].start()
        copy(0, sibling, me).wait_recv()
        for j, chip in enumerate(chips):
            copy(4 + j, (*chip, 1 - mc), me).wait_recv()
        for cp in first + passed:
            cp.wait_send()
        mine.wait()

    return pl.pallas_call(
        body, name=name, in_specs=[ANY], out_specs=ANY,
        out_shape=jax.ShapeDtypeStruct((N_DEV, Rr, C), x.dtype),
        scratch_shapes=[pltpu.SemaphoreType.DMA((7,)), pltpu.SemaphoreType.DMA((7,)), pltpu.SemaphoreType.DMA],
    )(x)


def _pair_exchange(buf, name):
    _, n, Rr, C = buf.shape
    parts = 4
    pr = Rr // parts

    def body(b_ref, got_ref, send_sems, recv_sems):
        mx, my, mc = _place()
        copies = []
        for q in range(n):
            for p in range(parts):
                rows = pl.ds(p * pr, pr)
                cp = pltpu.make_async_remote_copy(
                    src_ref=b_ref.at[1 - mc, q, rows], dst_ref=got_ref.at[q, rows],
                    send_sem=send_sems.at[q * parts + p], recv_sem=recv_sems.at[q * parts + p],
                    device_id=(mx, my, 1 - mc), device_id_type=pl.DeviceIdType.MESH)
                cp.start()
                copies.append(cp)
        for cp in copies:
            cp.wait()

    return pl.pallas_call(
        body, name=name, in_specs=[ANY], out_specs=ANY,
        out_shape=jax.ShapeDtypeStruct((n, Rr, C), buf.dtype),
        scratch_shapes=[pltpu.SemaphoreType.DMA((n * parts,)), pltpu.SemaphoreType.DMA((n * parts,))],
    )(buf)


def _pair_add(buf, got, my_c, name):
    _, n, Rr, C = buf.shape
    tr = _tile(Rr, (512, 256, 128, 64, 32, 16))

    def body(c_ref, b_ref, g_ref, o_ref):
        o_ref[...] = (b_ref[...] + g_ref[...]).astype(BF16)

    return pl.pallas_call(
        body, name=name,
        grid_spec=pltpu.PrefetchScalarGridSpec(
            num_scalar_prefetch=1, grid=(n, Rr // tr),
            in_specs=[pl.BlockSpec((None, None, tr, C), lambda q, i, c: (c[0], q, i, 0)),
                      pl.BlockSpec((None, tr, C), lambda q, i, c: (q, i, 0))],
            out_specs=pl.BlockSpec((None, tr, C), lambda q, i, c: (q, i, 0))),
        out_shape=jax.ShapeDtypeStruct((n, Rr, C), BF16),
        compiler_params=_cparams(("parallel", "parallel")))(my_c, buf, got)


def _chip_exchange(red, name):
    n, Rr, C = red.shape

    def body(r_ref, out_ref, send_sems, recv_sems, local_sem):
        mx, my, mc = _place()
        chips = [(1 - mx, my), (mx, 1 - my), (1 - mx, 1 - my)]
        mine = pltpu.make_async_copy(r_ref.at[2 * mx + my], out_ref.at[2 * mx + my], local_sem)
        mine.start()
        sends = []
        for k, (px, py) in enumerate(chips):
            cp = pltpu.make_async_remote_copy(
                src_ref=r_ref.at[2 * px + py], dst_ref=out_ref.at[2 * mx + my],
                send_sem=send_sems.at[k], recv_sem=recv_sems.at[k],
                device_id=(px, py, mc), device_id_type=pl.DeviceIdType.MESH)
            cp.start()
            sends.append(cp)
        for k, (px, py) in enumerate(chips):
            pltpu.make_async_remote_copy(
                src_ref=r_ref.at[2 * px + py], dst_ref=out_ref.at[2 * px + py],
                send_sem=send_sems.at[k], recv_sem=recv_sems.at[k],
                device_id=(px, py, mc), device_id_type=pl.DeviceIdType.MESH).wait_recv()
        for cp in sends:
            cp.wait_send()
        mine.wait()

    return pl.pallas_call(
        body, name=name, in_specs=[ANY], out_specs=ANY,
        out_shape=jax.ShapeDtypeStruct((n, Rr, C), red.dtype),
        scratch_shapes=[pltpu.SemaphoreType.DMA((3,)), pltpu.SemaphoreType.DMA((3,)), pltpu.SemaphoreType.DMA],
    )(red)


def _pad_rows(a, rows):
    return jnp.pad(a, ((0, rows - a.shape[0]), (0, 0)))


class _Layout:
    def __init__(self, D, shards):
        self.D = D
        self.pieces = []
        off = 0
        for name, layer, rows in shards:
            pr = _round_up(rows, 16)
            self.pieces.append((name, layer, rows, pr, off))
            off += pr
        self.rows = _round_up(off, 64)

    def find(self, name, layer):
        for p in self.pieces:
            if p[0] == name and p[1] == layer:
                return p
        raise KeyError(name)


def kernel(x, c, ctx, c_ctx, w_ada, b_ada, g_mix, w_in, conv_w, conv_b, dt_bias, a_log, d_skip, ssd_norm_w, w_ssd_out, pool_w, pool_scale, w_pool_out, w_out, g_ffn, w_gate_up, w_down, g_final, loss_target, m_c_ctx, m_w_ada, m_b_ada, m_g_mix, m_w_in, m_conv_w, m_conv_b, m_dt_bias, m_a_log, m_d_skip, m_ssd_norm_w, m_w_ssd_out, m_pool_w, m_pool_scale, m_w_pool_out, m_w_out, m_g_ffn, m_w_gate_up, m_w_down, m_g_final, v_c_ctx, v_w_ada, v_b_ada, v_g_mix, v_w_in, v_conv_w, v_conv_b, v_dt_bias, v_a_log, v_d_skip, v_ssd_norm_w, v_w_ssd_out, v_pool_w, v_pool_scale, v_w_pool_out, v_w_out, v_g_ffn, v_w_gate_up, v_w_down, v_g_final):
    weights = dict(c_ctx=c_ctx, w_ada=w_ada, b_ada=b_ada, g_mix=g_mix, w_in=w_in, conv_w=conv_w, conv_b=conv_b,
                   dt_bias=dt_bias, a_log=a_log, d_skip=d_skip, ssd_norm_w=ssd_norm_w, w_ssd_out=w_ssd_out,
                   pool_w=pool_w, pool_scale=pool_scale, w_pool_out=w_pool_out, w_out=w_out, g_ffn=g_ffn,
                   w_gate_up=w_gate_up, w_down=w_down, g_final=g_final)
    moms_m = dict(c_ctx=m_c_ctx, w_ada=m_w_ada, b_ada=m_b_ada, g_mix=m_g_mix, w_in=m_w_in, conv_w=m_conv_w,
                  conv_b=m_conv_b, dt_bias=m_dt_bias, a_log=m_a_log, d_skip=m_d_skip, ssd_norm_w=m_ssd_norm_w,
                  w_ssd_out=m_w_ssd_out, pool_w=m_pool_w, pool_scale=m_pool_scale, w_pool_out=m_w_pool_out,
                  w_out=m_w_out, g_ffn=m_g_ffn, w_gate_up=m_w_gate_up, w_down=m_w_down, g_final=m_g_final)
    moms_v = dict(c_ctx=v_c_ctx, w_ada=v_w_ada, b_ada=v_b_ada, g_mix=v_g_mix, w_in=v_w_in, conv_w=v_conv_w,
                  conv_b=v_conv_b, dt_bias=v_dt_bias, a_log=v_a_log, d_skip=v_d_skip, ssd_norm_w=v_ssd_norm_w,
                  w_ssd_out=v_w_ssd_out, pool_w=v_pool_w, pool_scale=v_pool_scale, w_pool_out=v_w_pool_out,
                  w_out=v_w_out, g_ffn=v_g_ffn, w_gate_up=v_w_gate_up, w_down=v_w_down, g_final=v_g_final)
    order = ["c_ctx", "w_ada", "b_ada", "g_mix", "w_in", "conv_w", "conv_b", "dt_bias", "a_log", "d_skip",
             "ssd_norm_w", "w_ssd_out", "pool_w", "pool_scale", "w_pool_out", "w_out", "g_ffn", "w_gate_up",
             "w_down", "g_final"]
    big = ["w_ada", "w_in", "conv_w", "w_ssd_out", "pool_w", "w_pool_out", "w_out", "w_gate_up", "w_down"]
    small = [n for n in order if n not in big]

    depth = w_in.shape[0]
    L, D = x.shape[1], x.shape[2]
    n_ctx = ctx.shape[1]
    T = n_ctx + L
    in_cols = w_in.shape[2] * N_DEV
    xbc_w = conv_w.shape[2] * N_DEV
    dinner = ssd_norm_w.shape[1]
    H = dt_bias.shape[2]
    G = H // HPG
    GN = G * STATE
    assert xbc_w == dinner + 2 * GN and dinner == H * HEADDIM
    assert in_cols == dinner + xbc_w + 2 * H + D + 2 * D
    assert dinner == 2 * D and GN == D and 2 * H <= 128
    F = w_down.shape[1] * N_DEV
    pg = pool_w.shape[3]
    tr = n_ctx
    assert L % tr == 0 and tr % GRID_W == 0 and tr % CHUNK == 0 and L % CHUNK == 0
    n_ctx_tiles = 1
    NP = _round_up(9 * D + 128, 512)
    gs = dinner // G
    off_xbc, off_dt, off_pool = dinner, dinner + xbc_w, dinner + xbc_w + 2 * H
    off_gate = off_pool + D

    def shard_rows(name, l):
        w = weights[name][l]
        if name in ("w_ada", "w_in", "w_gate_up"):
            return w.T
        if name == "conv_w":
            w8 = _pad_rows(w, 8)
            hi = w8.astype(BF16)
            lo = (w8 - hi.astype(F32)).astype(BF16)
            return jnp.concatenate([hi, lo], axis=0).reshape(-1, D)
        if name == "pool_w":
            return w.reshape(-1, D)
        return w

    shards = [(n, l, shard_rows(n, l)) for l in range(depth) for n in big]
    lay = _Layout(D, [(n, l, a.shape[0]) for n, l, a in shards])
    packed = jnp.concatenate([_pad_rows(a.astype(BF16), lay.find(n, l)[3]) for n, l, a in shards], axis=0)
    gathered = _all_gather(_pad_rows(packed, lay.rows), "gather_weights")

    def full(name, l):
        _, _, rows, _, off = lay.find(name, l)
        return gathered[:, off:off + rows, :]

    def w_inT_new(l):
        w = full("w_in", l).reshape(in_cols, D)
        parts = [w[:off_xbc], w[off_xbc:off_dt], w[off_pool:off_gate], w[off_gate:], w[off_dt:off_pool]]
        return _pad_rows(jnp.concatenate(parts, axis=0), NP)

    xs0 = jnp.concatenate([ctx[0], x[0]], axis=0)
    cc8 = _pad_rows(jnp.concatenate([c, c_ctx[None, :]], axis=0), 8)
    tgt = loss_target[0]

    def vec(a):
        return a.reshape(1, -1)

    def pad128(a):
        return jnp.pad(a.reshape(1, -1), ((0, 0), (0, 128 - 2 * H)))

    def to4(arr):
        return arr[:, :2 * H].reshape(T, 2, G, HPG).transpose(1, 2, 0, 3)

    def from4(arr):
        return jnp.pad(arr.transpose(2, 0, 1, 3).reshape(T, 2 * H), ((0, 0), (0, 128 - 2 * H)))

    dt_blk = (9 * D) // 128
    conv_tc = 128
    ssd_gpb = 2
    saved = []
    xcur = xs0
    for l in range(depth):
        W = dict(
            adaT=full("w_ada", l).reshape(6 * D, D), inT=w_inT_new(l),
            ssd=full("w_ssd_out", l).reshape(dinner, D), po=full("w_pool_out", l).reshape(D, D),
            out=full("w_out", l).reshape(D, D), guT=full("w_gate_up", l).reshape(2 * F, D),
            down=full("w_down", l).reshape(F, D),
            pool=full("pool_w", l).reshape(N_DEV, len(POOL_WINDOWS), pg // N_DEV, pg).transpose(1, 0, 2, 3)
            .reshape(len(POOL_WINDOWS), pg, pg),
        )
        cw = full("conv_w", l).reshape(N_DEV, 16, xbc_w // N_DEV).astype(F32)
        W["conv8"] = (cw[:, :8] + cw[:, 8:]).transpose(1, 0, 2).reshape(8, xbc_w)
        m6 = _ada_fwd(cc8, W["adaT"], vec(b_ada[l]), "ada_fwd")
        h = _norm_mod(xcur, vec(g_mix[l]), m6, 0, 1, n_ctx_tiles, tr, "norm_mod")
        proj = _mm(h, W["inT"], "nt", F32, "mm_in")
        dt, lam = _dt_prep(proj, dt_blk, pad128(dt_bias[l]), pad128(a_log[l]), H, tr, "dt_prep")
        dt4, a4 = to4(dt), to4(lam)
        aT4 = a4.transpose(0, 1, 3, 2)
        dsk = jnp.zeros((2, G, 8, 128), F32).at[:, :, 0, :HPG].set(d_skip[l].reshape(2, G, HPG))
        xbc = _conv_fwd(proj, dinner // conv_tc, xbc_w, W["conv8"], vec(conv_b[l]), n_ctx, conv_tc, "conv_fwd")
        y2, states = _ssd_fwd(xbc, dt4, a4, aT4, dsk, n_ctx, ssd_gpb, "ssd_fwd")
        yn = _gnorm_fwd(y2, proj, vec(ssd_norm_w[l]), gs, tr, "gnorm_fwd")
        pm = _pool_apply(proj, (6 * D) // pg, BF16, n_ctx, tr, pg, False, "pool_fwd")
        pms = _pool_mix_fwd(pm, W["pool"], vec(pool_scale[l]), tr, "pool_mix_fwd")
        o_ssd = _mm(yn, W["ssd"], "nn", F32, "mm_ssd_out")
        o_pool = _mm(pms, W["po"], "nn", F32, "mm_pool_out")
        mg = _merge_fwd(proj, 7, o_ssd, o_pool, tr, "merge_fwd")
        mo = _mm(mg, W["out"], "nn", F32, "mm_out")
        x1, h2 = _norm_mod(xcur, vec(g_ffn[l]), m6, 3, 4, n_ctx_tiles, tr, "resid_norm_mod", resid=(mo, 2))
        gu = _mm(h2, W["guT"], "nt", F32, "mm_gate_up")
        act = _swiglu_fwd(gu, tr, "swiglu_fwd")
        f = _mm(act, W["down"], "nn", F32, "mm_down")
        saved.append(dict(W=W, m6=m6, x0=xcur, h=h, proj=proj, dt4=dt4, a4=a4, aT4=aT4, xbc=xbc, y2=y2,
                          states=states, yn=yn, pm=pm, pms=pms, o_ssd=o_ssd, o_pool=o_pool, mg=mg, mo=mo, x1=x1,
                          h2=h2, gu=gu, act=act, f=f))
        xcur = _resid(x1, f, m6, 5, n_ctx_tiles, tr, "resid")

    loss_blk, dx, dgf = _loss_head(xcur, tgt, vec(g_final), n_ctx_tiles, tr, "loss_head")
    loss = lax.psum(loss_blk[0, 0], MESH_AXES)

    big_rows = {}
    small_g = {n: [None] * depth for n in small}
    d_c_ctx = jnp.zeros((D,), F32)
    for l in reversed(range(depth)):
        S = saved[l]
        W, m6, proj = S["W"], S["m6"], S["proj"]
        df, dga2 = _resid_bwd(dx, S["f"], m6, 5, n_ctx_tiles, tr, "resid_bwd")
        dact = _mm(df, W["down"], "nt", F32, "mm_down_dx")
        g_down = _mm(S["act"], df, "tn", F32, "mm_down_dw")
        dgu = _swiglu_bwd(S["gu"], dact, tr, "swiglu_bwd")
        dh2 = _mm(dgu, W["guT"], "nn", F32, "mm_gate_up_dx")
        g_guT = _mm(dgu, S["h2"], "tn", F32, "mm_gate_up_dw")
        dx1, st2 = _norm_mod_bwd(S["x1"], dh2, dx, vec(g_ffn[l]), m6, 4, n_ctx_tiles, tr, "norm_mod_bwd")
        dmo, dga1 = _resid_bwd(dx1, S["mo"], m6, 2, n_ctx_tiles, tr, "resid_bwd")
        dmg = _mm(dmo, W["out"], "nt", F32, "mm_out_dx")
        g_out = _mm(S["mg"], dmo, "tn", F32, "mm_out_dw")
        do_ssd, do_pool, dgl = _merge_bwd(proj, 7, S["o_ssd"], S["o_pool"], dmg, tr, "merge_bwd")
        dyn = _mm(do_ssd, W["ssd"], "nt", F32, "mm_ssd_out_dx")
        g_ssd = _mm(S["yn"], do_ssd, "tn", F32, "mm_ssd_out_dw")
        dpms = _mm(do_pool, W["po"], "nt", F32, "mm_pool_out_dx")
        g_po = _mm(S["pms"], do_pool, "tn", F32, "mm_pool_out_dw")
        dpm, g_pool, dps = _pool_mix_bwd(S["pm"], dpms, W["pool"], vec(pool_scale[l]), tr, "pool_mix_bwd")
        dup = _pool_apply(dpm, 0, BF16, n_ctx, tr, pg, True, "pool_bwd")
        dy, dz, dnw = _gnorm_bwd(S["y2"], proj, vec(ssd_norm_w[l]), dyn, gs, tr, "gnorm_bwd")
        dxs2, db2, dc2, ddt4, da4, daT4 = _ssd_bwd(S["xbc"], dy, S["states"], S["dt4"], S["a4"], S["aT4"], n_ctx,
                                                   ssd_gpb, "ssd_bwd")
        dskv = _pad_rows(jnp.repeat(d_skip[l], HEADDIM, axis=1), 8)
        dxbc_act, ddsk = _ssd_combine(dxs2, db2, dc2, dy, S["xbc"], dskv, tr, "ssd_combine")
        dxbc, dconv = _conv_bwd(proj, dinner // conv_tc, dxbc_act, W["conv8"], vec(conv_b[l]), n_ctx, conv_tc,
                                "conv_bwd")
        ddt_raw, dtst = _dt_bwd(proj, dt_blk, pad128(dt_bias[l]), pad128(a_log[l]), from4(ddt4),
                                from4(da4 + daT4.transpose(0, 1, 3, 2)), H, tr, "dt_bwd")
        dproj = jnp.concatenate([dz, dxbc, dup, dgl, ddt_raw, jnp.zeros((T, NP - 9 * D - 128), BF16)], axis=1)
        dh = _mm(dproj, W["inT"], "nn", F32, "mm_in_dx")
        g_inT_new = _mm(dproj, S["h"], "tn", F32, "mm_in_dw")
        dx0, st1 = _norm_mod_bwd(S["x0"], dh, dx1, vec(g_mix[l]), m6, 1, n_ctx_tiles, tr, "norm_mod_bwd")
        dm6 = _pad_rows(jnp.concatenate([st1[0:2], st1[2:4], dga1[0:2], st2[0:2], st2[2:4], dga2[0:2]], axis=1), 8)
        dsil = _mm(dm6, W["adaT"], "nn", F32, "mm_ada_dx")
        sil_b, dcc, dbada = _ada_bwd_small(cc8, dsil, dm6, "ada_bwd_small")
        g_adaT = _mm(dm6, sil_b, "tn", F32, "mm_ada_dw")
        d_c_ctx = d_c_ctx + dcc[1]
        dx = dx0

        g_inT = jnp.concatenate([g_inT_new[:6 * D], g_inT_new[9 * D:9 * D + 2 * H], g_inT_new[6 * D:9 * D]], axis=0)
        big_rows[("w_ada", l)] = g_adaT.reshape(N_DEV, -1, D)
        big_rows[("w_in", l)] = g_inT.reshape(N_DEV, -1, D)
        big_rows[("conv_w", l)] = jnp.pad(
            dconv[:CONV_K].reshape(CONV_K, N_DEV, xbc_w // N_DEV).transpose(1, 0, 2),
            ((0, 0), (0, 16 - CONV_K), (0, 0))).reshape(N_DEV, -1, D)
        big_rows[("w_ssd_out", l)] = g_ssd.reshape(N_DEV, -1, D)
        big_rows[("pool_w", l)] = g_pool.reshape(len(POOL_WINDOWS), N_DEV, pg // N_DEV, pg).transpose(1, 0, 2, 3) \
            .reshape(N_DEV, -1, D)
        big_rows[("w_pool_out", l)] = g_po.reshape(N_DEV, -1, D)
        big_rows[("w_out", l)] = g_out.reshape(N_DEV, -1, D)
        big_rows[("w_gate_up", l)] = g_guT.reshape(N_DEV, -1, D)
        big_rows[("w_down", l)] = g_down.reshape(N_DEV, -1, D)
        small_g["b_ada"][l] = dbada[0]
        small_g["g_mix"][l] = st1[4]
        small_g["conv_b"][l] = dconv[CONV_K]
        small_g["dt_bias"][l] = dtst[0, :2 * H].reshape(2, H)
        small_g["a_log"][l] = dtst[1, :2 * H].reshape(2, H)
        dsk_h = ddsk[0].reshape(H, HEADDIM).sum(axis=-1)
        small_g["d_skip"][l] = jnp.stack([dsk_h, dsk_h])
        small_g["ssd_norm_w"][l] = dnw[0]
        small_g["pool_scale"][l] = dps[0]
        small_g["g_ffn"][l] = st2[4]
    grad_x = dx[n_ctx:][None]

    gparts = [jnp.pad(big_rows[(n, l)], ((0, 0), (0, pr - rows), (0, 0))) for n, l, rows, pr, _ in lay.pieces]
    gparts.append(jnp.zeros((N_DEV, lay.rows - sum(p[3] for p in lay.pieces), D), F32))
    gbuf = jnp.concatenate(gparts, axis=1)
    gbuf = gbuf.reshape(2, 2, 2, lay.rows, D).transpose(2, 0, 1, 3, 4).reshape(2, 4, lay.rows, D)
    got = _pair_exchange(gbuf, "rs_pair_exchange")
    red = _pair_add(gbuf, got, lax.axis_index("c").astype(jnp.int32).reshape(1), "rs_pair_add")
    slots = _chip_exchange(red, "rs_chip_exchange")
    g_local = _sum_slots(slots, "rs_chip_add")

    def local_grad(name):
        outs = []
        for l in range(depth):
            _, _, rows, _, off = lay.find(name, l)
            piece = g_local[off:off + rows]
            if name in ("w_ada", "w_in", "w_gate_up"):
                piece = piece.T
            elif name == "conv_w":
                piece = piece.reshape(16, -1)[:CONV_K]
            elif name == "pool_w":
                piece = piece.reshape(weights[name].shape[1:])
            outs.append(piece)
        return jnp.stack(outs)

    grads = {n: local_grad(n) for n in big}

    small_full = {"c_ctx": d_c_ctx, "g_final": dgf[0]}
    for n in small:
        if n not in small_full:
            small_full[n] = jnp.stack(small_g[n])

    def pack_small(tree):
        flat = jnp.concatenate([tree[n].reshape(-1).astype(F32) for n in small])
        rows = _round_up(-(-flat.shape[0] // D), 8)
        return jnp.pad(flat, (0, rows * D - flat.shape[0])).reshape(rows, D)

    def unpack_small(buf):
        flat, out, off = buf.reshape(-1), {}, 0
        for n in small:
            sz = weights[n].size
            out[n] = flat[off:off + sz].reshape(weights[n].shape)
            off += sz
        return out

    g_small = _sum_slots(_all_gather(pack_small(small_full), "gather_small_grads"), "sum_small_grads")
    grads.update(unpack_small(g_small))

    delta, new_m, new_v = {}, {}, {}
    for n in big:
        shp = weights[n].shape
        d_, m_, v_ = _adamw(weights[n].reshape(-1, shp[-1]), grads[n].reshape(-1, shp[-1]),
                            moms_m[n].reshape(-1, shp[-1]), moms_v[n].reshape(-1, shp[-1]), "adamw_" + n)
        delta[n], new_m[n], new_v[n] = d_.reshape(shp), m_.reshape(shp), v_.reshape(shp)
    d_, m_, v_ = _adamw(pack_small(weights), g_small, pack_small(moms_m), pack_small(moms_v), "adamw_small")
    delta.update(unpack_small(d_))
    new_m.update(unpack_small(m_))
    new_v.update(unpack_small(v_))

    return (loss, grad_x, *[grads[n] for n in order], *[delta[n] for n in order],
            *[new_m[n] for n in order], *[new_v[n] for n in order])
```

```python
import functools

import jax
import jax.numpy as jnp
from jax import lax
from jax.experimental import pallas as pl
from jax.experimental.pallas import tpu as pltpu

F32 = jnp.float32
BF16 = jnp.bfloat16
N_DEV = 8
EPS = 1e-6
GRID_W = 64
POOL_WINDOWS = (2, 4, 8, 16)
HEADDIM = 64
STATE = 128
CHUNK = 128
HPG = 4
CONV_K = 5
ADAM_LR, ADAM_B1, ADAM_B2, ADAM_EPS, ADAM_WD, ADAM_STEP = 0.001, 0.9, 0.999, 1e-08, 0.01, 10
NEG_BIG = -1e30
MESH_AXES = ("x", "y", "c")
ANY = pl.BlockSpec(memory_space=pl.ANY)


def _tile(n, cands):
    for t in cands:
        if n % t == 0:
            return t
    return n


def _round_up(n, m):
    return -(-n // m) * m


def _silu(x):
    return x * jax.nn.sigmoid(x)


def _dsilu(x):
    s = jax.nn.sigmoid(x)
    return s * (1.0 + x * (1.0 - s))


def _cparams(sem):
    return pltpu.CompilerParams(dimension_semantics=sem, vmem_limit_bytes=56 * 1024 * 1024)


def _mm(a, b, mode, out_dtype, name):
    if mode == "tn":
        K, M = a.shape
        N = b.shape[1]
        tm = _tile(M, (512, 256, 128))
        tn = _tile(N, (1024, 512, 256, 128))
        tk = _tile(K, (1088, 544, 512, 256, 128))
        a_spec = pl.BlockSpec((tk, tm), lambda i, j, k: (k, i))
        b_spec = pl.BlockSpec((tk, tn), lambda i, j, k: (k, j))
        dims = (((0,), (0,)), ((), ()))
    else:
        M, K = a.shape
        N = b.shape[0] if mode == "nt" else b.shape[1]
        tm = _tile(M, (1088, 544, 512, 256, 128))
        tn = _tile(N, (512, 256, 128))
        tk = K if K <= 2048 else _tile(K, (1024, 512, 256, 128))
        a_spec = pl.BlockSpec((tm, tk), lambda i, j, k: (i, k))
        if mode == "nt":
            b_spec = pl.BlockSpec((tn, tk), lambda i, j, k: (j, k))
            dims = (((1,), (1,)), ((), ()))
        else:
            b_spec = pl.BlockSpec((tk, tn), lambda i, j, k: (k, j))
            dims = (((1,), (0,)), ((), ()))
    nk = K // tk

    def body(a_ref, b_ref, o_ref, acc):
        k = pl.program_id(2)

        @pl.when(k == 0)
        def _():
            acc[...] = jnp.zeros_like(acc)

        acc[...] += lax.dot_general(a_ref[...].astype(BF16), b_ref[...].astype(BF16), dims,
                                    preferred_element_type=F32)

        @pl.when(k == nk - 1)
        def _():
            o_ref[...] = acc[...].astype(o_ref.dtype)

    return pl.pallas_call(
        body, name=name, grid=(M // tm, N // tn, nk),
        in_specs=[a_spec, b_spec], out_specs=pl.BlockSpec((tm, tn), lambda i, j, k: (i, j)),
        out_shape=jax.ShapeDtypeStruct((M, N), out_dtype),
        scratch_shapes=[pltpu.VMEM((tm, tn), F32)],
        compiler_params=_cparams(("parallel", "parallel", "arbitrary")),
    )(a, b)


def _ada_fwd(cc8, w_adaT, b_ada, name):
    D = cc8.shape[1]
    N = w_adaT.shape[0]
    tn = _tile(N, (512, 256, 128))

    def body(c_ref, w_ref, b_ref, o_ref):
        a = _silu(c_ref[...]).astype(BF16)
        o_ref[...] = lax.dot_general(a, w_ref[...], (((1,), (1,)), ((), ())),
                                     preferred_element_type=F32) + b_ref[...]

    return pl.pallas_call(
        body, name=name, grid=(N // tn,),
        in_specs=[pl.BlockSpec((8, D), lambda j: (0, 0)), pl.BlockSpec((tn, D), lambda j: (j, 0)),
                  pl.BlockSpec((1, tn), lambda j: (0, j))],
        out_specs=pl.BlockSpec((8, tn), lambda j: (0, j)),
        out_shape=jax.ShapeDtypeStruct((8, N), F32),
        compiler_params=_cparams(("parallel",)),
    )(cc8, w_adaT, b_ada)


def _ada_bwd_small(cc8, dsil, dm6, name):
    D = cc8.shape[1]
    N = dm6.shape[1]

    def body(c_ref, ds_ref, dm_ref, sil_ref, dc_ref, db_ref):
        c = c_ref[...]
        sil_ref[...] = _silu(c).astype(BF16)
        dc_ref[...] = ds_ref[...] * _dsilu(c)
        dm = dm_ref[...]
        row = lax.broadcasted_iota(jnp.int32, dm.shape, 0)
        db_ref[...] = jnp.where(row == 0, jnp.sum(dm, axis=0, keepdims=True), 0.0)

    return pl.pallas_call(
        body, name=name, grid=(1,),
        in_specs=[pl.BlockSpec((8, D), lambda i: (0, 0)), pl.BlockSpec((8, D), lambda i: (0, 0)),
                  pl.BlockSpec((8, N), lambda i: (0, 0))],
        out_specs=[pl.BlockSpec((8, D), lambda i: (0, 0)), pl.BlockSpec((8, D), lambda i: (0, 0)),
                   pl.BlockSpec((8, N), lambda i: (0, 0))],
        out_shape=[jax.ShapeDtypeStruct((8, D), BF16), jax.ShapeDtypeStruct((8, D), F32),
                   jax.ShapeDtypeStruct((8, N), F32)],
        compiler_params=_cparams(("arbitrary",)),
    )(cc8, dsil, dm6)


def _seg_pick(m_ref, is_ctx):
    return jnp.where(is_ctx, m_ref[1:2, :], m_ref[0:1, :])


def _norm_mod(x, g, m6, sh_idx, sc_idx, n_ctx_tiles, tr, name, resid=None):
    T, D = x.shape
    row = pl.BlockSpec((tr, D), lambda i: (i, 0))
    vec = pl.BlockSpec((1, D), lambda i: (0, 0))

    def mcol(idx):
        return pl.BlockSpec((8, D), lambda i: (0, idx))

    def body(*refs):
        if resid is None:
            x_ref, g_ref, sh_ref, sc_ref, h_ref = refs
            xv = x_ref[...]
        else:
            x_ref, f_ref, ga_ref, g_ref, sh_ref, sc_ref, xo_ref, h_ref = refs
        is_ctx = pl.program_id(0) < n_ctx_tiles
        if resid is not None:
            xv = x_ref[...] + _seg_pick(ga_ref, is_ctx) * f_ref[...]
            xo_ref[...] = xv
        rstd = lax.rsqrt(jnp.mean(xv * xv, axis=-1, keepdims=True) + EPS)
        hn = xv * rstd * g_ref[...]
        h_ref[...] = (hn * (1.0 + _seg_pick(sc_ref, is_ctx)) + _seg_pick(sh_ref, is_ctx)).astype(BF16)

    if resid is None:
        ins, in_specs = [x, g, m6, m6], [row, vec, mcol(sh_idx), mcol(sc_idx)]
        out_specs, out_shape = row, jax.ShapeDtypeStruct((T, D), BF16)
    else:
        f, ga_idx = resid
        ins = [x, f, m6, g, m6, m6]
        in_specs = [row, row, mcol(ga_idx), vec, mcol(sh_idx), mcol(sc_idx)]
        out_specs = [row, row]
        out_shape = [jax.ShapeDtypeStruct((T, D), F32), jax.ShapeDtypeStruct((T, D), BF16)]
    return pl.pallas_call(body, name=name, grid=(T // tr,), in_specs=in_specs, out_specs=out_specs,
                          out_shape=out_shape, compiler_params=_cparams(("parallel",)))(*ins)


def _resid(x, f, m6, ga_idx, n_ctx_tiles, tr, name):
    T, D = x.shape
    row = pl.BlockSpec((tr, D), lambda i: (i, 0))

    def body(x_ref, f_ref, ga_ref, o_ref):
        is_ctx = pl.program_id(0) < n_ctx_tiles
        o_ref[...] = x_ref[...] + _seg_pick(ga_ref, is_ctx) * f_ref[...]

    return pl.pallas_call(body, name=name, grid=(T // tr,),
                          in_specs=[row, row, pl.BlockSpec((8, D), lambda i: (0, ga_idx))], out_specs=row,
                          out_shape=jax.ShapeDtypeStruct((T, D), F32),
                          compiler_params=_cparams(("parallel",)))(x, f, m6)


def _resid_bwd(dx, f, m6, ga_idx, n_ctx_tiles, tr, name):
    T, D = dx.shape
    row = pl.BlockSpec((tr, D), lambda i: (i, 0))
    acc = pl.BlockSpec((8, D), lambda i: (0, 0))

    def body(dx_ref, f_ref, ga_ref, df_ref, dga_ref):
        i = pl.program_id(0)
        is_ctx = i < n_ctx_tiles

        @pl.when(i == 0)
        def _():
            dga_ref[...] = jnp.zeros_like(dga_ref)

        dxv = dx_ref[...]
        df_ref[...] = (_seg_pick(ga_ref, is_ctx) * dxv).astype(BF16)
        s = jnp.sum(dxv * f_ref[...], axis=0, keepdims=True)
        r = lax.broadcasted_iota(jnp.int32, (8, D), 0)
        dga_ref[...] += jnp.where(r == jnp.where(is_ctx, 1, 0), s, 0.0)

    return pl.pallas_call(body, name=name, grid=(T // tr,),
                          in_specs=[row, row, pl.BlockSpec((8, D), lambda i: (0, ga_idx))],
                          out_specs=[row, acc],
                          out_shape=[jax.ShapeDtypeStruct((T, D), BF16), jax.ShapeDtypeStruct((8, D), F32)],
                          compiler_params=_cparams(("arbitrary",)))(dx, f, m6)


def _norm_mod_bwd(x, dh, dxres, g, m6, sc_idx, n_ctx_tiles, tr, name):
    T, D = x.shape
    row = pl.BlockSpec((tr, D), lambda i: (i, 0))
    acc = pl.BlockSpec((8, D), lambda i: (0, 0))

    def body(x_ref, dh_ref, dr_ref, g_ref, sc_ref, dx_ref, st_ref):
        i = pl.program_id(0)
        is_ctx = i < n_ctx_tiles

        @pl.when(i == 0)
        def _():
            st_ref[...] = jnp.zeros_like(st_ref)

        xv, dh_v, gv = x_ref[...], dh_ref[...], g_ref[...]
        sc1 = 1.0 + _seg_pick(sc_ref, is_ctx)
        rstd = lax.rsqrt(jnp.mean(xv * xv, axis=-1, keepdims=True) + EPS)
        xhat = xv * rstd
        dxhat = dh_v * sc1 * gv
        dx_ref[...] = dr_ref[...] + rstd * (dxhat - xhat * jnp.mean(dxhat * xhat, axis=-1, keepdims=True))
        dsh = jnp.sum(dh_v, axis=0, keepdims=True)
        dsc = jnp.sum(dh_v * xhat * gv, axis=0, keepdims=True)
        dg = jnp.sum(dh_v * sc1 * xhat, axis=0, keepdims=True)
        r = lax.broadcasted_iota(jnp.int32, (8, D), 0)
        seg = jnp.where(is_ctx, 1, 0)
        st_ref[...] += (jnp.where(r == seg, dsh, 0.0) + jnp.where(r == 2 + seg, dsc, 0.0)
                        + jnp.where(r == 4, dg, 0.0))

    return pl.pallas_call(body, name=name, grid=(T // tr,),
                          in_specs=[row, row, row, pl.BlockSpec((1, D), lambda i: (0, 0)),
                                    pl.BlockSpec((8, D), lambda i: (0, sc_idx))],
                          out_specs=[row, acc],
                          out_shape=[jax.ShapeDtypeStruct((T, D), F32), jax.ShapeDtypeStruct((8, D), F32)],
                          compiler_params=_cparams(("arbitrary",)))(x, dh, dxres, g, m6)


def _loss_head(x, tgt, g, n_ctx_tiles, tr, name):
    T, D = x.shape
    row = pl.BlockSpec((tr, D), lambda i: (i, 0))

    def body(x_ref, t_ref, g_ref, l_ref, dx_ref, dg_ref):
        i = pl.program_id(0)

        @pl.when(i == 0)
        def _():
            l_ref[...] = jnp.zeros_like(l_ref)
            dg_ref[...] = jnp.zeros_like(dg_ref)

        @pl.when(i < n_ctx_tiles)
        def _():
            dx_ref[...] = jnp.zeros_like(dx_ref)

        @pl.when(i >= n_ctx_tiles)
        def _():
            xv, gv = x_ref[...], g_ref[...]
            rstd = lax.rsqrt(jnp.mean(xv * xv, axis=-1, keepdims=True) + EPS)
            xhat = xv * rstd
            e = xhat * gv - t_ref[...]
            l_ref[...] += 0.5 * jnp.sum(jnp.mean(e * e, axis=-1, keepdims=True), axis=0, keepdims=True)
            dy = e * (1.0 / D)
            dxhat = dy * gv
            dx_ref[...] = rstd * (dxhat - xhat * jnp.mean(dxhat * xhat, axis=-1, keepdims=True))
            r = lax.broadcasted_iota(jnp.int32, (8, D), 0)
            dg_ref[...] += jnp.where(r == 0, jnp.sum(dy * xhat, axis=0, keepdims=True), 0.0)

    return pl.pallas_call(
        body, name=name, grid=(T // tr,),
        in_specs=[row, pl.BlockSpec((tr, D), lambda i: (jnp.maximum(i - n_ctx_tiles, 0), 0)),
                  pl.BlockSpec((1, D), lambda i: (0, 0))],
        out_specs=[pl.BlockSpec((8, 128), lambda i: (0, 0)), row, pl.BlockSpec((8, D), lambda i: (0, 0))],
        out_shape=[jax.ShapeDtypeStruct((8, 128), F32), jax.ShapeDtypeStruct((T, D), F32),
                   jax.ShapeDtypeStruct((8, D), F32)],
        compiler_params=_cparams(("arbitrary",)))(x, tgt, g)


def _seq_masks(T, n_ctx, width):
    row = lax.broadcasted_iota(jnp.int32, (T, width), 0)
    in_ctx = row < n_ctx
    return jnp.where(in_ctx, row, row - n_ctx), jnp.where(in_ctx, n_ctx, T - n_ctx)


def _shift_rows(u, off, t_loc, seg_len):
    T = u.shape[0]
    if off == 0:
        return u
    v = pltpu.roll(u, (-off) % T, 0)
    ok = (t_loc + off >= 0) & (t_loc + off < seg_len)
    return jnp.where(ok, v, 0.0)


def _conv_fwd(proj, col0_blk, ncol, conv_w8, conv_b, n_ctx, tc, name):
    T = proj.shape[0]

    def body(u_ref, w_ref, b_ref, o_ref):
        u = u_ref[...]
        t_loc, seg_len = _seq_masks(T, n_ctx, tc)
        acc = jnp.broadcast_to(b_ref[...], u.shape)
        for i in range(CONV_K):
            acc = acc + w_ref[i:i + 1, :] * _shift_rows(u, i - CONV_K // 2, t_loc, seg_len)
        o_ref[...] = _silu(acc)

    return pl.pallas_call(
        body, name=name, grid=(ncol // tc,),
        in_specs=[pl.BlockSpec((T, tc), lambda j: (0, col0_blk + j)), pl.BlockSpec((8, tc), lambda j: (0, j)),
                  pl.BlockSpec((1, tc), lambda j: (0, j))],
        out_specs=pl.BlockSpec((T, tc), lambda j: (0, j)),
        out_shape=jax.ShapeDtypeStruct((T, ncol), F32),
        compiler_params=_cparams(("parallel",)))(proj, conv_w8, conv_b)


def _conv_bwd(proj, col0_blk, dact, conv_w8, conv_b, n_ctx, tc, name):
    T, ncol = dact.shape

    def body(u_ref, d_ref, w_ref, b_ref, du_ref, dw_ref):
        u = u_ref[...]
        t_loc, seg_len = _seq_masks(T, n_ctx, tc)
        pre = jnp.broadcast_to(b_ref[...], u.shape)
        shifted = []
        for i in range(CONV_K):
            s = _shift_rows(u, i - CONV_K // 2, t_loc, seg_len)
            shifted.append(s)
            pre = pre + w_ref[i:i + 1, :] * s
        dpre = d_ref[...] * _dsilu(pre)
        du = jnp.zeros_like(u)
        r = lax.broadcasted_iota(jnp.int32, (8, tc), 0)
        dw = jnp.where(r == CONV_K, jnp.sum(dpre, axis=0, keepdims=True), 0.0)
        for i in range(CONV_K):
            du = du + w_ref[i:i + 1, :] * _shift_rows(dpre, -(i - CONV_K // 2), t_loc, seg_len)
            dw = dw + jnp.where(r == i, jnp.sum(dpre * shifted[i], axis=0, keepdims=True), 0.0)
        du_ref[...] = du.astype(BF16)
        dw_ref[...] = dw

    return pl.pallas_call(
        body, name=name, grid=(ncol // tc,),
        in_specs=[pl.BlockSpec((T, tc), lambda j: (0, col0_blk + j)), pl.BlockSpec((T, tc), lambda j: (0, j)),
                  pl.BlockSpec((8, tc), lambda j: (0, j)), pl.BlockSpec((1, tc), lambda j: (0, j))],
        out_specs=[pl.BlockSpec((T, tc), lambda j: (0, j)), pl.BlockSpec((8, tc), lambda j: (0, j))],
        out_shape=[jax.ShapeDtypeStruct((T, ncol), BF16), jax.ShapeDtypeStruct((8, ncol), F32)],
        compiler_params=_cparams(("parallel",)))(proj, dact, conv_w8, conv_b)


def _pool_core(u, half, t_loc, seg_len, transpose):
    tr = u.shape[0]

    def shift(v, s):
        w = pltpu.roll(v, s % tr, 0)
        ok = (t_loc - s >= 0) & (t_loc - s < seg_len)
        return jnp.where(ok, w, 0.0)

    cnt = (jnp.minimum(t_loc, half) + jnp.minimum(seg_len - t_loc, half)).astype(F32)
    q = u / cnt if transpose else u
    back, ahead, h = q, q, 1
    while h < half:
        back = back + shift(back, h)
        ahead = ahead + shift(ahead, -h)
        h *= 2
    if transpose:
        tot = back + shift(ahead, -1)
        return tot - u
    tot = shift(back, 1) + ahead
    return tot / cnt - u


def _pool_apply(src, col0_blk, out_dtype, n_ctx, tr, pg, transpose, name):
    T = src.shape[0]
    n_ctx_tiles = n_ctx // tr

    def body(u_ref, o_ref):
        i, gi = pl.program_id(0), pl.program_id(1)
        row = lax.broadcasted_iota(jnp.int32, (tr, pg), 0)
        seg_len = jnp.where(i < n_ctx_tiles, tr, GRID_W)
        t_loc = row & (seg_len - 1)
        u = u_ref[...].astype(F32)
        for k_idx, k in enumerate(POOL_WINDOWS):
            @pl.when(gi == k_idx)
            def _(k=k):
                o_ref[...] = _pool_core(u, k // 2, t_loc, seg_len, transpose).astype(o_ref.dtype)

    return pl.pallas_call(
        body, name=name, grid=(T // tr, len(POOL_WINDOWS)),
        in_specs=[pl.BlockSpec((tr, pg), lambda i, gi: (i, col0_blk + gi))],
        out_specs=pl.BlockSpec((tr, pg), lambda i, gi: (i, gi)),
        out_shape=jax.ShapeDtypeStruct((T, pg * len(POOL_WINDOWS)), out_dtype),
        compiler_params=_cparams(("parallel", "parallel")))(src)


def _pool_mix_fwd(pm, pool_w, pool_scale, tr, name):
    T, W = pm.shape
    ng, pg = pool_w.shape[0], pool_w.shape[1]

    def body(p_ref, w_ref, s_ref, o_ref):
        o_ref[...] = (jnp.dot(p_ref[...], w_ref[...], preferred_element_type=F32) * s_ref[...]).astype(BF16)

    return pl.pallas_call(
        body, name=name, grid=(T // tr, ng),
        in_specs=[pl.BlockSpec((tr, pg), lambda i, g: (i, g)), pl.BlockSpec((None, pg, pg), lambda i, g: (g, 0, 0)),
                  pl.BlockSpec((1, pg), lambda i, g: (0, g))],
        out_specs=pl.BlockSpec((tr, pg), lambda i, g: (i, g)),
        out_shape=jax.ShapeDtypeStruct((T, W), BF16),
        compiler_params=_cparams(("parallel", "parallel")))(pm, pool_w, pool_scale)


def _pool_mix_bwd(pm, dpms, pool_w, pool_scale, tr, name):
    T, W = pm.shape
    ng, pg = pool_w.shape[0], pool_w.shape[1]

    def body(p_ref, d_ref, w_ref, s_ref, dp_ref, dw_ref, ds_ref):
        i = pl.program_id(1)

        @pl.when(i == 0)
        def _():
            dw_ref[...] = jnp.zeros_like(dw_ref)
            ds_ref[...] = jnp.zeros_like(ds_ref)

        p, w = p_ref[...], w_ref[...]
        d = d_ref[...].astype(F32)
        pmix = jnp.dot(p, w, preferred_element_type=F32)
        r = lax.broadcasted_iota(jnp.int32, (8, pg), 0)
        ds_ref[...] += jnp.where(r == 0, jnp.sum(d * pmix, axis=0, keepdims=True), 0.0)
        dmix = (d * s_ref[...]).astype(BF16)
        dp_ref[...] = lax.dot_general(dmix, w, (((1,), (1,)), ((), ())), preferred_element_type=F32)
        dw_ref[...] += lax.dot_general(p, dmix, (((0,), (0,)), ((), ())), preferred_element_type=F32)

    return pl.pallas_call(
        body, name=name, grid=(ng, T // tr),
        in_specs=[pl.BlockSpec((tr, pg), lambda g, i: (i, g)), pl.BlockSpec((tr, pg), lambda g, i: (i, g)),
                  pl.BlockSpec((None, pg, pg), lambda g, i: (g, 0, 0)), pl.BlockSpec((1, pg), lambda g, i: (0, g))],
        out_specs=[pl.BlockSpec((tr, pg), lambda g, i: (i, g)), pl.BlockSpec((None, pg, pg), lambda g, i: (g, 0, 0)),
                   pl.BlockSpec((8, pg), lambda g, i: (0, g))],
        out_shape=[jax.ShapeDtypeStruct((T, W), F32), jax.ShapeDtypeStruct((ng, pg, pg), F32),
                   jax.ShapeDtypeStruct((8, W), F32)],
        compiler_params=_cparams(("parallel", "arbitrary")))(pm, dpms, pool_w, pool_scale)


def _chunk_cumsum(v, upper):
    Q = v.shape[0]
    ii = lax.broadcasted_iota(jnp.int32, (Q, Q), 0)
    jj = lax.broadcasted_iota(jnp.int32, (Q, Q), 1)
    tri = ((jj >= ii) if upper else (jj <= ii)).astype(BF16)
    h1 = v.astype(BF16)
    r1 = v - h1.astype(F32)
    h2 = r1.astype(BF16)
    h3 = (r1 - h2.astype(F32)).astype(BF16)
    return (jnp.dot(tri, h1, preferred_element_type=F32) + jnp.dot(tri, h2, preferred_element_type=F32)
            + jnp.dot(tri, h3, preferred_element_type=F32))


def _split3(v):
    h1 = v.astype(BF16)
    r1 = v - h1.astype(F32)
    h2 = r1.astype(BF16)
    return h1, h2, (r1 - h2.astype(F32)).astype(BF16)


def _dt_prep(proj, dt_blk, bias, a_log, expand, n_heads, name):
    T = proj.shape[0]
    Wd = expand.shape[1]
    Q = CHUNK
    row = pl.BlockSpec((Q, 128), lambda i: (i, 0))
    wide = pl.BlockSpec((Q, Wd), lambda i: (i, 0))

    def body(r_ref, b_ref, al_ref, e_ref, l1_ref, l2_ref, l3_ref, dtb_ref, ein_ref, dte_ref, etot_ref):
        xv = r_ref[...] + b_ref[...]
        dt = jnp.maximum(xv, 0.0) + jnp.log(1.0 + jnp.exp(-jnp.abs(xv)))
        a = -jnp.exp(al_ref[...]) * dt
        fwd_col = lax.broadcasted_iota(jnp.int32, (Q, 128), 1) < n_heads
        lam = jnp.where(fwd_col, _chunk_cumsum(a, False), _chunk_cumsum(a, True))
        tot = jnp.where(fwd_col[0:1], lam[Q - 1:Q], lam[0:1])
        l1_ref[...], l2_ref[...], l3_ref[...] = _split3(lam)
        etot_ref[...] = jnp.broadcast_to(jnp.exp(tot), (8, 128))
        ex = e_ref[...]

        def rep(v):
            p1, p2, p3 = _split3(v)
            return (jnp.dot(p1, ex, preferred_element_type=F32) + jnp.dot(p2, ex, preferred_element_type=F32)
                    + jnp.dot(p3, ex, preferred_element_type=F32))

        dtb_ref[...] = rep(dt)
        ein_ref[...] = rep(jnp.exp(lam))
        dte_ref[...] = rep(jnp.exp(tot - lam))

    vec = pl.BlockSpec((1, 128), lambda i: (0, 0))
    return pl.pallas_call(
        body, name=name, grid=(T // Q,),
        in_specs=[pl.BlockSpec((Q, 128), lambda i: (i, dt_blk)), vec, vec, pl.BlockSpec((128, Wd), lambda i: (0, 0))],
        out_specs=[row, row, row, wide, wide, wide, pl.BlockSpec((8, 128), lambda i: (i, 0))],
        out_shape=[jax.ShapeDtypeStruct((T, 128), BF16)] * 3 + [jax.ShapeDtypeStruct((T, Wd), F32)] * 3
        + [jax.ShapeDtypeStruct((T // Q * 8, 128), F32)],
        compiler_params=_cparams(("parallel",)))(proj, bias, a_log, expand)


def _dt_bwd(proj, dt_blk, bias, a_log, ddt, dlam, n_heads, tr, name):
    T = proj.shape[0]
    row = pl.BlockSpec((tr, 128), lambda i: (i, 0))
    vec = pl.BlockSpec((1, 128), lambda i: (0, 0))

    def body(r_ref, b_ref, al_ref, ddt_ref, dl_ref, o_ref, st_ref):
        @pl.when(pl.program_id(0) == 0)
        def _():
            st_ref[...] = jnp.zeros_like(st_ref)

        xv = r_ref[...] + b_ref[...]
        dt = jnp.maximum(xv, 0.0) + jnp.log(1.0 + jnp.exp(-jnp.abs(xv)))
        a_neg = -jnp.exp(al_ref[...])
        col = lax.broadcasted_iota(jnp.int32, (CHUNK, 128), 1)
        dl = dl_ref[...]
        parts = []
        for k in range(tr // CHUNK):
            dk = dl[k * CHUNK:(k + 1) * CHUNK]
            parts.append(jnp.where(col < n_heads, _chunk_cumsum(dk, True), _chunk_cumsum(dk, False)))
        dav = jnp.concatenate(parts, axis=0)
        draw = (ddt_ref[...] + dav * a_neg) * jax.nn.sigmoid(xv)
        o_ref[...] = draw.astype(BF16)
        r = lax.broadcasted_iota(jnp.int32, (8, 128), 0)
        st_ref[...] += (jnp.where(r == 0, jnp.sum(draw, axis=0, keepdims=True), 0.0)
                        + jnp.where(r == 1, jnp.sum(dav * dt, axis=0, keepdims=True) * a_neg, 0.0))

    return pl.pallas_call(
        body, name=name, grid=(T // tr,),
        in_specs=[pl.BlockSpec((tr, 128), lambda i: (i, dt_blk)), vec, vec, row, row],
        out_specs=[row, pl.BlockSpec((8, 128), lambda i: (0, 0))],
        out_shape=[jax.ShapeDtypeStruct((T, 128), BF16), jax.ShapeDtypeStruct((8, 128), F32)],
        compiler_params=_cparams(("arbitrary",)))(proj, bias, a_log, ddt, dlam)


def _scan_chunk(d, pos, nc_ctx, nc):
    rev = jnp.where(pos < nc_ctx, nc_ctx - 1 - pos, nc - 1 - (pos - nc_ctx))
    return jnp.where(d == 0, pos, rev)


def _chunk_mask(d):
    ii = lax.broadcasted_iota(jnp.int32, (CHUNK, CHUNK), 0)
    jj = lax.broadcasted_iota(jnp.int32, (CHUNK, CHUNK), 1)
    return (ii - jj) * jnp.where(d == 0, 1, -1) >= 0


def _ssd_specs(T, G, n_ctx, gpb, chunk_of):
    R, P, N, Q = HPG, HEADDIM, STATE, CHUNK
    H = G * R
    nc, nc_ctx = T // Q, n_ctx // Q
    xw, bw = gpb * R * P, gpb * N
    b_blk0 = (H * P) // bw
    c_blk0 = b_blk0 + G // gpb

    def ch(d, s):
        return chunk_of(d, s, nc_ctx, nc)

    return dict(
        x=pl.BlockSpec((Q, xw), lambda d, g, s: (ch(d, s), g)),
        b=pl.BlockSpec((Q, bw), lambda d, g, s: (ch(d, s), b_blk0 + g)),
        c=pl.BlockSpec((Q, bw), lambda d, g, s: (ch(d, s), c_blk0 + g)),
        col=pl.BlockSpec((None, gpb, Q, R), lambda d, g, s: (d, g, ch(d, s), 0)),
        row=pl.BlockSpec((None, gpb, R, Q), lambda d, g, s: (d, g, 0, ch(d, s))),
        rep=pl.BlockSpec((Q, xw), lambda d, g, s: (ch(d, s), d * (G // gpb) + g)),
        lam_a=pl.BlockSpec((None, gpb * R, Q, 16), lambda d, g, s: (d, g, ch(d, s), 0)),
        lam_b=pl.BlockSpec((None, gpb * R, 16, Q), lambda d, g, s: (d, g, 0, ch(d, s))),
        etot=pl.BlockSpec((None, gpb, None, 8, 128), lambda d, g, s: (d, g, ch(d, s), 0, 0)),
        dsk=pl.BlockSpec((None, 8, xw), lambda d, g, s: (d, 0, g)),
        xd=pl.BlockSpec((None, Q, xw), lambda d, g, s: (d, ch(d, s), g)),
        bd=pl.BlockSpec((None, Q, bw), lambda d, g, s: (d, ch(d, s), g)),
        st=pl.BlockSpec((None, None, gpb * R // 2, 2 * P, N), lambda d, g, s: (d, ch(d, s), g, 0, 0)),
    )


def _ssd_fwd(xbc, dtb, ein, dte, lam_a, lam_b, etot, dsk, G, n_ctx, gpb, name):
    T = xbc.shape[0]
    R, P, N, Q = HPG, HEADDIM, STATE, CHUNK
    H = G * R
    nc = T // Q
    sp = _ssd_specs(T, G, n_ctx, gpb, _scan_chunk)

    def body(x_ref, b_ref, c_ref, dt_ref, ein_ref, dte_ref, la_ref, lb_ref, et_ref, dsk_ref, y_ref, st_ref, S):
        d, s = pl.program_id(0), pl.program_id(2)

        @pl.when(s == 0)
        def _():
            S[...] = jnp.zeros_like(S)

        mask = _chunk_mask(d)
        head0 = lax.broadcasted_iota(jnp.int32, (Q, 2 * P), 1) < P
        rows0 = lax.broadcasted_iota(jnp.int32, (2 * P, N), 0) < P
        for gg in range(gpb):
            Bm = b_ref[:, gg * N:(gg + 1) * N].astype(BF16)
            Cm = c_ref[:, gg * N:(gg + 1) * N].astype(BF16)
            Gm = lax.dot_general(Cm, Bm, (((1,), (1,)), ((), ())), preferred_element_type=F32)
            for k in range(R // 2):
                pk = gg * (R // 2) + k
                sl = slice(pk * 2 * P, (pk + 1) * 2 * P)
                xp = x_ref[:, sl]
                xc = xp * dt_ref[:, sl]
                s_in = S[pk]
                y = lax.dot_general(Cm, s_in.astype(BF16), (((1,), (1,)), ((), ())),
                                    preferred_element_type=F32) * ein_ref[:, sl] + dsk_ref[0:1, sl] * xp
                for j in range(2):
                    hr = 2 * pk + j
                    diff = jnp.dot(la_ref[hr], lb_ref[hr], preferred_element_type=F32)
                    ldec = jnp.exp(jnp.where(mask, diff, NEG_BIG))
                    xc_j = (jnp.where(head0, xc, 0.0) if j == 0 else jnp.where(head0, 0.0, xc)).astype(BF16)
                    y = y + jnp.dot((Gm * ldec).astype(BF16), xc_j, preferred_element_type=F32)
                y_ref[:, sl] = y
                st_ref[pk] = s_in
                e_all = jnp.where(rows0, et_ref[gg, 2 * k:2 * k + 1, :], et_ref[gg, 2 * k + 1:2 * k + 2, :])
                xd = (xc * dte_ref[:, sl]).astype(BF16)
                S[pk] = e_all * s_in + lax.dot_general(xd, Bm, (((0,), (0,)), ((), ())),
                                                       preferred_element_type=F32)

    return pl.pallas_call(
        body, name=name, grid=(2, G // gpb, nc),
        in_specs=[sp["x"], sp["b"], sp["c"], sp["rep"], sp["rep"], sp["rep"], sp["lam_a"], sp["lam_b"], sp["etot"],
                  sp["dsk"]],
        out_specs=[sp["xd"], sp["st"]],
        out_shape=[jax.ShapeDtypeStruct((2, T, H * P), F32), jax.ShapeDtypeStruct((2, nc, H // 2, 2 * P, N), F32)],
        scratch_shapes=[pltpu.VMEM((gpb * R // 2, 2 * P, N), F32)],
        compiler_params=_cparams(("parallel", "parallel", "arbitrary")))(
            xbc, xbc, xbc, dtb, ein, dte, lam_a, lam_b, etot, dsk)


def _ssd_bwd(xbc, dy, states, dtb, ein, dte, lam_a, lam_b, etot, G, n_ctx, gpb, name):
    T = xbc.shape[0]
    R, P, N, Q = HPG, HEADDIM, STATE, CHUNK
    H = G * R
    nc = T // Q
    sp = _ssd_specs(T, G, n_ctx, gpb, lambda d, s, nc_ctx, n: _scan_chunk(d, n - 1 - s, nc_ctx, n))

    def body(x_ref, b_ref, c_ref, dy_ref, st_ref, dt_ref, ein_ref, dte_ref, la_ref, lb_ref, et_ref,
             dx_ref, db_ref, dc_ref, ddt_ref, dlc_ref, dlr_ref, dS):
        d, s = pl.program_id(0), pl.program_id(2)

        @pl.when(s == 0)
        def _():
            dS[...] = jnp.zeros_like(dS)

        mask = _chunk_mask(d)
        ri = lax.broadcasted_iota(jnp.int32, (Q, 1), 0)
        is_last = ri == jnp.where(d == 0, Q - 1, 0)
        head0 = lax.broadcasted_iota(jnp.int32, (Q, 2 * P), 1) < P
        rows0 = lax.broadcasted_iota(jnp.int32, (2 * P, N), 0) < P

        def total(v):
            return jnp.sum(jnp.sum(v, axis=1, keepdims=True), axis=0, keepdims=True)

        for gg in range(gpb):
            Bm = b_ref[:, gg * N:(gg + 1) * N].astype(BF16)
            Cm = c_ref[:, gg * N:(gg + 1) * N].astype(BF16)
            Gm = lax.dot_general(Cm, Bm, (((1,), (1,)), ((), ())), preferred_element_type=F32)
            dG = jnp.zeros((Q, Q), F32)
            dB = jnp.zeros((Q, N), F32)
            dC = jnp.zeros((Q, N), F32)
            for k in range(R // 2):
                pk = gg * (R // 2) + k
                sl = slice(pk * 2 * P, (pk + 1) * 2 * P)
                e_in = ein_ref[:, sl]
                dte = dte_ref[:, sl]
                e_all = jnp.where(rows0, et_ref[gg, 2 * k:2 * k + 1, :], et_ref[gg, 2 * k + 1:2 * k + 2, :])
                xp = x_ref[:, sl]
                dtp = dt_ref[:, sl]
                xc = xp * dtp
                xc_b = xc.astype(BF16)
                dyp = dy_ref[:, sl]
                s_in = st_ref[pk]
                s_in_b = s_in.astype(BF16)
                ds_out = dS[pk]
                ds_out_b = ds_out.astype(BF16)
                y_int = lax.dot_general(Cm, s_in_b, (((1,), (1,)), ((), ())), preferred_element_type=F32) * e_in
                b_ds = lax.dot_general(Bm, ds_out_b, (((1,), (1,)), ((), ())), preferred_element_type=F32)
                dxc = dte * b_ds
                u = xc * dxc
                v = dyp * y_int - u
                sse = ds_out * s_in * e_all
                for j in range(2):
                    hr, r = 2 * pk + j, 2 * k + j
                    def pick(a, m0=head0, j=j):
                        return jnp.where(m0, a, 0.0) if j == 0 else jnp.where(m0, 0.0, a)

                    diff = jnp.dot(la_ref[hr], lb_ref[hr], preferred_element_type=F32)
                    ldec = jnp.exp(jnp.where(mask, diff, NEG_BIG))
                    dy_j = pick(dyp).astype(BF16)
                    dM = lax.dot_general(dy_j, xc_b, (((1,), (1,)), ((), ())), preferred_element_type=F32)
                    dMl = dM * ldec
                    Wm = dMl * Gm
                    dlam_c = jnp.sum(Wm, axis=1, keepdims=True) + jnp.sum(pick(v), axis=1, keepdims=True)
                    last = total(pick(sse, rows0)) + total(pick(u))
                    dlc_ref[gg, :, r:r + 1] = dlam_c + jnp.where(is_last, last, 0.0)
                    dlr_ref[gg, r:r + 1, :] = -jnp.sum(Wm, axis=0, keepdims=True)
                    dxc = dxc + lax.dot_general((Gm * ldec).astype(BF16), dy_j, (((0,), (0,)), ((), ())),
                                                preferred_element_type=F32)
                    dG = dG + dMl
                dx_ref[:, sl] = dxc * dtp
                t = dxc * xp
                ddt_ref[gg, :, 2 * k:2 * k + 1] = jnp.sum(jnp.where(head0, t, 0.0), axis=1, keepdims=True)
                ddt_ref[gg, :, 2 * k + 1:2 * k + 2] = jnp.sum(jnp.where(head0, 0.0, t), axis=1, keepdims=True)
                edy_b = (e_in * dyp).astype(BF16)
                dC = dC + jnp.dot(edy_b, s_in_b, preferred_element_type=F32)
                dB = dB + jnp.dot((dte * xc).astype(BF16), ds_out_b, preferred_element_type=F32)
                dS[pk] = e_all * ds_out + lax.dot_general(edy_b, Cm, (((0,), (0,)), ((), ())),
                                                          preferred_element_type=F32)
            dG_b = dG.astype(BF16)
            dc_ref[:, gg * N:(gg + 1) * N] = dC + jnp.dot(dG_b, Bm, preferred_element_type=F32)
            db_ref[:, gg * N:(gg + 1) * N] = dB + lax.dot_general(dG_b, Cm, (((0,), (0,)), ((), ())),
                                                                  preferred_element_type=F32)

    return pl.pallas_call(
        body, name=name, grid=(2, G // gpb, nc),
        in_specs=[sp["x"], sp["b"], sp["c"], sp["x"], sp["st"], sp["rep"], sp["rep"], sp["rep"], sp["lam_a"],
                  sp["lam_b"], sp["etot"]],
        out_specs=[sp["xd"], sp["bd"], sp["bd"], sp["col"], sp["col"], sp["row"]],
        out_shape=[jax.ShapeDtypeStruct((2, T, H * P), F32), jax.ShapeDtypeStruct((2, T, G * N), F32),
                   jax.ShapeDtypeStruct((2, T, G * N), F32), jax.ShapeDtypeStruct((2, G, T, R), F32),
                   jax.ShapeDtypeStruct((2, G, T, R), F32), jax.ShapeDtypeStruct((2, G, R, T), F32)],
        scratch_shapes=[pltpu.VMEM((gpb * R // 2, 2 * P, N), F32)],
        compiler_params=_cparams(("parallel", "parallel", "arbitrary")))(
            xbc, xbc, xbc, dy, states, dtb, ein, dte, lam_a, lam_b, etot)


def _ssd_combine(dxs2, db2, dc2, dy, xbc, dskv, tr, name):
    T, HP = dy.shape
    GN = db2.shape[2]
    ncol = HP + 2 * GN
    tc = _tile(GN, (512, 256, 128))
    nx, nb = HP // tc, GN // tc

    def body(dx_ref, db_ref, dc_ref, dy_ref, x_ref, k_ref, o_ref, dk_ref):
        i, j = pl.program_id(1), pl.program_id(0)

        @pl.when((i == 0) & (j < nx))
        def _():
            dk_ref[...] = jnp.zeros_like(dk_ref)

        @pl.when(j < nx)
        def _():
            dyv = dy_ref[...]
            o_ref[...] = dx_ref[0] + dx_ref[1] + (k_ref[0:1, :] + k_ref[1:2, :]) * dyv
            r = lax.broadcasted_iota(jnp.int32, (8, tc), 0)
            dk_ref[...] += jnp.where(r < 2, jnp.sum(dyv * x_ref[...], axis=0, keepdims=True), 0.0)

        @pl.when((j >= nx) & (j < nx + nb))
        def _():
            o_ref[...] = db_ref[0] + db_ref[1]

        @pl.when(j >= nx + nb)
        def _():
            o_ref[...] = dc_ref[0] + dc_ref[1]

    def cl(j, lo, n):
        return jnp.clip(j - lo, 0, n - 1)

    return pl.pallas_call(
        body, name=name, grid=(ncol // tc, T // tr),
        in_specs=[pl.BlockSpec((2, tr, tc), lambda j, i: (0, i, cl(j, 0, nx))),
                  pl.BlockSpec((2, tr, tc), lambda j, i: (0, i, cl(j, nx, nb))),
                  pl.BlockSpec((2, tr, tc), lambda j, i: (0, i, cl(j, nx + nb, nb))),
                  pl.BlockSpec((tr, tc), lambda j, i: (i, cl(j, 0, nx))),
                  pl.BlockSpec((tr, tc), lambda j, i: (i, cl(j, 0, nx))),
                  pl.BlockSpec((8, tc), lambda j, i: (0, cl(j, 0, nx)))],
        out_specs=[pl.BlockSpec((tr, tc), lambda j, i: (i, j)), pl.BlockSpec((8, tc), lambda j, i: (0, cl(j, 0, nx)))],
        out_shape=[jax.ShapeDtypeStruct((T, ncol), F32), jax.ShapeDtypeStruct((8, HP), F32)],
        compiler_params=_cparams(("arbitrary", "arbitrary")))(dxs2, db2, dc2, dy, xbc, dskv)


def _gnorm_fwd(y2, proj, w, gs, tr, name):
    T, HP = y2.shape[1], y2.shape[2]

    def body(y_ref, z_ref, w_ref, o_ref):
        yz = (y_ref[0] + y_ref[1]) * _silu(z_ref[...])
        for g in range(HP // gs):
            v = yz[:, g * gs:(g + 1) * gs]
            rstd = lax.rsqrt(jnp.mean(v * v, axis=-1, keepdims=True) + EPS)
            o_ref[:, g * gs:(g + 1) * gs] = (v * rstd * w_ref[:, g * gs:(g + 1) * gs]).astype(BF16)

    return pl.pallas_call(
        body, name=name, grid=(T // tr,),
        in_specs=[pl.BlockSpec((2, tr, HP), lambda i: (0, i, 0)), pl.BlockSpec((tr, HP), lambda i: (i, 0)),
                  pl.BlockSpec((1, HP), lambda i: (0, 0))],
        out_specs=pl.BlockSpec((tr, HP), lambda i: (i, 0)),
        out_shape=jax.ShapeDtypeStruct((T, HP), BF16),
        compiler_params=_cparams(("parallel",)))(y2, proj, w)


def _gnorm_bwd(y2, proj, w, dyn, gs, tr, name):
    T, HP = y2.shape[1], y2.shape[2]
    row = pl.BlockSpec((tr, HP), lambda i: (i, 0))

    def body(y_ref, z_ref, w_ref, d_ref, dy_ref, dz_ref, dw_ref):
        @pl.when(pl.program_id(0) == 0)
        def _():
            dw_ref[...] = jnp.zeros_like(dw_ref)

        yv = y_ref[0] + y_ref[1]
        zv = z_ref[...]
        sz = _silu(zv)
        yz = yv * sz
        dv = d_ref[...]
        r = lax.broadcasted_iota(jnp.int32, (8, gs), 0)
        for g in range(HP // gs):
            sl = slice(g * gs, (g + 1) * gs)
            v = yz[:, sl]
            rstd = lax.rsqrt(jnp.mean(v * v, axis=-1, keepdims=True) + EPS)
            xhat = v * rstd
            dyn_g = dv[:, sl]
            dhat = dyn_g * w_ref[:, sl]
            dyz = rstd * (dhat - xhat * jnp.mean(dhat * xhat, axis=-1, keepdims=True))
            dy_ref[:, sl] = dyz * sz[:, sl]
            dz_ref[:, sl] = (dyz * yv[:, sl] * _dsilu(zv[:, sl])).astype(BF16)
            dw_ref[:, sl] += jnp.where(r == 0, jnp.sum(dyn_g * xhat, axis=0, keepdims=True), 0.0)

    return pl.pallas_call(
        body, name=name, grid=(T // tr,),
        in_specs=[pl.BlockSpec((2, tr, HP), lambda i: (0, i, 0)), row, pl.BlockSpec((1, HP), lambda i: (0, 0)), row],
        out_specs=[row, row, pl.BlockSpec((8, HP), lambda i: (0, 0))],
        out_shape=[jax.ShapeDtypeStruct((T, HP), F32), jax.ShapeDtypeStruct((T, HP), BF16),
                   jax.ShapeDtypeStruct((8, HP), F32)],
        compiler_params=_cparams(("arbitrary",)))(y2, proj, w, dyn)


def _merge_fwd(proj, g1_blk, o_ssd, o_pool, tr, name):
    T, D = o_ssd.shape
    row = pl.BlockSpec((tr, D), lambda i: (i, 0))

    def body(g1_ref, g2_ref, a_ref, b_ref, o_ref):
        o_ref[...] = (jax.nn.sigmoid(g1_ref[...]) * a_ref[...]
                      + jax.nn.sigmoid(g2_ref[...]) * b_ref[...]).astype(BF16)

    return pl.pallas_call(
        body, name=name, grid=(T // tr,),
        in_specs=[pl.BlockSpec((tr, D), lambda i: (i, g1_blk)), pl.BlockSpec((tr, D), lambda i: (i, g1_blk + 1)),
                  row, row],
        out_specs=row, out_shape=jax.ShapeDtypeStruct((T, D), BF16),
        compiler_params=_cparams(("parallel",)))(proj, proj, o_ssd, o_pool)


def _merge_bwd(proj, g1_blk, o_ssd, o_pool, dmg, tr, name):
    T, D = o_ssd.shape
    row = pl.BlockSpec((tr, D), lambda i: (i, 0))

    def body(g1_ref, g2_ref, a_ref, b_ref, d_ref, da_ref, db_ref, dg_ref):
        s1, s2 = jax.nn.sigmoid(g1_ref[...]), jax.nn.sigmoid(g2_ref[...])
        dv = d_ref[...]
        da_ref[...] = (s1 * dv).astype(BF16)
        db_ref[...] = (s2 * dv).astype(BF16)
        dg_ref[:, :D] = (dv * a_ref[...] * s1 * (1.0 - s1)).astype(BF16)
        dg_ref[:, D:] = (dv * b_ref[...] * s2 * (1.0 - s2)).astype(BF16)

    return pl.pallas_call(
        body, name=name, grid=(T // tr,),
        in_specs=[pl.BlockSpec((tr, D), lambda i: (i, g1_blk)), pl.BlockSpec((tr, D), lambda i: (i, g1_blk + 1)),
                  row, row, row],
        out_specs=[row, row, pl.BlockSpec((tr, 2 * D), lambda i: (i, 0))],
        out_shape=[jax.ShapeDtypeStruct((T, D), BF16), jax.ShapeDtypeStruct((T, D), BF16),
                   jax.ShapeDtypeStruct((T, 2 * D), BF16)],
        compiler_params=_cparams(("parallel",)))(proj, proj, o_ssd, o_pool, dmg)


def _swiglu_fwd(gu, tr, name):
    T, F2 = gu.shape
    F = F2 // 2
    tc = _tile(F, (1408, 768, 512, 256, 128))
    nb = F // tc

    def body(a_ref, b_ref, o_ref):
        o_ref[...] = (_silu(a_ref[...]) * b_ref[...]).astype(BF16)

    return pl.pallas_call(
        body, name=name, grid=(T // tr, nb),
        in_specs=[pl.BlockSpec((tr, tc), lambda i, j: (i, j)), pl.BlockSpec((tr, tc), lambda i, j: (i, nb + j))],
        out_specs=pl.BlockSpec((tr, tc), lambda i, j: (i, j)),
        out_shape=jax.ShapeDtypeStruct((T, F), BF16),
        compiler_params=_cparams(("parallel", "parallel")))(gu, gu)


def _swiglu_bwd(gu, dact, tr, name):
    T, F2 = gu.shape
    F = F2 // 2
    tc = _tile(F, (1408, 768, 512, 256, 128))
    nb = F // tc

    def body(a_ref, b_ref, d_ref, o_ref):
        is_a = pl.program_id(1) < nb
        av, bv, dv = a_ref[...], b_ref[...], d_ref[...]
        o_ref[...] = jnp.where(is_a, dv * bv * _dsilu(av), dv * _silu(av)).astype(BF16)

    return pl.pallas_call(
        body, name=name, grid=(T // tr, 2 * nb),
        in_specs=[pl.BlockSpec((tr, tc), lambda i, j: (i, j % nb)), pl.BlockSpec((tr, tc), lambda i, j: (i, nb + j % nb)),
                  pl.BlockSpec((tr, tc), lambda i, j: (i, j % nb))],
        out_specs=pl.BlockSpec((tr, tc), lambda i, j: (i, j)),
        out_shape=jax.ShapeDtypeStruct((T, F2), BF16),
        compiler_params=_cparams(("parallel", "parallel")))(gu, gu, dact)


def _adamw(w, g, m, v, name):
    Rr, C = w.shape
    tr = _tile(Rr, (256, 128, 64, 32, 16, 8))
    row = pl.BlockSpec((tr, C), lambda i: (i, 0))

    def body(w_ref, g_ref, m_ref, v_ref, d_ref, mo_ref, vo_ref):
        gv = g_ref[...]
        mn = ADAM_B1 * m_ref[...] + (1.0 - ADAM_B1) * gv
        vn = ADAM_B2 * v_ref[...] + (1.0 - ADAM_B2) * (gv * gv)
        m_hat = mn / (1.0 - ADAM_B1 ** ADAM_STEP)
        v_hat = vn / (1.0 - ADAM_B2 ** ADAM_STEP)
        d_ref[...] = -ADAM_LR * (m_hat / (jnp.sqrt(v_hat) + ADAM_EPS) + ADAM_WD * w_ref[...])
        mo_ref[...] = mn
        vo_ref[...] = vn

    sds = jax.ShapeDtypeStruct((Rr, C), F32)
    return pl.pallas_call(body, name=name, grid=(Rr // tr,), in_specs=[row] * 4, out_specs=[row] * 3,
                          out_shape=[sds] * 3, compiler_params=_cparams(("parallel",)))(w, g, m, v)


def _sum_slots(x, name):
    n, Rr, C = x.shape
    tr = _tile(Rr, (512, 256, 128, 64, 32, 16, 8))

    def body(x_ref, o_ref):
        acc = x_ref[0].astype(F32)
        for k in range(1, n):
            acc = acc + x_ref[k].astype(F32)
        o_ref[...] = acc

    return pl.pallas_call(body, name=name, grid=(Rr // tr,),
                          in_specs=[pl.BlockSpec((n, tr, C), lambda i: (0, i, 0))],
                          out_specs=pl.BlockSpec((tr, C), lambda i: (i, 0)),
                          out_shape=jax.ShapeDtypeStruct((Rr, C), F32),
                          compiler_params=_cparams(("parallel",)))(x)


def _place():
    return lax.axis_index("x"), lax.axis_index("y"), lax.axis_index("c")


def _all_gather(x, name):
    Rr, C = x.shape

    def body(x_ref, out_ref, send_sems, recv_sems, local_sem):
        mx, my, mc = _place()
        me, sibling = (mx, my, mc), (mx, my, 1 - mc)
        chips = [(1 - mx, my), (mx, 1 - my), (1 - mx, 1 - my)]

        def slot(px, py, pc):
            return out_ref.at[4 * px + 2 * py + pc]

        def copy(k, block, to, src=None):
            return pltpu.make_async_remote_copy(
                src_ref=slot(*block) if src is None else src, dst_ref=slot(*block),
                send_sem=send_sems.at[k], recv_sem=recv_sems.at[k],
                device_id=to, device_id_type=pl.DeviceIdType.MESH)

        mine = pltpu.make_async_copy(x_ref, slot(*me), local_sem)
        mine.start()
        first = [copy(0, me, sibling, src=x_ref)]
        first += [copy(1 + j, me, (*chip, mc), src=x_ref) for j, chip in enumerate(chips)]
        for cp in first:
            cp.start()
        passed = [copy(4 + j, (*chip, mc), sibling) for j, chip in enumerate(chips)]
        for j, chip in enumerate(chips):
            copy(1 + j, (*chip, mc), me).wait_recv()
            passed[j].start()
        copy(0, sibling, me).wait_recv()
        for j, chip in enumerate(chips):
            copy(4 + j, (*chip, 1 - mc), me).wait_recv()
        for cp in first + passed:
            cp.wait_send()
        mine.wait()

    return pl.pallas_call(
        body, name=name, in_specs=[ANY], out_specs=ANY,
        out_shape=jax.ShapeDtypeStruct((N_DEV, Rr, C), x.dtype),
        scratch_shapes=[pltpu.SemaphoreType.DMA((7,)), pltpu.SemaphoreType.DMA((7,)), pltpu.SemaphoreType.DMA],
    )(x)


def _pair_exchange(buf, name):
    _, n, Rr, C = buf.shape
    parts = 4
    pr = Rr // parts

    def body(b_ref, got_ref, send_sems, recv_sems):
        mx, my, mc = _place()
        copies = []
        for q in range(n):
            for p in range(parts):
                rows = pl.ds(p * pr, pr)
                cp = pltpu.make_async_remote_copy(
                    src_ref=b_ref.at[1 - mc, q, rows], dst_ref=got_ref.at[q, rows],
                    send_sem=send_sems.at[q * parts + p], recv_sem=recv_sems.at[q * parts + p],
                    device_id=(mx, my, 1 - mc), device_id_type=pl.DeviceIdType.MESH)
                cp.start()
                copies.append(cp)
        for cp in copies:
            cp.wait()

    return pl.pallas_call(
        body, name=name, in_specs=[ANY], out_specs=ANY,
        out_shape=jax.ShapeDtypeStruct((n, Rr, C), buf.dtype),
        scratch_shapes=[pltpu.SemaphoreType.DMA((n * parts,)), pltpu.SemaphoreType.DMA((n * parts,))],
    )(buf)


def _pair_add(buf, got, my_c, name):
    _, n, Rr, C = buf.shape
    tr = _tile(Rr, (512, 256, 128, 64, 32, 16))

    def body(c_ref, b_ref, g_ref, o_ref):
        o_ref[...] = (b_ref[...].astype(F32) + g_ref[...].astype(F32)).astype(BF16)

    return pl.pallas_call(
        body, name=name,
        grid_spec=pltpu.PrefetchScalarGridSpec(
            num_scalar_prefetch=1, grid=(n, Rr // tr),
            in_specs=[pl.BlockSpec((None, None, tr, C), lambda q, i, c: (c[0], q, i, 0)),
                      pl.BlockSpec((None, tr, C), lambda q, i, c: (q, i, 0))],
            out_specs=pl.BlockSpec((None, tr, C), lambda q, i, c: (q, i, 0))),
        out_shape=jax.ShapeDtypeStruct((n, Rr, C), BF16),
        compiler_params=_cparams(("parallel", "parallel")))(my_c, buf, got)


def _chip_exchange(red, name):
    n, Rr, C = red.shape

    def body(r_ref, out_ref, send_sems, recv_sems, local_sem):
        mx, my, mc = _place()
        chips = [(1 - mx, my), (mx, 1 - my), (1 - mx, 1 - my)]
        mine = pltpu.make_async_copy(r_ref.at[2 * mx + my], out_ref.at[2 * mx + my], local_sem)
        mine.start()
        sends = []
        for k, (px, py) in enumerate(chips):
            cp = pltpu.make_async_remote_copy(
                src_ref=r_ref.at[2 * px + py], dst_ref=out_ref.at[2 * mx + my],
                send_sem=send_sems.at[k], recv_sem=recv_sems.at[k],
                device_id=(px, py, mc), device_id_type=pl.DeviceIdType.MESH)
            cp.start()
            sends.append(cp)
        for k, (px, py) in enumerate(chips):
            pltpu.make_async_remote_copy(
                src_ref=r_ref.at[2 * px + py], dst_ref=out_ref.at[2 * px + py],
                send_sem=send_sems.at[k], recv_sem=recv_sems.at[k],
                device_id=(px, py, mc), device_id_type=pl.DeviceIdType.MESH).wait_recv()
        for cp in sends:
            cp.wait_send()
        mine.wait()

    return pl.pallas_call(
        body, name=name, in_specs=[ANY], out_specs=ANY,
        out_shape=jax.ShapeDtypeStruct((n, Rr, C), red.dtype),
        scratch_shapes=[pltpu.SemaphoreType.DMA((3,)), pltpu.SemaphoreType.DMA((3,)), pltpu.SemaphoreType.DMA],
    )(red)


def _pad_rows(a, rows):
    return jnp.pad(a, ((0, rows - a.shape[0]), (0, 0)))


class _Layout:
    def __init__(self, D, shards):
        self.D = D
        self.pieces = []
        off = 0
        for name, layer, rows in shards:
            pr = _round_up(rows, 16)
            self.pieces.append((name, layer, rows, pr, off))
            off += pr
        self.rows = _round_up(off, 64)

    def find(self, name, layer):
        for p in self.pieces:
            if p[0] == name and p[1] == layer:
                return p
        raise KeyError(name)


def kernel(x, c, ctx, c_ctx, w_ada, b_ada, g_mix, w_in, conv_w, conv_b, dt_bias, a_log, d_skip, ssd_norm_w, w_ssd_out, pool_w, pool_scale, w_pool_out, w_out, g_ffn, w_gate_up, w_down, g_final, loss_target, m_c_ctx, m_w_ada, m_b_ada, m_g_mix, m_w_in, m_conv_w, m_conv_b, m_dt_bias, m_a_log, m_d_skip, m_ssd_norm_w, m_w_ssd_out, m_pool_w, m_pool_scale, m_w_pool_out, m_w_out, m_g_ffn, m_w_gate_up, m_w_down, m_g_final, v_c_ctx, v_w_ada, v_b_ada, v_g_mix, v_w_in, v_conv_w, v_conv_b, v_dt_bias, v_a_log, v_d_skip, v_ssd_norm_w, v_w_ssd_out, v_pool_w, v_pool_scale, v_w_pool_out, v_w_out, v_g_ffn, v_w_gate_up, v_w_down, v_g_final):
    weights = dict(c_ctx=c_ctx, w_ada=w_ada, b_ada=b_ada, g_mix=g_mix, w_in=w_in, conv_w=conv_w, conv_b=conv_b,
                   dt_bias=dt_bias, a_log=a_log, d_skip=d_skip, ssd_norm_w=ssd_norm_w, w_ssd_out=w_ssd_out,
                   pool_w=pool_w, pool_scale=pool_scale, w_pool_out=w_pool_out, w_out=w_out, g_ffn=g_ffn,
                   w_gate_up=w_gate_up, w_down=w_down, g_final=g_final)
    moms_m = dict(c_ctx=m_c_ctx, w_ada=m_w_ada, b_ada=m_b_ada, g_mix=m_g_mix, w_in=m_w_in, conv_w=m_conv_w,
                  conv_b=m_conv_b, dt_bias=m_dt_bias, a_log=m_a_log, d_skip=m_d_skip, ssd_norm_w=m_ssd_norm_w,
                  w_ssd_out=m_w_ssd_out, pool_w=m_pool_w, pool_scale=m_pool_scale, w_pool_out=m_w_pool_out,
                  w_out=m_w_out, g_ffn=m_g_ffn, w_gate_up=m_w_gate_up, w_down=m_w_down, g_final=m_g_final)
    moms_v = dict(c_ctx=v_c_ctx, w_ada=v_w_ada, b_ada=v_b_ada, g_mix=v_g_mix, w_in=v_w_in, conv_w=v_conv_w,
                  conv_b=v_conv_b, dt_bias=v_dt_bias, a_log=v_a_log, d_skip=v_d_skip, ssd_norm_w=v_ssd_norm_w,
                  w_ssd_out=v_w_ssd_out, pool_w=v_pool_w, pool_scale=v_pool_scale, w_pool_out=v_w_pool_out,
                  w_out=v_w_out, g_ffn=v_g_ffn, w_gate_up=v_w_gate_up, w_down=v_w_down, g_final=v_g_final)
    order = ["c_ctx", "w_ada", "b_ada", "g_mix", "w_in", "conv_w", "conv_b", "dt_bias", "a_log", "d_skip",
             "ssd_norm_w", "w_ssd_out", "pool_w", "pool_scale", "w_pool_out", "w_out", "g_ffn", "w_gate_up",
             "w_down", "g_final"]
    big = ["w_ada", "w_in", "conv_w", "w_ssd_out", "pool_w", "w_pool_out", "w_out", "w_gate_up", "w_down"]
    small = [n for n in order if n not in big]

    depth = w_in.shape[0]
    L, D = x.shape[1], x.shape[2]
    n_ctx = ctx.shape[1]
    T = n_ctx + L
    in_cols = w_in.shape[2] * N_DEV
    xbc_w = conv_w.shape[2] * N_DEV
    dinner = ssd_norm_w.shape[1]
    H = dt_bias.shape[2]
    G = H // HPG
    GN = G * STATE
    assert xbc_w == dinner + 2 * GN and dinner == H * HEADDIM
    assert in_cols == dinner + xbc_w + 2 * H + D + 2 * D
    assert dinner == 2 * D and GN == D and 2 * H <= 128
    F = w_down.shape[1] * N_DEV
    pg = pool_w.shape[3]
    tr = n_ctx
    assert L % tr == 0 and tr % GRID_W == 0 and tr % CHUNK == 0 and L % CHUNK == 0
    n_ctx_tiles = 1
    NP = _round_up(9 * D + 128, 512)
    gs = dinner // G
    off_xbc, off_dt, off_pool = dinner, dinner + xbc_w, dinner + xbc_w + 2 * H
    off_gate = off_pool + D

    def shard_rows(name, l):
        w = weights[name][l]
        if name in ("w_ada", "w_in", "w_gate_up"):
            return w.T
        if name == "conv_w":
            w8 = _pad_rows(w, 8)
            hi = w8.astype(BF16)
            lo = (w8 - hi.astype(F32)).astype(BF16)
            return jnp.concatenate([hi, lo], axis=0).reshape(-1, D)
        if name == "pool_w":
            return w.reshape(-1, D)
        return w

    shards = [(n, l, shard_rows(n, l)) for l in range(depth) for n in big]
    lay = _Layout(D, [(n, l, a.shape[0]) for n, l, a in shards])
    packed = jnp.concatenate([_pad_rows(a.astype(BF16), lay.find(n, l)[3]) for n, l, a in shards], axis=0)
    gathered = _all_gather(_pad_rows(packed, lay.rows), "gather_weights")

    def full(name, l):
        _, _, rows, _, off = lay.find(name, l)
        return gathered[:, off:off + rows, :]

    def w_inT_new(l):
        w = full("w_in", l).reshape(in_cols, D)
        parts = [w[:off_xbc], w[off_xbc:off_dt], w[off_pool:off_gate], w[off_gate:], w[off_dt:off_pool]]
        return _pad_rows(jnp.concatenate(parts, axis=0), NP)

    xs0 = jnp.concatenate([ctx[0], x[0]], axis=0)
    cc8 = _pad_rows(jnp.concatenate([c, c_ctx[None, :]], axis=0), 8)
    tgt = loss_target[0]

    def vec(a):
        return a.reshape(1, -1)

    def pad128(a):
        return jnp.pad(a.reshape(1, -1), ((0, 0), (0, 128 - 2 * H)))

    def heads16(arr):
        return arr[:, :2 * H].reshape(T, 2, H, 16)

    expand = (jnp.arange(128)[:, None] == jnp.arange(2 * H * HEADDIM)[None, :] // HEADDIM).astype(BF16)

    def from4(arr):
        return jnp.pad(arr.transpose(2, 0, 1, 3).reshape(T, 2 * H), ((0, 0), (0, 128 - 2 * H)))

    dt_blk = (9 * D) // 128
    conv_tc = 128
    ssd_gpb = 2
    saved = []
    xcur = xs0
    for l in range(depth):
        W = dict(
            adaT=full("w_ada", l).reshape(6 * D, D), inT=w_inT_new(l),
            ssd=full("w_ssd_out", l).reshape(dinner, D), po=full("w_pool_out", l).reshape(D, D),
            out=full("w_out", l).reshape(D, D), guT=full("w_gate_up", l).reshape(2 * F, D),
            down=full("w_down", l).reshape(F, D),
            pool=full("pool_w", l).reshape(N_DEV, len(POOL_WINDOWS), pg // N_DEV, pg).transpose(1, 0, 2, 3)
            .reshape(len(POOL_WINDOWS), pg, pg),
        )
        cw = full("conv_w", l).reshape(N_DEV, 16, xbc_w // N_DEV).astype(F32)
        W["conv8"] = (cw[:, :8] + cw[:, 8:]).transpose(1, 0, 2).reshape(8, xbc_w)
        m6 = _ada_fwd(cc8, W["adaT"], vec(b_ada[l]), "ada_fwd")
        h = _norm_mod(xcur, vec(g_mix[l]), m6, 0, 1, n_ctx_tiles, tr, "norm_mod")
        proj = _mm(h, W["inT"], "nt", F32, "mm_in")
        l1, l2, l3, dtb, ein, dte, etot = _dt_prep(proj, dt_blk, pad128(dt_bias[l]), pad128(a_log[l]), expand, H,
                                                   "dt_prep")
        one, zero = jnp.ones_like(l1), jnp.zeros_like(l1)
        lam_a = heads16(jnp.stack([l1, l2, l3, one, one, one] + [zero] * 10, axis=-1)).transpose(1, 2, 0, 3)
        lam_b = heads16(jnp.stack([one, one, one, -l1, -l2, -l3] + [zero] * 10, axis=-1)).transpose(1, 2, 3, 0)
        et = etot.reshape(T // CHUNK, 8, 128)[:, 0, :2 * H].reshape(T // CHUNK, 2, G, HPG).transpose(1, 2, 0, 3)
        etot5 = jnp.pad(jnp.broadcast_to(et[..., None], et.shape + (128,)),
                        ((0, 0), (0, 0), (0, 0), (0, 8 - HPG), (0, 0)))
        dsk = jnp.pad(jnp.repeat(d_skip[l], HEADDIM, axis=1)[:, None, :], ((0, 0), (0, 7), (0, 0)))
        scan_ops = (dtb, ein, dte, lam_a, lam_b, etot5)
        xbc = _conv_fwd(proj, dinner // conv_tc, xbc_w, W["conv8"], vec(conv_b[l]), n_ctx, conv_tc, "conv_fwd")
        y2, states = _ssd_fwd(xbc, *scan_ops, dsk, G, n_ctx, ssd_gpb, "ssd_fwd")
        yn = _gnorm_fwd(y2, proj, vec(ssd_norm_w[l]), gs, tr, "gnorm_fwd")
        pm = _pool_apply(proj, (6 * D) // pg, BF16, n_ctx, tr, pg, False, "pool_fwd")
        pms = _pool_mix_fwd(pm, W["pool"], vec(pool_scale[l]), tr, "pool_mix_fwd")
        o_ssd = _mm(yn, W["ssd"], "nn", F32, "mm_ssd_out")
        o_pool = _mm(pms, W["po"], "nn", F32, "mm_pool_out")
        mg = _merge_fwd(proj, 7, o_ssd, o_pool, tr, "merge_fwd")
        mo = _mm(mg, W["out"], "nn", F32, "mm_out")
        x1, h2 = _norm_mod(xcur, vec(g_ffn[l]), m6, 3, 4, n_ctx_tiles, tr, "resid_norm_mod", resid=(mo, 2))
        gu = _mm(h2, W["guT"], "nt", F32, "mm_gate_up")
        act = _swiglu_fwd(gu, tr, "swiglu_fwd")
        f = _mm(act, W["down"], "nn", F32, "mm_down")
        saved.append(dict(W=W, m6=m6, x0=xcur, h=h, proj=proj, scan_ops=scan_ops, xbc=xbc, y2=y2,
                          states=states, yn=yn, pm=pm, pms=pms, o_ssd=o_ssd, o_pool=o_pool, mg=mg, mo=mo, x1=x1,
                          h2=h2, gu=gu, act=act, f=f))
        xcur = _resid(x1, f, m6, 5, n_ctx_tiles, tr, "resid")

    loss_blk, dx, dgf = _loss_head(xcur, tgt, vec(g_final), n_ctx_tiles, tr, "loss_head")
    loss = lax.psum(loss_blk[0, 0], MESH_AXES)

    big_rows = {}
    small_g = {n: [None] * depth for n in small}
    d_c_ctx = jnp.zeros((D,), F32)
    for l in reversed(range(depth)):
        S = saved[l]
        W, m6, proj = S["W"], S["m6"], S["proj"]
        df, dga2 = _resid_bwd(dx, S["f"], m6, 5, n_ctx_tiles, tr, "resid_bwd")
        dact = _mm(df, W["down"], "nt", F32, "mm_down_dx")
        g_down = _mm(S["act"], df, "tn", BF16,"mm_down_dw")
        dgu = _swiglu_bwd(S["gu"], dact, tr, "swiglu_bwd")
        dh2 = _mm(dgu, W["guT"], "nn", F32, "mm_gate_up_dx")
        g_guT = _mm(dgu, S["h2"], "tn", BF16,"mm_gate_up_dw")
        dx1, st2 = _norm_mod_bwd(S["x1"], dh2, dx, vec(g_ffn[l]), m6, 4, n_ctx_tiles, tr, "norm_mod_bwd")
        dmo, dga1 = _resid_bwd(dx1, S["mo"], m6, 2, n_ctx_tiles, tr, "resid_bwd")
        dmg = _mm(dmo, W["out"], "nt", F32, "mm_out_dx")
        g_out = _mm(S["mg"], dmo, "tn", BF16,"mm_out_dw")
        do_ssd, do_pool, dgl = _merge_bwd(proj, 7, S["o_ssd"], S["o_pool"], dmg, tr, "merge_bwd")
        dyn = _mm(do_ssd, W["ssd"], "nt", F32, "mm_ssd_out_dx")
        g_ssd = _mm(S["yn"], do_ssd, "tn", BF16,"mm_ssd_out_dw")
        dpms = _mm(do_pool, W["po"], "nt", F32, "mm_pool_out_dx")
        g_po = _mm(S["pms"], do_pool, "tn", BF16,"mm_pool_out_dw")
        dpm, g_pool, dps = _pool_mix_bwd(S["pm"], dpms, W["pool"], vec(pool_scale[l]), tr, "pool_mix_bwd")
        dup = _pool_apply(dpm, 0, BF16, n_ctx, tr, pg, True, "pool_bwd")
        dy, dz, dnw = _gnorm_bwd(S["y2"], proj, vec(ssd_norm_w[l]), dyn, gs, tr, "gnorm_bwd")
        dxs2, db2, dc2, ddt4, dlc4, dlr4 = _ssd_bwd(S["xbc"], dy, S["states"], *S["scan_ops"], G, n_ctx, ssd_gpb,
                                                    "ssd_bwd")
        dskv = _pad_rows(jnp.repeat(d_skip[l], HEADDIM, axis=1), 8)
        dxbc_act, ddsk = _ssd_combine(dxs2, db2, dc2, dy, S["xbc"], dskv, tr, "ssd_combine")
        dxbc, dconv = _conv_bwd(proj, dinner // conv_tc, dxbc_act, W["conv8"], vec(conv_b[l]), n_ctx, conv_tc,
                                "conv_bwd")
        ddt_raw, dtst = _dt_bwd(proj, dt_blk, pad128(dt_bias[l]), pad128(a_log[l]), from4(ddt4),
                                from4(dlc4 + dlr4.transpose(0, 1, 3, 2)), H, tr, "dt_bwd")
        dproj = jnp.concatenate([dz, dxbc, dup, dgl, ddt_raw, jnp.zeros((T, NP - 9 * D - 128), BF16)], axis=1)
        dh = _mm(dproj, W["inT"], "nn", F32, "mm_in_dx")
        g_inT_new = _mm(dproj, S["h"], "tn", BF16,"mm_in_dw")
        dx0, st1 = _norm_mod_bwd(S["x0"], dh, dx1, vec(g_mix[l]), m6, 1, n_ctx_tiles, tr, "norm_mod_bwd")
        dm6 = _pad_rows(jnp.concatenate([st1[0:2], st1[2:4], dga1[0:2], st2[0:2], st2[2:4], dga2[0:2]], axis=1), 8)
        dsil = _mm(dm6, W["adaT"], "nn", F32, "mm_ada_dx")
        sil_b, dcc, dbada = _ada_bwd_small(cc8, dsil, dm6, "ada_bwd_small")
        g_adaT = _mm(dm6, sil_b, "tn", BF16,"mm_ada_dw")
        d_c_ctx = d_c_ctx + dcc[1]
        dx = dx0

        g_inT = jnp.concatenate([g_inT_new[:6 * D], g_inT_new[9 * D:9 * D + 2 * H], g_inT_new[6 * D:9 * D]], axis=0)
        big_rows[("w_ada", l)] = g_adaT.reshape(N_DEV, -1, D)
        big_rows[("w_in", l)] = g_inT.reshape(N_DEV, -1, D)
        big_rows[("conv_w", l)] = jnp.pad(
            dconv[:CONV_K].reshape(CONV_K, N_DEV, xbc_w // N_DEV).transpose(1, 0, 2),
            ((0, 0), (0, 16 - CONV_K), (0, 0))).reshape(N_DEV, -1, D)
        big_rows[("w_ssd_out", l)] = g_ssd.reshape(N_DEV, -1, D)
        big_rows[("pool_w", l)] = g_pool.reshape(len(POOL_WINDOWS), N_DEV, pg // N_DEV, pg).transpose(1, 0, 2, 3) \
            .reshape(N_DEV, -1, D)
        big_rows[("w_pool_out", l)] = g_po.reshape(N_DEV, -1, D)
        big_rows[("w_out", l)] = g_out.reshape(N_DEV, -1, D)
        big_rows[("w_gate_up", l)] = g_guT.reshape(N_DEV, -1, D)
        big_rows[("w_down", l)] = g_down.reshape(N_DEV, -1, D)
        small_g["b_ada"][l] = dbada[0]
        small_g["g_mix"][l] = st1[4]
        small_g["conv_b"][l] = dconv[CONV_K]
        small_g["dt_bias"][l] = dtst[0, :2 * H].reshape(2, H)
        small_g["a_log"][l] = dtst[1, :2 * H].reshape(2, H)
        dsk_h = ddsk[0].reshape(H, HEADDIM).sum(axis=-1)
        small_g["d_skip"][l] = jnp.stack([dsk_h, dsk_h])
        small_g["ssd_norm_w"][l] = dnw[0]
        small_g["pool_scale"][l] = dps[0]
        small_g["g_ffn"][l] = st2[4]
    grad_x = dx[n_ctx:][None]

    gparts = [jnp.pad(big_rows[(n, l)].astype(BF16), ((0, 0), (0, pr - rows), (0, 0)))
              for n, l, rows, pr, _ in lay.pieces]
    gparts.append(jnp.zeros((N_DEV, lay.rows - sum(p[3] for p in lay.pieces), D), BF16))
    gbuf = jnp.concatenate(gparts, axis=1)
    gbuf = gbuf.reshape(2, 2, 2, lay.rows, D).transpose(2, 0, 1, 3, 4).reshape(2, 4, lay.rows, D)
    got = _pair_exchange(gbuf, "rs_pair_exchange")
    red = _pair_add(gbuf, got, lax.axis_index("c").astype(jnp.int32).reshape(1), "rs_pair_add")
    slots = _chip_exchange(red, "rs_chip_exchange")
    g_local = _sum_slots(slots, "rs_chip_add")

    def local_grad(name):
        outs = []
        for l in range(depth):
            _, _, rows, _, off = lay.find(name, l)
            piece = g_local[off:off + rows]
            if name in ("w_ada", "w_in", "w_gate_up"):
                piece = piece.T
            elif name == "conv_w":
                piece = piece.reshape(16, -1)[:CONV_K]
            elif name == "pool_w":
                piece = piece.reshape(weights[name].shape[1:])
            outs.append(piece)
        return jnp.stack(outs)

    grads = {n: local_grad(n) for n in big}

    small_full = {"c_ctx": d_c_ctx, "g_final": dgf[0]}
    for n in small:
        if n not in small_full:
            small_full[n] = jnp.stack(small_g[n])

    def pack_small(tree):
        flat = jnp.concatenate([tree[n].reshape(-1).astype(F32) for n in small])
        rows = _round_up(-(-flat.shape[0] // D), 8)
        return jnp.pad(flat, (0, rows * D - flat.shape[0])).reshape(rows, D)

    def unpack_small(buf):
        flat, out, off = buf.reshape(-1), {}, 0
        for n in small:
            sz = weights[n].size
            out[n] = flat[off:off + sz].reshape(weights[n].shape)
            off += sz
        return out

    g_small = _sum_slots(_all_gather(pack_small(small_full), "gather_small_grads"), "sum_small_grads")
    grads.update(unpack_small(g_small))

    delta, new_m, new_v = {}, {}, {}
    for n in big:
        shp = weights[n].shape
        d_, m_, v_ = _adamw(weights[n].reshape(-1, shp[-1]), grads[n].reshape(-1, shp[-1]),
                            moms_m[n].reshape(-1, shp[-1]), moms_v[n].reshape(-1, shp[-1]), "adamw_" + n)
        delta[n], new_m[n], new_v[n] = d_.reshape(shp), m_.reshape(shp), v_.reshape(shp)
    d_, m_, v_ = _adamw(pack_small(weights), g_small, pack_small(moms_m), pack_small(moms_v), "adamw_small")
    delta.update(unpack_small(d_))
    new_m.update(unpack_small(m_))
    new_v.update(unpack_small(v_))

    return (loss, grad_x, *[grads[n] for n in order], *[delta[n] for n in order],
            *[new_m[n] for n in order], *[new_v[n] for n in order])
```

```python
import functools

import jax
import jax.numpy as jnp
from jax import lax
from jax.experimental import pallas as pl
from jax.experimental.pallas import tpu as pltpu

F32 = jnp.float32
BF16 = jnp.bfloat16
N_DEV = 8
EPS = 1e-6
GRID_W = 64
POOL_WINDOWS = (2, 4, 8, 16)
HEADDIM = 64
STATE = 128
CHUNK = 128
HPG = 4
CONV_K = 5
ADAM_LR, ADAM_B1, ADAM_B2, ADAM_EPS, ADAM_WD, ADAM_STEP = 0.001, 0.9, 0.999, 1e-08, 0.01, 10
NEG_BIG = -1e30
MESH_AXES = ("x", "y", "c")
ANY = pl.BlockSpec(memory_space=pl.ANY)


def _tile(n, cands):
    for t in cands:
        if n % t == 0:
            return t
    return n


def _round_up(n, m):
    return -(-n // m) * m


def _silu(x):
    return x * jax.nn.sigmoid(x)


def _dsilu(x):
    s = jax.nn.sigmoid(x)
    return s * (1.0 + x * (1.0 - s))


def _cparams(sem):
    return pltpu.CompilerParams(dimension_semantics=sem, vmem_limit_bytes=56 * 1024 * 1024)


def _mm(a, b, mode, out_dtype, name):
    if mode == "tn":
        K, M = a.shape
        N = b.shape[1]
        tm = _tile(M, (512, 256, 128))
        tn = _tile(N, (1024, 512, 256, 128))
        tk = _tile(K, (1088, 544, 512, 256, 128))
        a_spec = pl.BlockSpec((tk, tm), lambda i, j, k: (k, i))
        b_spec = pl.BlockSpec((tk, tn), lambda i, j, k: (k, j))
        dims = (((0,), (0,)), ((), ()))
    else:
        M, K = a.shape
        N = b.shape[0] if mode == "nt" else b.shape[1]
        tm = _tile(M, (1088, 544, 512, 256, 128))
        tn = _tile(N, (512, 256, 128))
        tk = K if K <= 2048 else _tile(K, (1024, 512, 256, 128))
        a_spec = pl.BlockSpec((tm, tk), lambda i, j, k: (i, k))
        if mode == "nt":
            b_spec = pl.BlockSpec((tn, tk), lambda i, j, k: (j, k))
            dims = (((1,), (1,)), ((), ()))
        else:
            b_spec = pl.BlockSpec((tk, tn), lambda i, j, k: (k, j))
            dims = (((1,), (0,)), ((), ()))
    nk = K // tk

    def body(a_ref, b_ref, o_ref, *acc):
        part = lax.dot_general(a_ref[...].astype(BF16), b_ref[...].astype(BF16), dims, preferred_element_type=F32)
        if nk == 1:
            o_ref[...] = part.astype(o_ref.dtype)
            return
        k = pl.program_id(2)

        @pl.when(k == 0)
        def _():
            acc[0][...] = part

        @pl.when(k > 0)
        def _():
            acc[0][...] += part

        @pl.when(k == nk - 1)
        def _():
            o_ref[...] = acc[0][...].astype(o_ref.dtype)

    return pl.pallas_call(
        body, name=name, grid=(M // tm, N // tn, nk),
        in_specs=[a_spec, b_spec], out_specs=pl.BlockSpec((tm, tn), lambda i, j, k: (i, j)),
        out_shape=jax.ShapeDtypeStruct((M, N), out_dtype),
        scratch_shapes=[pltpu.VMEM((tm, tn), F32)] if nk > 1 else [],
        compiler_params=_cparams(("parallel", "parallel", "arbitrary")),
    )(a, b)


def _ada_fwd(cc8, w_adaT, b_ada, name):
    D = cc8.shape[1]
    N = w_adaT.shape[0]
    tn = _tile(N, (512, 256, 128))

    def body(c_ref, w_ref, b_ref, o_ref):
        a = _silu(c_ref[...]).astype(BF16)
        o_ref[...] = lax.dot_general(a, w_ref[...], (((1,), (1,)), ((), ())),
                                     preferred_element_type=F32) + b_ref[...]

    return pl.pallas_call(
        body, name=name, grid=(N // tn,),
        in_specs=[pl.BlockSpec((8, D), lambda j: (0, 0)), pl.BlockSpec((tn, D), lambda j: (j, 0)),
                  pl.BlockSpec((1, tn), lambda j: (0, j))],
        out_specs=pl.BlockSpec((8, tn), lambda j: (0, j)),
        out_shape=jax.ShapeDtypeStruct((8, N), F32),
        compiler_params=_cparams(("parallel",)),
    )(cc8, w_adaT, b_ada)


def _ada_bwd_small(cc8, dsil, dm6, name):
    D = cc8.shape[1]
    N = dm6.shape[1]

    def body(c_ref, ds_ref, dm_ref, sil_ref, dc_ref, db_ref):
        c = c_ref[...]
        sil_ref[...] = _silu(c).astype(BF16)
        dc_ref[...] = ds_ref[...] * _dsilu(c)
        dm = dm_ref[...]
        row = lax.broadcasted_iota(jnp.int32, dm.shape, 0)
        db_ref[...] = jnp.where(row == 0, jnp.sum(dm, axis=0, keepdims=True), 0.0)

    return pl.pallas_call(
        body, name=name, grid=(1,),
        in_specs=[pl.BlockSpec((8, D), lambda i: (0, 0)), pl.BlockSpec((8, D), lambda i: (0, 0)),
                  pl.BlockSpec((8, N), lambda i: (0, 0))],
        out_specs=[pl.BlockSpec((8, D), lambda i: (0, 0)), pl.BlockSpec((8, D), lambda i: (0, 0)),
                   pl.BlockSpec((8, N), lambda i: (0, 0))],
        out_shape=[jax.ShapeDtypeStruct((8, D), BF16), jax.ShapeDtypeStruct((8, D), F32),
                   jax.ShapeDtypeStruct((8, N), F32)],
        compiler_params=_cparams(("arbitrary",)),
    )(cc8, dsil, dm6)


def _seg_pick(m_ref, is_ctx):
    return jnp.where(is_ctx, m_ref[1:2, :], m_ref[0:1, :])


def _norm_mod(x, g, m6, sh_idx, sc_idx, n_ctx_tiles, tr, name, resid=None):
    T, D = x.shape
    row = pl.BlockSpec((tr, D), lambda i: (i, 0))
    vec = pl.BlockSpec((1, D), lambda i: (0, 0))

    def mcol(idx):
        return pl.BlockSpec((8, D), lambda i: (0, idx))

    def body(*refs):
        if resid is None:
            x_ref, g_ref, sh_ref, sc_ref, h_ref = refs
            xv = x_ref[...]
        else:
            x_ref, f_ref, ga_ref, g_ref, sh_ref, sc_ref, xo_ref, h_ref = refs
        is_ctx = pl.program_id(0) < n_ctx_tiles
        if resid is not None:
            xv = x_ref[...] + _seg_pick(ga_ref, is_ctx) * f_ref[...]
            xo_ref[...] = xv
        rstd = lax.rsqrt(jnp.mean(xv * xv, axis=-1, keepdims=True) + EPS)
        hn = xv * rstd * g_ref[...]
        h_ref[...] = (hn * (1.0 + _seg_pick(sc_ref, is_ctx)) + _seg_pick(sh_ref, is_ctx)).astype(BF16)

    if resid is None:
        ins, in_specs = [x, g, m6, m6], [row, vec, mcol(sh_idx), mcol(sc_idx)]
        out_specs, out_shape = row, jax.ShapeDtypeStruct((T, D), BF16)
    else:
        f, ga_idx = resid
        ins = [x, f, m6, g, m6, m6]
        in_specs = [row, row, mcol(ga_idx), vec, mcol(sh_idx), mcol(sc_idx)]
        out_specs = [row, row]
        out_shape = [jax.ShapeDtypeStruct((T, D), F32), jax.ShapeDtypeStruct((T, D), BF16)]
    return pl.pallas_call(body, name=name, grid=(T // tr,), in_specs=in_specs, out_specs=out_specs,
                          out_shape=out_shape, compiler_params=_cparams(("parallel",)))(*ins)


def _resid(x, f, m6, ga_idx, n_ctx_tiles, tr, name):
    T, D = x.shape
    row = pl.BlockSpec((tr, D), lambda i: (i, 0))

    def body(x_ref, f_ref, ga_ref, o_ref):
        is_ctx = pl.program_id(0) < n_ctx_tiles
        o_ref[...] = x_ref[...] + _seg_pick(ga_ref, is_ctx) * f_ref[...]

    return pl.pallas_call(body, name=name, grid=(T // tr,),
                          in_specs=[row, row, pl.BlockSpec((8, D), lambda i: (0, ga_idx))], out_specs=row,
                          out_shape=jax.ShapeDtypeStruct((T, D), F32),
                          compiler_params=_cparams(("parallel",)))(x, f, m6)


def _resid_bwd(dx, f, m6, ga_idx, n_ctx_tiles, tr, name):
    T, D = dx.shape
    row = pl.BlockSpec((tr, D), lambda i: (i, 0))
    acc = pl.BlockSpec((8, D), lambda i: (0, 0))

    def body(dx_ref, f_ref, ga_ref, df_ref, dga_ref):
        i = pl.program_id(0)
        is_ctx = i < n_ctx_tiles

        @pl.when(i == 0)
        def _():
            dga_ref[...] = jnp.zeros_like(dga_ref)

        dxv = dx_ref[...]
        df_ref[...] = (_seg_pick(ga_ref, is_ctx) * dxv).astype(BF16)
        s = jnp.sum(dxv * f_ref[...], axis=0, keepdims=True)
        r = lax.broadcasted_iota(jnp.int32, (8, D), 0)
        dga_ref[...] += jnp.where(r == jnp.where(is_ctx, 1, 0), s, 0.0)

    return pl.pallas_call(body, name=name, grid=(T // tr,),
                          in_specs=[row, row, pl.BlockSpec((8, D), lambda i: (0, ga_idx))],
                          out_specs=[row, acc],
                          out_shape=[jax.ShapeDtypeStruct((T, D), BF16), jax.ShapeDtypeStruct((8, D), F32)],
                          compiler_params=_cparams(("arbitrary",)))(dx, f, m6)


def _norm_mod_bwd(x, dh, dxres, g, m6, sc_idx, n_ctx_tiles, tr, name):
    T, D = x.shape
    row = pl.BlockSpec((tr, D), lambda i: (i, 0))
    acc = pl.BlockSpec((8, D), lambda i: (0, 0))

    def body(x_ref, dh_ref, dr_ref, g_ref, sc_ref, dx_ref, st_ref):
        i = pl.program_id(0)
        is_ctx = i < n_ctx_tiles

        @pl.when(i == 0)
        def _():
            st_ref[...] = jnp.zeros_like(st_ref)

        xv, dh_v, gv = x_ref[...], dh_ref[...], g_ref[...]
        sc1 = 1.0 + _seg_pick(sc_ref, is_ctx)
        rstd = lax.rsqrt(jnp.mean(xv * xv, axis=-1, keepdims=True) + EPS)
        xhat = xv * rstd
        dxhat = dh_v * sc1 * gv
        dx_ref[...] = dr_ref[...] + rstd * (dxhat - xhat * jnp.mean(dxhat * xhat, axis=-1, keepdims=True))
        dsh = jnp.sum(dh_v, axis=0, keepdims=True)
        dsc = jnp.sum(dh_v * xhat * gv, axis=0, keepdims=True)
        dg = jnp.sum(dh_v * sc1 * xhat, axis=0, keepdims=True)
        r = lax.broadcasted_iota(jnp.int32, (8, D), 0)
        seg = jnp.where(is_ctx, 1, 0)
        st_ref[...] += (jnp.where(r == seg, dsh, 0.0) + jnp.where(r == 2 + seg, dsc, 0.0)
                        + jnp.where(r == 4, dg, 0.0))

    return pl.pallas_call(body, name=name, grid=(T // tr,),
                          in_specs=[row, row, row, pl.BlockSpec((1, D), lambda i: (0, 0)),
                                    pl.BlockSpec((8, D), lambda i: (0, sc_idx))],
                          out_specs=[row, acc],
                          out_shape=[jax.ShapeDtypeStruct((T, D), F32), jax.ShapeDtypeStruct((8, D), F32)],
                          compiler_params=_cparams(("arbitrary",)))(x, dh, dxres, g, m6)


def _loss_head(x, tgt, g, n_ctx_tiles, tr, name):
    T, D = x.shape
    row = pl.BlockSpec((tr, D), lambda i: (i, 0))

    def body(x_ref, t_ref, g_ref, l_ref, dx_ref, dg_ref):
        i = pl.program_id(0)

        @pl.when(i == 0)
        def _():
            l_ref[...] = jnp.zeros_like(l_ref)
            dg_ref[...] = jnp.zeros_like(dg_ref)

        @pl.when(i < n_ctx_tiles)
        def _():
            dx_ref[...] = jnp.zeros_like(dx_ref)

        @pl.when(i >= n_ctx_tiles)
        def _():
            xv, gv = x_ref[...], g_ref[...]
            rstd = lax.rsqrt(jnp.mean(xv * xv, axis=-1, keepdims=True) + EPS)
            xhat = xv * rstd
            e = xhat * gv - t_ref[...]
            l_ref[...] += 0.5 * jnp.sum(jnp.mean(e * e, axis=-1, keepdims=True), axis=0, keepdims=True)
            dy = e * (1.0 / D)
            dxhat = dy * gv
            dx_ref[...] = rstd * (dxhat - xhat * jnp.mean(dxhat * xhat, axis=-1, keepdims=True))
            r = lax.broadcasted_iota(jnp.int32, (8, D), 0)
            dg_ref[...] += jnp.where(r == 0, jnp.sum(dy * xhat, axis=0, keepdims=True), 0.0)

    return pl.pallas_call(
        body, name=name, grid=(T // tr,),
        in_specs=[row, pl.BlockSpec((tr, D), lambda i: (jnp.maximum(i - n_ctx_tiles, 0), 0)),
                  pl.BlockSpec((1, D), lambda i: (0, 0))],
        out_specs=[pl.BlockSpec((8, 128), lambda i: (0, 0)), row, pl.BlockSpec((8, D), lambda i: (0, 0))],
        out_shape=[jax.ShapeDtypeStruct((8, 128), F32), jax.ShapeDtypeStruct((T, D), F32),
                   jax.ShapeDtypeStruct((8, D), F32)],
        compiler_params=_cparams(("arbitrary",)))(x, tgt, g)


def _seq_masks(T, n_ctx, width):
    row = lax.broadcasted_iota(jnp.int32, (T, width), 0)
    in_ctx = row < n_ctx
    return jnp.where(in_ctx, row, row - n_ctx), jnp.where(in_ctx, n_ctx, T - n_ctx)


def _shift_rows(u, off, t_loc, seg_len):
    T = u.shape[0]
    if off == 0:
        return u
    v = pltpu.roll(u, (-off) % T, 0)
    ok = (t_loc + off >= 0) & (t_loc + off < seg_len)
    return jnp.where(ok, v, 0.0)


def _conv_fwd(proj, col0_blk, ncol, conv_w8, conv_b, n_ctx, tc, name):
    T = proj.shape[0]

    def body(u_ref, w_ref, b_ref, o_ref):
        u = u_ref[...]
        t_loc, seg_len = _seq_masks(T, n_ctx, tc)
        acc = jnp.broadcast_to(b_ref[...], u.shape)
        for i in range(CONV_K):
            acc = acc + w_ref[i:i + 1, :] * _shift_rows(u, i - CONV_K // 2, t_loc, seg_len)
        o_ref[...] = _silu(acc)

    return pl.pallas_call(
        body, name=name, grid=(ncol // tc,),
        in_specs=[pl.BlockSpec((T, tc), lambda j: (0, col0_blk + j)), pl.BlockSpec((8, tc), lambda j: (0, j)),
                  pl.BlockSpec((1, tc), lambda j: (0, j))],
        out_specs=pl.BlockSpec((T, tc), lambda j: (0, j)),
        out_shape=jax.ShapeDtypeStruct((T, ncol), F32),
        compiler_params=_cparams(("parallel",)))(proj, conv_w8, conv_b)


def _conv_bwd(proj, col0_blk, dact, conv_w8, conv_b, n_ctx, tc, name):
    T, ncol = dact.shape

    def body(u_ref, d_ref, w_ref, b_ref, du_ref, dw_ref):
        u = u_ref[...]
        t_loc, seg_len = _seq_masks(T, n_ctx, tc)
        pre = jnp.broadcast_to(b_ref[...], u.shape)
        shifted = []
        for i in range(CONV_K):
            s = _shift_rows(u, i - CONV_K // 2, t_loc, seg_len)
            shifted.append(s)
            pre = pre + w_ref[i:i + 1, :] * s
        dpre = d_ref[...] * _dsilu(pre)
        du = jnp.zeros_like(u)
        r = lax.broadcasted_iota(jnp.int32, (8, tc), 0)
        dw = jnp.where(r == CONV_K, jnp.sum(dpre, axis=0, keepdims=True), 0.0)
        for i in range(CONV_K):
            du = du + w_ref[i:i + 1, :] * _shift_rows(dpre, -(i - CONV_K // 2), t_loc, seg_len)
            dw = dw + jnp.where(r == i, jnp.sum(dpre * shifted[i], axis=0, keepdims=True), 0.0)
        du_ref[...] = du.astype(BF16)
        dw_ref[...] = dw

    return pl.pallas_call(
        body, name=name, grid=(ncol // tc,),
        in_specs=[pl.BlockSpec((T, tc), lambda j: (0, col0_blk + j)), pl.BlockSpec((T, tc), lambda j: (0, j)),
                  pl.BlockSpec((8, tc), lambda j: (0, j)), pl.BlockSpec((1, tc), lambda j: (0, j))],
        out_specs=[pl.BlockSpec((T, tc), lambda j: (0, j)), pl.BlockSpec((8, tc), lambda j: (0, j))],
        out_shape=[jax.ShapeDtypeStruct((T, ncol), BF16), jax.ShapeDtypeStruct((8, ncol), F32)],
        compiler_params=_cparams(("parallel",)))(proj, dact, conv_w8, conv_b)


def _pool_core(u, half, t_loc, seg_len, transpose):
    tr = u.shape[0]

    def shift(v, s):
        w = pltpu.roll(v, s % tr, 0)
        ok = (t_loc - s >= 0) & (t_loc - s < seg_len)
        return jnp.where(ok, w, 0.0)

    cnt = (jnp.minimum(t_loc, half) + jnp.minimum(seg_len - t_loc, half)).astype(F32)
    q = u / cnt if transpose else u
    back, ahead, h = q, q, 1
    while h < half:
        back = back + shift(back, h)
        ahead = ahead + shift(ahead, -h)
        h *= 2
    if transpose:
        tot = back + shift(ahead, -1)
        return tot - u
    tot = shift(back, 1) + ahead
    return tot / cnt - u


def _pool_apply(src, col0_blk, out_dtype, n_ctx, tr, pg, transpose, name):
    T = src.shape[0]
    n_ctx_tiles = n_ctx // tr

    def body(u_ref, o_ref):
        i, gi = pl.program_id(0), pl.program_id(1)
        row = lax.broadcasted_iota(jnp.int32, (tr, pg), 0)
        seg_len = jnp.where(i < n_ctx_tiles, tr, GRID_W)
        t_loc = row & (seg_len - 1)
        u = u_ref[...].astype(F32)
        for k_idx, k in enumerate(POOL_WINDOWS):
            @pl.when(gi == k_idx)
            def _(k=k):
                o_ref[...] = _pool_core(u, k // 2, t_loc, seg_len, transpose).astype(o_ref.dtype)

    return pl.pallas_call(
        body, name=name, grid=(T // tr, len(POOL_WINDOWS)),
        in_specs=[pl.BlockSpec((tr, pg), lambda i, gi: (i, col0_blk + gi))],
        out_specs=pl.BlockSpec((tr, pg), lambda i, gi: (i, gi)),
        out_shape=jax.ShapeDtypeStruct((T, pg * len(POOL_WINDOWS)), out_dtype),
        compiler_params=_cparams(("parallel", "parallel")))(src)


def _pool_mix_fwd(pm, pool_w, pool_scale, tr, name):
    T, W = pm.shape
    ng, pg = pool_w.shape[0], pool_w.shape[1]

    def body(p_ref, w_ref, s_ref, o_ref):
        o_ref[...] = (jnp.dot(p_ref[...], w_ref[...], preferred_element_type=F32) * s_ref[...]).astype(BF16)

    return pl.pallas_call(
        body, name=name, grid=(T // tr, ng),
        in_specs=[pl.BlockSpec((tr, pg), lambda i, g: (i, g)), pl.BlockSpec((None, pg, pg), lambda i, g: (g, 0, 0)),
                  pl.BlockSpec((1, pg), lambda i, g: (0, g))],
        out_specs=pl.BlockSpec((tr, pg), lambda i, g: (i, g)),
        out_shape=jax.ShapeDtypeStruct((T, W), BF16),
        compiler_params=_cparams(("parallel", "parallel")))(pm, pool_w, pool_scale)


def _pool_mix_bwd(pm, dpms, pool_w, pool_scale, tr, name):
    T, W = pm.shape
    ng, pg = pool_w.shape[0], pool_w.shape[1]

    def body(p_ref, d_ref, w_ref, s_ref, dp_ref, dw_ref, ds_ref):
        i = pl.program_id(1)

        @pl.when(i == 0)
        def _():
            dw_ref[...] = jnp.zeros_like(dw_ref)
            ds_ref[...] = jnp.zeros_like(ds_ref)

        p, w = p_ref[...], w_ref[...]
        d = d_ref[...].astype(F32)
        pmix = jnp.dot(p, w, preferred_element_type=F32)
        r = lax.broadcasted_iota(jnp.int32, (8, pg), 0)
        ds_ref[...] += jnp.where(r == 0, jnp.sum(d * pmix, axis=0, keepdims=True), 0.0)
        dmix = (d * s_ref[...]).astype(BF16)
        dp_ref[...] = lax.dot_general(dmix, w, (((1,), (1,)), ((), ())), preferred_element_type=F32)
        dw_ref[...] += lax.dot_general(p, dmix, (((0,), (0,)), ((), ())), preferred_element_type=F32)

    return pl.pallas_call(
        body, name=name, grid=(ng, T // tr),
        in_specs=[pl.BlockSpec((tr, pg), lambda g, i: (i, g)), pl.BlockSpec((tr, pg), lambda g, i: (i, g)),
                  pl.BlockSpec((None, pg, pg), lambda g, i: (g, 0, 0)), pl.BlockSpec((1, pg), lambda g, i: (0, g))],
        out_specs=[pl.BlockSpec((tr, pg), lambda g, i: (i, g)), pl.BlockSpec((None, pg, pg), lambda g, i: (g, 0, 0)),
                   pl.BlockSpec((8, pg), lambda g, i: (0, g))],
        out_shape=[jax.ShapeDtypeStruct((T, W), F32), jax.ShapeDtypeStruct((ng, pg, pg), F32),
                   jax.ShapeDtypeStruct((8, W), F32)],
        compiler_params=_cparams(("parallel", "arbitrary")))(pm, dpms, pool_w, pool_scale)


def _chunk_cumsum(v, upper):
    Q = v.shape[0]
    ii = lax.broadcasted_iota(jnp.int32, (Q, Q), 0)
    jj = lax.broadcasted_iota(jnp.int32, (Q, Q), 1)
    tri = ((jj >= ii) if upper else (jj <= ii)).astype(BF16)
    h1 = v.astype(BF16)
    r1 = v - h1.astype(F32)
    h2 = r1.astype(BF16)
    h3 = (r1 - h2.astype(F32)).astype(BF16)
    return (jnp.dot(tri, h1, preferred_element_type=F32) + jnp.dot(tri, h2, preferred_element_type=F32)
            + jnp.dot(tri, h3, preferred_element_type=F32))


def _split3(v):
    h1 = v.astype(BF16)
    r1 = v - h1.astype(F32)
    h2 = r1.astype(BF16)
    return h1, h2, (r1 - h2.astype(F32)).astype(BF16)


def _dt_prep(proj, dt_blk, bias, a_log, expand, n_heads, name):
    T = proj.shape[0]
    Wd = expand.shape[1]
    Q = CHUNK
    row = pl.BlockSpec((Q, 128), lambda i: (i, 0))
    wide = pl.BlockSpec((Q, Wd), lambda i: (i, 0))

    def body(r_ref, b_ref, al_ref, e_ref, l1_ref, l2_ref, l3_ref, dtb_ref, ein_ref, dte_ref, etot_ref):
        xv = r_ref[...] + b_ref[...]
        dt = jnp.maximum(xv, 0.0) + jnp.log(1.0 + jnp.exp(-jnp.abs(xv)))
        a = -jnp.exp(al_ref[...]) * dt
        fwd_col = lax.broadcasted_iota(jnp.int32, (Q, 128), 1) < n_heads
        lam = jnp.where(fwd_col, _chunk_cumsum(a, False), _chunk_cumsum(a, True))
        tot = jnp.where(fwd_col[0:1], lam[Q - 1:Q], lam[0:1])
        l1_ref[...], l2_ref[...], l3_ref[...] = _split3(lam)
        etot_ref[...] = jnp.broadcast_to(jnp.exp(tot), (8, 128))
        ex = e_ref[...]

        def rep(v):
            p1, p2, p3 = _split3(v)
            return (jnp.dot(p1, ex, preferred_element_type=F32) + jnp.dot(p2, ex, preferred_element_type=F32)
                    + jnp.dot(p3, ex, preferred_element_type=F32))

        dtb_ref[...] = rep(dt)
        ein_ref[...] = rep(jnp.exp(lam))
        dte_ref[...] = rep(jnp.exp(tot - lam))

    vec = pl.BlockSpec((1, 128), lambda i: (0, 0))
    return pl.pallas_call(
        body, name=name, grid=(T // Q,),
        in_specs=[pl.BlockSpec((Q, 128), lambda i: (i, dt_blk)), vec, vec, pl.BlockSpec((128, Wd), lambda i: (0, 0))],
        out_specs=[row, row, row, wide, wide, wide, pl.BlockSpec((8, 128), lambda i: (i, 0))],
        out_shape=[jax.ShapeDtypeStruct((T, 128), BF16)] * 3 + [jax.ShapeDtypeStruct((T, Wd), F32)] * 3
        + [jax.ShapeDtypeStruct((T // Q * 8, 128), F32)],
        compiler_params=_cparams(("parallel",)))(proj, bias, a_log, expand)


def _dt_bwd(proj, dt_blk, bias, a_log, ddt, dlam, n_heads, tr, name):
    T = proj.shape[0]
    row = pl.BlockSpec((tr, 128), lambda i: (i, 0))
    vec = pl.BlockSpec((1, 128), lambda i: (0, 0))

    def body(r_ref, b_ref, al_ref, ddt_ref, dl_ref, o_ref, st_ref):
        @pl.when(pl.program_id(0) == 0)
        def _():
            st_ref[...] = jnp.zeros_like(st_ref)

        xv = r_ref[...] + b_ref[...]
        dt = jnp.maximum(xv, 0.0) + jnp.log(1.0 + jnp.exp(-jnp.abs(xv)))
        a_neg = -jnp.exp(al_ref[...])
        col = lax.broadcasted_iota(jnp.int32, (CHUNK, 128), 1)
        dl = dl_ref[...]
        parts = []
        for k in range(tr // CHUNK):
            dk = dl[k * CHUNK:(k + 1) * CHUNK]
            parts.append(jnp.where(col < n_heads, _chunk_cumsum(dk, True), _chunk_cumsum(dk, False)))
        dav = jnp.concatenate(parts, axis=0)
        draw = (ddt_ref[...] + dav * a_neg) * jax.nn.sigmoid(xv)
        o_ref[...] = draw.astype(BF16)
        r = lax.broadcasted_iota(jnp.int32, (8, 128), 0)
        st_ref[...] += (jnp.where(r == 0, jnp.sum(draw, axis=0, keepdims=True), 0.0)
                        + jnp.where(r == 1, jnp.sum(dav * dt, axis=0, keepdims=True) * a_neg, 0.0))

    return pl.pallas_call(
        body, name=name, grid=(T // tr,),
        in_specs=[pl.BlockSpec((tr, 128), lambda i: (i, dt_blk)), vec, vec, row, row],
        out_specs=[row, pl.BlockSpec((8, 128), lambda i: (0, 0))],
        out_shape=[jax.ShapeDtypeStruct((T, 128), BF16), jax.ShapeDtypeStruct((8, 128), F32)],
        compiler_params=_cparams(("arbitrary",)))(proj, bias, a_log, ddt, dlam)


def _scan_chunk(d, pos, nc_ctx, nc):
    rev = jnp.where(pos < nc_ctx, nc_ctx - 1 - pos, nc - 1 - (pos - nc_ctx))
    return jnp.where(d == 0, pos, rev)


def _chunk_mask(d):
    ii = lax.broadcasted_iota(jnp.int32, (CHUNK, CHUNK), 0)
    jj = lax.broadcasted_iota(jnp.int32, (CHUNK, CHUNK), 1)
    return (ii - jj) * jnp.where(d == 0, 1, -1) >= 0


def _ssd_specs(T, G, n_ctx, gpb, chunk_of):
    R, P, N, Q = HPG, HEADDIM, STATE, CHUNK
    H = G * R
    nc, nc_ctx = T // Q, n_ctx // Q
    xw, bw = gpb * R * P, gpb * N
    b_blk0 = (H * P) // bw
    c_blk0 = b_blk0 + G // gpb

    def ch(d, s):
        return chunk_of(d, s, nc_ctx, nc)

    return dict(
        x=pl.BlockSpec((Q, xw), lambda d, g, s: (ch(d, s), g)),
        b=pl.BlockSpec((Q, bw), lambda d, g, s: (ch(d, s), b_blk0 + g)),
        c=pl.BlockSpec((Q, bw), lambda d, g, s: (ch(d, s), c_blk0 + g)),
        col=pl.BlockSpec((None, gpb, Q, R), lambda d, g, s: (d, g, ch(d, s), 0)),
        row=pl.BlockSpec((None, gpb, R, Q), lambda d, g, s: (d, g, 0, ch(d, s))),
        rep=pl.BlockSpec((Q, xw), lambda d, g, s: (ch(d, s), d * (G // gpb) + g)),
        lam_a=pl.BlockSpec((None, gpb * R, 16, Q), lambda d, g, s: (d, g, 0, ch(d, s))),
        lam_b=pl.BlockSpec((None, gpb * R, 16, Q), lambda d, g, s: (d, g, 0, ch(d, s))),
        etot=pl.BlockSpec((None, gpb, None, 8, 128), lambda d, g, s: (d, g, ch(d, s), 0, 0)),
        dsk=pl.BlockSpec((None, 8, xw), lambda d, g, s: (d, 0, g)),
        xd=pl.BlockSpec((None, Q, xw), lambda d, g, s: (d, ch(d, s), g)),
        bd=pl.BlockSpec((None, Q, bw), lambda d, g, s: (d, ch(d, s), g)),
        st=pl.BlockSpec((None, None, gpb * R // 2, 2 * P, N), lambda d, g, s: (d, ch(d, s), g, 0, 0)),
    )


def _ssd_fwd(xbc, dtb, ein, dte, lam_a, lam_b, etot, dsk, G, n_ctx, gpb, name):
    T = xbc.shape[0]
    R, P, N, Q = HPG, HEADDIM, STATE, CHUNK
    H = G * R
    nc = T // Q
    sp = _ssd_specs(T, G, n_ctx, gpb, _scan_chunk)

    def body(x_ref, b_ref, c_ref, dt_ref, ein_ref, dte_ref, la_ref, lb_ref, et_ref, dsk_ref, y_ref, st_ref, S):
        d, s = pl.program_id(0), pl.program_id(2)

        @pl.when(s == 0)
        def _():
            S[...] = jnp.zeros_like(S)

        mask = _chunk_mask(d)
        head0 = lax.broadcasted_iota(jnp.int32, (Q, 2 * P), 1) < P
        rows0 = lax.broadcasted_iota(jnp.int32, (2 * P, N), 0) < P
        for gg in range(gpb):
            Bm = b_ref[:, gg * N:(gg + 1) * N].astype(BF16)
            Cm = c_ref[:, gg * N:(gg + 1) * N].astype(BF16)
            Gm = lax.dot_general(Cm, Bm, (((1,), (1,)), ((), ())), preferred_element_type=F32)
            for k in range(R // 2):
                pk = gg * (R // 2) + k
                sl = slice(pk * 2 * P, (pk + 1) * 2 * P)
                xp = x_ref[:, sl]
                xc = xp * dt_ref[:, sl]
                s_in = S[pk]
                y = lax.dot_general(Cm, s_in.astype(BF16), (((1,), (1,)), ((), ())),
                                    preferred_element_type=F32) * ein_ref[:, sl] + dsk_ref[0:1, sl] * xp
                for j in range(2):
                    hr = 2 * pk + j
                    diff = lax.dot_general(la_ref[hr], lb_ref[hr], (((0,), (0,)), ((), ())), preferred_element_type=F32)
                    ldec = jnp.exp(jnp.where(mask, diff, NEG_BIG))
                    xc_j = (jnp.where(head0, xc, 0.0) if j == 0 else jnp.where(head0, 0.0, xc)).astype(BF16)
                    y = y + jnp.dot((Gm * ldec).astype(BF16), xc_j, preferred_element_type=F32)
                y_ref[:, sl] = y
                st_ref[pk] = s_in
                e_all = jnp.where(rows0, et_ref[gg, 2 * k:2 * k + 1, :], et_ref[gg, 2 * k + 1:2 * k + 2, :])
                xd = (xc * dte_ref[:, sl]).astype(BF16)
                S[pk] = e_all * s_in + lax.dot_general(xd, Bm, (((0,), (0,)), ((), ())),
                                                       preferred_element_type=F32)

    return pl.pallas_call(
        body, name=name, grid=(2, G // gpb, nc),
        in_specs=[sp["x"], sp["b"], sp["c"], sp["rep"], sp["rep"], sp["rep"], sp["lam_a"], sp["lam_b"], sp["etot"],
                  sp["dsk"]],
        out_specs=[sp["xd"], sp["st"]],
        out_shape=[jax.ShapeDtypeStruct((2, T, H * P), F32), jax.ShapeDtypeStruct((2, nc, H // 2, 2 * P, N), F32)],
        scratch_shapes=[pltpu.VMEM((gpb * R // 2, 2 * P, N), F32)],
        compiler_params=_cparams(("parallel", "parallel", "arbitrary")))(
            xbc, xbc, xbc, dtb, ein, dte, lam_a, lam_b, etot, dsk)


def _ssd_bwd(xbc, dy, states, dtb, ein, dte, lam_a, lam_b, etot, G, n_ctx, gpb, name):
    T = xbc.shape[0]
    R, P, N, Q = HPG, HEADDIM, STATE, CHUNK
    H = G * R
    nc = T // Q
    sp = _ssd_specs(T, G, n_ctx, gpb, lambda d, s, nc_ctx, n: _scan_chunk(d, n - 1 - s, nc_ctx, n))

    def body(x_ref, b_ref, c_ref, dy_ref, st_ref, dt_ref, ein_ref, dte_ref, la_ref, lb_ref, et_ref,
             dx_ref, db_ref, dc_ref, ddt_ref, dlc_ref, dlr_ref, dS):
        d, s = pl.program_id(0), pl.program_id(2)

        @pl.when(s == 0)
        def _():
            dS[...] = jnp.zeros_like(dS)

        mask = _chunk_mask(d)
        ri = lax.broadcasted_iota(jnp.int32, (Q, 1), 0)
        is_last = ri == jnp.where(d == 0, Q - 1, 0)
        head0 = lax.broadcasted_iota(jnp.int32, (Q, 2 * P), 1) < P
        rows0 = lax.broadcasted_iota(jnp.int32, (2 * P, N), 0) < P

        def total(v):
            return jnp.sum(jnp.sum(v, axis=1, keepdims=True), axis=0, keepdims=True)

        for gg in range(gpb):
            Bm = b_ref[:, gg * N:(gg + 1) * N].astype(BF16)
            Cm = c_ref[:, gg * N:(gg + 1) * N].astype(BF16)
            Gm = lax.dot_general(Cm, Bm, (((1,), (1,)), ((), ())), preferred_element_type=F32)
            dG = jnp.zeros((Q, Q), F32)
            dB = jnp.zeros((Q, N), F32)
            dC = jnp.zeros((Q, N), F32)
            for k in range(R // 2):
                pk = gg * (R // 2) + k
                sl = slice(pk * 2 * P, (pk + 1) * 2 * P)
                e_in = ein_ref[:, sl]
                dte = dte_ref[:, sl]
                e_all = jnp.where(rows0, et_ref[gg, 2 * k:2 * k + 1, :], et_ref[gg, 2 * k + 1:2 * k + 2, :])
                xp = x_ref[:, sl]
                dtp = dt_ref[:, sl]
                xc = xp * dtp
                xc_b = xc.astype(BF16)
                dyp = dy_ref[:, sl]
                s_in = st_ref[pk]
                s_in_b = s_in.astype(BF16)
                ds_out = dS[pk]
                ds_out_b = ds_out.astype(BF16)
                y_int = lax.dot_general(Cm, s_in_b, (((1,), (1,)), ((), ())), preferred_element_type=F32) * e_in
                b_ds = lax.dot_general(Bm, ds_out_b, (((1,), (1,)), ((), ())), preferred_element_type=F32)
                dxc = dte * b_ds
                u = xc * dxc
                v = dyp * y_int - u
                sse = ds_out * s_in * e_all
                for j in range(2):
                    hr, r = 2 * pk + j, 2 * k + j
                    def pick(a, m0=head0, j=j):
                        return jnp.where(m0, a, 0.0) if j == 0 else jnp.where(m0, 0.0, a)

                    diff = lax.dot_general(la_ref[hr], lb_ref[hr], (((0,), (0,)), ((), ())), preferred_element_type=F32)
                    ldec = jnp.exp(jnp.where(mask, diff, NEG_BIG))
                    dy_j = pick(dyp).astype(BF16)
                    dM = lax.dot_general(dy_j, xc_b, (((1,), (1,)), ((), ())), preferred_element_type=F32)
                    dMl = dM * ldec
                    Wm = dMl * Gm
                    dlam_c = jnp.sum(Wm, axis=1, keepdims=True) + jnp.sum(pick(v), axis=1, keepdims=True)
                    last = total(pick(sse, rows0)) + total(pick(u))
                    dlc_ref[gg, :, r:r + 1] = dlam_c + jnp.where(is_last, last, 0.0)
                    dlr_ref[gg, r:r + 1, :] = -jnp.sum(Wm, axis=0, keepdims=True)
                    dxc = dxc + lax.dot_general((Gm * ldec).astype(BF16), dy_j, (((0,), (0,)), ((), ())),
                                                preferred_element_type=F32)
                    dG = dG + dMl
                dx_ref[:, sl] = dxc * dtp
                t = dxc * xp
                ddt_ref[gg, :, 2 * k:2 * k + 1] = jnp.sum(jnp.where(head0, t, 0.0), axis=1, keepdims=True)
                ddt_ref[gg, :, 2 * k + 1:2 * k + 2] = jnp.sum(jnp.where(head0, 0.0, t), axis=1, keepdims=True)
                edy_b = (e_in * dyp).astype(BF16)
                dC = dC + jnp.dot(edy_b, s_in_b, preferred_element_type=F32)
                dB = dB + jnp.dot((dte * xc).astype(BF16), ds_out_b, preferred_element_type=F32)
                dS[pk] = e_all * ds_out + lax.dot_general(edy_b, Cm, (((0,), (0,)), ((), ())),
                                                          preferred_element_type=F32)
            dG_b = dG.astype(BF16)
            dc_ref[:, gg * N:(gg + 1) * N] = dC + jnp.dot(dG_b, Bm, preferred_element_type=F32)
            db_ref[:, gg * N:(gg + 1) * N] = dB + lax.dot_general(dG_b, Cm, (((0,), (0,)), ((), ())),
                                                                  preferred_element_type=F32)

    return pl.pallas_call(
        body, name=name, grid=(2, G // gpb, nc),
        in_specs=[sp["x"], sp["b"], sp["c"], sp["x"], sp["st"], sp["rep"], sp["rep"], sp["rep"], sp["lam_a"],
                  sp["lam_b"], sp["etot"]],
        out_specs=[sp["xd"], sp["bd"], sp["bd"], sp["col"], sp["col"], sp["row"]],
        out_shape=[jax.ShapeDtypeStruct((2, T, H * P), F32), jax.ShapeDtypeStruct((2, T, G * N), F32),
                   jax.ShapeDtypeStruct((2, T, G * N), F32), jax.ShapeDtypeStruct((2, G, T, R), F32),
                   jax.ShapeDtypeStruct((2, G, T, R), F32), jax.ShapeDtypeStruct((2, G, R, T), F32)],
        scratch_shapes=[pltpu.VMEM((gpb * R // 2, 2 * P, N), F32)],
        compiler_params=_cparams(("parallel", "parallel", "arbitrary")))(
            xbc, xbc, xbc, dy, states, dtb, ein, dte, lam_a, lam_b, etot)


def _ssd_combine(dxs2, db2, dc2, dy, xbc, dskv, tr, name):
    T, HP = dy.shape
    GN = db2.shape[2]
    ncol = HP + 2 * GN
    tc = _tile(GN, (512, 256, 128))
    nx, nb = HP // tc, GN // tc

    def body(dx_ref, db_ref, dc_ref, dy_ref, x_ref, k_ref, o_ref, dk_ref):
        i, j = pl.program_id(1), pl.program_id(0)

        @pl.when((i == 0) & (j < nx))
        def _():
            dk_ref[...] = jnp.zeros_like(dk_ref)

        @pl.when(j < nx)
        def _():
            dyv = dy_ref[...]
            o_ref[...] = dx_ref[0] + dx_ref[1] + (k_ref[0:1, :] + k_ref[1:2, :]) * dyv
            r = lax.broadcasted_iota(jnp.int32, (8, tc), 0)
            dk_ref[...] += jnp.where(r < 2, jnp.sum(dyv * x_ref[...], axis=0, keepdims=True), 0.0)

        @pl.when((j >= nx) & (j < nx + nb))
        def _():
            o_ref[...] = db_ref[0] + db_ref[1]

        @pl.when(j >= nx + nb)
        def _():
            o_ref[...] = dc_ref[0] + dc_ref[1]

    def cl(j, lo, n):
        return jnp.clip(j - lo, 0, n - 1)

    return pl.pallas_call(
        body, name=name, grid=(ncol // tc, T // tr),
        in_specs=[pl.BlockSpec((2, tr, tc), lambda j, i: (0, i, cl(j, 0, nx))),
                  pl.BlockSpec((2, tr, tc), lambda j, i: (0, i, cl(j, nx, nb))),
                  pl.BlockSpec((2, tr, tc), lambda j, i: (0, i, cl(j, nx + nb, nb))),
                  pl.BlockSpec((tr, tc), lambda j, i: (i, cl(j, 0, nx))),
                  pl.BlockSpec((tr, tc), lambda j, i: (i, cl(j, 0, nx))),
                  pl.BlockSpec((8, tc), lambda j, i: (0, cl(j, 0, nx)))],
        out_specs=[pl.BlockSpec((tr, tc), lambda j, i: (i, j)), pl.BlockSpec((8, tc), lambda j, i: (0, cl(j, 0, nx)))],
        out_shape=[jax.ShapeDtypeStruct((T, ncol), F32), jax.ShapeDtypeStruct((8, HP), F32)],
        compiler_params=_cparams(("arbitrary", "arbitrary")))(dxs2, db2, dc2, dy, xbc, dskv)


def _gnorm_fwd(y2, proj, w, gs, tr, name):
    T, HP = y2.shape[1], y2.shape[2]

    def body(y_ref, z_ref, w_ref, o_ref):
        yz = (y_ref[0] + y_ref[1]) * _silu(z_ref[...])
        for g in range(HP // gs):
            v = yz[:, g * gs:(g + 1) * gs]
            rstd = lax.rsqrt(jnp.mean(v * v, axis=-1, keepdims=True) + EPS)
            o_ref[:, g * gs:(g + 1) * gs] = (v * rstd * w_ref[:, g * gs:(g + 1) * gs]).astype(BF16)

    return pl.pallas_call(
        body, name=name, grid=(T // tr,),
        in_specs=[pl.BlockSpec((2, tr, HP), lambda i: (0, i, 0)), pl.BlockSpec((tr, HP), lambda i: (i, 0)),
                  pl.BlockSpec((1, HP), lambda i: (0, 0))],
        out_specs=pl.BlockSpec((tr, HP), lambda i: (i, 0)),
        out_shape=jax.ShapeDtypeStruct((T, HP), BF16),
        compiler_params=_cparams(("parallel",)))(y2, proj, w)


def _gnorm_bwd(y2, proj, w, dyn, gs, tr, name):
    T, HP = y2.shape[1], y2.shape[2]
    row = pl.BlockSpec((tr, HP), lambda i: (i, 0))

    def body(y_ref, z_ref, w_ref, d_ref, dy_ref, dz_ref, dw_ref):
        @pl.when(pl.program_id(0) == 0)
        def _():
            dw_ref[...] = jnp.zeros_like(dw_ref)

        yv = y_ref[0] + y_ref[1]
        zv = z_ref[...]
        sz = _silu(zv)
        yz = yv * sz
        dv = d_ref[...]
        r = lax.broadcasted_iota(jnp.int32, (8, gs), 0)
        for g in range(HP // gs):
            sl = slice(g * gs, (g + 1) * gs)
            v = yz[:, sl]
            rstd = lax.rsqrt(jnp.mean(v * v, axis=-1, keepdims=True) + EPS)
            xhat = v * rstd
            dyn_g = dv[:, sl]
            dhat = dyn_g * w_ref[:, sl]
            dyz = rstd * (dhat - xhat * jnp.mean(dhat * xhat, axis=-1, keepdims=True))
            dy_ref[:, sl] = dyz * sz[:, sl]
            dz_ref[:, sl] = (dyz * yv[:, sl] * _dsilu(zv[:, sl])).astype(BF16)
            dw_ref[:, sl] += jnp.where(r == 0, jnp.sum(dyn_g * xhat, axis=0, keepdims=True), 0.0)

    return pl.pallas_call(
        body, name=name, grid=(T // tr,),
        in_specs=[pl.BlockSpec((2, tr, HP), lambda i: (0, i, 0)), row, pl.BlockSpec((1, HP), lambda i: (0, 0)), row],
        out_specs=[row, row, pl.BlockSpec((8, HP), lambda i: (0, 0))],
        out_shape=[jax.ShapeDtypeStruct((T, HP), F32), jax.ShapeDtypeStruct((T, HP), BF16),
                   jax.ShapeDtypeStruct((8, HP), F32)],
        compiler_params=_cparams(("arbitrary",)))(y2, proj, w, dyn)


def _merge_fwd(proj, g1_blk, o_ssd, o_pool, tr, name):
    T, D = o_ssd.shape
    row = pl.BlockSpec((tr, D), lambda i: (i, 0))

    def body(g1_ref, g2_ref, a_ref, b_ref, o_ref):
        o_ref[...] = (jax.nn.sigmoid(g1_ref[...]) * a_ref[...]
                      + jax.nn.sigmoid(g2_ref[...]) * b_ref[...]).astype(BF16)

    return pl.pallas_call(
        body, name=name, grid=(T // tr,),
        in_specs=[pl.BlockSpec((tr, D), lambda i: (i, g1_blk)), pl.BlockSpec((tr, D), lambda i: (i, g1_blk + 1)),
                  row, row],
        out_specs=row, out_shape=jax.ShapeDtypeStruct((T, D), BF16),
        compiler_params=_cparams(("parallel",)))(proj, proj, o_ssd, o_pool)


def _merge_bwd(proj, g1_blk, o_ssd, o_pool, dmg, tr, name):
    T, D = o_ssd.shape
    row = pl.BlockSpec((tr, D), lambda i: (i, 0))

    def body(g1_ref, g2_ref, a_ref, b_ref, d_ref, da_ref, db_ref, dg_ref):
        s1, s2 = jax.nn.sigmoid(g1_ref[...]), jax.nn.sigmoid(g2_ref[...])
        dv = d_ref[...]
        da_ref[...] = (s1 * dv).astype(BF16)
        db_ref[...] = (s2 * dv).astype(BF16)
        dg_ref[:, :D] = (dv * a_ref[...] * s1 * (1.0 - s1)).astype(BF16)
        dg_ref[:, D:] = (dv * b_ref[...] * s2 * (1.0 - s2)).astype(BF16)

    return pl.pallas_call(
        body, name=name, grid=(T // tr,),
        in_specs=[pl.BlockSpec((tr, D), lambda i: (i, g1_blk)), pl.BlockSpec((tr, D), lambda i: (i, g1_blk + 1)),
                  row, row, row],
        out_specs=[row, row, pl.BlockSpec((tr, 2 * D), lambda i: (i, 0))],
        out_shape=[jax.ShapeDtypeStruct((T, D), BF16), jax.ShapeDtypeStruct((T, D), BF16),
                   jax.ShapeDtypeStruct((T, 2 * D), BF16)],
        compiler_params=_cparams(("parallel",)))(proj, proj, o_ssd, o_pool, dmg)


def _swiglu_fwd(gu, tr, name):
    T, F2 = gu.shape
    F = F2 // 2
    tc = _tile(F, (1408, 768, 512, 256, 128))
    nb = F // tc

    def body(a_ref, b_ref, o_ref):
        o_ref[...] = (_silu(a_ref[...]) * b_ref[...]).astype(BF16)

    return pl.pallas_call(
        body, name=name, grid=(T // tr, nb),
        in_specs=[pl.BlockSpec((tr, tc), lambda i, j: (i, j)), pl.BlockSpec((tr, tc), lambda i, j: (i, nb + j))],
        out_specs=pl.BlockSpec((tr, tc), lambda i, j: (i, j)),
        out_shape=jax.ShapeDtypeStruct((T, F), BF16),
        compiler_params=_cparams(("parallel", "parallel")))(gu, gu)


def _swiglu_bwd(gu, dact, tr, name):
    T, F2 = gu.shape
    F = F2 // 2
    tc = _tile(F, (1408, 768, 512, 256, 128))
    nb = F // tc

    def body(a_ref, b_ref, d_ref, o_ref):
        is_a = pl.program_id(1) < nb
        av, bv, dv = a_ref[...], b_ref[...], d_ref[...]
        o_ref[...] = jnp.where(is_a, dv * bv * _dsilu(av), dv * _silu(av)).astype(BF16)

    return pl.pallas_call(
        body, name=name, grid=(T // tr, 2 * nb),
        in_specs=[pl.BlockSpec((tr, tc), lambda i, j: (i, j % nb)), pl.BlockSpec((tr, tc), lambda i, j: (i, nb + j % nb)),
                  pl.BlockSpec((tr, tc), lambda i, j: (i, j % nb))],
        out_specs=pl.BlockSpec((tr, tc), lambda i, j: (i, j)),
        out_shape=jax.ShapeDtypeStruct((T, F2), BF16),
        compiler_params=_cparams(("parallel", "parallel")))(gu, gu, dact)


def _adamw(w, g, m, v, name):
    Rr, C = w.shape
    tr = _tile(Rr, (256, 128, 64, 32, 16, 8))
    row = pl.BlockSpec((tr, C), lambda i: (i, 0))

    def body(w_ref, g_ref, m_ref, v_ref, d_ref, mo_ref, vo_ref):
        gv = g_ref[...]
        mn = ADAM_B1 * m_ref[...] + (1.0 - ADAM_B1) * gv
        vn = ADAM_B2 * v_ref[...] + (1.0 - ADAM_B2) * (gv * gv)
        m_hat = mn / (1.0 - ADAM_B1 ** ADAM_STEP)
        v_hat = vn / (1.0 - ADAM_B2 ** ADAM_STEP)
        d_ref[...] = -ADAM_LR * (m_hat / (jnp.sqrt(v_hat) + ADAM_EPS) + ADAM_WD * w_ref[...])
        mo_ref[...] = mn
        vo_ref[...] = vn

    sds = jax.ShapeDtypeStruct((Rr, C), F32)
    return pl.pallas_call(body, name=name, grid=(Rr // tr,), in_specs=[row] * 4, out_specs=[row] * 3,
                          out_shape=[sds] * 3, compiler_params=_cparams(("parallel",)))(w, g, m, v)


def _sum_slots(x, name):
    n, Rr, C = x.shape
    tr = _tile(Rr, (512, 256, 128, 64, 32, 16, 8))

    def body(x_ref, o_ref):
        acc = x_ref[0].astype(F32)
        for k in range(1, n):
            acc = acc + x_ref[k].astype(F32)
        o_ref[...] = acc

    return pl.pallas_call(body, name=name, grid=(Rr // tr,),
                          in_specs=[pl.BlockSpec((n, tr, C), lambda i: (0, i, 0))],
                          out_specs=pl.BlockSpec((tr, C), lambda i: (i, 0)),
                          out_shape=jax.ShapeDtypeStruct((Rr, C), F32),
                          compiler_params=_cparams(("parallel",)))(x)


def _place():
    return lax.axis_index("x"), lax.axis_index("y"), lax.axis_index("c")


def _all_gather(x, name):
    Rr, C = x.shape

    def body(x_ref, out_ref, send_sems, recv_sems, local_sem):
        mx, my, mc = _place()
        me, sibling = (mx, my, mc), (mx, my, 1 - mc)
        chips = [(1 - mx, my), (mx, 1 - my), (1 - mx, 1 - my)]

        def slot(px, py, pc):
            return out_ref.at[4 * px + 2 * py + pc]

        def copy(k, block, to, src=None):
            return pltpu.make_async_remote_copy(
                src_ref=slot(*block) if src is None else src, dst_ref=slot(*block),
                send_sem=send_sems.at[k], recv_sem=recv_sems.at[k],
                device_id=to, device_id_type=pl.DeviceIdType.MESH)

        mine = pltpu.make_async_copy(x_ref, slot(*me), local_sem)
        mine.start()
        first = [copy(0, me, sibling, src=x_ref)]
        first += [copy(1 + j, me, (*chip, mc), src=x_ref) for j, chip in enumerate(chips)]
        for cp in first:
            cp.start()
        passed = [copy(4 + j, (*chip, mc), sibling) for j, chip in enumerate(chips)]
        for j, chip in enumerate(chips):
            copy(1 + j, (*chip, mc), me).wait_recv()
            passed[j].start()
        copy(0, sibling, me).wait_recv()
        for j, chip in enumerate(chips):
            copy(4 + j, (*chip, 1 - mc), me).wait_recv()
        for cp in first + passed:
            cp.wait_send()
        mine.wait()

    return pl.pallas_call(
        body, name=name, in_specs=[ANY], out_specs=ANY,
        out_shape=jax.ShapeDtypeStruct((N_DEV, Rr, C), x.dtype),
        scratch_shapes=[pltpu.SemaphoreType.DMA((7,)), pltpu.SemaphoreType.DMA((7,)), pltpu.SemaphoreType.DMA],
    )(x)


def _pair_exchange(buf, name):
    _, n, Rr, C = buf.shape
    parts = 4
    pr = Rr // parts

    def body(b_ref, got_ref, send_sems, recv_sems):
        mx, my, mc = _place()
        copies = []
        for q in range(n):
            for p in range(parts):
                rows = pl.ds(p * pr, pr)
                cp = pltpu.make_async_remote_copy(
                    src_ref=b_ref.at[1 - mc, q, rows], dst_ref=got_ref.at[q, rows],
                    send_sem=send_sems.at[q * parts + p], recv_sem=recv_sems.at[q * parts + p],
                    device_id=(mx, my, 1 - mc), device_id_type=pl.DeviceIdType.MESH)
                cp.start()
                copies.append(cp)
        for cp in copies:
            cp.wait()

    return pl.pallas_call(
        body, name=name, in_specs=[ANY], out_specs=ANY,
        out_shape=jax.ShapeDtypeStruct((n, Rr, C), buf.dtype),
        scratch_shapes=[pltpu.SemaphoreType.DMA((n * parts,)), pltpu.SemaphoreType.DMA((n * parts,))],
    )(buf)


def _pair_add(buf, got, my_c, name):
    _, n, Rr, C = buf.shape
    tr = _tile(Rr, (512, 256, 128, 64, 32, 16))

    def body(c_ref, b_ref, g_ref, o_ref):
        o_ref[...] = (b_ref[...].astype(F32) + g_ref[...].astype(F32)).astype(BF16)

    return pl.pallas_call(
        body, name=name,
        grid_spec=pltpu.PrefetchScalarGridSpec(
            num_scalar_prefetch=1, grid=(n, Rr // tr),
            in_specs=[pl.BlockSpec((None, None, tr, C), lambda q, i, c: (c[0], q, i, 0)),
                      pl.BlockSpec((None, tr, C), lambda q, i, c: (q, i, 0))],
            out_specs=pl.BlockSpec((None, tr, C), lambda q, i, c: (q, i, 0))),
        out_shape=jax.ShapeDtypeStruct((n, Rr, C), BF16),
        compiler_params=_cparams(("parallel", "parallel")))(my_c, buf, got)


def _chip_exchange(red, name):
    n, Rr, C = red.shape

    def body(r_ref, out_ref, send_sems, recv_sems, local_sem):
        mx, my, mc = _place()
        chips = [(1 - mx, my), (mx, 1 - my), (1 - mx, 1 - my)]
        mine = pltpu.make_async_copy(r_ref.at[2 * mx + my], out_ref.at[2 * mx + my], local_sem)
        mine.start()
        sends = []
        for k, (px, py) in enumerate(chips):
            cp = pltpu.make_async_remote_copy(
                src_ref=r_ref.at[2 * px + py], dst_ref=out_ref.at[2 * mx + my],
                send_sem=send_sems.at[k], recv_sem=recv_sems.at[k],
                device_id=(px, py, mc), device_id_type=pl.DeviceIdType.MESH)
            cp.start()
            sends.append(cp)
        for k, (px, py) in enumerate(chips):
            pltpu.make_async_remote_copy(
                src_ref=r_ref.at[2 * px + py], dst_ref=out_ref.at[2 * px + py],
                send_sem=send_sems.at[k], recv_sem=recv_sems.at[k],
                device_id=(px, py, mc), device_id_type=pl.DeviceIdType.MESH).wait_recv()
        for cp in sends:
            cp.wait_send()
        mine.wait()

    return pl.pallas_call(
        body, name=name, in_specs=[ANY], out_specs=ANY,
        out_shape=jax.ShapeDtypeStruct((n, Rr, C), red.dtype),
        scratch_shapes=[pltpu.SemaphoreType.DMA((3,)), pltpu.SemaphoreType.DMA((3,)), pltpu.SemaphoreType.DMA],
    )(red)


def _pad_rows(a, rows):
    return jnp.pad(a, ((0, rows - a.shape[0]), (0, 0)))


class _Layout:
    def __init__(self, D, shards):
        self.D = D
        self.pieces = []
        off = 0
        for name, layer, rows in shards:
            pr = _round_up(rows, 16)
            self.pieces.append((name, layer, rows, pr, off))
            off += pr
        self.rows = _round_up(off, 512)

    def find(self, name, layer):
        for p in self.pieces:
            if p[0] == name and p[1] == layer:
                return p
        raise KeyError(name)


def kernel(x, c, ctx, c_ctx, w_ada, b_ada, g_mix, w_in, conv_w, conv_b, dt_bias, a_log, d_skip, ssd_norm_w, w_ssd_out, pool_w, pool_scale, w_pool_out, w_out, g_ffn, w_gate_up, w_down, g_final, loss_target, m_c_ctx, m_w_ada, m_b_ada, m_g_mix, m_w_in, m_conv_w, m_conv_b, m_dt_bias, m_a_log, m_d_skip, m_ssd_norm_w, m_w_ssd_out, m_pool_w, m_pool_scale, m_w_pool_out, m_w_out, m_g_ffn, m_w_gate_up, m_w_down, m_g_final, v_c_ctx, v_w_ada, v_b_ada, v_g_mix, v_w_in, v_conv_w, v_conv_b, v_dt_bias, v_a_log, v_d_skip, v_ssd_norm_w, v_w_ssd_out, v_pool_w, v_pool_scale, v_w_pool_out, v_w_out, v_g_ffn, v_w_gate_up, v_w_down, v_g_final):
    weights = dict(c_ctx=c_ctx, w_ada=w_ada, b_ada=b_ada, g_mix=g_mix, w_in=w_in, conv_w=conv_w, conv_b=conv_b,
                   dt_bias=dt_bias, a_log=a_log, d_skip=d_skip, ssd_norm_w=ssd_norm_w, w_ssd_out=w_ssd_out,
                   pool_w=pool_w, pool_scale=pool_scale, w_pool_out=w_pool_out, w_out=w_out, g_ffn=g_ffn,
                   w_gate_up=w_gate_up, w_down=w_down, g_final=g_final)
    moms_m = dict(c_ctx=m_c_ctx, w_ada=m_w_ada, b_ada=m_b_ada, g_mix=m_g_mix, w_in=m_w_in, conv_w=m_conv_w,
                  conv_b=m_conv_b, dt_bias=m_dt_bias, a_log=m_a_log, d_skip=m_d_skip, ssd_norm_w=m_ssd_norm_w,
                  w_ssd_out=m_w_ssd_out, pool_w=m_pool_w, pool_scale=m_pool_scale, w_pool_out=m_w_pool_out,
                  w_out=m_w_out, g_ffn=m_g_ffn, w_gate_up=m_w_gate_up, w_down=m_w_down, g_final=m_g_final)
    moms_v = dict(c_ctx=v_c_ctx, w_ada=v_w_ada, b_ada=v_b_ada, g_mix=v_g_mix, w_in=v_w_in, conv_w=v_conv_w,
                  conv_b=v_conv_b, dt_bias=v_dt_bias, a_log=v_a_log, d_skip=v_d_skip, ssd_norm_w=v_ssd_norm_w,
                  w_ssd_out=v_w_ssd_out, pool_w=v_pool_w, pool_scale=v_pool_scale, w_pool_out=v_w_pool_out,
                  w_out=v_w_out, g_ffn=v_g_ffn, w_gate_up=v_w_gate_up, w_down=v_w_down, g_final=v_g_final)
    order = ["c_ctx", "w_ada", "b_ada", "g_mix", "w_in", "conv_w", "conv_b", "dt_bias", "a_log", "d_skip",
             "ssd_norm_w", "w_ssd_out", "pool_w", "pool_scale", "w_pool_out", "w_out", "g_ffn", "w_gate_up",
             "w_down", "g_final"]
    big = ["w_ada", "w_in", "conv_w", "w_ssd_out", "pool_w", "w_pool_out", "w_out", "w_gate_up", "w_down"]
    small = [n for n in order if n not in big]

    depth = w_in.shape[0]
    L, D = x.shape[1], x.shape[2]
    n_ctx = ctx.shape[1]
    T = n_ctx + L
    in_cols = w_in.shape[2] * N_DEV
    xbc_w = conv_w.shape[2] * N_DEV
    dinner = ssd_norm_w.shape[1]
    H = dt_bias.shape[2]
    G = H // HPG
    GN = G * STATE
    assert xbc_w == dinner + 2 * GN and dinner == H * HEADDIM
    assert in_cols == dinner + xbc_w + 2 * H + D + 2 * D
    assert dinner == 2 * D and GN == D and 2 * H <= 128
    F = w_down.shape[1] * N_DEV
    pg = pool_w.shape[3]
    tr = n_ctx
    assert L % tr == 0 and tr % GRID_W == 0 and tr % CHUNK == 0 and L % CHUNK == 0
    n_ctx_tiles = 1
    NP = _round_up(9 * D + 128, 512)
    gs = dinner // G
    off_xbc, off_dt, off_pool = dinner, dinner + xbc_w, dinner + xbc_w + 2 * H
    off_gate = off_pool + D

    def shard_rows(name, l):
        w = weights[name][l]
        if name in ("w_ada", "w_in", "w_gate_up"):
            return w.T
        if name == "conv_w":
            w8 = _pad_rows(w, 8)
            hi = w8.astype(BF16)
            lo = (w8 - hi.astype(F32)).astype(BF16)
            return jnp.concatenate([hi, lo], axis=0).reshape(-1, D)
        if name == "pool_w":
            return w.reshape(-1, D)
        return w

    shards = [(n, l, shard_rows(n, l)) for l in range(depth) for n in big]
    lay = _Layout(D, [(n, l, a.shape[0]) for n, l, a in shards])
    packed = jnp.concatenate([_pad_rows(a.astype(BF16), lay.find(n, l)[3]) for n, l, a in shards], axis=0)
    gathered = _all_gather(_pad_rows(packed, lay.rows), "gather_weights")

    def full(name, l):
        _, _, rows, _, off = lay.find(name, l)
        return gathered[:, off:off + rows, :]

    def w_inT_new(l):
        w = full("w_in", l).reshape(in_cols, D)
        parts = [w[:off_xbc], w[off_xbc:off_dt], w[off_pool:off_gate], w[off_gate:], w[off_dt:off_pool]]
        return _pad_rows(jnp.concatenate(parts, axis=0), NP)

    xs0 = jnp.concatenate([ctx[0], x[0]], axis=0)
    cc8 = _pad_rows(jnp.concatenate([c, c_ctx[None, :]], axis=0), 8)
    tgt = loss_target[0]

    def vec(a):
        return a.reshape(1, -1)

    def pad128(a):
        return jnp.pad(a.reshape(1, -1), ((0, 0), (0, 128 - 2 * H)))

    expand = (jnp.arange(128)[:, None] == jnp.arange(2 * H * HEADDIM)[None, :] // HEADDIM).astype(BF16)

    def from4(arr):
        return jnp.pad(arr.transpose(2, 0, 1, 3).reshape(T, 2 * H), ((0, 0), (0, 128 - 2 * H)))

    dt_blk = (9 * D) // 128
    conv_tc = 128
    ssd_gpb = 2
    saved = []
    xcur = xs0
    for l in range(depth):
        W = dict(
            adaT=full("w_ada", l).reshape(6 * D, D), inT=w_inT_new(l),
            ssd=full("w_ssd_out", l).reshape(dinner, D), po=full("w_pool_out", l).reshape(D, D),
            out=full("w_out", l).reshape(D, D), guT=full("w_gate_up", l).reshape(2 * F, D),
            down=full("w_down", l).reshape(F, D),
            pool=full("pool_w", l).reshape(N_DEV, len(POOL_WINDOWS), pg // N_DEV, pg).transpose(1, 0, 2, 3)
            .reshape(len(POOL_WINDOWS), pg, pg),
        )
        cw = full("conv_w", l).reshape(N_DEV, 16, xbc_w // N_DEV).astype(F32)
        W["conv8"] = (cw[:, :8] + cw[:, 8:]).transpose(1, 0, 2).reshape(8, xbc_w)
        m6 = _ada_fwd(cc8, W["adaT"], vec(b_ada[l]), "ada_fwd")
        h = _norm_mod(xcur, vec(g_mix[l]), m6, 0, 1, n_ctx_tiles, tr, "norm_mod")
        proj = _mm(h, W["inT"], "nt", F32, "mm_in")
        l1, l2, l3, dtb, ein, dte, etot = _dt_prep(proj, dt_blk, pad128(dt_bias[l]), pad128(a_log[l]), expand, H,
                                                   "dt_prep")
        L1, L2, L3 = (v[:, :2 * H].T.reshape(2, H, 1, T) for v in (l1, l2, l3))
        k16 = jnp.arange(16).reshape(1, 1, 16, 1)

        def rows16(at):
            terms = jnp.where(k16 == at, L1, jnp.where(k16 == at + 1, L2, L3))
            return jnp.where((k16 >= at) & (k16 < at + 3), terms, (k16 < 6).astype(BF16))

        lam_a, lam_b = rows16(0), -rows16(3) + 2 * (k16 < 3).astype(BF16)
        et = etot.reshape(T // CHUNK, 8, 128)[:, 0, :2 * H].reshape(T // CHUNK, 2, G, HPG).transpose(1, 2, 0, 3)
        etot5 = jnp.pad(jnp.broadcast_to(et[..., None], et.shape + (128,)),
                        ((0, 0), (0, 0), (0, 0), (0, 8 - HPG), (0, 0)))
        dsk = jnp.pad(jnp.repeat(d_skip[l], HEADDIM, axis=1)[:, None, :], ((0, 0), (0, 7), (0, 0)))
        scan_ops = (dtb, ein, dte, lam_a, lam_b, etot5)
        xbc = _conv_fwd(proj, dinner // conv_tc, xbc_w, W["conv8"], vec(conv_b[l]), n_ctx, conv_tc, "conv_fwd")
        y2, states = _ssd_fwd(xbc, *scan_ops, dsk, G, n_ctx, ssd_gpb, "ssd_fwd")
        yn = _gnorm_fwd(y2, proj, vec(ssd_norm_w[l]), gs, tr, "gnorm_fwd")
        pm = _pool_apply(proj, (6 * D) // pg, BF16, n_ctx, tr, pg, False, "pool_fwd")
        pms = _pool_mix_fwd(pm, W["pool"], vec(pool_scale[l]), tr, "pool_mix_fwd")
        o_ssd = _mm(yn, W["ssd"], "nn", F32, "mm_ssd_out")
        o_pool = _mm(pms, W["po"], "nn", F32, "mm_pool_out")
        mg = _merge_fwd(proj, 7, o_ssd, o_pool, tr, "merge_fwd")
        mo = _mm(mg, W["out"], "nn", F32, "mm_out")
        x1, h2 = _norm_mod(xcur, vec(g_ffn[l]), m6, 3, 4, n_ctx_tiles, tr, "resid_norm_mod", resid=(mo, 2))
        gu = _mm(h2, W["guT"], "nt", F32, "mm_gate_up")
        act = _swiglu_fwd(gu, tr, "swiglu_fwd")
        f = _mm(act, W["down"], "nn", F32, "mm_down")
        saved.append(dict(W=W, m6=m6, x0=xcur, h=h, proj=proj, scan_ops=scan_ops, xbc=xbc, y2=y2,
                          states=states, yn=yn, pm=pm, pms=pms, o_ssd=o_ssd, o_pool=o_pool, mg=mg, mo=mo, x1=x1,
                          h2=h2, gu=gu, act=act, f=f))
        xcur = _resid(x1, f, m6, 5, n_ctx_tiles, tr, "resid")

    loss_blk, dx, dgf = _loss_head(xcur, tgt, vec(g_final), n_ctx_tiles, tr, "loss_head")
    loss = lax.psum(loss_blk[0, 0], MESH_AXES)

    big_rows = {}
    small_g = {n: [None] * depth for n in small}
    d_c_ctx = jnp.zeros((D,), F32)
    for l in reversed(range(depth)):
        S = saved[l]
        W, m6, proj = S["W"], S["m6"], S["proj"]
        df, dga2 = _resid_bwd(dx, S["f"], m6, 5, n_ctx_tiles, tr, "resid_bwd")
        dact = _mm(df, W["down"], "nt", F32, "mm_down_dx")
        g_down = _mm(S["act"], df, "tn", BF16,"mm_down_dw")
        dgu = _swiglu_bwd(S["gu"], dact, tr, "swiglu_bwd")
        dh2 = _mm(dgu, W["guT"], "nn", F32, "mm_gate_up_dx")
        g_guT = _mm(dgu, S["h2"], "tn", BF16,"mm_gate_up_dw")
        dx1, st2 = _norm_mod_bwd(S["x1"], dh2, dx, vec(g_ffn[l]), m6, 4, n_ctx_tiles, tr, "norm_mod_bwd")
        dmo, dga1 = _resid_bwd(dx1, S["mo"], m6, 2, n_ctx_tiles, tr, "resid_bwd")
        dmg = _mm(dmo, W["out"], "nt", F32, "mm_out_dx")
        g_out = _mm(S["mg"], dmo, "tn", BF16,"mm_out_dw")
        do_ssd, do_pool, dgl = _merge_bwd(proj, 7, S["o_ssd"], S["o_pool"], dmg, tr, "merge_bwd")
        dyn = _mm(do_ssd, W["ssd"], "nt", F32, "mm_ssd_out_dx")
        g_ssd = _mm(S["yn"], do_ssd, "tn", BF16,"mm_ssd_out_dw")
        dpms = _mm(do_pool, W["po"], "nt", F32, "mm_pool_out_dx")
        g_po = _mm(S["pms"], do_pool, "tn", BF16,"mm_pool_out_dw")
        dpm, g_pool, dps = _pool_mix_bwd(S["pm"], dpms, W["pool"], vec(pool_scale[l]), tr, "pool_mix_bwd")
        dup = _pool_apply(dpm, 0, BF16, n_ctx, tr, pg, True, "pool_bwd")
        dy, dz, dnw = _gnorm_bwd(S["y2"], proj, vec(ssd_norm_w[l]), dyn, gs, tr, "gnorm_bwd")
        dxs2, db2, dc2, ddt4, dlc4, dlr4 = _ssd_bwd(S["xbc"], dy, S["states"], *S["scan_ops"], G, n_ctx, ssd_gpb,
                                                    "ssd_bwd")
        dskv = _pad_rows(jnp.repeat(d_skip[l], HEADDIM, axis=1), 8)
        dxbc_act, ddsk = _ssd_combine(dxs2, db2, dc2, dy, S["xbc"], dskv, tr, "ssd_combine")
        dxbc, dconv = _conv_bwd(proj, dinner // conv_tc, dxbc_act, W["conv8"], vec(conv_b[l]), n_ctx, conv_tc,
                                "conv_bwd")
        ddt_raw, dtst = _dt_bwd(proj, dt_blk, pad128(dt_bias[l]), pad128(a_log[l]), from4(ddt4),
                                from4(dlc4 + dlr4.transpose(0, 1, 3, 2)), H, tr, "dt_bwd")
        dproj = jnp.concatenate([dz, dxbc, dup, dgl, ddt_raw, jnp.zeros((T, NP - 9 * D - 128), BF16)], axis=1)
        dh = _mm(dproj, W["inT"], "nn", F32, "mm_in_dx")
        g_inT_new = _mm(dproj, S["h"], "tn", BF16,"mm_in_dw")
        dx0, st1 = _norm_mod_bwd(S["x0"], dh, dx1, vec(g_mix[l]), m6, 1, n_ctx_tiles, tr, "norm_mod_bwd")
        dm6 = _pad_rows(jnp.concatenate([st1[0:2], st1[2:4], dga1[0:2], st2[0:2], st2[2:4], dga2[0:2]], axis=1), 8)
        dsil = _mm(dm6, W["adaT"], "nn", F32, "mm_ada_dx")
        sil_b, dcc, dbada = _ada_bwd_small(cc8, dsil, dm6, "ada_bwd_small")
        g_adaT = _mm(dm6, sil_b, "tn", BF16,"mm_ada_dw")
        d_c_ctx = d_c_ctx + dcc[1]
        dx = dx0

        g_inT = jnp.concatenate([g_inT_new[:6 * D], g_inT_new[9 * D:9 * D + 2 * H], g_inT_new[6 * D:9 * D]], axis=0)
        big_rows[("w_ada", l)] = g_adaT.reshape(N_DEV, -1, D)
        big_rows[("w_in", l)] = g_inT.reshape(N_DEV, -1, D)
        big_rows[("conv_w", l)] = jnp.pad(
            dconv[:CONV_K].reshape(CONV_K, N_DEV, xbc_w // N_DEV).transpose(1, 0, 2),
            ((0, 0), (0, 16 - CONV_K), (0, 0))).reshape(N_DEV, -1, D)
        big_rows[("w_ssd_out", l)] = g_ssd.reshape(N_DEV, -1, D)
        big_rows[("pool_w", l)] = g_pool.reshape(len(POOL_WINDOWS), N_DEV, pg // N_DEV, pg).transpose(1, 0, 2, 3) \
            .reshape(N_DEV, -1, D)
        big_rows[("w_pool_out", l)] = g_po.reshape(N_DEV, -1, D)
        big_rows[("w_out", l)] = g_out.reshape(N_DEV, -1, D)
        big_rows[("w_gate_up", l)] = g_guT.reshape(N_DEV, -1, D)
        big_rows[("w_down", l)] = g_down.reshape(N_DEV, -1, D)
        small_g["b_ada"][l] = dbada[0]
        small_g["g_mix"][l] = st1[4]
        small_g["conv_b"][l] = dconv[CONV_K]
        small_g["dt_bias"][l] = dtst[0, :2 * H].reshape(2, H)
        small_g["a_log"][l] = dtst[1, :2 * H].reshape(2, H)
        dsk_h = ddsk[0].reshape(H, HEADDIM).sum(axis=-1)
        small_g["d_skip"][l] = jnp.stack([dsk_h, dsk_h])
        small_g["ssd_norm_w"][l] = dnw[0]
        small_g["pool_scale"][l] = dps[0]
        small_g["g_ffn"][l] = st2[4]
    grad_x = dx[n_ctx:][None]

    gparts = [jnp.pad(big_rows[(n, l)].astype(BF16), ((0, 0), (0, pr - rows), (0, 0)))
              for n, l, rows, pr, _ in lay.pieces]
    gparts.append(jnp.zeros((N_DEV, lay.rows - sum(p[3] for p in lay.pieces), D), BF16))
    gbuf = jnp.concatenate(gparts, axis=1)
    gbuf = gbuf.reshape(2, 2, 2, lay.rows, D).transpose(2, 0, 1, 3, 4).reshape(2, 4, lay.rows, D)
    got = _pair_exchange(gbuf, "rs_pair_exchange")
    red = _pair_add(gbuf, got, lax.axis_index("c").astype(jnp.int32).reshape(1), "rs_pair_add")
    slots = _chip_exchange(red, "rs_chip_exchange")
    g_local = _sum_slots(slots, "rs_chip_add")

    def local_grad(name):
        outs = []
        for l in range(depth):
            _, _, rows, _, off = lay.find(name, l)
            piece = g_local[off:off + rows]
            if name in ("w_ada", "w_in", "w_gate_up"):
                piece = piece.T
            elif name == "conv_w":
                piece = piece.reshape(16, -1)[:CONV_K]
            elif name == "pool_w":
                piece = piece.reshape(weights[name].shape[1:])
            outs.append(piece)
        return jnp.stack(outs)

    grads = {n: local_grad(n) for n in big}

    small_full = {"c_ctx": d_c_ctx, "g_final": dgf[0]}
    for n in small:
        if n not in small_full:
            small_full[n] = jnp.stack(small_g[n])

    def pack_small(tree):
        flat = jnp.concatenate([tree[n].reshape(-1).astype(F32) for n in small])
        rows = _round_up(-(-flat.shape[0] // D), 8)
        return jnp.pad(flat, (0, rows * D - flat.shape[0])).reshape(rows, D)

    def unpack_small(buf):
        flat, out, off = buf.reshape(-1), {}, 0
        for n in small:
            sz = weights[n].size
            out[n] = flat[off:off + sz].reshape(weights[n].shape)
            off += sz
        return out

    g_small = _sum_slots(_all_gather(pack_small(small_full), "gather_small_grads"), "sum_small_grads")
    grads.update(unpack_small(g_small))

    delta, new_m, new_v = {}, {}, {}
    for n in big:
        shp = weights[n].shape
        d_, m_, v_ = _adamw(weights[n].reshape(-1, shp[-1]), grads[n].reshape(-1, shp[-1]),
                            moms_m[n].reshape(-1, shp[-1]), moms_v[n].reshape(-1, shp[-1]), "adamw_" + n)
        delta[n], new_m[n], new_v[n] = d_.reshape(shp), m_.reshape(shp), v_.reshape(shp)
    d_, m_, v_ = _adamw(pack_small(weights), g_small, pack_small(moms_m), pack_small(moms_v), "adamw_small")
    delta.update(unpack_small(d_))
    new_m.update(unpack_small(m_))
    new_v.update(unpack_small(v_))

    return (loss, grad_x, *[grads[n] for n in order], *[delta[n] for n in order],
            *[new_m[n] for n in order], *[new_v[n] for n in order])
```

```python
import functools

import jax
import jax.numpy as jnp
from jax import lax
from jax.experimental import pallas as pl
from jax.experimental.pallas import tpu as pltpu

F32 = jnp.float32
BF16 = jnp.bfloat16
N_DEV = 8
EPS = 1e-6
GRID_W = 64
POOL_WINDOWS = (2, 4, 8, 16)
HEADDIM = 64
STATE = 128
CHUNK = 128
HPG = 4
CONV_K = 5
ADAM_LR, ADAM_B1, ADAM_B2, ADAM_EPS, ADAM_WD, ADAM_STEP = 0.001, 0.9, 0.999, 1e-08, 0.01, 10
NEG_BIG = -1e30
MESH_AXES = ("x", "y", "c")
ANY = pl.BlockSpec(memory_space=pl.ANY)


def _tile(n, cands):
    for t in cands:
        if n % t == 0:
            return t
    return n


def _round_up(n, m):
    return -(-n // m) * m


def _silu(x):
    return x * jax.nn.sigmoid(x)


def _dsilu(x):
    s = jax.nn.sigmoid(x)
    return s * (1.0 + x * (1.0 - s))


def _cparams(sem):
    return pltpu.CompilerParams(dimension_semantics=sem, vmem_limit_bytes=56 * 1024 * 1024)


def _mm(a, b, mode, out_dtype, name):
    if mode == "tn":
        K, M = a.shape
        N = b.shape[1]
        tm = _tile(M, (512, 256, 128))
        tn = _tile(N, (1024, 512, 256, 128))
        tk = _tile(K, (1088, 544, 512, 256, 128))
        a_spec = pl.BlockSpec((tk, tm), lambda i, j, k: (k, i))
        b_spec = pl.BlockSpec((tk, tn), lambda i, j, k: (k, j))
        dims = (((0,), (0,)), ((), ()))
    else:
        M, K = a.shape
        N = b.shape[0] if mode == "nt" else b.shape[1]
        tm = _tile(M, (1088, 544, 512, 256, 128))
        tk = K if K <= 4096 else max(t for t in range(128, 2817, 128) if K % t == 0)
        tn = _tile(N, (512, 256, 128)) if tk == K else _tile(N, (1024, 512, 256, 128))
        a_spec = pl.BlockSpec((tm, tk), lambda i, j, k: (i, k))
        if mode == "nt":
            b_spec = pl.BlockSpec((tn, tk), lambda i, j, k: (j, k))
            dims = (((1,), (1,)), ((), ()))
        else:
            b_spec = pl.BlockSpec((tk, tn), lambda i, j, k: (k, j))
            dims = (((1,), (0,)), ((), ()))
    nk = K // tk

    def body(a_ref, b_ref, o_ref, *acc):
        if nk == 1:
            o_ref[...] = lax.dot_general(a_ref[...].astype(BF16), b_ref[...].astype(BF16), dims,
                                         preferred_element_type=F32).astype(o_ref.dtype)
            return
        k = pl.program_id(2)

        @pl.when(k == 0)
        def _():
            acc[0][...] = jnp.zeros_like(acc[0])

        acc[0][...] += lax.dot_general(a_ref[...].astype(BF16), b_ref[...].astype(BF16), dims,
                                       preferred_element_type=F32)

        @pl.when(k == nk - 1)
        def _():
            o_ref[...] = acc[0][...].astype(o_ref.dtype)

    return pl.pallas_call(
        body, name=name, grid=(M // tm, N // tn, nk),
        in_specs=[a_spec, b_spec], out_specs=pl.BlockSpec((tm, tn), lambda i, j, k: (i, j)),
        out_shape=jax.ShapeDtypeStruct((M, N), out_dtype),
        scratch_shapes=[pltpu.VMEM((tm, tn), F32)] if nk > 1 else [],
        compiler_params=_cparams(("parallel", "parallel", "arbitrary")),
    )(a, b)


def _ada_fwd(cc8, w_adaT, b_ada, name):
    D = cc8.shape[1]
    N = w_adaT.shape[0]
    tn = _tile(N, (512, 256, 128))

    def body(c_ref, w_ref, b_ref, o_ref):
        a = _silu(c_ref[...]).astype(BF16)
        o_ref[...] = lax.dot_general(a, w_ref[...], (((1,), (1,)), ((), ())),
                                     preferred_element_type=F32) + b_ref[...]

    return pl.pallas_call(
        body, name=name, grid=(N // tn,),
        in_specs=[pl.BlockSpec((8, D), lambda j: (0, 0)), pl.BlockSpec((tn, D), lambda j: (j, 0)),
                  pl.BlockSpec((1, tn), lambda j: (0, j))],
        out_specs=pl.BlockSpec((8, tn), lambda j: (0, j)),
        out_shape=jax.ShapeDtypeStruct((8, N), F32),
        compiler_params=_cparams(("parallel",)),
    )(cc8, w_adaT, b_ada)


def _ada_bwd_small(cc8, dsil, dm6, name):
    D = cc8.shape[1]
    N = dm6.shape[1]

    def body(c_ref, ds_ref, dm_ref, sil_ref, dc_ref, db_ref):
        c = c_ref[...]
        sil_ref[...] = _silu(c).astype(BF16)
        dc_ref[...] = ds_ref[...] * _dsilu(c)
        dm = dm_ref[...]
        row = lax.broadcasted_iota(jnp.int32, dm.shape, 0)
        db_ref[...] = jnp.where(row == 0, jnp.sum(dm, axis=0, keepdims=True), 0.0)

    return pl.pallas_call(
        body, name=name, grid=(1,),
        in_specs=[pl.BlockSpec((8, D), lambda i: (0, 0)), pl.BlockSpec((8, D), lambda i: (0, 0)),
                  pl.BlockSpec((8, N), lambda i: (0, 0))],
        out_specs=[pl.BlockSpec((8, D), lambda i: (0, 0)), pl.BlockSpec((8, D), lambda i: (0, 0)),
                   pl.BlockSpec((8, N), lambda i: (0, 0))],
        out_shape=[jax.ShapeDtypeStruct((8, D), BF16), jax.ShapeDtypeStruct((8, D), F32),
                   jax.ShapeDtypeStruct((8, N), F32)],
        compiler_params=_cparams(("arbitrary",)),
    )(cc8, dsil, dm6)


def _seg_pick(m_ref, is_ctx):
    return jnp.where(is_ctx, m_ref[1:2, :], m_ref[0:1, :])


def _norm_mod(x, g, m6, sh_idx, sc_idx, n_ctx_tiles, tr, name, resid=None):
    T, D = x.shape
    row = pl.BlockSpec((tr, D), lambda i: (i, 0))
    vec = pl.BlockSpec((1, D), lambda i: (0, 0))

    def mcol(idx):
        return pl.BlockSpec((8, D), lambda i: (0, idx))

    def body(*refs):
        if resid is None:
            x_ref, g_ref, sh_ref, sc_ref, h_ref = refs
            xv = x_ref[...]
        else:
            x_ref, f_ref, ga_ref, g_ref, sh_ref, sc_ref, xo_ref, h_ref = refs
        is_ctx = pl.program_id(0) < n_ctx_tiles
        if resid is not None:
            xv = x_ref[...] + _seg_pick(ga_ref, is_ctx) * f_ref[...]
            xo_ref[...] = xv
        rstd = lax.rsqrt(jnp.mean(xv * xv, axis=-1, keepdims=True) + EPS)
        hn = xv * rstd * g_ref[...]
        h_ref[...] = (hn * (1.0 + _seg_pick(sc_ref, is_ctx)) + _seg_pick(sh_ref, is_ctx)).astype(BF16)

    if resid is None:
        ins, in_specs = [x, g, m6, m6], [row, vec, mcol(sh_idx), mcol(sc_idx)]
        out_specs, out_shape = row, jax.ShapeDtypeStruct((T, D), BF16)
    else:
        f, ga_idx = resid
        ins = [x, f, m6, g, m6, m6]
        in_specs = [row, row, mcol(ga_idx), vec, mcol(sh_idx), mcol(sc_idx)]
        out_specs = [row, row]
        out_shape = [jax.ShapeDtypeStruct((T, D), F32), jax.ShapeDtypeStruct((T, D), BF16)]
    return pl.pallas_call(body, name=name, grid=(T // tr,), in_specs=in_specs, out_specs=out_specs,
                          out_shape=out_shape, compiler_params=_cparams(("parallel",)))(*ins)


def _resid(x, f, m6, ga_idx, n_ctx_tiles, tr, name):
    T, D = x.shape
    row = pl.BlockSpec((tr, D), lambda i: (i, 0))

    def body(x_ref, f_ref, ga_ref, o_ref):
        is_ctx = pl.program_id(0) < n_ctx_tiles
        o_ref[...] = x_ref[...] + _seg_pick(ga_ref, is_ctx) * f_ref[...]

    return pl.pallas_call(body, name=name, grid=(T // tr,),
                          in_specs=[row, row, pl.BlockSpec((8, D), lambda i: (0, ga_idx))], out_specs=row,
                          out_shape=jax.ShapeDtypeStruct((T, D), F32),
                          compiler_params=_cparams(("parallel",)))(x, f, m6)


def _resid_bwd(dx, f, m6, ga_idx, n_ctx_tiles, tr, name):
    T, D = dx.shape
    row = pl.BlockSpec((tr, D), lambda i: (i, 0))
    acc = pl.BlockSpec((8, D), lambda i: (0, 0))

    def body(dx_ref, f_ref, ga_ref, df_ref, dga_ref):
        i = pl.program_id(0)
        is_ctx = i < n_ctx_tiles

        @pl.when(i == 0)
        def _():
            dga_ref[...] = jnp.zeros_like(dga_ref)

        dxv = dx_ref[...]
        df_ref[...] = (_seg_pick(ga_ref, is_ctx) * dxv).astype(BF16)
        s = jnp.sum(dxv * f_ref[...], axis=0, keepdims=True)
        r = lax.broadcasted_iota(jnp.int32, (8, D), 0)
        dga_ref[...] += jnp.where(r == jnp.where(is_ctx, 1, 0), s, 0.0)

    return pl.pallas_call(body, name=name, grid=(T // tr,),
                          in_specs=[row, row, pl.BlockSpec((8, D), lambda i: (0, ga_idx))],
                          out_specs=[row, acc],
                          out_shape=[jax.ShapeDtypeStruct((T, D), BF16), jax.ShapeDtypeStruct((8, D), F32)],
                          compiler_params=_cparams(("arbitrary",)))(dx, f, m6)


def _norm_mod_bwd(x, dh, dxres, g, m6, sc_idx, n_ctx_tiles, tr, name):
    T, D = x.shape
    row = pl.BlockSpec((tr, D), lambda i: (i, 0))
    acc = pl.BlockSpec((8, D), lambda i: (0, 0))

    def body(x_ref, dh_ref, dr_ref, g_ref, sc_ref, dx_ref, st_ref):
        i = pl.program_id(0)
        is_ctx = i < n_ctx_tiles

        @pl.when(i == 0)
        def _():
            st_ref[...] = jnp.zeros_like(st_ref)

        xv, dh_v, gv = x_ref[...], dh_ref[...], g_ref[...]
        sc1 = 1.0 + _seg_pick(sc_ref, is_ctx)
        rstd = lax.rsqrt(jnp.mean(xv * xv, axis=-1, keepdims=True) + EPS)
        xhat = xv * rstd
        dxhat = dh_v * sc1 * gv
        dx_ref[...] = dr_ref[...] + rstd * (dxhat - xhat * jnp.mean(dxhat * xhat, axis=-1, keepdims=True))
        dsh = jnp.sum(dh_v, axis=0, keepdims=True)
        dsc = jnp.sum(dh_v * xhat * gv, axis=0, keepdims=True)
        dg = jnp.sum(dh_v * sc1 * xhat, axis=0, keepdims=True)
        r = lax.broadcasted_iota(jnp.int32, (8, D), 0)
        seg = jnp.where(is_ctx, 1, 0)
        st_ref[...] += (jnp.where(r == seg, dsh, 0.0) + jnp.where(r == 2 + seg, dsc, 0.0)
                        + jnp.where(r == 4, dg, 0.0))

    return pl.pallas_call(body, name=name, grid=(T // tr,),
                          in_specs=[row, row, row, pl.BlockSpec((1, D), lambda i: (0, 0)),
                                    pl.BlockSpec((8, D), lambda i: (0, sc_idx))],
                          out_specs=[row, acc],
                          out_shape=[jax.ShapeDtypeStruct((T, D), F32), jax.ShapeDtypeStruct((8, D), F32)],
                          compiler_params=_cparams(("arbitrary",)))(x, dh, dxres, g, m6)


def _loss_head(x, tgt, g, n_ctx_tiles, tr, name):
    T, D = x.shape
    row = pl.BlockSpec((tr, D), lambda i: (i, 0))

    def body(x_ref, t_ref, g_ref, l_ref, dx_ref, dg_ref):
        i = pl.program_id(0)

        @pl.when(i == 0)
        def _():
            l_ref[...] = jnp.zeros_like(l_ref)
            dg_ref[...] = jnp.zeros_like(dg_ref)

        @pl.when(i < n_ctx_tiles)
        def _():
            dx_ref[...] = jnp.zeros_like(dx_ref)

        @pl.when(i >= n_ctx_tiles)
        def _():
            xv, gv = x_ref[...], g_ref[...]
            rstd = lax.rsqrt(jnp.mean(xv * xv, axis=-1, keepdims=True) + EPS)
            xhat = xv * rstd
            e = xhat * gv - t_ref[...]
            l_ref[...] += 0.5 * jnp.sum(jnp.mean(e * e, axis=-1, keepdims=True), axis=0, keepdims=True)
            dy = e * (1.0 / D)
            dxhat = dy * gv
            dx_ref[...] = rstd * (dxhat - xhat * jnp.mean(dxhat * xhat, axis=-1, keepdims=True))
            r = lax.broadcasted_iota(jnp.int32, (8, D), 0)
            dg_ref[...] += jnp.where(r == 0, jnp.sum(dy * xhat, axis=0, keepdims=True), 0.0)

    return pl.pallas_call(
        body, name=name, grid=(T // tr,),
        in_specs=[row, pl.BlockSpec((tr, D), lambda i: (jnp.maximum(i - n_ctx_tiles, 0), 0)),
                  pl.BlockSpec((1, D), lambda i: (0, 0))],
        out_specs=[pl.BlockSpec((8, 128), lambda i: (0, 0)), row, pl.BlockSpec((8, D), lambda i: (0, 0))],
        out_shape=[jax.ShapeDtypeStruct((8, 128), F32), jax.ShapeDtypeStruct((T, D), F32),
                   jax.ShapeDtypeStruct((8, D), F32)],
        compiler_params=_cparams(("arbitrary",)))(x, tgt, g)


def _seq_masks(T, n_ctx, width):
    row = lax.broadcasted_iota(jnp.int32, (T, width), 0)
    in_ctx = row < n_ctx
    return jnp.where(in_ctx, row, row - n_ctx), jnp.where(in_ctx, n_ctx, T - n_ctx)


def _shift_rows(u, off, t_loc, seg_len):
    T = u.shape[0]
    if off == 0:
        return u
    v = pltpu.roll(u, (-off) % T, 0)
    ok = (t_loc + off >= 0) & (t_loc + off < seg_len)
    return jnp.where(ok, v, 0.0)


def _conv_fwd(proj, col0_blk, ncol, conv_w8, conv_b, n_ctx, tc, name):
    T = proj.shape[0]

    def body(u_ref, w_ref, b_ref, o_ref):
        u = u_ref[...]
        t_loc, seg_len = _seq_masks(T, n_ctx, tc)
        acc = jnp.broadcast_to(b_ref[...], u.shape)
        for i in range(CONV_K):
            acc = acc + w_ref[i:i + 1, :] * _shift_rows(u, i - CONV_K // 2, t_loc, seg_len)
        o_ref[...] = _silu(acc)

    return pl.pallas_call(
        body, name=name, grid=(ncol // tc,),
        in_specs=[pl.BlockSpec((T, tc), lambda j: (0, col0_blk + j)), pl.BlockSpec((8, tc), lambda j: (0, j)),
                  pl.BlockSpec((1, tc), lambda j: (0, j))],
        out_specs=pl.BlockSpec((T, tc), lambda j: (0, j)),
        out_shape=jax.ShapeDtypeStruct((T, ncol), F32),
        compiler_params=_cparams(("parallel",)))(proj, conv_w8, conv_b)


def _conv_bwd(proj, col0_blk, dact, conv_w8, conv_b, n_ctx, tc, name):
    T, ncol = dact.shape

    def body(u_ref, d_ref, w_ref, b_ref, du_ref, dw_ref):
        u = u_ref[...]
        t_loc, seg_len = _seq_masks(T, n_ctx, tc)
        pre = jnp.broadcast_to(b_ref[...], u.shape)
        shifted = []
        for i in range(CONV_K):
            s = _shift_rows(u, i - CONV_K // 2, t_loc, seg_len)
            shifted.append(s)
            pre = pre + w_ref[i:i + 1, :] * s
        dpre = d_ref[...] * _dsilu(pre)
        du = jnp.zeros_like(u)
        r = lax.broadcasted_iota(jnp.int32, (8, tc), 0)
        dw = jnp.where(r == CONV_K, jnp.sum(dpre, axis=0, keepdims=True), 0.0)
        for i in range(CONV_K):
            du = du + w_ref[i:i + 1, :] * _shift_rows(dpre, -(i - CONV_K // 2), t_loc, seg_len)
            dw = dw + jnp.where(r == i, jnp.sum(dpre * shifted[i], axis=0, keepdims=True), 0.0)
        du_ref[...] = du.astype(BF16)
        dw_ref[...] = dw

    return pl.pallas_call(
        body, name=name, grid=(ncol // tc,),
        in_specs=[pl.BlockSpec((T, tc), lambda j: (0, col0_blk + j)), pl.BlockSpec((T, tc), lambda j: (0, j)),
                  pl.BlockSpec((8, tc), lambda j: (0, j)), pl.BlockSpec((1, tc), lambda j: (0, j))],
        out_specs=[pl.BlockSpec((T, tc), lambda j: (0, j)), pl.BlockSpec((8, tc), lambda j: (0, j))],
        out_shape=[jax.ShapeDtypeStruct((T, ncol), BF16), jax.ShapeDtypeStruct((8, ncol), F32)],
        compiler_params=_cparams(("parallel",)))(proj, dact, conv_w8, conv_b)


def _pool_core(u, half, t_loc, seg_len, transpose):
    tr = u.shape[0]

    def shift(v, s):
        w = pltpu.roll(v, s % tr, 0)
        ok = (t_loc - s >= 0) & (t_loc - s < seg_len)
        return jnp.where(ok, w, 0.0)

    cnt = (jnp.minimum(t_loc, half) + jnp.minimum(seg_len - t_loc, half)).astype(F32)
    q = u / cnt if transpose else u
    back, ahead, h = q, q, 1
    while h < half:
        back = back + shift(back, h)
        ahead = ahead + shift(ahead, -h)
        h *= 2
    if transpose:
        tot = back + shift(ahead, -1)
        return tot - u
    tot = shift(back, 1) + ahead
    return tot / cnt - u


def _pool_apply(src, col0_blk, out_dtype, n_ctx, tr, pg, transpose, name):
    T = src.shape[0]
    n_ctx_tiles = n_ctx // tr

    def body(u_ref, o_ref):
        i, gi = pl.program_id(0), pl.program_id(1)
        row = lax.broadcasted_iota(jnp.int32, (tr, pg), 0)
        seg_len = jnp.where(i < n_ctx_tiles, tr, GRID_W)
        t_loc = row & (seg_len - 1)
        u = u_ref[...].astype(F32)
        for k_idx, k in enumerate(POOL_WINDOWS):
            @pl.when(gi == k_idx)
            def _(k=k):
                o_ref[...] = _pool_core(u, k // 2, t_loc, seg_len, transpose).astype(o_ref.dtype)

    return pl.pallas_call(
        body, name=name, grid=(T // tr, len(POOL_WINDOWS)),
        in_specs=[pl.BlockSpec((tr, pg), lambda i, gi: (i, col0_blk + gi))],
        out_specs=pl.BlockSpec((tr, pg), lambda i, gi: (i, gi)),
        out_shape=jax.ShapeDtypeStruct((T, pg * len(POOL_WINDOWS)), out_dtype),
        compiler_params=_cparams(("parallel", "parallel")))(src)


def _pool_mix_fwd(pm, pool_w, pool_scale, tr, name):
    T, W = pm.shape
    ng, pg = pool_w.shape[0], pool_w.shape[1]

    def body(p_ref, w_ref, s_ref, o_ref):
        o_ref[...] = (jnp.dot(p_ref[...], w_ref[...], preferred_element_type=F32) * s_ref[...]).astype(BF16)

    return pl.pallas_call(
        body, name=name, grid=(T // tr, ng),
        in_specs=[pl.BlockSpec((tr, pg), lambda i, g: (i, g)), pl.BlockSpec((None, pg, pg), lambda i, g: (g, 0, 0)),
                  pl.BlockSpec((1, pg), lambda i, g: (0, g))],
        out_specs=pl.BlockSpec((tr, pg), lambda i, g: (i, g)),
        out_shape=jax.ShapeDtypeStruct((T, W), BF16),
        compiler_params=_cparams(("parallel", "parallel")))(pm, pool_w, pool_scale)


def _pool_mix_bwd(pm, dpms, pool_w, pool_scale, tr, name):
    T, W = pm.shape
    ng, pg = pool_w.shape[0], pool_w.shape[1]

    def body(p_ref, d_ref, w_ref, s_ref, dp_ref, dw_ref, ds_ref):
        i = pl.program_id(1)

        @pl.when(i == 0)
        def _():
            dw_ref[...] = jnp.zeros_like(dw_ref)
            ds_ref[...] = jnp.zeros_like(ds_ref)

        p, w = p_ref[...], w_ref[...]
        d = d_ref[...].astype(F32)
        pmix = jnp.dot(p, w, preferred_element_type=F32)
        r = lax.broadcasted_iota(jnp.int32, (8, pg), 0)
        ds_ref[...] += jnp.where(r == 0, jnp.sum(d * pmix, axis=0, keepdims=True), 0.0)
        dmix = (d * s_ref[...]).astype(BF16)
        dp_ref[...] = lax.dot_general(dmix, w, (((1,), (1,)), ((), ())), preferred_element_type=F32)
        dw_ref[...] += lax.dot_general(p, dmix, (((0,), (0,)), ((), ())), preferred_element_type=F32)

    return pl.pallas_call(
        body, name=name, grid=(ng, T // tr),
        in_specs=[pl.BlockSpec((tr, pg), lambda g, i: (i, g)), pl.BlockSpec((tr, pg), lambda g, i: (i, g)),
                  pl.BlockSpec((None, pg, pg), lambda g, i: (g, 0, 0)), pl.BlockSpec((1, pg), lambda g, i: (0, g))],
        out_specs=[pl.BlockSpec((tr, pg), lambda g, i: (i, g)), pl.BlockSpec((None, pg, pg), lambda g, i: (g, 0, 0)),
                   pl.BlockSpec((8, pg), lambda g, i: (0, g))],
        out_shape=[jax.ShapeDtypeStruct((T, W), F32), jax.ShapeDtypeStruct((ng, pg, pg), F32),
                   jax.ShapeDtypeStruct((8, W), F32)],
        compiler_params=_cparams(("parallel", "arbitrary")))(pm, dpms, pool_w, pool_scale)


def _chunk_cumsum(v, upper):
    Q = v.shape[0]
    ii = lax.broadcasted_iota(jnp.int32, (Q, Q), 0)
    jj = lax.broadcasted_iota(jnp.int32, (Q, Q), 1)
    tri = ((jj >= ii) if upper else (jj <= ii)).astype(BF16)
    h1 = v.astype(BF16)
    r1 = v - h1.astype(F32)
    h2 = r1.astype(BF16)
    h3 = (r1 - h2.astype(F32)).astype(BF16)
    return (jnp.dot(tri, h1, preferred_element_type=F32) + jnp.dot(tri, h2, preferred_element_type=F32)
            + jnp.dot(tri, h3, preferred_element_type=F32))


def _split3(v):
    h1 = v.astype(BF16)
    r1 = v - h1.astype(F32)
    h2 = r1.astype(BF16)
    return h1, h2, (r1 - h2.astype(F32)).astype(BF16)


def _dt_prep(proj, dt_blk, bias, a_log, expand, n_heads, name):
    T = proj.shape[0]
    Wd = expand.shape[1]
    Q = CHUNK
    row = pl.BlockSpec((Q, 128), lambda i: (i, 0))
    wide = pl.BlockSpec((Q, Wd), lambda i: (i, 0))

    def body(r_ref, b_ref, al_ref, e_ref, l1_ref, l2_ref, l3_ref, dtb_ref, ein_ref, dte_ref, etot_ref):
        xv = r_ref[...] + b_ref[...]
        dt = jnp.maximum(xv, 0.0) + jnp.log(1.0 + jnp.exp(-jnp.abs(xv)))
        a = -jnp.exp(al_ref[...]) * dt
        fwd_col = lax.broadcasted_iota(jnp.int32, (Q, 128), 1) < n_heads
        lam = jnp.where(fwd_col, _chunk_cumsum(a, False), _chunk_cumsum(a, True))
        tot = jnp.where(fwd_col[0:1], lam[Q - 1:Q], lam[0:1])
        l1_ref[...], l2_ref[...], l3_ref[...] = _split3(lam)
        etot_ref[...] = jnp.broadcast_to(jnp.exp(tot), (8, 128))
        ex = e_ref[...]

        def rep(v):
            p1, p2, p3 = _split3(v)
            return (jnp.dot(p1, ex, preferred_element_type=F32) + jnp.dot(p2, ex, preferred_element_type=F32)
                    + jnp.dot(p3, ex, preferred_element_type=F32))

        dtb_ref[...] = rep(dt)
        ein_ref[...] = rep(jnp.exp(lam))
        dte_ref[...] = rep(jnp.exp(tot - lam))

    vec = pl.BlockSpec((1, 128), lambda i: (0, 0))
    return pl.pallas_call(
        body, name=name, grid=(T // Q,),
        in_specs=[pl.BlockSpec((Q, 128), lambda i: (i, dt_blk)), vec, vec, pl.BlockSpec((128, Wd), lambda i: (0, 0))],
        out_specs=[row, row, row, wide, wide, wide, pl.BlockSpec((8, 128), lambda i: (i, 0))],
        out_shape=[jax.ShapeDtypeStruct((T, 128), BF16)] * 3 + [jax.ShapeDtypeStruct((T, Wd), F32)] * 3
        + [jax.ShapeDtypeStruct((T // Q * 8, 128), F32)],
        compiler_params=_cparams(("parallel",)))(proj, bias, a_log, expand)


def _dt_bwd(proj, dt_blk, bias, a_log, ddt, dlam, n_heads, tr, name):
    T = proj.shape[0]
    row = pl.BlockSpec((tr, 128), lambda i: (i, 0))
    vec = pl.BlockSpec((1, 128), lambda i: (0, 0))

    def body(r_ref, b_ref, al_ref, ddt_ref, dl_ref, o_ref, st_ref):
        @pl.when(pl.program_id(0) == 0)
        def _():
            st_ref[...] = jnp.zeros_like(st_ref)

        xv = r_ref[...] + b_ref[...]
        dt = jnp.maximum(xv, 0.0) + jnp.log(1.0 + jnp.exp(-jnp.abs(xv)))
        a_neg = -jnp.exp(al_ref[...])
        col = lax.broadcasted_iota(jnp.int32, (CHUNK, 128), 1)
        dl = dl_ref[...]
        parts = []
        for k in range(tr // CHUNK):
            dk = dl[k * CHUNK:(k + 1) * CHUNK]
            parts.append(jnp.where(col < n_heads, _chunk_cumsum(dk, True), _chunk_cumsum(dk, False)))
        dav = jnp.concatenate(parts, axis=0)
        draw = (ddt_ref[...] + dav * a_neg) * jax.nn.sigmoid(xv)
        o_ref[...] = draw.astype(BF16)
        r = lax.broadcasted_iota(jnp.int32, (8, 128), 0)
        st_ref[...] += (jnp.where(r == 0, jnp.sum(draw, axis=0, keepdims=True), 0.0)
                        + jnp.where(r == 1, jnp.sum(dav * dt, axis=0, keepdims=True) * a_neg, 0.0))

    return pl.pallas_call(
        body, name=name, grid=(T // tr,),
        in_specs=[pl.BlockSpec((tr, 128), lambda i: (i, dt_blk)), vec, vec, row, row],
        out_specs=[row, pl.BlockSpec((8, 128), lambda i: (0, 0))],
        out_shape=[jax.ShapeDtypeStruct((T, 128), BF16), jax.ShapeDtypeStruct((8, 128), F32)],
        compiler_params=_cparams(("arbitrary",)))(proj, bias, a_log, ddt, dlam)


def _scan_chunk(d, pos, nc_ctx, nc):
    rev = jnp.where(pos < nc_ctx, nc_ctx - 1 - pos, nc - 1 - (pos - nc_ctx))
    return jnp.where(d == 0, pos, rev)


def _chunk_mask(d):
    ii = lax.broadcasted_iota(jnp.int32, (CHUNK, CHUNK), 0)
    jj = lax.broadcasted_iota(jnp.int32, (CHUNK, CHUNK), 1)
    return (ii - jj) * jnp.where(d == 0, 1, -1) >= 0


def _ssd_specs(T, G, n_ctx, gpb, chunk_of):
    R, P, N, Q = HPG, HEADDIM, STATE, CHUNK
    H = G * R
    nc, nc_ctx = T // Q, n_ctx // Q
    xw, bw = gpb * R * P, gpb * N
    b_blk0 = (H * P) // bw
    c_blk0 = b_blk0 + G // gpb

    def ch(d, s):
        return chunk_of(d, s, nc_ctx, nc)

    return dict(
        x=pl.BlockSpec((Q, xw), lambda d, g, s: (ch(d, s), g)),
        b=pl.BlockSpec((Q, bw), lambda d, g, s: (ch(d, s), b_blk0 + g)),
        c=pl.BlockSpec((Q, bw), lambda d, g, s: (ch(d, s), c_blk0 + g)),
        col=pl.BlockSpec((None, gpb, Q, R), lambda d, g, s: (d, g, ch(d, s), 0)),
        row=pl.BlockSpec((None, gpb, R, Q), lambda d, g, s: (d, g, 0, ch(d, s))),
        rep=pl.BlockSpec((Q, xw), lambda d, g, s: (ch(d, s), d * (G // gpb) + g)),
        lam_a=pl.BlockSpec((None, gpb * R, 16, Q), lambda d, g, s: (d, g, 0, ch(d, s))),
        lam_b=pl.BlockSpec((None, gpb * R, 16, Q), lambda d, g, s: (d, g, 0, ch(d, s))),
        etot=pl.BlockSpec((None, gpb, None, 8, 128), lambda d, g, s: (d, g, ch(d, s), 0, 0)),
        dsk=pl.BlockSpec((None, 8, xw), lambda d, g, s: (d, 0, g)),
        xd=pl.BlockSpec((None, Q, xw), lambda d, g, s: (d, ch(d, s), g)),
        bd=pl.BlockSpec((None, Q, bw), lambda d, g, s: (d, ch(d, s), g)),
        st=pl.BlockSpec((None, None, gpb * R // 2, 2 * P, N), lambda d, g, s: (d, ch(d, s), g, 0, 0)),
    )


class _Exchange:
    def __init__(self, operand, out_sds, sems, start, finish):
        self.operand, self.out_sds, self.sems, self.start, self.finish = operand, out_sds, sems, start, finish


def _call_with_exchange(body, exch, *, name, grid, in_specs, out_specs, out_shape, scratch_shapes, operands):
    if exch is None:
        return pl.pallas_call(body, name=name, grid=grid, in_specs=in_specs, out_specs=out_specs,
                              out_shape=out_shape, scratch_shapes=scratch_shapes,
                              compiler_params=_cparams(("arbitrary",) * len(grid)))(*operands)
    n_in, n_out, n_scr = len(in_specs), len(out_specs), len(scratch_shapes)

    def fused(*refs):
        ins, c_in = refs[:n_in], refs[n_in]
        outs, c_out = refs[n_in + 1:n_in + 1 + n_out], refs[n_in + 1 + n_out]
        scr = refs[n_in + 2 + n_out:n_in + 2 + n_out + n_scr]
        sems = refs[n_in + 2 + n_out + n_scr:]
        ids = [pl.program_id(a) for a in range(len(grid))]
        first = functools.reduce(lambda p, q: p & q, [i == 0 for i in ids])
        last = functools.reduce(lambda p, q: p & q, [i == n - 1 for i, n in zip(ids, grid)])

        @pl.when(first)
        def _():
            exch.start(c_in, c_out, *sems)

        body(*ins, *outs, *scr)

        @pl.when(last)
        def _():
            exch.finish(c_in, c_out, *sems)

    return pl.pallas_call(fused, name=name, grid=grid, in_specs=list(in_specs) + [ANY],
                          out_specs=list(out_specs) + [ANY], out_shape=list(out_shape) + [exch.out_sds],
                          scratch_shapes=list(scratch_shapes) + list(exch.sems),
                          compiler_params=_cparams(("arbitrary",) * len(grid)))(*operands, exch.operand)


def _ssd_fwd(xbc, dtb, ein, dte, lam_a, lam_b, etot, dsk, G, n_ctx, gpb, name, exch=None):
    T = xbc.shape[0]
    R, P, N, Q = HPG, HEADDIM, STATE, CHUNK
    H = G * R
    nc = T // Q
    sp = _ssd_specs(T, G, n_ctx, gpb, _scan_chunk)

    def body(x_ref, b_ref, c_ref, dt_ref, ein_ref, dte_ref, la_ref, lb_ref, et_ref, dsk_ref, y_ref, st_ref, S):
        d, s = pl.program_id(0), pl.program_id(2)

        @pl.when(s == 0)
        def _():
            S[...] = jnp.zeros_like(S)

        mask = _chunk_mask(d)
        head0 = lax.broadcasted_iota(jnp.int32, (Q, 2 * P), 1) < P
        rows0 = lax.broadcasted_iota(jnp.int32, (2 * P, N), 0) < P
        for gg in range(gpb):
            Bm = b_ref[:, gg * N:(gg + 1) * N].astype(BF16)
            Cm = c_ref[:, gg * N:(gg + 1) * N].astype(BF16)
            Gm = lax.dot_general(Cm, Bm, (((1,), (1,)), ((), ())), preferred_element_type=F32)
            for k in range(R // 2):
                pk = gg * (R // 2) + k
                sl = slice(pk * 2 * P, (pk + 1) * 2 * P)
                xp = x_ref[:, sl]
                xc = xp * dt_ref[:, sl]
                s_in = S[pk]
                y = lax.dot_general(Cm, s_in.astype(BF16), (((1,), (1,)), ((), ())),
                                    preferred_element_type=F32) * ein_ref[:, sl] + dsk_ref[0:1, sl] * xp
                for j in range(2):
                    hr = 2 * pk + j
                    diff = lax.dot_general(la_ref[hr], lb_ref[hr], (((0,), (0,)), ((), ())), preferred_element_type=F32)
                    ldec = jnp.exp(jnp.where(mask, diff, NEG_BIG))
                    xc_j = (jnp.where(head0, xc, 0.0) if j == 0 else jnp.where(head0, 0.0, xc)).astype(BF16)
                    y = y + jnp.dot((Gm * ldec).astype(BF16), xc_j, preferred_element_type=F32)
                y_ref[:, sl] = y
                st_ref[pk] = s_in
                e_all = jnp.where(rows0, et_ref[gg, 2 * k:2 * k + 1, :], et_ref[gg, 2 * k + 1:2 * k + 2, :])
                xd = (xc * dte_ref[:, sl]).astype(BF16)
                S[pk] = e_all * s_in + lax.dot_general(xd, Bm, (((0,), (0,)), ((), ())),
                                                       preferred_element_type=F32)

    return _call_with_exchange(
        body, exch, name=name, grid=(2, G // gpb, nc),
        in_specs=[sp["x"], sp["b"], sp["c"], sp["rep"], sp["rep"], sp["rep"], sp["lam_a"], sp["lam_b"], sp["etot"],
                  sp["dsk"]],
        out_specs=[sp["xd"], sp["st"]],
        out_shape=[jax.ShapeDtypeStruct((2, T, H * P), F32), jax.ShapeDtypeStruct((2, nc, H // 2, 2 * P, N), F32)],
        scratch_shapes=[pltpu.VMEM((gpb * R // 2, 2 * P, N), F32)],
        operands=(xbc, xbc, xbc, dtb, ein, dte, lam_a, lam_b, etot, dsk))


def _ssd_bwd(xbc, dy, states, dtb, ein, dte, lam_a, lam_b, etot, G, n_ctx, gpb, name, exch=None):
    T = xbc.shape[0]
    R, P, N, Q = HPG, HEADDIM, STATE, CHUNK
    H = G * R
    nc = T // Q
    sp = _ssd_specs(T, G, n_ctx, gpb, lambda d, s, nc_ctx, n: _scan_chunk(d, n - 1 - s, nc_ctx, n))

    def body(x_ref, b_ref, c_ref, dy_ref, st_ref, dt_ref, ein_ref, dte_ref, la_ref, lb_ref, et_ref,
             dx_ref, db_ref, dc_ref, ddt_ref, dlc_ref, dlr_ref, dS):
        d, s = pl.program_id(0), pl.program_id(2)

        @pl.when(s == 0)
        def _():
            dS[...] = jnp.zeros_like(dS)

        mask = _chunk_mask(d)
        ri = lax.broadcasted_iota(jnp.int32, (Q, 1), 0)
        is_last = ri == jnp.where(d == 0, Q - 1, 0)
        head0 = lax.broadcasted_iota(jnp.int32, (Q, 2 * P), 1) < P
        rows0 = lax.broadcasted_iota(jnp.int32, (2 * P, N), 0) < P

        def total(v):
            return jnp.sum(jnp.sum(v, axis=1, keepdims=True), axis=0, keepdims=True)

        for gg in range(gpb):
            Bm = b_ref[:, gg * N:(gg + 1) * N].astype(BF16)
            Cm = c_ref[:, gg * N:(gg + 1) * N].astype(BF16)
            Gm = lax.dot_general(Cm, Bm, (((1,), (1,)), ((), ())), preferred_element_type=F32)
            dG = jnp.zeros((Q, Q), F32)
            dB = jnp.zeros((Q, N), F32)
            dC = jnp.zeros((Q, N), F32)
            for k in range(R // 2):
                pk = gg * (R // 2) + k
                sl = slice(pk * 2 * P, (pk + 1) * 2 * P)
                e_in = ein_ref[:, sl]
                dte = dte_ref[:, sl]
                e_all = jnp.where(rows0, et_ref[gg, 2 * k:2 * k + 1, :], et_ref[gg, 2 * k + 1:2 * k + 2, :])
                xp = x_ref[:, sl]
                dtp = dt_ref[:, sl]
                xc = xp * dtp
                xc_b = xc.astype(BF16)
                dyp = dy_ref[:, sl]
                s_in = st_ref[pk]
                s_in_b = s_in.astype(BF16)
                ds_out = dS[pk]
                ds_out_b = ds_out.astype(BF16)
                y_int = lax.dot_general(Cm, s_in_b, (((1,), (1,)), ((), ())), preferred_element_type=F32) * e_in
                b_ds = lax.dot_general(Bm, ds_out_b, (((1,), (1,)), ((), ())), preferred_element_type=F32)
                dxc = dte * b_ds
                u = xc * dxc
                v = dyp * y_int - u
                sse = ds_out * s_in * e_all
                for j in range(2):
                    hr, r = 2 * pk + j, 2 * k + j
                    def pick(a, m0=head0, j=j):
                        return jnp.where(m0, a, 0.0) if j == 0 else jnp.where(m0, 0.0, a)

                    diff = lax.dot_general(la_ref[hr], lb_ref[hr], (((0,), (0,)), ((), ())), preferred_element_type=F32)
                    ldec = jnp.exp(jnp.where(mask, diff, NEG_BIG))
                    dy_j = pick(dyp).astype(BF16)
                    dM = lax.dot_general(dy_j, xc_b, (((1,), (1,)), ((), ())), preferred_element_type=F32)
                    dMl = dM * ldec
                    Wm = dMl * Gm
                    dlam_c = jnp.sum(Wm, axis=1, keepdims=True) + jnp.sum(pick(v), axis=1, keepdims=True)
                    last = total(pick(sse, rows0)) + total(pick(u))
                    dlc_ref[gg, :, r:r + 1] = dlam_c + jnp.where(is_last, last, 0.0)
                    dlr_ref[gg, r:r + 1, :] = -jnp.sum(Wm, axis=0, keepdims=True)
                    dxc = dxc + lax.dot_general((Gm * ldec).astype(BF16), dy_j, (((0,), (0,)), ((), ())),
                                                preferred_element_type=F32)
                    dG = dG + dMl
                dx_ref[:, sl] = dxc * dtp
                t = dxc * xp
                ddt_ref[gg, :, 2 * k:2 * k + 1] = jnp.sum(jnp.where(head0, t, 0.0), axis=1, keepdims=True)
                ddt_ref[gg, :, 2 * k + 1:2 * k + 2] = jnp.sum(jnp.where(head0, 0.0, t), axis=1, keepdims=True)
                edy_b = (e_in * dyp).astype(BF16)
                dC = dC + jnp.dot(edy_b, s_in_b, preferred_element_type=F32)
                dB = dB + jnp.dot((dte * xc).astype(BF16), ds_out_b, preferred_element_type=F32)
                dS[pk] = e_all * ds_out + lax.dot_general(edy_b, Cm, (((0,), (0,)), ((), ())),
                                                          preferred_element_type=F32)
            dG_b = dG.astype(BF16)
            dc_ref[:, gg * N:(gg + 1) * N] = dC + jnp.dot(dG_b, Bm, preferred_element_type=F32)
            db_ref[:, gg * N:(gg + 1) * N] = dB + lax.dot_general(dG_b, Cm, (((0,), (0,)), ((), ())),
                                                                  preferred_element_type=F32)

    return _call_with_exchange(
        body, exch, name=name, grid=(2, G // gpb, nc),
        in_specs=[sp["x"], sp["b"], sp["c"], sp["x"], sp["st"], sp["rep"], sp["rep"], sp["rep"], sp["lam_a"],
                  sp["lam_b"], sp["etot"]],
        out_specs=[sp["xd"], sp["bd"], sp["bd"], sp["col"], sp["col"], sp["row"]],
        out_shape=[jax.ShapeDtypeStruct((2, T, H * P), F32), jax.ShapeDtypeStruct((2, T, G * N), F32),
                   jax.ShapeDtypeStruct((2, T, G * N), F32), jax.ShapeDtypeStruct((2, G, T, R), F32),
                   jax.ShapeDtypeStruct((2, G, T, R), F32), jax.ShapeDtypeStruct((2, G, R, T), F32)],
        scratch_shapes=[pltpu.VMEM((gpb * R // 2, 2 * P, N), F32)],
        operands=(xbc, xbc, xbc, dy, states, dtb, ein, dte, lam_a, lam_b, etot))


def _ssd_combine(dxs2, db2, dc2, dy, xbc, dskv, tr, name):
    T, HP = dy.shape
    GN = db2.shape[2]
    ncol = HP + 2 * GN
    tc = _tile(GN, (512, 256, 128))
    nx, nb = HP // tc, GN // tc

    def body(dx_ref, db_ref, dc_ref, dy_ref, x_ref, k_ref, o_ref, dk_ref):
        i, j = pl.program_id(1), pl.program_id(0)

        @pl.when((i == 0) & (j < nx))
        def _():
            dk_ref[...] = jnp.zeros_like(dk_ref)

        @pl.when(j < nx)
        def _():
            dyv = dy_ref[...]
            o_ref[...] = dx_ref[0] + dx_ref[1] + (k_ref[0:1, :] + k_ref[1:2, :]) * dyv
            r = lax.broadcasted_iota(jnp.int32, (8, tc), 0)
            dk_ref[...] += jnp.where(r < 2, jnp.sum(dyv * x_ref[...], axis=0, keepdims=True), 0.0)

        @pl.when((j >= nx) & (j < nx + nb))
        def _():
            o_ref[...] = db_ref[0] + db_ref[1]

        @pl.when(j >= nx + nb)
        def _():
            o_ref[...] = dc_ref[0] + dc_ref[1]

    def cl(j, lo, n):
        return jnp.clip(j - lo, 0, n - 1)

    return pl.pallas_call(
        body, name=name, grid=(ncol // tc, T // tr),
        in_specs=[pl.BlockSpec((2, tr, tc), lambda j, i: (0, i, cl(j, 0, nx))),
                  pl.BlockSpec((2, tr, tc), lambda j, i: (0, i, cl(j, nx, nb))),
                  pl.BlockSpec((2, tr, tc), lambda j, i: (0, i, cl(j, nx + nb, nb))),
                  pl.BlockSpec((tr, tc), lambda j, i: (i, cl(j, 0, nx))),
                  pl.BlockSpec((tr, tc), lambda j, i: (i, cl(j, 0, nx))),
                  pl.BlockSpec((8, tc), lambda j, i: (0, cl(j, 0, nx)))],
        out_specs=[pl.BlockSpec((tr, tc), lambda j, i: (i, j)), pl.BlockSpec((8, tc), lambda j, i: (0, cl(j, 0, nx)))],
        out_shape=[jax.ShapeDtypeStruct((T, ncol), F32), jax.ShapeDtypeStruct((8, HP), F32)],
        compiler_params=_cparams(("arbitrary", "arbitrary")))(dxs2, db2, dc2, dy, xbc, dskv)


def _gnorm_fwd(y2, proj, w, gs, tr, name):
    T, HP = y2.shape[1], y2.shape[2]

    def body(y_ref, z_ref, w_ref, o_ref):
        yz = (y_ref[0] + y_ref[1]) * _silu(z_ref[...])
        for g in range(HP // gs):
            v = yz[:, g * gs:(g + 1) * gs]
            rstd = lax.rsqrt(jnp.mean(v * v, axis=-1, keepdims=True) + EPS)
            o_ref[:, g * gs:(g + 1) * gs] = (v * rstd * w_ref[:, g * gs:(g + 1) * gs]).astype(BF16)

    return pl.pallas_call(
        body, name=name, grid=(T // tr,),
        in_specs=[pl.BlockSpec((2, tr, HP), lambda i: (0, i, 0)), pl.BlockSpec((tr, HP), lambda i: (i, 0)),
                  pl.BlockSpec((1, HP), lambda i: (0, 0))],
        out_specs=pl.BlockSpec((tr, HP), lambda i: (i, 0)),
        out_shape=jax.ShapeDtypeStruct((T, HP), BF16),
        compiler_params=_cparams(("parallel",)))(y2, proj, w)


def _gnorm_bwd(y2, proj, w, dyn, gs, tr, name):
    T, HP = y2.shape[1], y2.shape[2]
    row = pl.BlockSpec((tr, HP), lambda i: (i, 0))

    def body(y_ref, z_ref, w_ref, d_ref, dy_ref, dz_ref, dw_ref):
        @pl.when(pl.program_id(0) == 0)
        def _():
            dw_ref[...] = jnp.zeros_like(dw_ref)

        yv = y_ref[0] + y_ref[1]
        zv = z_ref[...]
        sz = _silu(zv)
        yz = yv * sz
        dv = d_ref[...]
        r = lax.broadcasted_iota(jnp.int32, (8, gs), 0)
        for g in range(HP // gs):
            sl = slice(g * gs, (g + 1) * gs)
            v = yz[:, sl]
            rstd = lax.rsqrt(jnp.mean(v * v, axis=-1, keepdims=True) + EPS)
            xhat = v * rstd
            dyn_g = dv[:, sl]
            dhat = dyn_g * w_ref[:, sl]
            dyz = rstd * (dhat - xhat * jnp.mean(dhat * xhat, axis=-1, keepdims=True))
            dy_ref[:, sl] = dyz * sz[:, sl]
            dz_ref[:, sl] = (dyz * yv[:, sl] * _dsilu(zv[:, sl])).astype(BF16)
            dw_ref[:, sl] += jnp.where(r == 0, jnp.sum(dyn_g * xhat, axis=0, keepdims=True), 0.0)

    return pl.pallas_call(
        body, name=name, grid=(T // tr,),
        in_specs=[pl.BlockSpec((2, tr, HP), lambda i: (0, i, 0)), row, pl.BlockSpec((1, HP), lambda i: (0, 0)), row],
        out_specs=[row, row, pl.BlockSpec((8, HP), lambda i: (0, 0))],
        out_shape=[jax.ShapeDtypeStruct((T, HP), F32), jax.ShapeDtypeStruct((T, HP), BF16),
                   jax.ShapeDtypeStruct((8, HP), F32)],
        compiler_params=_cparams(("arbitrary",)))(y2, proj, w, dyn)


def _merge_fwd(proj, g1_blk, o_ssd, o_pool, tr, name):
    T, D = o_ssd.shape
    row = pl.BlockSpec((tr, D), lambda i: (i, 0))

    def body(g1_ref, g2_ref, a_ref, b_ref, o_ref):
        o_ref[...] = (jax.nn.sigmoid(g1_ref[...]) * a_ref[...]
                      + jax.nn.sigmoid(g2_ref[...]) * b_ref[...]).astype(BF16)

    return pl.pallas_call(
        body, name=name, grid=(T // tr,),
        in_specs=[pl.BlockSpec((tr, D), lambda i: (i, g1_blk)), pl.BlockSpec((tr, D), lambda i: (i, g1_blk + 1)),
                  row, row],
        out_specs=row, out_shape=jax.ShapeDtypeStruct((T, D), BF16),
        compiler_params=_cparams(("parallel",)))(proj, proj, o_ssd, o_pool)


def _merge_bwd(proj, g1_blk, o_ssd, o_pool, dmg, tr, name):
    T, D = o_ssd.shape
    row = pl.BlockSpec((tr, D), lambda i: (i, 0))

    def body(g1_ref, g2_ref, a_ref, b_ref, d_ref, da_ref, db_ref, dg_ref):
        s1, s2 = jax.nn.sigmoid(g1_ref[...]), jax.nn.sigmoid(g2_ref[...])
        dv = d_ref[...]
        da_ref[...] = (s1 * dv).astype(BF16)
        db_ref[...] = (s2 * dv).astype(BF16)
        dg_ref[:, :D] = (dv * a_ref[...] * s1 * (1.0 - s1)).astype(BF16)
        dg_ref[:, D:] = (dv * b_ref[...] * s2 * (1.0 - s2)).astype(BF16)

    return pl.pallas_call(
        body, name=name, grid=(T // tr,),
        in_specs=[pl.BlockSpec((tr, D), lambda i: (i, g1_blk)), pl.BlockSpec((tr, D), lambda i: (i, g1_blk + 1)),
                  row, row, row],
        out_specs=[row, row, pl.BlockSpec((tr, 2 * D), lambda i: (i, 0))],
        out_shape=[jax.ShapeDtypeStruct((T, D), BF16), jax.ShapeDtypeStruct((T, D), BF16),
                   jax.ShapeDtypeStruct((T, 2 * D), BF16)],
        compiler_params=_cparams(("parallel",)))(proj, proj, o_ssd, o_pool, dmg)


def _swiglu_fwd(gu, tr, name):
    T, F2 = gu.shape
    F = F2 // 2
    tc = _tile(F, (1408, 768, 512, 256, 128))
    nb = F // tc

    def body(a_ref, b_ref, o_ref):
        o_ref[...] = (_silu(a_ref[...]) * b_ref[...]).astype(BF16)

    return pl.pallas_call(
        body, name=name, grid=(T // tr, nb),
        in_specs=[pl.BlockSpec((tr, tc), lambda i, j: (i, j)), pl.BlockSpec((tr, tc), lambda i, j: (i, nb + j))],
        out_specs=pl.BlockSpec((tr, tc), lambda i, j: (i, j)),
        out_shape=jax.ShapeDtypeStruct((T, F), BF16),
        compiler_params=_cparams(("parallel", "parallel")))(gu, gu)


def _swiglu_bwd(gu, dact, tr, name):
    T, F2 = gu.shape
    F = F2 // 2
    tc = _tile(F, (1408, 768, 512, 256, 128))
    nb = F // tc

    def body(a_ref, b_ref, d_ref, o_ref):
        is_a = pl.program_id(1) < nb
        av, bv, dv = a_ref[...], b_ref[...], d_ref[...]
        o_ref[...] = jnp.where(is_a, dv * bv * _dsilu(av), dv * _silu(av)).astype(BF16)

    return pl.pallas_call(
        body, name=name, grid=(T // tr, 2 * nb),
        in_specs=[pl.BlockSpec((tr, tc), lambda i, j: (i, j % nb)), pl.BlockSpec((tr, tc), lambda i, j: (i, nb + j % nb)),
                  pl.BlockSpec((tr, tc), lambda i, j: (i, j % nb))],
        out_specs=pl.BlockSpec((tr, tc), lambda i, j: (i, j)),
        out_shape=jax.ShapeDtypeStruct((T, F2), BF16),
        compiler_params=_cparams(("parallel", "parallel")))(gu, gu, dact)


def _adamw(w, g, m, v, name):
    Rr, C = w.shape
    tr = _tile(Rr, (256, 128, 64, 32, 16, 8))
    row = pl.BlockSpec((tr, C), lambda i: (i, 0))

    def body(w_ref, g_ref, m_ref, v_ref, d_ref, mo_ref, vo_ref):
        gv = g_ref[...]
        mn = ADAM_B1 * m_ref[...] + (1.0 - ADAM_B1) * gv
        vn = ADAM_B2 * v_ref[...] + (1.0 - ADAM_B2) * (gv * gv)
        m_hat = mn / (1.0 - ADAM_B1 ** ADAM_STEP)
        v_hat = vn / (1.0 - ADAM_B2 ** ADAM_STEP)
        d_ref[...] = -ADAM_LR * (m_hat / (jnp.sqrt(v_hat) + ADAM_EPS) + ADAM_WD * w_ref[...])
        mo_ref[...] = mn
        vo_ref[...] = vn

    sds = jax.ShapeDtypeStruct((Rr, C), F32)
    return pl.pallas_call(body, name=name, grid=(Rr // tr,), in_specs=[row] * 4, out_specs=[row] * 3,
                          out_shape=[sds] * 3, compiler_params=_cparams(("parallel",)))(w, g, m, v)


def _sum_slots(x, name):
    n, Rr, C = x.shape
    tr = _tile(Rr, (512, 256, 128, 64, 32, 16, 8))

    def body(x_ref, o_ref):
        acc = x_ref[0].astype(F32)
        for k in range(1, n):
            acc = acc + x_ref[k].astype(F32)
        o_ref[...] = acc

    return pl.pallas_call(body, name=name, grid=(Rr // tr,),
                          in_specs=[pl.BlockSpec((n, tr, C), lambda i: (0, i, 0))],
                          out_specs=pl.BlockSpec((tr, C), lambda i: (i, 0)),
                          out_shape=jax.ShapeDtypeStruct((Rr, C), F32),
                          compiler_params=_cparams(("parallel",)))(x)


def _place():
    return lax.axis_index("x"), lax.axis_index("y"), lax.axis_index("c")


def _all_gather(x, name):
    Rr, C = x.shape

    def body(x_ref, out_ref, send_sems, recv_sems, local_sem):
        _gather_start(x_ref, out_ref, send_sems, recv_sems, local_sem)
        _gather_finish(x_ref, out_ref, send_sems, recv_sems, local_sem)

    return pl.pallas_call(
        body, name=name, in_specs=[ANY], out_specs=ANY,
        out_shape=jax.ShapeDtypeStruct((N_DEV, Rr, C), x.dtype), scratch_shapes=_GATHER_SEMS,
    )(x)


_GATHER_SEMS = [pltpu.SemaphoreType.DMA((7,)), pltpu.SemaphoreType.DMA((7,)), pltpu.SemaphoreType.DMA]


def _gather_copies(x_ref, out_ref, send_sems, recv_sems, local_sem):
    mx, my, mc = _place()
    me, sibling = (mx, my, mc), (mx, my, 1 - mc)
    chips = [(1 - mx, my), (mx, 1 - my), (1 - mx, 1 - my)]

    def slot(px, py, pc):
        return out_ref.at[4 * px + 2 * py + pc]

    def copy(k, block, to, src=None):
        return pltpu.make_async_remote_copy(
            src_ref=slot(*block) if src is None else src, dst_ref=slot(*block),
            send_sem=send_sems.at[k], recv_sem=recv_sems.at[k],
            device_id=to, device_id_type=pl.DeviceIdType.MESH)

    return dict(
        mine=pltpu.make_async_copy(x_ref, slot(*me), local_sem),
        first=[copy(0, me, sibling, src=x_ref)] + [copy(1 + j, me, (*chip, mc), src=x_ref)
                                                   for j, chip in enumerate(chips)],
        passed=[copy(4 + j, (*chip, mc), sibling) for j, chip in enumerate(chips)],
        from_chips=[copy(1 + j, (*chip, mc), me) for j, chip in enumerate(chips)],
        from_sibling=[copy(0, sibling, me)] + [copy(4 + j, (*chip, 1 - mc), me) for j, chip in enumerate(chips)],
    )


def _gather_start(*refs):
    cps = _gather_copies(*refs)
    cps["mine"].start()
    for cp in cps["first"]:
        cp.start()


def _gather_finish(*refs):
    cps = _gather_copies(*refs)
    for j in range(3):
        cps["from_chips"][j].wait_recv()
        cps["passed"][j].start()
    for cp in cps["from_sibling"]:
        cp.wait_recv()
    for cp in cps["first"] + cps["passed"]:
        cp.wait_send()
    cps["mine"].wait()


def _pair_exchange(buf, name):
    _, n, Rr, C = buf.shape
    parts = 4
    pr = Rr // parts

    def body(b_ref, got_ref, send_sems, recv_sems):
        mx, my, mc = _place()
        copies = []
        for q in range(n):
            for p in range(parts):
                rows = pl.ds(p * pr, pr)
                cp = pltpu.make_async_remote_copy(
                    src_ref=b_ref.at[1 - mc, q, rows], dst_ref=got_ref.at[q, rows],
                    send_sem=send_sems.at[q * parts + p], recv_sem=recv_sems.at[q * parts + p],
                    device_id=(mx, my, 1 - mc), device_id_type=pl.DeviceIdType.MESH)
                cp.start()
                copies.append(cp)
        for cp in copies:
            cp.wait()

    return pl.pallas_call(
        body, name=name, in_specs=[ANY], out_specs=ANY,
        out_shape=jax.ShapeDtypeStruct((n, Rr, C), buf.dtype),
        scratch_shapes=[pltpu.SemaphoreType.DMA((n * parts,)), pltpu.SemaphoreType.DMA((n * parts,))],
    )(buf)


def _pair_add(buf, got, my_c, name):
    _, n, Rr, C = buf.shape
    tr = _tile(Rr, (512, 256, 128, 64, 32, 16))

    def body(c_ref, b_ref, g_ref, o_ref):
        o_ref[...] = (b_ref[...].astype(F32) + g_ref[...].astype(F32)).astype(BF16)

    return pl.pallas_call(
        body, name=name,
        grid_spec=pltpu.PrefetchScalarGridSpec(
            num_scalar_prefetch=1, grid=(n, Rr // tr),
            in_specs=[pl.BlockSpec((None, None, tr, C), lambda q, i, c: (c[0], q, i, 0)),
                      pl.BlockSpec((None, tr, C), lambda q, i, c: (q, i, 0))],
            out_specs=pl.BlockSpec((None, tr, C), lambda q, i, c: (q, i, 0))),
        out_shape=jax.ShapeDtypeStruct((n, Rr, C), BF16),
        compiler_params=_cparams(("parallel", "parallel")))(my_c, buf, got)


def _chip_exchange(red, name):
    def body(r_ref, out_ref, send_sems, recv_sems, local_sem):
        _chip_exchange_start(r_ref, out_ref, send_sems, recv_sems, local_sem)
        _chip_exchange_finish(r_ref, out_ref, send_sems, recv_sems, local_sem)

    return pl.pallas_call(
        body, name=name, in_specs=[ANY], out_specs=ANY,
        out_shape=jax.ShapeDtypeStruct(red.shape, red.dtype), scratch_shapes=_CHIP_SEMS,
    )(red)


_CHIP_SEMS = [pltpu.SemaphoreType.DMA((3,)), pltpu.SemaphoreType.DMA((3,)), pltpu.SemaphoreType.DMA]


def _chip_exchange_copies(r_ref, out_ref, send_sems, recv_sems, local_sem):
    mx, my, mc = _place()
    chips = [(1 - mx, my), (mx, 1 - my), (1 - mx, 1 - my)]

    def copy(k, src_slot, dst_slot, to):
        return pltpu.make_async_remote_copy(
            src_ref=r_ref.at[src_slot], dst_ref=out_ref.at[dst_slot],
            send_sem=send_sems.at[k], recv_sem=recv_sems.at[k],
            device_id=(*to, mc), device_id_type=pl.DeviceIdType.MESH)

    return dict(
        mine=pltpu.make_async_copy(r_ref.at[2 * mx + my], out_ref.at[2 * mx + my], local_sem),
        sends=[copy(k, 2 * px + py, 2 * mx + my, (px, py)) for k, (px, py) in enumerate(chips)],
        recvs=[copy(k, 2 * px + py, 2 * px + py, (px, py)) for k, (px, py) in enumerate(chips)],
    )


def _chip_exchange_start(*refs):
    cps = _chip_exchange_copies(*refs)
    cps["mine"].start()
    for cp in cps["sends"]:
        cp.start()


def _chip_exchange_finish(*refs):
    cps = _chip_exchange_copies(*refs)
    for cp in cps["recvs"]:
        cp.wait_recv()
    for cp in cps["sends"]:
        cp.wait_send()
    cps["mine"].wait()


def _pad_rows(a, rows):
    return jnp.pad(a, ((0, rows - a.shape[0]), (0, 0)))


class _Layout:
    def __init__(self, D, shards):
        self.D = D
        self.pieces = []
        off = 0
        for name, layer, rows in shards:
            pr = _round_up(rows, 16)
            self.pieces.append((name, layer, rows, pr, off))
            off += pr
        self.rows = _round_up(off, 512)

    def find(self, name, layer):
        for p in self.pieces:
            if p[0] == name and p[1] == layer:
                return p
        raise KeyError(name)


def kernel(x, c, ctx, c_ctx, w_ada, b_ada, g_mix, w_in, conv_w, conv_b, dt_bias, a_log, d_skip, ssd_norm_w, w_ssd_out, pool_w, pool_scale, w_pool_out, w_out, g_ffn, w_gate_up, w_down, g_final, loss_target, m_c_ctx, m_w_ada, m_b_ada, m_g_mix, m_w_in, m_conv_w, m_conv_b, m_dt_bias, m_a_log, m_d_skip, m_ssd_norm_w, m_w_ssd_out, m_pool_w, m_pool_scale, m_w_pool_out, m_w_out, m_g_ffn, m_w_gate_up, m_w_down, m_g_final, v_c_ctx, v_w_ada, v_b_ada, v_g_mix, v_w_in, v_conv_w, v_conv_b, v_dt_bias, v_a_log, v_d_skip, v_ssd_norm_w, v_w_ssd_out, v_pool_w, v_pool_scale, v_w_pool_out, v_w_out, v_g_ffn, v_w_gate_up, v_w_down, v_g_final):
    weights = dict(c_ctx=c_ctx, w_ada=w_ada, b_ada=b_ada, g_mix=g_mix, w_in=w_in, conv_w=conv_w, conv_b=conv_b,
                   dt_bias=dt_bias, a_log=a_log, d_skip=d_skip, ssd_norm_w=ssd_norm_w, w_ssd_out=w_ssd_out,
                   pool_w=pool_w, pool_scale=pool_scale, w_pool_out=w_pool_out, w_out=w_out, g_ffn=g_ffn,
                   w_gate_up=w_gate_up, w_down=w_down, g_final=g_final)
    moms_m = dict(c_ctx=m_c_ctx, w_ada=m_w_ada, b_ada=m_b_ada, g_mix=m_g_mix, w_in=m_w_in, conv_w=m_conv_w,
                  conv_b=m_conv_b, dt_bias=m_dt_bias, a_log=m_a_log, d_skip=m_d_skip, ssd_norm_w=m_ssd_norm_w,
                  w_ssd_out=m_w_ssd_out, pool_w=m_pool_w, pool_scale=m_pool_scale, w_pool_out=m_w_pool_out,
                  w_out=m_w_out, g_ffn=m_g_ffn, w_gate_up=m_w_gate_up, w_down=m_w_down, g_final=m_g_final)
    moms_v = dict(c_ctx=v_c_ctx, w_ada=v_w_ada, b_ada=v_b_ada, g_mix=v_g_mix, w_in=v_w_in, conv_w=v_conv_w,
                  conv_b=v_conv_b, dt_bias=v_dt_bias, a_log=v_a_log, d_skip=v_d_skip, ssd_norm_w=v_ssd_norm_w,
                  w_ssd_out=v_w_ssd_out, pool_w=v_pool_w, pool_scale=v_pool_scale, w_pool_out=v_w_pool_out,
                  w_out=v_w_out, g_ffn=v_g_ffn, w_gate_up=v_w_gate_up, w_down=v_w_down, g_final=v_g_final)
    order = ["c_ctx", "w_ada", "b_ada", "g_mix", "w_in", "conv_w", "conv_b", "dt_bias", "a_log", "d_skip",
             "ssd_norm_w", "w_ssd_out", "pool_w", "pool_scale", "w_pool_out", "w_out", "g_ffn", "w_gate_up",
             "w_down", "g_final"]
    big = ["w_ada", "w_in", "conv_w", "w_ssd_out", "pool_w", "w_pool_out", "w_out", "w_gate_up", "w_down"]
    small = [n for n in order if n not in big]

    depth = w_in.shape[0]
    L, D = x.shape[1], x.shape[2]
    n_ctx = ctx.shape[1]
    T = n_ctx + L
    in_cols = w_in.shape[2] * N_DEV
    xbc_w = conv_w.shape[2] * N_DEV
    dinner = ssd_norm_w.shape[1]
    H = dt_bias.shape[2]
    G = H // HPG
    GN = G * STATE
    assert xbc_w == dinner + 2 * GN and dinner == H * HEADDIM
    assert in_cols == dinner + xbc_w + 2 * H + D + 2 * D
    assert dinner == 2 * D and GN == D and 2 * H <= 128
    F = w_down.shape[1] * N_DEV
    pg = pool_w.shape[3]
    tr = n_ctx
    assert L % tr == 0 and tr % GRID_W == 0 and tr % CHUNK == 0 and L % CHUNK == 0
    n_ctx_tiles = 1
    NP = _round_up(9 * D + 128, 512)
    gs = dinner // G
    off_xbc, off_dt, off_pool = dinner, dinner + xbc_w, dinner + xbc_w + 2 * H
    off_gate = off_pool + D

    def shard_rows(name, l):
        w = weights[name][l]
        if name in ("w_ada", "w_in", "w_gate_up"):
            return w.T
        if name == "conv_w":
            w8 = _pad_rows(w, 8)
            hi = w8.astype(BF16)
            lo = (w8 - hi.astype(F32)).astype(BF16)
            return jnp.concatenate([hi, lo], axis=0).reshape(-1, D)
        if name == "pool_w":
            return w.reshape(-1, D)
        return w

    shards = [[(n, shard_rows(n, l)) for n in big] for l in range(depth)]
    lays = [_Layout(D, [(n, l, a.shape[0]) for n, a in shards[l]]) for l in range(depth)]

    def packed(l):
        rows = jnp.concatenate([_pad_rows(a.astype(BF16), lays[l].find(n, l)[3]) for n, a in shards[l]], axis=0)
        return _pad_rows(rows, lays[l].rows)

    gathered = [None] * depth
    gathered[0] = _all_gather(packed(0), "gather_weights")

    def full(name, l):
        _, _, rows, _, off = lays[l].find(name, l)
        return gathered[l][:, off:off + rows, :]

    def w_inT_new(l):
        w = full("w_in", l).reshape(in_cols, D)
        parts = [w[:off_xbc], w[off_xbc:off_dt], w[off_pool:off_gate], w[off_gate:], w[off_dt:off_pool]]
        return _pad_rows(jnp.concatenate(parts, axis=0), NP)

    xs0 = jnp.concatenate([ctx[0], x[0]], axis=0)
    cc8 = _pad_rows(jnp.concatenate([c, c_ctx[None, :]], axis=0), 8)
    tgt = loss_target[0]

    def vec(a):
        return a.reshape(1, -1)

    def pad128(a):
        return jnp.pad(a.reshape(1, -1), ((0, 0), (0, 128 - 2 * H)))

    expand = (jnp.arange(128)[:, None] == jnp.arange(2 * H * HEADDIM)[None, :] // HEADDIM).astype(BF16)

    def from4(arr):
        return jnp.pad(arr.transpose(2, 0, 1, 3).reshape(T, 2 * H), ((0, 0), (0, 128 - 2 * H)))

    dt_blk = (9 * D) // 128
    conv_tc = 128
    ssd_gpb = 2
    saved = []
    xcur = xs0
    for l in range(depth):
        W = dict(
            adaT=full("w_ada", l).reshape(6 * D, D), inT=w_inT_new(l),
            ssd=full("w_ssd_out", l).reshape(dinner, D), po=full("w_pool_out", l).reshape(D, D),
            out=full("w_out", l).reshape(D, D), guT=full("w_gate_up", l).reshape(2 * F, D),
            down=full("w_down", l).reshape(F, D),
            pool=full("pool_w", l).reshape(N_DEV, len(POOL_WINDOWS), pg // N_DEV, pg).transpose(1, 0, 2, 3)
            .reshape(len(POOL_WINDOWS), pg, pg),
        )
        cw = full("conv_w", l).reshape(N_DEV, 16, xbc_w // N_DEV).astype(F32)
        W["conv8"] = (cw[:, :8] + cw[:, 8:]).transpose(1, 0, 2).reshape(8, xbc_w)
        m6 = _ada_fwd(cc8, W["adaT"], vec(b_ada[l]), "ada_fwd")
        h = _norm_mod(xcur, vec(g_mix[l]), m6, 0, 1, n_ctx_tiles, tr, "norm_mod")
        proj = _mm(h, W["inT"], "nt", F32, "mm_in")
        l1, l2, l3, dtb, ein, dte, etot = _dt_prep(proj, dt_blk, pad128(dt_bias[l]), pad128(a_log[l]), expand, H,
                                                   "dt_prep")
        L1, L2, L3 = (v[:, :2 * H].T.reshape(2, H, 1, T) for v in (l1, l2, l3))
        k16 = jnp.arange(16).reshape(1, 1, 16, 1)

        def rows16(at):
            terms = jnp.where(k16 == at, L1, jnp.where(k16 == at + 1, L2, L3))
            return jnp.where((k16 >= at) & (k16 < at + 3), terms, (k16 < 6).astype(BF16))

        lam_a, lam_b = rows16(0), -rows16(3) + 2 * (k16 < 3).astype(BF16)
        et = etot.reshape(T // CHUNK, 8, 128)[:, 0, :2 * H].reshape(T // CHUNK, 2, G, HPG).transpose(1, 2, 0, 3)
        etot5 = jnp.pad(jnp.broadcast_to(et[..., None], et.shape + (128,)),
                        ((0, 0), (0, 0), (0, 0), (0, 8 - HPG), (0, 0)))
        dsk = jnp.pad(jnp.repeat(d_skip[l], HEADDIM, axis=1)[:, None, :], ((0, 0), (0, 7), (0, 0)))
        scan_ops = (dtb, ein, dte, lam_a, lam_b, etot5)
        xbc = _conv_fwd(proj, dinner // conv_tc, xbc_w, W["conv8"], vec(conv_b[l]), n_ctx, conv_tc, "conv_fwd")
        if l + 1 < depth:
            nxt = _Exchange(packed(l + 1), jax.ShapeDtypeStruct((N_DEV, lays[l + 1].rows, D), BF16), _GATHER_SEMS,
                            _gather_start, _gather_finish)
            y2, states, gathered[l + 1] = _ssd_fwd(xbc, *scan_ops, dsk, G, n_ctx, ssd_gpb, "ssd_fwd_gather", nxt)
        else:
            y2, states = _ssd_fwd(xbc, *scan_ops, dsk, G, n_ctx, ssd_gpb, "ssd_fwd")
        yn = _gnorm_fwd(y2, proj, vec(ssd_norm_w[l]), gs, tr, "gnorm_fwd")
        pm = _pool_apply(proj, (6 * D) // pg, BF16, n_ctx, tr, pg, False, "pool_fwd")
        pms = _pool_mix_fwd(pm, W["pool"], vec(pool_scale[l]), tr, "pool_mix_fwd")
        o_ssd = _mm(yn, W["ssd"], "nn", F32, "mm_ssd_out")
        o_pool = _mm(pms, W["po"], "nn", F32, "mm_pool_out")
        mg = _merge_fwd(proj, 7, o_ssd, o_pool, tr, "merge_fwd")
        mo = _mm(mg, W["out"], "nn", F32, "mm_out")
        x1, h2 = _norm_mod(xcur, vec(g_ffn[l]), m6, 3, 4, n_ctx_tiles, tr, "resid_norm_mod", resid=(mo, 2))
        gu = _mm(h2, W["guT"], "nt", F32, "mm_gate_up")
        act = _swiglu_fwd(gu, tr, "swiglu_fwd")
        f = _mm(act, W["down"], "nn", F32, "mm_down")
        saved.append(dict(W=W, m6=m6, x0=xcur, h=h, proj=proj, scan_ops=scan_ops, xbc=xbc, y2=y2,
                          states=states, yn=yn, pm=pm, pms=pms, o_ssd=o_ssd, o_pool=o_pool, mg=mg, mo=mo, x1=x1,
                          h2=h2, gu=gu, act=act, f=f))
        xcur = _resid(x1, f, m6, 5, n_ctx_tiles, tr, "resid")

    loss_blk, dx, dgf = _loss_head(xcur, tgt, vec(g_final), n_ctx_tiles, tr, "loss_head")
    loss = lax.psum(loss_blk[0, 0], MESH_AXES)

    big_rows = {}
    small_g = {n: [None] * depth for n in small}
    d_c_ctx = jnp.zeros((D,), F32)
    slots = [None] * depth
    pending = None
    for l in reversed(range(depth)):
        S = saved[l]
        W, m6, proj = S["W"], S["m6"], S["proj"]
        df, dga2 = _resid_bwd(dx, S["f"], m6, 5, n_ctx_tiles, tr, "resid_bwd")
        dact = _mm(df, W["down"], "nt", F32, "mm_down_dx")
        g_down = _mm(S["act"], df, "tn", BF16,"mm_down_dw")
        dgu = _swiglu_bwd(S["gu"], dact, tr, "swiglu_bwd")
        dh2 = _mm(dgu, W["guT"], "nn", F32, "mm_gate_up_dx")
        g_guT = _mm(dgu, S["h2"], "tn", BF16,"mm_gate_up_dw")
        dx1, st2 = _norm_mod_bwd(S["x1"], dh2, dx, vec(g_ffn[l]), m6, 4, n_ctx_tiles, tr, "norm_mod_bwd")
        dmo, dga1 = _resid_bwd(dx1, S["mo"], m6, 2, n_ctx_tiles, tr, "resid_bwd")
        dmg = _mm(dmo, W["out"], "nt", F32, "mm_out_dx")
        g_out = _mm(S["mg"], dmo, "tn", BF16,"mm_out_dw")
        do_ssd, do_pool, dgl = _merge_bwd(proj, 7, S["o_ssd"], S["o_pool"], dmg, tr, "merge_bwd")
        dyn = _mm(do_ssd, W["ssd"], "nt", F32, "mm_ssd_out_dx")
        g_ssd = _mm(S["yn"], do_ssd, "tn", BF16,"mm_ssd_out_dw")
        dpms = _mm(do_pool, W["po"], "nt", F32, "mm_pool_out_dx")
        g_po = _mm(S["pms"], do_pool, "tn", BF16,"mm_pool_out_dw")
        dpm, g_pool, dps = _pool_mix_bwd(S["pm"], dpms, W["pool"], vec(pool_scale[l]), tr, "pool_mix_bwd")
        dup = _pool_apply(dpm, 0, BF16, n_ctx, tr, pg, True, "pool_bwd")
        dy, dz, dnw = _gnorm_bwd(S["y2"], proj, vec(ssd_norm_w[l]), dyn, gs, tr, "gnorm_bwd")
        if pending is None:
            dxs2, db2, dc2, ddt4, dlc4, dlr4 = _ssd_bwd(S["xbc"], dy, S["states"], *S["scan_ops"], G, n_ctx,
                                                        ssd_gpb, "ssd_bwd")
        else:
            exch = _Exchange(pending, jax.ShapeDtypeStruct(pending.shape, BF16), _CHIP_SEMS, _chip_exchange_start,
                             _chip_exchange_finish)
            dxs2, db2, dc2, ddt4, dlc4, dlr4, slots[l + 1] = _ssd_bwd(
                S["xbc"], dy, S["states"], *S["scan_ops"], G, n_ctx, ssd_gpb, "ssd_bwd_exchange", exch)
        dskv = _pad_rows(jnp.repeat(d_skip[l], HEADDIM, axis=1), 8)
        dxbc_act, ddsk = _ssd_combine(dxs2, db2, dc2, dy, S["xbc"], dskv, tr, "ssd_combine")
        dxbc, dconv = _conv_bwd(proj, dinner // conv_tc, dxbc_act, W["conv8"], vec(conv_b[l]), n_ctx, conv_tc,
                                "conv_bwd")
        ddt_raw, dtst = _dt_bwd(proj, dt_blk, pad128(dt_bias[l]), pad128(a_log[l]), from4(ddt4),
                                from4(dlc4 + dlr4.transpose(0, 1, 3, 2)), H, tr, "dt_bwd")
        dproj = jnp.concatenate([dz, dxbc, dup, dgl, ddt_raw, jnp.zeros((T, NP - 9 * D - 128), BF16)], axis=1)
        dh = _mm(dproj, W["inT"], "nn", F32, "mm_in_dx")
        g_inT_new = _mm(dproj, S["h"], "tn", BF16,"mm_in_dw")
        dx0, st1 = _norm_mod_bwd(S["x0"], dh, dx1, vec(g_mix[l]), m6, 1, n_ctx_tiles, tr, "norm_mod_bwd")
        dm6 = _pad_rows(jnp.concatenate([st1[0:2], st1[2:4], dga1[0:2], st2[0:2], st2[2:4], dga2[0:2]], axis=1), 8)
        dsil = _mm(dm6, W["adaT"], "nn", F32, "mm_ada_dx")
        sil_b, dcc, dbada = _ada_bwd_small(cc8, dsil, dm6, "ada_bwd_small")
        g_adaT = _mm(dm6, sil_b, "tn", BF16,"mm_ada_dw")
        d_c_ctx = d_c_ctx + dcc[1]
        dx = dx0

        g_inT = jnp.concatenate([g_inT_new[:6 * D], g_inT_new[9 * D:9 * D + 2 * H], g_inT_new[6 * D:9 * D]], axis=0)
        big_rows[("w_ada", l)] = g_adaT.reshape(N_DEV, -1, D)
        big_rows[("w_in", l)] = g_inT.reshape(N_DEV, -1, D)
        big_rows[("conv_w", l)] = jnp.pad(
            dconv[:CONV_K].reshape(CONV_K, N_DEV, xbc_w // N_DEV).transpose(1, 0, 2),
            ((0, 0), (0, 16 - CONV_K), (0, 0))).reshape(N_DEV, -1, D)
        big_rows[("w_ssd_out", l)] = g_ssd.reshape(N_DEV, -1, D)
        big_rows[("pool_w", l)] = g_pool.reshape(len(POOL_WINDOWS), N_DEV, pg // N_DEV, pg).transpose(1, 0, 2, 3) \
            .reshape(N_DEV, -1, D)
        big_rows[("w_pool_out", l)] = g_po.reshape(N_DEV, -1, D)
        big_rows[("w_out", l)] = g_out.reshape(N_DEV, -1, D)
        big_rows[("w_gate_up", l)] = g_guT.reshape(N_DEV, -1, D)
        big_rows[("w_down", l)] = g_down.reshape(N_DEV, -1, D)
        small_g["b_ada"][l] = dbada[0]
        small_g["g_mix"][l] = st1[4]
        small_g["conv_b"][l] = dconv[CONV_K]
        small_g["dt_bias"][l] = dtst[0, :2 * H].reshape(2, H)
        small_g["a_log"][l] = dtst[1, :2 * H].reshape(2, H)
        dsk_h = ddsk[0].reshape(H, HEADDIM).sum(axis=-1)
        small_g["d_skip"][l] = jnp.stack([dsk_h, dsk_h])
        small_g["ssd_norm_w"][l] = dnw[0]
        small_g["pool_scale"][l] = dps[0]
        small_g["g_ffn"][l] = st2[4]

        lay = lays[l]
        gparts = [jnp.pad(big_rows[(n, l)].astype(BF16), ((0, 0), (0, pr - rows), (0, 0)))
                  for n, _, rows, pr, _ in lay.pieces]
        gparts.append(jnp.zeros((N_DEV, lay.rows - sum(p[3] for p in lay.pieces), D), BF16))
        gbuf = jnp.concatenate(gparts, axis=1)
        gbuf = gbuf.reshape(2, 2, 2, lay.rows, D).transpose(2, 0, 1, 3, 4).reshape(2, 4, lay.rows, D)
        got = _pair_exchange(gbuf, "rs_pair_exchange")
        pending = _pair_add(gbuf, got, lax.axis_index("c").astype(jnp.int32).reshape(1), "rs_pair_add")
    slots[0] = _chip_exchange(pending, "rs_chip_exchange")
    g_local = [_sum_slots(s, "rs_chip_add") for s in slots]
    grad_x = dx[n_ctx:][None]

    def local_grad(name):
        outs = []
        for l in range(depth):
            _, _, rows, _, off = lays[l].find(name, l)
            piece = g_local[l][off:off + rows]
            if name in ("w_ada", "w_in", "w_gate_up"):
                piece = piece.T
            elif name == "conv_w":
                piece = piece.reshape(16, -1)[:CONV_K]
            elif name == "pool_w":
                piece = piece.reshape(weights[name].shape[1:])
            outs.append(piece)
        return jnp.stack(outs)

    grads = {n: local_grad(n) for n in big}

    small_full = {"c_ctx": d_c_ctx, "g_final": dgf[0]}
    for n in small:
        if n not in small_full:
            small_full[n] = jnp.stack(small_g[n])

    def pack_small(tree):
        flat = jnp.concatenate([tree[n].reshape(-1).astype(F32) for n in small])
        rows = _round_up(-(-flat.shape[0] // D), 8)
        return jnp.pad(flat, (0, rows * D - flat.shape[0])).reshape(rows, D)

    def unpack_small(buf):
        flat, out, off = buf.reshape(-1), {}, 0
        for n in small:
            sz = weights[n].size
            out[n] = flat[off:off + sz].reshape(weights[n].shape)
            off += sz
        return out

    g_small = _sum_slots(_all_gather(pack_small(small_full), "gather_small_grads"), "sum_small_grads")
    grads.update(unpack_small(g_small))

    delta, new_m, new_v = {}, {}, {}
    for n in big:
        shp = weights[n].shape
        d_, m_, v_ = _adamw(weights[n].reshape(-1, shp[-1]), grads[n].reshape(-1, shp[-1]),
                            moms_m[n].reshape(-1, shp[-1]), moms_v[n].reshape(-1, shp[-1]), "adamw_" + n)
        delta[n], new_m[n], new_v[n] = d_.reshape(shp), m_.reshape(shp), v_.reshape(shp)
    d_, m_, v_ = _adamw(pack_small(weights), g_small, pack_small(moms_m), pack_small(moms_v), "adamw_small")
    delta.update(unpack_small(d_))
    new_m.update(unpack_small(m_))
    new_v.update(unpack_small(v_))

    return (loss, grad_x, *[grads[n] for n in order], *[delta[n] for n in order],
            *[new_m[n] for n in order], *[new_v[n] for n in order])
```

```python
import functools

import jax
import jax.numpy as jnp
from jax import lax
from jax.experimental import pallas as pl
from jax.experimental.pallas import tpu as pltpu

F32 = jnp.float32
BF16 = jnp.bfloat16
N_DEV = 8
EPS = 1e-6
GRID_W = 64
POOL_WINDOWS = (2, 4, 8, 16)
HEADDIM = 64
STATE = 128
CHUNK = 128
HPG = 4
CONV_K = 5
ADAM_LR, ADAM_B1, ADAM_B2, ADAM_EPS, ADAM_WD, ADAM_STEP = 0.001, 0.9, 0.999, 1e-08, 0.01, 10
NEG_BIG = -1e30
MESH_AXES = ("x", "y", "c")
ANY = pl.BlockSpec(memory_space=pl.ANY)


def _tile(n, cands):
    for t in cands:
        if n % t == 0:
            return t
    return n


def _round_up(n, m):
    return -(-n // m) * m


def _silu(x):
    return x * jax.nn.sigmoid(x)


def _dsilu(x):
    s = jax.nn.sigmoid(x)
    return s * (1.0 + x * (1.0 - s))


def _cparams(sem):
    return pltpu.CompilerParams(dimension_semantics=sem, vmem_limit_bytes=56 * 1024 * 1024)


def _mm(a, b, mode, out_dtype, name):
    if mode == "tn":
        K, M = a.shape
        N = b.shape[1]
        tm = _tile(M, (512, 256, 128))
        tn = _tile(N, (1024, 512, 256, 128))
        tk = _tile(K, (1088, 544, 512, 256, 128))
        a_spec = pl.BlockSpec((tk, tm), lambda i, j, k: (k, i))
        b_spec = pl.BlockSpec((tk, tn), lambda i, j, k: (k, j))
        dims = (((0,), (0,)), ((), ()))
    else:
        M, K = a.shape
        N = b.shape[0] if mode == "nt" else b.shape[1]
        tm = _tile(M, (1088, 544, 512, 256, 128))
        tk = K if K <= 4096 else max(t for t in range(128, 2817, 128) if K % t == 0)
        tn = _tile(N, (512, 256, 128)) if tk == K else _tile(N, (1024, 512, 256, 128))
        a_spec = pl.BlockSpec((tm, tk), lambda i, j, k: (i, k))
        if mode == "nt":
            b_spec = pl.BlockSpec((tn, tk), lambda i, j, k: (j, k))
            dims = (((1,), (1,)), ((), ()))
        else:
            b_spec = pl.BlockSpec((tk, tn), lambda i, j, k: (k, j))
            dims = (((1,), (0,)), ((), ()))
    nk = K // tk

    def body(a_ref, b_ref, o_ref, *acc):
        if nk == 1:
            o_ref[...] = lax.dot_general(a_ref[...].astype(BF16), b_ref[...].astype(BF16), dims,
                                         preferred_element_type=F32).astype(o_ref.dtype)
            return
        k = pl.program_id(2)

        @pl.when(k == 0)
        def _():
            acc[0][...] = jnp.zeros_like(acc[0])

        acc[0][...] += lax.dot_general(a_ref[...].astype(BF16), b_ref[...].astype(BF16), dims,
                                       preferred_element_type=F32)

        @pl.when(k == nk - 1)
        def _():
            o_ref[...] = acc[0][...].astype(o_ref.dtype)

    return pl.pallas_call(
        body, name=name, grid=(M // tm, N // tn, nk),
        in_specs=[a_spec, b_spec], out_specs=pl.BlockSpec((tm, tn), lambda i, j, k: (i, j)),
        out_shape=jax.ShapeDtypeStruct((M, N), out_dtype),
        scratch_shapes=[pltpu.VMEM((tm, tn), F32)] if nk > 1 else [],
        compiler_params=_cparams(("parallel", "parallel", "arbitrary")),
    )(a, b)


def _ada_fwd(cc8, w_adaT, b_ada, name):
    D = cc8.shape[1]
    N = w_adaT.shape[0]
    tn = _tile(N, (512, 256, 128))

    def body(c_ref, w_ref, b_ref, o_ref):
        a = _silu(c_ref[...]).astype(BF16)
        o_ref[...] = lax.dot_general(a, w_ref[...], (((1,), (1,)), ((), ())),
                                     preferred_element_type=F32) + b_ref[...]

    return pl.pallas_call(
        body, name=name, grid=(N // tn,),
        in_specs=[pl.BlockSpec((8, D), lambda j: (0, 0)), pl.BlockSpec((tn, D), lambda j: (j, 0)),
                  pl.BlockSpec((1, tn), lambda j: (0, j))],
        out_specs=pl.BlockSpec((8, tn), lambda j: (0, j)),
        out_shape=jax.ShapeDtypeStruct((8, N), F32),
        compiler_params=_cparams(("parallel",)),
    )(cc8, w_adaT, b_ada)


def _ada_bwd_small(cc8, dsil, dm6, name):
    D = cc8.shape[1]
    N = dm6.shape[1]

    def body(c_ref, ds_ref, dm_ref, sil_ref, dc_ref, db_ref):
        c = c_ref[...]
        sil_ref[...] = _silu(c).astype(BF16)
        dc_ref[...] = ds_ref[...] * _dsilu(c)
        dm = dm_ref[...]
        row = lax.broadcasted_iota(jnp.int32, dm.shape, 0)
        db_ref[...] = jnp.where(row == 0, jnp.sum(dm, axis=0, keepdims=True), 0.0)

    return pl.pallas_call(
        body, name=name, grid=(1,),
        in_specs=[pl.BlockSpec((8, D), lambda i: (0, 0)), pl.BlockSpec((8, D), lambda i: (0, 0)),
                  pl.BlockSpec((8, N), lambda i: (0, 0))],
        out_specs=[pl.BlockSpec((8, D), lambda i: (0, 0)), pl.BlockSpec((8, D), lambda i: (0, 0)),
                   pl.BlockSpec((8, N), lambda i: (0, 0))],
        out_shape=[jax.ShapeDtypeStruct((8, D), BF16), jax.ShapeDtypeStruct((8, D), F32),
                   jax.ShapeDtypeStruct((8, N), F32)],
        compiler_params=_cparams(("arbitrary",)),
    )(cc8, dsil, dm6)


def _seg_pick(m_ref, is_ctx):
    return jnp.where(is_ctx, m_ref[1:2, :], m_ref[0:1, :])


def _norm_mod(x, g, m6, sh_idx, sc_idx, n_ctx_tiles, tr, name, resid=None):
    T, D = x.shape
    row = pl.BlockSpec((tr, D), lambda i: (i, 0))
    vec = pl.BlockSpec((1, D), lambda i: (0, 0))

    def mcol(idx):
        return pl.BlockSpec((8, D), lambda i: (0, idx))

    def body(*refs):
        if resid is None:
            x_ref, g_ref, sh_ref, sc_ref, h_ref = refs
            xv = x_ref[...]
        else:
            x_ref, f_ref, ga_ref, g_ref, sh_ref, sc_ref, xo_ref, h_ref = refs
        is_ctx = pl.program_id(0) < n_ctx_tiles
        if resid is not None:
            xv = x_ref[...] + _seg_pick(ga_ref, is_ctx) * f_ref[...]
            xo_ref[...] = xv
        rstd = lax.rsqrt(jnp.mean(xv * xv, axis=-1, keepdims=True) + EPS)
        hn = xv * rstd * g_ref[...]
        h_ref[...] = (hn * (1.0 + _seg_pick(sc_ref, is_ctx)) + _seg_pick(sh_ref, is_ctx)).astype(BF16)

    if resid is None:
        ins, in_specs = [x, g, m6, m6], [row, vec, mcol(sh_idx), mcol(sc_idx)]
        out_specs, out_shape = row, jax.ShapeDtypeStruct((T, D), BF16)
    else:
        f, ga_idx = resid
        ins = [x, f, m6, g, m6, m6]
        in_specs = [row, row, mcol(ga_idx), vec, mcol(sh_idx), mcol(sc_idx)]
        out_specs = [row, row]
        out_shape = [jax.ShapeDtypeStruct((T, D), F32), jax.ShapeDtypeStruct((T, D), BF16)]
    return pl.pallas_call(body, name=name, grid=(T // tr,), in_specs=in_specs, out_specs=out_specs,
                          out_shape=out_shape, compiler_params=_cparams(("parallel",)))(*ins)


def _resid(x, f, m6, ga_idx, n_ctx_tiles, tr, name):
    T, D = x.shape
    row = pl.BlockSpec((tr, D), lambda i: (i, 0))

    def body(x_ref, f_ref, ga_ref, o_ref):
        is_ctx = pl.program_id(0) < n_ctx_tiles
        o_ref[...] = x_ref[...] + _seg_pick(ga_ref, is_ctx) * f_ref[...]

    return pl.pallas_call(body, name=name, grid=(T // tr,),
                          in_specs=[row, row, pl.BlockSpec((8, D), lambda i: (0, ga_idx))], out_specs=row,
                          out_shape=jax.ShapeDtypeStruct((T, D), F32),
                          compiler_params=_cparams(("parallel",)))(x, f, m6)


def _resid_bwd(dx, f, m6, ga_idx, n_ctx_tiles, tr, name):
    T, D = dx.shape
    row = pl.BlockSpec((tr, D), lambda i: (i, 0))
    acc = pl.BlockSpec((8, D), lambda i: (0, 0))

    def body(dx_ref, f_ref, ga_ref, df_ref, dga_ref):
        i = pl.program_id(0)
        is_ctx = i < n_ctx_tiles

        @pl.when(i == 0)
        def _():
            dga_ref[...] = jnp.zeros_like(dga_ref)

        dxv = dx_ref[...]
        df_ref[...] = (_seg_pick(ga_ref, is_ctx) * dxv).astype(BF16)
        s = jnp.sum(dxv * f_ref[...], axis=0, keepdims=True)
        r = lax.broadcasted_iota(jnp.int32, (8, D), 0)
        dga_ref[...] += jnp.where(r == jnp.where(is_ctx, 1, 0), s, 0.0)

    return pl.pallas_call(body, name=name, grid=(T // tr,),
                          in_specs=[row, row, pl.BlockSpec((8, D), lambda i: (0, ga_idx))],
                          out_specs=[row, acc],
                          out_shape=[jax.ShapeDtypeStruct((T, D), BF16), jax.ShapeDtypeStruct((8, D), F32)],
                          compiler_params=_cparams(("arbitrary",)))(dx, f, m6)


def _norm_mod_bwd(x, dh, dxres, g, m6, sc_idx, n_ctx_tiles, tr, name):
    T, D = x.shape
    row = pl.BlockSpec((tr, D), lambda i: (i, 0))
    acc = pl.BlockSpec((8, D), lambda i: (0, 0))

    def body(x_ref, dh_ref, dr_ref, g_ref, sc_ref, dx_ref, st_ref):
        i = pl.program_id(0)
        is_ctx = i < n_ctx_tiles

        @pl.when(i == 0)
        def _():
            st_ref[...] = jnp.zeros_like(st_ref)

        xv, dh_v, gv = x_ref[...], dh_ref[...], g_ref[...]
        sc1 = 1.0 + _seg_pick(sc_ref, is_ctx)
        rstd = lax.rsqrt(jnp.mean(xv * xv, axis=-1, keepdims=True) + EPS)
        xhat = xv * rstd
        dxhat = dh_v * sc1 * gv
        dx_ref[...] = dr_ref[...] + rstd * (dxhat - xhat * jnp.mean(dxhat * xhat, axis=-1, keepdims=True))
        dsh = jnp.sum(dh_v, axis=0, keepdims=True)
        dsc = jnp.sum(dh_v * xhat * gv, axis=0, keepdims=True)
        dg = jnp.sum(dh_v * sc1 * xhat, axis=0, keepdims=True)
        r = lax.broadcasted_iota(jnp.int32, (8, D), 0)
        seg = jnp.where(is_ctx, 1, 0)
        st_ref[...] += (jnp.where(r == seg, dsh, 0.0) + jnp.where(r == 2 + seg, dsc, 0.0)
                        + jnp.where(r == 4, dg, 0.0))

    return pl.pallas_call(body, name=name, grid=(T // tr,),
                          in_specs=[row, row, row, pl.BlockSpec((1, D), lambda i: (0, 0)),
                                    pl.BlockSpec((8, D), lambda i: (0, sc_idx))],
                          out_specs=[row, acc],
                          out_shape=[jax.ShapeDtypeStruct((T, D), F32), jax.ShapeDtypeStruct((8, D), F32)],
                          compiler_params=_cparams(("arbitrary",)))(x, dh, dxres, g, m6)


def _loss_head(x, tgt, g, n_ctx_tiles, tr, name):
    T, D = x.shape
    row = pl.BlockSpec((tr, D), lambda i: (i, 0))

    def body(x_ref, t_ref, g_ref, l_ref, dx_ref, dg_ref):
        i = pl.program_id(0)

        @pl.when(i == 0)
        def _():
            l_ref[...] = jnp.zeros_like(l_ref)
            dg_ref[...] = jnp.zeros_like(dg_ref)

        @pl.when(i < n_ctx_tiles)
        def _():
            dx_ref[...] = jnp.zeros_like(dx_ref)

        @pl.when(i >= n_ctx_tiles)
        def _():
            xv, gv = x_ref[...], g_ref[...]
            rstd = lax.rsqrt(jnp.mean(xv * xv, axis=-1, keepdims=True) + EPS)
            xhat = xv * rstd
            e = xhat * gv - t_ref[...]
            l_ref[...] += 0.5 * jnp.sum(jnp.mean(e * e, axis=-1, keepdims=True), axis=0, keepdims=True)
            dy = e * (1.0 / D)
            dxhat = dy * gv
            dx_ref[...] = rstd * (dxhat - xhat * jnp.mean(dxhat * xhat, axis=-1, keepdims=True))
            r = lax.broadcasted_iota(jnp.int32, (8, D), 0)
            dg_ref[...] += jnp.where(r == 0, jnp.sum(dy * xhat, axis=0, keepdims=True), 0.0)

    return pl.pallas_call(
        body, name=name, grid=(T // tr,),
        in_specs=[row, pl.BlockSpec((tr, D), lambda i: (jnp.maximum(i - n_ctx_tiles, 0), 0)),
                  pl.BlockSpec((1, D), lambda i: (0, 0))],
        out_specs=[pl.BlockSpec((8, 128), lambda i: (0, 0)), row, pl.BlockSpec((8, D), lambda i: (0, 0))],
        out_shape=[jax.ShapeDtypeStruct((8, 128), F32), jax.ShapeDtypeStruct((T, D), F32),
                   jax.ShapeDtypeStruct((8, D), F32)],
        compiler_params=_cparams(("arbitrary",)))(x, tgt, g)


def _seq_masks(T, n_ctx, width):
    row = lax.broadcasted_iota(jnp.int32, (T, width), 0)
    in_ctx = row < n_ctx
    return jnp.where(in_ctx, row, row - n_ctx), jnp.where(in_ctx, n_ctx, T - n_ctx)


def _shift_rows(u, off, t_loc, seg_len):
    T = u.shape[0]
    if off == 0:
        return u
    v = pltpu.roll(u, (-off) % T, 0)
    ok = (t_loc + off >= 0) & (t_loc + off < seg_len)
    return jnp.where(ok, v, 0.0)


def _conv_fwd(proj, col0_blk, ncol, conv_w8, conv_b, n_ctx, tc, name):
    T = proj.shape[0]

    def body(u_ref, w_ref, b_ref, o_ref):
        u = u_ref[...]
        t_loc, seg_len = _seq_masks(T, n_ctx, tc)
        acc = jnp.broadcast_to(b_ref[...], u.shape)
        for i in range(CONV_K):
            acc = acc + w_ref[i:i + 1, :] * _shift_rows(u, i - CONV_K // 2, t_loc, seg_len)
        o_ref[...] = _silu(acc)

    return pl.pallas_call(
        body, name=name, grid=(ncol // tc,),
        in_specs=[pl.BlockSpec((T, tc), lambda j: (0, col0_blk + j)), pl.BlockSpec((8, tc), lambda j: (0, j)),
                  pl.BlockSpec((1, tc), lambda j: (0, j))],
        out_specs=pl.BlockSpec((T, tc), lambda j: (0, j)),
        out_shape=jax.ShapeDtypeStruct((T, ncol), F32),
        compiler_params=_cparams(("parallel",)))(proj, conv_w8, conv_b)


def _conv_bwd(proj, col0_blk, d2, w_blk0, conv_w8, conv_b, n_ctx, tc, name, skip=None):
    T, ncol = d2.shape[1], d2.shape[2]

    def body(u_ref, d_ref, w_ref, b_ref, *rest):
        if skip is None:
            du_ref, dw_ref = rest
        else:
            dy_ref, k_ref, du_ref, dw_ref = rest
        u = u_ref[...]
        t_loc, seg_len = _seq_masks(T, n_ctx, tc)
        pre = jnp.broadcast_to(b_ref[...], u.shape)
        for i in range(CONV_K):
            pre = pre + w_ref[i:i + 1, :] * _shift_rows(u, i - CONV_K // 2, t_loc, seg_len)
        r = lax.broadcasted_iota(jnp.int32, (8, tc), 0)
        dact = d_ref[0] + d_ref[1]
        dw = jnp.zeros((8, tc), F32)
        if skip is not None:
            dyv = dy_ref[...]
            dact = dact + (k_ref[0:1, :] + k_ref[1:2, :]) * dyv
            dw = jnp.where(r == CONV_K + 1, jnp.sum(dyv * _silu(pre), axis=0, keepdims=True), 0.0)
        dpre = dact * _dsilu(pre)
        du = jnp.zeros_like(u)
        dw = dw + jnp.where(r == CONV_K, jnp.sum(dpre, axis=0, keepdims=True), 0.0)
        for i in range(CONV_K):
            off = i - CONV_K // 2
            du = du + w_ref[i:i + 1, :] * _shift_rows(dpre, -off, t_loc, seg_len)
            dw = dw + jnp.where(r == i, jnp.sum(dpre * _shift_rows(u, off, t_loc, seg_len), axis=0, keepdims=True),
                                0.0)
        du_ref[...] = du.astype(BF16)
        dw_ref[...] = dw

    col = pl.BlockSpec((T, tc), lambda j: (0, j))
    in_specs = [pl.BlockSpec((T, tc), lambda j: (0, col0_blk + j)), pl.BlockSpec((2, T, tc), lambda j: (0, 0, j)),
                pl.BlockSpec((8, tc), lambda j: (0, w_blk0 + j)), pl.BlockSpec((1, tc), lambda j: (0, w_blk0 + j))]
    operands = [proj, d2, conv_w8, conv_b]
    if skip is not None:
        in_specs += [col, pl.BlockSpec((8, tc), lambda j: (0, j))]
        operands += list(skip)
    return pl.pallas_call(
        body, name=name, grid=(ncol // tc,), in_specs=in_specs,
        out_specs=[col, pl.BlockSpec((8, tc), lambda j: (0, j))],
        out_shape=[jax.ShapeDtypeStruct((T, ncol), BF16), jax.ShapeDtypeStruct((8, ncol), F32)],
        compiler_params=_cparams(("parallel",)))(*operands)


def _pool_core(u, half, t_loc, seg_len, transpose):
    tr = u.shape[0]

    def shift(v, s):
        w = pltpu.roll(v, s % tr, 0)
        ok = (t_loc - s >= 0) & (t_loc - s < seg_len)
        return jnp.where(ok, w, 0.0)

    cnt = (jnp.minimum(t_loc, half) + jnp.minimum(seg_len - t_loc, half)).astype(F32)
    q = u / cnt if transpose else u
    back, ahead, h = q, q, 1
    while h < half:
        back = back + shift(back, h)
        ahead = ahead + shift(ahead, -h)
        h *= 2
    if transpose:
        tot = back + shift(ahead, -1)
        return tot - u
    tot = shift(back, 1) + ahead
    return tot / cnt - u


def _pool_apply(src, col0_blk, out_dtype, n_ctx, tr, pg, transpose, name):
    T = src.shape[0]
    n_ctx_tiles = n_ctx // tr

    def body(u_ref, o_ref):
        i, gi = pl.program_id(0), pl.program_id(1)
        row = lax.broadcasted_iota(jnp.int32, (tr, pg), 0)
        seg_len = jnp.where(i < n_ctx_tiles, tr, GRID_W)
        t_loc = row & (seg_len - 1)
        u = u_ref[...].astype(F32)
        for k_idx, k in enumerate(POOL_WINDOWS):
            @pl.when(gi == k_idx)
            def _(k=k):
                o_ref[...] = _pool_core(u, k // 2, t_loc, seg_len, transpose).astype(o_ref.dtype)

    return pl.pallas_call(
        body, name=name, grid=(T // tr, len(POOL_WINDOWS)),
        in_specs=[pl.BlockSpec((tr, pg), lambda i, gi: (i, col0_blk + gi))],
        out_specs=pl.BlockSpec((tr, pg), lambda i, gi: (i, gi)),
        out_shape=jax.ShapeDtypeStruct((T, pg * len(POOL_WINDOWS)), out_dtype),
        compiler_params=_cparams(("parallel", "parallel")))(src)


def _pool_mix_fwd(pm, pool_w, pool_scale, tr, name):
    T, W = pm.shape
    ng, pg = pool_w.shape[0], pool_w.shape[1]

    def body(p_ref, w_ref, s_ref, o_ref):
        o_ref[...] = (jnp.dot(p_ref[...], w_ref[...], preferred_element_type=F32) * s_ref[...]).astype(BF16)

    return pl.pallas_call(
        body, name=name, grid=(T // tr, ng),
        in_specs=[pl.BlockSpec((tr, pg), lambda i, g: (i, g)), pl.BlockSpec((None, pg, pg), lambda i, g: (g, 0, 0)),
                  pl.BlockSpec((1, pg), lambda i, g: (0, g))],
        out_specs=pl.BlockSpec((tr, pg), lambda i, g: (i, g)),
        out_shape=jax.ShapeDtypeStruct((T, W), BF16),
        compiler_params=_cparams(("parallel", "parallel")))(pm, pool_w, pool_scale)


def _pool_mix_bwd(pm, dpms, pool_w, pool_scale, tr, name):
    T, W = pm.shape
    ng, pg = pool_w.shape[0], pool_w.shape[1]

    def body(p_ref, d_ref, w_ref, s_ref, dp_ref, dw_ref, ds_ref):
        i = pl.program_id(1)

        @pl.when(i == 0)
        def _():
            dw_ref[...] = jnp.zeros_like(dw_ref)
            ds_ref[...] = jnp.zeros_like(ds_ref)

        p, w = p_ref[...], w_ref[...]
        d = d_ref[...].astype(F32)
        pmix = jnp.dot(p, w, preferred_element_type=F32)
        r = lax.broadcasted_iota(jnp.int32, (8, pg), 0)
        ds_ref[...] += jnp.where(r == 0, jnp.sum(d * pmix, axis=0, keepdims=True), 0.0)
        dmix = (d * s_ref[...]).astype(BF16)
        dp_ref[...] = lax.dot_general(dmix, w, (((1,), (1,)), ((), ())), preferred_element_type=F32)
        dw_ref[...] += lax.dot_general(p, dmix, (((0,), (0,)), ((), ())), preferred_element_type=F32)

    return pl.pallas_call(
        body, name=name, grid=(ng, T // tr),
        in_specs=[pl.BlockSpec((tr, pg), lambda g, i: (i, g)), pl.BlockSpec((tr, pg), lambda g, i: (i, g)),
                  pl.BlockSpec((None, pg, pg), lambda g, i: (g, 0, 0)), pl.BlockSpec((1, pg), lambda g, i: (0, g))],
        out_specs=[pl.BlockSpec((tr, pg), lambda g, i: (i, g)), pl.BlockSpec((None, pg, pg), lambda g, i: (g, 0, 0)),
                   pl.BlockSpec((8, pg), lambda g, i: (0, g))],
        out_shape=[jax.ShapeDtypeStruct((T, W), F32), jax.ShapeDtypeStruct((ng, pg, pg), F32),
                   jax.ShapeDtypeStruct((8, W), F32)],
        compiler_params=_cparams(("parallel", "arbitrary")))(pm, dpms, pool_w, pool_scale)


def _chunk_cumsum(v, upper):
    Q = v.shape[0]
    ii = lax.broadcasted_iota(jnp.int32, (Q, Q), 0)
    jj = lax.broadcasted_iota(jnp.int32, (Q, Q), 1)
    tri = ((jj >= ii) if upper else (jj <= ii)).astype(BF16)
    h1 = v.astype(BF16)
    r1 = v - h1.astype(F32)
    h2 = r1.astype(BF16)
    h3 = (r1 - h2.astype(F32)).astype(BF16)
    return (jnp.dot(tri, h1, preferred_element_type=F32) + jnp.dot(tri, h2, preferred_element_type=F32)
            + jnp.dot(tri, h3, preferred_element_type=F32))


def _split3(v):
    h1 = v.astype(BF16)
    r1 = v - h1.astype(F32)
    h2 = r1.astype(BF16)
    return h1, h2, (r1 - h2.astype(F32)).astype(BF16)


def _dt_prep(proj, dt_blk, bias, a_log, expand, n_heads, name):
    T = proj.shape[0]
    Wd = expand.shape[1]
    Q = CHUNK
    row = pl.BlockSpec((Q, 128), lambda i: (i, 0))
    wide = pl.BlockSpec((Q, Wd), lambda i: (i, 0))

    def body(r_ref, b_ref, al_ref, e_ref, l1_ref, l2_ref, l3_ref, dtb_ref, ein_ref, dte_ref, etot_ref):
        xv = r_ref[...] + b_ref[...]
        dt = jnp.maximum(xv, 0.0) + jnp.log(1.0 + jnp.exp(-jnp.abs(xv)))
        a = -jnp.exp(al_ref[...]) * dt
        fwd_col = lax.broadcasted_iota(jnp.int32, (Q, 128), 1) < n_heads
        lam = jnp.where(fwd_col, _chunk_cumsum(a, False), _chunk_cumsum(a, True))
        tot = jnp.where(fwd_col[0:1], lam[Q - 1:Q], lam[0:1])
        l1_ref[...], l2_ref[...], l3_ref[...] = _split3(lam)
        etot_ref[...] = jnp.broadcast_to(jnp.exp(tot), (8, 128))
        ex = e_ref[...]

        def rep(v):
            p1, p2, p3 = _split3(v)
            return (jnp.dot(p1, ex, preferred_element_type=F32) + jnp.dot(p2, ex, preferred_element_type=F32)
                    + jnp.dot(p3, ex, preferred_element_type=F32))

        dtb_ref[...] = rep(dt)
        ein_ref[...] = rep(jnp.exp(lam))
        dte_ref[...] = rep(jnp.exp(tot - lam))

    vec = pl.BlockSpec((1, 128), lambda i: (0, 0))
    return pl.pallas_call(
        body, name=name, grid=(T // Q,),
        in_specs=[pl.BlockSpec((Q, 128), lambda i: (i, dt_blk)), vec, vec, pl.BlockSpec((128, Wd), lambda i: (0, 0))],
        out_specs=[row, row, row, wide, wide, wide, pl.BlockSpec((8, 128), lambda i: (i, 0))],
        out_shape=[jax.ShapeDtypeStruct((T, 128), BF16)] * 3 + [jax.ShapeDtypeStruct((T, Wd), F32)] * 3
        + [jax.ShapeDtypeStruct((T // Q * 8, 128), F32)],
        compiler_params=_cparams(("parallel",)))(proj, bias, a_log, expand)


def _dt_bwd(proj, dt_blk, bias, a_log, ddt, dlam, n_heads, tr, name):
    T = proj.shape[0]
    row = pl.BlockSpec((tr, 128), lambda i: (i, 0))
    vec = pl.BlockSpec((1, 128), lambda i: (0, 0))

    def body(r_ref, b_ref, al_ref, ddt_ref, dl_ref, o_ref, st_ref):
        @pl.when(pl.program_id(0) == 0)
        def _():
            st_ref[...] = jnp.zeros_like(st_ref)

        xv = r_ref[...] + b_ref[...]
        dt = jnp.maximum(xv, 0.0) + jnp.log(1.0 + jnp.exp(-jnp.abs(xv)))
        a_neg = -jnp.exp(al_ref[...])
        col = lax.broadcasted_iota(jnp.int32, (CHUNK, 128), 1)
        dl = dl_ref[...]
        parts = []
        for k in range(tr // CHUNK):
            dk = dl[k * CHUNK:(k + 1) * CHUNK]
            parts.append(jnp.where(col < n_heads, _chunk_cumsum(dk, True), _chunk_cumsum(dk, False)))
        dav = jnp.concatenate(parts, axis=0)
        draw = (ddt_ref[...] + dav * a_neg) * jax.nn.sigmoid(xv)
        o_ref[...] = draw.astype(BF16)
        r = lax.broadcasted_iota(jnp.int32, (8, 128), 0)
        st_ref[...] += (jnp.where(r == 0, jnp.sum(draw, axis=0, keepdims=True), 0.0)
                        + jnp.where(r == 1, jnp.sum(dav * dt, axis=0, keepdims=True) * a_neg, 0.0))

    return pl.pallas_call(
        body, name=name, grid=(T // tr,),
        in_specs=[pl.BlockSpec((tr, 128), lambda i: (i, dt_blk)), vec, vec, row, row],
        out_specs=[row, pl.BlockSpec((8, 128), lambda i: (0, 0))],
        out_shape=[jax.ShapeDtypeStruct((T, 128), BF16), jax.ShapeDtypeStruct((8, 128), F32)],
        compiler_params=_cparams(("arbitrary",)))(proj, bias, a_log, ddt, dlam)


def _scan_chunk(d, pos, nc_ctx, nc):
    rev = jnp.where(pos < nc_ctx, nc_ctx - 1 - pos, nc - 1 - (pos - nc_ctx))
    return jnp.where(d == 0, pos, rev)


def _chunk_mask(d):
    ii = lax.broadcasted_iota(jnp.int32, (CHUNK, CHUNK), 0)
    jj = lax.broadcasted_iota(jnp.int32, (CHUNK, CHUNK), 1)
    return (ii - jj) * jnp.where(d == 0, 1, -1) >= 0


def _ssd_specs(T, G, n_ctx, gpb, chunk_of):
    R, P, N, Q = HPG, HEADDIM, STATE, CHUNK
    H = G * R
    nc, nc_ctx = T // Q, n_ctx // Q
    xw, bw = gpb * R * P, gpb * N
    b_blk0 = (H * P) // bw
    c_blk0 = b_blk0 + G // gpb

    def ch(d, s):
        return chunk_of(d, s, nc_ctx, nc)

    return dict(
        x=pl.BlockSpec((Q, xw), lambda d, g, s: (ch(d, s), g)),
        b=pl.BlockSpec((Q, bw), lambda d, g, s: (ch(d, s), b_blk0 + g)),
        c=pl.BlockSpec((Q, bw), lambda d, g, s: (ch(d, s), c_blk0 + g)),
        col=pl.BlockSpec((None, gpb, Q, R), lambda d, g, s: (d, g, ch(d, s), 0)),
        row=pl.BlockSpec((None, gpb, R, Q), lambda d, g, s: (d, g, 0, ch(d, s))),
        rep=pl.BlockSpec((Q, xw), lambda d, g, s: (ch(d, s), d * (G // gpb) + g)),
        lam_a=pl.BlockSpec((None, gpb * R, 16, Q), lambda d, g, s: (d, g, 0, ch(d, s))),
        lam_b=pl.BlockSpec((None, gpb * R, 16, Q), lambda d, g, s: (d, g, 0, ch(d, s))),
        etot=pl.BlockSpec((None, gpb, None, 8, 128), lambda d, g, s: (d, g, ch(d, s), 0, 0)),
        dsk=pl.BlockSpec((None, 8, xw), lambda d, g, s: (d, 0, g)),
        xd=pl.BlockSpec((None, Q, xw), lambda d, g, s: (d, ch(d, s), g)),
        bd=pl.BlockSpec((None, Q, bw), lambda d, g, s: (d, ch(d, s), g)),
        st=pl.BlockSpec((None, None, gpb * R // 2, 2 * P, N), lambda d, g, s: (d, ch(d, s), g, 0, 0)),
    )


class _Exchange:
    def __init__(self, operand, out_sds, sems, start, finish):
        self.operand, self.out_sds, self.sems, self.start, self.finish = operand, out_sds, sems, start, finish


def _call_with_exchange(body, exch, *, name, grid, in_specs, out_specs, out_shape, scratch_shapes, operands):
    if exch is None:
        return pl.pallas_call(body, name=name, grid=grid, in_specs=in_specs, out_specs=out_specs,
                              out_shape=out_shape, scratch_shapes=scratch_shapes,
                              compiler_params=_cparams(("arbitrary",) * len(grid)))(*operands)
    n_in, n_out, n_scr = len(in_specs), len(out_specs), len(scratch_shapes)

    def fused(*refs):
        ins, c_in = refs[:n_in], refs[n_in]
        outs, c_out = refs[n_in + 1:n_in + 1 + n_out], refs[n_in + 1 + n_out]
        scr = refs[n_in + 2 + n_out:n_in + 2 + n_out + n_scr]
        sems = refs[n_in + 2 + n_out + n_scr:]
        ids = [pl.program_id(a) for a in range(len(grid))]
        first = functools.reduce(lambda p, q: p & q, [i == 0 for i in ids])
        last = functools.reduce(lambda p, q: p & q, [i == n - 1 for i, n in zip(ids, grid)])

        @pl.when(first)
        def _():
            exch.start(c_in, c_out, *sems)

        body(*ins, *outs, *scr)

        @pl.when(last)
        def _():
            exch.finish(c_in, c_out, *sems)

    return pl.pallas_call(fused, name=name, grid=grid, in_specs=list(in_specs) + [ANY],
                          out_specs=list(out_specs) + [ANY], out_shape=list(out_shape) + [exch.out_sds],
                          scratch_shapes=list(scratch_shapes) + list(exch.sems),
                          compiler_params=_cparams(("arbitrary",) * len(grid)))(*operands, exch.operand)


def _ssd_fwd(xbc, dtb, ein, dte, lam_a, lam_b, etot, dsk, G, n_ctx, gpb, name, exch=None):
    T = xbc.shape[0]
    R, P, N, Q = HPG, HEADDIM, STATE, CHUNK
    H = G * R
    nc = T // Q
    sp = _ssd_specs(T, G, n_ctx, gpb, _scan_chunk)

    def body(x_ref, b_ref, c_ref, dt_ref, ein_ref, dte_ref, la_ref, lb_ref, et_ref, dsk_ref, y_ref, st_ref, S):
        d, s = pl.program_id(0), pl.program_id(2)

        @pl.when(s == 0)
        def _():
            S[...] = jnp.zeros_like(S)

        mask = _chunk_mask(d)
        head0 = lax.broadcasted_iota(jnp.int32, (Q, 2 * P), 1) < P
        rows0 = lax.broadcasted_iota(jnp.int32, (2 * P, N), 0) < P
        for gg in range(gpb):
            Bm = b_ref[:, gg * N:(gg + 1) * N].astype(BF16)
            Cm = c_ref[:, gg * N:(gg + 1) * N].astype(BF16)
            Gm = lax.dot_general(Cm, Bm, (((1,), (1,)), ((), ())), preferred_element_type=F32)
            for k in range(R // 2):
                pk = gg * (R // 2) + k
                sl = slice(pk * 2 * P, (pk + 1) * 2 * P)
                xp = x_ref[:, sl]
                xc = xp * dt_ref[:, sl]
                s_in = S[pk]
                y = lax.dot_general(Cm, s_in.astype(BF16), (((1,), (1,)), ((), ())),
                                    preferred_element_type=F32) * ein_ref[:, sl] + dsk_ref[0:1, sl] * xp
                for j in range(2):
                    hr = 2 * pk + j
                    diff = lax.dot_general(la_ref[hr], lb_ref[hr], (((0,), (0,)), ((), ())), preferred_element_type=F32)
                    ldec = jnp.exp(jnp.where(mask, diff, NEG_BIG))
                    xc_j = (jnp.where(head0, xc, 0.0) if j == 0 else jnp.where(head0, 0.0, xc)).astype(BF16)
                    y = y + jnp.dot((Gm * ldec).astype(BF16), xc_j, preferred_element_type=F32)
                y_ref[:, sl] = y
                st_ref[pk] = s_in
                e_all = jnp.where(rows0, et_ref[gg, 2 * k:2 * k + 1, :], et_ref[gg, 2 * k + 1:2 * k + 2, :])
                xd = (xc * dte_ref[:, sl]).astype(BF16)
                S[pk] = e_all * s_in + lax.dot_general(xd, Bm, (((0,), (0,)), ((), ())),
                                                       preferred_element_type=F32)

    return _call_with_exchange(
        body, exch, name=name, grid=(2, G // gpb, nc),
        in_specs=[sp["x"], sp["b"], sp["c"], sp["rep"], sp["rep"], sp["rep"], sp["lam_a"], sp["lam_b"], sp["etot"],
                  sp["dsk"]],
        out_specs=[sp["xd"], sp["st"]],
        out_shape=[jax.ShapeDtypeStruct((2, T, H * P), F32), jax.ShapeDtypeStruct((2, nc, H // 2, 2 * P, N), F32)],
        scratch_shapes=[pltpu.VMEM((gpb * R // 2, 2 * P, N), F32)],
        operands=(xbc, xbc, xbc, dtb, ein, dte, lam_a, lam_b, etot, dsk))


def _ssd_bwd(xbc, dy, states, dtb, ein, dte, lam_a, lam_b, etot, G, n_ctx, gpb, name, exch=None):
    T = xbc.shape[0]
    R, P, N, Q = HPG, HEADDIM, STATE, CHUNK
    H = G * R
    nc = T // Q
    sp = _ssd_specs(T, G, n_ctx, gpb, lambda d, s, nc_ctx, n: _scan_chunk(d, n - 1 - s, nc_ctx, n))

    def body(x_ref, b_ref, c_ref, dy_ref, st_ref, dt_ref, ein_ref, dte_ref, la_ref, lb_ref, et_ref,
             dx_ref, db_ref, dc_ref, ddt_ref, dlc_ref, dlr_ref, dS):
        d, s = pl.program_id(0), pl.program_id(2)

        @pl.when(s == 0)
        def _():
            dS[...] = jnp.zeros_like(dS)

        mask = _chunk_mask(d)
        ri = lax.broadcasted_iota(jnp.int32, (Q, 1), 0)
        is_last = ri == jnp.where(d == 0, Q - 1, 0)
        head0 = lax.broadcasted_iota(jnp.int32, (Q, 2 * P), 1) < P
        rows0 = lax.broadcasted_iota(jnp.int32, (2 * P, N), 0) < P

        def total(v):
            return jnp.sum(jnp.sum(v, axis=1, keepdims=True), axis=0, keepdims=True)

        for gg in range(gpb):
            Bm = b_ref[:, gg * N:(gg + 1) * N].astype(BF16)
            Cm = c_ref[:, gg * N:(gg + 1) * N].astype(BF16)
            Gm = lax.dot_general(Cm, Bm, (((1,), (1,)), ((), ())), preferred_element_type=F32)
            dG = jnp.zeros((Q, Q), F32)
            dB = jnp.zeros((Q, N), F32)
            dC = jnp.zeros((Q, N), F32)
            for k in range(R // 2):
                pk = gg * (R // 2) + k
                sl = slice(pk * 2 * P, (pk + 1) * 2 * P)
                e_in = ein_ref[:, sl]
                dte = dte_ref[:, sl]
                e_all = jnp.where(rows0, et_ref[gg, 2 * k:2 * k + 1, :], et_ref[gg, 2 * k + 1:2 * k + 2, :])
                xp = x_ref[:, sl]
                dtp = dt_ref[:, sl]
                xc = xp * dtp
                xc_b = xc.astype(BF16)
                dyp = dy_ref[:, sl]
                s_in = st_ref[pk]
                s_in_b = s_in.astype(BF16)
                ds_out = dS[pk]
                ds_out_b = ds_out.astype(BF16)
                y_int = lax.dot_general(Cm, s_in_b, (((1,), (1,)), ((), ())), preferred_element_type=F32) * e_in
                b_ds = lax.dot_general(Bm, ds_out_b, (((1,), (1,)), ((), ())), preferred_element_type=F32)
                dxc = dte * b_ds
                u = xc * dxc
                v = dyp * y_int - u
                sse = ds_out * s_in * e_all
                for j in range(2):
                    hr, r = 2 * pk + j, 2 * k + j
                    def pick(a, m0=head0, j=j):
                        return jnp.where(m0, a, 0.0) if j == 0 else jnp.where(m0, 0.0, a)

                    diff = lax.dot_general(la_ref[hr], lb_ref[hr], (((0,), (0,)), ((), ())), preferred_element_type=F32)
                    ldec = jnp.exp(jnp.where(mask, diff, NEG_BIG))
                    dy_j = pick(dyp).astype(BF16)
                    dM = lax.dot_general(dy_j, xc_b, (((1,), (1,)), ((), ())), preferred_element_type=F32)
                    dMl = dM * ldec
                    Wm = dMl * Gm
                    dlam_c = jnp.sum(Wm, axis=1, keepdims=True) + jnp.sum(pick(v), axis=1, keepdims=True)
                    last = total(pick(sse, rows0)) + total(pick(u))
                    dlc_ref[gg, :, r:r + 1] = dlam_c + jnp.where(is_last, last, 0.0)
                    dlr_ref[gg, r:r + 1, :] = -jnp.sum(Wm, axis=0, keepdims=True)
                    dxc = dxc + lax.dot_general((Gm * ldec).astype(BF16), dy_j, (((0,), (0,)), ((), ())),
                                                preferred_element_type=F32)
                    dG = dG + dMl
                dx_ref[:, sl] = dxc * dtp
                t = dxc * xp
                ddt_ref[gg, :, 2 * k:2 * k + 1] = jnp.sum(jnp.where(head0, t, 0.0), axis=1, keepdims=True)
                ddt_ref[gg, :, 2 * k + 1:2 * k + 2] = jnp.sum(jnp.where(head0, 0.0, t), axis=1, keepdims=True)
                edy_b = (e_in * dyp).astype(BF16)
                dC = dC + jnp.dot(edy_b, s_in_b, preferred_element_type=F32)
                dB = dB + jnp.dot((dte * xc).astype(BF16), ds_out_b, preferred_element_type=F32)
                dS[pk] = e_all * ds_out + lax.dot_general(edy_b, Cm, (((0,), (0,)), ((), ())),
                                                          preferred_element_type=F32)
            dG_b = dG.astype(BF16)
            dc_ref[:, gg * N:(gg + 1) * N] = dC + jnp.dot(dG_b, Bm, preferred_element_type=F32)
            db_ref[:, gg * N:(gg + 1) * N] = dB + lax.dot_general(dG_b, Cm, (((0,), (0,)), ((), ())),
                                                                  preferred_element_type=F32)

    return _call_with_exchange(
        body, exch, name=name, grid=(2, G // gpb, nc),
        in_specs=[sp["x"], sp["b"], sp["c"], sp["x"], sp["st"], sp["rep"], sp["rep"], sp["rep"], sp["lam_a"],
                  sp["lam_b"], sp["etot"]],
        out_specs=[sp["xd"], sp["bd"], sp["bd"], sp["col"], sp["col"], sp["row"]],
        out_shape=[jax.ShapeDtypeStruct((2, T, H * P), F32), jax.ShapeDtypeStruct((2, T, G * N), F32),
                   jax.ShapeDtypeStruct((2, T, G * N), F32), jax.ShapeDtypeStruct((2, G, T, R), F32),
                   jax.ShapeDtypeStruct((2, G, T, R), F32), jax.ShapeDtypeStruct((2, G, R, T), F32)],
        scratch_shapes=[pltpu.VMEM((gpb * R // 2, 2 * P, N), F32)],
        operands=(xbc, xbc, xbc, dy, states, dtb, ein, dte, lam_a, lam_b, etot))


def _gnorm_fwd(y2, proj, w, gs, tr, name):
    T, HP = y2.shape[1], y2.shape[2]

    def body(y_ref, z_ref, w_ref, o_ref):
        yz = (y_ref[0] + y_ref[1]) * _silu(z_ref[...])
        for g in range(HP // gs):
            v = yz[:, g * gs:(g + 1) * gs]
            rstd = lax.rsqrt(jnp.mean(v * v, axis=-1, keepdims=True) + EPS)
            o_ref[:, g * gs:(g + 1) * gs] = (v * rstd * w_ref[:, g * gs:(g + 1) * gs]).astype(BF16)

    return pl.pallas_call(
        body, name=name, grid=(T // tr,),
        in_specs=[pl.BlockSpec((2, tr, HP), lambda i: (0, i, 0)), pl.BlockSpec((tr, HP), lambda i: (i, 0)),
                  pl.BlockSpec((1, HP), lambda i: (0, 0))],
        out_specs=pl.BlockSpec((tr, HP), lambda i: (i, 0)),
        out_shape=jax.ShapeDtypeStruct((T, HP), BF16),
        compiler_params=_cparams(("parallel",)))(y2, proj, w)


def _gnorm_bwd(y2, proj, w, dyn, gs, tr, name):
    T, HP = y2.shape[1], y2.shape[2]
    row = pl.BlockSpec((tr, HP), lambda i: (i, 0))

    def body(y_ref, z_ref, w_ref, d_ref, dy_ref, dz_ref, dw_ref):
        @pl.when(pl.program_id(0) == 0)
        def _():
            dw_ref[...] = jnp.zeros_like(dw_ref)

        yv = y_ref[0] + y_ref[1]
        zv = z_ref[...]
        sz = _silu(zv)
        yz = yv * sz
        dv = d_ref[...]
        r = lax.broadcasted_iota(jnp.int32, (8, gs), 0)
        for g in range(HP // gs):
            sl = slice(g * gs, (g + 1) * gs)
            v = yz[:, sl]
            rstd = lax.rsqrt(jnp.mean(v * v, axis=-1, keepdims=True) + EPS)
            xhat = v * rstd
            dyn_g = dv[:, sl]
            dhat = dyn_g * w_ref[:, sl]
            dyz = rstd * (dhat - xhat * jnp.mean(dhat * xhat, axis=-1, keepdims=True))
            dy_ref[:, sl] = dyz * sz[:, sl]
            dz_ref[:, sl] = (dyz * yv[:, sl] * _dsilu(zv[:, sl])).astype(BF16)
            dw_ref[:, sl] += jnp.where(r == 0, jnp.sum(dyn_g * xhat, axis=0, keepdims=True), 0.0)

    return pl.pallas_call(
        body, name=name, grid=(T // tr,),
        in_specs=[pl.BlockSpec((2, tr, HP), lambda i: (0, i, 0)), row, pl.BlockSpec((1, HP), lambda i: (0, 0)), row],
        out_specs=[row, row, pl.BlockSpec((8, HP), lambda i: (0, 0))],
        out_shape=[jax.ShapeDtypeStruct((T, HP), F32), jax.ShapeDtypeStruct((T, HP), BF16),
                   jax.ShapeDtypeStruct((8, HP), F32)],
        compiler_params=_cparams(("arbitrary",)))(y2, proj, w, dyn)


def _merge_fwd(proj, g1_blk, o_ssd, o_pool, tr, name):
    T, D = o_ssd.shape
    row = pl.BlockSpec((tr, D), lambda i: (i, 0))

    def body(g1_ref, g2_ref, a_ref, b_ref, o_ref):
        o_ref[...] = (jax.nn.sigmoid(g1_ref[...]) * a_ref[...]
                      + jax.nn.sigmoid(g2_ref[...]) * b_ref[...]).astype(BF16)

    return pl.pallas_call(
        body, name=name, grid=(T // tr,),
        in_specs=[pl.BlockSpec((tr, D), lambda i: (i, g1_blk)), pl.BlockSpec((tr, D), lambda i: (i, g1_blk + 1)),
                  row, row],
        out_specs=row, out_shape=jax.ShapeDtypeStruct((T, D), BF16),
        compiler_params=_cparams(("parallel",)))(proj, proj, o_ssd, o_pool)


def _merge_bwd(proj, g1_blk, o_ssd, o_pool, dmg, tr, name):
    T, D = o_ssd.shape
    row = pl.BlockSpec((tr, D), lambda i: (i, 0))

    def body(g1_ref, g2_ref, a_ref, b_ref, d_ref, da_ref, db_ref, dg_ref):
        s1, s2 = jax.nn.sigmoid(g1_ref[...]), jax.nn.sigmoid(g2_ref[...])
        dv = d_ref[...]
        da_ref[...] = (s1 * dv).astype(BF16)
        db_ref[...] = (s2 * dv).astype(BF16)
        dg_ref[:, :D] = (dv * a_ref[...] * s1 * (1.0 - s1)).astype(BF16)
        dg_ref[:, D:] = (dv * b_ref[...] * s2 * (1.0 - s2)).astype(BF16)

    return pl.pallas_call(
        body, name=name, grid=(T // tr,),
        in_specs=[pl.BlockSpec((tr, D), lambda i: (i, g1_blk)), pl.BlockSpec((tr, D), lambda i: (i, g1_blk + 1)),
                  row, row, row],
        out_specs=[row, row, pl.BlockSpec((tr, 2 * D), lambda i: (i, 0))],
        out_shape=[jax.ShapeDtypeStruct((T, D), BF16), jax.ShapeDtypeStruct((T, D), BF16),
                   jax.ShapeDtypeStruct((T, 2 * D), BF16)],
        compiler_params=_cparams(("parallel",)))(proj, proj, o_ssd, o_pool, dmg)


def _swiglu_fwd(gu, tr, name):
    T, F2 = gu.shape
    F = F2 // 2
    tc = _tile(F, (1408, 768, 512, 256, 128))
    nb = F // tc

    def body(a_ref, b_ref, o_ref):
        o_ref[...] = (_silu(a_ref[...]) * b_ref[...]).astype(BF16)

    return pl.pallas_call(
        body, name=name, grid=(T // tr, nb),
        in_specs=[pl.BlockSpec((tr, tc), lambda i, j: (i, j)), pl.BlockSpec((tr, tc), lambda i, j: (i, nb + j))],
        out_specs=pl.BlockSpec((tr, tc), lambda i, j: (i, j)),
        out_shape=jax.ShapeDtypeStruct((T, F), BF16),
        compiler_params=_cparams(("parallel", "parallel")))(gu, gu)


def _swiglu_bwd(gu, dact, tr, name):
    T, F2 = gu.shape
    F = F2 // 2
    tc = _tile(F, (1408, 768, 512, 256, 128))
    nb = F // tc

    def body(a_ref, b_ref, d_ref, o_ref):
        is_a = pl.program_id(1) < nb
        av, bv, dv = a_ref[...], b_ref[...], d_ref[...]
        o_ref[...] = jnp.where(is_a, dv * bv * _dsilu(av), dv * _silu(av)).astype(BF16)

    return pl.pallas_call(
        body, name=name, grid=(T // tr, 2 * nb),
        in_specs=[pl.BlockSpec((tr, tc), lambda i, j: (i, j % nb)), pl.BlockSpec((tr, tc), lambda i, j: (i, nb + j % nb)),
                  pl.BlockSpec((tr, tc), lambda i, j: (i, j % nb))],
        out_specs=pl.BlockSpec((tr, tc), lambda i, j: (i, j)),
        out_shape=jax.ShapeDtypeStruct((T, F2), BF16),
        compiler_params=_cparams(("parallel", "parallel")))(gu, gu, dact)


def _adamw(w, g, m, v, name):
    Rr, C = w.shape
    tr = _tile(Rr, (256, 128, 64, 32, 16, 8))
    row = pl.BlockSpec((tr, C), lambda i: (i, 0))

    def body(w_ref, g_ref, m_ref, v_ref, d_ref, mo_ref, vo_ref):
        gv = g_ref[...]
        mn = ADAM_B1 * m_ref[...] + (1.0 - ADAM_B1) * gv
        vn = ADAM_B2 * v_ref[...] + (1.0 - ADAM_B2) * (gv * gv)
        m_hat = mn / (1.0 - ADAM_B1 ** ADAM_STEP)
        v_hat = vn / (1.0 - ADAM_B2 ** ADAM_STEP)
        d_ref[...] = -ADAM_LR * (m_hat / (jnp.sqrt(v_hat) + ADAM_EPS) + ADAM_WD * w_ref[...])
        mo_ref[...] = mn
        vo_ref[...] = vn

    sds = jax.ShapeDtypeStruct((Rr, C), F32)
    return pl.pallas_call(body, name=name, grid=(Rr // tr,), in_specs=[row] * 4, out_specs=[row] * 3,
                          out_shape=[sds] * 3, compiler_params=_cparams(("parallel",)))(w, g, m, v)


def _sum_slots(x, name):
    n, Rr, C = x.shape
    tr = _tile(Rr, (512, 256, 128, 64, 32, 16, 8))

    def body(x_ref, o_ref):
        acc = x_ref[0].astype(F32)
        for k in range(1, n):
            acc = acc + x_ref[k].astype(F32)
        o_ref[...] = acc

    return pl.pallas_call(body, name=name, grid=(Rr // tr,),
                          in_specs=[pl.BlockSpec((n, tr, C), lambda i: (0, i, 0))],
                          out_specs=pl.BlockSpec((tr, C), lambda i: (i, 0)),
                          out_shape=jax.ShapeDtypeStruct((Rr, C), F32),
                          compiler_params=_cparams(("parallel",)))(x)


def _place():
    return lax.axis_index("x"), lax.axis_index("y"), lax.axis_index("c")


def _all_gather(x, name):
    Rr, C = x.shape

    def body(x_ref, out_ref, send_sems, recv_sems, local_sem):
        _gather_start(x_ref, out_ref, send_sems, recv_sems, local_sem)
        _gather_finish(x_ref, out_ref, send_sems, recv_sems, local_sem)

    return pl.pallas_call(
        body, name=name, in_specs=[ANY], out_specs=ANY,
        out_shape=jax.ShapeDtypeStruct((N_DEV, Rr, C), x.dtype), scratch_shapes=_GATHER_SEMS,
    )(x)


_GATHER_SEMS = [pltpu.SemaphoreType.DMA((7,)), pltpu.SemaphoreType.DMA((7,)), pltpu.SemaphoreType.DMA]


def _gather_copies(x_ref, out_ref, send_sems, recv_sems, local_sem):
    mx, my, mc = _place()
    me, sibling = (mx, my, mc), (mx, my, 1 - mc)
    chips = [(1 - mx, my), (mx, 1 - my), (1 - mx, 1 - my)]

    def slot(px, py, pc):
        return out_ref.at[4 * px + 2 * py + pc]

    def copy(k, block, to, src=None):
        return pltpu.make_async_remote_copy(
            src_ref=slot(*block) if src is None else src, dst_ref=slot(*block),
            send_sem=send_sems.at[k], recv_sem=recv_sems.at[k],
            device_id=to, device_id_type=pl.DeviceIdType.MESH)

    return dict(
        mine=pltpu.make_async_copy(x_ref, slot(*me), local_sem),
        first=[copy(0, me, sibling, src=x_ref)] + [copy(1 + j, me, (*chip, mc), src=x_ref)
                                                   for j, chip in enumerate(chips)],
        passed=[copy(4 + j, (*chip, mc), sibling) for j, chip in enumerate(chips)],
        from_chips=[copy(1 + j, (*chip, mc), me) for j, chip in enumerate(chips)],
        from_sibling=[copy(0, sibling, me)] + [copy(4 + j, (*chip, 1 - mc), me) for j, chip in enumerate(chips)],
    )


def _gather_start(*refs):
    cps = _gather_copies(*refs)
    cps["mine"].start()
    for cp in cps["first"]:
        cp.start()


def _gather_finish(*refs):
    cps = _gather_copies(*refs)
    for j in range(3):
        cps["from_chips"][j].wait_recv()
        cps["passed"][j].start()
    for cp in cps["from_sibling"]:
        cp.wait_recv()
    for cp in cps["first"] + cps["passed"]:
        cp.wait_send()
    cps["mine"].wait()


def _pair_exchange(buf, name):
    _, n, Rr, C = buf.shape
    parts = 4
    pr = Rr // parts

    def body(b_ref, got_ref, send_sems, recv_sems):
        mx, my, mc = _place()
        copies = []
        for q in range(n):
            for p in range(parts):
                rows = pl.ds(p * pr, pr)
                cp = pltpu.make_async_remote_copy(
                    src_ref=b_ref.at[1 - mc, q, rows], dst_ref=got_ref.at[q, rows],
                    send_sem=send_sems.at[q * parts + p], recv_sem=recv_sems.at[q * parts + p],
                    device_id=(mx, my, 1 - mc), device_id_type=pl.DeviceIdType.MESH)
                cp.start()
                copies.append(cp)
        for cp in copies:
            cp.wait()

    return pl.pallas_call(
        body, name=name, in_specs=[ANY], out_specs=ANY,
        out_shape=jax.ShapeDtypeStruct((n, Rr, C), buf.dtype),
        scratch_shapes=[pltpu.SemaphoreType.DMA((n * parts,)), pltpu.SemaphoreType.DMA((n * parts,))],
    )(buf)


def _pair_add(buf, got, my_c, name):
    _, n, Rr, C = buf.shape
    tr = _tile(Rr, (512, 256, 128, 64, 32, 16))

    def body(c_ref, b_ref, g_ref, o_ref):
        o_ref[...] = (b_ref[...].astype(F32) + g_ref[...].astype(F32)).astype(BF16)

    return pl.pallas_call(
        body, name=name,
        grid_spec=pltpu.PrefetchScalarGridSpec(
            num_scalar_prefetch=1, grid=(n, Rr // tr),
            in_specs=[pl.BlockSpec((None, None, tr, C), lambda q, i, c: (c[0], q, i, 0)),
                      pl.BlockSpec((None, tr, C), lambda q, i, c: (q, i, 0))],
            out_specs=pl.BlockSpec((None, tr, C), lambda q, i, c: (q, i, 0))),
        out_shape=jax.ShapeDtypeStruct((n, Rr, C), BF16),
        compiler_params=_cparams(("parallel", "parallel")))(my_c, buf, got)


def _chip_exchange(red, name):
    def body(r_ref, out_ref, send_sems, recv_sems, local_sem):
        _chip_exchange_start(r_ref, out_ref, send_sems, recv_sems, local_sem)
        _chip_exchange_finish(r_ref, out_ref, send_sems, recv_sems, local_sem)

    return pl.pallas_call(
        body, name=name, in_specs=[ANY], out_specs=ANY,
        out_shape=jax.ShapeDtypeStruct(red.shape, red.dtype), scratch_shapes=_CHIP_SEMS,
    )(red)


_CHIP_SEMS = [pltpu.SemaphoreType.DMA((3,)), pltpu.SemaphoreType.DMA((3,)), pltpu.SemaphoreType.DMA]


def _chip_exchange_copies(r_ref, out_ref, send_sems, recv_sems, local_sem):
    mx, my, mc = _place()
    chips = [(1 - mx, my), (mx, 1 - my), (1 - mx, 1 - my)]

    def copy(k, src_slot, dst_slot, to):
        return pltpu.make_async_remote_copy(
            src_ref=r_ref.at[src_slot], dst_ref=out_ref.at[dst_slot],
            send_sem=send_sems.at[k], recv_sem=recv_sems.at[k],
            device_id=(*to, mc), device_id_type=pl.DeviceIdType.MESH)

    return dict(
        mine=pltpu.make_async_copy(r_ref.at[2 * mx + my], out_ref.at[2 * mx + my], local_sem),
        sends=[copy(k, 2 * px + py, 2 * mx + my, (px, py)) for k, (px, py) in enumerate(chips)],
        recvs=[copy(k, 2 * px + py, 2 * px + py, (px, py)) for k, (px, py) in enumerate(chips)],
    )


def _chip_exchange_start(*refs):
    cps = _chip_exchange_copies(*refs)
    cps["mine"].start()
    for cp in cps["sends"]:
        cp.start()


def _chip_exchange_finish(*refs):
    cps = _chip_exchange_copies(*refs)
    for cp in cps["recvs"]:
        cp.wait_recv()
    for cp in cps["sends"]:
        cp.wait_send()
    cps["mine"].wait()


def _pad_rows(a, rows):
    return jnp.pad(a, ((0, rows - a.shape[0]), (0, 0)))


class _Layout:
    def __init__(self, D, shards):
        self.D = D
        self.pieces = []
        off = 0
        for name, layer, rows in shards:
            pr = _round_up(rows, 16)
            self.pieces.append((name, layer, rows, pr, off))
            off += pr
        self.rows = _round_up(off, 512)

    def find(self, name, layer):
        for p in self.pieces:
            if p[0] == name and p[1] == layer:
                return p
        raise KeyError(name)


def kernel(x, c, ctx, c_ctx, w_ada, b_ada, g_mix, w_in, conv_w, conv_b, dt_bias, a_log, d_skip, ssd_norm_w, w_ssd_out, pool_w, pool_scale, w_pool_out, w_out, g_ffn, w_gate_up, w_down, g_final, loss_target, m_c_ctx, m_w_ada, m_b_ada, m_g_mix, m_w_in, m_conv_w, m_conv_b, m_dt_bias, m_a_log, m_d_skip, m_ssd_norm_w, m_w_ssd_out, m_pool_w, m_pool_scale, m_w_pool_out, m_w_out, m_g_ffn, m_w_gate_up, m_w_down, m_g_final, v_c_ctx, v_w_ada, v_b_ada, v_g_mix, v_w_in, v_conv_w, v_conv_b, v_dt_bias, v_a_log, v_d_skip, v_ssd_norm_w, v_w_ssd_out, v_pool_w, v_pool_scale, v_w_pool_out, v_w_out, v_g_ffn, v_w_gate_up, v_w_down, v_g_final):
    weights = dict(c_ctx=c_ctx, w_ada=w_ada, b_ada=b_ada, g_mix=g_mix, w_in=w_in, conv_w=conv_w, conv_b=conv_b,
                   dt_bias=dt_bias, a_log=a_log, d_skip=d_skip, ssd_norm_w=ssd_norm_w, w_ssd_out=w_ssd_out,
                   pool_w=pool_w, pool_scale=pool_scale, w_pool_out=w_pool_out, w_out=w_out, g_ffn=g_ffn,
                   w_gate_up=w_gate_up, w_down=w_down, g_final=g_final)
    moms_m = dict(c_ctx=m_c_ctx, w_ada=m_w_ada, b_ada=m_b_ada, g_mix=m_g_mix, w_in=m_w_in, conv_w=m_conv_w,
                  conv_b=m_conv_b, dt_bias=m_dt_bias, a_log=m_a_log, d_skip=m_d_skip, ssd_norm_w=m_ssd_norm_w,
                  w_ssd_out=m_w_ssd_out, pool_w=m_pool_w, pool_scale=m_pool_scale, w_pool_out=m_w_pool_out,
                  w_out=m_w_out, g_ffn=m_g_ffn, w_gate_up=m_w_gate_up, w_down=m_w_down, g_final=m_g_final)
    moms_v = dict(c_ctx=v_c_ctx, w_ada=v_w_ada, b_ada=v_b_ada, g_mix=v_g_mix, w_in=v_w_in, conv_w=v_conv_w,
                  conv_b=v_conv_b, dt_bias=v_dt_bias, a_log=v_a_log, d_skip=v_d_skip, ssd_norm_w=v_ssd_norm_w,
                  w_ssd_out=v_w_ssd_out, pool_w=v_pool_w, pool_scale=v_pool_scale, w_pool_out=v_w_pool_out,
                  w_out=v_w_out, g_ffn=v_g_ffn, w_gate_up=v_w_gate_up, w_down=v_w_down, g_final=v_g_final)
    order = ["c_ctx", "w_ada", "b_ada", "g_mix", "w_in", "conv_w", "conv_b", "dt_bias", "a_log", "d_skip",
             "ssd_norm_w", "w_ssd_out", "pool_w", "pool_scale", "w_pool_out", "w_out", "g_ffn", "w_gate_up",
             "w_down", "g_final"]
    big = ["w_ada", "w_in", "conv_w", "w_ssd_out", "pool_w", "w_pool_out", "w_out", "w_gate_up", "w_down"]
    small = [n for n in order if n not in big]

    depth = w_in.shape[0]
    L, D = x.shape[1], x.shape[2]
    n_ctx = ctx.shape[1]
    T = n_ctx + L
    in_cols = w_in.shape[2] * N_DEV
    xbc_w = conv_w.shape[2] * N_DEV
    dinner = ssd_norm_w.shape[1]
    H = dt_bias.shape[2]
    G = H // HPG
    GN = G * STATE
    assert xbc_w == dinner + 2 * GN and dinner == H * HEADDIM
    assert in_cols == dinner + xbc_w + 2 * H + D + 2 * D
    assert dinner == 2 * D and GN == D and 2 * H <= 128
    F = w_down.shape[1] * N_DEV
    pg = pool_w.shape[3]
    tr = n_ctx
    assert L % tr == 0 and tr % GRID_W == 0 and tr % CHUNK == 0 and L % CHUNK == 0
    n_ctx_tiles = 1
    NP = _round_up(9 * D + 128, 512)
    gs = dinner // G
    off_xbc, off_dt, off_pool = dinner, dinner + xbc_w, dinner + xbc_w + 2 * H
    off_gate = off_pool + D

    def shard_rows(name, l):
        w = weights[name][l]
        if name in ("w_ada", "w_in", "w_gate_up"):
            return w.T
        if name == "conv_w":
            w8 = _pad_rows(w, 8)
            hi = w8.astype(BF16)
            lo = (w8 - hi.astype(F32)).astype(BF16)
            return jnp.concatenate([hi, lo], axis=0).reshape(-1, D)
        if name == "pool_w":
            return w.reshape(-1, D)
        return w

    shards = [[(n, shard_rows(n, l)) for n in big] for l in range(depth)]
    lays = [_Layout(D, [(n, l, a.shape[0]) for n, a in shards[l]]) for l in range(depth)]

    def packed(l):
        rows = jnp.concatenate([_pad_rows(a.astype(BF16), lays[l].find(n, l)[3]) for n, a in shards[l]], axis=0)
        return _pad_rows(rows, lays[l].rows)

    gathered = [None] * depth
    gathered[0] = _all_gather(packed(0), "gather_weights")

    def full(name, l):
        _, _, rows, _, off = lays[l].find(name, l)
        return gathered[l][:, off:off + rows, :]

    def w_inT_new(l):
        w = full("w_in", l).reshape(in_cols, D)
        parts = [w[:off_xbc], w[off_xbc:off_dt], w[off_pool:off_gate], w[off_gate:], w[off_dt:off_pool]]
        return _pad_rows(jnp.concatenate(parts, axis=0), NP)

    xs0 = jnp.concatenate([ctx[0], x[0]], axis=0)
    cc8 = _pad_rows(jnp.concatenate([c, c_ctx[None, :]], axis=0), 8)
    tgt = loss_target[0]

    def vec(a):
        return a.reshape(1, -1)

    def pad128(a):
        return jnp.pad(a.reshape(1, -1), ((0, 0), (0, 128 - 2 * H)))

    expand = (jnp.arange(128)[:, None] == jnp.arange(2 * H * HEADDIM)[None, :] // HEADDIM).astype(BF16)

    def from4(arr):
        return jnp.pad(arr.transpose(2, 0, 1, 3).reshape(T, 2 * H), ((0, 0), (0, 128 - 2 * H)))

    dt_blk = (9 * D) // 128
    conv_tc = 128
    ssd_gpb, ssd_gpb_bwd = 4, 2
    saved = []
    xcur = xs0
    for l in range(depth):
        W = dict(
            adaT=full("w_ada", l).reshape(6 * D, D), inT=w_inT_new(l),
            ssd=full("w_ssd_out", l).reshape(dinner, D), po=full("w_pool_out", l).reshape(D, D),
            out=full("w_out", l).reshape(D, D), guT=full("w_gate_up", l).reshape(2 * F, D),
            down=full("w_down", l).reshape(F, D),
            pool=full("pool_w", l).reshape(N_DEV, len(POOL_WINDOWS), pg // N_DEV, pg).transpose(1, 0, 2, 3)
            .reshape(len(POOL_WINDOWS), pg, pg),
        )
        cw = full("conv_w", l).reshape(N_DEV, 16, xbc_w // N_DEV).astype(F32)
        W["conv8"] = (cw[:, :8] + cw[:, 8:]).transpose(1, 0, 2).reshape(8, xbc_w)
        m6 = _ada_fwd(cc8, W["adaT"], vec(b_ada[l]), "ada_fwd")
        h = _norm_mod(xcur, vec(g_mix[l]), m6, 0, 1, n_ctx_tiles, tr, "norm_mod")
        proj = _mm(h, W["inT"], "nt", F32, "mm_in")
        l1, l2, l3, dtb, ein, dte, etot = _dt_prep(proj, dt_blk, pad128(dt_bias[l]), pad128(a_log[l]), expand, H,
                                                   "dt_prep")
        L1, L2, L3 = (v[:, :2 * H].T.reshape(2, H, 1, T) for v in (l1, l2, l3))
        k16 = jnp.arange(16).reshape(1, 1, 16, 1)

        def rows16(at):
            terms = jnp.where(k16 == at, L1, jnp.where(k16 == at + 1, L2, L3))
            return jnp.where((k16 >= at) & (k16 < at + 3), terms, (k16 < 6).astype(BF16))

        lam_a, lam_b = rows16(0), -rows16(3) + 2 * (k16 < 3).astype(BF16)
        et = etot.reshape(T // CHUNK, 8, 128)[:, 0, :2 * H].reshape(T // CHUNK, 2, G, HPG).transpose(1, 2, 0, 3)
        etot5 = jnp.pad(jnp.broadcast_to(et[..., None], et.shape + (128,)),
                        ((0, 0), (0, 0), (0, 0), (0, 8 - HPG), (0, 0)))
        dsk = jnp.pad(jnp.repeat(d_skip[l], HEADDIM, axis=1)[:, None, :], ((0, 0), (0, 7), (0, 0)))
        scan_ops = (dtb, ein, dte, lam_a, lam_b, etot5)
        xbc = _conv_fwd(proj, dinner // conv_tc, xbc_w, W["conv8"], vec(conv_b[l]), n_ctx, conv_tc, "conv_fwd")
        if l + 1 < depth:
            nxt = _Exchange(packed(l + 1), jax.ShapeDtypeStruct((N_DEV, lays[l + 1].rows, D), BF16), _GATHER_SEMS,
                            _gather_start, _gather_finish)
            y2, states, gathered[l + 1] = _ssd_fwd(xbc, *scan_ops, dsk, G, n_ctx, ssd_gpb, "ssd_fwd_gather", nxt)
        else:
            y2, states = _ssd_fwd(xbc, *scan_ops, dsk, G, n_ctx, ssd_gpb, "ssd_fwd")
        yn = _gnorm_fwd(y2, proj, vec(ssd_norm_w[l]), gs, tr, "gnorm_fwd")
        pm = _pool_apply(proj, (6 * D) // pg, BF16, n_ctx, tr, pg, False, "pool_fwd")
        pms = _pool_mix_fwd(pm, W["pool"], vec(pool_scale[l]), tr, "pool_mix_fwd")
        o_ssd = _mm(yn, W["ssd"], "nn", F32, "mm_ssd_out")
        o_pool = _mm(pms, W["po"], "nn", F32, "mm_pool_out")
        mg = _merge_fwd(proj, 7, o_ssd, o_pool, tr, "merge_fwd")
        mo = _mm(mg, W["out"], "nn", F32, "mm_out")
        x1, h2 = _norm_mod(xcur, vec(g_ffn[l]), m6, 3, 4, n_ctx_tiles, tr, "resid_norm_mod", resid=(mo, 2))
        gu = _mm(h2, W["guT"], "nt", F32, "mm_gate_up")
        act = _swiglu_fwd(gu, tr, "swiglu_fwd")
        f = _mm(act, W["down"], "nn", F32, "mm_down")
        saved.append(dict(W=W, m6=m6, x0=xcur, h=h, proj=proj, scan_ops=scan_ops, xbc=xbc, y2=y2,
                          states=states, yn=yn, pm=pm, pms=pms, o_ssd=o_ssd, o_pool=o_pool, mg=mg, mo=mo, x1=x1,
                          h2=h2, gu=gu, act=act, f=f))
        xcur = _resid(x1, f, m6, 5, n_ctx_tiles, tr, "resid")

    loss_blk, dx, dgf = _loss_head(xcur, tgt, vec(g_final), n_ctx_tiles, tr, "loss_head")
    loss = lax.psum(loss_blk[0, 0], MESH_AXES)

    big_rows = {}
    small_g = {n: [None] * depth for n in small}
    d_c_ctx = jnp.zeros((D,), F32)
    slots = [None] * depth
    pending = None
    for l in reversed(range(depth)):
        S = saved[l]
        W, m6, proj = S["W"], S["m6"], S["proj"]
        df, dga2 = _resid_bwd(dx, S["f"], m6, 5, n_ctx_tiles, tr, "resid_bwd")
        dact = _mm(df, W["down"], "nt", F32, "mm_down_dx")
        g_down = _mm(S["act"], df, "tn", BF16,"mm_down_dw")
        dgu = _swiglu_bwd(S["gu"], dact, tr, "swiglu_bwd")
        dh2 = _mm(dgu, W["guT"], "nn", F32, "mm_gate_up_dx")
        g_guT = _mm(dgu, S["h2"], "tn", BF16,"mm_gate_up_dw")
        dx1, st2 = _norm_mod_bwd(S["x1"], dh2, dx, vec(g_ffn[l]), m6, 4, n_ctx_tiles, tr, "norm_mod_bwd")
        dmo, dga1 = _resid_bwd(dx1, S["mo"], m6, 2, n_ctx_tiles, tr, "resid_bwd")
        dmg = _mm(dmo, W["out"], "nt", F32, "mm_out_dx")
        g_out = _mm(S["mg"], dmo, "tn", BF16,"mm_out_dw")
        do_ssd, do_pool, dgl = _merge_bwd(proj, 7, S["o_ssd"], S["o_pool"], dmg, tr, "merge_bwd")
        dyn = _mm(do_ssd, W["ssd"], "nt", F32, "mm_ssd_out_dx")
        g_ssd = _mm(S["yn"], do_ssd, "tn", BF16,"mm_ssd_out_dw")
        dpms = _mm(do_pool, W["po"], "nt", F32, "mm_pool_out_dx")
        g_po = _mm(S["pms"], do_pool, "tn", BF16,"mm_pool_out_dw")
        dpm, g_pool, dps = _pool_mix_bwd(S["pm"], dpms, W["pool"], vec(pool_scale[l]), tr, "pool_mix_bwd")
        dup = _pool_apply(dpm, 0, BF16, n_ctx, tr, pg, True, "pool_bwd")
        dy, dz, dnw = _gnorm_bwd(S["y2"], proj, vec(ssd_norm_w[l]), dyn, gs, tr, "gnorm_bwd")
        if pending is None:
            dxs2, db2, dc2, ddt4, dlc4, dlr4 = _ssd_bwd(S["xbc"], dy, S["states"], *S["scan_ops"], G, n_ctx,
                                                        ssd_gpb_bwd, "ssd_bwd")
        else:
            exch = _Exchange(pending, jax.ShapeDtypeStruct(pending.shape, BF16), _CHIP_SEMS, _chip_exchange_start,
                             _chip_exchange_finish)
            dxs2, db2, dc2, ddt4, dlc4, dlr4, slots[l + 1] = _ssd_bwd(
                S["xbc"], dy, S["states"], *S["scan_ops"], G, n_ctx, ssd_gpb_bwd, "ssd_bwd_exchange", exch)
        dskv = _pad_rows(jnp.repeat(d_skip[l], HEADDIM, axis=1), 8)
        cb = vec(conv_b[l])
        dxbc_x, dconv_x = _conv_bwd(proj, dinner // conv_tc, dxs2, 0, W["conv8"], cb, n_ctx, conv_tc, "conv_bwd_x",
                                    skip=(dy, dskv))
        dxbc_b, dconv_b = _conv_bwd(proj, 2 * dinner // conv_tc, db2, dinner // conv_tc, W["conv8"], cb, n_ctx,
                                    conv_tc, "conv_bwd_bc")
        dxbc_c, dconv_c = _conv_bwd(proj, (2 * dinner + GN) // conv_tc, dc2, (dinner + GN) // conv_tc, W["conv8"], cb,
                                    n_ctx, conv_tc, "conv_bwd_bc")
        dconv = jnp.concatenate([dconv_x, dconv_b, dconv_c], axis=1)
        ddt_raw, dtst = _dt_bwd(proj, dt_blk, pad128(dt_bias[l]), pad128(a_log[l]), from4(ddt4),
                                from4(dlc4 + dlr4.transpose(0, 1, 3, 2)), H, tr, "dt_bwd")
        dproj = jnp.concatenate([dz, dxbc_x, dxbc_b, dxbc_c, dup, dgl, ddt_raw,
                                 jnp.zeros((T, NP - 9 * D - 128), BF16)], axis=1)
        dh = _mm(dproj, W["inT"], "nn", F32, "mm_in_dx")
        g_inT_new = _mm(dproj, S["h"], "tn", BF16,"mm_in_dw")
        dx0, st1 = _norm_mod_bwd(S["x0"], dh, dx1, vec(g_mix[l]), m6, 1, n_ctx_tiles, tr, "norm_mod_bwd")
        dm6 = _pad_rows(jnp.concatenate([st1[0:2], st1[2:4], dga1[0:2], st2[0:2], st2[2:4], dga2[0:2]], axis=1), 8)
        dsil = _mm(dm6, W["adaT"], "nn", F32, "mm_ada_dx")
        sil_b, dcc, dbada = _ada_bwd_small(cc8, dsil, dm6, "ada_bwd_small")
        g_adaT = _mm(dm6, sil_b, "tn", BF16,"mm_ada_dw")
        d_c_ctx = d_c_ctx + dcc[1]
        dx = dx0

        g_inT = jnp.concatenate([g_inT_new[:6 * D], g_inT_new[9 * D:9 * D + 2 * H], g_inT_new[6 * D:9 * D]], axis=0)
        big_rows[("w_ada", l)] = g_adaT.reshape(N_DEV, -1, D)
        big_rows[("w_in", l)] = g_inT.reshape(N_DEV, -1, D)
        big_rows[("conv_w", l)] = jnp.pad(
            dconv[:CONV_K].reshape(CONV_K, N_DEV, xbc_w // N_DEV).transpose(1, 0, 2),
            ((0, 0), (0, 16 - CONV_K), (0, 0))).reshape(N_DEV, -1, D)
        big_rows[("w_ssd_out", l)] = g_ssd.reshape(N_DEV, -1, D)
        big_rows[("pool_w", l)] = g_pool.reshape(len(POOL_WINDOWS), N_DEV, pg // N_DEV, pg).transpose(1, 0, 2, 3) \
            .reshape(N_DEV, -1, D)
        big_rows[("w_pool_out", l)] = g_po.reshape(N_DEV, -1, D)
        big_rows[("w_out", l)] = g_out.reshape(N_DEV, -1, D)
        big_rows[("w_gate_up", l)] = g_guT.reshape(N_DEV, -1, D)
        big_rows[("w_down", l)] = g_down.reshape(N_DEV, -1, D)
        small_g["b_ada"][l] = dbada[0]
        small_g["g_mix"][l] = st1[4]
        small_g["conv_b"][l] = dconv[CONV_K]
        small_g["dt_bias"][l] = dtst[0, :2 * H].reshape(2, H)
        small_g["a_log"][l] = dtst[1, :2 * H].reshape(2, H)
        dsk_h = dconv_x[CONV_K + 1].reshape(H, HEADDIM).sum(axis=-1)
        small_g["d_skip"][l] = jnp.stack([dsk_h, dsk_h])
        small_g["ssd_norm_w"][l] = dnw[0]
        small_g["pool_scale"][l] = dps[0]
        small_g["g_ffn"][l] = st2[4]

        lay = lays[l]
        gparts = [jnp.pad(big_rows[(n, l)].astype(BF16), ((0, 0), (0, pr - rows), (0, 0)))
                  for n, _, rows, pr, _ in lay.pieces]
        gparts.append(jnp.zeros((N_DEV, lay.rows - sum(p[3] for p in lay.pieces), D), BF16))
        gbuf = jnp.concatenate(gparts, axis=1)
        gbuf = gbuf.reshape(2, 2, 2, lay.rows, D).transpose(2, 0, 1, 3, 4).reshape(2, 4, lay.rows, D)
        got = _pair_exchange(gbuf, "rs_pair_exchange")
        pending = _pair_add(gbuf, got, lax.axis_index("c").astype(jnp.int32).reshape(1), "rs_pair_add")
    slots[0] = _chip_exchange(pending, "rs_chip_exchange")
    g_local = [_sum_slots(s, "rs_chip_add") for s in slots]
    grad_x = dx[n_ctx:][None]

    def local_grad(name):
        outs = []
        for l in range(depth):
            _, _, rows, _, off = lays[l].find(name, l)
            piece = g_local[l][off:off + rows]
            if name in ("w_ada", "w_in", "w_gate_up"):
                piece = piece.T
            elif name == "conv_w":
                piece = piece.reshape(16, -1)[:CONV_K]
            elif name == "pool_w":
                piece = piece.reshape(weights[name].shape[1:])
            outs.append(piece)
        return jnp.stack(outs)

    grads = {n: local_grad(n) for n in big}

    small_full = {"c_ctx": d_c_ctx, "g_final": dgf[0]}
    for n in small:
        if n not in small_full:
            small_full[n] = jnp.stack(small_g[n])

    def pack_small(tree):
        flat = jnp.concatenate([tree[n].reshape(-1).astype(F32) for n in small])
        rows = _round_up(-(-flat.shape[0] // D), 8)
        return jnp.pad(flat, (0, rows * D - flat.shape[0])).reshape(rows, D)

    def unpack_small(buf):
        flat, out, off = buf.reshape(-1), {}, 0
        for n in small:
            sz = weights[n].size
            out[n] = flat[off:off + sz].reshape(weights[n].shape)
            off += sz
        return out

    g_small = _sum_slots(_all_gather(pack_small(small_full), "gather_small_grads"), "sum_small_grads")
    grads.update(unpack_small(g_small))

    delta, new_m, new_v = {}, {}, {}
    for n in big:
        shp = weights[n].shape
        d_, m_, v_ = _adamw(weights[n].reshape(-1, shp[-1]), grads[n].reshape(-1, shp[-1]),
                            moms_m[n].reshape(-1, shp[-1]), moms_v[n].reshape(-1, shp[-1]), "adamw_" + n)
        delta[n], new_m[n], new_v[n] = d_.reshape(shp), m_.reshape(shp), v_.reshape(shp)
    d_, m_, v_ = _adamw(pack_small(weights), g_small, pack_small(moms_m), pack_small(moms_v), "adamw_small")
    delta.update(unpack_small(d_))
    new_m.update(unpack_small(m_))
    new_v.update(unpack_small(v_))

    return (loss, grad_x, *[grads[n] for n in order], *[delta[n] for n in order],
            *[new_m[n] for n in order], *[new_v[n] for n in order])
```

```python
import functools

import jax
import jax.numpy as jnp
from jax import lax
from jax.experimental import pallas as pl
from jax.experimental.pallas import tpu as pltpu

F32 = jnp.float32
BF16 = jnp.bfloat16
N_DEV = 8
EPS = 1e-6
GRID_W = 64
POOL_WINDOWS = (2, 4, 8, 16)
HEADDIM = 64
STATE = 128
CHUNK = 128
HPG = 4
CONV_K = 5
ADAM_LR, ADAM_B1, ADAM_B2, ADAM_EPS, ADAM_WD, ADAM_STEP = 0.001, 0.9, 0.999, 1e-08, 0.01, 10
NEG_BIG = -1e30
MESH_AXES = ("x", "y", "c")
ANY = pl.BlockSpec(memory_space=pl.ANY)


def _tile(n, cands):
    for t in cands:
        if n % t == 0:
            return t
    return n


def _round_up(n, m):
    return -(-n // m) * m


def _silu(x):
    return x * jax.nn.sigmoid(x)


def _dsilu(x):
    s = jax.nn.sigmoid(x)
    return s * (1.0 + x * (1.0 - s))


def _cparams(sem):
    return pltpu.CompilerParams(dimension_semantics=sem, vmem_limit_bytes=56 * 1024 * 1024)


def _mm(a, b, mode, out_dtype, name, exch=None):
    if mode == "tn":
        K, M = a.shape
        N = b.shape[1]
        tm = _tile(M, (512, 256, 128))
        tn = _tile(N, (1024, 512, 256, 128))
        tk = _tile(K, (1088, 544, 512, 256, 128))
        a_spec = pl.BlockSpec((tk, tm), lambda i, j, k: (k, i))
        b_spec = pl.BlockSpec((tk, tn), lambda i, j, k: (k, j))
        dims = (((0,), (0,)), ((), ()))
    else:
        M, K = a.shape
        N = b.shape[0] if mode == "nt" else b.shape[1]
        tm = _tile(M, (1088, 544, 512, 256, 128))
        tk = K if K <= 4096 else max(t for t in range(128, 2817, 128) if K % t == 0)
        tn = _tile(N, (512, 256, 128)) if tk == K else _tile(N, (1024, 512, 256, 128))
        a_spec = pl.BlockSpec((tm, tk), lambda i, j, k: (i, k))
        if mode == "nt":
            b_spec = pl.BlockSpec((tn, tk), lambda i, j, k: (j, k))
            dims = (((1,), (1,)), ((), ()))
        else:
            b_spec = pl.BlockSpec((tk, tn), lambda i, j, k: (k, j))
            dims = (((1,), (0,)), ((), ()))
    nk = K // tk

    def body(a_ref, b_ref, o_ref, *acc):
        if nk == 1:
            o_ref[...] = lax.dot_general(a_ref[...].astype(BF16), b_ref[...].astype(BF16), dims,
                                         preferred_element_type=F32).astype(o_ref.dtype)
            return
        k = pl.program_id(2)

        @pl.when(k == 0)
        def _():
            acc[0][...] = jnp.zeros_like(acc[0])

        acc[0][...] += lax.dot_general(a_ref[...].astype(BF16), b_ref[...].astype(BF16), dims,
                                       preferred_element_type=F32)

        @pl.when(k == nk - 1)
        def _():
            o_ref[...] = acc[0][...].astype(o_ref.dtype)

    call = dict(name=name, grid=(M // tm, N // tn, nk), in_specs=[a_spec, b_spec],
                scratch_shapes=[pltpu.VMEM((tm, tn), F32)] if nk > 1 else [])
    o_spec, o_sds = pl.BlockSpec((tm, tn), lambda i, j, k: (i, j)), jax.ShapeDtypeStruct((M, N), out_dtype)
    if exch is not None:
        return _call_with_exchange(body, exch, out_specs=[o_spec], out_shape=[o_sds], operands=(a, b), **call)
    return pl.pallas_call(body, out_specs=o_spec, out_shape=o_sds,
                          compiler_params=_cparams(("parallel", "parallel", "arbitrary")), **call)(a, b)


def _ada_fwd(cc8, w_adaT, b_ada, name):
    D = cc8.shape[1]
    N = w_adaT.shape[0]
    tn = _tile(N, (512, 256, 128))

    def body(c_ref, w_ref, b_ref, o_ref):
        a = _silu(c_ref[...]).astype(BF16)
        o_ref[...] = lax.dot_general(a, w_ref[...], (((1,), (1,)), ((), ())),
                                     preferred_element_type=F32) + b_ref[...]

    return pl.pallas_call(
        body, name=name, grid=(N // tn,),
        in_specs=[pl.BlockSpec((8, D), lambda j: (0, 0)), pl.BlockSpec((tn, D), lambda j: (j, 0)),
                  pl.BlockSpec((1, tn), lambda j: (0, j))],
        out_specs=pl.BlockSpec((8, tn), lambda j: (0, j)),
        out_shape=jax.ShapeDtypeStruct((8, N), F32),
        compiler_params=_cparams(("parallel",)),
    )(cc8, w_adaT, b_ada)


def _ada_bwd_small(cc8, dsil, dm6, name):
    D = cc8.shape[1]
    N = dm6.shape[1]

    def body(c_ref, ds_ref, dm_ref, sil_ref, dc_ref, db_ref):
        c = c_ref[...]
        sil_ref[...] = _silu(c).astype(BF16)
        dc_ref[...] = ds_ref[...] * _dsilu(c)
        dm = dm_ref[...]
        row = lax.broadcasted_iota(jnp.int32, dm.shape, 0)
        db_ref[...] = jnp.where(row == 0, jnp.sum(dm, axis=0, keepdims=True), 0.0)

    return pl.pallas_call(
        body, name=name, grid=(1,),
        in_specs=[pl.BlockSpec((8, D), lambda i: (0, 0)), pl.BlockSpec((8, D), lambda i: (0, 0)),
                  pl.BlockSpec((8, N), lambda i: (0, 0))],
        out_specs=[pl.BlockSpec((8, D), lambda i: (0, 0)), pl.BlockSpec((8, D), lambda i: (0, 0)),
                   pl.BlockSpec((8, N), lambda i: (0, 0))],
        out_shape=[jax.ShapeDtypeStruct((8, D), BF16), jax.ShapeDtypeStruct((8, D), F32),
                   jax.ShapeDtypeStruct((8, N), F32)],
        compiler_params=_cparams(("arbitrary",)),
    )(cc8, dsil, dm6)


def _seg_pick(m_ref, is_ctx):
    return jnp.where(is_ctx, m_ref[1:2, :], m_ref[0:1, :])


def _norm_mod(x, g, m6, sh_idx, sc_idx, n_ctx_tiles, tr, name, resid=None):
    T, D = x.shape
    row = pl.BlockSpec((tr, D), lambda i: (i, 0))
    vec = pl.BlockSpec((1, D), lambda i: (0, 0))

    def mcol(idx):
        return pl.BlockSpec((8, D), lambda i: (0, idx))

    def body(*refs):
        if resid is None:
            x_ref, g_ref, sh_ref, sc_ref, h_ref = refs
            xv = x_ref[...]
        else:
            x_ref, f_ref, ga_ref, g_ref, sh_ref, sc_ref, xo_ref, h_ref = refs
        is_ctx = pl.program_id(0) < n_ctx_tiles
        if resid is not None:
            xv = x_ref[...] + _seg_pick(ga_ref, is_ctx) * f_ref[...]
            xo_ref[...] = xv
        rstd = lax.rsqrt(jnp.mean(xv * xv, axis=-1, keepdims=True) + EPS)
        hn = xv * rstd * g_ref[...]
        h_ref[...] = (hn * (1.0 + _seg_pick(sc_ref, is_ctx)) + _seg_pick(sh_ref, is_ctx)).astype(BF16)

    if resid is None:
        ins, in_specs = [x, g, m6, m6], [row, vec, mcol(sh_idx), mcol(sc_idx)]
        out_specs, out_shape = row, jax.ShapeDtypeStruct((T, D), BF16)
    else:
        f, ga_idx = resid
        ins = [x, f, m6, g, m6, m6]
        in_specs = [row, row, mcol(ga_idx), vec, mcol(sh_idx), mcol(sc_idx)]
        out_specs = [row, row]
        out_shape = [jax.ShapeDtypeStruct((T, D), F32), jax.ShapeDtypeStruct((T, D), BF16)]
    return pl.pallas_call(body, name=name, grid=(T // tr,), in_specs=in_specs, out_specs=out_specs,
                          out_shape=out_shape, compiler_params=_cparams(("parallel",)))(*ins)


def _resid(x, f, m6, ga_idx, n_ctx_tiles, tr, name):
    T, D = x.shape
    row = pl.BlockSpec((tr, D), lambda i: (i, 0))

    def body(x_ref, f_ref, ga_ref, o_ref):
        is_ctx = pl.program_id(0) < n_ctx_tiles
        o_ref[...] = x_ref[...] + _seg_pick(ga_ref, is_ctx) * f_ref[...]

    return pl.pallas_call(body, name=name, grid=(T // tr,),
                          in_specs=[row, row, pl.BlockSpec((8, D), lambda i: (0, ga_idx))], out_specs=row,
                          out_shape=jax.ShapeDtypeStruct((T, D), F32),
                          compiler_params=_cparams(("parallel",)))(x, f, m6)


def _resid_bwd(dx, f, m6, ga_idx, n_ctx_tiles, tr, name):
    T, D = dx.shape
    row = pl.BlockSpec((tr, D), lambda i: (i, 0))
    acc = pl.BlockSpec((8, D), lambda i: (0, 0))

    def body(dx_ref, f_ref, ga_ref, df_ref, dga_ref):
        i = pl.program_id(0)
        is_ctx = i < n_ctx_tiles

        @pl.when(i == 0)
        def _():
            dga_ref[...] = jnp.zeros_like(dga_ref)

        dxv = dx_ref[...]
        df_ref[...] = (_seg_pick(ga_ref, is_ctx) * dxv).astype(BF16)
        s = jnp.sum(dxv * f_ref[...], axis=0, keepdims=True)
        r = lax.broadcasted_iota(jnp.int32, (8, D), 0)
        dga_ref[...] += jnp.where(r == jnp.where(is_ctx, 1, 0), s, 0.0)

    return pl.pallas_call(body, name=name, grid=(T // tr,),
                          in_specs=[row, row, pl.BlockSpec((8, D), lambda i: (0, ga_idx))],
                          out_specs=[row, acc],
                          out_shape=[jax.ShapeDtypeStruct((T, D), BF16), jax.ShapeDtypeStruct((8, D), F32)],
                          compiler_params=_cparams(("arbitrary",)))(dx, f, m6)


def _norm_mod_bwd(x, dh, dxres, g, m6, sc_idx, n_ctx_tiles, tr, name):
    T, D = x.shape
    row = pl.BlockSpec((tr, D), lambda i: (i, 0))
    acc = pl.BlockSpec((8, D), lambda i: (0, 0))

    def body(x_ref, dh_ref, dr_ref, g_ref, sc_ref, dx_ref, st_ref):
        i = pl.program_id(0)
        is_ctx = i < n_ctx_tiles

        @pl.when(i == 0)
        def _():
            st_ref[...] = jnp.zeros_like(st_ref)

        xv, dh_v, gv = x_ref[...], dh_ref[...], g_ref[...]
        sc1 = 1.0 + _seg_pick(sc_ref, is_ctx)
        rstd = lax.rsqrt(jnp.mean(xv * xv, axis=-1, keepdims=True) + EPS)
        xhat = xv * rstd
        dxhat = dh_v * sc1 * gv
        dx_ref[...] = dr_ref[...] + rstd * (dxhat - xhat * jnp.mean(dxhat * xhat, axis=-1, keepdims=True))
        dsh = jnp.sum(dh_v, axis=0, keepdims=True)
        dsc = jnp.sum(dh_v * xhat * gv, axis=0, keepdims=True)
        dg = jnp.sum(dh_v * sc1 * xhat, axis=0, keepdims=True)
        r = lax.broadcasted_iota(jnp.int32, (8, D), 0)
        seg = jnp.where(is_ctx, 1, 0)
        st_ref[...] += (jnp.where(r == seg, dsh, 0.0) + jnp.where(r == 2 + seg, dsc, 0.0)
                        + jnp.where(r == 4, dg, 0.0))

    return pl.pallas_call(body, name=name, grid=(T // tr,),
                          in_specs=[row, row, row, pl.BlockSpec((1, D), lambda i: (0, 0)),
                                    pl.BlockSpec((8, D), lambda i: (0, sc_idx))],
                          out_specs=[row, acc],
                          out_shape=[jax.ShapeDtypeStruct((T, D), F32), jax.ShapeDtypeStruct((8, D), F32)],
                          compiler_params=_cparams(("arbitrary",)))(x, dh, dxres, g, m6)


def _loss_head(x, tgt, g, n_ctx_tiles, tr, name):
    T, D = x.shape
    row = pl.BlockSpec((tr, D), lambda i: (i, 0))

    def body(x_ref, t_ref, g_ref, l_ref, dx_ref, dg_ref):
        i = pl.program_id(0)

        @pl.when(i == 0)
        def _():
            l_ref[...] = jnp.zeros_like(l_ref)
            dg_ref[...] = jnp.zeros_like(dg_ref)

        @pl.when(i < n_ctx_tiles)
        def _():
            dx_ref[...] = jnp.zeros_like(dx_ref)

        @pl.when(i >= n_ctx_tiles)
        def _():
            xv, gv = x_ref[...], g_ref[...]
            rstd = lax.rsqrt(jnp.mean(xv * xv, axis=-1, keepdims=True) + EPS)
            xhat = xv * rstd
            e = xhat * gv - t_ref[...]
            l_ref[...] += 0.5 * jnp.sum(jnp.mean(e * e, axis=-1, keepdims=True), axis=0, keepdims=True)
            dy = e * (1.0 / D)
            dxhat = dy * gv
            dx_ref[...] = rstd * (dxhat - xhat * jnp.mean(dxhat * xhat, axis=-1, keepdims=True))
            r = lax.broadcasted_iota(jnp.int32, (8, D), 0)
            dg_ref[...] += jnp.where(r == 0, jnp.sum(dy * xhat, axis=0, keepdims=True), 0.0)

    return pl.pallas_call(
        body, name=name, grid=(T // tr,),
        in_specs=[row, pl.BlockSpec((tr, D), lambda i: (jnp.maximum(i - n_ctx_tiles, 0), 0)),
                  pl.BlockSpec((1, D), lambda i: (0, 0))],
        out_specs=[pl.BlockSpec((8, 128), lambda i: (0, 0)), row, pl.BlockSpec((8, D), lambda i: (0, 0))],
        out_shape=[jax.ShapeDtypeStruct((8, 128), F32), jax.ShapeDtypeStruct((T, D), F32),
                   jax.ShapeDtypeStruct((8, D), F32)],
        compiler_params=_cparams(("arbitrary",)))(x, tgt, g)


def _seq_masks(T, n_ctx, width):
    row = lax.broadcasted_iota(jnp.int32, (T, width), 0)
    in_ctx = row < n_ctx
    return jnp.where(in_ctx, row, row - n_ctx), jnp.where(in_ctx, n_ctx, T - n_ctx)


def _shift_rows(u, off, t_loc, seg_len):
    T = u.shape[0]
    if off == 0:
        return u
    v = pltpu.roll(u, (-off) % T, 0)
    ok = (t_loc + off >= 0) & (t_loc + off < seg_len)
    return jnp.where(ok, v, 0.0)


def _conv_fwd(proj, col0_blk, ncol, conv_w8, conv_b, n_ctx, tc, name):
    T = proj.shape[0]

    def body(u_ref, w_ref, b_ref, o_ref):
        u = u_ref[...]
        t_loc, seg_len = _seq_masks(T, n_ctx, tc)
        acc = jnp.broadcast_to(b_ref[...], u.shape)
        for i in range(CONV_K):
            acc = acc + w_ref[i:i + 1, :] * _shift_rows(u, i - CONV_K // 2, t_loc, seg_len)
        o_ref[...] = _silu(acc)

    return pl.pallas_call(
        body, name=name, grid=(ncol // tc,),
        in_specs=[pl.BlockSpec((T, tc), lambda j: (0, col0_blk + j)), pl.BlockSpec((8, tc), lambda j: (0, j)),
                  pl.BlockSpec((1, tc), lambda j: (0, j))],
        out_specs=pl.BlockSpec((T, tc), lambda j: (0, j)),
        out_shape=jax.ShapeDtypeStruct((T, ncol), F32),
        compiler_params=_cparams(("parallel",)))(proj, conv_w8, conv_b)


def _conv_bwd(proj, col0_blk, d2, w_blk0, conv_w8, conv_b, n_ctx, tc, name, skip=None):
    T, ncol = d2.shape[1], d2.shape[2]

    def body(u_ref, d_ref, w_ref, b_ref, *rest):
        if skip is None:
            du_ref, dw_ref = rest
        else:
            dy_ref, k_ref, du_ref, dw_ref = rest
        u = u_ref[...]
        t_loc, seg_len = _seq_masks(T, n_ctx, tc)
        pre = jnp.broadcast_to(b_ref[...], u.shape)
        for i in range(CONV_K):
            pre = pre + w_ref[i:i + 1, :] * _shift_rows(u, i - CONV_K // 2, t_loc, seg_len)
        r = lax.broadcasted_iota(jnp.int32, (8, tc), 0)
        dact = d_ref[0] + d_ref[1]
        dw = jnp.zeros((8, tc), F32)
        if skip is not None:
            dyv = dy_ref[...]
            dact = dact + (k_ref[0:1, :] + k_ref[1:2, :]) * dyv
            dw = jnp.where(r == CONV_K + 1, jnp.sum(dyv * _silu(pre), axis=0, keepdims=True), 0.0)
        dpre = dact * _dsilu(pre)
        du = jnp.zeros_like(u)
        dw = dw + jnp.where(r == CONV_K, jnp.sum(dpre, axis=0, keepdims=True), 0.0)
        for i in range(CONV_K):
            off = i - CONV_K // 2
            du = du + w_ref[i:i + 1, :] * _shift_rows(dpre, -off, t_loc, seg_len)
            dw = dw + jnp.where(r == i, jnp.sum(dpre * _shift_rows(u, off, t_loc, seg_len), axis=0, keepdims=True),
                                0.0)
        du_ref[...] = du.astype(BF16)
        dw_ref[...] = dw

    col = pl.BlockSpec((T, tc), lambda j: (0, j))
    in_specs = [pl.BlockSpec((T, tc), lambda j: (0, col0_blk + j)), pl.BlockSpec((2, T, tc), lambda j: (0, 0, j)),
                pl.BlockSpec((8, tc), lambda j: (0, w_blk0 + j)), pl.BlockSpec((1, tc), lambda j: (0, w_blk0 + j))]
    operands = [proj, d2, conv_w8, conv_b]
    if skip is not None:
        in_specs += [col, pl.BlockSpec((8, tc), lambda j: (0, j))]
        operands += list(skip)
    return pl.pallas_call(
        body, name=name, grid=(ncol // tc,), in_specs=in_specs,
        out_specs=[col, pl.BlockSpec((8, tc), lambda j: (0, j))],
        out_shape=[jax.ShapeDtypeStruct((T, ncol), BF16), jax.ShapeDtypeStruct((8, ncol), F32)],
        compiler_params=_cparams(("parallel",)))(*operands)


def _pool_core(u, half, t_loc, seg_len, transpose):
    tr = u.shape[0]

    def shift(v, s):
        w = pltpu.roll(v, s % tr, 0)
        ok = (t_loc - s >= 0) & (t_loc - s < seg_len)
        return jnp.where(ok, w, 0.0)

    cnt = (jnp.minimum(t_loc, half) + jnp.minimum(seg_len - t_loc, half)).astype(F32)
    q = u / cnt if transpose else u
    back, ahead, h = q, q, 1
    while h < half:
        back = back + shift(back, h)
        ahead = ahead + shift(ahead, -h)
        h *= 2
    if transpose:
        tot = back + shift(ahead, -1)
        return tot - u
    tot = shift(back, 1) + ahead
    return tot / cnt - u


def _pool_apply(src, col0_blk, out_dtype, n_ctx, tr, pg, transpose, name):
    T = src.shape[0]
    n_ctx_tiles = n_ctx // tr

    def body(u_ref, o_ref):
        i, gi = pl.program_id(0), pl.program_id(1)
        row = lax.broadcasted_iota(jnp.int32, (tr, pg), 0)
        seg_len = jnp.where(i < n_ctx_tiles, tr, GRID_W)
        t_loc = row & (seg_len - 1)
        u = u_ref[...].astype(F32)
        for k_idx, k in enumerate(POOL_WINDOWS):
            @pl.when(gi == k_idx)
            def _(k=k):
                o_ref[...] = _pool_core(u, k // 2, t_loc, seg_len, transpose).astype(o_ref.dtype)

    return pl.pallas_call(
        body, name=name, grid=(T // tr, len(POOL_WINDOWS)),
        in_specs=[pl.BlockSpec((tr, pg), lambda i, gi: (i, col0_blk + gi))],
        out_specs=pl.BlockSpec((tr, pg), lambda i, gi: (i, gi)),
        out_shape=jax.ShapeDtypeStruct((T, pg * len(POOL_WINDOWS)), out_dtype),
        compiler_params=_cparams(("parallel", "parallel")))(src)


def _pool_mix_fwd(pm, pool_w, pool_scale, tr, name):
    T, W = pm.shape
    ng, pg = pool_w.shape[0], pool_w.shape[1]

    def body(p_ref, w_ref, s_ref, o_ref):
        o_ref[...] = (jnp.dot(p_ref[...], w_ref[...], preferred_element_type=F32) * s_ref[...]).astype(BF16)

    return pl.pallas_call(
        body, name=name, grid=(T // tr, ng),
        in_specs=[pl.BlockSpec((tr, pg), lambda i, g: (i, g)), pl.BlockSpec((None, pg, pg), lambda i, g: (g, 0, 0)),
                  pl.BlockSpec((1, pg), lambda i, g: (0, g))],
        out_specs=pl.BlockSpec((tr, pg), lambda i, g: (i, g)),
        out_shape=jax.ShapeDtypeStruct((T, W), BF16),
        compiler_params=_cparams(("parallel", "parallel")))(pm, pool_w, pool_scale)


def _pool_mix_bwd(pm, dpms, pool_w, pool_scale, tr, name):
    T, W = pm.shape
    ng, pg = pool_w.shape[0], pool_w.shape[1]

    def body(p_ref, d_ref, w_ref, s_ref, dp_ref, dw_ref, ds_ref):
        i = pl.program_id(1)

        @pl.when(i == 0)
        def _():
            dw_ref[...] = jnp.zeros_like(dw_ref)
            ds_ref[...] = jnp.zeros_like(ds_ref)

        p, w = p_ref[...], w_ref[...]
        d = d_ref[...].astype(F32)
        pmix = jnp.dot(p, w, preferred_element_type=F32)
        r = lax.broadcasted_iota(jnp.int32, (8, pg), 0)
        ds_ref[...] += jnp.where(r == 0, jnp.sum(d * pmix, axis=0, keepdims=True), 0.0)
        dmix = (d * s_ref[...]).astype(BF16)
        dp_ref[...] = lax.dot_general(dmix, w, (((1,), (1,)), ((), ())), preferred_element_type=F32)
        dw_ref[...] += lax.dot_general(p, dmix, (((0,), (0,)), ((), ())), preferred_element_type=F32)

    return pl.pallas_call(
        body, name=name, grid=(ng, T // tr),
        in_specs=[pl.BlockSpec((tr, pg), lambda g, i: (i, g)), pl.BlockSpec((tr, pg), lambda g, i: (i, g)),
                  pl.BlockSpec((None, pg, pg), lambda g, i: (g, 0, 0)), pl.BlockSpec((1, pg), lambda g, i: (0, g))],
        out_specs=[pl.BlockSpec((tr, pg), lambda g, i: (i, g)), pl.BlockSpec((None, pg, pg), lambda g, i: (g, 0, 0)),
                   pl.BlockSpec((8, pg), lambda g, i: (0, g))],
        out_shape=[jax.ShapeDtypeStruct((T, W), F32), jax.ShapeDtypeStruct((ng, pg, pg), F32),
                   jax.ShapeDtypeStruct((8, W), F32)],
        compiler_params=_cparams(("parallel", "arbitrary")))(pm, dpms, pool_w, pool_scale)


def _chunk_cumsum(v, upper):
    Q = v.shape[0]
    ii = lax.broadcasted_iota(jnp.int32, (Q, Q), 0)
    jj = lax.broadcasted_iota(jnp.int32, (Q, Q), 1)
    tri = ((jj >= ii) if upper else (jj <= ii)).astype(BF16)
    h1 = v.astype(BF16)
    r1 = v - h1.astype(F32)
    h2 = r1.astype(BF16)
    h3 = (r1 - h2.astype(F32)).astype(BF16)
    return (jnp.dot(tri, h1, preferred_element_type=F32) + jnp.dot(tri, h2, preferred_element_type=F32)
            + jnp.dot(tri, h3, preferred_element_type=F32))


def _split3(v):
    h1 = v.astype(BF16)
    r1 = v - h1.astype(F32)
    h2 = r1.astype(BF16)
    return h1, h2, (r1 - h2.astype(F32)).astype(BF16)


def _dt_prep(proj, dt_blk, bias, a_log, expand, n_heads, name):
    T = proj.shape[0]
    Wd = expand.shape[1]
    Q = CHUNK
    row = pl.BlockSpec((Q, 128), lambda i: (i, 0))
    wide = pl.BlockSpec((Q, Wd), lambda i: (i, 0))

    def body(r_ref, b_ref, al_ref, e_ref, l1_ref, l2_ref, l3_ref, dtb_ref, ein_ref, dte_ref, etot_ref):
        xv = r_ref[...] + b_ref[...]
        dt = jnp.maximum(xv, 0.0) + jnp.log(1.0 + jnp.exp(-jnp.abs(xv)))
        a = -jnp.exp(al_ref[...]) * dt
        fwd_col = lax.broadcasted_iota(jnp.int32, (Q, 128), 1) < n_heads
        lam = jnp.where(fwd_col, _chunk_cumsum(a, False), _chunk_cumsum(a, True))
        tot = jnp.where(fwd_col[0:1], lam[Q - 1:Q], lam[0:1])
        l1_ref[...], l2_ref[...], l3_ref[...] = _split3(lam)
        etot_ref[...] = jnp.broadcast_to(jnp.exp(tot), (8, 128))
        ex = e_ref[...]

        def rep(v):
            p1, p2, p3 = _split3(v)
            return (jnp.dot(p1, ex, preferred_element_type=F32) + jnp.dot(p2, ex, preferred_element_type=F32)
                    + jnp.dot(p3, ex, preferred_element_type=F32))

        dtb_ref[...] = rep(dt)
        ein_ref[...] = rep(jnp.exp(lam))
        dte_ref[...] = rep(jnp.exp(tot - lam))

    vec = pl.BlockSpec((1, 128), lambda i: (0, 0))
    return pl.pallas_call(
        body, name=name, grid=(T // Q,),
        in_specs=[pl.BlockSpec((Q, 128), lambda i: (i, dt_blk)), vec, vec, pl.BlockSpec((128, Wd), lambda i: (0, 0))],
        out_specs=[row, row, row, wide, wide, wide, pl.BlockSpec((8, 128), lambda i: (i, 0))],
        out_shape=[jax.ShapeDtypeStruct((T, 128), BF16)] * 3 + [jax.ShapeDtypeStruct((T, Wd), F32)] * 3
        + [jax.ShapeDtypeStruct((T // Q * 8, 128), F32)],
        compiler_params=_cparams(("parallel",)))(proj, bias, a_log, expand)


def _dt_bwd(proj, dt_blk, bias, a_log, ddt, dlam, n_heads, tr, name):
    T = proj.shape[0]
    row = pl.BlockSpec((tr, 128), lambda i: (i, 0))
    vec = pl.BlockSpec((1, 128), lambda i: (0, 0))

    def body(r_ref, b_ref, al_ref, ddt_ref, dl_ref, o_ref, st_ref):
        @pl.when(pl.program_id(0) == 0)
        def _():
            st_ref[...] = jnp.zeros_like(st_ref)

        xv = r_ref[...] + b_ref[...]
        dt = jnp.maximum(xv, 0.0) + jnp.log(1.0 + jnp.exp(-jnp.abs(xv)))
        a_neg = -jnp.exp(al_ref[...])
        col = lax.broadcasted_iota(jnp.int32, (CHUNK, 128), 1)
        dl = dl_ref[...]
        parts = []
        for k in range(tr // CHUNK):
            dk = dl[k * CHUNK:(k + 1) * CHUNK]
            parts.append(jnp.where(col < n_heads, _chunk_cumsum(dk, True), _chunk_cumsum(dk, False)))
        dav = jnp.concatenate(parts, axis=0)
        draw = (ddt_ref[...] + dav * a_neg) * jax.nn.sigmoid(xv)
        o_ref[...] = draw.astype(BF16)
        r = lax.broadcasted_iota(jnp.int32, (8, 128), 0)
        st_ref[...] += (jnp.where(r == 0, jnp.sum(draw, axis=0, keepdims=True), 0.0)
                        + jnp.where(r == 1, jnp.sum(dav * dt, axis=0, keepdims=True) * a_neg, 0.0))

    return pl.pallas_call(
        body, name=name, grid=(T // tr,),
        in_specs=[pl.BlockSpec((tr, 128), lambda i: (i, dt_blk)), vec, vec, row, row],
        out_specs=[row, pl.BlockSpec((8, 128), lambda i: (0, 0))],
        out_shape=[jax.ShapeDtypeStruct((T, 128), BF16), jax.ShapeDtypeStruct((8, 128), F32)],
        compiler_params=_cparams(("arbitrary",)))(proj, bias, a_log, ddt, dlam)


def _scan_chunk(d, pos, nc_ctx, nc):
    rev = jnp.where(pos < nc_ctx, nc_ctx - 1 - pos, nc - 1 - (pos - nc_ctx))
    return jnp.where(d == 0, pos, rev)


def _chunk_mask(d):
    ii = lax.broadcasted_iota(jnp.int32, (CHUNK, CHUNK), 0)
    jj = lax.broadcasted_iota(jnp.int32, (CHUNK, CHUNK), 1)
    return (ii - jj) * jnp.where(d == 0, 1, -1) >= 0


def _ssd_specs(T, G, n_ctx, gpb, chunk_of):
    R, P, N, Q = HPG, HEADDIM, STATE, CHUNK
    H = G * R
    nc, nc_ctx = T // Q, n_ctx // Q
    xw, bw = gpb * R * P, gpb * N
    b_blk0 = (H * P) // bw
    c_blk0 = b_blk0 + G // gpb

    def ch(d, s):
        return chunk_of(d, s, nc_ctx, nc)

    return dict(
        x=pl.BlockSpec((Q, xw), lambda d, g, s: (ch(d, s), g)),
        b=pl.BlockSpec((Q, bw), lambda d, g, s: (ch(d, s), b_blk0 + g)),
        c=pl.BlockSpec((Q, bw), lambda d, g, s: (ch(d, s), c_blk0 + g)),
        col=pl.BlockSpec((None, gpb, Q, R), lambda d, g, s: (d, g, ch(d, s), 0)),
        row=pl.BlockSpec((None, gpb, R, Q), lambda d, g, s: (d, g, 0, ch(d, s))),
        rep=pl.BlockSpec((Q, xw), lambda d, g, s: (ch(d, s), d * (G // gpb) + g)),
        lam_a=pl.BlockSpec((None, gpb * R, 16, Q), lambda d, g, s: (d, g, 0, ch(d, s))),
        lam_b=pl.BlockSpec((None, gpb * R, 16, Q), lambda d, g, s: (d, g, 0, ch(d, s))),
        etot=pl.BlockSpec((None, gpb, None, 8, 128), lambda d, g, s: (d, g, ch(d, s), 0, 0)),
        dsk=pl.BlockSpec((None, 8, xw), lambda d, g, s: (d, 0, g)),
        xd=pl.BlockSpec((None, Q, xw), lambda d, g, s: (d, ch(d, s), g)),
        bd=pl.BlockSpec((None, Q, bw), lambda d, g, s: (d, ch(d, s), g)),
        st=pl.BlockSpec((None, None, gpb * R // 2, 2 * P, N), lambda d, g, s: (d, ch(d, s), g, 0, 0)),
    )


class _Exchange:
    def __init__(self, operand, out_sds, sems, start, finish):
        self.operand, self.out_sds, self.sems, self.start, self.finish = operand, out_sds, sems, start, finish


def _call_with_exchange(body, exch, *, name, grid, in_specs, out_specs, out_shape, scratch_shapes, operands):
    if exch is None:
        return pl.pallas_call(body, name=name, grid=grid, in_specs=in_specs, out_specs=out_specs,
                              out_shape=out_shape, scratch_shapes=scratch_shapes,
                              compiler_params=_cparams(("arbitrary",) * len(grid)))(*operands)
    n_in, n_out, n_scr = len(in_specs), len(out_specs), len(scratch_shapes)

    def fused(*refs):
        ins, c_in = refs[:n_in], refs[n_in]
        outs, c_out = refs[n_in + 1:n_in + 1 + n_out], refs[n_in + 1 + n_out]
        scr = refs[n_in + 2 + n_out:n_in + 2 + n_out + n_scr]
        sems = refs[n_in + 2 + n_out + n_scr:]
        ids = [pl.program_id(a) for a in range(len(grid))]
        first = functools.reduce(lambda p, q: p & q, [i == 0 for i in ids])
        last = functools.reduce(lambda p, q: p & q, [i == n - 1 for i, n in zip(ids, grid)])

        @pl.when(first)
        def _():
            exch.start(c_in, c_out, *sems)

        body(*ins, *outs, *scr)

        @pl.when(last)
        def _():
            exch.finish(c_in, c_out, *sems)

    return pl.pallas_call(fused, name=name, grid=grid, in_specs=list(in_specs) + [ANY],
                          out_specs=list(out_specs) + [ANY], out_shape=list(out_shape) + [exch.out_sds],
                          scratch_shapes=list(scratch_shapes) + list(exch.sems),
                          compiler_params=_cparams(("arbitrary",) * len(grid)))(*operands, exch.operand)


def _ssd_fwd(xbc, dtb, ein, dte, lam_a, lam_b, etot, dsk, G, n_ctx, gpb, name, exch=None):
    T = xbc.shape[0]
    R, P, N, Q = HPG, HEADDIM, STATE, CHUNK
    H = G * R
    nc = T // Q
    sp = _ssd_specs(T, G, n_ctx, gpb, _scan_chunk)

    def body(x_ref, b_ref, c_ref, dt_ref, ein_ref, dte_ref, la_ref, lb_ref, et_ref, dsk_ref, y_ref, st_ref, S):
        d, s = pl.program_id(0), pl.program_id(2)

        @pl.when(s == 0)
        def _():
            S[...] = jnp.zeros_like(S)

        mask = _chunk_mask(d)
        head0 = lax.broadcasted_iota(jnp.int32, (Q, 2 * P), 1) < P
        rows0 = lax.broadcasted_iota(jnp.int32, (2 * P, N), 0) < P
        for gg in range(gpb):
            Bm = b_ref[:, gg * N:(gg + 1) * N].astype(BF16)
            Cm = c_ref[:, gg * N:(gg + 1) * N].astype(BF16)
            Gm = lax.dot_general(Cm, Bm, (((1,), (1,)), ((), ())), preferred_element_type=F32)
            for k in range(R // 2):
                pk = gg * (R // 2) + k
                sl = slice(pk * 2 * P, (pk + 1) * 2 * P)
                xp = x_ref[:, sl]
                xc = xp * dt_ref[:, sl]
                s_in = S[pk]
                y = lax.dot_general(Cm, s_in.astype(BF16), (((1,), (1,)), ((), ())),
                                    preferred_element_type=F32) * ein_ref[:, sl] + dsk_ref[0:1, sl] * xp
                for j in range(2):
                    hr = 2 * pk + j
                    diff = lax.dot_general(la_ref[hr], lb_ref[hr], (((0,), (0,)), ((), ())), preferred_element_type=F32)
                    ldec = jnp.exp(jnp.where(mask, diff, NEG_BIG))
                    xc_j = (jnp.where(head0, xc, 0.0) if j == 0 else jnp.where(head0, 0.0, xc)).astype(BF16)
                    y = y + jnp.dot((Gm * ldec).astype(BF16), xc_j, preferred_element_type=F32)
                y_ref[:, sl] = y
                st_ref[pk] = s_in
                e_all = jnp.where(rows0, et_ref[gg, 2 * k:2 * k + 1, :], et_ref[gg, 2 * k + 1:2 * k + 2, :])
                xd = (xc * dte_ref[:, sl]).astype(BF16)
                S[pk] = e_all * s_in + lax.dot_general(xd, Bm, (((0,), (0,)), ((), ())),
                                                       preferred_element_type=F32)

    return _call_with_exchange(
        body, exch, name=name, grid=(2, G // gpb, nc),
        in_specs=[sp["x"], sp["b"], sp["c"], sp["rep"], sp["rep"], sp["rep"], sp["lam_a"], sp["lam_b"], sp["etot"],
                  sp["dsk"]],
        out_specs=[sp["xd"], sp["st"]],
        out_shape=[jax.ShapeDtypeStruct((2, T, H * P), F32), jax.ShapeDtypeStruct((2, nc, H // 2, 2 * P, N), F32)],
        scratch_shapes=[pltpu.VMEM((gpb * R // 2, 2 * P, N), F32)],
        operands=(xbc, xbc, xbc, dtb, ein, dte, lam_a, lam_b, etot, dsk))


def _ssd_bwd(xbc, dy, states, dtb, ein, dte, lam_a, lam_b, etot, G, n_ctx, gpb, name, exch=None):
    T = xbc.shape[0]
    R, P, N, Q = HPG, HEADDIM, STATE, CHUNK
    H = G * R
    nc = T // Q
    sp = _ssd_specs(T, G, n_ctx, gpb, lambda d, s, nc_ctx, n: _scan_chunk(d, n - 1 - s, nc_ctx, n))

    def body(x_ref, b_ref, c_ref, dy_ref, st_ref, dt_ref, ein_ref, dte_ref, la_ref, lb_ref, et_ref,
             dx_ref, db_ref, dc_ref, ddt_ref, dlc_ref, dlr_ref, dS):
        d, s = pl.program_id(0), pl.program_id(2)

        @pl.when(s == 0)
        def _():
            dS[...] = jnp.zeros_like(dS)

        mask = _chunk_mask(d)
        ri = lax.broadcasted_iota(jnp.int32, (Q, 1), 0)
        is_last = ri == jnp.where(d == 0, Q - 1, 0)
        head0 = lax.broadcasted_iota(jnp.int32, (Q, 2 * P), 1) < P
        rows0 = lax.broadcasted_iota(jnp.int32, (2 * P, N), 0) < P

        def total(v):
            return jnp.sum(jnp.sum(v, axis=1, keepdims=True), axis=0, keepdims=True)

        for gg in range(gpb):
            Bm = b_ref[:, gg * N:(gg + 1) * N].astype(BF16)
            Cm = c_ref[:, gg * N:(gg + 1) * N].astype(BF16)
            Gm = lax.dot_general(Cm, Bm, (((1,), (1,)), ((), ())), preferred_element_type=F32)
            dG = jnp.zeros((Q, Q), F32)
            dB = jnp.zeros((Q, N), F32)
            dC = jnp.zeros((Q, N), F32)
            for k in range(R // 2):
                pk = gg * (R // 2) + k
                sl = slice(pk * 2 * P, (pk + 1) * 2 * P)
                e_in = ein_ref[:, sl]
                dte = dte_ref[:, sl]
                e_all = jnp.where(rows0, et_ref[gg, 2 * k:2 * k + 1, :], et_ref[gg, 2 * k + 1:2 * k + 2, :])
                xp = x_ref[:, sl]
                dtp = dt_ref[:, sl]
                xc = xp * dtp
                xc_b = xc.astype(BF16)
                dyp = dy_ref[:, sl]
                s_in = st_ref[pk]
                s_in_b = s_in.astype(BF16)
                ds_out = dS[pk]
                ds_out_b = ds_out.astype(BF16)
                y_int = lax.dot_general(Cm, s_in_b, (((1,), (1,)), ((), ())), preferred_element_type=F32) * e_in
                b_ds = lax.dot_general(Bm, ds_out_b, (((1,), (1,)), ((), ())), preferred_element_type=F32)
                dxc = dte * b_ds
                u = xc * dxc
                v = dyp * y_int - u
                sse = ds_out * s_in * e_all
                for j in range(2):
                    hr, r = 2 * pk + j, 2 * k + j

                    def pick(a, m0=head0, j=j):
                        return jnp.where(m0, a, 0.0) if j == 0 else jnp.where(m0, 0.0, a)

                    diff = lax.dot_general(la_ref[hr], lb_ref[hr], (((0,), (0,)), ((), ())), preferred_element_type=F32)
                    ldec = jnp.exp(jnp.where(mask, diff, NEG_BIG))
                    dy_j = pick(dyp).astype(BF16)
                    dM = lax.dot_general(dy_j, xc_b, (((1,), (1,)), ((), ())), preferred_element_type=F32)
                    dMl = dM * ldec
                    Wm = dMl * Gm
                    dlam_c = jnp.sum(Wm, axis=1, keepdims=True) + jnp.sum(pick(v), axis=1, keepdims=True)
                    last = total(pick(sse, rows0)) + total(pick(u))
                    dlc_ref[gg, :, r:r + 1] = dlam_c + jnp.where(is_last, last, 0.0)
                    dlr_ref[gg, r:r + 1, :] = -jnp.sum(Wm, axis=0, keepdims=True)
                    dxc = dxc + lax.dot_general((Gm * ldec).astype(BF16), dy_j, (((0,), (0,)), ((), ())),
                                                preferred_element_type=F32)
                    dG = dG + dMl
                dx_ref[:, sl] = dxc * dtp
                t = dxc * xp
                ddt_ref[gg, :, 2 * k:2 * k + 1] = jnp.sum(jnp.where(head0, t, 0.0), axis=1, keepdims=True)
                ddt_ref[gg, :, 2 * k + 1:2 * k + 2] = jnp.sum(jnp.where(head0, 0.0, t), axis=1, keepdims=True)
                edy_b = (e_in * dyp).astype(BF16)
                dC = dC + jnp.dot(edy_b, s_in_b, preferred_element_type=F32)
                dB = dB + jnp.dot((dte * xc).astype(BF16), ds_out_b, preferred_element_type=F32)
                dS[pk] = e_all * ds_out + lax.dot_general(edy_b, Cm, (((0,), (0,)), ((), ())),
                                                          preferred_element_type=F32)
            dG_b = dG.astype(BF16)
            dc_ref[:, gg * N:(gg + 1) * N] = dC + jnp.dot(dG_b, Bm, preferred_element_type=F32)
            db_ref[:, gg * N:(gg + 1) * N] = dB + lax.dot_general(dG_b, Cm, (((0,), (0,)), ((), ())),
                                                                  preferred_element_type=F32)

    return _call_with_exchange(
        body, exch, name=name, grid=(2, G // gpb, nc),
        in_specs=[sp["x"], sp["b"], sp["c"], sp["x"], sp["st"], sp["rep"], sp["rep"], sp["rep"], sp["lam_a"],
                  sp["lam_b"], sp["etot"]],
        out_specs=[sp["xd"], sp["bd"], sp["bd"], sp["col"], sp["col"], sp["row"]],
        out_shape=[jax.ShapeDtypeStruct((2, T, H * P), F32), jax.ShapeDtypeStruct((2, T, G * N), F32),
                   jax.ShapeDtypeStruct((2, T, G * N), F32), jax.ShapeDtypeStruct((2, G, T, R), F32),
                   jax.ShapeDtypeStruct((2, G, T, R), F32), jax.ShapeDtypeStruct((2, G, R, T), F32)],
        scratch_shapes=[pltpu.VMEM((gpb * R // 2, 2 * P, N), F32)],
        operands=(xbc, xbc, xbc, dy, states, dtb, ein, dte, lam_a, lam_b, etot))


def _gnorm_fwd(y2, proj, w, gs, tr, name):
    T, HP = y2.shape[1], y2.shape[2]

    def body(y_ref, z_ref, w_ref, o_ref):
        yz = (y_ref[0] + y_ref[1]) * _silu(z_ref[...])
        for g in range(HP // gs):
            v = yz[:, g * gs:(g + 1) * gs]
            rstd = lax.rsqrt(jnp.mean(v * v, axis=-1, keepdims=True) + EPS)
            o_ref[:, g * gs:(g + 1) * gs] = (v * rstd * w_ref[:, g * gs:(g + 1) * gs]).astype(BF16)

    return pl.pallas_call(
        body, name=name, grid=(T // tr,),
        in_specs=[pl.BlockSpec((2, tr, HP), lambda i: (0, i, 0)), pl.BlockSpec((tr, HP), lambda i: (i, 0)),
                  pl.BlockSpec((1, HP), lambda i: (0, 0))],
        out_specs=pl.BlockSpec((tr, HP), lambda i: (i, 0)),
        out_shape=jax.ShapeDtypeStruct((T, HP), BF16),
        compiler_params=_cparams(("parallel",)))(y2, proj, w)


def _gnorm_bwd(y2, proj, w, dyn, gs, tr, name):
    T, HP = y2.shape[1], y2.shape[2]
    row = pl.BlockSpec((tr, HP), lambda i: (i, 0))

    def body(y_ref, z_ref, w_ref, d_ref, dy_ref, dz_ref, dw_ref):
        @pl.when(pl.program_id(0) == 0)
        def _():
            dw_ref[...] = jnp.zeros_like(dw_ref)

        yv = y_ref[0] + y_ref[1]
        zv = z_ref[...]
        sz = _silu(zv)
        yz = yv * sz
        dv = d_ref[...]
        r = lax.broadcasted_iota(jnp.int32, (8, gs), 0)
        for g in range(HP // gs):
            sl = slice(g * gs, (g + 1) * gs)
            v = yz[:, sl]
            rstd = lax.rsqrt(jnp.mean(v * v, axis=-1, keepdims=True) + EPS)
            xhat = v * rstd
            dyn_g = dv[:, sl]
            dhat = dyn_g * w_ref[:, sl]
            dyz = rstd * (dhat - xhat * jnp.mean(dhat * xhat, axis=-1, keepdims=True))
            dy_ref[:, sl] = dyz * sz[:, sl]
            dz_ref[:, sl] = (dyz * yv[:, sl] * _dsilu(zv[:, sl])).astype(BF16)
            dw_ref[:, sl] += jnp.where(r == 0, jnp.sum(dyn_g * xhat, axis=0, keepdims=True), 0.0)

    return pl.pallas_call(
        body, name=name, grid=(T // tr,),
        in_specs=[pl.BlockSpec((2, tr, HP), lambda i: (0, i, 0)), row, pl.BlockSpec((1, HP), lambda i: (0, 0)), row],
        out_specs=[row, row, pl.BlockSpec((8, HP), lambda i: (0, 0))],
        out_shape=[jax.ShapeDtypeStruct((T, HP), F32), jax.ShapeDtypeStruct((T, HP), BF16),
                   jax.ShapeDtypeStruct((8, HP), F32)],
        compiler_params=_cparams(("arbitrary",)))(y2, proj, w, dyn)


def _merge_fwd(proj, g1_blk, o_ssd, o_pool, tr, name):
    T, D = o_ssd.shape
    row = pl.BlockSpec((tr, D), lambda i: (i, 0))

    def body(g1_ref, g2_ref, a_ref, b_ref, o_ref):
        o_ref[...] = (jax.nn.sigmoid(g1_ref[...]) * a_ref[...]
                      + jax.nn.sigmoid(g2_ref[...]) * b_ref[...]).astype(BF16)

    return pl.pallas_call(
        body, name=name, grid=(T // tr,),
        in_specs=[pl.BlockSpec((tr, D), lambda i: (i, g1_blk)), pl.BlockSpec((tr, D), lambda i: (i, g1_blk + 1)),
                  row, row],
        out_specs=row, out_shape=jax.ShapeDtypeStruct((T, D), BF16),
        compiler_params=_cparams(("parallel",)))(proj, proj, o_ssd, o_pool)


def _merge_bwd(proj, g1_blk, o_ssd, o_pool, dmg, tr, name):
    T, D = o_ssd.shape
    row = pl.BlockSpec((tr, D), lambda i: (i, 0))

    def body(g1_ref, g2_ref, a_ref, b_ref, d_ref, da_ref, db_ref, dg_ref):
        s1, s2 = jax.nn.sigmoid(g1_ref[...]), jax.nn.sigmoid(g2_ref[...])
        dv = d_ref[...]
        da_ref[...] = (s1 * dv).astype(BF16)
        db_ref[...] = (s2 * dv).astype(BF16)
        dg_ref[:, :D] = (dv * a_ref[...] * s1 * (1.0 - s1)).astype(BF16)
        dg_ref[:, D:] = (dv * b_ref[...] * s2 * (1.0 - s2)).astype(BF16)

    return pl.pallas_call(
        body, name=name, grid=(T // tr,),
        in_specs=[pl.BlockSpec((tr, D), lambda i: (i, g1_blk)), pl.BlockSpec((tr, D), lambda i: (i, g1_blk + 1)),
                  row, row, row],
        out_specs=[row, row, pl.BlockSpec((tr, 2 * D), lambda i: (i, 0))],
        out_shape=[jax.ShapeDtypeStruct((T, D), BF16), jax.ShapeDtypeStruct((T, D), BF16),
                   jax.ShapeDtypeStruct((T, 2 * D), BF16)],
        compiler_params=_cparams(("parallel",)))(proj, proj, o_ssd, o_pool, dmg)


def _swiglu_fwd(gu, tr, name):
    T, F2 = gu.shape
    F = F2 // 2
    tc = _tile(F, (1408, 768, 512, 256, 128))
    nb = F // tc

    def body(a_ref, b_ref, o_ref):
        o_ref[...] = (_silu(a_ref[...]) * b_ref[...]).astype(BF16)

    return pl.pallas_call(
        body, name=name, grid=(T // tr, nb),
        in_specs=[pl.BlockSpec((tr, tc), lambda i, j: (i, j)), pl.BlockSpec((tr, tc), lambda i, j: (i, nb + j))],
        out_specs=pl.BlockSpec((tr, tc), lambda i, j: (i, j)),
        out_shape=jax.ShapeDtypeStruct((T, F), BF16),
        compiler_params=_cparams(("parallel", "parallel")))(gu, gu)


def _swiglu_bwd(gu, dact, tr, name):
    T, F2 = gu.shape
    F = F2 // 2
    tc = _tile(F, (1408, 768, 512, 256, 128))
    nb = F // tc

    def body(a_ref, b_ref, d_ref, o_ref):
        is_a = pl.program_id(1) < nb
        av, bv, dv = a_ref[...], b_ref[...], d_ref[...]
        o_ref[...] = jnp.where(is_a, dv * bv * _dsilu(av), dv * _silu(av)).astype(BF16)

    return pl.pallas_call(
        body, name=name, grid=(T // tr, 2 * nb),
        in_specs=[pl.BlockSpec((tr, tc), lambda i, j: (i, j % nb)), pl.BlockSpec((tr, tc), lambda i, j: (i, nb + j % nb)),
                  pl.BlockSpec((tr, tc), lambda i, j: (i, j % nb))],
        out_specs=pl.BlockSpec((tr, tc), lambda i, j: (i, j)),
        out_shape=jax.ShapeDtypeStruct((T, F2), BF16),
        compiler_params=_cparams(("parallel", "parallel")))(gu, gu, dact)


def _adamw(w, g, m, v, name):
    Rr, C = w.shape
    tr = _tile(Rr, (256, 128, 64, 32, 16, 8))
    row = pl.BlockSpec((tr, C), lambda i: (i, 0))

    def body(w_ref, g_ref, m_ref, v_ref, d_ref, mo_ref, vo_ref):
        gv = g_ref[...]
        mn = ADAM_B1 * m_ref[...] + (1.0 - ADAM_B1) * gv
        vn = ADAM_B2 * v_ref[...] + (1.0 - ADAM_B2) * (gv * gv)
        m_hat = mn / (1.0 - ADAM_B1 ** ADAM_STEP)
        v_hat = vn / (1.0 - ADAM_B2 ** ADAM_STEP)
        d_ref[...] = -ADAM_LR * (m_hat / (jnp.sqrt(v_hat) + ADAM_EPS) + ADAM_WD * w_ref[...])
        mo_ref[...] = mn
        vo_ref[...] = vn

    sds = jax.ShapeDtypeStruct((Rr, C), F32)
    return pl.pallas_call(body, name=name, grid=(Rr // tr,), in_specs=[row] * 4, out_specs=[row] * 3,
                          out_shape=[sds] * 3, compiler_params=_cparams(("parallel",)))(w, g, m, v)


def _sum_slots(x, name):
    n, Rr, C = x.shape
    tr = _tile(Rr, (512, 256, 128, 64, 32, 16, 8))

    def body(x_ref, o_ref):
        acc = x_ref[0].astype(F32)
        for k in range(1, n):
            acc = acc + x_ref[k].astype(F32)
        o_ref[...] = acc

    return pl.pallas_call(body, name=name, grid=(Rr // tr,),
                          in_specs=[pl.BlockSpec((n, tr, C), lambda i: (0, i, 0))],
                          out_specs=pl.BlockSpec((tr, C), lambda i: (i, 0)),
                          out_shape=jax.ShapeDtypeStruct((Rr, C), F32),
                          compiler_params=_cparams(("parallel",)))(x)


def _place():
    return lax.axis_index("x"), lax.axis_index("y"), lax.axis_index("c")


def _all_gather(x, name):
    Rr, C = x.shape

    def body(x_ref, out_ref, send_sems, recv_sems, local_sem):
        _gather_start(x_ref, out_ref, send_sems, recv_sems, local_sem)
        _gather_finish(x_ref, out_ref, send_sems, recv_sems, local_sem)

    return pl.pallas_call(
        body, name=name, in_specs=[ANY], out_specs=ANY,
        out_shape=jax.ShapeDtypeStruct((N_DEV, Rr, C), x.dtype), scratch_shapes=_GATHER_SEMS,
    )(x)


_GATHER_SEMS = [pltpu.SemaphoreType.DMA((7,)), pltpu.SemaphoreType.DMA((7,)), pltpu.SemaphoreType.DMA]


def _gather_copies(x_ref, out_ref, send_sems, recv_sems, local_sem):
    mx, my, mc = _place()
    me, sibling = (mx, my, mc), (mx, my, 1 - mc)
    chips = [(1 - mx, my), (mx, 1 - my), (1 - mx, 1 - my)]

    def slot(px, py, pc):
        return out_ref.at[4 * px + 2 * py + pc]

    def copy(k, block, to, src=None):
        return pltpu.make_async_remote_copy(
            src_ref=slot(*block) if src is None else src, dst_ref=slot(*block),
            send_sem=send_sems.at[k], recv_sem=recv_sems.at[k],
            device_id=to, device_id_type=pl.DeviceIdType.MESH)

    return dict(
        mine=pltpu.make_async_copy(x_ref, slot(*me), local_sem),
        first=[copy(0, me, sibling, src=x_ref)] + [copy(1 + j, me, (*chip, mc), src=x_ref)
                                                   for j, chip in enumerate(chips)],
        passed=[copy(4 + j, (*chip, mc), sibling) for j, chip in enumerate(chips)],
        from_chips=[copy(1 + j, (*chip, mc), me) for j, chip in enumerate(chips)],
        from_sibling=[copy(0, sibling, me)] + [copy(4 + j, (*chip, 1 - mc), me) for j, chip in enumerate(chips)],
    )


def _gather_start(*refs):
    cps = _gather_copies(*refs)
    cps["mine"].start()
    for cp in cps["first"]:
        cp.start()


def _gather_finish(*refs):
    cps = _gather_copies(*refs)
    for j in range(3):
        cps["from_chips"][j].wait_recv()
        cps["passed"][j].start()
    for cp in cps["from_sibling"]:
        cp.wait_recv()
    for cp in cps["first"] + cps["passed"]:
        cp.wait_send()
    cps["mine"].wait()


def _pair_exchange(buf, name):
    _, n, Rr, C = buf.shape
    parts = 4
    pr = Rr // parts

    def body(b_ref, got_ref, send_sems, recv_sems):
        mx, my, mc = _place()
        copies = []
        for q in range(n):
            for p in range(parts):
                rows = pl.ds(p * pr, pr)
                cp = pltpu.make_async_remote_copy(
                    src_ref=b_ref.at[1 - mc, q, rows], dst_ref=got_ref.at[q, rows],
                    send_sem=send_sems.at[q * parts + p], recv_sem=recv_sems.at[q * parts + p],
                    device_id=(mx, my, 1 - mc), device_id_type=pl.DeviceIdType.MESH)
                cp.start()
                copies.append(cp)
        for cp in copies:
            cp.wait()

    return pl.pallas_call(
        body, name=name, in_specs=[ANY], out_specs=ANY,
        out_shape=jax.ShapeDtypeStruct((n, Rr, C), buf.dtype),
        scratch_shapes=[pltpu.SemaphoreType.DMA((n * parts,)), pltpu.SemaphoreType.DMA((n * parts,))],
    )(buf)


def _pair_add(buf, got, my_c, name):
    _, n, Rr, C = buf.shape
    tr = _tile(Rr, (512, 256, 128, 64, 32, 16))

    def body(c_ref, b_ref, g_ref, o_ref):
        o_ref[...] = (b_ref[...].astype(F32) + g_ref[...].astype(F32)).astype(BF16)

    return pl.pallas_call(
        body, name=name,
        grid_spec=pltpu.PrefetchScalarGridSpec(
            num_scalar_prefetch=1, grid=(n, Rr // tr),
            in_specs=[pl.BlockSpec((None, None, tr, C), lambda q, i, c: (c[0], q, i, 0)),
                      pl.BlockSpec((None, tr, C), lambda q, i, c: (q, i, 0))],
            out_specs=pl.BlockSpec((None, tr, C), lambda q, i, c: (q, i, 0))),
        out_shape=jax.ShapeDtypeStruct((n, Rr, C), BF16),
        compiler_params=_cparams(("parallel", "parallel")))(my_c, buf, got)


def _chip_exchange(red, name):
    def body(r_ref, out_ref, send_sems, recv_sems, local_sem):
        _chip_exchange_start(r_ref, out_ref, send_sems, recv_sems, local_sem)
        _chip_exchange_finish(r_ref, out_ref, send_sems, recv_sems, local_sem)

    return pl.pallas_call(
        body, name=name, in_specs=[ANY], out_specs=ANY,
        out_shape=jax.ShapeDtypeStruct(red.shape, red.dtype), scratch_shapes=_CHIP_SEMS,
    )(red)


_CHIP_SEMS = [pltpu.SemaphoreType.DMA((3,)), pltpu.SemaphoreType.DMA((3,)), pltpu.SemaphoreType.DMA]


def _chip_exchange_copies(r_ref, out_ref, send_sems, recv_sems, local_sem):
    mx, my, mc = _place()
    chips = [(1 - mx, my), (mx, 1 - my), (1 - mx, 1 - my)]

    def copy(k, src_slot, dst_slot, to):
        return pltpu.make_async_remote_copy(
            src_ref=r_ref.at[src_slot], dst_ref=out_ref.at[dst_slot],
            send_sem=send_sems.at[k], recv_sem=recv_sems.at[k],
            device_id=(*to, mc), device_id_type=pl.DeviceIdType.MESH)

    return dict(
        mine=pltpu.make_async_copy(r_ref.at[2 * mx + my], out_ref.at[2 * mx + my], local_sem),
        sends=[copy(k, 2 * px + py, 2 * mx + my, (px, py)) for k, (px, py) in enumerate(chips)],
        recvs=[copy(k, 2 * px + py, 2 * px + py, (px, py)) for k, (px, py) in enumerate(chips)],
    )


def _chip_exchange_start(*refs):
    cps = _chip_exchange_copies(*refs)
    cps["mine"].start()
    for cp in cps["sends"]:
        cp.start()


def _chip_exchange_finish(*refs):
    cps = _chip_exchange_copies(*refs)
    for cp in cps["recvs"]:
        cp.wait_recv()
    for cp in cps["sends"]:
        cp.wait_send()
    cps["mine"].wait()


def _pad_rows(a, rows):
    return jnp.pad(a, ((0, rows - a.shape[0]), (0, 0)))


class _Layout:
    def __init__(self, D, shards):
        self.D = D
        self.pieces = []
        off = 0
        for name, layer, rows in shards:
            pr = _round_up(rows, 16)
            self.pieces.append((name, layer, rows, pr, off))
            off += pr
        self.rows = _round_up(off, 256)

    def find(self, name, layer):
        for p in self.pieces:
            if p[0] == name and p[1] == layer:
                return p
        raise KeyError(name)


def kernel(x, c, ctx, c_ctx, w_ada, b_ada, g_mix, w_in, conv_w, conv_b, dt_bias, a_log, d_skip, ssd_norm_w, w_ssd_out, pool_w, pool_scale, w_pool_out, w_out, g_ffn, w_gate_up, w_down, g_final, loss_target, m_c_ctx, m_w_ada, m_b_ada, m_g_mix, m_w_in, m_conv_w, m_conv_b, m_dt_bias, m_a_log, m_d_skip, m_ssd_norm_w, m_w_ssd_out, m_pool_w, m_pool_scale, m_w_pool_out, m_w_out, m_g_ffn, m_w_gate_up, m_w_down, m_g_final, v_c_ctx, v_w_ada, v_b_ada, v_g_mix, v_w_in, v_conv_w, v_conv_b, v_dt_bias, v_a_log, v_d_skip, v_ssd_norm_w, v_w_ssd_out, v_pool_w, v_pool_scale, v_w_pool_out, v_w_out, v_g_ffn, v_w_gate_up, v_w_down, v_g_final):
    weights = dict(c_ctx=c_ctx, w_ada=w_ada, b_ada=b_ada, g_mix=g_mix, w_in=w_in, conv_w=conv_w, conv_b=conv_b,
                   dt_bias=dt_bias, a_log=a_log, d_skip=d_skip, ssd_norm_w=ssd_norm_w, w_ssd_out=w_ssd_out,
                   pool_w=pool_w, pool_scale=pool_scale, w_pool_out=w_pool_out, w_out=w_out, g_ffn=g_ffn,
                   w_gate_up=w_gate_up, w_down=w_down, g_final=g_final)
    moms_m = dict(c_ctx=m_c_ctx, w_ada=m_w_ada, b_ada=m_b_ada, g_mix=m_g_mix, w_in=m_w_in, conv_w=m_conv_w,
                  conv_b=m_conv_b, dt_bias=m_dt_bias, a_log=m_a_log, d_skip=m_d_skip, ssd_norm_w=m_ssd_norm_w,
                  w_ssd_out=m_w_ssd_out, pool_w=m_pool_w, pool_scale=m_pool_scale, w_pool_out=m_w_pool_out,
                  w_out=m_w_out, g_ffn=m_g_ffn, w_gate_up=m_w_gate_up, w_down=m_w_down, g_final=m_g_final)
    moms_v = dict(c_ctx=v_c_ctx, w_ada=v_w_ada, b_ada=v_b_ada, g_mix=v_g_mix, w_in=v_w_in, conv_w=v_conv_w,
                  conv_b=v_conv_b, dt_bias=v_dt_bias, a_log=v_a_log, d_skip=v_d_skip, ssd_norm_w=v_ssd_norm_w,
                  w_ssd_out=v_w_ssd_out, pool_w=v_pool_w, pool_scale=v_pool_scale, w_pool_out=v_w_pool_out,
                  w_out=v_w_out, g_ffn=v_g_ffn, w_gate_up=v_w_gate_up, w_down=v_w_down, g_final=v_g_final)
    order = ["c_ctx", "w_ada", "b_ada", "g_mix", "w_in", "conv_w", "conv_b", "dt_bias", "a_log", "d_skip",
             "ssd_norm_w", "w_ssd_out", "pool_w", "pool_scale", "w_pool_out", "w_out", "g_ffn", "w_gate_up",
             "w_down", "g_final"]
    big = ["w_ada", "w_in", "conv_w", "w_ssd_out", "pool_w", "w_pool_out", "w_out", "w_gate_up", "w_down"]
    small = [n for n in order if n not in big]

    depth = w_in.shape[0]
    L, D = x.shape[1], x.shape[2]
    n_ctx = ctx.shape[1]
    T = n_ctx + L
    in_cols = w_in.shape[2] * N_DEV
    xbc_w = conv_w.shape[2] * N_DEV
    dinner = ssd_norm_w.shape[1]
    H = dt_bias.shape[2]
    G = H // HPG
    GN = G * STATE
    assert xbc_w == dinner + 2 * GN and dinner == H * HEADDIM
    assert in_cols == dinner + xbc_w + 2 * H + D + 2 * D
    assert dinner == 2 * D and GN == D and 2 * H <= 128
    F = w_down.shape[1] * N_DEV
    pg = pool_w.shape[3]
    tr = n_ctx
    assert L % tr == 0 and tr % GRID_W == 0 and tr % CHUNK == 0 and L % CHUNK == 0
    n_ctx_tiles = 1
    NP = _round_up(9 * D + 128, 512)
    gs = dinner // G
    off_xbc, off_dt, off_pool = dinner, dinner + xbc_w, dinner + xbc_w + 2 * H
    off_gate = off_pool + D

    def shard_rows(name, l):
        w = weights[name][l]
        if name in ("w_ada", "w_in", "w_gate_up"):
            return w.T
        if name == "conv_w":
            w8 = _pad_rows(w, 8)
            hi = w8.astype(BF16)
            lo = (w8 - hi.astype(F32)).astype(BF16)
            return jnp.concatenate([hi, lo], axis=0).reshape(-1, D)
        if name == "pool_w":
            return w.reshape(-1, D)
        return w

    first_needed = ["w_ada", "w_in"]
    shard = {(n, l): shard_rows(n, l) for l in range(depth) for n in big}
    gather_groups = [[first_needed, [n for n in big if n not in first_needed]] if l == 0 else [big]
                     for l in range(depth)]
    glays = [[_Layout(D, [(n, l, shard[(n, l)].shape[0]) for n in grp]) for grp in gather_groups[l]]
             for l in range(depth)]

    def packed(l, gi):
        lay = glays[l][gi]
        rows = jnp.concatenate([_pad_rows(shard[(n, l)].astype(BF16), lay.find(n, l)[3])
                                for n in gather_groups[l][gi]], axis=0)
        return _pad_rows(rows, lay.rows)

    def gather_exchange(l, gi):
        return _Exchange(packed(l, gi), jax.ShapeDtypeStruct((N_DEV, glays[l][gi].rows, D), BF16), _GATHER_SEMS,
                         _gather_start, _gather_finish)

    gathered = [[None] * len(g) for g in gather_groups]
    gathered[0][0] = _all_gather(packed(0, 0), "gather_weights")

    def full(name, l):
        gi = [name in grp for grp in gather_groups[l]].index(True)
        _, _, rows, _, off = glays[l][gi].find(name, l)
        return gathered[l][gi][:, off:off + rows, :]

    def w_inT_new(l):
        w = full("w_in", l).reshape(in_cols, D)
        parts = [w[:off_xbc], w[off_xbc:off_dt], w[off_pool:off_gate], w[off_gate:], w[off_dt:off_pool]]
        return _pad_rows(jnp.concatenate(parts, axis=0), NP)

    xs0 = jnp.concatenate([ctx[0], x[0]], axis=0)
    cc8 = _pad_rows(jnp.concatenate([c, c_ctx[None, :]], axis=0), 8)
    tgt = loss_target[0]

    def vec(a):
        return a.reshape(1, -1)

    def pad128(a):
        return jnp.pad(a.reshape(1, -1), ((0, 0), (0, 128 - 2 * H)))

    expand = (jnp.arange(128)[:, None] == jnp.arange(2 * H * HEADDIM)[None, :] // HEADDIM).astype(BF16)

    def from4(arr):
        return jnp.pad(arr.transpose(2, 0, 1, 3).reshape(T, 2 * H), ((0, 0), (0, 128 - 2 * H)))

    dt_blk = (9 * D) // 128
    conv_tc = 128
    ssd_gpb, ssd_gpb_bwd = 4, 2
    saved = []
    xcur = xs0
    for l in range(depth):
        W = dict(adaT=full("w_ada", l).reshape(6 * D, D), inT=w_inT_new(l))
        m6 = _ada_fwd(cc8, W["adaT"], vec(b_ada[l]), "ada_fwd")
        h = _norm_mod(xcur, vec(g_mix[l]), m6, 0, 1, n_ctx_tiles, tr, "norm_mod")
        if len(gather_groups[l]) > 1:
            proj, gathered[l][1] = _mm(h, W["inT"], "nt", F32, "mm_in_gather", exch=gather_exchange(l, 1))
        else:
            proj = _mm(h, W["inT"], "nt", F32, "mm_in")
        W.update(
            ssd=full("w_ssd_out", l).reshape(dinner, D), po=full("w_pool_out", l).reshape(D, D),
            out=full("w_out", l).reshape(D, D), guT=full("w_gate_up", l).reshape(2 * F, D),
            down=full("w_down", l).reshape(F, D),
            pool=full("pool_w", l).reshape(N_DEV, len(POOL_WINDOWS), pg // N_DEV, pg).transpose(1, 0, 2, 3)
            .reshape(len(POOL_WINDOWS), pg, pg),
        )
        cw = full("conv_w", l).reshape(N_DEV, 16, xbc_w // N_DEV).astype(F32)
        W["conv8"] = (cw[:, :8] + cw[:, 8:]).transpose(1, 0, 2).reshape(8, xbc_w)
        l1, l2, l3, dtb, ein, dte, etot = _dt_prep(proj, dt_blk, pad128(dt_bias[l]), pad128(a_log[l]), expand, H,
                                                   "dt_prep")
        L1, L2, L3 = (v[:, :2 * H].T.reshape(2, H, 1, T) for v in (l1, l2, l3))
        k16 = jnp.arange(16).reshape(1, 1, 16, 1)

        def rows16(at):
            terms = jnp.where(k16 == at, L1, jnp.where(k16 == at + 1, L2, L3))
            return jnp.where((k16 >= at) & (k16 < at + 3), terms, (k16 < 6).astype(BF16))

        lam_a, lam_b = rows16(0), -rows16(3) + 2 * (k16 < 3).astype(BF16)
        et = etot.reshape(T // CHUNK, 8, 128)[:, 0, :2 * H].reshape(T // CHUNK, 2, G, HPG).transpose(1, 2, 0, 3)
        etot5 = jnp.pad(jnp.broadcast_to(et[..., None], et.shape + (128,)),
                        ((0, 0), (0, 0), (0, 0), (0, 8 - HPG), (0, 0)))
        dsk = jnp.pad(jnp.repeat(d_skip[l], HEADDIM, axis=1)[:, None, :], ((0, 0), (0, 7), (0, 0)))
        scan_ops = (dtb, ein, dte, lam_a, lam_b, etot5)
        xbc = _conv_fwd(proj, dinner // conv_tc, xbc_w, W["conv8"], vec(conv_b[l]), n_ctx, conv_tc, "conv_fwd")
        if l + 1 < depth:
            y2, states, gathered[l + 1][0] = _ssd_fwd(xbc, *scan_ops, dsk, G, n_ctx, ssd_gpb, "ssd_fwd_gather",
                                                      gather_exchange(l + 1, 0))
        else:
            y2, states = _ssd_fwd(xbc, *scan_ops, dsk, G, n_ctx, ssd_gpb, "ssd_fwd")
        yn = _gnorm_fwd(y2, proj, vec(ssd_norm_w[l]), gs, tr, "gnorm_fwd")
        pm = _pool_apply(proj, (6 * D) // pg, BF16, n_ctx, tr, pg, False, "pool_fwd")
        pms = _pool_mix_fwd(pm, W["pool"], vec(pool_scale[l]), tr, "pool_mix_fwd")
        o_ssd = _mm(yn, W["ssd"], "nn", F32, "mm_ssd_out")
        o_pool = _mm(pms, W["po"], "nn", F32, "mm_pool_out")
        mg = _merge_fwd(proj, 7, o_ssd, o_pool, tr, "merge_fwd")
        mo = _mm(mg, W["out"], "nn", F32, "mm_out")
        x1, h2 = _norm_mod(xcur, vec(g_ffn[l]), m6, 3, 4, n_ctx_tiles, tr, "resid_norm_mod", resid=(mo, 2))
        gu = _mm(h2, W["guT"], "nt", F32, "mm_gate_up")
        act = _swiglu_fwd(gu, tr, "swiglu_fwd")
        f = _mm(act, W["down"], "nn", F32, "mm_down")
        saved.append(dict(W=W, m6=m6, x0=xcur, h=h, proj=proj, scan_ops=scan_ops, xbc=xbc, y2=y2,
                          states=states, yn=yn, pm=pm, pms=pms, o_ssd=o_ssd, o_pool=o_pool, mg=mg, mo=mo, x1=x1,
                          h2=h2, gu=gu, act=act, f=f))
        xcur = _resid(x1, f, m6, 5, n_ctx_tiles, tr, "resid")

    loss_blk, dx, dgf = _loss_head(xcur, tgt, vec(g_final), n_ctx_tiles, tr, "loss_head")
    loss = lax.psum(loss_blk[0, 0], MESH_AXES)

    big_rows = {}
    small_g = {n: [None] * depth for n in small}
    d_c_ctx = jnp.zeros((D,), F32)
    late = ["w_ada", "w_in", "conv_w"]
    rs_groups = [[[n for n in big if n not in late], late] if l == 0 else [big] for l in range(depth)]
    rlays = [[_Layout(D, [(n, l, shard[(n, l)].shape[0]) for n in grp]) for grp in rs_groups[l]]
             for l in range(depth)]
    my_c = lax.axis_index("c").astype(jnp.int32).reshape(1)

    def reduce_pair(l, gi):
        lay = rlays[l][gi]
        gparts = [jnp.pad(big_rows[(n, l)].astype(BF16), ((0, 0), (0, pr - rows), (0, 0)))
                  for n, _, rows, pr, _ in lay.pieces]
        gparts.append(jnp.zeros((N_DEV, lay.rows - sum(p[3] for p in lay.pieces), D), BF16))
        gbuf = jnp.concatenate(gparts, axis=1)
        gbuf = gbuf.reshape(2, 2, 2, lay.rows, D).transpose(2, 0, 1, 3, 4).reshape(2, 4, lay.rows, D)
        return _pair_add(gbuf, _pair_exchange(gbuf, "rs_pair_exchange"), my_c, "rs_pair_add")

    slots = {}
    pending = None
    for l in reversed(range(depth)):
        S = saved[l]
        W, m6, proj = S["W"], S["m6"], S["proj"]
        df, dga2 = _resid_bwd(dx, S["f"], m6, 5, n_ctx_tiles, tr, "resid_bwd")
        dact = _mm(df, W["down"], "nt", F32, "mm_down_dx")
        g_down = _mm(S["act"], df, "tn", BF16,"mm_down_dw")
        dgu = _swiglu_bwd(S["gu"], dact, tr, "swiglu_bwd")
        dh2 = _mm(dgu, W["guT"], "nn", F32, "mm_gate_up_dx")
        g_guT = _mm(dgu, S["h2"], "tn", BF16,"mm_gate_up_dw")
        dx1, st2 = _norm_mod_bwd(S["x1"], dh2, dx, vec(g_ffn[l]), m6, 4, n_ctx_tiles, tr, "norm_mod_bwd")
        dmo, dga1 = _resid_bwd(dx1, S["mo"], m6, 2, n_ctx_tiles, tr, "resid_bwd")
        dmg = _mm(dmo, W["out"], "nt", F32, "mm_out_dx")
        g_out = _mm(S["mg"], dmo, "tn", BF16,"mm_out_dw")
        do_ssd, do_pool, dgl = _merge_bwd(proj, 7, S["o_ssd"], S["o_pool"], dmg, tr, "merge_bwd")
        dyn = _mm(do_ssd, W["ssd"], "nt", F32, "mm_ssd_out_dx")
        g_ssd = _mm(S["yn"], do_ssd, "tn", BF16,"mm_ssd_out_dw")
        dpms = _mm(do_pool, W["po"], "nt", F32, "mm_pool_out_dx")
        g_po = _mm(S["pms"], do_pool, "tn", BF16,"mm_pool_out_dw")
        dpm, g_pool, dps = _pool_mix_bwd(S["pm"], dpms, W["pool"], vec(pool_scale[l]), tr, "pool_mix_bwd")
        dup = _pool_apply(dpm, 0, BF16, n_ctx, tr, pg, True, "pool_bwd")
        dy, dz, dnw = _gnorm_bwd(S["y2"], proj, vec(ssd_norm_w[l]), dyn, gs, tr, "gnorm_bwd")
        big_rows[("w_ssd_out", l)] = g_ssd.reshape(N_DEV, -1, D)
        big_rows[("pool_w", l)] = g_pool.reshape(len(POOL_WINDOWS), N_DEV, pg // N_DEV, pg).transpose(1, 0, 2, 3) \
            .reshape(N_DEV, -1, D)
        big_rows[("w_pool_out", l)] = g_po.reshape(N_DEV, -1, D)
        big_rows[("w_out", l)] = g_out.reshape(N_DEV, -1, D)
        big_rows[("w_gate_up", l)] = g_guT.reshape(N_DEV, -1, D)
        big_rows[("w_down", l)] = g_down.reshape(N_DEV, -1, D)
        riding = [] if pending is None else [pending]
        riding += [((l, gi), reduce_pair(l, gi)) for gi in range(len(rs_groups[l]) - 1)]
        if not riding:
            dxs2, db2, dc2, ddt4, dlc4, dlr4 = _ssd_bwd(S["xbc"], dy, S["states"], *S["scan_ops"], G, n_ctx,
                                                        ssd_gpb_bwd, "ssd_bwd")
        else:
            red = jnp.concatenate([r for _, r in riding], axis=1)
            exch = _Exchange(red, jax.ShapeDtypeStruct(red.shape, BF16), _CHIP_SEMS, _chip_exchange_start,
                             _chip_exchange_finish)
            dxs2, db2, dc2, ddt4, dlc4, dlr4, got = _ssd_bwd(
                S["xbc"], dy, S["states"], *S["scan_ops"], G, n_ctx, ssd_gpb_bwd, "ssd_bwd_exchange", exch)
            at = 0
            for key, r in riding:
                slots[key] = got[:, at:at + r.shape[1]]
                at += r.shape[1]
        dskv = _pad_rows(jnp.repeat(d_skip[l], HEADDIM, axis=1), 8)
        cb = vec(conv_b[l])
        dxbc_x, dconv_x = _conv_bwd(proj, dinner // conv_tc, dxs2, 0, W["conv8"], cb, n_ctx, conv_tc, "conv_bwd_x",
                                    skip=(dy, dskv))
        dxbc_b, dconv_b = _conv_bwd(proj, 2 * dinner // conv_tc, db2, dinner // conv_tc, W["conv8"], cb, n_ctx,
                                    conv_tc, "conv_bwd_bc")
        dxbc_c, dconv_c = _conv_bwd(proj, (2 * dinner + GN) // conv_tc, dc2, (dinner + GN) // conv_tc, W["conv8"], cb,
                                    n_ctx, conv_tc, "conv_bwd_bc")
        dconv = jnp.concatenate([dconv_x, dconv_b, dconv_c], axis=1)
        ddt_raw, dtst = _dt_bwd(proj, dt_blk, pad128(dt_bias[l]), pad128(a_log[l]), from4(ddt4),
                                from4(dlc4 + dlr4.transpose(0, 1, 3, 2)), H, tr, "dt_bwd")
        dproj = jnp.concatenate([dz, dxbc_x, dxbc_b, dxbc_c, dup, dgl, ddt_raw,
                                 jnp.zeros((T, NP - 9 * D - 128), BF16)], axis=1)
        dh = _mm(dproj, W["inT"], "nn", F32, "mm_in_dx")
        g_inT_new = _mm(dproj, S["h"], "tn", BF16,"mm_in_dw")
        dx0, st1 = _norm_mod_bwd(S["x0"], dh, dx1, vec(g_mix[l]), m6, 1, n_ctx_tiles, tr, "norm_mod_bwd")
        dm6 = _pad_rows(jnp.concatenate([st1[0:2], st1[2:4], dga1[0:2], st2[0:2], st2[2:4], dga2[0:2]], axis=1), 8)
        dsil = _mm(dm6, W["adaT"], "nn", F32, "mm_ada_dx")
        sil_b, dcc, dbada = _ada_bwd_small(cc8, dsil, dm6, "ada_bwd_small")
        g_adaT = _mm(dm6, sil_b, "tn", BF16,"mm_ada_dw")
        d_c_ctx = d_c_ctx + dcc[1]
        dx = dx0

        g_inT = jnp.concatenate([g_inT_new[:6 * D], g_inT_new[9 * D:9 * D + 2 * H], g_inT_new[6 * D:9 * D]], axis=0)
        big_rows[("w_ada", l)] = g_adaT.reshape(N_DEV, -1, D)
        big_rows[("w_in", l)] = g_inT.reshape(N_DEV, -1, D)
        big_rows[("conv_w", l)] = jnp.pad(
            dconv[:CONV_K].reshape(CONV_K, N_DEV, xbc_w // N_DEV).transpose(1, 0, 2),
            ((0, 0), (0, 16 - CONV_K), (0, 0))).reshape(N_DEV, -1, D)
        small_g["b_ada"][l] = dbada[0]
        small_g["g_mix"][l] = st1[4]
        small_g["conv_b"][l] = dconv[CONV_K]
        small_g["dt_bias"][l] = dtst[0, :2 * H].reshape(2, H)
        small_g["a_log"][l] = dtst[1, :2 * H].reshape(2, H)
        dsk_h = dconv_x[CONV_K + 1].reshape(H, HEADDIM).sum(axis=-1)
        small_g["d_skip"][l] = jnp.stack([dsk_h, dsk_h])
        small_g["ssd_norm_w"][l] = dnw[0]
        small_g["pool_scale"][l] = dps[0]
        small_g["g_ffn"][l] = st2[4]

        last_gi = len(rs_groups[l]) - 1
        pending = ((l, last_gi), reduce_pair(l, last_gi))
    slots[pending[0]] = _chip_exchange(pending[1], "rs_chip_exchange")
    g_local = {key: _sum_slots(s, "rs_chip_add") for key, s in slots.items()}
    grad_x = dx[n_ctx:][None]

    def local_grad(name):
        outs = []
        for l in range(depth):
            gi = [name in grp for grp in rs_groups[l]].index(True)
            _, _, rows, _, off = rlays[l][gi].find(name, l)
            piece = g_local[(l, gi)][off:off + rows]
            if name in ("w_ada", "w_in", "w_gate_up"):
                piece = piece.T
            elif name == "conv_w":
                piece = piece.reshape(16, -1)[:CONV_K]
            elif name == "pool_w":
                piece = piece.reshape(weights[name].shape[1:])
            outs.append(piece)
        return jnp.stack(outs)

    grads = {n: local_grad(n) for n in big}

    small_full = {"c_ctx": d_c_ctx, "g_final": dgf[0]}
    for n in small:
        if n not in small_full:
            small_full[n] = jnp.stack(small_g[n])

    def pack_small(tree):
        flat = jnp.concatenate([tree[n].reshape(-1).astype(F32) for n in small])
        rows = _round_up(-(-flat.shape[0] // D), 8)
        return jnp.pad(flat, (0, rows * D - flat.shape[0])).reshape(rows, D)

    def unpack_small(buf):
        flat, out, off = buf.reshape(-1), {}, 0
        for n in small:
            sz = weights[n].size
            out[n] = flat[off:off + sz].reshape(weights[n].shape)
            off += sz
        return out

    g_small = _sum_slots(_all_gather(pack_small(small_full), "gather_small_grads"), "sum_small_grads")
    grads.update(unpack_small(g_small))

    delta, new_m, new_v = {}, {}, {}
    for n in big:
        shp = weights[n].shape
        d_, m_, v_ = _adamw(weights[n].reshape(-1, shp[-1]), grads[n].reshape(-1, shp[-1]),
                            moms_m[n].reshape(-1, shp[-1]), moms_v[n].reshape(-1, shp[-1]), "adamw_" + n)
        delta[n], new_m[n], new_v[n] = d_.reshape(shp), m_.reshape(shp), v_.reshape(shp)
    d_, m_, v_ = _adamw(pack_small(weights), g_small, pack_small(moms_m), pack_small(moms_v), "adamw_small")
    delta.update(unpack_small(d_))
    new_m.update(unpack_small(m_))
    new_v.update(unpack_small(v_))

    return (loss, grad_x, *[grads[n] for n in order], *[delta[n] for n in order],
            *[new_m[n] for n in order], *[new_v[n] for n in order])
```

```python
import functools

import jax
import jax.numpy as jnp
from jax import lax
from jax.experimental import pallas as pl
from jax.experimental.pallas import tpu as pltpu

F32 = jnp.float32
BF16 = jnp.bfloat16
N_DEV = 8
EPS = 1e-6
GRID_W = 64
POOL_WINDOWS = (2, 4, 8, 16)
HEADDIM = 64
STATE = 128
CHUNK = 128
HPG = 4
CONV_K = 5
ADAM_LR, ADAM_B1, ADAM_B2, ADAM_EPS, ADAM_WD, ADAM_STEP = 0.001, 0.9, 0.999, 1e-08, 0.01, 10
NEG_BIG = -1e30
MESH_AXES = ("x", "y", "c")
ANY = pl.BlockSpec(memory_space=pl.ANY)


def _tile(n, cands):
    for t in cands:
        if n % t == 0:
            return t
    return n


def _round_up(n, m):
    return -(-n // m) * m


def _silu(x):
    return x * jax.nn.sigmoid(x)


def _dsilu(x):
    s = jax.nn.sigmoid(x)
    return s * (1.0 + x * (1.0 - s))


def _cparams(sem):
    return pltpu.CompilerParams(dimension_semantics=sem, vmem_limit_bytes=56 * 1024 * 1024)


def _mm(a, b, mode, out_dtype, name, exch=None):
    if mode == "tn":
        K, M = a.shape
        N = b.shape[1]
        tm = _tile(M, (512, 256, 128))
        tn = _tile(N, (1024, 512, 256, 128))
        tk = K if K <= 4608 else _tile(K, (1088, 544, 512, 256, 128))
        a_spec = pl.BlockSpec((tk, tm), lambda i, j, k: (k, i))
        b_spec = pl.BlockSpec((tk, tn), lambda i, j, k: (k, j))
        dims = (((0,), (0,)), ((), ()))
    else:
        M, K = a.shape
        N = b.shape[0] if mode == "nt" else b.shape[1]
        tm = _tile(M, (1088, 544, 512, 256, 128))
        tk = K if K <= 4096 else max(t for t in range(128, 2817, 128) if K % t == 0)
        tn = _tile(N, (512, 256, 128)) if tk == K else _tile(N, (1024, 512, 256, 128))
        a_spec = pl.BlockSpec((tm, tk), lambda i, j, k: (i, k))
        if mode == "nt":
            b_spec = pl.BlockSpec((tn, tk), lambda i, j, k: (j, k))
            dims = (((1,), (1,)), ((), ()))
        else:
            b_spec = pl.BlockSpec((tk, tn), lambda i, j, k: (k, j))
            dims = (((1,), (0,)), ((), ()))
    nk = K // tk

    def body(a_ref, b_ref, o_ref, *acc):
        if nk == 1:
            o_ref[...] = lax.dot_general(a_ref[...].astype(BF16), b_ref[...].astype(BF16), dims,
                                         preferred_element_type=F32).astype(o_ref.dtype)
            return
        k = pl.program_id(2)

        @pl.when(k == 0)
        def _():
            acc[0][...] = jnp.zeros_like(acc[0])

        acc[0][...] += lax.dot_general(a_ref[...].astype(BF16), b_ref[...].astype(BF16), dims,
                                       preferred_element_type=F32)

        @pl.when(k == nk - 1)
        def _():
            o_ref[...] = acc[0][...].astype(o_ref.dtype)

    call = dict(name=name, grid=(M // tm, N // tn, nk), in_specs=[a_spec, b_spec],
                scratch_shapes=[pltpu.VMEM((tm, tn), F32)] if nk > 1 else [])
    o_spec, o_sds = pl.BlockSpec((tm, tn), lambda i, j, k: (i, j)), jax.ShapeDtypeStruct((M, N), out_dtype)
    if exch is not None:
        return _call_with_exchange(body, exch, out_specs=[o_spec], out_shape=[o_sds], operands=(a, b), **call)
    return pl.pallas_call(body, out_specs=o_spec, out_shape=o_sds,
                          compiler_params=_cparams(("parallel", "parallel", "arbitrary")), **call)(a, b)


def _ada_fwd(cc8, w_adaT, b_ada, name):
    D = cc8.shape[1]
    N = w_adaT.shape[0]
    tn = _tile(N, (512, 256, 128))

    def body(c_ref, w_ref, b_ref, o_ref):
        a = _silu(c_ref[...]).astype(BF16)
        o_ref[...] = lax.dot_general(a, w_ref[...], (((1,), (1,)), ((), ())),
                                     preferred_element_type=F32) + b_ref[...]

    return pl.pallas_call(
        body, name=name, grid=(N // tn,),
        in_specs=[pl.BlockSpec((8, D), lambda j: (0, 0)), pl.BlockSpec((tn, D), lambda j: (j, 0)),
                  pl.BlockSpec((1, tn), lambda j: (0, j))],
        out_specs=pl.BlockSpec((8, tn), lambda j: (0, j)),
        out_shape=jax.ShapeDtypeStruct((8, N), F32),
        compiler_params=_cparams(("parallel",)),
    )(cc8, w_adaT, b_ada)


def _ada_bwd_small(cc8, dsil, dm6, name):
    D = cc8.shape[1]
    N = dm6.shape[1]

    def body(c_ref, ds_ref, dm_ref, sil_ref, dc_ref, db_ref):
        c = c_ref[...]
        sil_ref[...] = _silu(c).astype(BF16)
        dc_ref[...] = ds_ref[...] * _dsilu(c)
        dm = dm_ref[...]
        row = lax.broadcasted_iota(jnp.int32, dm.shape, 0)
        db_ref[...] = jnp.where(row == 0, jnp.sum(dm, axis=0, keepdims=True), 0.0)

    return pl.pallas_call(
        body, name=name, grid=(1,),
        in_specs=[pl.BlockSpec((8, D), lambda i: (0, 0)), pl.BlockSpec((8, D), lambda i: (0, 0)),
                  pl.BlockSpec((8, N), lambda i: (0, 0))],
        out_specs=[pl.BlockSpec((8, D), lambda i: (0, 0)), pl.BlockSpec((8, D), lambda i: (0, 0)),
                   pl.BlockSpec((8, N), lambda i: (0, 0))],
        out_shape=[jax.ShapeDtypeStruct((8, D), BF16), jax.ShapeDtypeStruct((8, D), F32),
                   jax.ShapeDtypeStruct((8, N), F32)],
        compiler_params=_cparams(("arbitrary",)),
    )(cc8, dsil, dm6)


def _seg_pick(m_ref, is_ctx):
    return jnp.where(is_ctx, m_ref[1:2, :], m_ref[0:1, :])


def _norm_mod(x, g, m6, sh_idx, sc_idx, n_ctx_tiles, tr, name, resid=None):
    T, D = x.shape
    row = pl.BlockSpec((tr, D), lambda i: (i, 0))
    vec = pl.BlockSpec((1, D), lambda i: (0, 0))

    def mcol(idx):
        return pl.BlockSpec((8, D), lambda i: (0, idx))

    def body(*refs):
        if resid is None:
            x_ref, g_ref, sh_ref, sc_ref, h_ref = refs
            xv = x_ref[...]
        else:
            x_ref, f_ref, ga_ref, g_ref, sh_ref, sc_ref, xo_ref, h_ref = refs
        is_ctx = pl.program_id(0) < n_ctx_tiles
        if resid is not None:
            xv = x_ref[...] + _seg_pick(ga_ref, is_ctx) * f_ref[...]
            xo_ref[...] = xv
        rstd = lax.rsqrt(jnp.mean(xv * xv, axis=-1, keepdims=True) + EPS)
        hn = xv * rstd * g_ref[...]
        h_ref[...] = (hn * (1.0 + _seg_pick(sc_ref, is_ctx)) + _seg_pick(sh_ref, is_ctx)).astype(BF16)

    if resid is None:
        ins, in_specs = [x, g, m6, m6], [row, vec, mcol(sh_idx), mcol(sc_idx)]
        out_specs, out_shape = row, jax.ShapeDtypeStruct((T, D), BF16)
    else:
        f, ga_idx = resid
        ins = [x, f, m6, g, m6, m6]
        in_specs = [row, row, mcol(ga_idx), vec, mcol(sh_idx), mcol(sc_idx)]
        out_specs = [row, row]
        out_shape = [jax.ShapeDtypeStruct((T, D), F32), jax.ShapeDtypeStruct((T, D), BF16)]
    return pl.pallas_call(body, name=name, grid=(T // tr,), in_specs=in_specs, out_specs=out_specs,
                          out_shape=out_shape, compiler_params=_cparams(("parallel",)))(*ins)


def _resid(x, f, m6, ga_idx, n_ctx_tiles, tr, name):
    T, D = x.shape
    row = pl.BlockSpec((tr, D), lambda i: (i, 0))

    def body(x_ref, f_ref, ga_ref, o_ref):
        is_ctx = pl.program_id(0) < n_ctx_tiles
        o_ref[...] = x_ref[...] + _seg_pick(ga_ref, is_ctx) * f_ref[...]

    return pl.pallas_call(body, name=name, grid=(T // tr,),
                          in_specs=[row, row, pl.BlockSpec((8, D), lambda i: (0, ga_idx))], out_specs=row,
                          out_shape=jax.ShapeDtypeStruct((T, D), F32),
                          compiler_params=_cparams(("parallel",)))(x, f, m6)


def _resid_bwd(dx, f, m6, ga_idx, n_ctx_tiles, tr, name):
    T, D = dx.shape
    row = pl.BlockSpec((tr, D), lambda i: (i, 0))
    acc = pl.BlockSpec((8, D), lambda i: (0, 0))

    def body(dx_ref, f_ref, ga_ref, df_ref, dga_ref):
        i = pl.program_id(0)
        is_ctx = i < n_ctx_tiles

        @pl.when(i == 0)
        def _():
            dga_ref[...] = jnp.zeros_like(dga_ref)

        dxv = dx_ref[...]
        df_ref[...] = (_seg_pick(ga_ref, is_ctx) * dxv).astype(BF16)
        s = jnp.sum(dxv * f_ref[...], axis=0, keepdims=True)
        r = lax.broadcasted_iota(jnp.int32, (8, D), 0)
        dga_ref[...] += jnp.where(r == jnp.where(is_ctx, 1, 0), s, 0.0)

    return pl.pallas_call(body, name=name, grid=(T // tr,),
                          in_specs=[row, row, pl.BlockSpec((8, D), lambda i: (0, ga_idx))],
                          out_specs=[row, acc],
                          out_shape=[jax.ShapeDtypeStruct((T, D), BF16), jax.ShapeDtypeStruct((8, D), F32)],
                          compiler_params=_cparams(("arbitrary",)))(dx, f, m6)


def _norm_mod_bwd(x, dh, dxres, g, m6, sc_idx, n_ctx_tiles, tr, name):
    T, D = x.shape
    row = pl.BlockSpec((tr, D), lambda i: (i, 0))
    acc = pl.BlockSpec((8, D), lambda i: (0, 0))

    def body(x_ref, dh_ref, dr_ref, g_ref, sc_ref, dx_ref, st_ref):
        i = pl.program_id(0)
        is_ctx = i < n_ctx_tiles

        @pl.when(i == 0)
        def _():
            st_ref[...] = jnp.zeros_like(st_ref)

        xv, dh_v, gv = x_ref[...], dh_ref[...], g_ref[...]
        sc1 = 1.0 + _seg_pick(sc_ref, is_ctx)
        rstd = lax.rsqrt(jnp.mean(xv * xv, axis=-1, keepdims=True) + EPS)
        xhat = xv * rstd
        dxhat = dh_v * sc1 * gv
        dx_ref[...] = dr_ref[...] + rstd * (dxhat - xhat * jnp.mean(dxhat * xhat, axis=-1, keepdims=True))
        dsh = jnp.sum(dh_v, axis=0, keepdims=True)
        dsc = jnp.sum(dh_v * xhat * gv, axis=0, keepdims=True)
        dg = jnp.sum(dh_v * sc1 * xhat, axis=0, keepdims=True)
        r = lax.broadcasted_iota(jnp.int32, (8, D), 0)
        seg = jnp.where(is_ctx, 1, 0)
        st_ref[...] += (jnp.where(r == seg, dsh, 0.0) + jnp.where(r == 2 + seg, dsc, 0.0)
                        + jnp.where(r == 4, dg, 0.0))

    return pl.pallas_call(body, name=name, grid=(T // tr,),
                          in_specs=[row, row, row, pl.BlockSpec((1, D), lambda i: (0, 0)),
                                    pl.BlockSpec((8, D), lambda i: (0, sc_idx))],
                          out_specs=[row, acc],
                          out_shape=[jax.ShapeDtypeStruct((T, D), F32), jax.ShapeDtypeStruct((8, D), F32)],
                          compiler_params=_cparams(("arbitrary",)))(x, dh, dxres, g, m6)


def _loss_head(x, tgt, g, n_ctx_tiles, tr, name):
    T, D = x.shape
    row = pl.BlockSpec((tr, D), lambda i: (i, 0))

    def body(x_ref, t_ref, g_ref, l_ref, dx_ref, dg_ref):
        i = pl.program_id(0)

        @pl.when(i == 0)
        def _():
            l_ref[...] = jnp.zeros_like(l_ref)
            dg_ref[...] = jnp.zeros_like(dg_ref)

        @pl.when(i < n_ctx_tiles)
        def _():
            dx_ref[...] = jnp.zeros_like(dx_ref)

        @pl.when(i >= n_ctx_tiles)
        def _():
            xv, gv = x_ref[...], g_ref[...]
            rstd = lax.rsqrt(jnp.mean(xv * xv, axis=-1, keepdims=True) + EPS)
            xhat = xv * rstd
            e = xhat * gv - t_ref[...]
            l_ref[...] += 0.5 * jnp.sum(jnp.mean(e * e, axis=-1, keepdims=True), axis=0, keepdims=True)
            dy = e * (1.0 / D)
            dxhat = dy * gv
            dx_ref[...] = rstd * (dxhat - xhat * jnp.mean(dxhat * xhat, axis=-1, keepdims=True))
            r = lax.broadcasted_iota(jnp.int32, (8, D), 0)
            dg_ref[...] += jnp.where(r == 0, jnp.sum(dy * xhat, axis=0, keepdims=True), 0.0)

    return pl.pallas_call(
        body, name=name, grid=(T // tr,),
        in_specs=[row, pl.BlockSpec((tr, D), lambda i: (jnp.maximum(i - n_ctx_tiles, 0), 0)),
                  pl.BlockSpec((1, D), lambda i: (0, 0))],
        out_specs=[pl.BlockSpec((8, 128), lambda i: (0, 0)), row, pl.BlockSpec((8, D), lambda i: (0, 0))],
        out_shape=[jax.ShapeDtypeStruct((8, 128), F32), jax.ShapeDtypeStruct((T, D), F32),
                   jax.ShapeDtypeStruct((8, D), F32)],
        compiler_params=_cparams(("arbitrary",)))(x, tgt, g)


def _seq_masks(T, n_ctx, width):
    row = lax.broadcasted_iota(jnp.int32, (T, width), 0)
    in_ctx = row < n_ctx
    return jnp.where(in_ctx, row, row - n_ctx), jnp.where(in_ctx, n_ctx, T - n_ctx)


def _shift_rows(u, off, t_loc, seg_len):
    T = u.shape[0]
    if off == 0:
        return u
    v = pltpu.roll(u, (-off) % T, 0)
    ok = (t_loc + off >= 0) & (t_loc + off < seg_len)
    return jnp.where(ok, v, 0.0)


def _conv_fwd(proj, col0_blk, ncol, conv_w8, conv_b, n_ctx, tc, name):
    T = proj.shape[0]

    def body(u_ref, w_ref, b_ref, o_ref):
        u = u_ref[...]
        t_loc, seg_len = _seq_masks(T, n_ctx, tc)
        acc = jnp.broadcast_to(b_ref[...], u.shape)
        for i in range(CONV_K):
            acc = acc + w_ref[i:i + 1, :] * _shift_rows(u, i - CONV_K // 2, t_loc, seg_len)
        o_ref[...] = _silu(acc)

    return pl.pallas_call(
        body, name=name, grid=(ncol // tc,),
        in_specs=[pl.BlockSpec((T, tc), lambda j: (0, col0_blk + j)), pl.BlockSpec((8, tc), lambda j: (0, j)),
                  pl.BlockSpec((1, tc), lambda j: (0, j))],
        out_specs=pl.BlockSpec((T, tc), lambda j: (0, j)),
        out_shape=jax.ShapeDtypeStruct((T, ncol), F32),
        compiler_params=_cparams(("parallel",)))(proj, conv_w8, conv_b)


def _conv_bwd(proj, col0_blk, d2, w_blk0, conv_w8, conv_b, n_ctx, tc, name, skip=None):
    T, ncol = d2.shape[1], d2.shape[2]

    def body(u_ref, d_ref, w_ref, b_ref, *rest):
        if skip is None:
            du_ref, dw_ref = rest
        else:
            dy_ref, k_ref, du_ref, dw_ref = rest
        u = u_ref[...]
        t_loc, seg_len = _seq_masks(T, n_ctx, tc)
        pre = jnp.broadcast_to(b_ref[...], u.shape)
        for i in range(CONV_K):
            pre = pre + w_ref[i:i + 1, :] * _shift_rows(u, i - CONV_K // 2, t_loc, seg_len)
        r = lax.broadcasted_iota(jnp.int32, (8, tc), 0)
        dact = d_ref[0] + d_ref[1]
        dw = jnp.zeros((8, tc), F32)
        if skip is not None:
            dyv = dy_ref[...]
            dact = dact + (k_ref[0:1, :] + k_ref[1:2, :]) * dyv
            dw = jnp.where(r == CONV_K + 1, jnp.sum(dyv * _silu(pre), axis=0, keepdims=True), 0.0)
        dpre = dact * _dsilu(pre)
        du = jnp.zeros_like(u)
        dw = dw + jnp.where(r == CONV_K, jnp.sum(dpre, axis=0, keepdims=True), 0.0)
        for i in range(CONV_K):
            off = i - CONV_K // 2
            du = du + w_ref[i:i + 1, :] * _shift_rows(dpre, -off, t_loc, seg_len)
            dw = dw + jnp.where(r == i, jnp.sum(dpre * _shift_rows(u, off, t_loc, seg_len), axis=0, keepdims=True),
                                0.0)
        du_ref[...] = du.astype(BF16)
        dw_ref[...] = dw

    col = pl.BlockSpec((T, tc), lambda j: (0, j))
    in_specs = [pl.BlockSpec((T, tc), lambda j: (0, col0_blk + j)), pl.BlockSpec((2, T, tc), lambda j: (0, 0, j)),
                pl.BlockSpec((8, tc), lambda j: (0, w_blk0 + j)), pl.BlockSpec((1, tc), lambda j: (0, w_blk0 + j))]
    operands = [proj, d2, conv_w8, conv_b]
    if skip is not None:
        in_specs += [col, pl.BlockSpec((8, tc), lambda j: (0, j))]
        operands += list(skip)
    return pl.pallas_call(
        body, name=name, grid=(ncol // tc,), in_specs=in_specs,
        out_specs=[col, pl.BlockSpec((8, tc), lambda j: (0, j))],
        out_shape=[jax.ShapeDtypeStruct((T, ncol), BF16), jax.ShapeDtypeStruct((8, ncol), F32)],
        compiler_params=_cparams(("parallel",)))(*operands)


def _pool_core(u, half, t_loc, seg_len, transpose):
    tr = u.shape[0]

    def shift(v, s):
        w = pltpu.roll(v, s % tr, 0)
        ok = (t_loc - s >= 0) & (t_loc - s < seg_len)
        return jnp.where(ok, w, 0.0)

    cnt = (jnp.minimum(t_loc, half) + jnp.minimum(seg_len - t_loc, half)).astype(F32)
    q = u / cnt if transpose else u
    back, ahead, h = q, q, 1
    while h < half:
        back = back + shift(back, h)
        ahead = ahead + shift(ahead, -h)
        h *= 2
    if transpose:
        tot = back + shift(ahead, -1)
        return tot - u
    tot = shift(back, 1) + ahead
    return tot / cnt - u


def _pool_apply(src, col0_blk, out_dtype, n_ctx, pg, transpose, name):
    T = src.shape[0]

    def body(u_ref, o_ref):
        gi = pl.program_id(0)
        row = lax.broadcasted_iota(jnp.int32, (T, pg), 0)
        seg_len = jnp.where(row < n_ctx, n_ctx, GRID_W)
        t_loc = row & (seg_len - 1)
        u = u_ref[...].astype(F32)
        for k_idx, k in enumerate(POOL_WINDOWS):
            @pl.when(gi == k_idx)
            def _(k=k):
                o_ref[...] = _pool_core(u, k // 2, t_loc, seg_len, transpose).astype(o_ref.dtype)

    return pl.pallas_call(
        body, name=name, grid=(len(POOL_WINDOWS),),
        in_specs=[pl.BlockSpec((T, pg), lambda gi: (0, col0_blk + gi))],
        out_specs=pl.BlockSpec((T, pg), lambda gi: (0, gi)),
        out_shape=jax.ShapeDtypeStruct((T, pg * len(POOL_WINDOWS)), out_dtype),
        compiler_params=_cparams(("parallel",)))(src)


def _pool_mix_fwd(pm, pool_w, pool_scale, tr, name):
    T, W = pm.shape
    ng, pg = pool_w.shape[0], pool_w.shape[1]

    def body(p_ref, w_ref, s_ref, o_ref):
        o_ref[...] = (jnp.dot(p_ref[...], w_ref[...], preferred_element_type=F32) * s_ref[...]).astype(BF16)

    return pl.pallas_call(
        body, name=name, grid=(T // tr, ng),
        in_specs=[pl.BlockSpec((tr, pg), lambda i, g: (i, g)), pl.BlockSpec((None, pg, pg), lambda i, g: (g, 0, 0)),
                  pl.BlockSpec((1, pg), lambda i, g: (0, g))],
        out_specs=pl.BlockSpec((tr, pg), lambda i, g: (i, g)),
        out_shape=jax.ShapeDtypeStruct((T, W), BF16),
        compiler_params=_cparams(("parallel", "parallel")))(pm, pool_w, pool_scale)


def _pool_mix_bwd(pm, dpms, pool_w, pool_scale, tr, name):
    T, W = pm.shape
    ng, pg = pool_w.shape[0], pool_w.shape[1]

    def body(p_ref, d_ref, w_ref, s_ref, dp_ref, dw_ref, ds_ref):
        i = pl.program_id(1)

        @pl.when(i == 0)
        def _():
            dw_ref[...] = jnp.zeros_like(dw_ref)
            ds_ref[...] = jnp.zeros_like(ds_ref)

        p, w = p_ref[...], w_ref[...]
        d = d_ref[...].astype(F32)
        pmix = jnp.dot(p, w, preferred_element_type=F32)
        r = lax.broadcasted_iota(jnp.int32, (8, pg), 0)
        ds_ref[...] += jnp.where(r == 0, jnp.sum(d * pmix, axis=0, keepdims=True), 0.0)
        dmix = (d * s_ref[...]).astype(BF16)
        dp_ref[...] = lax.dot_general(dmix, w, (((1,), (1,)), ((), ())), preferred_element_type=F32)
        dw_ref[...] += lax.dot_general(p, dmix, (((0,), (0,)), ((), ())), preferred_element_type=F32)

    return pl.pallas_call(
        body, name=name, grid=(ng, T // tr),
        in_specs=[pl.BlockSpec((tr, pg), lambda g, i: (i, g)), pl.BlockSpec((tr, pg), lambda g, i: (i, g)),
                  pl.BlockSpec((None, pg, pg), lambda g, i: (g, 0, 0)), pl.BlockSpec((1, pg), lambda g, i: (0, g))],
        out_specs=[pl.BlockSpec((tr, pg), lambda g, i: (i, g)), pl.BlockSpec((None, pg, pg), lambda g, i: (g, 0, 0)),
                   pl.BlockSpec((8, pg), lambda g, i: (0, g))],
        out_shape=[jax.ShapeDtypeStruct((T, W), F32), jax.ShapeDtypeStruct((ng, pg, pg), F32),
                   jax.ShapeDtypeStruct((8, W), F32)],
        compiler_params=_cparams(("parallel", "arbitrary")))(pm, dpms, pool_w, pool_scale)


def _chunk_cumsum(v, upper):
    Q = v.shape[0]
    ii = lax.broadcasted_iota(jnp.int32, (Q, Q), 0)
    jj = lax.broadcasted_iota(jnp.int32, (Q, Q), 1)
    tri = ((jj >= ii) if upper else (jj <= ii)).astype(BF16)
    h1 = v.astype(BF16)
    r1 = v - h1.astype(F32)
    h2 = r1.astype(BF16)
    h3 = (r1 - h2.astype(F32)).astype(BF16)
    return (jnp.dot(tri, h1, preferred_element_type=F32) + jnp.dot(tri, h2, preferred_element_type=F32)
            + jnp.dot(tri, h3, preferred_element_type=F32))


def _split3(v):
    h1 = v.astype(BF16)
    r1 = v - h1.astype(F32)
    h2 = r1.astype(BF16)
    return h1, h2, (r1 - h2.astype(F32)).astype(BF16)


def _dt_prep(proj, dt_blk, bias, a_log, expand, n_heads, name):
    T = proj.shape[0]
    Wd = expand.shape[1]
    Q = CHUNK
    row = pl.BlockSpec((Q, 128), lambda i: (i, 0))
    wide = pl.BlockSpec((Q, Wd), lambda i: (i, 0))

    def body(r_ref, b_ref, al_ref, e_ref, l1_ref, l2_ref, l3_ref, dtb_ref, ein_ref, dte_ref, etot_ref):
        xv = r_ref[...] + b_ref[...]
        dt = jnp.maximum(xv, 0.0) + jnp.log(1.0 + jnp.exp(-jnp.abs(xv)))
        a = -jnp.exp(al_ref[...]) * dt
        fwd_col = lax.broadcasted_iota(jnp.int32, (Q, 128), 1) < n_heads
        lam = jnp.where(fwd_col, _chunk_cumsum(a, False), _chunk_cumsum(a, True))
        tot = jnp.where(fwd_col[0:1], lam[Q - 1:Q], lam[0:1])
        l1_ref[...], l2_ref[...], l3_ref[...] = _split3(lam)
        etot_ref[...] = jnp.broadcast_to(jnp.exp(tot), (8, 128))
        ex = e_ref[...]

        def rep(v):
            p1, p2, p3 = _split3(v)
            return (jnp.dot(p1, ex, preferred_element_type=F32) + jnp.dot(p2, ex, preferred_element_type=F32)
                    + jnp.dot(p3, ex, preferred_element_type=F32))

        dtb_ref[...] = rep(dt)
        ein_ref[...] = rep(jnp.exp(lam))
        dte_ref[...] = rep(jnp.exp(tot - lam))

    vec = pl.BlockSpec((1, 128), lambda i: (0, 0))
    return pl.pallas_call(
        body, name=name, grid=(T // Q,),
        in_specs=[pl.BlockSpec((Q, 128), lambda i: (i, dt_blk)), vec, vec, pl.BlockSpec((128, Wd), lambda i: (0, 0))],
        out_specs=[row, row, row, wide, wide, wide, pl.BlockSpec((8, 128), lambda i: (i, 0))],
        out_shape=[jax.ShapeDtypeStruct((T, 128), BF16)] * 3 + [jax.ShapeDtypeStruct((T, Wd), F32)] * 3
        + [jax.ShapeDtypeStruct((T // Q * 8, 128), F32)],
        compiler_params=_cparams(("parallel",)))(proj, bias, a_log, expand)


def _dt_bwd(proj, dt_blk, bias, a_log, ddt, dlam, n_heads, tr, name):
    T = proj.shape[0]
    row = pl.BlockSpec((tr, 128), lambda i: (i, 0))
    vec = pl.BlockSpec((1, 128), lambda i: (0, 0))

    def body(r_ref, b_ref, al_ref, ddt_ref, dl_ref, o_ref, st_ref):
        @pl.when(pl.program_id(0) == 0)
        def _():
            st_ref[...] = jnp.zeros_like(st_ref)

        xv = r_ref[...] + b_ref[...]
        dt = jnp.maximum(xv, 0.0) + jnp.log(1.0 + jnp.exp(-jnp.abs(xv)))
        a_neg = -jnp.exp(al_ref[...])
        col = lax.broadcasted_iota(jnp.int32, (CHUNK, 128), 1)
        dl = dl_ref[...]
        parts = []
        for k in range(tr // CHUNK):
            dk = dl[k * CHUNK:(k + 1) * CHUNK]
            parts.append(jnp.where(col < n_heads, _chunk_cumsum(dk, True), _chunk_cumsum(dk, False)))
        dav = jnp.concatenate(parts, axis=0)
        draw = (ddt_ref[...] + dav * a_neg) * jax.nn.sigmoid(xv)
        o_ref[...] = draw.astype(BF16)
        r = lax.broadcasted_iota(jnp.int32, (8, 128), 0)
        st_ref[...] += (jnp.where(r == 0, jnp.sum(draw, axis=0, keepdims=True), 0.0)
                        + jnp.where(r == 1, jnp.sum(dav * dt, axis=0, keepdims=True) * a_neg, 0.0))

    return pl.pallas_call(
        body, name=name, grid=(T // tr,),
        in_specs=[pl.BlockSpec((tr, 128), lambda i: (i, dt_blk)), vec, vec, row, row],
        out_specs=[row, pl.BlockSpec((8, 128), lambda i: (0, 0))],
        out_shape=[jax.ShapeDtypeStruct((T, 128), BF16), jax.ShapeDtypeStruct((8, 128), F32)],
        compiler_params=_cparams(("arbitrary",)))(proj, bias, a_log, ddt, dlam)


def _scan_chunk(d, pos, nc_ctx, nc):
    rev = jnp.where(pos < nc_ctx, nc_ctx - 1 - pos, nc - 1 - (pos - nc_ctx))
    return jnp.where(d == 0, pos, rev)


def _chunk_mask(d):
    ii = lax.broadcasted_iota(jnp.int32, (CHUNK, CHUNK), 0)
    jj = lax.broadcasted_iota(jnp.int32, (CHUNK, CHUNK), 1)
    return (ii - jj) * jnp.where(d == 0, 1, -1) >= 0


def _ssd_specs(T, G, n_ctx, gpb, chunk_of):
    R, P, N, Q = HPG, HEADDIM, STATE, CHUNK
    H = G * R
    nc, nc_ctx = T // Q, n_ctx // Q
    xw, bw = gpb * R * P, gpb * N
    b_blk0 = (H * P) // bw
    c_blk0 = b_blk0 + G // gpb

    def ch(d, s):
        return chunk_of(d, s, nc_ctx, nc)

    return dict(
        x=pl.BlockSpec((Q, xw), lambda d, g, s: (ch(d, s), g)),
        b=pl.BlockSpec((Q, bw), lambda d, g, s: (ch(d, s), b_blk0 + g)),
        c=pl.BlockSpec((Q, bw), lambda d, g, s: (ch(d, s), c_blk0 + g)),
        col=pl.BlockSpec((None, gpb, Q, R), lambda d, g, s: (d, g, ch(d, s), 0)),
        row=pl.BlockSpec((None, gpb, R, Q), lambda d, g, s: (d, g, 0, ch(d, s))),
        rep=pl.BlockSpec((Q, xw), lambda d, g, s: (ch(d, s), d * (G // gpb) + g)),
        lam_a=pl.BlockSpec((None, gpb * R, 16, Q), lambda d, g, s: (d, g, 0, ch(d, s))),
        lam_b=pl.BlockSpec((None, gpb * R, 16, Q), lambda d, g, s: (d, g, 0, ch(d, s))),
        etot=pl.BlockSpec((None, gpb, None, 8, 128), lambda d, g, s: (d, g, ch(d, s), 0, 0)),
        dsk=pl.BlockSpec((None, 8, xw), lambda d, g, s: (d, 0, g)),
        xd=pl.BlockSpec((None, Q, xw), lambda d, g, s: (d, ch(d, s), g)),
        bd=pl.BlockSpec((None, Q, bw), lambda d, g, s: (d, ch(d, s), g)),
        st=pl.BlockSpec((None, None, gpb * R // 2, 2 * P, N), lambda d, g, s: (d, ch(d, s), g, 0, 0)),
    )


class _Exchange:
    def __init__(self, operand, out_sds, sems, start, finish):
        self.operand, self.out_sds, self.sems, self.start, self.finish = operand, out_sds, sems, start, finish


def _call_with_exchange(body, exch, *, name, grid, in_specs, out_specs, out_shape, scratch_shapes, operands):
    if exch is None:
        return pl.pallas_call(body, name=name, grid=grid, in_specs=in_specs, out_specs=out_specs,
                              out_shape=out_shape, scratch_shapes=scratch_shapes,
                              compiler_params=_cparams(("arbitrary",) * len(grid)))(*operands)
    n_in, n_out, n_scr = len(in_specs), len(out_specs), len(scratch_shapes)

    def fused(*refs):
        ins, c_in = refs[:n_in], refs[n_in]
        outs, c_out = refs[n_in + 1:n_in + 1 + n_out], refs[n_in + 1 + n_out]
        scr = refs[n_in + 2 + n_out:n_in + 2 + n_out + n_scr]
        sems = refs[n_in + 2 + n_out + n_scr:]
        ids = [pl.program_id(a) for a in range(len(grid))]
        first = functools.reduce(lambda p, q: p & q, [i == 0 for i in ids])
        last = functools.reduce(lambda p, q: p & q, [i == n - 1 for i, n in zip(ids, grid)])

        @pl.when(first)
        def _():
            exch.start(c_in, c_out, *sems)

        body(*ins, *outs, *scr)

        @pl.when(last)
        def _():
            exch.finish(c_in, c_out, *sems)

    return pl.pallas_call(fused, name=name, grid=grid, in_specs=list(in_specs) + [ANY],
                          out_specs=list(out_specs) + [ANY], out_shape=list(out_shape) + [exch.out_sds],
                          scratch_shapes=list(scratch_shapes) + list(exch.sems),
                          compiler_params=_cparams(("arbitrary",) * len(grid)))(*operands, exch.operand)


def _ssd_fwd(xbc, dtb, ein, dte, lam_a, lam_b, etot, dsk, G, n_ctx, gpb, name, exch=None):
    T = xbc.shape[0]
    R, P, N, Q = HPG, HEADDIM, STATE, CHUNK
    H = G * R
    nc = T // Q
    sp = _ssd_specs(T, G, n_ctx, gpb, _scan_chunk)

    def body(x_ref, b_ref, c_ref, dt_ref, ein_ref, dte_ref, la_ref, lb_ref, et_ref, dsk_ref, y_ref, st_ref, S):
        d, s = pl.program_id(0), pl.program_id(2)

        @pl.when(s == 0)
        def _():
            S[...] = jnp.zeros_like(S)

        mask = _chunk_mask(d)
        head0 = lax.broadcasted_iota(jnp.int32, (Q, 2 * P), 1) < P
        rows0 = lax.broadcasted_iota(jnp.int32, (2 * P, N), 0) < P
        for gg in range(gpb):
            Bm = b_ref[:, gg * N:(gg + 1) * N].astype(BF16)
            Cm = c_ref[:, gg * N:(gg + 1) * N].astype(BF16)
            Gm = lax.dot_general(Cm, Bm, (((1,), (1,)), ((), ())), preferred_element_type=F32)
            for k in range(R // 2):
                pk = gg * (R // 2) + k
                sl = slice(pk * 2 * P, (pk + 1) * 2 * P)
                xp = x_ref[:, sl]
                xc = xp * dt_ref[:, sl]
                s_in = S[pk]
                y = lax.dot_general(Cm, s_in.astype(BF16), (((1,), (1,)), ((), ())),
                                    preferred_element_type=F32) * ein_ref[:, sl] + dsk_ref[0:1, sl] * xp
                for j in range(2):
                    hr = 2 * pk + j
                    diff = lax.dot_general(la_ref[hr], lb_ref[hr], (((0,), (0,)), ((), ())), preferred_element_type=F32)
                    ldec = jnp.exp(jnp.where(mask, diff, NEG_BIG))
                    xc_j = (jnp.where(head0, xc, 0.0) if j == 0 else jnp.where(head0, 0.0, xc)).astype(BF16)
                    y = y + jnp.dot((Gm * ldec).astype(BF16), xc_j, preferred_element_type=F32)
                y_ref[:, sl] = y
                st_ref[pk] = s_in
                e_all = jnp.where(rows0, et_ref[gg, 2 * k:2 * k + 1, :], et_ref[gg, 2 * k + 1:2 * k + 2, :])
                xd = (xc * dte_ref[:, sl]).astype(BF16)
                S[pk] = e_all * s_in + lax.dot_general(xd, Bm, (((0,), (0,)), ((), ())),
                                                       preferred_element_type=F32)

    return _call_with_exchange(
        body, exch, name=name, grid=(2, G // gpb, nc),
        in_specs=[sp["x"], sp["b"], sp["c"], sp["rep"], sp["rep"], sp["rep"], sp["lam_a"], sp["lam_b"], sp["etot"],
                  sp["dsk"]],
        out_specs=[sp["xd"], sp["st"]],
        out_shape=[jax.ShapeDtypeStruct((2, T, H * P), F32), jax.ShapeDtypeStruct((2, nc, H // 2, 2 * P, N), F32)],
        scratch_shapes=[pltpu.VMEM((gpb * R // 2, 2 * P, N), F32)],
        operands=(xbc, xbc, xbc, dtb, ein, dte, lam_a, lam_b, etot, dsk))


def _ssd_bwd(xbc, dy, states, dtb, ein, dte, lam_a, lam_b, etot, G, n_ctx, gpb, name, exch=None):
    T = xbc.shape[0]
    R, P, N, Q = HPG, HEADDIM, STATE, CHUNK
    H = G * R
    nc = T // Q
    sp = _ssd_specs(T, G, n_ctx, gpb, lambda d, s, nc_ctx, n: _scan_chunk(d, n - 1 - s, nc_ctx, n))

    def body(x_ref, b_ref, c_ref, dy_ref, st_ref, dt_ref, ein_ref, dte_ref, la_ref, lb_ref, et_ref,
             dx_ref, db_ref, dc_ref, ddt_ref, dlc_ref, dlr_ref, dS):
        d, s = pl.program_id(0), pl.program_id(2)

        @pl.when(s == 0)
        def _():
            dS[...] = jnp.zeros_like(dS)

        mask = _chunk_mask(d)
        ri = lax.broadcasted_iota(jnp.int32, (Q, 1), 0)
        is_last = ri == jnp.where(d == 0, Q - 1, 0)
        head0 = lax.broadcasted_iota(jnp.int32, (Q, 2 * P), 1) < P
        rows0 = lax.broadcasted_iota(jnp.int32, (2 * P, N), 0) < P

        def total(v):
            return jnp.sum(jnp.sum(v, axis=1, keepdims=True), axis=0, keepdims=True)

        for gg in range(gpb):
            Bm = b_ref[:, gg * N:(gg + 1) * N].astype(BF16)
            Cm = c_ref[:, gg * N:(gg + 1) * N].astype(BF16)
            Gm = lax.dot_general(Cm, Bm, (((1,), (1,)), ((), ())), preferred_element_type=F32)
            dG = jnp.zeros((Q, Q), F32)
            dB = jnp.zeros((Q, N), F32)
            dC = jnp.zeros((Q, N), F32)
            for k in range(R // 2):
                pk = gg * (R // 2) + k
                sl = slice(pk * 2 * P, (pk + 1) * 2 * P)
                e_in = ein_ref[:, sl]
                dte = dte_ref[:, sl]
                e_all = jnp.where(rows0, et_ref[gg, 2 * k:2 * k + 1, :], et_ref[gg, 2 * k + 1:2 * k + 2, :])
                xp = x_ref[:, sl]
                dtp = dt_ref[:, sl]
                xc = xp * dtp
                xc_b = xc.astype(BF16)
                dyp = dy_ref[:, sl]
                s_in = st_ref[pk]
                s_in_b = s_in.astype(BF16)
                ds_out = dS[pk]
                ds_out_b = ds_out.astype(BF16)
                y_int = lax.dot_general(Cm, s_in_b, (((1,), (1,)), ((), ())), preferred_element_type=F32) * e_in
                b_ds = lax.dot_general(Bm, ds_out_b, (((1,), (1,)), ((), ())), preferred_element_type=F32)
                dxc = dte * b_ds
                u = xc * dxc
                v = dyp * y_int - u
                sse = ds_out * s_in * e_all
                for j in range(2):
                    hr, r = 2 * pk + j, 2 * k + j

                    def pick(a, m0=head0, j=j):
                        return jnp.where(m0, a, 0.0) if j == 0 else jnp.where(m0, 0.0, a)

                    diff = lax.dot_general(la_ref[hr], lb_ref[hr], (((0,), (0,)), ((), ())), preferred_element_type=F32)
                    ldec = jnp.exp(jnp.where(mask, diff, NEG_BIG))
                    dy_j = pick(dyp).astype(BF16)
                    dM = lax.dot_general(dy_j, xc_b, (((1,), (1,)), ((), ())), preferred_element_type=F32)
                    dMl = dM * ldec
                    Wm = dMl * Gm
                    dlam_c = jnp.sum(Wm, axis=1, keepdims=True) + jnp.sum(pick(v), axis=1, keepdims=True)
                    last = total(pick(sse, rows0)) + total(pick(u))
                    dlc_ref[gg, :, r:r + 1] = dlam_c + jnp.where(is_last, last, 0.0)
                    dlr_ref[gg, r:r + 1, :] = -jnp.sum(Wm, axis=0, keepdims=True)
                    dxc = dxc + lax.dot_general((Gm * ldec).astype(BF16), dy_j, (((0,), (0,)), ((), ())),
                                                preferred_element_type=F32)
                    dG = dG + dMl
                dx_ref[:, sl] = dxc * dtp
                t = dxc * xp
                ddt_ref[gg, :, 2 * k:2 * k + 1] = jnp.sum(jnp.where(head0, t, 0.0), axis=1, keepdims=True)
                ddt_ref[gg, :, 2 * k + 1:2 * k + 2] = jnp.sum(jnp.where(head0, 0.0, t), axis=1, keepdims=True)
                edy_b = (e_in * dyp).astype(BF16)
                dC = dC + jnp.dot(edy_b, s_in_b, preferred_element_type=F32)
                dB = dB + jnp.dot((dte * xc).astype(BF16), ds_out_b, preferred_element_type=F32)
                dS[pk] = e_all * ds_out + lax.dot_general(edy_b, Cm, (((0,), (0,)), ((), ())),
                                                          preferred_element_type=F32)
            dG_b = dG.astype(BF16)
            dc_ref[:, gg * N:(gg + 1) * N] = dC + jnp.dot(dG_b, Bm, preferred_element_type=F32)
            db_ref[:, gg * N:(gg + 1) * N] = dB + lax.dot_general(dG_b, Cm, (((0,), (0,)), ((), ())),
                                                                  preferred_element_type=F32)

    return _call_with_exchange(
        body, exch, name=name, grid=(2, G // gpb, nc),
        in_specs=[sp["x"], sp["b"], sp["c"], sp["x"], sp["st"], sp["rep"], sp["rep"], sp["rep"], sp["lam_a"],
                  sp["lam_b"], sp["etot"]],
        out_specs=[sp["xd"], sp["bd"], sp["bd"], sp["col"], sp["col"], sp["row"]],
        out_shape=[jax.ShapeDtypeStruct((2, T, H * P), F32), jax.ShapeDtypeStruct((2, T, G * N), F32),
                   jax.ShapeDtypeStruct((2, T, G * N), F32), jax.ShapeDtypeStruct((2, G, T, R), F32),
                   jax.ShapeDtypeStruct((2, G, T, R), F32), jax.ShapeDtypeStruct((2, G, R, T), F32)],
        scratch_shapes=[pltpu.VMEM((gpb * R // 2, 2 * P, N), F32)],
        operands=(xbc, xbc, xbc, dy, states, dtb, ein, dte, lam_a, lam_b, etot))


def _gnorm_fwd(y2, proj, w, gs, tr, name):
    T, HP = y2.shape[1], y2.shape[2]

    def body(y_ref, z_ref, w_ref, o_ref):
        yz = (y_ref[0] + y_ref[1]) * _silu(z_ref[...])
        for g in range(HP // gs):
            v = yz[:, g * gs:(g + 1) * gs]
            rstd = lax.rsqrt(jnp.mean(v * v, axis=-1, keepdims=True) + EPS)
            o_ref[:, g * gs:(g + 1) * gs] = (v * rstd * w_ref[:, g * gs:(g + 1) * gs]).astype(BF16)

    return pl.pallas_call(
        body, name=name, grid=(T // tr,),
        in_specs=[pl.BlockSpec((2, tr, HP), lambda i: (0, i, 0)), pl.BlockSpec((tr, HP), lambda i: (i, 0)),
                  pl.BlockSpec((1, HP), lambda i: (0, 0))],
        out_specs=pl.BlockSpec((tr, HP), lambda i: (i, 0)),
        out_shape=jax.ShapeDtypeStruct((T, HP), BF16),
        compiler_params=_cparams(("parallel",)))(y2, proj, w)


def _gnorm_bwd(y2, proj, w, dyn, gs, tr, name):
    T, HP = y2.shape[1], y2.shape[2]
    row = pl.BlockSpec((tr, HP), lambda i: (i, 0))

    def body(y_ref, z_ref, w_ref, d_ref, dy_ref, dz_ref, dw_ref):
        @pl.when(pl.program_id(0) == 0)
        def _():
            dw_ref[...] = jnp.zeros_like(dw_ref)

        yv = y_ref[0] + y_ref[1]
        zv = z_ref[...]
        sz = _silu(zv)
        yz = yv * sz
        dv = d_ref[...]
        r = lax.broadcasted_iota(jnp.int32, (8, gs), 0)
        for g in range(HP // gs):
            sl = slice(g * gs, (g + 1) * gs)
            v = yz[:, sl]
            rstd = lax.rsqrt(jnp.mean(v * v, axis=-1, keepdims=True) + EPS)
            xhat = v * rstd
            dyn_g = dv[:, sl]
            dhat = dyn_g * w_ref[:, sl]
            dyz = rstd * (dhat - xhat * jnp.mean(dhat * xhat, axis=-1, keepdims=True))
            dy_ref[:, sl] = dyz * sz[:, sl]
            dz_ref[:, sl] = (dyz * yv[:, sl] * _dsilu(zv[:, sl])).astype(BF16)
            dw_ref[:, sl] += jnp.where(r == 0, jnp.sum(dyn_g * xhat, axis=0, keepdims=True), 0.0)

    return pl.pallas_call(
        body, name=name, grid=(T // tr,),
        in_specs=[pl.BlockSpec((2, tr, HP), lambda i: (0, i, 0)), row, pl.BlockSpec((1, HP), lambda i: (0, 0)), row],
        out_specs=[row, row, pl.BlockSpec((8, HP), lambda i: (0, 0))],
        out_shape=[jax.ShapeDtypeStruct((T, HP), F32), jax.ShapeDtypeStruct((T, HP), BF16),
                   jax.ShapeDtypeStruct((8, HP), F32)],
        compiler_params=_cparams(("arbitrary",)))(y2, proj, w, dyn)


def _merge_fwd(proj, g1_blk, o_ssd, o_pool, tr, name):
    T, D = o_ssd.shape
    row = pl.BlockSpec((tr, D), lambda i: (i, 0))

    def body(g1_ref, g2_ref, a_ref, b_ref, o_ref):
        o_ref[...] = (jax.nn.sigmoid(g1_ref[...]) * a_ref[...]
                      + jax.nn.sigmoid(g2_ref[...]) * b_ref[...]).astype(BF16)

    return pl.pallas_call(
        body, name=name, grid=(T // tr,),
        in_specs=[pl.BlockSpec((tr, D), lambda i: (i, g1_blk)), pl.BlockSpec((tr, D), lambda i: (i, g1_blk + 1)),
                  row, row],
        out_specs=row, out_shape=jax.ShapeDtypeStruct((T, D), BF16),
        compiler_params=_cparams(("parallel",)))(proj, proj, o_ssd, o_pool)


def _merge_bwd(proj, g1_blk, o_ssd, o_pool, dmg, tr, name):
    T, D = o_ssd.shape
    row = pl.BlockSpec((tr, D), lambda i: (i, 0))

    def body(g1_ref, g2_ref, a_ref, b_ref, d_ref, da_ref, db_ref, dg_ref):
        s1, s2 = jax.nn.sigmoid(g1_ref[...]), jax.nn.sigmoid(g2_ref[...])
        dv = d_ref[...]
        da_ref[...] = (s1 * dv).astype(BF16)
        db_ref[...] = (s2 * dv).astype(BF16)
        dg_ref[:, :D] = (dv * a_ref[...] * s1 * (1.0 - s1)).astype(BF16)
        dg_ref[:, D:] = (dv * b_ref[...] * s2 * (1.0 - s2)).astype(BF16)

    return pl.pallas_call(
        body, name=name, grid=(T // tr,),
        in_specs=[pl.BlockSpec((tr, D), lambda i: (i, g1_blk)), pl.BlockSpec((tr, D), lambda i: (i, g1_blk + 1)),
                  row, row, row],
        out_specs=[row, row, pl.BlockSpec((tr, 2 * D), lambda i: (i, 0))],
        out_shape=[jax.ShapeDtypeStruct((T, D), BF16), jax.ShapeDtypeStruct((T, D), BF16),
                   jax.ShapeDtypeStruct((T, 2 * D), BF16)],
        compiler_params=_cparams(("parallel",)))(proj, proj, o_ssd, o_pool, dmg)


def _swiglu_fwd(gu, tr, name):
    T, F2 = gu.shape
    F = F2 // 2
    tc = _tile(F, (1408, 768, 512, 256, 128))
    nb = F // tc

    def body(a_ref, b_ref, o_ref):
        o_ref[...] = (_silu(a_ref[...]) * b_ref[...]).astype(BF16)

    return pl.pallas_call(
        body, name=name, grid=(T // tr, nb),
        in_specs=[pl.BlockSpec((tr, tc), lambda i, j: (i, j)), pl.BlockSpec((tr, tc), lambda i, j: (i, nb + j))],
        out_specs=pl.BlockSpec((tr, tc), lambda i, j: (i, j)),
        out_shape=jax.ShapeDtypeStruct((T, F), BF16),
        compiler_params=_cparams(("parallel", "parallel")))(gu, gu)


def _swiglu_bwd(gu, dact, tr, name):
    T, F2 = gu.shape
    F = F2 // 2
    tc = _tile(F, (1408, 768, 512, 256, 128))
    nb = F // tc

    def body(a_ref, b_ref, d_ref, o_ref):
        is_a = pl.program_id(1) < nb
        av, bv, dv = a_ref[...], b_ref[...], d_ref[...]
        o_ref[...] = jnp.where(is_a, dv * bv * _dsilu(av), dv * _silu(av)).astype(BF16)

    return pl.pallas_call(
        body, name=name, grid=(T // tr, 2 * nb),
        in_specs=[pl.BlockSpec((tr, tc), lambda i, j: (i, j % nb)), pl.BlockSpec((tr, tc), lambda i, j: (i, nb + j % nb)),
                  pl.BlockSpec((tr, tc), lambda i, j: (i, j % nb))],
        out_specs=pl.BlockSpec((tr, tc), lambda i, j: (i, j)),
        out_shape=jax.ShapeDtypeStruct((T, F2), BF16),
        compiler_params=_cparams(("parallel", "parallel")))(gu, gu, dact)


def _adamw(w, g, m, v, name):
    Rr, C = w.shape
    tr = _tile(Rr, (256, 128, 64, 32, 16, 8))
    row = pl.BlockSpec((tr, C), lambda i: (i, 0))

    def body(w_ref, g_ref, m_ref, v_ref, d_ref, mo_ref, vo_ref):
        gv = g_ref[...]
        mn = ADAM_B1 * m_ref[...] + (1.0 - ADAM_B1) * gv
        vn = ADAM_B2 * v_ref[...] + (1.0 - ADAM_B2) * (gv * gv)
        m_hat = mn / (1.0 - ADAM_B1 ** ADAM_STEP)
        v_hat = vn / (1.0 - ADAM_B2 ** ADAM_STEP)
        d_ref[...] = -ADAM_LR * (m_hat / (jnp.sqrt(v_hat) + ADAM_EPS) + ADAM_WD * w_ref[...])
        mo_ref[...] = mn
        vo_ref[...] = vn

    sds = jax.ShapeDtypeStruct((Rr, C), F32)
    return pl.pallas_call(body, name=name, grid=(Rr // tr,), in_specs=[row] * 4, out_specs=[row] * 3,
                          out_shape=[sds] * 3, compiler_params=_cparams(("parallel",)))(w, g, m, v)


def _sum_slots(x, name):
    n, Rr, C = x.shape
    tr = _tile(Rr, (512, 256, 128, 64, 32, 16, 8))

    def body(x_ref, o_ref):
        acc = x_ref[0].astype(F32)
        for k in range(1, n):
            acc = acc + x_ref[k].astype(F32)
        o_ref[...] = acc

    return pl.pallas_call(body, name=name, grid=(Rr // tr,),
                          in_specs=[pl.BlockSpec((n, tr, C), lambda i: (0, i, 0))],
                          out_specs=pl.BlockSpec((tr, C), lambda i: (i, 0)),
                          out_shape=jax.ShapeDtypeStruct((Rr, C), F32),
                          compiler_params=_cparams(("parallel",)))(x)


def _place():
    return lax.axis_index("x"), lax.axis_index("y"), lax.axis_index("c")


def _all_gather(x, name):
    Rr, C = x.shape

    def body(x_ref, out_ref, send_sems, recv_sems, local_sem):
        _gather_start(x_ref, out_ref, send_sems, recv_sems, local_sem)
        _gather_finish(x_ref, out_ref, send_sems, recv_sems, local_sem)

    return pl.pallas_call(
        body, name=name, in_specs=[ANY], out_specs=ANY,
        out_shape=jax.ShapeDtypeStruct((N_DEV, Rr, C), x.dtype), scratch_shapes=_GATHER_SEMS,
    )(x)


_GATHER_SEMS = [pltpu.SemaphoreType.DMA((7,)), pltpu.SemaphoreType.DMA((7,)), pltpu.SemaphoreType.DMA]


def _gather_copies(x_ref, out_ref, send_sems, recv_sems, local_sem):
    mx, my, mc = _place()
    me, sibling = (mx, my, mc), (mx, my, 1 - mc)
    chips = [(1 - mx, my), (mx, 1 - my), (1 - mx, 1 - my)]

    def slot(px, py, pc):
        return out_ref.at[4 * px + 2 * py + pc]

    def copy(k, block, to, src=None):
        return pltpu.make_async_remote_copy(
            src_ref=slot(*block) if src is None else src, dst_ref=slot(*block),
            send_sem=send_sems.at[k], recv_sem=recv_sems.at[k],
            device_id=to, device_id_type=pl.DeviceIdType.MESH)

    return dict(
        mine=pltpu.make_async_copy(x_ref, slot(*me), local_sem),
        first=[copy(0, me, sibling, src=x_ref)] + [copy(1 + j, me, (*chip, mc), src=x_ref)
                                                   for j, chip in enumerate(chips)],
        passed=[copy(4 + j, (*chip, mc), sibling) for j, chip in enumerate(chips)],
        from_chips=[copy(1 + j, (*chip, mc), me) for j, chip in enumerate(chips)],
        from_sibling=[copy(0, sibling, me)] + [copy(4 + j, (*chip, 1 - mc), me) for j, chip in enumerate(chips)],
    )


def _gather_start(*refs):
    cps = _gather_copies(*refs)
    cps["mine"].start()
    for cp in cps["first"]:
        cp.start()


def _gather_finish(*refs):
    cps = _gather_copies(*refs)
    for j in range(3):
        cps["from_chips"][j].wait_recv()
        cps["passed"][j].start()
    for cp in cps["from_sibling"]:
        cp.wait_recv()
    for cp in cps["first"] + cps["passed"]:
        cp.wait_send()
    cps["mine"].wait()


def _pair_exchange(buf, name):
    _, n, Rr, C = buf.shape
    parts = 4
    pr = Rr // parts

    def body(b_ref, got_ref, send_sems, recv_sems):
        mx, my, mc = _place()
        copies = []
        for q in range(n):
            for p in range(parts):
                rows = pl.ds(p * pr, pr)
                cp = pltpu.make_async_remote_copy(
                    src_ref=b_ref.at[1 - mc, q, rows], dst_ref=got_ref.at[q, rows],
                    send_sem=send_sems.at[q * parts + p], recv_sem=recv_sems.at[q * parts + p],
                    device_id=(mx, my, 1 - mc), device_id_type=pl.DeviceIdType.MESH)
                cp.start()
                copies.append(cp)
        for cp in copies:
            cp.wait()

    return pl.pallas_call(
        body, name=name, in_specs=[ANY], out_specs=ANY,
        out_shape=jax.ShapeDtypeStruct((n, Rr, C), buf.dtype),
        scratch_shapes=[pltpu.SemaphoreType.DMA((n * parts,)), pltpu.SemaphoreType.DMA((n * parts,))],
    )(buf)


def _pair_add(buf, got, my_c, name):
    _, n, Rr, C = buf.shape
    tr = _tile(Rr, (512, 256, 128, 64, 32, 16))

    def body(c_ref, b_ref, g_ref, o_ref):
        o_ref[...] = (b_ref[...].astype(F32) + g_ref[...].astype(F32)).astype(BF16)

    return pl.pallas_call(
        body, name=name,
        grid_spec=pltpu.PrefetchScalarGridSpec(
            num_scalar_prefetch=1, grid=(n, Rr // tr),
            in_specs=[pl.BlockSpec((None, None, tr, C), lambda q, i, c: (c[0], q, i, 0)),
                      pl.BlockSpec((None, tr, C), lambda q, i, c: (q, i, 0))],
            out_specs=pl.BlockSpec((None, tr, C), lambda q, i, c: (q, i, 0))),
        out_shape=jax.ShapeDtypeStruct((n, Rr, C), BF16),
        compiler_params=_cparams(("parallel", "parallel")))(my_c, buf, got)


def _chip_exchange(red, name):
    def body(r_ref, out_ref, send_sems, recv_sems, local_sem):
        _chip_exchange_start(r_ref, out_ref, send_sems, recv_sems, local_sem)
        _chip_exchange_finish(r_ref, out_ref, send_sems, recv_sems, local_sem)

    return pl.pallas_call(
        body, name=name, in_specs=[ANY], out_specs=ANY,
        out_shape=jax.ShapeDtypeStruct(red.shape, red.dtype), scratch_shapes=_CHIP_SEMS,
    )(red)


_CHIP_SEMS = [pltpu.SemaphoreType.DMA((3,)), pltpu.SemaphoreType.DMA((3,)), pltpu.SemaphoreType.DMA]


def _chip_exchange_copies(r_ref, out_ref, send_sems, recv_sems, local_sem):
    mx, my, mc = _place()
    chips = [(1 - mx, my), (mx, 1 - my), (1 - mx, 1 - my)]

    def copy(k, src_slot, dst_slot, to):
        return pltpu.make_async_remote_copy(
            src_ref=r_ref.at[src_slot], dst_ref=out_ref.at[dst_slot],
            send_sem=send_sems.at[k], recv_sem=recv_sems.at[k],
            device_id=(*to, mc), device_id_type=pl.DeviceIdType.MESH)

    return dict(
        mine=pltpu.make_async_copy(r_ref.at[2 * mx + my], out_ref.at[2 * mx + my], local_sem),
        sends=[copy(k, 2 * px + py, 2 * mx + my, (px, py)) for k, (px, py) in enumerate(chips)],
        recvs=[copy(k, 2 * px + py, 2 * px + py, (px, py)) for k, (px, py) in enumerate(chips)],
    )


def _chip_exchange_start(*refs):
    cps = _chip_exchange_copies(*refs)
    cps["mine"].start()
    for cp in cps["sends"]:
        cp.start()


def _chip_exchange_finish(*refs):
    cps = _chip_exchange_copies(*refs)
    for cp in cps["recvs"]:
        cp.wait_recv()
    for cp in cps["sends"]:
        cp.wait_send()
    cps["mine"].wait()


def _pad_rows(a, rows):
    return jnp.pad(a, ((0, rows - a.shape[0]), (0, 0)))


class _Layout:
    def __init__(self, D, shards):
        self.D = D
        self.pieces = []
        off = 0
        for name, layer, rows in shards:
            pr = _round_up(rows, 16)
            self.pieces.append((name, layer, rows, pr, off))
            off += pr
        self.rows = _round_up(off, 256)

    def find(self, name, layer):
        for p in self.pieces:
            if p[0] == name and p[1] == layer:
                return p
        raise KeyError(name)


def kernel(x, c, ctx, c_ctx, w_ada, b_ada, g_mix, w_in, conv_w, conv_b, dt_bias, a_log, d_skip, ssd_norm_w, w_ssd_out, pool_w, pool_scale, w_pool_out, w_out, g_ffn, w_gate_up, w_down, g_final, loss_target, m_c_ctx, m_w_ada, m_b_ada, m_g_mix, m_w_in, m_conv_w, m_conv_b, m_dt_bias, m_a_log, m_d_skip, m_ssd_norm_w, m_w_ssd_out, m_pool_w, m_pool_scale, m_w_pool_out, m_w_out, m_g_ffn, m_w_gate_up, m_w_down, m_g_final, v_c_ctx, v_w_ada, v_b_ada, v_g_mix, v_w_in, v_conv_w, v_conv_b, v_dt_bias, v_a_log, v_d_skip, v_ssd_norm_w, v_w_ssd_out, v_pool_w, v_pool_scale, v_w_pool_out, v_w_out, v_g_ffn, v_w_gate_up, v_w_down, v_g_final):
    weights = dict(c_ctx=c_ctx, w_ada=w_ada, b_ada=b_ada, g_mix=g_mix, w_in=w_in, conv_w=conv_w, conv_b=conv_b,
                   dt_bias=dt_bias, a_log=a_log, d_skip=d_skip, ssd_norm_w=ssd_norm_w, w_ssd_out=w_ssd_out,
                   pool_w=pool_w, pool_scale=pool_scale, w_pool_out=w_pool_out, w_out=w_out, g_ffn=g_ffn,
                   w_gate_up=w_gate_up, w_down=w_down, g_final=g_final)
    moms_m = dict(c_ctx=m_c_ctx, w_ada=m_w_ada, b_ada=m_b_ada, g_mix=m_g_mix, w_in=m_w_in, conv_w=m_conv_w,
                  conv_b=m_conv_b, dt_bias=m_dt_bias, a_log=m_a_log, d_skip=m_d_skip, ssd_norm_w=m_ssd_norm_w,
                  w_ssd_out=m_w_ssd_out, pool_w=m_pool_w, pool_scale=m_pool_scale, w_pool_out=m_w_pool_out,
                  w_out=m_w_out, g_ffn=m_g_ffn, w_gate_up=m_w_gate_up, w_down=m_w_down, g_final=m_g_final)
    moms_v = dict(c_ctx=v_c_ctx, w_ada=v_w_ada, b_ada=v_b_ada, g_mix=v_g_mix, w_in=v_w_in, conv_w=v_conv_w,
                  conv_b=v_conv_b, dt_bias=v_dt_bias, a_log=v_a_log, d_skip=v_d_skip, ssd_norm_w=v_ssd_norm_w,
                  w_ssd_out=v_w_ssd_out, pool_w=v_pool_w, pool_scale=v_pool_scale, w_pool_out=v_w_pool_out,
                  w_out=v_w_out, g_ffn=v_g_ffn, w_gate_up=v_w_gate_up, w_down=v_w_down, g_final=v_g_final)
    order = ["c_ctx", "w_ada", "b_ada", "g_mix", "w_in", "conv_w", "conv_b", "dt_bias", "a_log", "d_skip",
             "ssd_norm_w", "w_ssd_out", "pool_w", "pool_scale", "w_pool_out", "w_out", "g_ffn", "w_gate_up",
             "w_down", "g_final"]
    big = ["w_ada", "w_in", "conv_w", "w_ssd_out", "pool_w", "w_pool_out", "w_out", "w_gate_up", "w_down"]
    small = [n for n in order if n not in big]

    depth = w_in.shape[0]
    L, D = x.shape[1], x.shape[2]
    n_ctx = ctx.shape[1]
    T = n_ctx + L
    in_cols = w_in.shape[2] * N_DEV
    xbc_w = conv_w.shape[2] * N_DEV
    dinner = ssd_norm_w.shape[1]
    H = dt_bias.shape[2]
    G = H // HPG
    GN = G * STATE
    assert xbc_w == dinner + 2 * GN and dinner == H * HEADDIM
    assert in_cols == dinner + xbc_w + 2 * H + D + 2 * D
    assert dinner == 2 * D and GN == D and 2 * H <= 128
    F = w_down.shape[1] * N_DEV
    pg = pool_w.shape[3]
    tr = n_ctx
    assert L % tr == 0 and tr % GRID_W == 0 and tr % CHUNK == 0 and L % CHUNK == 0
    n_ctx_tiles = 1
    tr_big = _tile(T, (1088, 544, 512, 256, 128))
    tr_mid = _tile(T, (544, 512, 256, 128))
    NP =_round_up(9 * D + 128, 512)
    gs = dinner // G
    off_xbc, off_dt, off_pool = dinner, dinner + xbc_w, dinner + xbc_w + 2 * H
    off_gate = off_pool + D

    def shard_rows(name, l):
        w = weights[name][l]
        if name in ("w_ada", "w_in", "w_gate_up"):
            return w.T
        if name == "conv_w":
            w8 = _pad_rows(w, 8)
            hi = w8.astype(BF16)
            lo = (w8 - hi.astype(F32)).astype(BF16)
            return jnp.concatenate([hi, lo], axis=0).reshape(-1, D)
        if name == "pool_w":
            return w.reshape(-1, D)
        return w

    first_needed = ["w_ada", "w_in"]
    shard = {(n, l): shard_rows(n, l) for l in range(depth) for n in big}
    gather_groups = [[first_needed, [n for n in big if n not in first_needed]] if l == 0 else [big]
                     for l in range(depth)]
    glays = [[_Layout(D, [(n, l, shard[(n, l)].shape[0]) for n in grp]) for grp in gather_groups[l]]
             for l in range(depth)]

    def packed(l, gi):
        lay = glays[l][gi]
        rows = jnp.concatenate([_pad_rows(shard[(n, l)].astype(BF16), lay.find(n, l)[3])
                                for n in gather_groups[l][gi]], axis=0)
        return _pad_rows(rows, lay.rows)

    def gather_exchange(l, gi):
        return _Exchange(packed(l, gi), jax.ShapeDtypeStruct((N_DEV, glays[l][gi].rows, D), BF16), _GATHER_SEMS,
                         _gather_start, _gather_finish)

    gathered = [[None] * len(g) for g in gather_groups]
    gathered[0][0] = _all_gather(packed(0, 0), "gather_weights")

    def full(name, l):
        gi = [name in grp for grp in gather_groups[l]].index(True)
        _, _, rows, _, off = glays[l][gi].find(name, l)
        return gathered[l][gi][:, off:off + rows, :]

    def w_inT_new(l):
        w = full("w_in", l).reshape(in_cols, D)
        parts = [w[:off_xbc], w[off_xbc:off_dt], w[off_pool:off_gate], w[off_gate:], w[off_dt:off_pool]]
        return _pad_rows(jnp.concatenate(parts, axis=0), NP)

    xs0 = jnp.concatenate([ctx[0], x[0]], axis=0)
    cc8 = _pad_rows(jnp.concatenate([c, c_ctx[None, :]], axis=0), 8)
    tgt = loss_target[0]

    def vec(a):
        return a.reshape(1, -1)

    def pad128(a):
        return jnp.pad(a.reshape(1, -1), ((0, 0), (0, 128 - 2 * H)))

    expand = (jnp.arange(128)[:, None] == jnp.arange(2 * H * HEADDIM)[None, :] // HEADDIM).astype(BF16)

    def from4(arr):
        return jnp.pad(arr.transpose(2, 0, 1, 3).reshape(T, 2 * H), ((0, 0), (0, 128 - 2 * H)))

    dt_blk = (9 * D) // 128
    conv_tc = 128
    ssd_gpb, ssd_gpb_bwd = 4, 2
    saved = []
    xcur = xs0
    for l in range(depth):
        W = dict(adaT=full("w_ada", l).reshape(6 * D, D), inT=w_inT_new(l))
        m6 = _ada_fwd(cc8, W["adaT"], vec(b_ada[l]), "ada_fwd")
        h = _norm_mod(xcur, vec(g_mix[l]), m6, 0, 1, n_ctx_tiles, tr, "norm_mod")
        if len(gather_groups[l]) > 1:
            proj, gathered[l][1] = _mm(h, W["inT"], "nt", F32, "mm_in_gather", exch=gather_exchange(l, 1))
        else:
            proj = _mm(h, W["inT"], "nt", F32, "mm_in")
        W.update(
            ssd=full("w_ssd_out", l).reshape(dinner, D), po=full("w_pool_out", l).reshape(D, D),
            out=full("w_out", l).reshape(D, D), guT=full("w_gate_up", l).reshape(2 * F, D),
            down=full("w_down", l).reshape(F, D),
            pool=full("pool_w", l).reshape(N_DEV, len(POOL_WINDOWS), pg // N_DEV, pg).transpose(1, 0, 2, 3)
            .reshape(len(POOL_WINDOWS), pg, pg),
        )
        cw = full("conv_w", l).reshape(N_DEV, 16, xbc_w // N_DEV).astype(F32)
        W["conv8"] = (cw[:, :8] + cw[:, 8:]).transpose(1, 0, 2).reshape(8, xbc_w)
        l1, l2, l3, dtb, ein, dte, etot = _dt_prep(proj, dt_blk, pad128(dt_bias[l]), pad128(a_log[l]), expand, H,
                                                   "dt_prep")
        L1, L2, L3 = (v[:, :2 * H].T.reshape(2, H, 1, T) for v in (l1, l2, l3))
        k16 = jnp.arange(16).reshape(1, 1, 16, 1)

        def rows16(at):
            terms = jnp.where(k16 == at, L1, jnp.where(k16 == at + 1, L2, L3))
            return jnp.where((k16 >= at) & (k16 < at + 3), terms, (k16 < 6).astype(BF16))

        lam_a, lam_b = rows16(0), -rows16(3) + 2 * (k16 < 3).astype(BF16)
        et = etot.reshape(T // CHUNK, 8, 128)[:, 0, :2 * H].reshape(T // CHUNK, 2, G, HPG).transpose(1, 2, 0, 3)
        etot5 = jnp.pad(jnp.broadcast_to(et[..., None], et.shape + (128,)),
                        ((0, 0), (0, 0), (0, 0), (0, 8 - HPG), (0, 0)))
        dsk = jnp.pad(jnp.repeat(d_skip[l], HEADDIM, axis=1)[:, None, :], ((0, 0), (0, 7), (0, 0)))
        scan_ops = (dtb, ein, dte, lam_a, lam_b, etot5)
        xbc = _conv_fwd(proj, dinner // conv_tc, xbc_w, W["conv8"], vec(conv_b[l]), n_ctx, conv_tc, "conv_fwd")
        if l + 1 < depth:
            y2, states, gathered[l + 1][0] = _ssd_fwd(xbc, *scan_ops, dsk, G, n_ctx, ssd_gpb, "ssd_fwd_gather",
                                                      gather_exchange(l + 1, 0))
        else:
            y2, states = _ssd_fwd(xbc, *scan_ops, dsk, G, n_ctx, ssd_gpb, "ssd_fwd")
        yn = _gnorm_fwd(y2, proj, vec(ssd_norm_w[l]), gs, tr_mid, "gnorm_fwd")
        pm = _pool_apply(proj, (6 * D) // pg, BF16, n_ctx, pg, False, "pool_fwd")
        pms = _pool_mix_fwd(pm, W["pool"], vec(pool_scale[l]), tr_big, "pool_mix_fwd")
        o_ssd = _mm(yn, W["ssd"], "nn", F32, "mm_ssd_out")
        o_pool = _mm(pms, W["po"], "nn", F32, "mm_pool_out")
        mg = _merge_fwd(proj, 7, o_ssd, o_pool, tr_mid, "merge_fwd")
        mo = _mm(mg, W["out"], "nn", F32, "mm_out")
        x1, h2 = _norm_mod(xcur, vec(g_ffn[l]), m6, 3, 4, n_ctx_tiles, tr, "resid_norm_mod", resid=(mo, 2))
        gu = _mm(h2, W["guT"], "nt", F32, "mm_gate_up")
        act = _swiglu_fwd(gu, tr_mid, "swiglu_fwd")
        f = _mm(act, W["down"], "nn", F32, "mm_down")
        saved.append(dict(W=W, m6=m6, x0=xcur, h=h, proj=proj, scan_ops=scan_ops, xbc=xbc, y2=y2,
                          states=states, yn=yn, pm=pm, pms=pms, o_ssd=o_ssd, o_pool=o_pool, mg=mg, mo=mo, x1=x1,
                          h2=h2, gu=gu, act=act, f=f))
        xcur = _resid(x1, f, m6, 5, n_ctx_tiles, tr, "resid")

    loss_blk, dx, dgf = _loss_head(xcur, tgt, vec(g_final), n_ctx_tiles, tr, "loss_head")
    loss = lax.psum(loss_blk[0, 0], MESH_AXES)

    big_rows = {}
    small_g = {n: [None] * depth for n in small}
    d_c_ctx = jnp.zeros((D,), F32)
    late = ["w_ada", "w_in", "conv_w"]
    rs_groups = [[[n for n in big if n not in late], late] if l == 0 else [big] for l in range(depth)]
    rlays = [[_Layout(D, [(n, l, shard[(n, l)].shape[0]) for n in grp]) for grp in rs_groups[l]]
             for l in range(depth)]
    my_c = lax.axis_index("c").astype(jnp.int32).reshape(1)

    def reduce_pair(l, gi):
        lay = rlays[l][gi]
        gparts = [jnp.pad(big_rows[(n, l)].astype(BF16), ((0, 0), (0, pr - rows), (0, 0)))
                  for n, _, rows, pr, _ in lay.pieces]
        gparts.append(jnp.zeros((N_DEV, lay.rows - sum(p[3] for p in lay.pieces), D), BF16))
        gbuf = jnp.concatenate(gparts, axis=1)
        gbuf = gbuf.reshape(2, 2, 2, lay.rows, D).transpose(2, 0, 1, 3, 4).reshape(2, 4, lay.rows, D)
        return _pair_add(gbuf, _pair_exchange(gbuf, "rs_pair_exchange"), my_c, "rs_pair_add")

    slots = {}
    pending = None
    for l in reversed(range(depth)):
        S = saved[l]
        W, m6, proj = S["W"], S["m6"], S["proj"]
        df, dga2 = _resid_bwd(dx, S["f"], m6, 5, n_ctx_tiles, tr, "resid_bwd")
        dact = _mm(df, W["down"], "nt", F32, "mm_down_dx")
        g_down = _mm(S["act"], df, "tn", BF16,"mm_down_dw")
        dgu = _swiglu_bwd(S["gu"], dact, tr_mid, "swiglu_bwd")
        dh2 = _mm(dgu, W["guT"], "nn", F32, "mm_gate_up_dx")
        g_guT = _mm(dgu, S["h2"], "tn", BF16,"mm_gate_up_dw")
        dx1, st2 = _norm_mod_bwd(S["x1"], dh2, dx, vec(g_ffn[l]), m6, 4, n_ctx_tiles, tr, "norm_mod_bwd")
        dmo, dga1 = _resid_bwd(dx1, S["mo"], m6, 2, n_ctx_tiles, tr, "resid_bwd")
        dmg = _mm(dmo, W["out"], "nt", F32, "mm_out_dx")
        g_out = _mm(S["mg"], dmo, "tn", BF16,"mm_out_dw")
        do_ssd, do_pool, dgl = _merge_bwd(proj, 7, S["o_ssd"], S["o_pool"], dmg, tr_mid, "merge_bwd")
        dyn = _mm(do_ssd, W["ssd"], "nt", F32, "mm_ssd_out_dx")
        g_ssd = _mm(S["yn"], do_ssd, "tn", BF16,"mm_ssd_out_dw")
        dpms = _mm(do_pool, W["po"], "nt", F32, "mm_pool_out_dx")
        g_po = _mm(S["pms"], do_pool, "tn", BF16,"mm_pool_out_dw")
        dpm, g_pool, dps = _pool_mix_bwd(S["pm"], dpms, W["pool"], vec(pool_scale[l]), tr_big, "pool_mix_bwd")
        dup = _pool_apply(dpm, 0, BF16, n_ctx, pg, True, "pool_bwd")
        dy, dz, dnw = _gnorm_bwd(S["y2"], proj, vec(ssd_norm_w[l]), dyn, gs, tr, "gnorm_bwd")
        big_rows[("w_ssd_out", l)] = g_ssd.reshape(N_DEV, -1, D)
        big_rows[("pool_w", l)] = g_pool.reshape(len(POOL_WINDOWS), N_DEV, pg // N_DEV, pg).transpose(1, 0, 2, 3) \
            .reshape(N_DEV, -1, D)
        big_rows[("w_pool_out", l)] = g_po.reshape(N_DEV, -1, D)
        big_rows[("w_out", l)] = g_out.reshape(N_DEV, -1, D)
        big_rows[("w_gate_up", l)] = g_guT.reshape(N_DEV, -1, D)
        big_rows[("w_down", l)] = g_down.reshape(N_DEV, -1, D)
        riding = [] if pending is None else [pending]
        riding += [((l, gi), reduce_pair(l, gi)) for gi in range(len(rs_groups[l]) - 1)]
        if not riding:
            dxs2, db2, dc2, ddt4, dlc4, dlr4 = _ssd_bwd(S["xbc"], dy, S["states"], *S["scan_ops"], G, n_ctx,
                                                        ssd_gpb_bwd, "ssd_bwd")
        else:
            red = jnp.concatenate([r for _, r in riding], axis=1)
            exch = _Exchange(red, jax.ShapeDtypeStruct(red.shape, BF16), _CHIP_SEMS, _chip_exchange_start,
                             _chip_exchange_finish)
            dxs2, db2, dc2, ddt4, dlc4, dlr4, got = _ssd_bwd(
                S["xbc"], dy, S["states"], *S["scan_ops"], G, n_ctx, ssd_gpb_bwd, "ssd_bwd_exchange", exch)
            at = 0
            for key, r in riding:
                slots[key] = got[:, at:at + r.shape[1]]
                at += r.shape[1]
        dskv = _pad_rows(jnp.repeat(d_skip[l], HEADDIM, axis=1), 8)
        cb = vec(conv_b[l])
        dxbc_x, dconv_x = _conv_bwd(proj, dinner // conv_tc, dxs2, 0, W["conv8"], cb, n_ctx, conv_tc, "conv_bwd_x",
                                    skip=(dy, dskv))
        dxbc_b, dconv_b = _conv_bwd(proj, 2 * dinner // conv_tc, db2, dinner // conv_tc, W["conv8"], cb, n_ctx,
                                    conv_tc, "conv_bwd_bc")
        dxbc_c, dconv_c = _conv_bwd(proj, (2 * dinner + GN) // conv_tc, dc2, (dinner + GN) // conv_tc, W["conv8"], cb,
                                    n_ctx, conv_tc, "conv_bwd_bc")
        dconv = jnp.concatenate([dconv_x, dconv_b, dconv_c], axis=1)
        ddt_raw, dtst = _dt_bwd(proj, dt_blk, pad128(dt_bias[l]), pad128(a_log[l]), from4(ddt4),
                                from4(dlc4 + dlr4.transpose(0, 1, 3, 2)), H, tr, "dt_bwd")
        dproj = jnp.concatenate([dz, dxbc_x, dxbc_b, dxbc_c, dup, dgl, ddt_raw,
                                 jnp.zeros((T, NP - 9 * D - 128), BF16)], axis=1)
        dh = _mm(dproj, W["inT"], "nn", F32, "mm_in_dx")
        g_inT_new = _mm(dproj, S["h"], "tn", BF16,"mm_in_dw")
        dx0, st1 = _norm_mod_bwd(S["x0"], dh, dx1, vec(g_mix[l]), m6, 1, n_ctx_tiles, tr, "norm_mod_bwd")
        dm6 = _pad_rows(jnp.concatenate([st1[0:2], st1[2:4], dga1[0:2], st2[0:2], st2[2:4], dga2[0:2]], axis=1), 8)
        dsil = _mm(dm6, W["adaT"], "nn", F32, "mm_ada_dx")
        sil_b, dcc, dbada = _ada_bwd_small(cc8, dsil, dm6, "ada_bwd_small")
        g_adaT = _mm(dm6, sil_b, "tn", BF16,"mm_ada_dw")
        d_c_ctx = d_c_ctx + dcc[1]
        dx = dx0

        g_inT = jnp.concatenate([g_inT_new[:6 * D], g_inT_new[9 * D:9 * D + 2 * H], g_inT_new[6 * D:9 * D]], axis=0)
        big_rows[("w_ada", l)] = g_adaT.reshape(N_DEV, -1, D)
        big_rows[("w_in", l)] = g_inT.reshape(N_DEV, -1, D)
        big_rows[("conv_w", l)] = jnp.pad(
            dconv[:CONV_K].reshape(CONV_K, N_DEV, xbc_w // N_DEV).transpose(1, 0, 2),
            ((0, 0), (0, 16 - CONV_K), (0, 0))).reshape(N_DEV, -1, D)
        small_g["b_ada"][l] = dbada[0]
        small_g["g_mix"][l] = st1[4]
        small_g["conv_b"][l] = dconv[CONV_K]
        small_g["dt_bias"][l] = dtst[0, :2 * H].reshape(2, H)
        small_g["a_log"][l] = dtst[1, :2 * H].reshape(2, H)
        dsk_h = dconv_x[CONV_K + 1].reshape(H, HEADDIM).sum(axis=-1)
        small_g["d_skip"][l] = jnp.stack([dsk_h, dsk_h])
        small_g["ssd_norm_w"][l] = dnw[0]
        small_g["pool_scale"][l] = dps[0]
        small_g["g_ffn"][l] = st2[4]

        last_gi = len(rs_groups[l]) - 1
        pending = ((l, last_gi), reduce_pair(l, last_gi))
    slots[pending[0]] = _chip_exchange(pending[1], "rs_chip_exchange")
    g_local = {key: _sum_slots(s, "rs_chip_add") for key, s in slots.items()}
    grad_x = dx[n_ctx:][None]

    def local_grad(name):
        outs = []
        for l in range(depth):
            gi = [name in grp for grp in rs_groups[l]].index(True)
            _, _, rows, _, off = rlays[l][gi].find(name, l)
            piece = g_local[(l, gi)][off:off + rows]
            if name in ("w_ada", "w_in", "w_gate_up"):
                piece = piece.T
            elif name == "conv_w":
                piece = piece.reshape(16, -1)[:CONV_K]
            elif name == "pool_w":
                piece = piece.reshape(weights[name].shape[1:])
            outs.append(piece)
        return jnp.stack(outs)

    grads = {n: local_grad(n) for n in big}

    small_full = {"c_ctx": d_c_ctx, "g_final": dgf[0]}
    for n in small:
        if n not in small_full:
            small_full[n] = jnp.stack(small_g[n])

    def pack_small(tree):
        flat = jnp.concatenate([tree[n].reshape(-1).astype(F32) for n in small])
        rows = _round_up(-(-flat.shape[0] // D), 8)
        return jnp.pad(flat, (0, rows * D - flat.shape[0])).reshape(rows, D)

    def unpack_small(buf):
        flat, out, off = buf.reshape(-1), {}, 0
        for n in small:
            sz = weights[n].size
            out[n] = flat[off:off + sz].reshape(weights[n].shape)
            off += sz
        return out

    g_small = _sum_slots(_all_gather(pack_small(small_full), "gather_small_grads"), "sum_small_grads")
    grads.update(unpack_small(g_small))

    delta, new_m, new_v = {}, {}, {}
    for n in big:
        shp = weights[n].shape
        d_, m_, v_ = _adamw(weights[n].reshape(-1, shp[-1]), grads[n].reshape(-1, shp[-1]),
                            moms_m[n].reshape(-1, shp[-1]), moms_v[n].reshape(-1, shp[-1]), "adamw_" + n)
        delta[n], new_m[n], new_v[n] = d_.reshape(shp), m_.reshape(shp), v_.reshape(shp)
    d_, m_, v_ = _adamw(pack_small(weights), g_small, pack_small(moms_m), pack_small(moms_v), "adamw_small")
    delta.update(unpack_small(d_))
    new_m.update(unpack_small(m_))
    new_v.update(unpack_small(v_))

    return (loss, grad_x, *[grads[n] for n in order], *[delta[n] for n in order],
            *[new_m[n] for n in order], *[new_v[n] for n in order])
```

```python
import functools

import jax
import jax.numpy as jnp
from jax import lax
from jax.experimental import pallas as pl
from jax.experimental.pallas import tpu as pltpu

F32 = jnp.float32
BF16 = jnp.bfloat16
N_DEV = 8
EPS = 1e-6
GRID_W = 64
POOL_WINDOWS = (2, 4, 8, 16)
HEADDIM = 64
STATE = 128
CHUNK = 128
HPG = 4
CONV_K = 5
ADAM_LR, ADAM_B1, ADAM_B2, ADAM_EPS, ADAM_WD, ADAM_STEP = 0.001, 0.9, 0.999, 1e-08, 0.01, 10
NEG_BIG = -1e30
MESH_AXES = ("x", "y", "c")
ANY = pl.BlockSpec(memory_space=pl.ANY)


def _tile(n, cands):
    for t in cands:
        if n % t == 0:
            return t
    return n


def _round_up(n, m):
    return -(-n // m) * m


def _silu(x):
    return x * jax.nn.sigmoid(x)


def _dsilu(x):
    s = jax.nn.sigmoid(x)
    return s * (1.0 + x * (1.0 - s))


def _cparams(sem):
    return pltpu.CompilerParams(dimension_semantics=sem, vmem_limit_bytes=56 * 1024 * 1024)


def _mm(a, b, mode, out_dtype, name, exch=None):
    if mode == "tn":
        K, M = a.shape
        N = b.shape[1]
        tm = _tile(M, (512, 256, 128))
        tn = _tile(N, (1024, 512, 256, 128))
        tk = K if K <= 4608 else _tile(K, (1088, 544, 512, 256, 128))
        a_spec = pl.BlockSpec((tk, tm), lambda i, j, k: (k, i))
        b_spec = pl.BlockSpec((tk, tn), lambda i, j, k: (k, j))
        dims = (((0,), (0,)), ((), ()))
    else:
        M, K = a.shape
        N = b.shape[0] if mode == "nt" else b.shape[1]
        tm = _tile(M, (1088, 544, 512, 256, 128))
        tk = K if K <= 4096 else max(t for t in range(128, 2817, 128) if K % t == 0)
        tn = _tile(N, (512, 256, 128)) if tk == K else _tile(N, (1024, 512, 256, 128))
        a_spec = pl.BlockSpec((tm, tk), lambda i, j, k: (i, k))
        if mode == "nt":
            b_spec = pl.BlockSpec((tn, tk), lambda i, j, k: (j, k))
            dims = (((1,), (1,)), ((), ()))
        else:
            b_spec = pl.BlockSpec((tk, tn), lambda i, j, k: (k, j))
            dims = (((1,), (0,)), ((), ()))
    nk = K // tk

    def body(a_ref, b_ref, o_ref, *acc):
        if nk == 1:
            o_ref[...] = lax.dot_general(a_ref[...].astype(BF16), b_ref[...].astype(BF16), dims,
                                         preferred_element_type=F32).astype(o_ref.dtype)
            return
        k = pl.program_id(2)

        @pl.when(k == 0)
        def _():
            acc[0][...] = jnp.zeros_like(acc[0])

        acc[0][...] += lax.dot_general(a_ref[...].astype(BF16), b_ref[...].astype(BF16), dims,
                                       preferred_element_type=F32)

        @pl.when(k == nk - 1)
        def _():
            o_ref[...] = acc[0][...].astype(o_ref.dtype)

    call = dict(name=name, grid=(M // tm, N // tn, nk), in_specs=[a_spec, b_spec],
                scratch_shapes=[pltpu.VMEM((tm, tn), F32)] if nk > 1 else [])
    o_spec, o_sds = pl.BlockSpec((tm, tn), lambda i, j, k: (i, j)), jax.ShapeDtypeStruct((M, N), out_dtype)
    if exch is not None:
        return _call_with_exchange(body, exch, out_specs=[o_spec], out_shape=[o_sds], operands=(a, b), **call)
    return pl.pallas_call(body, out_specs=o_spec, out_shape=o_sds,
                          compiler_params=_cparams(("parallel", "parallel", "arbitrary")), **call)(a, b)


def _ada_fwd(cc8, w_adaT, b_ada, name):
    D = cc8.shape[1]
    N = w_adaT.shape[0]
    tn = _tile(N, (512, 256, 128))

    def body(c_ref, w_ref, b_ref, o_ref):
        a = _silu(c_ref[...]).astype(BF16)
        o_ref[...] = lax.dot_general(a, w_ref[...], (((1,), (1,)), ((), ())),
                                     preferred_element_type=F32) + b_ref[...]

    return pl.pallas_call(
        body, name=name, grid=(N // tn,),
        in_specs=[pl.BlockSpec((8, D), lambda j: (0, 0)), pl.BlockSpec((tn, D), lambda j: (j, 0)),
                  pl.BlockSpec((1, tn), lambda j: (0, j))],
        out_specs=pl.BlockSpec((8, tn), lambda j: (0, j)),
        out_shape=jax.ShapeDtypeStruct((8, N), F32),
        compiler_params=_cparams(("parallel",)),
    )(cc8, w_adaT, b_ada)


def _ada_bwd_small(cc8, dsil, dm6, name):
    D = cc8.shape[1]
    N = dm6.shape[1]

    def body(c_ref, ds_ref, dm_ref, sil_ref, dc_ref, db_ref):
        c = c_ref[...]
        sil_ref[...] = _silu(c).astype(BF16)
        dc_ref[...] = ds_ref[...] * _dsilu(c)
        dm = dm_ref[...]
        row = lax.broadcasted_iota(jnp.int32, dm.shape, 0)
        db_ref[...] = jnp.where(row == 0, jnp.sum(dm, axis=0, keepdims=True), 0.0)

    return pl.pallas_call(
        body, name=name, grid=(1,),
        in_specs=[pl.BlockSpec((8, D), lambda i: (0, 0)), pl.BlockSpec((8, D), lambda i: (0, 0)),
                  pl.BlockSpec((8, N), lambda i: (0, 0))],
        out_specs=[pl.BlockSpec((8, D), lambda i: (0, 0)), pl.BlockSpec((8, D), lambda i: (0, 0)),
                   pl.BlockSpec((8, N), lambda i: (0, 0))],
        out_shape=[jax.ShapeDtypeStruct((8, D), BF16), jax.ShapeDtypeStruct((8, D), F32),
                   jax.ShapeDtypeStruct((8, N), F32)],
        compiler_params=_cparams(("arbitrary",)),
    )(cc8, dsil, dm6)


def _seg_pick(m_ref, is_ctx):
    return jnp.where(is_ctx, m_ref[1:2, :], m_ref[0:1, :])


def _norm_mod(x, g, m6, sh_idx, sc_idx, n_ctx_tiles, tr, name, resid=None):
    T, D = x.shape
    row = pl.BlockSpec((tr, D), lambda i: (i, 0))
    vec = pl.BlockSpec((1, D), lambda i: (0, 0))

    def mcol(idx):
        return pl.BlockSpec((8, D), lambda i: (0, idx))

    def body(*refs):
        if resid is None:
            x_ref, g_ref, sh_ref, sc_ref, h_ref = refs
            xv = x_ref[...]
        else:
            x_ref, f_ref, ga_ref, g_ref, sh_ref, sc_ref, xo_ref, h_ref = refs
        is_ctx = pl.program_id(0) < n_ctx_tiles
        if resid is not None:
            xv = x_ref[...] + _seg_pick(ga_ref, is_ctx) * f_ref[...]
            xo_ref[...] = xv
        rstd = lax.rsqrt(jnp.mean(xv * xv, axis=-1, keepdims=True) + EPS)
        hn = xv * rstd * g_ref[...]
        h_ref[...] = (hn * (1.0 + _seg_pick(sc_ref, is_ctx)) + _seg_pick(sh_ref, is_ctx)).astype(BF16)

    if resid is None:
        ins, in_specs = [x, g, m6, m6], [row, vec, mcol(sh_idx), mcol(sc_idx)]
        out_specs, out_shape = row, jax.ShapeDtypeStruct((T, D), BF16)
    else:
        f, ga_idx = resid
        ins = [x, f, m6, g, m6, m6]
        in_specs = [row, row, mcol(ga_idx), vec, mcol(sh_idx), mcol(sc_idx)]
        out_specs = [row, row]
        out_shape = [jax.ShapeDtypeStruct((T, D), F32), jax.ShapeDtypeStruct((T, D), BF16)]
    return pl.pallas_call(body, name=name, grid=(T // tr,), in_specs=in_specs, out_specs=out_specs,
                          out_shape=out_shape, compiler_params=_cparams(("parallel",)))(*ins)


def _resid(x, f, m6, ga_idx, n_ctx_tiles, tr, name):
    T, D = x.shape
    row = pl.BlockSpec((tr, D), lambda i: (i, 0))

    def body(x_ref, f_ref, ga_ref, o_ref):
        is_ctx = pl.program_id(0) < n_ctx_tiles
        o_ref[...] = x_ref[...] + _seg_pick(ga_ref, is_ctx) * f_ref[...]

    return pl.pallas_call(body, name=name, grid=(T // tr,),
                          in_specs=[row, row, pl.BlockSpec((8, D), lambda i: (0, ga_idx))], out_specs=row,
                          out_shape=jax.ShapeDtypeStruct((T, D), F32),
                          compiler_params=_cparams(("parallel",)))(x, f, m6)


def _resid_bwd(dx, f, m6, ga_idx, n_ctx_tiles, tr, name):
    T, D = dx.shape
    row = pl.BlockSpec((tr, D), lambda i: (i, 0))
    acc = pl.BlockSpec((8, D), lambda i: (0, 0))

    def body(dx_ref, f_ref, ga_ref, df_ref, dga_ref):
        i = pl.program_id(0)
        is_ctx = i < n_ctx_tiles

        @pl.when(i == 0)
        def _():
            dga_ref[...] = jnp.zeros_like(dga_ref)

        dxv = dx_ref[...]
        df_ref[...] = (_seg_pick(ga_ref, is_ctx) * dxv).astype(BF16)
        s = jnp.sum(dxv * f_ref[...], axis=0, keepdims=True)
        r = lax.broadcasted_iota(jnp.int32, (8, D), 0)
        dga_ref[...] += jnp.where(r == jnp.where(is_ctx, 1, 0), s, 0.0)

    return pl.pallas_call(body, name=name, grid=(T // tr,),
                          in_specs=[row, row, pl.BlockSpec((8, D), lambda i: (0, ga_idx))],
                          out_specs=[row, acc],
                          out_shape=[jax.ShapeDtypeStruct((T, D), BF16), jax.ShapeDtypeStruct((8, D), F32)],
                          compiler_params=_cparams(("arbitrary",)))(dx, f, m6)


def _norm_mod_bwd(x, dh, dxres, g, m6, sc_idx, n_ctx_tiles, tr, name):
    T, D = x.shape
    row = pl.BlockSpec((tr, D), lambda i: (i, 0))
    acc = pl.BlockSpec((8, D), lambda i: (0, 0))

    def body(x_ref, dh_ref, dr_ref, g_ref, sc_ref, dx_ref, st_ref):
        i = pl.program_id(0)
        is_ctx = i < n_ctx_tiles

        @pl.when(i == 0)
        def _():
            st_ref[...] = jnp.zeros_like(st_ref)

        xv, dh_v, gv = x_ref[...], dh_ref[...], g_ref[...]
        sc1 = 1.0 + _seg_pick(sc_ref, is_ctx)
        rstd = lax.rsqrt(jnp.mean(xv * xv, axis=-1, keepdims=True) + EPS)
        xhat = xv * rstd
        dxhat = dh_v * sc1 * gv
        dx_ref[...] = dr_ref[...] + rstd * (dxhat - xhat * jnp.mean(dxhat * xhat, axis=-1, keepdims=True))
        dsh = jnp.sum(dh_v, axis=0, keepdims=True)
        dsc = jnp.sum(dh_v * xhat * gv, axis=0, keepdims=True)
        dg = jnp.sum(dh_v * sc1 * xhat, axis=0, keepdims=True)
        r = lax.broadcasted_iota(jnp.int32, (8, D), 0)
        seg = jnp.where(is_ctx, 1, 0)
        st_ref[...] += (jnp.where(r == seg, dsh, 0.0) + jnp.where(r == 2 + seg, dsc, 0.0)
                        + jnp.where(r == 4, dg, 0.0))

    return pl.pallas_call(body, name=name, grid=(T // tr,),
                          in_specs=[row, row, row, pl.BlockSpec((1, D), lambda i: (0, 0)),
                                    pl.BlockSpec((8, D), lambda i: (0, sc_idx))],
                          out_specs=[row, acc],
                          out_shape=[jax.ShapeDtypeStruct((T, D), F32), jax.ShapeDtypeStruct((8, D), F32)],
                          compiler_params=_cparams(("arbitrary",)))(x, dh, dxres, g, m6)


def _loss_head(x, tgt, g, n_ctx_tiles, tr, name):
    T, D = x.shape
    row = pl.BlockSpec((tr, D), lambda i: (i, 0))

    def body(x_ref, t_ref, g_ref, l_ref, dx_ref, dg_ref):
        i = pl.program_id(0)

        @pl.when(i == 0)
        def _():
            l_ref[...] = jnp.zeros_like(l_ref)
            dg_ref[...] = jnp.zeros_like(dg_ref)

        @pl.when(i < n_ctx_tiles)
        def _():
            dx_ref[...] = jnp.zeros_like(dx_ref)

        @pl.when(i >= n_ctx_tiles)
        def _():
            xv, gv = x_ref[...], g_ref[...]
            rstd = lax.rsqrt(jnp.mean(xv * xv, axis=-1, keepdims=True) + EPS)
            xhat = xv * rstd
            e = xhat * gv - t_ref[...]
            l_ref[...] += 0.5 * jnp.sum(jnp.mean(e * e, axis=-1, keepdims=True), axis=0, keepdims=True)
            dy = e * (1.0 / D)
            dxhat = dy * gv
            dx_ref[...] = rstd * (dxhat - xhat * jnp.mean(dxhat * xhat, axis=-1, keepdims=True))
            r = lax.broadcasted_iota(jnp.int32, (8, D), 0)
            dg_ref[...] += jnp.where(r == 0, jnp.sum(dy * xhat, axis=0, keepdims=True), 0.0)

    return pl.pallas_call(
        body, name=name, grid=(T // tr,),
        in_specs=[row, pl.BlockSpec((tr, D), lambda i: (jnp.maximum(i - n_ctx_tiles, 0), 0)),
                  pl.BlockSpec((1, D), lambda i: (0, 0))],
        out_specs=[pl.BlockSpec((8, 128), lambda i: (0, 0)), row, pl.BlockSpec((8, D), lambda i: (0, 0))],
        out_shape=[jax.ShapeDtypeStruct((8, 128), F32), jax.ShapeDtypeStruct((T, D), F32),
                   jax.ShapeDtypeStruct((8, D), F32)],
        compiler_params=_cparams(("arbitrary",)))(x, tgt, g)


def _seq_masks(T, n_ctx, width):
    row = lax.broadcasted_iota(jnp.int32, (T, width), 0)
    in_ctx = row < n_ctx
    return jnp.where(in_ctx, row, row - n_ctx), jnp.where(in_ctx, n_ctx, T - n_ctx)


def _shift_rows(u, off, t_loc, seg_len):
    T = u.shape[0]
    if off == 0:
        return u
    v = pltpu.roll(u, (-off) % T, 0)
    ok = (t_loc + off >= 0) & (t_loc + off < seg_len)
    return jnp.where(ok, v, 0.0)


def _conv_fwd(proj, col0_blk, ncol, conv_w8, conv_b, n_ctx, tc, name):
    T = proj.shape[0]

    def body(u_ref, w_ref, b_ref, o_ref):
        u = u_ref[...]
        t_loc, seg_len = _seq_masks(T, n_ctx, tc)
        acc = jnp.broadcast_to(b_ref[...], u.shape)
        for i in range(CONV_K):
            acc = acc + w_ref[i:i + 1, :] * _shift_rows(u, i - CONV_K // 2, t_loc, seg_len)
        o_ref[...] = _silu(acc)

    return pl.pallas_call(
        body, name=name, grid=(ncol // tc,),
        in_specs=[pl.BlockSpec((T, tc), lambda j: (0, col0_blk + j)), pl.BlockSpec((8, tc), lambda j: (0, j)),
                  pl.BlockSpec((1, tc), lambda j: (0, j))],
        out_specs=pl.BlockSpec((T, tc), lambda j: (0, j)),
        out_shape=jax.ShapeDtypeStruct((T, ncol), F32),
        compiler_params=_cparams(("parallel",)))(proj, conv_w8, conv_b)


def _conv_bwd(proj, col0_blk, d2, w_blk0, conv_w8, conv_b, n_ctx, tc, name, skip=None):
    T, ncol = d2.shape[1], d2.shape[2]

    def body(u_ref, d_ref, w_ref, b_ref, *rest):
        if skip is None:
            du_ref, dw_ref = rest
        else:
            dy_ref, k_ref, du_ref, dw_ref = rest
        u = u_ref[...]
        t_loc, seg_len = _seq_masks(T, n_ctx, tc)
        pre = jnp.broadcast_to(b_ref[...], u.shape)
        for i in range(CONV_K):
            pre = pre + w_ref[i:i + 1, :] * _shift_rows(u, i - CONV_K // 2, t_loc, seg_len)
        r = lax.broadcasted_iota(jnp.int32, (8, tc), 0)
        dact = d_ref[0] + d_ref[1]
        dw = jnp.zeros((8, tc), F32)
        if skip is not None:
            dyv = dy_ref[...]
            dact = dact + (k_ref[0:1, :] + k_ref[1:2, :]) * dyv
            dw = jnp.where(r == CONV_K + 1, jnp.sum(dyv * _silu(pre), axis=0, keepdims=True), 0.0)
        dpre = dact * _dsilu(pre)
        du = jnp.zeros_like(u)
        dw = dw + jnp.where(r == CONV_K, jnp.sum(dpre, axis=0, keepdims=True), 0.0)
        for i in range(CONV_K):
            off = i - CONV_K // 2
            du = du + w_ref[i:i + 1, :] * _shift_rows(dpre, -off, t_loc, seg_len)
            dw = dw + jnp.where(r == i, jnp.sum(dpre * _shift_rows(u, off, t_loc, seg_len), axis=0, keepdims=True),
                                0.0)
        du_ref[...] = du.astype(BF16)
        dw_ref[...] = dw

    col = pl.BlockSpec((T, tc), lambda j: (0, j))
    in_specs = [pl.BlockSpec((T, tc), lambda j: (0, col0_blk + j)), pl.BlockSpec((2, T, tc), lambda j: (0, 0, j)),
                pl.BlockSpec((8, tc), lambda j: (0, w_blk0 + j)), pl.BlockSpec((1, tc), lambda j: (0, w_blk0 + j))]
    operands = [proj, d2, conv_w8, conv_b]
    if skip is not None:
        in_specs += [col, pl.BlockSpec((8, tc), lambda j: (0, j))]
        operands += list(skip)
    return pl.pallas_call(
        body, name=name, grid=(ncol // tc,), in_specs=in_specs,
        out_specs=[col, pl.BlockSpec((8, tc), lambda j: (0, j))],
        out_shape=[jax.ShapeDtypeStruct((T, ncol), BF16), jax.ShapeDtypeStruct((8, ncol), F32)],
        compiler_params=_cparams(("parallel",)))(*operands)


def _pool_core(u, half, t_loc, seg_len, transpose):
    tr = u.shape[0]

    def shift(v, s):
        w = pltpu.roll(v, s % tr, 0)
        ok = (t_loc - s >= 0) & (t_loc - s < seg_len)
        return jnp.where(ok, w, 0.0)

    cnt = (jnp.minimum(t_loc, half) + jnp.minimum(seg_len - t_loc, half)).astype(F32)
    q = u / cnt if transpose else u
    back, ahead, h = q, q, 1
    while h < half:
        back = back + shift(back, h)
        ahead = ahead + shift(ahead, -h)
        h *= 2
    if transpose:
        tot = back + shift(ahead, -1)
        return tot - u
    tot = shift(back, 1) + ahead
    return tot / cnt - u


def _pool_apply(src, col0_blk, out_dtype, n_ctx, pg, transpose, name):
    T = src.shape[0]

    def body(u_ref, o_ref):
        gi = pl.program_id(0)
        row = lax.broadcasted_iota(jnp.int32, (T, pg), 0)
        seg_len = jnp.where(row < n_ctx, n_ctx, GRID_W)
        t_loc = row & (seg_len - 1)
        u = u_ref[...].astype(F32)
        for k_idx, k in enumerate(POOL_WINDOWS):
            @pl.when(gi == k_idx)
            def _(k=k):
                o_ref[...] = _pool_core(u, k // 2, t_loc, seg_len, transpose).astype(o_ref.dtype)

    return pl.pallas_call(
        body, name=name, grid=(len(POOL_WINDOWS),),
        in_specs=[pl.BlockSpec((T, pg), lambda gi: (0, col0_blk + gi))],
        out_specs=pl.BlockSpec((T, pg), lambda gi: (0, gi)),
        out_shape=jax.ShapeDtypeStruct((T, pg * len(POOL_WINDOWS)), out_dtype),
        compiler_params=_cparams(("parallel",)))(src)


def _pool_mix_fwd(pm, pool_w, pool_scale, tr, name):
    T, W = pm.shape
    ng, pg = pool_w.shape[0], pool_w.shape[1]

    def body(p_ref, w_ref, s_ref, o_ref):
        o_ref[...] = (jnp.dot(p_ref[...], w_ref[...], preferred_element_type=F32) * s_ref[...]).astype(BF16)

    return pl.pallas_call(
        body, name=name, grid=(T // tr, ng),
        in_specs=[pl.BlockSpec((tr, pg), lambda i, g: (i, g)), pl.BlockSpec((None, pg, pg), lambda i, g: (g, 0, 0)),
                  pl.BlockSpec((1, pg), lambda i, g: (0, g))],
        out_specs=pl.BlockSpec((tr, pg), lambda i, g: (i, g)),
        out_shape=jax.ShapeDtypeStruct((T, W), BF16),
        compiler_params=_cparams(("parallel", "parallel")))(pm, pool_w, pool_scale)


def _pool_mix_bwd(pm, dpms, pool_w, pool_scale, tr, name):
    T, W = pm.shape
    ng, pg = pool_w.shape[0], pool_w.shape[1]

    def body(p_ref, d_ref, w_ref, s_ref, dp_ref, dw_ref, ds_ref):
        i = pl.program_id(1)

        @pl.when(i == 0)
        def _():
            dw_ref[...] = jnp.zeros_like(dw_ref)
            ds_ref[...] = jnp.zeros_like(ds_ref)

        p, w = p_ref[...], w_ref[...]
        d = d_ref[...].astype(F32)
        pmix = jnp.dot(p, w, preferred_element_type=F32)
        r = lax.broadcasted_iota(jnp.int32, (8, pg), 0)
        ds_ref[...] += jnp.where(r == 0, jnp.sum(d * pmix, axis=0, keepdims=True), 0.0)
        dmix = (d * s_ref[...]).astype(BF16)
        dp_ref[...] = lax.dot_general(dmix, w, (((1,), (1,)), ((), ())), preferred_element_type=F32)
        dw_ref[...] += lax.dot_general(p, dmix, (((0,), (0,)), ((), ())), preferred_element_type=F32)

    return pl.pallas_call(
        body, name=name, grid=(ng, T // tr),
        in_specs=[pl.BlockSpec((tr, pg), lambda g, i: (i, g)), pl.BlockSpec((tr, pg), lambda g, i: (i, g)),
                  pl.BlockSpec((None, pg, pg), lambda g, i: (g, 0, 0)), pl.BlockSpec((1, pg), lambda g, i: (0, g))],
        out_specs=[pl.BlockSpec((tr, pg), lambda g, i: (i, g)), pl.BlockSpec((None, pg, pg), lambda g, i: (g, 0, 0)),
                   pl.BlockSpec((8, pg), lambda g, i: (0, g))],
        out_shape=[jax.ShapeDtypeStruct((T, W), F32), jax.ShapeDtypeStruct((ng, pg, pg), F32),
                   jax.ShapeDtypeStruct((8, W), F32)],
        compiler_params=_cparams(("parallel", "arbitrary")))(pm, dpms, pool_w, pool_scale)


def _chunk_cumsum(v, upper):
    Q = v.shape[0]
    ii = lax.broadcasted_iota(jnp.int32, (Q, Q), 0)
    jj = lax.broadcasted_iota(jnp.int32, (Q, Q), 1)
    tri = ((jj >= ii) if upper else (jj <= ii)).astype(BF16)
    h1 = v.astype(BF16)
    r1 = v - h1.astype(F32)
    h2 = r1.astype(BF16)
    h3 = (r1 - h2.astype(F32)).astype(BF16)
    return (jnp.dot(tri, h1, preferred_element_type=F32) + jnp.dot(tri, h2, preferred_element_type=F32)
            + jnp.dot(tri, h3, preferred_element_type=F32))


def _split3(v):
    h1 = v.astype(BF16)
    r1 = v - h1.astype(F32)
    h2 = r1.astype(BF16)
    return h1, h2, (r1 - h2.astype(F32)).astype(BF16)


def _dt_prep(proj, dt_blk, bias, a_log, expand, n_heads, name):
    T = proj.shape[0]
    Wd = expand.shape[1]
    Q = CHUNK
    row = pl.BlockSpec((Q, 128), lambda i: (i, 0))
    wide = pl.BlockSpec((Q, Wd), lambda i: (i, 0))

    def body(r_ref, b_ref, al_ref, e_ref, l1_ref, l2_ref, l3_ref, dtb_ref, ein_ref, dte_ref, etot_ref):
        xv = r_ref[...] + b_ref[...]
        dt = jnp.maximum(xv, 0.0) + jnp.log(1.0 + jnp.exp(-jnp.abs(xv)))
        a = -jnp.exp(al_ref[...]) * dt
        fwd_col = lax.broadcasted_iota(jnp.int32, (Q, 128), 1) < n_heads
        lam = jnp.where(fwd_col, _chunk_cumsum(a, False), _chunk_cumsum(a, True))
        tot = jnp.where(fwd_col[0:1], lam[Q - 1:Q], lam[0:1])
        l1_ref[...], l2_ref[...], l3_ref[...] = _split3(lam)
        etot_ref[...] = jnp.broadcast_to(jnp.exp(tot), (8, 128))
        ex = e_ref[...]

        def rep(v):
            p1, p2, p3 = _split3(v)
            return (jnp.dot(p1, ex, preferred_element_type=F32) + jnp.dot(p2, ex, preferred_element_type=F32)
                    + jnp.dot(p3, ex, preferred_element_type=F32))

        dtb_ref[...] = rep(dt)
        ein_ref[...] = rep(jnp.exp(lam))
        dte_ref[...] = rep(jnp.exp(tot - lam))

    vec = pl.BlockSpec((1, 128), lambda i: (0, 0))
    return pl.pallas_call(
        body, name=name, grid=(T // Q,),
        in_specs=[pl.BlockSpec((Q, 128), lambda i: (i, dt_blk)), vec, vec, pl.BlockSpec((128, Wd), lambda i: (0, 0))],
        out_specs=[row, row, row, wide, wide, wide, pl.BlockSpec((8, 128), lambda i: (i, 0))],
        out_shape=[jax.ShapeDtypeStruct((T, 128), BF16)] * 3 + [jax.ShapeDtypeStruct((T, Wd), F32)] * 3
        + [jax.ShapeDtypeStruct((T // Q * 8, 128), F32)],
        compiler_params=_cparams(("parallel",)))(proj, bias, a_log, expand)


def _dt_bwd(proj, dt_blk, bias, a_log, ddt, dlam, n_heads, tr, name):
    T = proj.shape[0]
    row = pl.BlockSpec((tr, 128), lambda i: (i, 0))
    vec = pl.BlockSpec((1, 128), lambda i: (0, 0))

    def body(r_ref, b_ref, al_ref, ddt_ref, dl_ref, o_ref, st_ref):
        @pl.when(pl.program_id(0) == 0)
        def _():
            st_ref[...] = jnp.zeros_like(st_ref)

        xv = r_ref[...] + b_ref[...]
        dt = jnp.maximum(xv, 0.0) + jnp.log(1.0 + jnp.exp(-jnp.abs(xv)))
        a_neg = -jnp.exp(al_ref[...])
        col = lax.broadcasted_iota(jnp.int32, (CHUNK, 128), 1)
        dl = dl_ref[...]
        parts = []
        for k in range(tr // CHUNK):
            dk = dl[k * CHUNK:(k + 1) * CHUNK]
            parts.append(jnp.where(col < n_heads, _chunk_cumsum(dk, True), _chunk_cumsum(dk, False)))
        dav = jnp.concatenate(parts, axis=0)
        draw = (ddt_ref[...] + dav * a_neg) * jax.nn.sigmoid(xv)
        o_ref[...] = draw.astype(BF16)
        r = lax.broadcasted_iota(jnp.int32, (8, 128), 0)
        st_ref[...] += (jnp.where(r == 0, jnp.sum(draw, axis=0, keepdims=True), 0.0)
                        + jnp.where(r == 1, jnp.sum(dav * dt, axis=0, keepdims=True) * a_neg, 0.0))

    return pl.pallas_call(
        body, name=name, grid=(T // tr,),
        in_specs=[pl.BlockSpec((tr, 128), lambda i: (i, dt_blk)), vec, vec, row, row],
        out_specs=[row, pl.BlockSpec((8, 128), lambda i: (0, 0))],
        out_shape=[jax.ShapeDtypeStruct((T, 128), BF16), jax.ShapeDtypeStruct((8, 128), F32)],
        compiler_params=_cparams(("arbitrary",)))(proj, bias, a_log, ddt, dlam)


def _scan_chunk(d, pos, nc_ctx, nc):
    rev = jnp.where(pos < nc_ctx, nc_ctx - 1 - pos, nc - 1 - (pos - nc_ctx))
    return jnp.where(d == 0, pos, rev)


def _chunk_mask(d):
    ii = lax.broadcasted_iota(jnp.int32, (CHUNK, CHUNK), 0)
    jj = lax.broadcasted_iota(jnp.int32, (CHUNK, CHUNK), 1)
    return (ii - jj) * jnp.where(d == 0, 1, -1) >= 0


def _ssd_specs(T, G, n_ctx, gpb, chunk_of):
    R, P, N, Q = HPG, HEADDIM, STATE, CHUNK
    H = G * R
    nc, nc_ctx = T // Q, n_ctx // Q
    xw, bw = gpb * R * P, gpb * N
    b_blk0 = (H * P) // bw
    c_blk0 = b_blk0 + G // gpb

    def ch(d, s):
        return chunk_of(d, s, nc_ctx, nc)

    return dict(
        x=pl.BlockSpec((Q, xw), lambda d, g, s: (ch(d, s), g)),
        b=pl.BlockSpec((Q, bw), lambda d, g, s: (ch(d, s), b_blk0 + g)),
        c=pl.BlockSpec((Q, bw), lambda d, g, s: (ch(d, s), c_blk0 + g)),
        col=pl.BlockSpec((None, gpb, Q, R), lambda d, g, s: (d, g, ch(d, s), 0)),
        row=pl.BlockSpec((None, gpb, R, Q), lambda d, g, s: (d, g, 0, ch(d, s))),
        rep=pl.BlockSpec((Q, xw), lambda d, g, s: (ch(d, s), d * (G // gpb) + g)),
        lam_a=pl.BlockSpec((None, gpb * R, 16, Q), lambda d, g, s: (d, g, 0, ch(d, s))),
        lam_b=pl.BlockSpec((None, gpb * R, 16, Q), lambda d, g, s: (d, g, 0, ch(d, s))),
        etot=pl.BlockSpec((None, gpb, None, 8, 128), lambda d, g, s: (d, g, ch(d, s), 0, 0)),
        dsk=pl.BlockSpec((None, 8, xw), lambda d, g, s: (d, 0, g)),
        xd=pl.BlockSpec((None, Q, xw), lambda d, g, s: (d, ch(d, s), g)),
        bd=pl.BlockSpec((None, Q, bw), lambda d, g, s: (d, ch(d, s), g)),
        st=pl.BlockSpec((None, None, gpb * R // 2, 2 * P, N), lambda d, g, s: (d, ch(d, s), g, 0, 0)),
    )


class _Exchange:
    def __init__(self, operand, out_sds, sems, start, finish):
        self.operand, self.out_sds, self.sems, self.start, self.finish = operand, out_sds, sems, start, finish


def _call_with_exchange(body, exch, *, name, grid, in_specs, out_specs, out_shape, scratch_shapes, operands):
    if exch is None:
        return pl.pallas_call(body, name=name, grid=grid, in_specs=in_specs, out_specs=out_specs,
                              out_shape=out_shape, scratch_shapes=scratch_shapes,
                              compiler_params=_cparams(("arbitrary",) * len(grid)))(*operands)
    n_in, n_out, n_scr = len(in_specs), len(out_specs), len(scratch_shapes)

    def fused(*refs):
        ins, c_in = refs[:n_in], refs[n_in]
        outs, c_out = refs[n_in + 1:n_in + 1 + n_out], refs[n_in + 1 + n_out]
        scr = refs[n_in + 2 + n_out:n_in + 2 + n_out + n_scr]
        sems = refs[n_in + 2 + n_out + n_scr:]
        ids = [pl.program_id(a) for a in range(len(grid))]
        first = functools.reduce(lambda p, q: p & q, [i == 0 for i in ids])
        last = functools.reduce(lambda p, q: p & q, [i == n - 1 for i, n in zip(ids, grid)])

        @pl.when(first)
        def _():
            exch.start(c_in, c_out, *sems)

        body(*ins, *outs, *scr)

        @pl.when(last)
        def _():
            exch.finish(c_in, c_out, *sems)

    return pl.pallas_call(fused, name=name, grid=grid, in_specs=list(in_specs) + [ANY],
                          out_specs=list(out_specs) + [ANY], out_shape=list(out_shape) + [exch.out_sds],
                          scratch_shapes=list(scratch_shapes) + list(exch.sems),
                          compiler_params=_cparams(("arbitrary",) * len(grid)))(*operands, exch.operand)


def _ssd_fwd(xbc, dtb, ein, dte, lam_a, lam_b, etot, dsk, G, n_ctx, gpb, name, exch=None):
    T = xbc.shape[0]
    R, P, N, Q = HPG, HEADDIM, STATE, CHUNK
    H = G * R
    nc = T // Q
    sp = _ssd_specs(T, G, n_ctx, gpb, _scan_chunk)

    def body(x_ref, b_ref, c_ref, dt_ref, ein_ref, dte_ref, la_ref, lb_ref, et_ref, dsk_ref, y_ref, st_ref, S):
        d, s = pl.program_id(0), pl.program_id(2)

        @pl.when(s == 0)
        def _():
            S[...] = jnp.zeros_like(S)

        mask = _chunk_mask(d)
        head0 = lax.broadcasted_iota(jnp.int32, (Q, 2 * P), 1) < P
        rows0 = lax.broadcasted_iota(jnp.int32, (2 * P, N), 0) < P
        for gg in range(gpb):
            Bm = b_ref[:, gg * N:(gg + 1) * N].astype(BF16)
            Cm = c_ref[:, gg * N:(gg + 1) * N].astype(BF16)
            Gm = lax.dot_general(Cm, Bm, (((1,), (1,)), ((), ())), preferred_element_type=F32)
            for k in range(R // 2):
                pk = gg * (R // 2) + k
                sl = slice(pk * 2 * P, (pk + 1) * 2 * P)
                xp = x_ref[:, sl]
                xc = xp * dt_ref[:, sl]
                s_in = S[pk]
                y = lax.dot_general(Cm, s_in.astype(BF16), (((1,), (1,)), ((), ())),
                                    preferred_element_type=F32) * ein_ref[:, sl] + dsk_ref[0:1, sl] * xp
                for j in range(2):
                    hr = 2 * pk + j
                    diff = lax.dot_general(la_ref[hr], lb_ref[hr], (((0,), (0,)), ((), ())), preferred_element_type=F32)
                    ldec = jnp.exp(jnp.where(mask, diff, NEG_BIG))
                    xc_j = (jnp.where(head0, xc, 0.0) if j == 0 else jnp.where(head0, 0.0, xc)).astype(BF16)
                    y = y + jnp.dot((Gm * ldec).astype(BF16), xc_j, preferred_element_type=F32)
                y_ref[:, sl] = y
                st_ref[pk] = s_in
                e_all = jnp.where(rows0, et_ref[gg, 2 * k:2 * k + 1, :], et_ref[gg, 2 * k + 1:2 * k + 2, :])
                xd = (xc * dte_ref[:, sl]).astype(BF16)
                S[pk] = e_all * s_in + lax.dot_general(xd, Bm, (((0,), (0,)), ((), ())),
                                                       preferred_element_type=F32)

    return _call_with_exchange(
        body, exch, name=name, grid=(2, G // gpb, nc),
        in_specs=[sp["x"], sp["b"], sp["c"], sp["rep"], sp["rep"], sp["rep"], sp["lam_a"], sp["lam_b"], sp["etot"],
                  sp["dsk"]],
        out_specs=[sp["xd"], sp["st"]],
        out_shape=[jax.ShapeDtypeStruct((2, T, H * P), F32), jax.ShapeDtypeStruct((2, nc, H // 2, 2 * P, N), F32)],
        scratch_shapes=[pltpu.VMEM((gpb * R // 2, 2 * P, N), F32)],
        operands=(xbc, xbc, xbc, dtb, ein, dte, lam_a, lam_b, etot, dsk))


def _ssd_bwd(xbc, dy, states, dtb, ein, dte, lam_a, lam_b, etot, G, n_ctx, gpb, name, exch=None):
    T = xbc.shape[0]
    R, P, N, Q = HPG, HEADDIM, STATE, CHUNK
    H = G * R
    nc = T // Q
    sp = _ssd_specs(T, G, n_ctx, gpb, lambda d, s, nc_ctx, n: _scan_chunk(d, n - 1 - s, nc_ctx, n))

    def body(x_ref, b_ref, c_ref, dy_ref, st_ref, dt_ref, ein_ref, dte_ref, la_ref, lb_ref, et_ref,
             dx_ref, db_ref, dc_ref, ddt_ref, dlc_ref, dlr_ref, dS):
        d, s = pl.program_id(0), pl.program_id(2)

        @pl.when(s == 0)
        def _():
            dS[...] = jnp.zeros_like(dS)

        mask = _chunk_mask(d)
        ri = lax.broadcasted_iota(jnp.int32, (Q, 1), 0)
        is_last = ri == jnp.where(d == 0, Q - 1, 0)
        head0 = lax.broadcasted_iota(jnp.int32, (Q, 2 * P), 1) < P
        rows0 = lax.broadcasted_iota(jnp.int32, (2 * P, N), 0) < P

        def total(v):
            return jnp.sum(jnp.sum(v, axis=1, keepdims=True), axis=0, keepdims=True)

        for gg in range(gpb):
            Bm = b_ref[:, gg * N:(gg + 1) * N].astype(BF16)
            Cm = c_ref[:, gg * N:(gg + 1) * N].astype(BF16)
            Gm = lax.dot_general(Cm, Bm, (((1,), (1,)), ((), ())), preferred_element_type=F32)
            dG = jnp.zeros((Q, Q), F32)
            dB = jnp.zeros((Q, N), F32)
            dC = jnp.zeros((Q, N), F32)
            for k in range(R // 2):
                pk = gg * (R // 2) + k
                sl = slice(pk * 2 * P, (pk + 1) * 2 * P)
                e_in = ein_ref[:, sl]
                dte = dte_ref[:, sl]
                e_all = jnp.where(rows0, et_ref[gg, 2 * k:2 * k + 1, :], et_ref[gg, 2 * k + 1:2 * k + 2, :])
                xp = x_ref[:, sl]
                dtp = dt_ref[:, sl]
                xc = xp * dtp
                xc_b = xc.astype(BF16)
                dyp = dy_ref[:, sl]
                s_in = st_ref[pk]
                s_in_b = s_in.astype(BF16)
                ds_out = dS[pk]
                ds_out_b = ds_out.astype(BF16)
                y_int = lax.dot_general(Cm, s_in_b, (((1,), (1,)), ((), ())), preferred_element_type=F32) * e_in
                b_ds = lax.dot_general(Bm, ds_out_b, (((1,), (1,)), ((), ())), preferred_element_type=F32)
                dxc = dte * b_ds
                u = xc * dxc
                v = dyp * y_int - u
                sse = ds_out * s_in * e_all
                for j in range(2):
                    hr, r = 2 * pk + j, 2 * k + j

                    def pick(a, m0=head0, j=j):
                        return jnp.where(m0, a, 0.0) if j == 0 else jnp.where(m0, 0.0, a)

                    diff = lax.dot_general(la_ref[hr], lb_ref[hr], (((0,), (0,)), ((), ())), preferred_element_type=F32)
                    ldec = jnp.exp(jnp.where(mask, diff, NEG_BIG))
                    dy_j = pick(dyp).astype(BF16)
                    dM = lax.dot_general(dy_j, xc_b, (((1,), (1,)), ((), ())), preferred_element_type=F32)
                    dMl = dM * ldec
                    Wm = dMl * Gm
                    dlam_c = jnp.sum(Wm, axis=1, keepdims=True) + jnp.sum(pick(v), axis=1, keepdims=True)
                    last = total(pick(sse, rows0)) + total(pick(u))
                    dlc_ref[gg, :, r:r + 1] = dlam_c + jnp.where(is_last, last, 0.0)
                    dlr_ref[gg, r:r + 1, :] = -jnp.sum(Wm, axis=0, keepdims=True)
                    dxc = dxc + lax.dot_general((Gm * ldec).astype(BF16), dy_j, (((0,), (0,)), ((), ())),
                                                preferred_element_type=F32)
                    dG = dG + dMl
                dx_ref[:, sl] = dxc * dtp
                t = dxc * xp
                ddt_ref[gg, :, 2 * k:2 * k + 1] = jnp.sum(jnp.where(head0, t, 0.0), axis=1, keepdims=True)
                ddt_ref[gg, :, 2 * k + 1:2 * k + 2] = jnp.sum(jnp.where(head0, 0.0, t), axis=1, keepdims=True)
                edy_b = (e_in * dyp).astype(BF16)
                dC = dC + jnp.dot(edy_b, s_in_b, preferred_element_type=F32)
                dB = dB + jnp.dot((dte * xc).astype(BF16), ds_out_b, preferred_element_type=F32)
                dS[pk] = e_all * ds_out + lax.dot_general(edy_b, Cm, (((0,), (0,)), ((), ())),
                                                          preferred_element_type=F32)
            dG_b = dG.astype(BF16)
            dc_ref[:, gg * N:(gg + 1) * N] = dC + jnp.dot(dG_b, Bm, preferred_element_type=F32)
            db_ref[:, gg * N:(gg + 1) * N] = dB + lax.dot_general(dG_b, Cm, (((0,), (0,)), ((), ())),
                                                                  preferred_element_type=F32)

    return _call_with_exchange(
        body, exch, name=name, grid=(2, G // gpb, nc),
        in_specs=[sp["x"], sp["b"], sp["c"], sp["x"], sp["st"], sp["rep"], sp["rep"], sp["rep"], sp["lam_a"],
                  sp["lam_b"], sp["etot"]],
        out_specs=[sp["xd"], sp["bd"], sp["bd"], sp["col"], sp["col"], sp["row"]],
        out_shape=[jax.ShapeDtypeStruct((2, T, H * P), F32), jax.ShapeDtypeStruct((2, T, G * N), F32),
                   jax.ShapeDtypeStruct((2, T, G * N), F32), jax.ShapeDtypeStruct((2, G, T, R), F32),
                   jax.ShapeDtypeStruct((2, G, T, R), F32), jax.ShapeDtypeStruct((2, G, R, T), F32)],
        scratch_shapes=[pltpu.VMEM((gpb * R // 2, 2 * P, N), F32)],
        operands=(xbc, xbc, xbc, dy, states, dtb, ein, dte, lam_a, lam_b, etot))


def _gnorm_fwd(y2, proj, w, gs, tr, name):
    T, HP = y2.shape[1], y2.shape[2]

    def body(y_ref, z_ref, w_ref, o_ref):
        yz = (y_ref[0] + y_ref[1]) * _silu(z_ref[...])
        for g in range(HP // gs):
            v = yz[:, g * gs:(g + 1) * gs]
            rstd = lax.rsqrt(jnp.mean(v * v, axis=-1, keepdims=True) + EPS)
            o_ref[:, g * gs:(g + 1) * gs] = (v * rstd * w_ref[:, g * gs:(g + 1) * gs]).astype(BF16)

    return pl.pallas_call(
        body, name=name, grid=(T // tr,),
        in_specs=[pl.BlockSpec((2, tr, HP), lambda i: (0, i, 0)), pl.BlockSpec((tr, HP), lambda i: (i, 0)),
                  pl.BlockSpec((1, HP), lambda i: (0, 0))],
        out_specs=pl.BlockSpec((tr, HP), lambda i: (i, 0)),
        out_shape=jax.ShapeDtypeStruct((T, HP), BF16),
        compiler_params=_cparams(("parallel",)))(y2, proj, w)


def _gnorm_bwd(y2, proj, w, dyn, gs, tr, name):
    T, HP = y2.shape[1], y2.shape[2]
    row = pl.BlockSpec((tr, HP), lambda i: (i, 0))

    def body(y_ref, z_ref, w_ref, d_ref, dy_ref, dz_ref, dw_ref):
        @pl.when(pl.program_id(0) == 0)
        def _():
            dw_ref[...] = jnp.zeros_like(dw_ref)

        yv = y_ref[0] + y_ref[1]
        zv = z_ref[...]
        sz = _silu(zv)
        yz = yv * sz
        dv = d_ref[...]
        r = lax.broadcasted_iota(jnp.int32, (8, gs), 0)
        for g in range(HP // gs):
            sl = slice(g * gs, (g + 1) * gs)
            v = yz[:, sl]
            rstd = lax.rsqrt(jnp.mean(v * v, axis=-1, keepdims=True) + EPS)
            xhat = v * rstd
            dyn_g = dv[:, sl]
            dhat = dyn_g * w_ref[:, sl]
            dyz = rstd * (dhat - xhat * jnp.mean(dhat * xhat, axis=-1, keepdims=True))
            dy_ref[:, sl] = dyz * sz[:, sl]
            dz_ref[:, sl] = (dyz * yv[:, sl] * _dsilu(zv[:, sl])).astype(BF16)
            dw_ref[:, sl] += jnp.where(r == 0, jnp.sum(dyn_g * xhat, axis=0, keepdims=True), 0.0)

    return pl.pallas_call(
        body, name=name, grid=(T // tr,),
        in_specs=[pl.BlockSpec((2, tr, HP), lambda i: (0, i, 0)), row, pl.BlockSpec((1, HP), lambda i: (0, 0)), row],
        out_specs=[row, row, pl.BlockSpec((8, HP), lambda i: (0, 0))],
        out_shape=[jax.ShapeDtypeStruct((T, HP), F32), jax.ShapeDtypeStruct((T, HP), BF16),
                   jax.ShapeDtypeStruct((8, HP), F32)],
        compiler_params=_cparams(("arbitrary",)))(y2, proj, w, dyn)


def _merge_fwd(proj, g1_blk, o_ssd, o_pool, tr, name):
    T, D = o_ssd.shape
    row = pl.BlockSpec((tr, D), lambda i: (i, 0))

    def body(g1_ref, g2_ref, a_ref, b_ref, o_ref):
        o_ref[...] = (jax.nn.sigmoid(g1_ref[...]) * a_ref[...]
                      + jax.nn.sigmoid(g2_ref[...]) * b_ref[...]).astype(BF16)

    return pl.pallas_call(
        body, name=name, grid=(T // tr,),
        in_specs=[pl.BlockSpec((tr, D), lambda i: (i, g1_blk)), pl.BlockSpec((tr, D), lambda i: (i, g1_blk + 1)),
                  row, row],
        out_specs=row, out_shape=jax.ShapeDtypeStruct((T, D), BF16),
        compiler_params=_cparams(("parallel",)))(proj, proj, o_ssd, o_pool)


def _merge_bwd(proj, g1_blk, o_ssd, o_pool, dmg, tr, name):
    T, D = o_ssd.shape
    row = pl.BlockSpec((tr, D), lambda i: (i, 0))

    def body(g1_ref, g2_ref, a_ref, b_ref, d_ref, da_ref, db_ref, dg_ref):
        s1, s2 = jax.nn.sigmoid(g1_ref[...]), jax.nn.sigmoid(g2_ref[...])
        dv = d_ref[...]
        da_ref[...] = (s1 * dv).astype(BF16)
        db_ref[...] = (s2 * dv).astype(BF16)
        dg_ref[:, :D] = (dv * a_ref[...] * s1 * (1.0 - s1)).astype(BF16)
        dg_ref[:, D:] = (dv * b_ref[...] * s2 * (1.0 - s2)).astype(BF16)

    return pl.pallas_call(
        body, name=name, grid=(T // tr,),
        in_specs=[pl.BlockSpec((tr, D), lambda i: (i, g1_blk)), pl.BlockSpec((tr, D), lambda i: (i, g1_blk + 1)),
                  row, row, row],
        out_specs=[row, row, pl.BlockSpec((tr, 2 * D), lambda i: (i, 0))],
        out_shape=[jax.ShapeDtypeStruct((T, D), BF16), jax.ShapeDtypeStruct((T, D), BF16),
                   jax.ShapeDtypeStruct((T, 2 * D), BF16)],
        compiler_params=_cparams(("parallel",)))(proj, proj, o_ssd, o_pool, dmg)


def _swiglu_fwd(gu, tr, name):
    T, F2 = gu.shape
    F = F2 // 2
    tc = _tile(F, (1408, 768, 512, 256, 128))
    nb = F // tc

    def body(a_ref, b_ref, o_ref):
        o_ref[...] = (_silu(a_ref[...]) * b_ref[...]).astype(BF16)

    return pl.pallas_call(
        body, name=name, grid=(T // tr, nb),
        in_specs=[pl.BlockSpec((tr, tc), lambda i, j: (i, j)), pl.BlockSpec((tr, tc), lambda i, j: (i, nb + j))],
        out_specs=pl.BlockSpec((tr, tc), lambda i, j: (i, j)),
        out_shape=jax.ShapeDtypeStruct((T, F), BF16),
        compiler_params=_cparams(("parallel", "parallel")))(gu, gu)


def _swiglu_bwd(gu, dact, tr, name):
    T, F2 = gu.shape
    F = F2 // 2
    tc = _tile(F, (1408, 768, 512, 256, 128))
    nb = F // tc

    def body(a_ref, b_ref, d_ref, o_ref):
        is_a = pl.program_id(1) < nb
        av, bv, dv = a_ref[...], b_ref[...], d_ref[...]
        o_ref[...] = jnp.where(is_a, dv * bv * _dsilu(av), dv * _silu(av)).astype(BF16)

    return pl.pallas_call(
        body, name=name, grid=(T // tr, 2 * nb),
        in_specs=[pl.BlockSpec((tr, tc), lambda i, j: (i, j % nb)), pl.BlockSpec((tr, tc), lambda i, j: (i, nb + j % nb)),
                  pl.BlockSpec((tr, tc), lambda i, j: (i, j % nb))],
        out_specs=pl.BlockSpec((tr, tc), lambda i, j: (i, j)),
        out_shape=jax.ShapeDtypeStruct((T, F2), BF16),
        compiler_params=_cparams(("parallel", "parallel")))(gu, gu, dact)


def _adamw(w, g, m, v, name):
    Rr, C = w.shape
    tr = _tile(Rr, (256, 128, 64, 32, 16, 8))
    row = pl.BlockSpec((tr, C), lambda i: (i, 0))

    def body(w_ref, g_ref, m_ref, v_ref, d_ref, mo_ref, vo_ref):
        gv = g_ref[...]
        mn = ADAM_B1 * m_ref[...] + (1.0 - ADAM_B1) * gv
        vn = ADAM_B2 * v_ref[...] + (1.0 - ADAM_B2) * (gv * gv)
        m_hat = mn / (1.0 - ADAM_B1 ** ADAM_STEP)
        v_hat = vn / (1.0 - ADAM_B2 ** ADAM_STEP)
        d_ref[...] = -ADAM_LR * (m_hat / (jnp.sqrt(v_hat) + ADAM_EPS) + ADAM_WD * w_ref[...])
        mo_ref[...] = mn
        vo_ref[...] = vn

    sds = jax.ShapeDtypeStruct((Rr, C), F32)
    return pl.pallas_call(body, name=name, grid=(Rr // tr,), in_specs=[row] * 4, out_specs=[row] * 3,
                          out_shape=[sds] * 3, compiler_params=_cparams(("parallel",)))(w, g, m, v)


def _sum_slots(x, name):
    n, Rr, C = x.shape
    tr = _tile(Rr, (512, 256, 128, 64, 32, 16, 8))

    def body(x_ref, o_ref):
        acc = x_ref[0].astype(F32)
        for k in range(1, n):
            acc = acc + x_ref[k].astype(F32)
        o_ref[...] = acc

    return pl.pallas_call(body, name=name, grid=(Rr // tr,),
                          in_specs=[pl.BlockSpec((n, tr, C), lambda i: (0, i, 0))],
                          out_specs=pl.BlockSpec((tr, C), lambda i: (i, 0)),
                          out_shape=jax.ShapeDtypeStruct((Rr, C), F32),
                          compiler_params=_cparams(("parallel",)))(x)


def _place():
    return lax.axis_index("x"), lax.axis_index("y"), lax.axis_index("c")


def _all_gather(x, name):
    Rr, C = x.shape

    def body(x_ref, out_ref, send_sems, recv_sems, local_sem):
        _gather_start(x_ref, out_ref, send_sems, recv_sems, local_sem)
        _gather_finish(x_ref, out_ref, send_sems, recv_sems, local_sem)

    return pl.pallas_call(
        body, name=name, in_specs=[ANY], out_specs=ANY,
        out_shape=jax.ShapeDtypeStruct((N_DEV, Rr, C), x.dtype), scratch_shapes=_GATHER_SEMS,
    )(x)


_GATHER_SEMS = [pltpu.SemaphoreType.DMA((7,)), pltpu.SemaphoreType.DMA((7,)), pltpu.SemaphoreType.DMA]


def _gather_copies(x_ref, out_ref, send_sems, recv_sems, local_sem):
    mx, my, mc = _place()
    me, sibling = (mx, my, mc), (mx, my, 1 - mc)
    chips = [(1 - mx, my), (mx, 1 - my), (1 - mx, 1 - my)]

    def slot(px, py, pc):
        return out_ref.at[4 * px + 2 * py + pc]

    def copy(k, block, to, src=None):
        return pltpu.make_async_remote_copy(
            src_ref=slot(*block) if src is None else src, dst_ref=slot(*block),
            send_sem=send_sems.at[k], recv_sem=recv_sems.at[k],
            device_id=to, device_id_type=pl.DeviceIdType.MESH)

    return dict(
        mine=pltpu.make_async_copy(x_ref, slot(*me), local_sem),
        first=[copy(0, me, sibling, src=x_ref)] + [copy(1 + j, me, (*chip, mc), src=x_ref)
                                                   for j, chip in enumerate(chips)],
        passed=[copy(4 + j, (*chip, mc), sibling) for j, chip in enumerate(chips)],
        from_chips=[copy(1 + j, (*chip, mc), me) for j, chip in enumerate(chips)],
        from_sibling=[copy(0, sibling, me)] + [copy(4 + j, (*chip, 1 - mc), me) for j, chip in enumerate(chips)],
    )


def _gather_start(*refs):
    cps = _gather_copies(*refs)
    cps["mine"].start()
    for cp in cps["first"]:
        cp.start()


def _gather_finish(*refs):
    cps = _gather_copies(*refs)
    for j in range(3):
        cps["from_chips"][j].wait_recv()
        cps["passed"][j].start()
    for cp in cps["from_sibling"]:
        cp.wait_recv()
    for cp in cps["first"] + cps["passed"]:
        cp.wait_send()
    cps["mine"].wait()


def _pair_exchange(buf, name):
    _, n, Rr, C = buf.shape
    parts = 4
    pr = Rr // parts

    def body(b_ref, got_ref, send_sems, recv_sems):
        mx, my, mc = _place()
        copies = []
        for q in range(n):
            for p in range(parts):
                rows = pl.ds(p * pr, pr)
                cp = pltpu.make_async_remote_copy(
                    src_ref=b_ref.at[1 - mc, q, rows], dst_ref=got_ref.at[q, rows],
                    send_sem=send_sems.at[q * parts + p], recv_sem=recv_sems.at[q * parts + p],
                    device_id=(mx, my, 1 - mc), device_id_type=pl.DeviceIdType.MESH)
                cp.start()
                copies.append(cp)
        for cp in copies:
            cp.wait()

    return pl.pallas_call(
        body, name=name, in_specs=[ANY], out_specs=ANY,
        out_shape=jax.ShapeDtypeStruct((n, Rr, C), buf.dtype),
        scratch_shapes=[pltpu.SemaphoreType.DMA((n * parts,)), pltpu.SemaphoreType.DMA((n * parts,))],
    )(buf)


def _pair_add(buf, got, my_c, name):
    _, n, Rr, C = buf.shape
    tr = _tile(Rr, (512, 256, 128, 64, 32, 16))

    def body(c_ref, b_ref, g_ref, o_ref):
        o_ref[...] = (b_ref[...].astype(F32) + g_ref[...].astype(F32)).astype(BF16)

    return pl.pallas_call(
        body, name=name,
        grid_spec=pltpu.PrefetchScalarGridSpec(
            num_scalar_prefetch=1, grid=(n, Rr // tr),
            in_specs=[pl.BlockSpec((None, None, tr, C), lambda q, i, c: (c[0], q, i, 0)),
                      pl.BlockSpec((None, tr, C), lambda q, i, c: (q, i, 0))],
            out_specs=pl.BlockSpec((None, tr, C), lambda q, i, c: (q, i, 0))),
        out_shape=jax.ShapeDtypeStruct((n, Rr, C), BF16),
        compiler_params=_cparams(("parallel", "parallel")))(my_c, buf, got)


def _chip_exchange(red, name):
    def body(r_ref, out_ref, send_sems, recv_sems, local_sem):
        _chip_exchange_start(r_ref, out_ref, send_sems, recv_sems, local_sem)
        _chip_exchange_finish(r_ref, out_ref, send_sems, recv_sems, local_sem)

    return pl.pallas_call(
        body, name=name, in_specs=[ANY], out_specs=ANY,
        out_shape=jax.ShapeDtypeStruct(red.shape, red.dtype), scratch_shapes=_CHIP_SEMS,
    )(red)


_CHIP_SEMS = [pltpu.SemaphoreType.DMA((3,)), pltpu.SemaphoreType.DMA((3,)), pltpu.SemaphoreType.DMA]


def _chip_exchange_copies(r_ref, out_ref, send_sems, recv_sems, local_sem):
    mx, my, mc = _place()
    chips = [(1 - mx, my), (mx, 1 - my), (1 - mx, 1 - my)]

    def copy(k, src_slot, dst_slot, to):
        return pltpu.make_async_remote_copy(
            src_ref=r_ref.at[src_slot], dst_ref=out_ref.at[dst_slot],
            send_sem=send_sems.at[k], recv_sem=recv_sems.at[k],
            device_id=(*to, mc), device_id_type=pl.DeviceIdType.MESH)

    return dict(
        mine=pltpu.make_async_copy(r_ref.at[2 * mx + my], out_ref.at[2 * mx + my], local_sem),
        sends=[copy(k, 2 * px + py, 2 * mx + my, (px, py)) for k, (px, py) in enumerate(chips)],
        recvs=[copy(k, 2 * px + py, 2 * px + py, (px, py)) for k, (px, py) in enumerate(chips)],
    )


def _chip_exchange_start(*refs):
    cps = _chip_exchange_copies(*refs)
    cps["mine"].start()
    for cp in cps["sends"]:
        cp.start()


def _chip_exchange_finish(*refs):
    cps = _chip_exchange_copies(*refs)
    for cp in cps["recvs"]:
        cp.wait_recv()
    for cp in cps["sends"]:
        cp.wait_send()
    cps["mine"].wait()


def _pad_rows(a, rows):
    return jnp.pad(a, ((0, rows - a.shape[0]), (0, 0)))


class _Layout:
    def __init__(self, D, shards):
        self.D = D
        self.pieces = []
        off = 0
        for name, layer, rows in shards:
            pr = _round_up(rows, 16)
            self.pieces.append((name, layer, rows, pr, off))
            off += pr
        self.rows = _round_up(off, 256)

    def find(self, name, layer):
        for p in self.pieces:
            if p[0] == name and p[1] == layer:
                return p
        raise KeyError(name)


def kernel(x, c, ctx, c_ctx, w_ada, b_ada, g_mix, w_in, conv_w, conv_b, dt_bias, a_log, d_skip, ssd_norm_w, w_ssd_out, pool_w, pool_scale, w_pool_out, w_out, g_ffn, w_gate_up, w_down, g_final, loss_target, m_c_ctx, m_w_ada, m_b_ada, m_g_mix, m_w_in, m_conv_w, m_conv_b, m_dt_bias, m_a_log, m_d_skip, m_ssd_norm_w, m_w_ssd_out, m_pool_w, m_pool_scale, m_w_pool_out, m_w_out, m_g_ffn, m_w_gate_up, m_w_down, m_g_final, v_c_ctx, v_w_ada, v_b_ada, v_g_mix, v_w_in, v_conv_w, v_conv_b, v_dt_bias, v_a_log, v_d_skip, v_ssd_norm_w, v_w_ssd_out, v_pool_w, v_pool_scale, v_w_pool_out, v_w_out, v_g_ffn, v_w_gate_up, v_w_down, v_g_final):
    weights = dict(c_ctx=c_ctx, w_ada=w_ada, b_ada=b_ada, g_mix=g_mix, w_in=w_in, conv_w=conv_w, conv_b=conv_b,
                   dt_bias=dt_bias, a_log=a_log, d_skip=d_skip, ssd_norm_w=ssd_norm_w, w_ssd_out=w_ssd_out,
                   pool_w=pool_w, pool_scale=pool_scale, w_pool_out=w_pool_out, w_out=w_out, g_ffn=g_ffn,
                   w_gate_up=w_gate_up, w_down=w_down, g_final=g_final)
    moms_m = dict(c_ctx=m_c_ctx, w_ada=m_w_ada, b_ada=m_b_ada, g_mix=m_g_mix, w_in=m_w_in, conv_w=m_conv_w,
                  conv_b=m_conv_b, dt_bias=m_dt_bias, a_log=m_a_log, d_skip=m_d_skip, ssd_norm_w=m_ssd_norm_w,
                  w_ssd_out=m_w_ssd_out, pool_w=m_pool_w, pool_scale=m_pool_scale, w_pool_out=m_w_pool_out,
                  w_out=m_w_out, g_ffn=m_g_ffn, w_gate_up=m_w_gate_up, w_down=m_w_down, g_final=m_g_final)
    moms_v = dict(c_ctx=v_c_ctx, w_ada=v_w_ada, b_ada=v_b_ada, g_mix=v_g_mix, w_in=v_w_in, conv_w=v_conv_w,
                  conv_b=v_conv_b, dt_bias=v_dt_bias, a_log=v_a_log, d_skip=v_d_skip, ssd_norm_w=v_ssd_norm_w,
                  w_ssd_out=v_w_ssd_out, pool_w=v_pool_w, pool_scale=v_pool_scale, w_pool_out=v_w_pool_out,
                  w_out=v_w_out, g_ffn=v_g_ffn, w_gate_up=v_w_gate_up, w_down=v_w_down, g_final=v_g_final)
    order = ["c_ctx", "w_ada", "b_ada", "g_mix", "w_in", "conv_w", "conv_b", "dt_bias", "a_log", "d_skip",
             "ssd_norm_w", "w_ssd_out", "pool_w", "pool_scale", "w_pool_out", "w_out", "g_ffn", "w_gate_up",
             "w_down", "g_final"]
    big = ["w_ada", "w_in", "conv_w", "w_ssd_out", "pool_w", "w_pool_out", "w_out", "w_gate_up", "w_down"]
    small = [n for n in order if n not in big]

    depth = w_in.shape[0]
    L, D = x.shape[1], x.shape[2]
    n_ctx = ctx.shape[1]
    T = n_ctx + L
    in_cols = w_in.shape[2] * N_DEV
    xbc_w = conv_w.shape[2] * N_DEV
    dinner = ssd_norm_w.shape[1]
    H = dt_bias.shape[2]
    G = H // HPG
    GN = G * STATE
    assert xbc_w == dinner + 2 * GN and dinner == H * HEADDIM
    assert in_cols == dinner + xbc_w + 2 * H + D + 2 * D
    assert dinner == 2 * D and GN == D and 2 * H <= 128
    F = w_down.shape[1] * N_DEV
    pg = pool_w.shape[3]
    tr = n_ctx
    assert L % tr == 0 and tr % GRID_W == 0 and tr % CHUNK == 0 and L % CHUNK == 0
    n_ctx_tiles = 1
    tr_big = _tile(T, (1088, 544, 512, 256, 128))
    tr_mid = _tile(T, (544, 512, 256, 128))
    NP =_round_up(9 * D + 128, 512)
    gs = dinner // G
    off_xbc, off_dt, off_pool = dinner, dinner + xbc_w, dinner + xbc_w + 2 * H
    off_gate = off_pool + D

    def shard_rows(name, l):
        w = weights[name][l]
        if name in ("w_ada", "w_in", "w_gate_up"):
            return w.T
        if name == "conv_w":
            w8 = _pad_rows(w, 8)
            hi = w8.astype(BF16)
            lo = (w8 - hi.astype(F32)).astype(BF16)
            return jnp.concatenate([hi, lo], axis=0).reshape(-1, D)
        if name == "pool_w":
            return w.reshape(-1, D)
        return w

    first_needed = ["w_ada", "w_in"]
    shard = {(n, l): shard_rows(n, l) for l in range(depth) for n in big}
    gather_groups = [[first_needed, [n for n in big if n not in first_needed]] if l == 0 else [big]
                     for l in range(depth)]
    glays = [[_Layout(D, [(n, l, shard[(n, l)].shape[0]) for n in grp]) for grp in gather_groups[l]]
             for l in range(depth)]

    def packed(l, gi):
        lay = glays[l][gi]
        rows = jnp.concatenate([_pad_rows(shard[(n, l)].astype(BF16), lay.find(n, l)[3])
                                for n in gather_groups[l][gi]], axis=0)
        return _pad_rows(rows, lay.rows)

    def gather_exchange(l, gi):
        return _Exchange(packed(l, gi), jax.ShapeDtypeStruct((N_DEV, glays[l][gi].rows, D), BF16), _GATHER_SEMS,
                         _gather_start, _gather_finish)

    gathered = [[None] * len(g) for g in gather_groups]
    gathered[0][0] = _all_gather(packed(0, 0), "gather_weights")

    def full(name, l):
        gi = [name in grp for grp in gather_groups[l]].index(True)
        _, _, rows, _, off = glays[l][gi].find(name, l)
        return gathered[l][gi][:, off:off + rows, :]

    def w_inT_new(l):
        w = full("w_in", l).reshape(in_cols, D)
        parts = [w[:off_xbc], w[off_xbc:off_dt], w[off_pool:off_gate], w[off_gate:], w[off_dt:off_pool]]
        return _pad_rows(jnp.concatenate(parts, axis=0), NP)

    xs0 = jnp.concatenate([ctx[0], x[0]], axis=0)
    cc8 = _pad_rows(jnp.concatenate([c, c_ctx[None, :]], axis=0), 8)
    tgt = loss_target[0]

    def vec(a):
        return a.reshape(1, -1)

    def pad128(a):
        return jnp.pad(a.reshape(1, -1), ((0, 0), (0, 128 - 2 * H)))

    expand = (jnp.arange(128)[:, None] == jnp.arange(2 * H * HEADDIM)[None, :] // HEADDIM).astype(BF16)

    def from4(arr):
        return jnp.pad(arr.transpose(2, 0, 1, 3).reshape(T, 2 * H), ((0, 0), (0, 128 - 2 * H)))

    dt_blk = (9 * D) // 128
    conv_tc = 128
    ssd_gpb, ssd_gpb_bwd = 4, 2
    saved = []
    xcur = xs0
    for l in range(depth):
        W = dict(adaT=full("w_ada", l).reshape(6 * D, D), inT=w_inT_new(l))
        m6 = _ada_fwd(cc8, W["adaT"], vec(b_ada[l]), "ada_fwd")
        h = _norm_mod(xcur, vec(g_mix[l]), m6, 0, 1, n_ctx_tiles, tr, "norm_mod")
        if len(gather_groups[l]) > 1:
            proj, gathered[l][1] = _mm(h, W["inT"], "nt", F32, "mm_in_gather", exch=gather_exchange(l, 1))
        else:
            proj = _mm(h, W["inT"], "nt", F32, "mm_in")
        W.update(
            ssd=full("w_ssd_out", l).reshape(dinner, D), po=full("w_pool_out", l).reshape(D, D),
            out=full("w_out", l).reshape(D, D), guT=full("w_gate_up", l).reshape(2 * F, D),
            down=full("w_down", l).reshape(F, D),
            pool=full("pool_w", l).reshape(N_DEV, len(POOL_WINDOWS), pg // N_DEV, pg).transpose(1, 0, 2, 3)
            .reshape(len(POOL_WINDOWS), pg, pg),
        )
        cw = full("conv_w", l).reshape(N_DEV, 16, xbc_w // N_DEV).astype(F32)
        W["conv8"] = (cw[:, :8] + cw[:, 8:]).transpose(1, 0, 2).reshape(8, xbc_w)
        l1, l2, l3, dtb, ein, dte, etot = _dt_prep(proj, dt_blk, pad128(dt_bias[l]), pad128(a_log[l]), expand, H,
                                                   "dt_prep")
        L1, L2, L3 = (v[:, :2 * H].T.reshape(2, H, 1, T) for v in (l1, l2, l3))
        k16 = jnp.arange(16).reshape(1, 1, 16, 1)

        def rows16(at):
            terms = jnp.where(k16 == at, L1, jnp.where(k16 == at + 1, L2, L3))
            return jnp.where((k16 >= at) & (k16 < at + 3), terms, (k16 < 6).astype(BF16))

        lam_a, lam_b = rows16(0), -rows16(3) + 2 * (k16 < 3).astype(BF16)
        et = etot.reshape(T // CHUNK, 8, 128)[:, 0, :2 * H].reshape(T // CHUNK, 2, G, HPG).transpose(1, 2, 0, 3)
        etot5 = jnp.pad(jnp.broadcast_to(et[..., None], et.shape + (128,)),
                        ((0, 0), (0, 0), (0, 0), (0, 8 - HPG), (0, 0)))
        dsk = jnp.pad(jnp.repeat(d_skip[l], HEADDIM, axis=1)[:, None, :], ((0, 0), (0, 7), (0, 0)))
        scan_ops = (dtb, ein, dte, lam_a, lam_b, etot5)
        xbc = _conv_fwd(proj, dinner // conv_tc, xbc_w, W["conv8"], vec(conv_b[l]), n_ctx, conv_tc, "conv_fwd")
        if l + 1 < depth:
            y2, states, gathered[l + 1][0] = _ssd_fwd(xbc, *scan_ops, dsk, G, n_ctx, ssd_gpb, "ssd_fwd_gather",
                                                      gather_exchange(l + 1, 0))
        else:
            y2, states = _ssd_fwd(xbc, *scan_ops, dsk, G, n_ctx, ssd_gpb, "ssd_fwd")
        yn = _gnorm_fwd(y2, proj, vec(ssd_norm_w[l]), gs, tr_mid, "gnorm_fwd")
        pm = _pool_apply(proj, (6 * D) // pg, BF16, n_ctx, pg, False, "pool_fwd")
        pms = _pool_mix_fwd(pm, W["pool"], vec(pool_scale[l]), tr_big, "pool_mix_fwd")
        o_ssd = _mm(yn, W["ssd"], "nn", F32, "mm_ssd_out")
        o_pool = _mm(pms, W["po"], "nn", F32, "mm_pool_out")
        mg = _merge_fwd(proj, 7, o_ssd, o_pool, tr_mid, "merge_fwd")
        mo = _mm(mg, W["out"], "nn", F32, "mm_out")
        x1, h2 = _norm_mod(xcur, vec(g_ffn[l]), m6, 3, 4, n_ctx_tiles, tr, "resid_norm_mod", resid=(mo, 2))
        gu = _mm(h2, W["guT"], "nt", F32, "mm_gate_up")
        act = _swiglu_fwd(gu, tr_mid, "swiglu_fwd")
        f = _mm(act, W["down"], "nn", F32, "mm_down")
        saved.append(dict(W=W, m6=m6, x0=xcur, h=h, proj=proj, scan_ops=scan_ops, xbc=xbc, y2=y2,
                          states=states, yn=yn, pm=pm, pms=pms, o_ssd=o_ssd, o_pool=o_pool, mg=mg, mo=mo, x1=x1,
                          h2=h2, gu=gu, act=act, f=f))
        xcur = _resid(x1, f, m6, 5, n_ctx_tiles, tr, "resid")

    loss_blk, dx, dgf = _loss_head(xcur, tgt, vec(g_final), n_ctx_tiles, tr, "loss_head")
    loss = lax.psum(loss_blk[0, 0], MESH_AXES)

    big_rows = {}
    small_g = {n: [None] * depth for n in small}
    d_c_ctx = jnp.zeros((D,), F32)
    late = ["w_in", "w_ada", "conv_w"]
    rs_groups = [[[n for n in big if n not in late], late[:1], late[1:]] if l == 0 else [big] for l in range(depth)]
    rlays = [[_Layout(D, [(n, l, shard[(n, l)].shape[0]) for n in grp]) for grp in rs_groups[l]]
             for l in range(depth)]
    my_c = lax.axis_index("c").astype(jnp.int32).reshape(1)

    def reduce_pair(l, gi):
        lay = rlays[l][gi]
        gparts = [jnp.pad(big_rows[(n, l)].astype(BF16), ((0, 0), (0, pr - rows), (0, 0)))
                  for n, _, rows, pr, _ in lay.pieces]
        gparts.append(jnp.zeros((N_DEV, lay.rows - sum(p[3] for p in lay.pieces), D), BF16))
        gbuf = jnp.concatenate(gparts, axis=1)
        gbuf = gbuf.reshape(2, 2, 2, lay.rows, D).transpose(2, 0, 1, 3, 4).reshape(2, 4, lay.rows, D)
        return _pair_add(gbuf, _pair_exchange(gbuf, "rs_pair_exchange"), my_c, "rs_pair_add")

    slots = {}
    pending = None
    for l in reversed(range(depth)):
        S = saved[l]
        W, m6, proj = S["W"], S["m6"], S["proj"]
        df, dga2 = _resid_bwd(dx, S["f"], m6, 5, n_ctx_tiles, tr, "resid_bwd")
        dact = _mm(df, W["down"], "nt", F32, "mm_down_dx")
        g_down = _mm(S["act"], df, "tn", BF16,"mm_down_dw")
        dgu = _swiglu_bwd(S["gu"], dact, tr_mid, "swiglu_bwd")
        dh2 = _mm(dgu, W["guT"], "nn", F32, "mm_gate_up_dx")
        g_guT = _mm(dgu, S["h2"], "tn", BF16,"mm_gate_up_dw")
        dx1, st2 = _norm_mod_bwd(S["x1"], dh2, dx, vec(g_ffn[l]), m6, 4, n_ctx_tiles, tr, "norm_mod_bwd")
        dmo, dga1 = _resid_bwd(dx1, S["mo"], m6, 2, n_ctx_tiles, tr, "resid_bwd")
        dmg = _mm(dmo, W["out"], "nt", F32, "mm_out_dx")
        g_out = _mm(S["mg"], dmo, "tn", BF16,"mm_out_dw")
        do_ssd, do_pool, dgl = _merge_bwd(proj, 7, S["o_ssd"], S["o_pool"], dmg, tr_mid, "merge_bwd")
        dyn = _mm(do_ssd, W["ssd"], "nt", F32, "mm_ssd_out_dx")
        g_ssd = _mm(S["yn"], do_ssd, "tn", BF16,"mm_ssd_out_dw")
        dpms = _mm(do_pool, W["po"], "nt", F32, "mm_pool_out_dx")
        g_po = _mm(S["pms"], do_pool, "tn", BF16,"mm_pool_out_dw")
        dpm, g_pool, dps = _pool_mix_bwd(S["pm"], dpms, W["pool"], vec(pool_scale[l]), tr_big, "pool_mix_bwd")
        dup = _pool_apply(dpm, 0, BF16, n_ctx, pg, True, "pool_bwd")
        dy, dz, dnw = _gnorm_bwd(S["y2"], proj, vec(ssd_norm_w[l]), dyn, gs, tr, "gnorm_bwd")
        big_rows[("w_ssd_out", l)] = g_ssd.reshape(N_DEV, -1, D)
        big_rows[("pool_w", l)] = g_pool.reshape(len(POOL_WINDOWS), N_DEV, pg // N_DEV, pg).transpose(1, 0, 2, 3) \
            .reshape(N_DEV, -1, D)
        big_rows[("w_pool_out", l)] = g_po.reshape(N_DEV, -1, D)
        big_rows[("w_out", l)] = g_out.reshape(N_DEV, -1, D)
        big_rows[("w_gate_up", l)] = g_guT.reshape(N_DEV, -1, D)
        big_rows[("w_down", l)] = g_down.reshape(N_DEV, -1, D)
        riding = [] if pending is None else [pending]
        if len(rs_groups[l]) > 1:
            riding.append(((l, 0), reduce_pair(l, 0)))
        if not riding:
            dxs2, db2, dc2, ddt4, dlc4, dlr4 = _ssd_bwd(S["xbc"], dy, S["states"], *S["scan_ops"], G, n_ctx,
                                                        ssd_gpb_bwd, "ssd_bwd")
        else:
            red = jnp.concatenate([r for _, r in riding], axis=1)
            exch = _Exchange(red, jax.ShapeDtypeStruct(red.shape, BF16), _CHIP_SEMS, _chip_exchange_start,
                             _chip_exchange_finish)
            dxs2, db2, dc2, ddt4, dlc4, dlr4, got = _ssd_bwd(
                S["xbc"], dy, S["states"], *S["scan_ops"], G, n_ctx, ssd_gpb_bwd, "ssd_bwd_exchange", exch)
            at = 0
            for key, r in riding:
                slots[key] = got[:, at:at + r.shape[1]]
                at += r.shape[1]
        dskv = _pad_rows(jnp.repeat(d_skip[l], HEADDIM, axis=1), 8)
        cb = vec(conv_b[l])
        dxbc_x, dconv_x = _conv_bwd(proj, dinner // conv_tc, dxs2, 0, W["conv8"], cb, n_ctx, conv_tc, "conv_bwd_x",
                                    skip=(dy, dskv))
        dxbc_b, dconv_b = _conv_bwd(proj, 2 * dinner // conv_tc, db2, dinner // conv_tc, W["conv8"], cb, n_ctx,
                                    conv_tc, "conv_bwd_bc")
        dxbc_c, dconv_c = _conv_bwd(proj, (2 * dinner + GN) // conv_tc, dc2, (dinner + GN) // conv_tc, W["conv8"], cb,
                                    n_ctx, conv_tc, "conv_bwd_bc")
        dconv = jnp.concatenate([dconv_x, dconv_b, dconv_c], axis=1)
        ddt_raw, dtst = _dt_bwd(proj, dt_blk, pad128(dt_bias[l]), pad128(a_log[l]), from4(ddt4),
                                from4(dlc4 + dlr4.transpose(0, 1, 3, 2)), H, tr, "dt_bwd")
        dproj = jnp.concatenate([dz, dxbc_x, dxbc_b, dxbc_c, dup, dgl, ddt_raw,
                                 jnp.zeros((T, NP - 9 * D - 128), BF16)], axis=1)
        g_inT_new = _mm(dproj, S["h"], "tn", BF16,"mm_in_dw")
        g_inT = jnp.concatenate([g_inT_new[:6 * D], g_inT_new[9 * D:9 * D + 2 * H], g_inT_new[6 * D:9 * D]], axis=0)
        big_rows[("w_in", l)] = g_inT.reshape(N_DEV, -1, D)
        if len(rs_groups[l]) == 3:
            red = reduce_pair(l, 1)
            dh, slots[(l, 1)] = _mm(dproj, W["inT"], "nn", F32, "mm_in_dx_exchange", exch=_Exchange(
                red, jax.ShapeDtypeStruct(red.shape, BF16), _CHIP_SEMS, _chip_exchange_start, _chip_exchange_finish))
        else:
            dh = _mm(dproj, W["inT"], "nn", F32, "mm_in_dx")
        dx0, st1 = _norm_mod_bwd(S["x0"], dh, dx1, vec(g_mix[l]), m6, 1, n_ctx_tiles, tr, "norm_mod_bwd")
        dm6 = _pad_rows(jnp.concatenate([st1[0:2], st1[2:4], dga1[0:2], st2[0:2], st2[2:4], dga2[0:2]], axis=1), 8)
        dsil = _mm(dm6, W["adaT"], "nn", F32, "mm_ada_dx")
        sil_b, dcc, dbada = _ada_bwd_small(cc8, dsil, dm6, "ada_bwd_small")
        g_adaT = _mm(dm6, sil_b, "tn", BF16,"mm_ada_dw")
        d_c_ctx = d_c_ctx + dcc[1]
        dx = dx0

        big_rows[("w_ada", l)] = g_adaT.reshape(N_DEV, -1, D)
        big_rows[("conv_w", l)] = jnp.pad(
            dconv[:CONV_K].reshape(CONV_K, N_DEV, xbc_w // N_DEV).transpose(1, 0, 2),
            ((0, 0), (0, 16 - CONV_K), (0, 0))).reshape(N_DEV, -1, D)
        small_g["b_ada"][l] = dbada[0]
        small_g["g_mix"][l] = st1[4]
        small_g["conv_b"][l] = dconv[CONV_K]
        small_g["dt_bias"][l] = dtst[0, :2 * H].reshape(2, H)
        small_g["a_log"][l] = dtst[1, :2 * H].reshape(2, H)
        dsk_h = dconv_x[CONV_K + 1].reshape(H, HEADDIM).sum(axis=-1)
        small_g["d_skip"][l] = jnp.stack([dsk_h, dsk_h])
        small_g["ssd_norm_w"][l] = dnw[0]
        small_g["pool_scale"][l] = dps[0]
        small_g["g_ffn"][l] = st2[4]

        last_gi = len(rs_groups[l]) - 1
        pending = ((l, last_gi), reduce_pair(l, last_gi))
    slots[pending[0]] = _chip_exchange(pending[1], "rs_chip_exchange")
    g_local = {key: _sum_slots(s, "rs_chip_add") for key, s in slots.items()}
    grad_x = dx[n_ctx:][None]

    def local_grad(name):
        outs = []
        for l in range(depth):
            gi = [name in grp for grp in rs_groups[l]].index(True)
            _, _, rows, _, off = rlays[l][gi].find(name, l)
            piece = g_local[(l, gi)][off:off + rows]
            if name in ("w_ada", "w_in", "w_gate_up"):
                piece = piece.T
            elif name == "conv_w":
                piece = piece.reshape(16, -1)[:CONV_K]
            elif name == "pool_w":
                piece = piece.reshape(weights[name].shape[1:])
            outs.append(piece)
        return jnp.stack(outs)

    grads = {n: local_grad(n) for n in big}

    small_full = {"c_ctx": d_c_ctx, "g_final": dgf[0]}
    for n in small:
        if n not in small_full:
            small_full[n] = jnp.stack(small_g[n])

    def pack_small(tree):
        flat = jnp.concatenate([tree[n].reshape(-1).astype(F32) for n in small])
        rows = _round_up(-(-flat.shape[0] // D), 8)
        return jnp.pad(flat, (0, rows * D - flat.shape[0])).reshape(rows, D)

    def unpack_small(buf):
        flat, out, off = buf.reshape(-1), {}, 0
        for n in small:
            sz = weights[n].size
            out[n] = flat[off:off + sz].reshape(weights[n].shape)
            off += sz
        return out

    g_small = _sum_slots(_all_gather(pack_small(small_full), "gather_small_grads"), "sum_small_grads")
    grads.update(unpack_small(g_small))

    delta, new_m, new_v = {}, {}, {}
    for n in big:
        shp = weights[n].shape
        d_, m_, v_ = _adamw(weights[n].reshape(-1, shp[-1]), grads[n].reshape(-1, shp[-1]),
                            moms_m[n].reshape(-1, shp[-1]), moms_v[n].reshape(-1, shp[-1]), "adamw_" + n)
        delta[n], new_m[n], new_v[n] = d_.reshape(shp), m_.reshape(shp), v_.reshape(shp)
    d_, m_, v_ = _adamw(pack_small(weights), g_small, pack_small(moms_m), pack_small(moms_v), "adamw_small")
    delta.update(unpack_small(d_))
    new_m.update(unpack_small(m_))
    new_v.update(unpack_small(v_))

    return (loss, grad_x, *[grads[n] for n in order], *[delta[n] for n in order],
            *[new_m[n] for n in order], *[new_v[n] for n in order])
```

```python
import functools

import jax
import jax.numpy as jnp
from jax import lax
from jax.experimental import pallas as pl
from jax.experimental.pallas import tpu as pltpu

F32 = jnp.float32
BF16 = jnp.bfloat16
N_DEV = 8
EPS = 1e-6
GRID_W = 64
POOL_WINDOWS = (2, 4, 8, 16)
HEADDIM = 64
STATE = 128
CHUNK = 128
HPG = 4
CONV_K = 5
ADAM_LR, ADAM_B1, ADAM_B2, ADAM_EPS, ADAM_WD, ADAM_STEP = 0.001, 0.9, 0.999, 1e-08, 0.01, 10
NEG_BIG = -1e30
MESH_AXES = ("x", "y", "c")
ANY = pl.BlockSpec(memory_space=pl.ANY)


def _tile(n, cands):
    for t in cands:
        if n % t == 0:
            return t
    return n


def _round_up(n, m):
    return -(-n // m) * m


def _silu(x):
    return x * jax.nn.sigmoid(x)


def _dsilu(x):
    s = jax.nn.sigmoid(x)
    return s * (1.0 + x * (1.0 - s))


def _cparams(sem):
    return pltpu.CompilerParams(dimension_semantics=sem, vmem_limit_bytes=56 * 1024 * 1024)


def _mm(a, b, mode, out_dtype, name, exch=None):
    if mode == "tn":
        K, M = a.shape
        N = b.shape[1]
        tm = _tile(M, (512, 256, 128))
        tn = _tile(N, (1024, 512, 256, 128))
        tk = K if K <= 4608 else _tile(K, (1088, 544, 512, 256, 128))
        a_spec = pl.BlockSpec((tk, tm), lambda i, j, k: (k, i))
        b_spec = pl.BlockSpec((tk, tn), lambda i, j, k: (k, j))
        dims = (((0,), (0,)), ((), ()))
    else:
        M, K = a.shape
        N = b.shape[0] if mode == "nt" else b.shape[1]
        tm = _tile(M, (1088, 544, 512, 256, 128))
        tk = K if K <= 4096 else max(t for t in range(128, 2817, 128) if K % t == 0)
        tn = _tile(N, (512, 256, 128)) if tk == K else _tile(N, (1024, 512, 256, 128))
        a_spec = pl.BlockSpec((tm, tk), lambda i, j, k: (i, k))
        if mode == "nt":
            b_spec = pl.BlockSpec((tn, tk), lambda i, j, k: (j, k))
            dims = (((1,), (1,)), ((), ()))
        else:
            b_spec = pl.BlockSpec((tk, tn), lambda i, j, k: (k, j))
            dims = (((1,), (0,)), ((), ()))
    nk = K // tk

    def body(a_ref, b_ref, o_ref, *acc):
        if nk == 1:
            o_ref[...] = lax.dot_general(a_ref[...].astype(BF16), b_ref[...].astype(BF16), dims,
                                         preferred_element_type=F32).astype(o_ref.dtype)
            return
        k = pl.program_id(2)

        @pl.when(k == 0)
        def _():
            acc[0][...] = jnp.zeros_like(acc[0])

        acc[0][...] += lax.dot_general(a_ref[...].astype(BF16), b_ref[...].astype(BF16), dims,
                                       preferred_element_type=F32)

        @pl.when(k == nk - 1)
        def _():
            o_ref[...] = acc[0][...].astype(o_ref.dtype)

    call = dict(name=name, grid=(M // tm, N // tn, nk), in_specs=[a_spec, b_spec],
                scratch_shapes=[pltpu.VMEM((tm, tn), F32)] if nk > 1 else [])
    o_spec, o_sds = pl.BlockSpec((tm, tn), lambda i, j, k: (i, j)), jax.ShapeDtypeStruct((M, N), out_dtype)
    if exch is not None:
        return _call_with_exchange(body, exch, out_specs=[o_spec], out_shape=[o_sds], operands=(a, b), **call)
    return pl.pallas_call(body, out_specs=o_spec, out_shape=o_sds,
                          compiler_params=_cparams(("parallel", "parallel", "arbitrary")), **call)(a, b)


def _ada_fwd(cc8, w_adaT, b_ada, name):
    D = cc8.shape[1]
    N = w_adaT.shape[0]
    tn = _tile(N, (512, 256, 128))

    def body(c_ref, w_ref, b_ref, o_ref):
        a = _silu(c_ref[...]).astype(BF16)
        o_ref[...] = lax.dot_general(a, w_ref[...], (((1,), (1,)), ((), ())),
                                     preferred_element_type=F32) + b_ref[...]

    return pl.pallas_call(
        body, name=name, grid=(N // tn,),
        in_specs=[pl.BlockSpec((8, D), lambda j: (0, 0)), pl.BlockSpec((tn, D), lambda j: (j, 0)),
                  pl.BlockSpec((1, tn), lambda j: (0, j))],
        out_specs=pl.BlockSpec((8, tn), lambda j: (0, j)),
        out_shape=jax.ShapeDtypeStruct((8, N), F32),
        compiler_params=_cparams(("parallel",)),
    )(cc8, w_adaT, b_ada)


def _ada_bwd_small(cc8, dsil, dm6, name):
    D = cc8.shape[1]
    N = dm6.shape[1]

    def body(c_ref, ds_ref, dm_ref, sil_ref, dc_ref, db_ref):
        c = c_ref[...]
        sil_ref[...] = _silu(c).astype(BF16)
        dc_ref[...] = ds_ref[...] * _dsilu(c)
        dm = dm_ref[...]
        row = lax.broadcasted_iota(jnp.int32, dm.shape, 0)
        db_ref[...] = jnp.where(row == 0, jnp.sum(dm, axis=0, keepdims=True), 0.0)

    return pl.pallas_call(
        body, name=name, grid=(1,),
        in_specs=[pl.BlockSpec((8, D), lambda i: (0, 0)), pl.BlockSpec((8, D), lambda i: (0, 0)),
                  pl.BlockSpec((8, N), lambda i: (0, 0))],
        out_specs=[pl.BlockSpec((8, D), lambda i: (0, 0)), pl.BlockSpec((8, D), lambda i: (0, 0)),
                   pl.BlockSpec((8, N), lambda i: (0, 0))],
        out_shape=[jax.ShapeDtypeStruct((8, D), BF16), jax.ShapeDtypeStruct((8, D), F32),
                   jax.ShapeDtypeStruct((8, N), F32)],
        compiler_params=_cparams(("arbitrary",)),
    )(cc8, dsil, dm6)


def _seg_pick(m_ref, is_ctx):
    return jnp.where(is_ctx, m_ref[1:2, :], m_ref[0:1, :])


def _norm_mod(x, g, m6, sh_idx, sc_idx, n_ctx_tiles, tr, name, resid=None):
    T, D = x.shape
    row = pl.BlockSpec((tr, D), lambda i: (i, 0))
    vec = pl.BlockSpec((1, D), lambda i: (0, 0))

    def mcol(idx):
        return pl.BlockSpec((8, D), lambda i: (0, idx))

    def body(*refs):
        if resid is None:
            x_ref, g_ref, sh_ref, sc_ref, h_ref = refs
            xv = x_ref[...]
        else:
            x_ref, f_ref, ga_ref, g_ref, sh_ref, sc_ref, xo_ref, h_ref = refs
        is_ctx = pl.program_id(0) < n_ctx_tiles
        if resid is not None:
            xv = x_ref[...] + _seg_pick(ga_ref, is_ctx) * f_ref[...]
            xo_ref[...] = xv
        rstd = lax.rsqrt(jnp.mean(xv * xv, axis=-1, keepdims=True) + EPS)
        hn = xv * rstd * g_ref[...]
        h_ref[...] = (hn * (1.0 + _seg_pick(sc_ref, is_ctx)) + _seg_pick(sh_ref, is_ctx)).astype(BF16)

    if resid is None:
        ins, in_specs = [x, g, m6, m6], [row, vec, mcol(sh_idx), mcol(sc_idx)]
        out_specs, out_shape = row, jax.ShapeDtypeStruct((T, D), BF16)
    else:
        f, ga_idx = resid
        ins = [x, f, m6, g, m6, m6]
        in_specs = [row, row, mcol(ga_idx), vec, mcol(sh_idx), mcol(sc_idx)]
        out_specs = [row, row]
        out_shape = [jax.ShapeDtypeStruct((T, D), F32), jax.ShapeDtypeStruct((T, D), BF16)]
    return pl.pallas_call(body, name=name, grid=(T // tr,), in_specs=in_specs, out_specs=out_specs,
                          out_shape=out_shape, compiler_params=_cparams(("parallel",)))(*ins)


def _resid(x, f, m6, ga_idx, n_ctx_tiles, tr, name):
    T, D = x.shape
    row = pl.BlockSpec((tr, D), lambda i: (i, 0))

    def body(x_ref, f_ref, ga_ref, o_ref):
        is_ctx = pl.program_id(0) < n_ctx_tiles
        o_ref[...] = x_ref[...] + _seg_pick(ga_ref, is_ctx) * f_ref[...]

    return pl.pallas_call(body, name=name, grid=(T // tr,),
                          in_specs=[row, row, pl.BlockSpec((8, D), lambda i: (0, ga_idx))], out_specs=row,
                          out_shape=jax.ShapeDtypeStruct((T, D), F32),
                          compiler_params=_cparams(("parallel",)))(x, f, m6)


def _resid_bwd(dx, f, m6, ga_idx, n_ctx_tiles, tr, name):
    T, D = dx.shape
    row = pl.BlockSpec((tr, D), lambda i: (i, 0))
    acc = pl.BlockSpec((8, D), lambda i: (0, 0))

    def body(dx_ref, f_ref, ga_ref, df_ref, dga_ref):
        i = pl.program_id(0)
        is_ctx = i < n_ctx_tiles

        @pl.when(i == 0)
        def _():
            dga_ref[...] = jnp.zeros_like(dga_ref)

        dxv = dx_ref[...]
        df_ref[...] = (_seg_pick(ga_ref, is_ctx) * dxv).astype(BF16)
        s = jnp.sum(dxv * f_ref[...], axis=0, keepdims=True)
        r = lax.broadcasted_iota(jnp.int32, (8, D), 0)
        dga_ref[...] += jnp.where(r == jnp.where(is_ctx, 1, 0), s, 0.0)

    return pl.pallas_call(body, name=name, grid=(T // tr,),
                          in_specs=[row, row, pl.BlockSpec((8, D), lambda i: (0, ga_idx))],
                          out_specs=[row, acc],
                          out_shape=[jax.ShapeDtypeStruct((T, D), BF16), jax.ShapeDtypeStruct((8, D), F32)],
                          compiler_params=_cparams(("arbitrary",)))(dx, f, m6)


def _norm_mod_bwd(x, dh, dxres, g, m6, sc_idx, n_ctx_tiles, tr, name):
    T, D = x.shape
    row = pl.BlockSpec((tr, D), lambda i: (i, 0))
    acc = pl.BlockSpec((8, D), lambda i: (0, 0))

    def body(x_ref, dh_ref, dr_ref, g_ref, sc_ref, dx_ref, st_ref):
        i = pl.program_id(0)
        is_ctx = i < n_ctx_tiles

        @pl.when(i == 0)
        def _():
            st_ref[...] = jnp.zeros_like(st_ref)

        xv, dh_v, gv = x_ref[...], dh_ref[...], g_ref[...]
        sc1 = 1.0 + _seg_pick(sc_ref, is_ctx)
        rstd = lax.rsqrt(jnp.mean(xv * xv, axis=-1, keepdims=True) + EPS)
        xhat = xv * rstd
        dxhat = dh_v * sc1 * gv
        dx_ref[...] = dr_ref[...] + rstd * (dxhat - xhat * jnp.mean(dxhat * xhat, axis=-1, keepdims=True))
        dsh = jnp.sum(dh_v, axis=0, keepdims=True)
        dsc = jnp.sum(dh_v * xhat * gv, axis=0, keepdims=True)
        dg = jnp.sum(dh_v * sc1 * xhat, axis=0, keepdims=True)
        r = lax.broadcasted_iota(jnp.int32, (8, D), 0)
        seg = jnp.where(is_ctx, 1, 0)
        st_ref[...] += (jnp.where(r == seg, dsh, 0.0) + jnp.where(r == 2 + seg, dsc, 0.0)
                        + jnp.where(r == 4, dg, 0.0))

    return pl.pallas_call(body, name=name, grid=(T // tr,),
                          in_specs=[row, row, row, pl.BlockSpec((1, D), lambda i: (0, 0)),
                                    pl.BlockSpec((8, D), lambda i: (0, sc_idx))],
                          out_specs=[row, acc],
                          out_shape=[jax.ShapeDtypeStruct((T, D), F32), jax.ShapeDtypeStruct((8, D), F32)],
                          compiler_params=_cparams(("arbitrary",)))(x, dh, dxres, g, m6)


def _loss_head(x, tgt, g, n_ctx_tiles, tr, name):
    T, D = x.shape
    row = pl.BlockSpec((tr, D), lambda i: (i, 0))

    def body(x_ref, t_ref, g_ref, l_ref, dx_ref, dg_ref):
        i = pl.program_id(0)

        @pl.when(i == 0)
        def _():
            l_ref[...] = jnp.zeros_like(l_ref)
            dg_ref[...] = jnp.zeros_like(dg_ref)

        @pl.when(i < n_ctx_tiles)
        def _():
            dx_ref[...] = jnp.zeros_like(dx_ref)

        @pl.when(i >= n_ctx_tiles)
        def _():
            xv, gv = x_ref[...], g_ref[...]
            rstd = lax.rsqrt(jnp.mean(xv * xv, axis=-1, keepdims=True) + EPS)
            xhat = xv * rstd
            e = xhat * gv - t_ref[...]
            l_ref[...] += 0.5 * jnp.sum(jnp.mean(e * e, axis=-1, keepdims=True), axis=0, keepdims=True)
            dy = e * (1.0 / D)
            dxhat = dy * gv
            dx_ref[...] = rstd * (dxhat - xhat * jnp.mean(dxhat * xhat, axis=-1, keepdims=True))
            r = lax.broadcasted_iota(jnp.int32, (8, D), 0)
            dg_ref[...] += jnp.where(r == 0, jnp.sum(dy * xhat, axis=0, keepdims=True), 0.0)

    return pl.pallas_call(
        body, name=name, grid=(T // tr,),
        in_specs=[row, pl.BlockSpec((tr, D), lambda i: (jnp.maximum(i - n_ctx_tiles, 0), 0)),
                  pl.BlockSpec((1, D), lambda i: (0, 0))],
        out_specs=[pl.BlockSpec((8, 128), lambda i: (0, 0)), row, pl.BlockSpec((8, D), lambda i: (0, 0))],
        out_shape=[jax.ShapeDtypeStruct((8, 128), F32), jax.ShapeDtypeStruct((T, D), F32),
                   jax.ShapeDtypeStruct((8, D), F32)],
        compiler_params=_cparams(("arbitrary",)))(x, tgt, g)


def _seq_masks(T, n_ctx, width):
    row = lax.broadcasted_iota(jnp.int32, (T, width), 0)
    in_ctx = row < n_ctx
    return jnp.where(in_ctx, row, row - n_ctx), jnp.where(in_ctx, n_ctx, T - n_ctx)


def _shift_rows(u, off, t_loc, seg_len):
    T = u.shape[0]
    if off == 0:
        return u
    v = pltpu.roll(u, (-off) % T, 0)
    ok = (t_loc + off >= 0) & (t_loc + off < seg_len)
    return jnp.where(ok, v, 0.0)


def _conv_fwd(proj, col0_blk, ncol, conv_w8, conv_b, n_ctx, tc, name):
    T = proj.shape[0]

    def body(u_ref, w_ref, b_ref, o_ref):
        u = u_ref[...]
        t_loc, seg_len = _seq_masks(T, n_ctx, tc)
        acc = jnp.broadcast_to(b_ref[...], u.shape)
        for i in range(CONV_K):
            acc = acc + w_ref[i:i + 1, :] * _shift_rows(u, i - CONV_K // 2, t_loc, seg_len)
        o_ref[...] = _silu(acc)

    return pl.pallas_call(
        body, name=name, grid=(ncol // tc,),
        in_specs=[pl.BlockSpec((T, tc), lambda j: (0, col0_blk + j)), pl.BlockSpec((8, tc), lambda j: (0, j)),
                  pl.BlockSpec((1, tc), lambda j: (0, j))],
        out_specs=pl.BlockSpec((T, tc), lambda j: (0, j)),
        out_shape=jax.ShapeDtypeStruct((T, ncol), F32),
        compiler_params=_cparams(("parallel",)))(proj, conv_w8, conv_b)


def _conv_bwd(proj, col0_blk, d2, w_blk0, conv_w8, conv_b, n_ctx, tc, name, skip=None):
    T, ncol = d2.shape[1], d2.shape[2]

    def body(u_ref, d_ref, w_ref, b_ref, *rest):
        if skip is None:
            du_ref, dw_ref = rest
        else:
            dy_ref, k_ref, du_ref, dw_ref = rest
        u = u_ref[...]
        t_loc, seg_len = _seq_masks(T, n_ctx, tc)
        pre = jnp.broadcast_to(b_ref[...], u.shape)
        for i in range(CONV_K):
            pre = pre + w_ref[i:i + 1, :] * _shift_rows(u, i - CONV_K // 2, t_loc, seg_len)
        r = lax.broadcasted_iota(jnp.int32, (8, tc), 0)
        dact = d_ref[0] + d_ref[1]
        dw = jnp.zeros((8, tc), F32)
        if skip is not None:
            dyv = dy_ref[...]
            dact = dact + (k_ref[0:1, :] + k_ref[1:2, :]) * dyv
            dw = jnp.where(r == CONV_K + 1, jnp.sum(dyv * _silu(pre), axis=0, keepdims=True), 0.0)
        dpre = dact * _dsilu(pre)
        du = jnp.zeros_like(u)
        dw = dw + jnp.where(r == CONV_K, jnp.sum(dpre, axis=0, keepdims=True), 0.0)
        for i in range(CONV_K):
            off = i - CONV_K // 2
            du = du + w_ref[i:i + 1, :] * _shift_rows(dpre, -off, t_loc, seg_len)
            dw = dw + jnp.where(r == i, jnp.sum(dpre * _shift_rows(u, off, t_loc, seg_len), axis=0, keepdims=True),
                                0.0)
        du_ref[...] = du.astype(BF16)
        dw_ref[...] = dw

    col = pl.BlockSpec((T, tc), lambda j: (0, j))
    in_specs = [pl.BlockSpec((T, tc), lambda j: (0, col0_blk + j)), pl.BlockSpec((2, T, tc), lambda j: (0, 0, j)),
                pl.BlockSpec((8, tc), lambda j: (0, w_blk0 + j)), pl.BlockSpec((1, tc), lambda j: (0, w_blk0 + j))]
    operands = [proj, d2, conv_w8, conv_b]
    if skip is not None:
        in_specs += [col, pl.BlockSpec((8, tc), lambda j: (0, j))]
        operands += list(skip)
    return pl.pallas_call(
        body, name=name, grid=(ncol // tc,), in_specs=in_specs,
        out_specs=[col, pl.BlockSpec((8, tc), lambda j: (0, j))],
        out_shape=[jax.ShapeDtypeStruct((T, ncol), BF16), jax.ShapeDtypeStruct((8, ncol), F32)],
        compiler_params=_cparams(("parallel",)))(*operands)


def _pool_core(u, half, t_loc, seg_len, transpose):
    tr = u.shape[0]

    def shift(v, s):
        w = pltpu.roll(v, s % tr, 0)
        ok = (t_loc - s >= 0) & (t_loc - s < seg_len)
        return jnp.where(ok, w, 0.0)

    cnt = (jnp.minimum(t_loc, half) + jnp.minimum(seg_len - t_loc, half)).astype(F32)
    q = u / cnt if transpose else u
    back, ahead, h = q, q, 1
    while h < half:
        back = back + shift(back, h)
        ahead = ahead + shift(ahead, -h)
        h *= 2
    if transpose:
        tot = back + shift(ahead, -1)
        return tot - u
    tot = shift(back, 1) + ahead
    return tot / cnt - u


def _pool_apply(src, col0_blk, out_dtype, n_ctx, pg, transpose, name):
    T = src.shape[0]

    def body(u_ref, o_ref):
        gi = pl.program_id(0)
        row = lax.broadcasted_iota(jnp.int32, (T, pg), 0)
        seg_len = jnp.where(row < n_ctx, n_ctx, GRID_W)
        t_loc = row & (seg_len - 1)
        u = u_ref[...].astype(F32)
        for k_idx, k in enumerate(POOL_WINDOWS):
            @pl.when(gi == k_idx)
            def _(k=k):
                o_ref[...] = _pool_core(u, k // 2, t_loc, seg_len, transpose).astype(o_ref.dtype)

    return pl.pallas_call(
        body, name=name, grid=(len(POOL_WINDOWS),),
        in_specs=[pl.BlockSpec((T, pg), lambda gi: (0, col0_blk + gi))],
        out_specs=pl.BlockSpec((T, pg), lambda gi: (0, gi)),
        out_shape=jax.ShapeDtypeStruct((T, pg * len(POOL_WINDOWS)), out_dtype),
        compiler_params=_cparams(("parallel",)))(src)


def _pool_mix_fwd(pm, pool_w, pool_scale, tr, name):
    T, W = pm.shape
    ng, pg = pool_w.shape[0], pool_w.shape[1]

    def body(p_ref, w_ref, s_ref, o_ref):
        o_ref[...] = (jnp.dot(p_ref[...], w_ref[...], preferred_element_type=F32) * s_ref[...]).astype(BF16)

    return pl.pallas_call(
        body, name=name, grid=(T // tr, ng),
        in_specs=[pl.BlockSpec((tr, pg), lambda i, g: (i, g)), pl.BlockSpec((None, pg, pg), lambda i, g: (g, 0, 0)),
                  pl.BlockSpec((1, pg), lambda i, g: (0, g))],
        out_specs=pl.BlockSpec((tr, pg), lambda i, g: (i, g)),
        out_shape=jax.ShapeDtypeStruct((T, W), BF16),
        compiler_params=_cparams(("parallel", "parallel")))(pm, pool_w, pool_scale)


def _pool_mix_bwd(pm, dpms, pool_w, pool_scale, tr, name):
    T, W = pm.shape
    ng, pg = pool_w.shape[0], pool_w.shape[1]

    def body(p_ref, d_ref, w_ref, s_ref, dp_ref, dw_ref, ds_ref):
        i = pl.program_id(1)

        @pl.when(i == 0)
        def _():
            dw_ref[...] = jnp.zeros_like(dw_ref)
            ds_ref[...] = jnp.zeros_like(ds_ref)

        p, w = p_ref[...], w_ref[...]
        d = d_ref[...].astype(F32)
        pmix = jnp.dot(p, w, preferred_element_type=F32)
        r = lax.broadcasted_iota(jnp.int32, (8, pg), 0)
        ds_ref[...] += jnp.where(r == 0, jnp.sum(d * pmix, axis=0, keepdims=True), 0.0)
        dmix = (d * s_ref[...]).astype(BF16)
        dp_ref[...] = lax.dot_general(dmix, w, (((1,), (1,)), ((), ())), preferred_element_type=F32)
        dw_ref[...] += lax.dot_general(p, dmix, (((0,), (0,)), ((), ())), preferred_element_type=F32)

    return pl.pallas_call(
        body, name=name, grid=(ng, T // tr),
        in_specs=[pl.BlockSpec((tr, pg), lambda g, i: (i, g)), pl.BlockSpec((tr, pg), lambda g, i: (i, g)),
                  pl.BlockSpec((None, pg, pg), lambda g, i: (g, 0, 0)), pl.BlockSpec((1, pg), lambda g, i: (0, g))],
        out_specs=[pl.BlockSpec((tr, pg), lambda g, i: (i, g)), pl.BlockSpec((None, pg, pg), lambda g, i: (g, 0, 0)),
                   pl.BlockSpec((8, pg), lambda g, i: (0, g))],
        out_shape=[jax.ShapeDtypeStruct((T, W), F32), jax.ShapeDtypeStruct((ng, pg, pg), F32),
                   jax.ShapeDtypeStruct((8, W), F32)],
        compiler_params=_cparams(("parallel", "arbitrary")))(pm, dpms, pool_w, pool_scale)


def _chunk_cumsum(v, upper):
    Q = v.shape[0]
    ii = lax.broadcasted_iota(jnp.int32, (Q, Q), 0)
    jj = lax.broadcasted_iota(jnp.int32, (Q, Q), 1)
    tri = ((jj >= ii) if upper else (jj <= ii)).astype(BF16)
    h1 = v.astype(BF16)
    r1 = v - h1.astype(F32)
    h2 = r1.astype(BF16)
    h3 = (r1 - h2.astype(F32)).astype(BF16)
    return (jnp.dot(tri, h1, preferred_element_type=F32) + jnp.dot(tri, h2, preferred_element_type=F32)
            + jnp.dot(tri, h3, preferred_element_type=F32))


def _split3(v):
    h1 = v.astype(BF16)
    r1 = v - h1.astype(F32)
    h2 = r1.astype(BF16)
    return h1, h2, (r1 - h2.astype(F32)).astype(BF16)


def _dt_prep(proj, dt_blk, bias, a_log, expand, n_heads, name):
    T = proj.shape[0]
    Wd = expand.shape[1]
    Q = CHUNK
    cps = 2 if (T // Q) % 2 == 0 else 1
    row = pl.BlockSpec((cps * Q, 128), lambda i: (i, 0))
    wide = pl.BlockSpec((cps * Q, Wd), lambda i: (i, 0))

    def body(r_ref, b_ref, al_ref, e_ref, l1_ref, l2_ref, l3_ref, dtb_ref, ein_ref, dte_ref, etot_ref):
        xv = r_ref[...] + b_ref[...]
        dt = jnp.maximum(xv, 0.0) + jnp.log(1.0 + jnp.exp(-jnp.abs(xv)))
        a_all = -jnp.exp(al_ref[...]) * dt
        fwd_col = lax.broadcasted_iota(jnp.int32, (Q, 128), 1) < n_heads
        lams, rests = [], []
        for c in range(cps):
            a = a_all[c * Q:(c + 1) * Q]
            lam_c = jnp.where(fwd_col, _chunk_cumsum(a, False), _chunk_cumsum(a, True))
            tot_c = jnp.where(fwd_col[0:1], lam_c[Q - 1:Q], lam_c[0:1])
            etot_ref[8 * c:8 * (c + 1), :] = jnp.broadcast_to(jnp.exp(tot_c), (8, 128))
            lams.append(lam_c)
            rests.append(tot_c - lam_c)
        lam = lams[0] if cps == 1 else jnp.concatenate(lams, axis=0)
        rest = rests[0] if cps == 1 else jnp.concatenate(rests, axis=0)
        l1_ref[...], l2_ref[...], l3_ref[...] = _split3(lam)
        ex = e_ref[...]

        def rep(v):
            p1, p2, p3 = _split3(v)
            return (jnp.dot(p1, ex, preferred_element_type=F32) + jnp.dot(p2, ex, preferred_element_type=F32)
                    + jnp.dot(p3, ex, preferred_element_type=F32))

        dtb_ref[...] = rep(dt)
        ein_ref[...] = rep(jnp.exp(lam))
        dte_ref[...] = rep(jnp.exp(rest))

    vec = pl.BlockSpec((1, 128), lambda i: (0, 0))
    return pl.pallas_call(
        body, name=name, grid=(T // (cps * Q),),
        in_specs=[pl.BlockSpec((cps * Q, 128), lambda i: (i, dt_blk)), vec, vec,
                  pl.BlockSpec((128, Wd), lambda i: (0, 0))],
        out_specs=[row, row, row, wide, wide, wide, pl.BlockSpec((8 * cps, 128), lambda i: (i, 0))],
        out_shape=[jax.ShapeDtypeStruct((T, 128), BF16)] * 3 + [jax.ShapeDtypeStruct((T, Wd), F32)] * 3
        + [jax.ShapeDtypeStruct((T // Q * 8, 128), F32)],
        compiler_params=_cparams(("parallel",)))(proj, bias, a_log, expand)


def _dt_bwd(proj, dt_blk, bias, a_log, ddt, dlam, n_heads, tr, name):
    T = proj.shape[0]
    row = pl.BlockSpec((tr, 128), lambda i: (i, 0))
    vec = pl.BlockSpec((1, 128), lambda i: (0, 0))

    def body(r_ref, b_ref, al_ref, ddt_ref, dl_ref, o_ref, st_ref):
        @pl.when(pl.program_id(0) == 0)
        def _():
            st_ref[...] = jnp.zeros_like(st_ref)

        xv = r_ref[...] + b_ref[...]
        dt = jnp.maximum(xv, 0.0) + jnp.log(1.0 + jnp.exp(-jnp.abs(xv)))
        a_neg = -jnp.exp(al_ref[...])
        col = lax.broadcasted_iota(jnp.int32, (CHUNK, 128), 1)
        dl = dl_ref[...]
        parts = []
        for k in range(tr // CHUNK):
            dk = dl[k * CHUNK:(k + 1) * CHUNK]
            parts.append(jnp.where(col < n_heads, _chunk_cumsum(dk, True), _chunk_cumsum(dk, False)))
        dav = jnp.concatenate(parts, axis=0)
        draw = (ddt_ref[...] + dav * a_neg) * jax.nn.sigmoid(xv)
        o_ref[...] = draw.astype(BF16)
        r = lax.broadcasted_iota(jnp.int32, (8, 128), 0)
        st_ref[...] += (jnp.where(r == 0, jnp.sum(draw, axis=0, keepdims=True), 0.0)
                        + jnp.where(r == 1, jnp.sum(dav * dt, axis=0, keepdims=True) * a_neg, 0.0))

    return pl.pallas_call(
        body, name=name, grid=(T // tr,),
        in_specs=[pl.BlockSpec((tr, 128), lambda i: (i, dt_blk)), vec, vec, row, row],
        out_specs=[row, pl.BlockSpec((8, 128), lambda i: (0, 0))],
        out_shape=[jax.ShapeDtypeStruct((T, 128), BF16), jax.ShapeDtypeStruct((8, 128), F32)],
        compiler_params=_cparams(("arbitrary",)))(proj, bias, a_log, ddt, dlam)


def _scan_chunk(d, pos, nc_ctx, nc):
    rev = jnp.where(pos < nc_ctx, nc_ctx - 1 - pos, nc - 1 - (pos - nc_ctx))
    return jnp.where(d == 0, pos, rev)


def _chunk_mask(d):
    ii = lax.broadcasted_iota(jnp.int32, (CHUNK, CHUNK), 0)
    jj = lax.broadcasted_iota(jnp.int32, (CHUNK, CHUNK), 1)
    return (ii - jj) * jnp.where(d == 0, 1, -1) >= 0


def _ssd_specs(T, G, n_ctx, gpb, chunk_of):
    R, P, N, Q = HPG, HEADDIM, STATE, CHUNK
    H = G * R
    nc, nc_ctx = T // Q, n_ctx // Q
    xw, bw = gpb * R * P, gpb * N
    b_blk0 = (H * P) // bw
    c_blk0 = b_blk0 + G // gpb

    def ch(d, s):
        return chunk_of(d, s, nc_ctx, nc)

    return dict(
        x=pl.BlockSpec((Q, xw), lambda d, g, s: (ch(d, s), g)),
        b=pl.BlockSpec((Q, bw), lambda d, g, s: (ch(d, s), b_blk0 + g)),
        c=pl.BlockSpec((Q, bw), lambda d, g, s: (ch(d, s), c_blk0 + g)),
        col=pl.BlockSpec((None, gpb, Q, R), lambda d, g, s: (d, g, ch(d, s), 0)),
        row=pl.BlockSpec((None, gpb, R, Q), lambda d, g, s: (d, g, 0, ch(d, s))),
        rep=pl.BlockSpec((Q, xw), lambda d, g, s: (ch(d, s), d * (G // gpb) + g)),
        lam_a=pl.BlockSpec((None, gpb * R, 16, Q), lambda d, g, s: (d, g, 0, ch(d, s))),
        lam_b=pl.BlockSpec((None, gpb * R, 16, Q), lambda d, g, s: (d, g, 0, ch(d, s))),
        etot=pl.BlockSpec((None, gpb, None, 8, 128), lambda d, g, s: (d, g, ch(d, s), 0, 0)),
        dsk=pl.BlockSpec((None, 8, xw), lambda d, g, s: (d, 0, g)),
        xd=pl.BlockSpec((None, Q, xw), lambda d, g, s: (d, ch(d, s), g)),
        bd=pl.BlockSpec((None, Q, bw), lambda d, g, s: (d, ch(d, s), g)),
        st=pl.BlockSpec((None, None, gpb * R // 2, 2 * P, N), lambda d, g, s: (d, ch(d, s), g, 0, 0)),
    )


class _Exchange:
    def __init__(self, operand, out_sds, sems, start, finish):
        self.operand, self.out_sds, self.sems, self.start, self.finish = operand, out_sds, sems, start, finish


def _call_with_exchange(body, exch, *, name, grid, in_specs, out_specs, out_shape, scratch_shapes, operands):
    if exch is None:
        return pl.pallas_call(body, name=name, grid=grid, in_specs=in_specs, out_specs=out_specs,
                              out_shape=out_shape, scratch_shapes=scratch_shapes,
                              compiler_params=_cparams(("arbitrary",) * len(grid)))(*operands)
    n_in, n_out, n_scr = len(in_specs), len(out_specs), len(scratch_shapes)

    def fused(*refs):
        ins, c_in = refs[:n_in], refs[n_in]
        outs, c_out = refs[n_in + 1:n_in + 1 + n_out], refs[n_in + 1 + n_out]
        scr = refs[n_in + 2 + n_out:n_in + 2 + n_out + n_scr]
        sems = refs[n_in + 2 + n_out + n_scr:]
        ids = [pl.program_id(a) for a in range(len(grid))]
        first = functools.reduce(lambda p, q: p & q, [i == 0 for i in ids])
        last = functools.reduce(lambda p, q: p & q, [i == n - 1 for i, n in zip(ids, grid)])

        @pl.when(first)
        def _():
            exch.start(c_in, c_out, *sems)

        body(*ins, *outs, *scr)

        @pl.when(last)
        def _():
            exch.finish(c_in, c_out, *sems)

    return pl.pallas_call(fused, name=name, grid=grid, in_specs=list(in_specs) + [ANY],
                          out_specs=list(out_specs) + [ANY], out_shape=list(out_shape) + [exch.out_sds],
                          scratch_shapes=list(scratch_shapes) + list(exch.sems),
                          compiler_params=_cparams(("arbitrary",) * len(grid)))(*operands, exch.operand)


def _ssd_fwd(xbc, dtb, ein, dte, lam_a, lam_b, etot, dsk, G, n_ctx, gpb, name, exch=None):
    T = xbc.shape[0]
    R, P, N, Q = HPG, HEADDIM, STATE, CHUNK
    H = G * R
    nc = T // Q
    sp = _ssd_specs(T, G, n_ctx, gpb, _scan_chunk)

    def body(x_ref, b_ref, c_ref, dt_ref, ein_ref, dte_ref, la_ref, lb_ref, et_ref, dsk_ref, y_ref, st_ref, S):
        d, s = pl.program_id(0), pl.program_id(2)

        @pl.when(s == 0)
        def _():
            S[...] = jnp.zeros_like(S)

        mask = _chunk_mask(d)
        head0 = lax.broadcasted_iota(jnp.int32, (Q, 2 * P), 1) < P
        rows0 = lax.broadcasted_iota(jnp.int32, (2 * P, N), 0) < P
        for gg in range(gpb):
            Bm = b_ref[:, gg * N:(gg + 1) * N].astype(BF16)
            Cm = c_ref[:, gg * N:(gg + 1) * N].astype(BF16)
            Gm = lax.dot_general(Cm, Bm, (((1,), (1,)), ((), ())), preferred_element_type=F32)
            for k in range(R // 2):
                pk = gg * (R // 2) + k
                sl = slice(pk * 2 * P, (pk + 1) * 2 * P)
                xp = x_ref[:, sl]
                xc = xp * dt_ref[:, sl]
                s_in = S[pk]
                y = lax.dot_general(Cm, s_in.astype(BF16), (((1,), (1,)), ((), ())),
                                    preferred_element_type=F32) * ein_ref[:, sl] + dsk_ref[0:1, sl] * xp
                for j in range(2):
                    hr = 2 * pk + j
                    diff = lax.dot_general(la_ref[hr], lb_ref[hr], (((0,), (0,)), ((), ())), preferred_element_type=F32)
                    ldec = jnp.exp(jnp.where(mask, diff, NEG_BIG))
                    xc_j = (jnp.where(head0, xc, 0.0) if j == 0 else jnp.where(head0, 0.0, xc)).astype(BF16)
                    y = y + jnp.dot((Gm * ldec).astype(BF16), xc_j, preferred_element_type=F32)
                y_ref[:, sl] = y
                st_ref[pk] = s_in
                e_all = jnp.where(rows0, et_ref[gg, 2 * k:2 * k + 1, :], et_ref[gg, 2 * k + 1:2 * k + 2, :])
                xd = (xc * dte_ref[:, sl]).astype(BF16)
                S[pk] = e_all * s_in + lax.dot_general(xd, Bm, (((0,), (0,)), ((), ())),
                                                       preferred_element_type=F32)

    return _call_with_exchange(
        body, exch, name=name, grid=(2, G // gpb, nc),
        in_specs=[sp["x"], sp["b"], sp["c"], sp["rep"], sp["rep"], sp["rep"], sp["lam_a"], sp["lam_b"], sp["etot"],
                  sp["dsk"]],
        out_specs=[sp["xd"], sp["st"]],
        out_shape=[jax.ShapeDtypeStruct((2, T, H * P), F32), jax.ShapeDtypeStruct((2, nc, H // 2, 2 * P, N), F32)],
        scratch_shapes=[pltpu.VMEM((gpb * R // 2, 2 * P, N), F32)],
        operands=(xbc, xbc, xbc, dtb, ein, dte, lam_a, lam_b, etot, dsk))


def _ssd_bwd(xbc, dy, states, dtb, ein, dte, lam_a, lam_b, etot, G, n_ctx, gpb, name, exch=None):
    T = xbc.shape[0]
    R, P, N, Q = HPG, HEADDIM, STATE, CHUNK
    H = G * R
    nc = T // Q
    sp = _ssd_specs(T, G, n_ctx, gpb, lambda d, s, nc_ctx, n: _scan_chunk(d, n - 1 - s, nc_ctx, n))

    def body(x_ref, b_ref, c_ref, dy_ref, st_ref, dt_ref, ein_ref, dte_ref, la_ref, lb_ref, et_ref,
             dx_ref, db_ref, dc_ref, ddt_ref, dlc_ref, dlr_ref, dS):
        d, s = pl.program_id(0), pl.program_id(2)

        @pl.when(s == 0)
        def _():
            dS[...] = jnp.zeros_like(dS)

        mask = _chunk_mask(d)
        ri = lax.broadcasted_iota(jnp.int32, (Q, 1), 0)
        is_last = ri == jnp.where(d == 0, Q - 1, 0)
        head0 = lax.broadcasted_iota(jnp.int32, (Q, 2 * P), 1) < P
        rows0 = lax.broadcasted_iota(jnp.int32, (2 * P, N), 0) < P

        def total(v):
            return jnp.sum(jnp.sum(v, axis=1, keepdims=True), axis=0, keepdims=True)

        for gg in range(gpb):
            Bm = b_ref[:, gg * N:(gg + 1) * N].astype(BF16)
            Cm = c_ref[:, gg * N:(gg + 1) * N].astype(BF16)
            Gm = lax.dot_general(Cm, Bm, (((1,), (1,)), ((), ())), preferred_element_type=F32)
            dG = jnp.zeros((Q, Q), F32)
            dB = jnp.zeros((Q, N), F32)
            dC = jnp.zeros((Q, N), F32)
            for k in range(R // 2):
                pk = gg * (R // 2) + k
                sl = slice(pk * 2 * P, (pk + 1) * 2 * P)
                e_in = ein_ref[:, sl]
                dte = dte_ref[:, sl]
                e_all = jnp.where(rows0, et_ref[gg, 2 * k:2 * k + 1, :], et_ref[gg, 2 * k + 1:2 * k + 2, :])
                xp = x_ref[:, sl]
                dtp = dt_ref[:, sl]
                xc = xp * dtp
                xc_b = xc.astype(BF16)
                dyp = dy_ref[:, sl]
                s_in = st_ref[pk]
                s_in_b = s_in.astype(BF16)
                ds_out = dS[pk]
                ds_out_b = ds_out.astype(BF16)
                y_int = lax.dot_general(Cm, s_in_b, (((1,), (1,)), ((), ())), preferred_element_type=F32) * e_in
                b_ds = lax.dot_general(Bm, ds_out_b, (((1,), (1,)), ((), ())), preferred_element_type=F32)
                dxc = dte * b_ds
                u = xc * dxc
                v = dyp * y_int - u
                sse = ds_out * s_in * e_all
                for j in range(2):
                    hr, r = 2 * pk + j, 2 * k + j

                    def pick(a, m0=head0, j=j):
                        return jnp.where(m0, a, 0.0) if j == 0 else jnp.where(m0, 0.0, a)

                    diff = lax.dot_general(la_ref[hr], lb_ref[hr], (((0,), (0,)), ((), ())), preferred_element_type=F32)
                    ldec = jnp.exp(jnp.where(mask, diff, NEG_BIG))
                    dy_j = pick(dyp).astype(BF16)
                    dM = lax.dot_general(dy_j, xc_b, (((1,), (1,)), ((), ())), preferred_element_type=F32)
                    dMl = dM * ldec
                    Wm = dMl * Gm
                    dlam_c = jnp.sum(Wm, axis=1, keepdims=True) + jnp.sum(pick(v), axis=1, keepdims=True)
                    last = total(pick(sse, rows0)) + total(pick(u))
                    dlc_ref[gg, :, r:r + 1] = dlam_c + jnp.where(is_last, last, 0.0)
                    dlr_ref[gg, r:r + 1, :] = -jnp.sum(Wm, axis=0, keepdims=True)
                    dxc = dxc + lax.dot_general((Gm * ldec).astype(BF16), dy_j, (((0,), (0,)), ((), ())),
                                                preferred_element_type=F32)
                    dG = dG + dMl
                dx_ref[:, sl] = dxc * dtp
                t = dxc * xp
                ddt_ref[gg, :, 2 * k:2 * k + 1] = jnp.sum(jnp.where(head0, t, 0.0), axis=1, keepdims=True)
                ddt_ref[gg, :, 2 * k + 1:2 * k + 2] = jnp.sum(jnp.where(head0, 0.0, t), axis=1, keepdims=True)
                edy_b = (e_in * dyp).astype(BF16)
                dC = dC + jnp.dot(edy_b, s_in_b, preferred_element_type=F32)
                dB = dB + jnp.dot((dte * xc).astype(BF16), ds_out_b, preferred_element_type=F32)
                dS[pk] = e_all * ds_out + lax.dot_general(edy_b, Cm, (((0,), (0,)), ((), ())),
                                                          preferred_element_type=F32)
            dG_b = dG.astype(BF16)
            dc_ref[:, gg * N:(gg + 1) * N] = dC + jnp.dot(dG_b, Bm, preferred_element_type=F32)
            db_ref[:, gg * N:(gg + 1) * N] = dB + lax.dot_general(dG_b, Cm, (((0,), (0,)), ((), ())),
                                                                  preferred_element_type=F32)

    return _call_with_exchange(
        body, exch, name=name, grid=(2, G // gpb, nc),
        in_specs=[sp["x"], sp["b"], sp["c"], sp["x"], sp["st"], sp["rep"], sp["rep"], sp["rep"], sp["lam_a"],
                  sp["lam_b"], sp["etot"]],
        out_specs=[sp["xd"], sp["bd"], sp["bd"], sp["col"], sp["col"], sp["row"]],
        out_shape=[jax.ShapeDtypeStruct((2, T, H * P), F32), jax.ShapeDtypeStruct((2, T, G * N), F32),
                   jax.ShapeDtypeStruct((2, T, G * N), F32), jax.ShapeDtypeStruct((2, G, T, R), F32),
                   jax.ShapeDtypeStruct((2, G, T, R), F32), jax.ShapeDtypeStruct((2, G, R, T), F32)],
        scratch_shapes=[pltpu.VMEM((gpb * R // 2, 2 * P, N), F32)],
        operands=(xbc, xbc, xbc, dy, states, dtb, ein, dte, lam_a, lam_b, etot))


def _gnorm_fwd(y2, proj, w, gs, tr, name):
    T, HP = y2.shape[1], y2.shape[2]

    def body(y_ref, z_ref, w_ref, o_ref):
        yz = (y_ref[0] + y_ref[1]) * _silu(z_ref[...])
        for g in range(HP // gs):
            v = yz[:, g * gs:(g + 1) * gs]
            rstd = lax.rsqrt(jnp.mean(v * v, axis=-1, keepdims=True) + EPS)
            o_ref[:, g * gs:(g + 1) * gs] = (v * rstd * w_ref[:, g * gs:(g + 1) * gs]).astype(BF16)

    return pl.pallas_call(
        body, name=name, grid=(T // tr,),
        in_specs=[pl.BlockSpec((2, tr, HP), lambda i: (0, i, 0)), pl.BlockSpec((tr, HP), lambda i: (i, 0)),
                  pl.BlockSpec((1, HP), lambda i: (0, 0))],
        out_specs=pl.BlockSpec((tr, HP), lambda i: (i, 0)),
        out_shape=jax.ShapeDtypeStruct((T, HP), BF16),
        compiler_params=_cparams(("parallel",)))(y2, proj, w)


def _gnorm_bwd(y2, proj, w, dyn, gs, tr, name):
    T, HP = y2.shape[1], y2.shape[2]
    row = pl.BlockSpec((tr, HP), lambda i: (i, 0))

    def body(y_ref, z_ref, w_ref, d_ref, dy_ref, dz_ref, dw_ref):
        @pl.when(pl.program_id(0) == 0)
        def _():
            dw_ref[...] = jnp.zeros_like(dw_ref)

        yv = y_ref[0] + y_ref[1]
        zv = z_ref[...]
        sz = _silu(zv)
        yz = yv * sz
        dv = d_ref[...]
        r = lax.broadcasted_iota(jnp.int32, (8, gs), 0)
        for g in range(HP // gs):
            sl = slice(g * gs, (g + 1) * gs)
            v = yz[:, sl]
            rstd = lax.rsqrt(jnp.mean(v * v, axis=-1, keepdims=True) + EPS)
            xhat = v * rstd
            dyn_g = dv[:, sl]
            dhat = dyn_g * w_ref[:, sl]
            dyz = rstd * (dhat - xhat * jnp.mean(dhat * xhat, axis=-1, keepdims=True))
            dy_ref[:, sl] = dyz * sz[:, sl]
            dz_ref[:, sl] = (dyz * yv[:, sl] * _dsilu(zv[:, sl])).astype(BF16)
            dw_ref[:, sl] += jnp.where(r == 0, jnp.sum(dyn_g * xhat, axis=0, keepdims=True), 0.0)

    return pl.pallas_call(
        body, name=name, grid=(T // tr,),
        in_specs=[pl.BlockSpec((2, tr, HP), lambda i: (0, i, 0)), row, pl.BlockSpec((1, HP), lambda i: (0, 0)), row],
        out_specs=[row, row, pl.BlockSpec((8, HP), lambda i: (0, 0))],
        out_shape=[jax.ShapeDtypeStruct((T, HP), F32), jax.ShapeDtypeStruct((T, HP), BF16),
                   jax.ShapeDtypeStruct((8, HP), F32)],
        compiler_params=_cparams(("arbitrary",)))(y2, proj, w, dyn)


def _merge_fwd(proj, g1_blk, o_ssd, o_pool, tr, name):
    T, D = o_ssd.shape
    row = pl.BlockSpec((tr, D), lambda i: (i, 0))

    def body(g1_ref, g2_ref, a_ref, b_ref, o_ref):
        o_ref[...] = (jax.nn.sigmoid(g1_ref[...]) * a_ref[...]
                      + jax.nn.sigmoid(g2_ref[...]) * b_ref[...]).astype(BF16)

    return pl.pallas_call(
        body, name=name, grid=(T // tr,),
        in_specs=[pl.BlockSpec((tr, D), lambda i: (i, g1_blk)), pl.BlockSpec((tr, D), lambda i: (i, g1_blk + 1)),
                  row, row],
        out_specs=row, out_shape=jax.ShapeDtypeStruct((T, D), BF16),
        compiler_params=_cparams(("parallel",)))(proj, proj, o_ssd, o_pool)


def _merge_bwd(proj, g1_blk, o_ssd, o_pool, dmg, tr, name):
    T, D = o_ssd.shape
    row = pl.BlockSpec((tr, D), lambda i: (i, 0))

    def body(g1_ref, g2_ref, a_ref, b_ref, d_ref, da_ref, db_ref, dg_ref):
        s1, s2 = jax.nn.sigmoid(g1_ref[...]), jax.nn.sigmoid(g2_ref[...])
        dv = d_ref[...]
        da_ref[...] = (s1 * dv).astype(BF16)
        db_ref[...] = (s2 * dv).astype(BF16)
        dg_ref[:, :D] = (dv * a_ref[...] * s1 * (1.0 - s1)).astype(BF16)
        dg_ref[:, D:] = (dv * b_ref[...] * s2 * (1.0 - s2)).astype(BF16)

    return pl.pallas_call(
        body, name=name, grid=(T // tr,),
        in_specs=[pl.BlockSpec((tr, D), lambda i: (i, g1_blk)), pl.BlockSpec((tr, D), lambda i: (i, g1_blk + 1)),
                  row, row, row],
        out_specs=[row, row, pl.BlockSpec((tr, 2 * D), lambda i: (i, 0))],
        out_shape=[jax.ShapeDtypeStruct((T, D), BF16), jax.ShapeDtypeStruct((T, D), BF16),
                   jax.ShapeDtypeStruct((T, 2 * D), BF16)],
        compiler_params=_cparams(("parallel",)))(proj, proj, o_ssd, o_pool, dmg)


def _swiglu_fwd(gu, tr, name):
    T, F2 = gu.shape
    F = F2 // 2
    tc = _tile(F, (1408, 768, 512, 256, 128))
    nb = F // tc

    def body(a_ref, b_ref, o_ref):
        o_ref[...] = (_silu(a_ref[...]) * b_ref[...]).astype(BF16)

    return pl.pallas_call(
        body, name=name, grid=(T // tr, nb),
        in_specs=[pl.BlockSpec((tr, tc), lambda i, j: (i, j)), pl.BlockSpec((tr, tc), lambda i, j: (i, nb + j))],
        out_specs=pl.BlockSpec((tr, tc), lambda i, j: (i, j)),
        out_shape=jax.ShapeDtypeStruct((T, F), BF16),
        compiler_params=_cparams(("parallel", "parallel")))(gu, gu)


def _swiglu_bwd(gu, dact, tr, name):
    T, F2 = gu.shape
    F = F2 // 2
    tc = _tile(F, (1408, 768, 512, 256, 128))
    nb = F // tc

    def body(a_ref, b_ref, d_ref, o_ref):
        is_a = pl.program_id(1) < nb
        av, bv, dv = a_ref[...], b_ref[...], d_ref[...]
        o_ref[...] = jnp.where(is_a, dv * bv * _dsilu(av), dv * _silu(av)).astype(BF16)

    return pl.pallas_call(
        body, name=name, grid=(T // tr, 2 * nb),
        in_specs=[pl.BlockSpec((tr, tc), lambda i, j: (i, j % nb)), pl.BlockSpec((tr, tc), lambda i, j: (i, nb + j % nb)),
                  pl.BlockSpec((tr, tc), lambda i, j: (i, j % nb))],
        out_specs=pl.BlockSpec((tr, tc), lambda i, j: (i, j)),
        out_shape=jax.ShapeDtypeStruct((T, F2), BF16),
        compiler_params=_cparams(("parallel", "parallel")))(gu, gu, dact)


def _adamw(w, g, m, v, name):
    Rr, C = w.shape
    tr = _tile(Rr, (256, 128, 64, 32, 16, 8))
    row = pl.BlockSpec((tr, C), lambda i: (i, 0))

    def body(w_ref, g_ref, m_ref, v_ref, d_ref, mo_ref, vo_ref):
        gv = g_ref[...]
        mn = ADAM_B1 * m_ref[...] + (1.0 - ADAM_B1) * gv
        vn = ADAM_B2 * v_ref[...] + (1.0 - ADAM_B2) * (gv * gv)
        m_hat = mn / (1.0 - ADAM_B1 ** ADAM_STEP)
        v_hat = vn / (1.0 - ADAM_B2 ** ADAM_STEP)
        d_ref[...] = -ADAM_LR * (m_hat / (jnp.sqrt(v_hat) + ADAM_EPS) + ADAM_WD * w_ref[...])
        mo_ref[...] = mn
        vo_ref[...] = vn

    sds = jax.ShapeDtypeStruct((Rr, C), F32)
    return pl.pallas_call(body, name=name, grid=(Rr // tr,), in_specs=[row] * 4, out_specs=[row] * 3,
                          out_shape=[sds] * 3, compiler_params=_cparams(("parallel",)))(w, g, m, v)


def _sum_slots(x, name):
    n, Rr, C = x.shape
    tr = _tile(Rr, (512, 256, 128, 64, 32, 16, 8))

    def body(x_ref, o_ref):
        acc = x_ref[0].astype(F32)
        for k in range(1, n):
            acc = acc + x_ref[k].astype(F32)
        o_ref[...] = acc

    return pl.pallas_call(body, name=name, grid=(Rr // tr,),
                          in_specs=[pl.BlockSpec((n, tr, C), lambda i: (0, i, 0))],
                          out_specs=pl.BlockSpec((tr, C), lambda i: (i, 0)),
                          out_shape=jax.ShapeDtypeStruct((Rr, C), F32),
                          compiler_params=_cparams(("parallel",)))(x)


def _place():
    return lax.axis_index("x"), lax.axis_index("y"), lax.axis_index("c")


def _all_gather(x, name):
    Rr, C = x.shape

    def body(x_ref, out_ref, send_sems, recv_sems, local_sem):
        _gather_start(x_ref, out_ref, send_sems, recv_sems, local_sem)
        _gather_finish(x_ref, out_ref, send_sems, recv_sems, local_sem)

    return pl.pallas_call(
        body, name=name, in_specs=[ANY], out_specs=ANY,
        out_shape=jax.ShapeDtypeStruct((N_DEV, Rr, C), x.dtype), scratch_shapes=_GATHER_SEMS,
    )(x)


_GATHER_SEMS = [pltpu.SemaphoreType.DMA((7,)), pltpu.SemaphoreType.DMA((7,)), pltpu.SemaphoreType.DMA]


def _gather_copies(x_ref, out_ref, send_sems, recv_sems, local_sem):
    mx, my, mc = _place()
    me, sibling = (mx, my, mc), (mx, my, 1 - mc)
    chips = [(1 - mx, my), (mx, 1 - my), (1 - mx, 1 - my)]

    def slot(px, py, pc):
        return out_ref.at[4 * px + 2 * py + pc]

    def copy(k, block, to, src=None):
        return pltpu.make_async_remote_copy(
            src_ref=slot(*block) if src is None else src, dst_ref=slot(*block),
            send_sem=send_sems.at[k], recv_sem=recv_sems.at[k],
            device_id=to, device_id_type=pl.DeviceIdType.MESH)

    return dict(
        mine=pltpu.make_async_copy(x_ref, slot(*me), local_sem),
        first=[copy(0, me, sibling, src=x_ref)] + [copy(1 + j, me, (*chip, mc), src=x_ref)
                                                   for j, chip in enumerate(chips)],
        passed=[copy(4 + j, (*chip, mc), sibling) for j, chip in enumerate(chips)],
        from_chips=[copy(1 + j, (*chip, mc), me) for j, chip in enumerate(chips)],
        from_sibling=[copy(0, sibling, me)] + [copy(4 + j, (*chip, 1 - mc), me) for j, chip in enumerate(chips)],
    )


def _gather_start(*refs):
    cps = _gather_copies(*refs)
    cps["mine"].start()
    for cp in cps["first"]:
        cp.start()


def _gather_finish(*refs):
    cps = _gather_copies(*refs)
    for j in range(3):
        cps["from_chips"][j].wait_recv()
        cps["passed"][j].start()
    for cp in cps["from_sibling"]:
        cp.wait_recv()
    for cp in cps["first"] + cps["passed"]:
        cp.wait_send()
    cps["mine"].wait()


def _pair_exchange(buf, name):
    _, n, Rr, C = buf.shape
    parts = 4
    pr = Rr // parts

    def body(b_ref, got_ref, send_sems, recv_sems):
        mx, my, mc = _place()
        copies = []
        for q in range(n):
            for p in range(parts):
                rows = pl.ds(p * pr, pr)
                cp = pltpu.make_async_remote_copy(
                    src_ref=b_ref.at[1 - mc, q, rows], dst_ref=got_ref.at[q, rows],
                    send_sem=send_sems.at[q * parts + p], recv_sem=recv_sems.at[q * parts + p],
                    device_id=(mx, my, 1 - mc), device_id_type=pl.DeviceIdType.MESH)
                cp.start()
                copies.append(cp)
        for cp in copies:
            cp.wait()

    return pl.pallas_call(
        body, name=name, in_specs=[ANY], out_specs=ANY,
        out_shape=jax.ShapeDtypeStruct((n, Rr, C), buf.dtype),
        scratch_shapes=[pltpu.SemaphoreType.DMA((n * parts,)), pltpu.SemaphoreType.DMA((n * parts,))],
    )(buf)


def _pair_add(buf, got, my_c, name):
    _, n, Rr, C = buf.shape
    tr = _tile(Rr, (512, 256, 128, 64, 32, 16))

    def body(c_ref, b_ref, g_ref, o_ref):
        o_ref[...] = (b_ref[...].astype(F32) + g_ref[...].astype(F32)).astype(BF16)

    return pl.pallas_call(
        body, name=name,
        grid_spec=pltpu.PrefetchScalarGridSpec(
            num_scalar_prefetch=1, grid=(n, Rr // tr),
            in_specs=[pl.BlockSpec((None, None, tr, C), lambda q, i, c: (c[0], q, i, 0)),
                      pl.BlockSpec((None, tr, C), lambda q, i, c: (q, i, 0))],
            out_specs=pl.BlockSpec((None, tr, C), lambda q, i, c: (q, i, 0))),
        out_shape=jax.ShapeDtypeStruct((n, Rr, C), BF16),
        compiler_params=_cparams(("parallel", "parallel")))(my_c, buf, got)


def _chip_exchange(red, name):
    def body(r_ref, out_ref, send_sems, recv_sems, local_sem):
        _chip_exchange_start(r_ref, out_ref, send_sems, recv_sems, local_sem)
        _chip_exchange_finish(r_ref, out_ref, send_sems, recv_sems, local_sem)

    return pl.pallas_call(
        body, name=name, in_specs=[ANY], out_specs=ANY,
        out_shape=jax.ShapeDtypeStruct(red.shape, red.dtype), scratch_shapes=_CHIP_SEMS,
    )(red)


_CHIP_SEMS = [pltpu.SemaphoreType.DMA((3,)), pltpu.SemaphoreType.DMA((3,)), pltpu.SemaphoreType.DMA]


def _chip_exchange_copies(r_ref, out_ref, send_sems, recv_sems, local_sem):
    mx, my, mc = _place()
    chips = [(1 - mx, my), (mx, 1 - my), (1 - mx, 1 - my)]

    def copy(k, src_slot, dst_slot, to):
        return pltpu.make_async_remote_copy(
            src_ref=r_ref.at[src_slot], dst_ref=out_ref.at[dst_slot],
            send_sem=send_sems.at[k], recv_sem=recv_sems.at[k],
            device_id=(*to, mc), device_id_type=pl.DeviceIdType.MESH)

    return dict(
        mine=pltpu.make_async_copy(r_ref.at[2 * mx + my], out_ref.at[2 * mx + my], local_sem),
        sends=[copy(k, 2 * px + py, 2 * mx + my, (px, py)) for k, (px, py) in enumerate(chips)],
        recvs=[copy(k, 2 * px + py, 2 * px + py, (px, py)) for k, (px, py) in enumerate(chips)],
    )


def _chip_exchange_start(*refs):
    cps = _chip_exchange_copies(*refs)
    cps["mine"].start()
    for cp in cps["sends"]:
        cp.start()


def _chip_exchange_finish(*refs):
    cps = _chip_exchange_copies(*refs)
    for cp in cps["recvs"]:
        cp.wait_recv()
    for cp in cps["sends"]:
        cp.wait_send()
    cps["mine"].wait()


def _pad_rows(a, rows):
    return jnp.pad(a, ((0, rows - a.shape[0]), (0, 0)))


class _Layout:
    def __init__(self, D, shards):
        self.D = D
        self.pieces = []
        off = 0
        for name, layer, rows in shards:
            pr = _round_up(rows, 16)
            self.pieces.append((name, layer, rows, pr, off))
            off += pr
        self.rows = _round_up(off, 256)

    def find(self, name, layer):
        for p in self.pieces:
            if p[0] == name and p[1] == layer:
                return p
        raise KeyError(name)


def kernel(x, c, ctx, c_ctx, w_ada, b_ada, g_mix, w_in, conv_w, conv_b, dt_bias, a_log, d_skip, ssd_norm_w, w_ssd_out, pool_w, pool_scale, w_pool_out, w_out, g_ffn, w_gate_up, w_down, g_final, loss_target, m_c_ctx, m_w_ada, m_b_ada, m_g_mix, m_w_in, m_conv_w, m_conv_b, m_dt_bias, m_a_log, m_d_skip, m_ssd_norm_w, m_w_ssd_out, m_pool_w, m_pool_scale, m_w_pool_out, m_w_out, m_g_ffn, m_w_gate_up, m_w_down, m_g_final, v_c_ctx, v_w_ada, v_b_ada, v_g_mix, v_w_in, v_conv_w, v_conv_b, v_dt_bias, v_a_log, v_d_skip, v_ssd_norm_w, v_w_ssd_out, v_pool_w, v_pool_scale, v_w_pool_out, v_w_out, v_g_ffn, v_w_gate_up, v_w_down, v_g_final):
    weights = dict(c_ctx=c_ctx, w_ada=w_ada, b_ada=b_ada, g_mix=g_mix, w_in=w_in, conv_w=conv_w, conv_b=conv_b,
                   dt_bias=dt_bias, a_log=a_log, d_skip=d_skip, ssd_norm_w=ssd_norm_w, w_ssd_out=w_ssd_out,
                   pool_w=pool_w, pool_scale=pool_scale, w_pool_out=w_pool_out, w_out=w_out, g_ffn=g_ffn,
                   w_gate_up=w_gate_up, w_down=w_down, g_final=g_final)
    moms_m = dict(c_ctx=m_c_ctx, w_ada=m_w_ada, b_ada=m_b_ada, g_mix=m_g_mix, w_in=m_w_in, conv_w=m_conv_w,
                  conv_b=m_conv_b, dt_bias=m_dt_bias, a_log=m_a_log, d_skip=m_d_skip, ssd_norm_w=m_ssd_norm_w,
                  w_ssd_out=m_w_ssd_out, pool_w=m_pool_w, pool_scale=m_pool_scale, w_pool_out=m_w_pool_out,
                  w_out=m_w_out, g_ffn=m_g_ffn, w_gate_up=m_w_gate_up, w_down=m_w_down, g_final=m_g_final)
    moms_v = dict(c_ctx=v_c_ctx, w_ada=v_w_ada, b_ada=v_b_ada, g_mix=v_g_mix, w_in=v_w_in, conv_w=v_conv_w,
                  conv_b=v_conv_b, dt_bias=v_dt_bias, a_log=v_a_log, d_skip=v_d_skip, ssd_norm_w=v_ssd_norm_w,
                  w_ssd_out=v_w_ssd_out, pool_w=v_pool_w, pool_scale=v_pool_scale, w_pool_out=v_w_pool_out,
                  w_out=v_w_out, g_ffn=v_g_ffn, w_gate_up=v_w_gate_up, w_down=v_w_down, g_final=v_g_final)
    order = ["c_ctx", "w_ada", "b_ada", "g_mix", "w_in", "conv_w", "conv_b", "dt_bias", "a_log", "d_skip",
             "ssd_norm_w", "w_ssd_out", "pool_w", "pool_scale", "w_pool_out", "w_out", "g_ffn", "w_gate_up",
             "w_down", "g_final"]
    big = ["w_ada", "w_in", "conv_w", "w_ssd_out", "pool_w", "w_pool_out", "w_out", "w_gate_up", "w_down"]
    small = [n for n in order if n not in big]

    depth = w_in.shape[0]
    L, D = x.shape[1], x.shape[2]
    n_ctx = ctx.shape[1]
    T = n_ctx + L
    in_cols = w_in.shape[2] * N_DEV
    xbc_w = conv_w.shape[2] * N_DEV
    dinner = ssd_norm_w.shape[1]
    H = dt_bias.shape[2]
    G = H // HPG
    GN = G * STATE
    assert xbc_w == dinner + 2 * GN and dinner == H * HEADDIM
    assert in_cols == dinner + xbc_w + 2 * H + D + 2 * D
    assert dinner == 2 * D and GN == D and 2 * H <= 128
    F = w_down.shape[1] * N_DEV
    pg = pool_w.shape[3]
    tr = n_ctx
    assert L % tr == 0 and tr % GRID_W == 0 and tr % CHUNK == 0 and L % CHUNK == 0
    n_ctx_tiles = 1
    tr_big = _tile(T, (1088, 544, 512, 256, 128))
    tr_mid = _tile(T, (544, 512, 256, 128))
    NP =_round_up(9 * D + 128, 512)
    gs = dinner // G
    off_xbc, off_dt, off_pool = dinner, dinner + xbc_w, dinner + xbc_w + 2 * H
    off_gate = off_pool + D

    def shard_rows(name, l):
        w = weights[name][l]
        if name in ("w_ada", "w_in", "w_gate_up"):
            return w.T
        if name == "conv_w":
            w8 = _pad_rows(w, 8)
            hi = w8.astype(BF16)
            lo = (w8 - hi.astype(F32)).astype(BF16)
            return jnp.concatenate([hi, lo], axis=0).reshape(-1, D)
        if name == "pool_w":
            return w.reshape(-1, D)
        return w

    first_needed = ["w_ada", "w_in"]
    shard = {(n, l): shard_rows(n, l) for l in range(depth) for n in big}
    gather_groups = [[first_needed, [n for n in big if n not in first_needed]] if l == 0 else [big]
                     for l in range(depth)]
    glays = [[_Layout(D, [(n, l, shard[(n, l)].shape[0]) for n in grp]) for grp in gather_groups[l]]
             for l in range(depth)]

    def packed(l, gi):
        lay = glays[l][gi]
        rows = jnp.concatenate([_pad_rows(shard[(n, l)].astype(BF16), lay.find(n, l)[3])
                                for n in gather_groups[l][gi]], axis=0)
        return _pad_rows(rows, lay.rows)

    def gather_exchange(l, gi):
        return _Exchange(packed(l, gi), jax.ShapeDtypeStruct((N_DEV, glays[l][gi].rows, D), BF16), _GATHER_SEMS,
                         _gather_start, _gather_finish)

    gathered = [[None] * len(g) for g in gather_groups]
    gathered[0][0] = _all_gather(packed(0, 0), "gather_weights")

    def full(name, l):
        gi = [name in grp for grp in gather_groups[l]].index(True)
        _, _, rows, _, off = glays[l][gi].find(name, l)
        return gathered[l][gi][:, off:off + rows, :]

    def w_inT_new(l):
        w = full("w_in", l).reshape(in_cols, D)
        parts = [w[:off_xbc], w[off_xbc:off_dt], w[off_pool:off_gate], w[off_gate:], w[off_dt:off_pool]]
        return _pad_rows(jnp.concatenate(parts, axis=0), NP)

    xs0 = jnp.concatenate([ctx[0], x[0]], axis=0)
    cc8 = _pad_rows(jnp.concatenate([c, c_ctx[None, :]], axis=0), 8)
    tgt = loss_target[0]

    def vec(a):
        return a.reshape(1, -1)

    def pad128(a):
        return jnp.pad(a.reshape(1, -1), ((0, 0), (0, 128 - 2 * H)))

    expand = (jnp.arange(128)[:, None] == jnp.arange(2 * H * HEADDIM)[None, :] // HEADDIM).astype(BF16)

    def from4(arr):
        return jnp.pad(arr.transpose(2, 0, 1, 3).reshape(T, 2 * H), ((0, 0), (0, 128 - 2 * H)))

    dt_blk = (9 * D) // 128
    conv_tc = 128
    ssd_gpb, ssd_gpb_bwd = min(G, 8), 2
    saved = []
    xcur = xs0
    for l in range(depth):
        W = dict(adaT=full("w_ada", l).reshape(6 * D, D), inT=w_inT_new(l))
        m6 = _ada_fwd(cc8, W["adaT"], vec(b_ada[l]), "ada_fwd")
        h = _norm_mod(xcur, vec(g_mix[l]), m6, 0, 1, n_ctx_tiles, tr, "norm_mod")
        if len(gather_groups[l]) > 1:
            proj, gathered[l][1] = _mm(h, W["inT"], "nt", F32, "mm_in_gather", exch=gather_exchange(l, 1))
        else:
            proj = _mm(h, W["inT"], "nt", F32, "mm_in")
        W.update(
            ssd=full("w_ssd_out", l).reshape(dinner, D), po=full("w_pool_out", l).reshape(D, D),
            out=full("w_out", l).reshape(D, D), guT=full("w_gate_up", l).reshape(2 * F, D),
            down=full("w_down", l).reshape(F, D),
            pool=full("pool_w", l).reshape(N_DEV, len(POOL_WINDOWS), pg // N_DEV, pg).transpose(1, 0, 2, 3)
            .reshape(len(POOL_WINDOWS), pg, pg),
        )
        cw = full("conv_w", l).reshape(N_DEV, 16, xbc_w // N_DEV).astype(F32)
        W["conv8"] = (cw[:, :8] + cw[:, 8:]).transpose(1, 0, 2).reshape(8, xbc_w)
        l1, l2, l3, dtb, ein, dte, etot = _dt_prep(proj, dt_blk, pad128(dt_bias[l]), pad128(a_log[l]), expand, H,
                                                   "dt_prep")
        L1, L2, L3 = (v[:, :2 * H].T.reshape(2, H, 1, T) for v in (l1, l2, l3))
        k16 = jnp.arange(16).reshape(1, 1, 16, 1)

        def rows16(at):
            terms = jnp.where(k16 == at, L1, jnp.where(k16 == at + 1, L2, L3))
            return jnp.where((k16 >= at) & (k16 < at + 3), terms, (k16 < 6).astype(BF16))

        lam_a, lam_b = rows16(0), -rows16(3) + 2 * (k16 < 3).astype(BF16)
        et = etot.reshape(T // CHUNK, 8, 128)[:, 0, :2 * H].reshape(T // CHUNK, 2, G, HPG).transpose(1, 2, 0, 3)
        etot5 = jnp.pad(jnp.broadcast_to(et[..., None], et.shape + (128,)),
                        ((0, 0), (0, 0), (0, 0), (0, 8 - HPG), (0, 0)))
        dsk = jnp.pad(jnp.repeat(d_skip[l], HEADDIM, axis=1)[:, None, :], ((0, 0), (0, 7), (0, 0)))
        scan_ops = (dtb, ein, dte, lam_a, lam_b, etot5)
        xbc = _conv_fwd(proj, dinner // conv_tc, xbc_w, W["conv8"], vec(conv_b[l]), n_ctx, conv_tc, "conv_fwd")
        if l + 1 < depth:
            y2, states, gathered[l + 1][0] = _ssd_fwd(xbc, *scan_ops, dsk, G, n_ctx, ssd_gpb, "ssd_fwd_gather",
                                                      gather_exchange(l + 1, 0))
        else:
            y2, states = _ssd_fwd(xbc, *scan_ops, dsk, G, n_ctx, ssd_gpb, "ssd_fwd")
        yn = _gnorm_fwd(y2, proj, vec(ssd_norm_w[l]), gs, tr_mid, "gnorm_fwd")
        pm = _pool_apply(proj, (6 * D) // pg, BF16, n_ctx, pg, False, "pool_fwd")
        pms = _pool_mix_fwd(pm, W["pool"], vec(pool_scale[l]), tr_big, "pool_mix_fwd")
        o_ssd = _mm(yn, W["ssd"], "nn", F32, "mm_ssd_out")
        o_pool = _mm(pms, W["po"], "nn", F32, "mm_pool_out")
        mg = _merge_fwd(proj, 7, o_ssd, o_pool, tr_mid, "merge_fwd")
        mo = _mm(mg, W["out"], "nn", F32, "mm_out")
        x1, h2 = _norm_mod(xcur, vec(g_ffn[l]), m6, 3, 4, n_ctx_tiles, tr, "resid_norm_mod", resid=(mo, 2))
        gu = _mm(h2, W["guT"], "nt", F32, "mm_gate_up")
        act = _swiglu_fwd(gu, tr_mid, "swiglu_fwd")
        f = _mm(act, W["down"], "nn", F32, "mm_down")
        saved.append(dict(W=W, m6=m6, x0=xcur, h=h, proj=proj, scan_ops=scan_ops, xbc=xbc, y2=y2,
                          states=states, yn=yn, pm=pm, pms=pms, o_ssd=o_ssd, o_pool=o_pool, mg=mg, mo=mo, x1=x1,
                          h2=h2, gu=gu, act=act, f=f))
        xcur = _resid(x1, f, m6, 5, n_ctx_tiles, tr, "resid")

    loss_blk, dx, dgf = _loss_head(xcur, tgt, vec(g_final), n_ctx_tiles, tr, "loss_head")
    loss = lax.psum(loss_blk[0, 0], MESH_AXES)

    big_rows = {}
    small_g = {n: [None] * depth for n in small}
    d_c_ctx = jnp.zeros((D,), F32)
    late = ["w_in", "w_ada", "conv_w"]
    rs_groups = [[[n for n in big if n not in late], late[:1], late[1:]] if l == 0 else [big] for l in range(depth)]
    rlays = [[_Layout(D, [(n, l, shard[(n, l)].shape[0]) for n in grp]) for grp in rs_groups[l]]
             for l in range(depth)]
    my_c = lax.axis_index("c").astype(jnp.int32).reshape(1)

    def reduce_pair(l, gi):
        lay = rlays[l][gi]
        gparts = [jnp.pad(big_rows[(n, l)].astype(BF16), ((0, 0), (0, pr - rows), (0, 0)))
                  for n, _, rows, pr, _ in lay.pieces]
        gparts.append(jnp.zeros((N_DEV, lay.rows - sum(p[3] for p in lay.pieces), D), BF16))
        gbuf = jnp.concatenate(gparts, axis=1)
        gbuf = gbuf.reshape(2, 2, 2, lay.rows, D).transpose(2, 0, 1, 3, 4).reshape(2, 4, lay.rows, D)
        return _pair_add(gbuf, _pair_exchange(gbuf, "rs_pair_exchange"), my_c, "rs_pair_add")

    slots = {}
    pending = None
    for l in reversed(range(depth)):
        S = saved[l]
        W, m6, proj = S["W"], S["m6"], S["proj"]
        df, dga2 = _resid_bwd(dx, S["f"], m6, 5, n_ctx_tiles, tr, "resid_bwd")
        dact = _mm(df, W["down"], "nt", F32, "mm_down_dx")
        g_down = _mm(S["act"], df, "tn", BF16,"mm_down_dw")
        dgu = _swiglu_bwd(S["gu"], dact, tr_mid, "swiglu_bwd")
        dh2 = _mm(dgu, W["guT"], "nn", F32, "mm_gate_up_dx")
        g_guT = _mm(dgu, S["h2"], "tn", BF16,"mm_gate_up_dw")
        dx1, st2 = _norm_mod_bwd(S["x1"], dh2, dx, vec(g_ffn[l]), m6, 4, n_ctx_tiles, tr, "norm_mod_bwd")
        dmo, dga1 = _resid_bwd(dx1, S["mo"], m6, 2, n_ctx_tiles, tr, "resid_bwd")
        dmg = _mm(dmo, W["out"], "nt", F32, "mm_out_dx")
        g_out = _mm(S["mg"], dmo, "tn", BF16,"mm_out_dw")
        do_ssd, do_pool, dgl = _merge_bwd(proj, 7, S["o_ssd"], S["o_pool"], dmg, tr_mid, "merge_bwd")
        dyn = _mm(do_ssd, W["ssd"], "nt", F32, "mm_ssd_out_dx")
        g_ssd = _mm(S["yn"], do_ssd, "tn", BF16,"mm_ssd_out_dw")
        dpms = _mm(do_pool, W["po"], "nt", F32, "mm_pool_out_dx")
        g_po = _mm(S["pms"], do_pool, "tn", BF16,"mm_pool_out_dw")
        dpm, g_pool, dps = _pool_mix_bwd(S["pm"], dpms, W["pool"], vec(pool_scale[l]), tr_big, "pool_mix_bwd")
        dup = _pool_apply(dpm, 0, BF16, n_ctx, pg, True, "pool_bwd")
        dy, dz, dnw = _gnorm_bwd(S["y2"], proj, vec(ssd_norm_w[l]), dyn, gs, tr, "gnorm_bwd")
        big_rows[("w_ssd_out", l)] = g_ssd.reshape(N_DEV, -1, D)
        big_rows[("pool_w", l)] = g_pool.reshape(len(POOL_WINDOWS), N_DEV, pg // N_DEV, pg).transpose(1, 0, 2, 3) \
            .reshape(N_DEV, -1, D)
        big_rows[("w_pool_out", l)] = g_po.reshape(N_DEV, -1, D)
        big_rows[("w_out", l)] = g_out.reshape(N_DEV, -1, D)
        big_rows[("w_gate_up", l)] = g_guT.reshape(N_DEV, -1, D)
        big_rows[("w_down", l)] = g_down.reshape(N_DEV, -1, D)
        riding = [] if pending is None else [pending]
        if len(rs_groups[l]) > 1:
            riding.append(((l, 0), reduce_pair(l, 0)))
        if not riding:
            dxs2, db2, dc2, ddt4, dlc4, dlr4 = _ssd_bwd(S["xbc"], dy, S["states"], *S["scan_ops"], G, n_ctx,
                                                        ssd_gpb_bwd, "ssd_bwd")
        else:
            red = jnp.concatenate([r for _, r in riding], axis=1)
            exch = _Exchange(red, jax.ShapeDtypeStruct(red.shape, BF16), _CHIP_SEMS, _chip_exchange_start,
                             _chip_exchange_finish)
            dxs2, db2, dc2, ddt4, dlc4, dlr4, got = _ssd_bwd(
                S["xbc"], dy, S["states"], *S["scan_ops"], G, n_ctx, ssd_gpb_bwd, "ssd_bwd_exchange", exch)
            at = 0
            for key, r in riding:
                slots[key] = got[:, at:at + r.shape[1]]
                at += r.shape[1]
        dskv = _pad_rows(jnp.repeat(d_skip[l], HEADDIM, axis=1), 8)
        cb = vec(conv_b[l])
        dxbc_x, dconv_x = _conv_bwd(proj, dinner // conv_tc, dxs2, 0, W["conv8"], cb, n_ctx, conv_tc, "conv_bwd_x",
                                    skip=(dy, dskv))
        dxbc_b, dconv_b = _conv_bwd(proj, 2 * dinner // conv_tc, db2, dinner // conv_tc, W["conv8"], cb, n_ctx,
                                    conv_tc, "conv_bwd_bc")
        dxbc_c, dconv_c = _conv_bwd(proj, (2 * dinner + GN) // conv_tc, dc2, (dinner + GN) // conv_tc, W["conv8"], cb,
                                    n_ctx, conv_tc, "conv_bwd_bc")
        dconv = jnp.concatenate([dconv_x, dconv_b, dconv_c], axis=1)
        ddt_raw, dtst = _dt_bwd(proj, dt_blk, pad128(dt_bias[l]), pad128(a_log[l]), from4(ddt4),
                                from4(dlc4 + dlr4.transpose(0, 1, 3, 2)), H, tr, "dt_bwd")
        dproj = jnp.concatenate([dz, dxbc_x, dxbc_b, dxbc_c, dup, dgl, ddt_raw,
                                 jnp.zeros((T, NP - 9 * D - 128), BF16)], axis=1)
        g_inT_new = _mm(dproj, S["h"], "tn", BF16,"mm_in_dw")
        g_inT = jnp.concatenate([g_inT_new[:6 * D], g_inT_new[9 * D:9 * D + 2 * H], g_inT_new[6 * D:9 * D]], axis=0)
        big_rows[("w_in", l)] = g_inT.reshape(N_DEV, -1, D)
        if len(rs_groups[l]) == 3:
            red = reduce_pair(l, 1)
            dh, slots[(l, 1)] = _mm(dproj, W["inT"], "nn", F32, "mm_in_dx_exchange", exch=_Exchange(
                red, jax.ShapeDtypeStruct(red.shape, BF16), _CHIP_SEMS, _chip_exchange_start, _chip_exchange_finish))
        else:
            dh = _mm(dproj, W["inT"], "nn", F32, "mm_in_dx")
        dx0, st1 = _norm_mod_bwd(S["x0"], dh, dx1, vec(g_mix[l]), m6, 1, n_ctx_tiles, tr, "norm_mod_bwd")
        dm6 = _pad_rows(jnp.concatenate([st1[0:2], st1[2:4], dga1[0:2], st2[0:2], st2[2:4], dga2[0:2]], axis=1), 8)
        dsil = _mm(dm6, W["adaT"], "nn", F32, "mm_ada_dx")
        sil_b, dcc, dbada = _ada_bwd_small(cc8, dsil, dm6, "ada_bwd_small")
        g_adaT = _mm(dm6, sil_b, "tn", BF16,"mm_ada_dw")
        d_c_ctx = d_c_ctx + dcc[1]
        dx = dx0

        big_rows[("w_ada", l)] = g_adaT.reshape(N_DEV, -1, D)
        big_rows[("conv_w", l)] = jnp.pad(
            dconv[:CONV_K].reshape(CONV_K, N_DEV, xbc_w // N_DEV).transpose(1, 0, 2),
            ((0, 0), (0, 16 - CONV_K), (0, 0))).reshape(N_DEV, -1, D)
        small_g["b_ada"][l] = dbada[0]
        small_g["g_mix"][l] = st1[4]
        small_g["conv_b"][l] = dconv[CONV_K]
        small_g["dt_bias"][l] = dtst[0, :2 * H].reshape(2, H)
        small_g["a_log"][l] = dtst[1, :2 * H].reshape(2, H)
        dsk_h = dconv_x[CONV_K + 1].reshape(H, HEADDIM).sum(axis=-1)
        small_g["d_skip"][l] = jnp.stack([dsk_h, dsk_h])
        small_g["ssd_norm_w"][l] = dnw[0]
        small_g["pool_scale"][l] = dps[0]
        small_g["g_ffn"][l] = st2[4]

        last_gi = len(rs_groups[l]) - 1
        pending = ((l, last_gi), reduce_pair(l, last_gi))
    slots[pending[0]] = _chip_exchange(pending[1], "rs_chip_exchange")
    g_local = {key: _sum_slots(s, "rs_chip_add") for key, s in slots.items()}
    grad_x = dx[n_ctx:][None]

    def local_grad(name):
        outs = []
        for l in range(depth):
            gi = [name in grp for grp in rs_groups[l]].index(True)
            _, _, rows, _, off = rlays[l][gi].find(name, l)
            piece = g_local[(l, gi)][off:off + rows]
            if name in ("w_ada", "w_in", "w_gate_up"):
                piece = piece.T
            elif name == "conv_w":
                piece = piece.reshape(16, -1)[:CONV_K]
            elif name == "pool_w":
                piece = piece.reshape(weights[name].shape[1:])
            outs.append(piece)
        return jnp.stack(outs)

    grads = {n: local_grad(n) for n in big}

    small_full = {"c_ctx": d_c_ctx, "g_final": dgf[0]}
    for n in small:
        if n not in small_full:
            small_full[n] = jnp.stack(small_g[n])

    def pack_small(tree):
        flat = jnp.concatenate([tree[n].reshape(-1).astype(F32) for n in small])
        rows = _round_up(-(-flat.shape[0] // D), 8)
        return jnp.pad(flat, (0, rows * D - flat.shape[0])).reshape(rows, D)

    def unpack_small(buf):
        flat, out, off = buf.reshape(-1), {}, 0
        for n in small:
            sz = weights[n].size
            out[n] = flat[off:off + sz].reshape(weights[n].shape)
            off += sz
        return out

    g_small = _sum_slots(_all_gather(pack_small(small_full), "gather_small_grads"), "sum_small_grads")
    grads.update(unpack_small(g_small))

    delta, new_m, new_v = {}, {}, {}
    for n in big:
        shp = weights[n].shape
        d_, m_, v_ = _adamw(weights[n].reshape(-1, shp[-1]), grads[n].reshape(-1, shp[-1]),
                            moms_m[n].reshape(-1, shp[-1]), moms_v[n].reshape(-1, shp[-1]), "adamw_" + n)
        delta[n], new_m[n], new_v[n] = d_.reshape(shp), m_.reshape(shp), v_.reshape(shp)
    d_, m_, v_ = _adamw(pack_small(weights), g_small, pack_small(moms_m), pack_small(moms_v), "adamw_small")
    delta.update(unpack_small(d_))
    new_m.update(unpack_small(m_))
    new_v.update(unpack_small(v_))

    return (loss, grad_x, *[grads[n] for n in order], *[delta[n] for n in order],
            *[new_m[n] for n in order], *[new_v[n] for n in order])
```

```python
import functools

import jax
import jax.numpy as jnp
from jax import lax
from jax.experimental import pallas as pl
from jax.experimental.pallas import tpu as pltpu

F32 = jnp.float32
BF16 = jnp.bfloat16
N_DEV = 8
EPS = 1e-6
GRID_W = 64
POOL_WINDOWS = (2, 4, 8, 16)
HEADDIM = 64
STATE = 128
CHUNK = 128
HPG = 4
CONV_K = 5
ADAM_LR, ADAM_B1, ADAM_B2, ADAM_EPS, ADAM_WD, ADAM_STEP = 0.001, 0.9, 0.999, 1e-08, 0.01, 10
NEG_BIG = -1e30
MESH_AXES = ("x", "y", "c")
ANY = pl.BlockSpec(memory_space=pl.ANY)


def _tile(n, cands):
    for t in cands:
        if n % t == 0:
            return t
    return n


def _round_up(n, m):
    return -(-n // m) * m


def _silu(x):
    return x * jax.nn.sigmoid(x)


def _dsilu(x):
    s = jax.nn.sigmoid(x)
    return s * (1.0 + x * (1.0 - s))


def _cparams(sem):
    return pltpu.CompilerParams(dimension_semantics=sem, vmem_limit_bytes=56 * 1024 * 1024)


def _mm(a, b, mode, out_dtype, name, exch=None):
    if mode == "tn":
        K, M = a.shape
        N = b.shape[1]
        tm = _tile(M, (512, 256, 128))
        tn = _tile(N, (1024, 512, 256, 128))
        tk = K if K <= 4608 else _tile(K, (1088, 544, 512, 256, 128))
        a_spec = pl.BlockSpec((tk, tm), lambda i, j, k: (k, i))
        b_spec = pl.BlockSpec((tk, tn), lambda i, j, k: (k, j))
        dims = (((0,), (0,)), ((), ()))
    else:
        M, K = a.shape
        N = b.shape[0] if mode == "nt" else b.shape[1]
        tm = _tile(M, (1088, 544, 512, 256, 128))
        tk = K if K <= 4096 else max(t for t in range(128, 2817, 128) if K % t == 0)
        tn = _tile(N, (512, 256, 128)) if tk == K else _tile(N, (1024, 512, 256, 128))
        a_spec = pl.BlockSpec((tm, tk), lambda i, j, k: (i, k))
        if mode == "nt":
            b_spec = pl.BlockSpec((tn, tk), lambda i, j, k: (j, k))
            dims = (((1,), (1,)), ((), ()))
        else:
            b_spec = pl.BlockSpec((tk, tn), lambda i, j, k: (k, j))
            dims = (((1,), (0,)), ((), ()))
    nk = K // tk

    def body(a_ref, b_ref, o_ref, *acc):
        if nk == 1:
            o_ref[...] = lax.dot_general(a_ref[...].astype(BF16), b_ref[...].astype(BF16), dims,
                                         preferred_element_type=F32).astype(o_ref.dtype)
            return
        k = pl.program_id(2)

        @pl.when(k == 0)
        def _():
            acc[0][...] = jnp.zeros_like(acc[0])

        acc[0][...] += lax.dot_general(a_ref[...].astype(BF16), b_ref[...].astype(BF16), dims,
                                       preferred_element_type=F32)

        @pl.when(k == nk - 1)
        def _():
            o_ref[...] = acc[0][...].astype(o_ref.dtype)

    call = dict(name=name, grid=(M // tm, N // tn, nk), in_specs=[a_spec, b_spec],
                scratch_shapes=[pltpu.VMEM((tm, tn), F32)] if nk > 1 else [])
    o_spec, o_sds = pl.BlockSpec((tm, tn), lambda i, j, k: (i, j)), jax.ShapeDtypeStruct((M, N), out_dtype)
    if exch is not None:
        return _call_with_exchange(body, exch, out_specs=[o_spec], out_shape=[o_sds], operands=(a, b), **call)
    return pl.pallas_call(body, out_specs=o_spec, out_shape=o_sds,
                          compiler_params=_cparams(("parallel", "parallel", "arbitrary")), **call)(a, b)


def _ada_fwd(cc8, w_adaT, b_ada, name):
    D = cc8.shape[1]
    N = w_adaT.shape[0]
    tn = _tile(N, (512, 256, 128))

    def body(c_ref, w_ref, b_ref, o_ref):
        a = _silu(c_ref[...]).astype(BF16)
        o_ref[...] = lax.dot_general(a, w_ref[...], (((1,), (1,)), ((), ())),
                                     preferred_element_type=F32) + b_ref[...]

    return pl.pallas_call(
        body, name=name, grid=(N // tn,),
        in_specs=[pl.BlockSpec((8, D), lambda j: (0, 0)), pl.BlockSpec((tn, D), lambda j: (j, 0)),
                  pl.BlockSpec((1, tn), lambda j: (0, j))],
        out_specs=pl.BlockSpec((8, tn), lambda j: (0, j)),
        out_shape=jax.ShapeDtypeStruct((8, N), F32),
        compiler_params=_cparams(("parallel",)),
    )(cc8, w_adaT, b_ada)


def _ada_bwd_small(cc8, dsil, dm6, name):
    D = cc8.shape[1]
    N = dm6.shape[1]

    def body(c_ref, ds_ref, dm_ref, sil_ref, dc_ref, db_ref):
        c = c_ref[...]
        sil_ref[...] = _silu(c).astype(BF16)
        dc_ref[...] = ds_ref[...] * _dsilu(c)
        dm = dm_ref[...]
        row = lax.broadcasted_iota(jnp.int32, dm.shape, 0)
        db_ref[...] = jnp.where(row == 0, jnp.sum(dm, axis=0, keepdims=True), 0.0)

    return pl.pallas_call(
        body, name=name, grid=(1,),
        in_specs=[pl.BlockSpec((8, D), lambda i: (0, 0)), pl.BlockSpec((8, D), lambda i: (0, 0)),
                  pl.BlockSpec((8, N), lambda i: (0, 0))],
        out_specs=[pl.BlockSpec((8, D), lambda i: (0, 0)), pl.BlockSpec((8, D), lambda i: (0, 0)),
                   pl.BlockSpec((8, N), lambda i: (0, 0))],
        out_shape=[jax.ShapeDtypeStruct((8, D), BF16), jax.ShapeDtypeStruct((8, D), F32),
                   jax.ShapeDtypeStruct((8, N), F32)],
        compiler_params=_cparams(("arbitrary",)),
    )(cc8, dsil, dm6)


def _seg_pick(m_ref, is_ctx):
    return jnp.where(is_ctx, m_ref[1:2, :], m_ref[0:1, :])


def _norm_mod(x, g, m6, sh_idx, sc_idx, n_ctx_tiles, tr, name, resid=None):
    T, D = x.shape
    row = pl.BlockSpec((tr, D), lambda i: (i, 0))
    vec = pl.BlockSpec((1, D), lambda i: (0, 0))

    def mcol(idx):
        return pl.BlockSpec((8, D), lambda i: (0, idx))

    def body(*refs):
        if resid is None:
            x_ref, g_ref, sh_ref, sc_ref, h_ref = refs
            xv = x_ref[...]
        else:
            x_ref, f_ref, ga_ref, g_ref, sh_ref, sc_ref, xo_ref, h_ref = refs
        is_ctx = pl.program_id(0) < n_ctx_tiles
        if resid is not None:
            xv = x_ref[...] + _seg_pick(ga_ref, is_ctx) * f_ref[...]
            xo_ref[...] = xv
        rstd = lax.rsqrt(jnp.mean(xv * xv, axis=-1, keepdims=True) + EPS)
        hn = xv * rstd * g_ref[...]
        h_ref[...] = (hn * (1.0 + _seg_pick(sc_ref, is_ctx)) + _seg_pick(sh_ref, is_ctx)).astype(BF16)

    if resid is None:
        ins, in_specs = [x, g, m6, m6], [row, vec, mcol(sh_idx), mcol(sc_idx)]
        out_specs, out_shape = row, jax.ShapeDtypeStruct((T, D), BF16)
    else:
        f, ga_idx = resid
        ins = [x, f, m6, g, m6, m6]
        in_specs = [row, row, mcol(ga_idx), vec, mcol(sh_idx), mcol(sc_idx)]
        out_specs = [row, row]
        out_shape = [jax.ShapeDtypeStruct((T, D), F32), jax.ShapeDtypeStruct((T, D), BF16)]
    return pl.pallas_call(body, name=name, grid=(T // tr,), in_specs=in_specs, out_specs=out_specs,
                          out_shape=out_shape, compiler_params=_cparams(("parallel",)))(*ins)


def _resid(x, f, m6, ga_idx, n_ctx_tiles, tr, name):
    T, D = x.shape
    row = pl.BlockSpec((tr, D), lambda i: (i, 0))

    def body(x_ref, f_ref, ga_ref, o_ref):
        is_ctx = pl.program_id(0) < n_ctx_tiles
        o_ref[...] = x_ref[...] + _seg_pick(ga_ref, is_ctx) * f_ref[...]

    return pl.pallas_call(body, name=name, grid=(T // tr,),
                          in_specs=[row, row, pl.BlockSpec((8, D), lambda i: (0, ga_idx))], out_specs=row,
                          out_shape=jax.ShapeDtypeStruct((T, D), F32),
                          compiler_params=_cparams(("parallel",)))(x, f, m6)


def _resid_bwd(dx, f, m6, ga_idx, n_ctx_tiles, tr, name):
    T, D = dx.shape
    row = pl.BlockSpec((tr, D), lambda i: (i, 0))
    acc = pl.BlockSpec((8, D), lambda i: (0, 0))

    def body(dx_ref, f_ref, ga_ref, df_ref, dga_ref):
        i = pl.program_id(0)
        is_ctx = i < n_ctx_tiles

        @pl.when(i == 0)
        def _():
            dga_ref[...] = jnp.zeros_like(dga_ref)

        dxv = dx_ref[...]
        df_ref[...] = (_seg_pick(ga_ref, is_ctx) * dxv).astype(BF16)
        s = jnp.sum(dxv * f_ref[...], axis=0, keepdims=True)
        r = lax.broadcasted_iota(jnp.int32, (8, D), 0)
        dga_ref[...] += jnp.where(r == jnp.where(is_ctx, 1, 0), s, 0.0)

    return pl.pallas_call(body, name=name, grid=(T // tr,),
                          in_specs=[row, row, pl.BlockSpec((8, D), lambda i: (0, ga_idx))],
                          out_specs=[row, acc],
                          out_shape=[jax.ShapeDtypeStruct((T, D), BF16), jax.ShapeDtypeStruct((8, D), F32)],
                          compiler_params=_cparams(("arbitrary",)))(dx, f, m6)


def _norm_mod_bwd(x, dh, dxres, g, m6, sc_idx, n_ctx_tiles, tr, name):
    T, D = x.shape
    row = pl.BlockSpec((tr, D), lambda i: (i, 0))
    acc = pl.BlockSpec((8, D), lambda i: (0, 0))

    def body(x_ref, dh_ref, dr_ref, g_ref, sc_ref, dx_ref, st_ref):
        i = pl.program_id(0)
        is_ctx = i < n_ctx_tiles

        @pl.when(i == 0)
        def _():
            st_ref[...] = jnp.zeros_like(st_ref)

        xv, dh_v, gv = x_ref[...], dh_ref[...], g_ref[...]
        sc1 = 1.0 + _seg_pick(sc_ref, is_ctx)
        rstd = lax.rsqrt(jnp.mean(xv * xv, axis=-1, keepdims=True) + EPS)
        xhat = xv * rstd
        dxhat = dh_v * sc1 * gv
        dx_ref[...] = dr_ref[...] + rstd * (dxhat - xhat * jnp.mean(dxhat * xhat, axis=-1, keepdims=True))
        dsh = jnp.sum(dh_v, axis=0, keepdims=True)
        dsc = jnp.sum(dh_v * xhat * gv, axis=0, keepdims=True)
        dg = jnp.sum(dh_v * sc1 * xhat, axis=0, keepdims=True)
        r = lax.broadcasted_iota(jnp.int32, (8, D), 0)
        seg = jnp.where(is_ctx, 1, 0)
        st_ref[...] += (jnp.where(r == seg, dsh, 0.0) + jnp.where(r == 2 + seg, dsc, 0.0)
                        + jnp.where(r == 4, dg, 0.0))

    return pl.pallas_call(body, name=name, grid=(T // tr,),
                          in_specs=[row, row, row, pl.BlockSpec((1, D), lambda i: (0, 0)),
                                    pl.BlockSpec((8, D), lambda i: (0, sc_idx))],
                          out_specs=[row, acc],
                          out_shape=[jax.ShapeDtypeStruct((T, D), F32), jax.ShapeDtypeStruct((8, D), F32)],
                          compiler_params=_cparams(("arbitrary",)))(x, dh, dxres, g, m6)


def _loss_head(x, tgt, g, n_ctx_tiles, tr, name):
    T, D = x.shape
    row = pl.BlockSpec((tr, D), lambda i: (i, 0))

    def body(x_ref, t_ref, g_ref, l_ref, dx_ref, dg_ref):
        i = pl.program_id(0)

        @pl.when(i == 0)
        def _():
            l_ref[...] = jnp.zeros_like(l_ref)
            dg_ref[...] = jnp.zeros_like(dg_ref)

        @pl.when(i < n_ctx_tiles)
        def _():
            dx_ref[...] = jnp.zeros_like(dx_ref)

        @pl.when(i >= n_ctx_tiles)
        def _():
            xv, gv = x_ref[...], g_ref[...]
            rstd = lax.rsqrt(jnp.mean(xv * xv, axis=-1, keepdims=True) + EPS)
            xhat = xv * rstd
            e = xhat * gv - t_ref[...]
            l_ref[...] += 0.5 * jnp.sum(jnp.mean(e * e, axis=-1, keepdims=True), axis=0, keepdims=True)
            dy = e * (1.0 / D)
            dxhat = dy * gv
            dx_ref[...] = rstd * (dxhat - xhat * jnp.mean(dxhat * xhat, axis=-1, keepdims=True))
            r = lax.broadcasted_iota(jnp.int32, (8, D), 0)
            dg_ref[...] += jnp.where(r == 0, jnp.sum(dy * xhat, axis=0, keepdims=True), 0.0)

    return pl.pallas_call(
        body, name=name, grid=(T // tr,),
        in_specs=[row, pl.BlockSpec((tr, D), lambda i: (jnp.maximum(i - n_ctx_tiles, 0), 0)),
                  pl.BlockSpec((1, D), lambda i: (0, 0))],
        out_specs=[pl.BlockSpec((8, 128), lambda i: (0, 0)), row, pl.BlockSpec((8, D), lambda i: (0, 0))],
        out_shape=[jax.ShapeDtypeStruct((8, 128), F32), jax.ShapeDtypeStruct((T, D), F32),
                   jax.ShapeDtypeStruct((8, D), F32)],
        compiler_params=_cparams(("arbitrary",)))(x, tgt, g)


def _seq_masks(T, n_ctx, width):
    row = lax.broadcasted_iota(jnp.int32, (T, width), 0)
    in_ctx = row < n_ctx
    return jnp.where(in_ctx, row, row - n_ctx), jnp.where(in_ctx, n_ctx, T - n_ctx)


def _shift_rows(u, off, t_loc, seg_len):
    T = u.shape[0]
    if off == 0:
        return u
    v = pltpu.roll(u, (-off) % T, 0)
    ok = (t_loc + off >= 0) & (t_loc + off < seg_len)
    return jnp.where(ok, v, 0.0)


def _conv_fwd(proj, col0_blk, ncol, conv_w8, conv_b, n_ctx, tc, name):
    T = proj.shape[0]

    def body(u_ref, w_ref, b_ref, o_ref):
        u = u_ref[...]
        t_loc, seg_len = _seq_masks(T, n_ctx, tc)
        acc = jnp.broadcast_to(b_ref[...], u.shape)
        for i in range(CONV_K):
            acc = acc + w_ref[i:i + 1, :] * _shift_rows(u, i - CONV_K // 2, t_loc, seg_len)
        o_ref[...] = _silu(acc)

    return pl.pallas_call(
        body, name=name, grid=(ncol // tc,),
        in_specs=[pl.BlockSpec((T, tc), lambda j: (0, col0_blk + j)), pl.BlockSpec((8, tc), lambda j: (0, j)),
                  pl.BlockSpec((1, tc), lambda j: (0, j))],
        out_specs=pl.BlockSpec((T, tc), lambda j: (0, j)),
        out_shape=jax.ShapeDtypeStruct((T, ncol), F32),
        compiler_params=_cparams(("parallel",)))(proj, conv_w8, conv_b)


def _conv_bwd(proj, col0_blk, d2, w_blk0, conv_w8, conv_b, n_ctx, tc, name, skip=None):
    T, ncol = d2.shape[1], d2.shape[2]

    def body(u_ref, d_ref, w_ref, b_ref, *rest):
        if skip is None:
            du_ref, dw_ref = rest
        else:
            dy_ref, k_ref, du_ref, dw_ref = rest
        u = u_ref[...]
        t_loc, seg_len = _seq_masks(T, n_ctx, tc)
        pre = jnp.broadcast_to(b_ref[...], u.shape)
        for i in range(CONV_K):
            pre = pre + w_ref[i:i + 1, :] * _shift_rows(u, i - CONV_K // 2, t_loc, seg_len)
        r = lax.broadcasted_iota(jnp.int32, (8, tc), 0)
        dact = d_ref[0] + d_ref[1]
        dw = jnp.zeros((8, tc), F32)
        if skip is not None:
            dyv = dy_ref[...]
            dact = dact + (k_ref[0:1, :] + k_ref[1:2, :]) * dyv
            dw = jnp.where(r == CONV_K + 1, jnp.sum(dyv * _silu(pre), axis=0, keepdims=True), 0.0)
        dpre = dact * _dsilu(pre)
        du = jnp.zeros_like(u)
        dw = dw + jnp.where(r == CONV_K, jnp.sum(dpre, axis=0, keepdims=True), 0.0)
        for i in range(CONV_K):
            off = i - CONV_K // 2
            du = du + w_ref[i:i + 1, :] * _shift_rows(dpre, -off, t_loc, seg_len)
            dw = dw + jnp.where(r == i, jnp.sum(dpre * _shift_rows(u, off, t_loc, seg_len), axis=0, keepdims=True),
                                0.0)
        du_ref[...] = du.astype(BF16)
        dw_ref[...] = dw

    col = pl.BlockSpec((T, tc), lambda j: (0, j))
    in_specs = [pl.BlockSpec((T, tc), lambda j: (0, col0_blk + j)), pl.BlockSpec((2, T, tc), lambda j: (0, 0, j)),
                pl.BlockSpec((8, tc), lambda j: (0, w_blk0 + j)), pl.BlockSpec((1, tc), lambda j: (0, w_blk0 + j))]
    operands = [proj, d2, conv_w8, conv_b]
    if skip is not None:
        in_specs += [col, pl.BlockSpec((8, tc), lambda j: (0, j))]
        operands += list(skip)
    return pl.pallas_call(
        body, name=name, grid=(ncol // tc,), in_specs=in_specs,
        out_specs=[col, pl.BlockSpec((8, tc), lambda j: (0, j))],
        out_shape=[jax.ShapeDtypeStruct((T, ncol), BF16), jax.ShapeDtypeStruct((8, ncol), F32)],
        compiler_params=_cparams(("parallel",)))(*operands)


def _pool_core(u, half, t_loc, seg_len, transpose):
    tr = u.shape[0]

    def shift(v, s):
        w = pltpu.roll(v, s % tr, 0)
        ok = (t_loc - s >= 0) & (t_loc - s < seg_len)
        return jnp.where(ok, w, 0.0)

    cnt = (jnp.minimum(t_loc, half) + jnp.minimum(seg_len - t_loc, half)).astype(F32)
    q = u / cnt if transpose else u
    back, ahead, h = q, q, 1
    while h < half:
        back = back + shift(back, h)
        ahead = ahead + shift(ahead, -h)
        h *= 2
    if transpose:
        tot = back + shift(ahead, -1)
        return tot - u
    tot = shift(back, 1) + ahead
    return tot / cnt - u


def _pool_apply(src, col0_blk, out_dtype, n_ctx, pg, transpose, name):
    T = src.shape[0]

    def body(u_ref, o_ref):
        gi = pl.program_id(0)
        row = lax.broadcasted_iota(jnp.int32, (T, pg), 0)
        seg_len = jnp.where(row < n_ctx, n_ctx, GRID_W)
        t_loc = row & (seg_len - 1)
        u = u_ref[...].astype(F32)
        for k_idx, k in enumerate(POOL_WINDOWS):
            @pl.when(gi == k_idx)
            def _(k=k):
                o_ref[...] = _pool_core(u, k // 2, t_loc, seg_len, transpose).astype(o_ref.dtype)

    return pl.pallas_call(
        body, name=name, grid=(len(POOL_WINDOWS),),
        in_specs=[pl.BlockSpec((T, pg), lambda gi: (0, col0_blk + gi))],
        out_specs=pl.BlockSpec((T, pg), lambda gi: (0, gi)),
        out_shape=jax.ShapeDtypeStruct((T, pg * len(POOL_WINDOWS)), out_dtype),
        compiler_params=_cparams(("parallel",)))(src)


def _pool_mix_fwd(pm, pool_w, pool_scale, tr, name):
    T, W = pm.shape
    ng, pg = pool_w.shape[0], pool_w.shape[1]

    def body(p_ref, w_ref, s_ref, o_ref):
        o_ref[...] = (jnp.dot(p_ref[...], w_ref[...], preferred_element_type=F32) * s_ref[...]).astype(BF16)

    return pl.pallas_call(
        body, name=name, grid=(T // tr, ng),
        in_specs=[pl.BlockSpec((tr, pg), lambda i, g: (i, g)), pl.BlockSpec((None, pg, pg), lambda i, g: (g, 0, 0)),
                  pl.BlockSpec((1, pg), lambda i, g: (0, g))],
        out_specs=pl.BlockSpec((tr, pg), lambda i, g: (i, g)),
        out_shape=jax.ShapeDtypeStruct((T, W), BF16),
        compiler_params=_cparams(("parallel", "parallel")))(pm, pool_w, pool_scale)


def _pool_mix_bwd(pm, dpms, pool_w, pool_scale, tr, name):
    T, W = pm.shape
    ng, pg = pool_w.shape[0], pool_w.shape[1]

    def body(p_ref, d_ref, w_ref, s_ref, dp_ref, dw_ref, ds_ref):
        i = pl.program_id(1)

        @pl.when(i == 0)
        def _():
            dw_ref[...] = jnp.zeros_like(dw_ref)
            ds_ref[...] = jnp.zeros_like(ds_ref)

        p, w = p_ref[...], w_ref[...]
        d = d_ref[...].astype(F32)
        pmix = jnp.dot(p, w, preferred_element_type=F32)
        r = lax.broadcasted_iota(jnp.int32, (8, pg), 0)
        ds_ref[...] += jnp.where(r == 0, jnp.sum(d * pmix, axis=0, keepdims=True), 0.0)
        dmix = (d * s_ref[...]).astype(BF16)
        dp_ref[...] = lax.dot_general(dmix, w, (((1,), (1,)), ((), ())), preferred_element_type=F32)
        dw_ref[...] += lax.dot_general(p, dmix, (((0,), (0,)), ((), ())), preferred_element_type=F32)

    return pl.pallas_call(
        body, name=name, grid=(ng, T // tr),
        in_specs=[pl.BlockSpec((tr, pg), lambda g, i: (i, g)), pl.BlockSpec((tr, pg), lambda g, i: (i, g)),
                  pl.BlockSpec((None, pg, pg), lambda g, i: (g, 0, 0)), pl.BlockSpec((1, pg), lambda g, i: (0, g))],
        out_specs=[pl.BlockSpec((tr, pg), lambda g, i: (i, g)), pl.BlockSpec((None, pg, pg), lambda g, i: (g, 0, 0)),
                   pl.BlockSpec((8, pg), lambda g, i: (0, g))],
        out_shape=[jax.ShapeDtypeStruct((T, W), F32), jax.ShapeDtypeStruct((ng, pg, pg), F32),
                   jax.ShapeDtypeStruct((8, W), F32)],
        compiler_params=_cparams(("parallel", "arbitrary")))(pm, dpms, pool_w, pool_scale)


def _chunk_cumsum(v, upper):
    Q = v.shape[0]
    ii = lax.broadcasted_iota(jnp.int32, (Q, Q), 0)
    jj = lax.broadcasted_iota(jnp.int32, (Q, Q), 1)
    tri = ((jj >= ii) if upper else (jj <= ii)).astype(BF16)
    h1 = v.astype(BF16)
    r1 = v - h1.astype(F32)
    h2 = r1.astype(BF16)
    h3 = (r1 - h2.astype(F32)).astype(BF16)
    return (jnp.dot(tri, h1, preferred_element_type=F32) + jnp.dot(tri, h2, preferred_element_type=F32)
            + jnp.dot(tri, h3, preferred_element_type=F32))


def _split3(v):
    h1 = v.astype(BF16)
    r1 = v - h1.astype(F32)
    h2 = r1.astype(BF16)
    return h1, h2, (r1 - h2.astype(F32)).astype(BF16)


def _dt_prep(proj, dt_blk, bias, a_log, expand, n_heads, name):
    T = proj.shape[0]
    Wd = expand.shape[1]
    Q = CHUNK
    cps = 2 if (T // Q) % 2 == 0 else 1
    row = pl.BlockSpec((cps * Q, 128), lambda i: (i, 0))
    wide = pl.BlockSpec((cps * Q, Wd), lambda i: (i, 0))

    def body(r_ref, b_ref, al_ref, e_ref, l1_ref, l2_ref, l3_ref, dtb_ref, ein_ref, dte_ref, etot_ref):
        xv = r_ref[...] + b_ref[...]
        dt = jnp.maximum(xv, 0.0) + jnp.log(1.0 + jnp.exp(-jnp.abs(xv)))
        a_all = -jnp.exp(al_ref[...]) * dt
        fwd_col = lax.broadcasted_iota(jnp.int32, (Q, 128), 1) < n_heads
        lams, rests = [], []
        for c in range(cps):
            a = a_all[c * Q:(c + 1) * Q]
            lam_c = jnp.where(fwd_col, _chunk_cumsum(a, False), _chunk_cumsum(a, True))
            tot_c = jnp.where(fwd_col[0:1], lam_c[Q - 1:Q], lam_c[0:1])
            etot_ref[8 * c:8 * (c + 1), :] = jnp.broadcast_to(jnp.exp(tot_c), (8, 128))
            lams.append(lam_c)
            rests.append(tot_c - lam_c)
        lam = lams[0] if cps == 1 else jnp.concatenate(lams, axis=0)
        rest = rests[0] if cps == 1 else jnp.concatenate(rests, axis=0)
        l1_ref[...], l2_ref[...], l3_ref[...] = _split3(lam)
        ex = e_ref[...]

        def rep(v):
            p1, p2, p3 = _split3(v)
            return (jnp.dot(p1, ex, preferred_element_type=F32) + jnp.dot(p2, ex, preferred_element_type=F32)
                    + jnp.dot(p3, ex, preferred_element_type=F32))

        dtb_ref[...] = rep(dt)
        ein_ref[...] = rep(jnp.exp(lam))
        dte_ref[...] = rep(jnp.exp(rest))

    vec = pl.BlockSpec((1, 128), lambda i: (0, 0))
    return pl.pallas_call(
        body, name=name, grid=(T // (cps * Q),),
        in_specs=[pl.BlockSpec((cps * Q, 128), lambda i: (i, dt_blk)), vec, vec,
                  pl.BlockSpec((128, Wd), lambda i: (0, 0))],
        out_specs=[row, row, row, wide, wide, wide, pl.BlockSpec((8 * cps, 128), lambda i: (i, 0))],
        out_shape=[jax.ShapeDtypeStruct((T, 128), BF16)] * 3 + [jax.ShapeDtypeStruct((T, Wd), F32)] * 3
        + [jax.ShapeDtypeStruct((T // Q * 8, 128), F32)],
        compiler_params=_cparams(("parallel",)))(proj, bias, a_log, expand)


def _dt_bwd(proj, dt_blk, bias, a_log, ddt, dlam, n_heads, tr, name):
    T = proj.shape[0]
    row = pl.BlockSpec((tr, 128), lambda i: (i, 0))
    vec = pl.BlockSpec((1, 128), lambda i: (0, 0))

    def body(r_ref, b_ref, al_ref, ddt_ref, dl_ref, o_ref, st_ref):
        @pl.when(pl.program_id(0) == 0)
        def _():
            st_ref[...] = jnp.zeros_like(st_ref)

        xv = r_ref[...] + b_ref[...]
        dt = jnp.maximum(xv, 0.0) + jnp.log(1.0 + jnp.exp(-jnp.abs(xv)))
        a_neg = -jnp.exp(al_ref[...])
        col = lax.broadcasted_iota(jnp.int32, (CHUNK, 128), 1)
        dl = dl_ref[...]
        parts = []
        for k in range(tr // CHUNK):
            dk = dl[k * CHUNK:(k + 1) * CHUNK]
            parts.append(jnp.where(col < n_heads, _chunk_cumsum(dk, True), _chunk_cumsum(dk, False)))
        dav = jnp.concatenate(parts, axis=0)
        draw = (ddt_ref[...] + dav * a_neg) * jax.nn.sigmoid(xv)
        o_ref[...] = draw.astype(BF16)
        r = lax.broadcasted_iota(jnp.int32, (8, 128), 0)
        st_ref[...] += (jnp.where(r == 0, jnp.sum(draw, axis=0, keepdims=True), 0.0)
                        + jnp.where(r == 1, jnp.sum(dav * dt, axis=0, keepdims=True) * a_neg, 0.0))

    return pl.pallas_call(
        body, name=name, grid=(T // tr,),
        in_specs=[pl.BlockSpec((tr, 128), lambda i: (i, dt_blk)), vec, vec, row, row],
        out_specs=[row, pl.BlockSpec((8, 128), lambda i: (0, 0))],
        out_shape=[jax.ShapeDtypeStruct((T, 128), BF16), jax.ShapeDtypeStruct((8, 128), F32)],
        compiler_params=_cparams(("arbitrary",)))(proj, bias, a_log, ddt, dlam)


def _scan_chunk(d, pos, nc_ctx, nc):
    rev = jnp.where(pos < nc_ctx, nc_ctx - 1 - pos, nc - 1 - (pos - nc_ctx))
    return jnp.where(d == 0, pos, rev)


def _chunk_mask(d):
    ii = lax.broadcasted_iota(jnp.int32, (CHUNK, CHUNK), 0)
    jj = lax.broadcasted_iota(jnp.int32, (CHUNK, CHUNK), 1)
    return (ii - jj) * jnp.where(d == 0, 1, -1) >= 0


def _ssd_specs(T, G, n_ctx, gpb, chunk_of):
    R, P, N, Q = HPG, HEADDIM, STATE, CHUNK
    H = G * R
    nc, nc_ctx = T // Q, n_ctx // Q
    xw, bw = gpb * R * P, gpb * N
    b_blk0 = (H * P) // bw
    c_blk0 = b_blk0 + G // gpb

    def ch(d, s):
        return chunk_of(d, s, nc_ctx, nc)

    return dict(
        x=pl.BlockSpec((Q, xw), lambda d, g, s: (ch(d, s), g)),
        b=pl.BlockSpec((Q, bw), lambda d, g, s: (ch(d, s), b_blk0 + g)),
        c=pl.BlockSpec((Q, bw), lambda d, g, s: (ch(d, s), c_blk0 + g)),
        col=pl.BlockSpec((None, gpb, Q, R), lambda d, g, s: (d, g, ch(d, s), 0)),
        row=pl.BlockSpec((None, gpb, R, Q), lambda d, g, s: (d, g, 0, ch(d, s))),
        rep=pl.BlockSpec((Q, xw), lambda d, g, s: (ch(d, s), d * (G // gpb) + g)),
        lam_a=pl.BlockSpec((None, gpb * R, 16, Q), lambda d, g, s: (d, g, 0, ch(d, s))),
        lam_b=pl.BlockSpec((None, gpb * R, 16, Q), lambda d, g, s: (d, g, 0, ch(d, s))),
        etot=pl.BlockSpec((None, gpb, None, 8, 128), lambda d, g, s: (d, g, ch(d, s), 0, 0)),
        dsk=pl.BlockSpec((None, 8, xw), lambda d, g, s: (d, 0, g)),
        xd=pl.BlockSpec((None, Q, xw), lambda d, g, s: (d, ch(d, s), g)),
        bd=pl.BlockSpec((None, Q, bw), lambda d, g, s: (d, ch(d, s), g)),
        st=pl.BlockSpec((None, None, gpb * R // 2, 2 * P, N), lambda d, g, s: (d, ch(d, s), g, 0, 0)),
    )


class _Exchange:
    def __init__(self, operand, out_sds, sems, start, finish):
        self.operand, self.out_sds, self.sems, self.start, self.finish = operand, out_sds, sems, start, finish


def _call_with_exchange(body, exch, *, name, grid, in_specs, out_specs, out_shape, scratch_shapes, operands):
    if exch is None:
        return pl.pallas_call(body, name=name, grid=grid, in_specs=in_specs, out_specs=out_specs,
                              out_shape=out_shape, scratch_shapes=scratch_shapes,
                              compiler_params=_cparams(("arbitrary",) * len(grid)))(*operands)
    n_in, n_out, n_scr = len(in_specs), len(out_specs), len(scratch_shapes)

    def fused(*refs):
        ins, c_in = refs[:n_in], refs[n_in]
        outs, c_out = refs[n_in + 1:n_in + 1 + n_out], refs[n_in + 1 + n_out]
        scr = refs[n_in + 2 + n_out:n_in + 2 + n_out + n_scr]
        sems = refs[n_in + 2 + n_out + n_scr:]
        ids = [pl.program_id(a) for a in range(len(grid))]
        first = functools.reduce(lambda p, q: p & q, [i == 0 for i in ids])
        last = functools.reduce(lambda p, q: p & q, [i == n - 1 for i, n in zip(ids, grid)])

        @pl.when(first)
        def _():
            exch.start(c_in, c_out, *sems)

        body(*ins, *outs, *scr)

        @pl.when(last)
        def _():
            exch.finish(c_in, c_out, *sems)

    return pl.pallas_call(fused, name=name, grid=grid, in_specs=list(in_specs) + [ANY],
                          out_specs=list(out_specs) + [ANY], out_shape=list(out_shape) + [exch.out_sds],
                          scratch_shapes=list(scratch_shapes) + list(exch.sems),
                          compiler_params=_cparams(("arbitrary",) * len(grid)))(*operands, exch.operand)


def _ssd_fwd(xbc, dtb, ein, dte, lam_a, lam_b, etot, dsk, G, n_ctx, gpb, name, exch=None):
    T = xbc.shape[0]
    R, P, N, Q = HPG, HEADDIM, STATE, CHUNK
    H = G * R
    nc = T // Q
    sp = _ssd_specs(T, G, n_ctx, gpb, _scan_chunk)

    def body(x_ref, b_ref, c_ref, dt_ref, ein_ref, dte_ref, la_ref, lb_ref, et_ref, dsk_ref, y_ref, st_ref, S):
        d, s = pl.program_id(0), pl.program_id(2)

        @pl.when(s == 0)
        def _():
            S[...] = jnp.zeros_like(S)

        mask = _chunk_mask(d)
        head0 = lax.broadcasted_iota(jnp.int32, (Q, 2 * P), 1) < P
        rows0 = lax.broadcasted_iota(jnp.int32, (2 * P, N), 0) < P
        for gg in range(gpb):
            Bm = b_ref[:, gg * N:(gg + 1) * N].astype(BF16)
            Cm = c_ref[:, gg * N:(gg + 1) * N].astype(BF16)
            Gm = lax.dot_general(Cm, Bm, (((1,), (1,)), ((), ())), preferred_element_type=F32)
            for k in range(R // 2):
                pk = gg * (R // 2) + k
                sl = slice(pk * 2 * P, (pk + 1) * 2 * P)
                xp = x_ref[:, sl]
                xc = xp * dt_ref[:, sl]
                s_in = S[pk]
                y = lax.dot_general(Cm, s_in.astype(BF16), (((1,), (1,)), ((), ())),
                                    preferred_element_type=F32) * ein_ref[:, sl] + dsk_ref[0:1, sl] * xp
                for j in range(2):
                    hr = 2 * pk + j
                    diff = lax.dot_general(la_ref[hr], lb_ref[hr], (((0,), (0,)), ((), ())), preferred_element_type=F32)
                    ldec = jnp.exp(jnp.where(mask, diff, NEG_BIG))
                    xc_j = (jnp.where(head0, xc, 0.0) if j == 0 else jnp.where(head0, 0.0, xc)).astype(BF16)
                    y = y + jnp.dot((Gm * ldec).astype(BF16), xc_j, preferred_element_type=F32)
                y_ref[:, sl] = y
                st_ref[pk] = s_in
                e_all = jnp.where(rows0, et_ref[gg, 2 * k:2 * k + 1, :], et_ref[gg, 2 * k + 1:2 * k + 2, :])
                xd = (xc * dte_ref[:, sl]).astype(BF16)
                S[pk] = e_all * s_in + lax.dot_general(xd, Bm, (((0,), (0,)), ((), ())),
                                                       preferred_element_type=F32)

    return _call_with_exchange(
        body, exch, name=name, grid=(2, G // gpb, nc),
        in_specs=[sp["x"], sp["b"], sp["c"], sp["rep"], sp["rep"], sp["rep"], sp["lam_a"], sp["lam_b"], sp["etot"],
                  sp["dsk"]],
        out_specs=[sp["xd"], sp["st"]],
        out_shape=[jax.ShapeDtypeStruct((2, T, H * P), F32), jax.ShapeDtypeStruct((2, nc, H // 2, 2 * P, N), F32)],
        scratch_shapes=[pltpu.VMEM((gpb * R // 2, 2 * P, N), F32)],
        operands=(xbc, xbc, xbc, dtb, ein, dte, lam_a, lam_b, etot, dsk))


def _ssd_bwd(xbc, dy, states, dtb, ein, dte, lam_a, lam_b, etot, G, n_ctx, gpb, name, exch=None):
    T = xbc.shape[0]
    R, P, N, Q = HPG, HEADDIM, STATE, CHUNK
    H = G * R
    nc = T // Q
    sp = _ssd_specs(T, G, n_ctx, gpb, lambda d, s, nc_ctx, n: _scan_chunk(d, n - 1 - s, nc_ctx, n))

    def body(x_ref, b_ref, c_ref, dy_ref, st_ref, dt_ref, ein_ref, dte_ref, la_ref, lb_ref, et_ref,
             dx_ref, db_ref, dc_ref, ddt_ref, dlc_ref, dlr_ref, dS):
        d, s = pl.program_id(0), pl.program_id(2)

        @pl.when(s == 0)
        def _():
            dS[...] = jnp.zeros_like(dS)

        mask = _chunk_mask(d)
        ri = lax.broadcasted_iota(jnp.int32, (Q, 1), 0)
        is_last = ri == jnp.where(d == 0, Q - 1, 0)
        head0 = lax.broadcasted_iota(jnp.int32, (Q, 2 * P), 1) < P
        rows0 = lax.broadcasted_iota(jnp.int32, (2 * P, N), 0) < P

        def total(v):
            return jnp.sum(jnp.sum(v, axis=1, keepdims=True), axis=0, keepdims=True)

        for gg in range(gpb):
            Bm = b_ref[:, gg * N:(gg + 1) * N].astype(BF16)
            Cm = c_ref[:, gg * N:(gg + 1) * N].astype(BF16)
            Gm = lax.dot_general(Cm, Bm, (((1,), (1,)), ((), ())), preferred_element_type=F32)
            dG = jnp.zeros((Q, Q), F32)
            dB = jnp.zeros((Q, N), F32)
            dC = jnp.zeros((Q, N), F32)
            for k in range(R // 2):
                pk = gg * (R // 2) + k
                sl = slice(pk * 2 * P, (pk + 1) * 2 * P)
                e_in = ein_ref[:, sl]
                dte = dte_ref[:, sl]
                e_all = jnp.where(rows0, et_ref[gg, 2 * k:2 * k + 1, :], et_ref[gg, 2 * k + 1:2 * k + 2, :])
                xp = x_ref[:, sl]
                dtp = dt_ref[:, sl]
                xc = xp * dtp
                xc_b = xc.astype(BF16)
                dyp = dy_ref[:, sl]
                s_in = st_ref[pk]
                s_in_b = s_in.astype(BF16)
                ds_out = dS[pk]
                ds_out_b = ds_out.astype(BF16)
                y_int = lax.dot_general(Cm, s_in_b, (((1,), (1,)), ((), ())), preferred_element_type=F32) * e_in
                b_ds = lax.dot_general(Bm, ds_out_b, (((1,), (1,)), ((), ())), preferred_element_type=F32)
                dxc = dte * b_ds
                u = xc * dxc
                v = dyp * y_int - u
                sse = ds_out * s_in * e_all
                for j in range(2):
                    hr, r = 2 * pk + j, 2 * k + j

                    def pick(a, m0=head0, j=j):
                        return jnp.where(m0, a, 0.0) if j == 0 else jnp.where(m0, 0.0, a)

                    diff = lax.dot_general(la_ref[hr], lb_ref[hr], (((0,), (0,)), ((), ())), preferred_element_type=F32)
                    ldec = jnp.exp(jnp.where(mask, diff, NEG_BIG))
                    dy_j = pick(dyp).astype(BF16)
                    dM = lax.dot_general(dy_j, xc_b, (((1,), (1,)), ((), ())), preferred_element_type=F32)
                    dMl = dM * ldec
                    Wm = dMl * Gm
                    dlam_c = jnp.sum(Wm, axis=1, keepdims=True) + jnp.sum(pick(v), axis=1, keepdims=True)
                    last = total(pick(sse, rows0)) + total(pick(u))
                    dlc_ref[gg, :, r:r + 1] = dlam_c + jnp.where(is_last, last, 0.0)
                    dlr_ref[gg, r:r + 1, :] = -jnp.sum(Wm, axis=0, keepdims=True)
                    dxc = dxc + lax.dot_general((Gm * ldec).astype(BF16), dy_j, (((0,), (0,)), ((), ())),
                                                preferred_element_type=F32)
                    dG = dG + dMl
                dx_ref[:, sl] = dxc * dtp
                t = dxc * xp
                ddt_ref[gg, :, 2 * k:2 * k + 1] = jnp.sum(jnp.where(head0, t, 0.0), axis=1, keepdims=True)
                ddt_ref[gg, :, 2 * k + 1:2 * k + 2] = jnp.sum(jnp.where(head0, 0.0, t), axis=1, keepdims=True)
                edy_b = (e_in * dyp).astype(BF16)
                dC = dC + jnp.dot(edy_b, s_in_b, preferred_element_type=F32)
                dB = dB + jnp.dot((dte * xc).astype(BF16), ds_out_b, preferred_element_type=F32)
                dS[pk] = e_all * ds_out + lax.dot_general(edy_b, Cm, (((0,), (0,)), ((), ())),
                                                          preferred_element_type=F32)
            dG_b = dG.astype(BF16)
            dc_ref[:, gg * N:(gg + 1) * N] = dC + jnp.dot(dG_b, Bm, preferred_element_type=F32)
            db_ref[:, gg * N:(gg + 1) * N] = dB + lax.dot_general(dG_b, Cm, (((0,), (0,)), ((), ())),
                                                                  preferred_element_type=F32)

    return _call_with_exchange(
        body, exch, name=name, grid=(2, G // gpb, nc),
        in_specs=[sp["x"], sp["b"], sp["c"], sp["x"], sp["st"], sp["rep"], sp["rep"], sp["rep"], sp["lam_a"],
                  sp["lam_b"], sp["etot"]],
        out_specs=[sp["xd"], sp["bd"], sp["bd"], sp["col"], sp["col"], sp["row"]],
        out_shape=[jax.ShapeDtypeStruct((2, T, H * P), F32), jax.ShapeDtypeStruct((2, T, G * N), F32),
                   jax.ShapeDtypeStruct((2, T, G * N), F32), jax.ShapeDtypeStruct((2, G, T, R), F32),
                   jax.ShapeDtypeStruct((2, G, T, R), F32), jax.ShapeDtypeStruct((2, G, R, T), F32)],
        scratch_shapes=[pltpu.VMEM((gpb * R // 2, 2 * P, N), F32)],
        operands=(xbc, xbc, xbc, dy, states, dtb, ein, dte, lam_a, lam_b, etot))


def _gnorm_fwd(y2, proj, w, gs, tr, name):
    T, HP = y2.shape[1], y2.shape[2]

    def body(y_ref, z_ref, w_ref, o_ref):
        yz = (y_ref[0] + y_ref[1]) * _silu(z_ref[...])
        for g in range(HP // gs):
            v = yz[:, g * gs:(g + 1) * gs]
            rstd = lax.rsqrt(jnp.mean(v * v, axis=-1, keepdims=True) + EPS)
            o_ref[:, g * gs:(g + 1) * gs] = (v * rstd * w_ref[:, g * gs:(g + 1) * gs]).astype(BF16)

    return pl.pallas_call(
        body, name=name, grid=(T // tr,),
        in_specs=[pl.BlockSpec((2, tr, HP), lambda i: (0, i, 0)), pl.BlockSpec((tr, HP), lambda i: (i, 0)),
                  pl.BlockSpec((1, HP), lambda i: (0, 0))],
        out_specs=pl.BlockSpec((tr, HP), lambda i: (i, 0)),
        out_shape=jax.ShapeDtypeStruct((T, HP), BF16),
        compiler_params=_cparams(("parallel",)))(y2, proj, w)


def _gnorm_bwd(y2, proj, w, dyn, gs, tr, name):
    T, HP = y2.shape[1], y2.shape[2]
    row = pl.BlockSpec((tr, HP), lambda i: (i, 0))

    def body(y_ref, z_ref, w_ref, d_ref, dy_ref, dz_ref, dw_ref):
        @pl.when(pl.program_id(0) == 0)
        def _():
            dw_ref[...] = jnp.zeros_like(dw_ref)

        yv = y_ref[0] + y_ref[1]
        zv = z_ref[...]
        sz = _silu(zv)
        yz = yv * sz
        dv = d_ref[...]
        r = lax.broadcasted_iota(jnp.int32, (8, gs), 0)
        for g in range(HP // gs):
            sl = slice(g * gs, (g + 1) * gs)
            v = yz[:, sl]
            rstd = lax.rsqrt(jnp.mean(v * v, axis=-1, keepdims=True) + EPS)
            xhat = v * rstd
            dyn_g = dv[:, sl]
            dhat = dyn_g * w_ref[:, sl]
            dyz = rstd * (dhat - xhat * jnp.mean(dhat * xhat, axis=-1, keepdims=True))
            dy_ref[:, sl] = dyz * sz[:, sl]
            dz_ref[:, sl] = (dyz * yv[:, sl] * _dsilu(zv[:, sl])).astype(BF16)
            dw_ref[:, sl] += jnp.where(r == 0, jnp.sum(dyn_g * xhat, axis=0, keepdims=True), 0.0)

    return pl.pallas_call(
        body, name=name, grid=(T // tr,),
        in_specs=[pl.BlockSpec((2, tr, HP), lambda i: (0, i, 0)), row, pl.BlockSpec((1, HP), lambda i: (0, 0)), row],
        out_specs=[row, row, pl.BlockSpec((8, HP), lambda i: (0, 0))],
        out_shape=[jax.ShapeDtypeStruct((T, HP), F32), jax.ShapeDtypeStruct((T, HP), BF16),
                   jax.ShapeDtypeStruct((8, HP), F32)],
        compiler_params=_cparams(("arbitrary",)))(y2, proj, w, dyn)


def _merge_fwd(proj, g1_blk, o_ssd, o_pool, tr, name):
    T, D = o_ssd.shape
    row = pl.BlockSpec((tr, D), lambda i: (i, 0))

    def body(g1_ref, g2_ref, a_ref, b_ref, o_ref):
        o_ref[...] = (jax.nn.sigmoid(g1_ref[...]) * a_ref[...]
                      + jax.nn.sigmoid(g2_ref[...]) * b_ref[...]).astype(BF16)

    return pl.pallas_call(
        body, name=name, grid=(T // tr,),
        in_specs=[pl.BlockSpec((tr, D), lambda i: (i, g1_blk)), pl.BlockSpec((tr, D), lambda i: (i, g1_blk + 1)),
                  row, row],
        out_specs=row, out_shape=jax.ShapeDtypeStruct((T, D), BF16),
        compiler_params=_cparams(("parallel",)))(proj, proj, o_ssd, o_pool)


def _merge_bwd(proj, g1_blk, o_ssd, o_pool, dmg, tr, name):
    T, D = o_ssd.shape
    row = pl.BlockSpec((tr, D), lambda i: (i, 0))

    def body(g1_ref, g2_ref, a_ref, b_ref, d_ref, da_ref, db_ref, dg_ref):
        s1, s2 = jax.nn.sigmoid(g1_ref[...]), jax.nn.sigmoid(g2_ref[...])
        dv = d_ref[...]
        da_ref[...] = (s1 * dv).astype(BF16)
        db_ref[...] = (s2 * dv).astype(BF16)
        dg_ref[:, :D] = (dv * a_ref[...] * s1 * (1.0 - s1)).astype(BF16)
        dg_ref[:, D:] = (dv * b_ref[...] * s2 * (1.0 - s2)).astype(BF16)

    return pl.pallas_call(
        body, name=name, grid=(T // tr,),
        in_specs=[pl.BlockSpec((tr, D), lambda i: (i, g1_blk)), pl.BlockSpec((tr, D), lambda i: (i, g1_blk + 1)),
                  row, row, row],
        out_specs=[row, row, pl.BlockSpec((tr, 2 * D), lambda i: (i, 0))],
        out_shape=[jax.ShapeDtypeStruct((T, D), BF16), jax.ShapeDtypeStruct((T, D), BF16),
                   jax.ShapeDtypeStruct((T, 2 * D), BF16)],
        compiler_params=_cparams(("parallel",)))(proj, proj, o_ssd, o_pool, dmg)


def _swiglu_fwd(gu, tr, name):
    T, F2 = gu.shape
    F = F2 // 2
    tc = _tile(F, (1408, 768, 512, 256, 128))
    nb = F // tc

    def body(a_ref, b_ref, o_ref):
        o_ref[...] = (_silu(a_ref[...]) * b_ref[...]).astype(BF16)

    return pl.pallas_call(
        body, name=name, grid=(T // tr, nb),
        in_specs=[pl.BlockSpec((tr, tc), lambda i, j: (i, j)), pl.BlockSpec((tr, tc), lambda i, j: (i, nb + j))],
        out_specs=pl.BlockSpec((tr, tc), lambda i, j: (i, j)),
        out_shape=jax.ShapeDtypeStruct((T, F), BF16),
        compiler_params=_cparams(("parallel", "parallel")))(gu, gu)


def _swiglu_bwd(gu, dact, tr, name):
    T, F2 = gu.shape
    F = F2 // 2
    tc = _tile(F, (1408, 768, 512, 256, 128))
    nb = F // tc

    def body(a_ref, b_ref, d_ref, o_ref):
        is_a = pl.program_id(1) < nb
        av, bv, dv = a_ref[...], b_ref[...], d_ref[...]
        o_ref[...] = jnp.where(is_a, dv * bv * _dsilu(av), dv * _silu(av)).astype(BF16)

    return pl.pallas_call(
        body, name=name, grid=(T // tr, 2 * nb),
        in_specs=[pl.BlockSpec((tr, tc), lambda i, j: (i, j % nb)), pl.BlockSpec((tr, tc), lambda i, j: (i, nb + j % nb)),
                  pl.BlockSpec((tr, tc), lambda i, j: (i, j % nb))],
        out_specs=pl.BlockSpec((tr, tc), lambda i, j: (i, j)),
        out_shape=jax.ShapeDtypeStruct((T, F2), BF16),
        compiler_params=_cparams(("parallel", "parallel")))(gu, gu, dact)


def _adamw(w, g, m, v, name):
    Rr, C = w.shape
    tr = _tile(Rr, (256, 128, 64, 32, 16, 8))
    row = pl.BlockSpec((tr, C), lambda i: (i, 0))

    def body(w_ref, g_ref, m_ref, v_ref, d_ref, mo_ref, vo_ref):
        gv = g_ref[...]
        mn = ADAM_B1 * m_ref[...] + (1.0 - ADAM_B1) * gv
        vn = ADAM_B2 * v_ref[...] + (1.0 - ADAM_B2) * (gv * gv)
        m_hat = mn / (1.0 - ADAM_B1 ** ADAM_STEP)
        v_hat = vn / (1.0 - ADAM_B2 ** ADAM_STEP)
        d_ref[...] = -ADAM_LR * (m_hat / (jnp.sqrt(v_hat) + ADAM_EPS) + ADAM_WD * w_ref[...])
        mo_ref[...] = mn
        vo_ref[...] = vn

    sds = jax.ShapeDtypeStruct((Rr, C), F32)
    return pl.pallas_call(body, name=name, grid=(Rr // tr,), in_specs=[row] * 4, out_specs=[row] * 3,
                          out_shape=[sds] * 3, compiler_params=_cparams(("parallel",)))(w, g, m, v)


def _sum_slots(x, name):
    n, Rr, C = x.shape
    tr = _tile(Rr, (512, 256, 128, 64, 32, 16, 8))

    def body(x_ref, o_ref):
        acc = x_ref[0].astype(F32)
        for k in range(1, n):
            acc = acc + x_ref[k].astype(F32)
        o_ref[...] = acc

    return pl.pallas_call(body, name=name, grid=(Rr // tr,),
                          in_specs=[pl.BlockSpec((n, tr, C), lambda i: (0, i, 0))],
                          out_specs=pl.BlockSpec((tr, C), lambda i: (i, 0)),
                          out_shape=jax.ShapeDtypeStruct((Rr, C), F32),
                          compiler_params=_cparams(("parallel",)))(x)


def _place():
    return lax.axis_index("x"), lax.axis_index("y"), lax.axis_index("c")


def _all_gather(x, name):
    Rr, C = x.shape

    def body(x_ref, out_ref, send_sems, recv_sems, local_sem):
        _gather_start(x_ref, out_ref, send_sems, recv_sems, local_sem)
        _gather_finish(x_ref, out_ref, send_sems, recv_sems, local_sem)

    return pl.pallas_call(
        body, name=name, in_specs=[ANY], out_specs=ANY,
        out_shape=jax.ShapeDtypeStruct((N_DEV, Rr, C), x.dtype), scratch_shapes=_GATHER_SEMS,
    )(x)


_GATHER_SEMS = [pltpu.SemaphoreType.DMA((7,)), pltpu.SemaphoreType.DMA((7,)), pltpu.SemaphoreType.DMA]


def _gather_copies(x_ref, out_ref, send_sems, recv_sems, local_sem):
    mx, my, mc = _place()
    me, sibling = (mx, my, mc), (mx, my, 1 - mc)
    chips = [(1 - mx, my), (mx, 1 - my), (1 - mx, 1 - my)]

    def slot(px, py, pc):
        return out_ref.at[4 * px + 2 * py + pc]

    def copy(k, block, to, src=None):
        return pltpu.make_async_remote_copy(
            src_ref=slot(*block) if src is None else src, dst_ref=slot(*block),
            send_sem=send_sems.at[k], recv_sem=recv_sems.at[k],
            device_id=to, device_id_type=pl.DeviceIdType.MESH)

    return dict(
        mine=pltpu.make_async_copy(x_ref, slot(*me), local_sem),
        first=[copy(0, me, sibling, src=x_ref)] + [copy(1 + j, me, (*chip, mc), src=x_ref)
                                                   for j, chip in enumerate(chips)],
        passed=[copy(4 + j, (*chip, mc), sibling) for j, chip in enumerate(chips)],
        from_chips=[copy(1 + j, (*chip, mc), me) for j, chip in enumerate(chips)],
        from_sibling=[copy(0, sibling, me)] + [copy(4 + j, (*chip, 1 - mc), me) for j, chip in enumerate(chips)],
    )


def _gather_start(*refs):
    cps = _gather_copies(*refs)
    cps["mine"].start()
    for cp in cps["first"]:
        cp.start()


def _gather_finish(*refs):
    cps = _gather_copies(*refs)
    for j in range(3):
        cps["from_chips"][j].wait_recv()
        cps["passed"][j].start()
    for cp in cps["from_sibling"]:
        cp.wait_recv()
    for cp in cps["first"] + cps["passed"]:
        cp.wait_send()
    cps["mine"].wait()


def _pair_exchange(buf, name):
    _, n, Rr, C = buf.shape
    parts = 4
    pr = Rr // parts

    def body(b_ref, got_ref, send_sems, recv_sems):
        mx, my, mc = _place()
        copies = []
        for q in range(n):
            for p in range(parts):
                rows = pl.ds(p * pr, pr)
                cp = pltpu.make_async_remote_copy(
                    src_ref=b_ref.at[1 - mc, q, rows], dst_ref=got_ref.at[q, rows],
                    send_sem=send_sems.at[q * parts + p], recv_sem=recv_sems.at[q * parts + p],
                    device_id=(mx, my, 1 - mc), device_id_type=pl.DeviceIdType.MESH)
                cp.start()
                copies.append(cp)
        for cp in copies:
            cp.wait()

    return pl.pallas_call(
        body, name=name, in_specs=[ANY], out_specs=ANY,
        out_shape=jax.ShapeDtypeStruct((n, Rr, C), buf.dtype),
        scratch_shapes=[pltpu.SemaphoreType.DMA((n * parts,)), pltpu.SemaphoreType.DMA((n * parts,))],
    )(buf)


def _pair_add(buf, got, my_c, name):
    _, n, Rr, C = buf.shape
    tr = _tile(Rr, (512, 256, 128, 64, 32, 16))

    def body(c_ref, b_ref, g_ref, o_ref):
        o_ref[...] = (b_ref[...].astype(F32) + g_ref[...].astype(F32)).astype(BF16)

    return pl.pallas_call(
        body, name=name,
        grid_spec=pltpu.PrefetchScalarGridSpec(
            num_scalar_prefetch=1, grid=(n, Rr // tr),
            in_specs=[pl.BlockSpec((None, None, tr, C), lambda q, i, c: (c[0], q, i, 0)),
                      pl.BlockSpec((None, tr, C), lambda q, i, c: (q, i, 0))],
            out_specs=pl.BlockSpec((None, tr, C), lambda q, i, c: (q, i, 0))),
        out_shape=jax.ShapeDtypeStruct((n, Rr, C), BF16),
        compiler_params=_cparams(("parallel", "parallel")))(my_c, buf, got)


def _chip_exchange(red, name):
    def body(r_ref, out_ref, send_sems, recv_sems, local_sem):
        _chip_exchange_start(r_ref, out_ref, send_sems, recv_sems, local_sem)
        _chip_exchange_finish(r_ref, out_ref, send_sems, recv_sems, local_sem)

    return pl.pallas_call(
        body, name=name, in_specs=[ANY], out_specs=ANY,
        out_shape=jax.ShapeDtypeStruct(red.shape, red.dtype), scratch_shapes=_CHIP_SEMS,
    )(red)


_CHIP_SEMS = [pltpu.SemaphoreType.DMA((3,)), pltpu.SemaphoreType.DMA((3,)), pltpu.SemaphoreType.DMA]


def _chip_exchange_copies(r_ref, out_ref, send_sems, recv_sems, local_sem):
    mx, my, mc = _place()
    chips = [(1 - mx, my), (mx, 1 - my), (1 - mx, 1 - my)]

    def copy(k, src_slot, dst_slot, to):
        return pltpu.make_async_remote_copy(
            src_ref=r_ref.at[src_slot], dst_ref=out_ref.at[dst_slot],
            send_sem=send_sems.at[k], recv_sem=recv_sems.at[k],
            device_id=(*to, mc), device_id_type=pl.DeviceIdType.MESH)

    return dict(
        mine=pltpu.make_async_copy(r_ref.at[2 * mx + my], out_ref.at[2 * mx + my], local_sem),
        sends=[copy(k, 2 * px + py, 2 * mx + my, (px, py)) for k, (px, py) in enumerate(chips)],
        recvs=[copy(k, 2 * px + py, 2 * px + py, (px, py)) for k, (px, py) in enumerate(chips)],
    )


def _chip_exchange_start(*refs):
    cps = _chip_exchange_copies(*refs)
    cps["mine"].start()
    for cp in cps["sends"]:
        cp.start()


def _chip_exchange_finish(*refs):
    cps = _chip_exchange_copies(*refs)
    for cp in cps["recvs"]:
        cp.wait_recv()
    for cp in cps["sends"]:
        cp.wait_send()
    cps["mine"].wait()


def _pad_rows(a, rows):
    return jnp.pad(a, ((0, rows - a.shape[0]), (0, 0)))


class _Layout:
    def __init__(self, D, shards):
        self.D = D
        self.pieces = []
        off = 0
        for name, layer, rows in shards:
            pr = _round_up(rows, 16)
            self.pieces.append((name, layer, rows, pr, off))
            off += pr
        self.rows = _round_up(off, 256)

    def find(self, name, layer):
        for p in self.pieces:
            if p[0] == name and p[1] == layer:
                return p
        raise KeyError(name)


def kernel(x, c, ctx, c_ctx, w_ada, b_ada, g_mix, w_in, conv_w, conv_b, dt_bias, a_log, d_skip, ssd_norm_w, w_ssd_out, pool_w, pool_scale, w_pool_out, w_out, g_ffn, w_gate_up, w_down, g_final, loss_target, m_c_ctx, m_w_ada, m_b_ada, m_g_mix, m_w_in, m_conv_w, m_conv_b, m_dt_bias, m_a_log, m_d_skip, m_ssd_norm_w, m_w_ssd_out, m_pool_w, m_pool_scale, m_w_pool_out, m_w_out, m_g_ffn, m_w_gate_up, m_w_down, m_g_final, v_c_ctx, v_w_ada, v_b_ada, v_g_mix, v_w_in, v_conv_w, v_conv_b, v_dt_bias, v_a_log, v_d_skip, v_ssd_norm_w, v_w_ssd_out, v_pool_w, v_pool_scale, v_w_pool_out, v_w_out, v_g_ffn, v_w_gate_up, v_w_down, v_g_final):
    weights = dict(c_ctx=c_ctx, w_ada=w_ada, b_ada=b_ada, g_mix=g_mix, w_in=w_in, conv_w=conv_w, conv_b=conv_b,
                   dt_bias=dt_bias, a_log=a_log, d_skip=d_skip, ssd_norm_w=ssd_norm_w, w_ssd_out=w_ssd_out,
                   pool_w=pool_w, pool_scale=pool_scale, w_pool_out=w_pool_out, w_out=w_out, g_ffn=g_ffn,
                   w_gate_up=w_gate_up, w_down=w_down, g_final=g_final)
    moms_m = dict(c_ctx=m_c_ctx, w_ada=m_w_ada, b_ada=m_b_ada, g_mix=m_g_mix, w_in=m_w_in, conv_w=m_conv_w,
                  conv_b=m_conv_b, dt_bias=m_dt_bias, a_log=m_a_log, d_skip=m_d_skip, ssd_norm_w=m_ssd_norm_w,
                  w_ssd_out=m_w_ssd_out, pool_w=m_pool_w, pool_scale=m_pool_scale, w_pool_out=m_w_pool_out,
                  w_out=m_w_out, g_ffn=m_g_ffn, w_gate_up=m_w_gate_up, w_down=m_w_down, g_final=m_g_final)
    moms_v = dict(c_ctx=v_c_ctx, w_ada=v_w_ada, b_ada=v_b_ada, g_mix=v_g_mix, w_in=v_w_in, conv_w=v_conv_w,
                  conv_b=v_conv_b, dt_bias=v_dt_bias, a_log=v_a_log, d_skip=v_d_skip, ssd_norm_w=v_ssd_norm_w,
                  w_ssd_out=v_w_ssd_out, pool_w=v_pool_w, pool_scale=v_pool_scale, w_pool_out=v_w_pool_out,
                  w_out=v_w_out, g_ffn=v_g_ffn, w_gate_up=v_w_gate_up, w_down=v_w_down, g_final=v_g_final)
    order = ["c_ctx", "w_ada", "b_ada", "g_mix", "w_in", "conv_w", "conv_b", "dt_bias", "a_log", "d_skip",
             "ssd_norm_w", "w_ssd_out", "pool_w", "pool_scale", "w_pool_out", "w_out", "g_ffn", "w_gate_up",
             "w_down", "g_final"]
    big = ["w_ada", "w_in", "conv_w", "w_ssd_out", "pool_w", "w_pool_out", "w_out", "w_gate_up", "w_down"]
    small = [n for n in order if n not in big]

    depth = w_in.shape[0]
    L, D = x.shape[1], x.shape[2]
    n_ctx = ctx.shape[1]
    T = n_ctx + L
    in_cols = w_in.shape[2] * N_DEV
    xbc_w = conv_w.shape[2] * N_DEV
    dinner = ssd_norm_w.shape[1]
    H = dt_bias.shape[2]
    G = H // HPG
    GN = G * STATE
    assert xbc_w == dinner + 2 * GN and dinner == H * HEADDIM
    assert in_cols == dinner + xbc_w + 2 * H + D + 2 * D
    assert dinner == 2 * D and GN == D and 2 * H <= 128
    F = w_down.shape[1] * N_DEV
    pg = pool_w.shape[3]
    tr = n_ctx
    assert L % tr == 0 and tr % GRID_W == 0 and tr % CHUNK == 0 and L % CHUNK == 0
    n_ctx_tiles = 1
    tr_big = _tile(T, (1088, 544, 512, 256, 128))
    tr_mid = _tile(T, (544, 512, 256, 128))
    NP =_round_up(9 * D + 128, 512)
    gs = dinner // G
    off_xbc, off_dt, off_pool = dinner, dinner + xbc_w, dinner + xbc_w + 2 * H
    off_gate = off_pool + D

    def shard_rows(name, l):
        w = weights[name][l]
        if name in ("w_ada", "w_in", "w_gate_up"):
            return w.T
        if name == "conv_w":
            w8 = _pad_rows(w, 8)
            hi = w8.astype(BF16)
            lo = (w8 - hi.astype(F32)).astype(BF16)
            return jnp.concatenate([hi, lo], axis=0).reshape(-1, D)
        if name == "pool_w":
            return w.reshape(-1, D)
        return w

    first_needed = ["w_ada", "w_in"]
    shard = {(n, l): shard_rows(n, l) for l in range(depth) for n in big}
    gather_groups = [[first_needed, [n for n in big if n not in first_needed]] for l in range(depth)]
    glays = [[_Layout(D, [(n, l, shard[(n, l)].shape[0]) for n in grp]) for grp in gather_groups[l]]
             for l in range(depth)]

    def packed(l, gi):
        lay = glays[l][gi]
        rows = jnp.concatenate([_pad_rows(shard[(n, l)].astype(BF16), lay.find(n, l)[3])
                                for n in gather_groups[l][gi]], axis=0)
        return _pad_rows(rows, lay.rows)

    def gather_exchange(l, gi):
        return _Exchange(packed(l, gi), jax.ShapeDtypeStruct((N_DEV, glays[l][gi].rows, D), BF16), _GATHER_SEMS,
                         _gather_start, _gather_finish)

    gathered = [[None] * len(g) for g in gather_groups]
    gathered[0][0] = _all_gather(packed(0, 0), "gather_weights")

    def full(name, l):
        gi = [name in grp for grp in gather_groups[l]].index(True)
        _, _, rows, _, off = glays[l][gi].find(name, l)
        return gathered[l][gi][:, off:off + rows, :]

    def w_inT_new(l):
        w = full("w_in", l).reshape(in_cols, D)
        parts = [w[:off_xbc], w[off_xbc:off_dt], w[off_pool:off_gate], w[off_gate:], w[off_dt:off_pool]]
        return _pad_rows(jnp.concatenate(parts, axis=0), NP)

    xs0 = jnp.concatenate([ctx[0], x[0]], axis=0)
    cc8 = _pad_rows(jnp.concatenate([c, c_ctx[None, :]], axis=0), 8)
    tgt = loss_target[0]

    def vec(a):
        return a.reshape(1, -1)

    def pad128(a):
        return jnp.pad(a.reshape(1, -1), ((0, 0), (0, 128 - 2 * H)))

    expand = (jnp.arange(128)[:, None] == jnp.arange(2 * H * HEADDIM)[None, :] // HEADDIM).astype(BF16)

    def from4(arr):
        return jnp.pad(arr.transpose(2, 0, 1, 3).reshape(T, 2 * H), ((0, 0), (0, 128 - 2 * H)))

    dt_blk = (9 * D) // 128
    conv_tc = 128
    ssd_gpb, ssd_gpb_bwd = min(G, 8), 2
    saved = []
    xcur = xs0
    for l in range(depth):
        W = dict(adaT=full("w_ada", l).reshape(6 * D, D), inT=w_inT_new(l))
        m6 = _ada_fwd(cc8, W["adaT"], vec(b_ada[l]), "ada_fwd")
        h = _norm_mod(xcur, vec(g_mix[l]), m6, 0, 1, n_ctx_tiles, tr, "norm_mod")
        if len(gather_groups[l]) > 1:
            proj, gathered[l][1] = _mm(h, W["inT"], "nt", F32, "mm_in_gather", exch=gather_exchange(l, 1))
        else:
            proj = _mm(h, W["inT"], "nt", F32, "mm_in")
        W.update(
            ssd=full("w_ssd_out", l).reshape(dinner, D), po=full("w_pool_out", l).reshape(D, D),
            out=full("w_out", l).reshape(D, D), guT=full("w_gate_up", l).reshape(2 * F, D),
            down=full("w_down", l).reshape(F, D),
            pool=full("pool_w", l).reshape(N_DEV, len(POOL_WINDOWS), pg // N_DEV, pg).transpose(1, 0, 2, 3)
            .reshape(len(POOL_WINDOWS), pg, pg),
        )
        cw = full("conv_w", l).reshape(N_DEV, 16, xbc_w // N_DEV).astype(F32)
        W["conv8"] = (cw[:, :8] + cw[:, 8:]).transpose(1, 0, 2).reshape(8, xbc_w)
        l1, l2, l3, dtb, ein, dte, etot = _dt_prep(proj, dt_blk, pad128(dt_bias[l]), pad128(a_log[l]), expand, H,
                                                   "dt_prep")
        L1, L2, L3 = (v[:, :2 * H].T.reshape(2, H, 1, T) for v in (l1, l2, l3))
        k16 = jnp.arange(16).reshape(1, 1, 16, 1)

        def rows16(at):
            terms = jnp.where(k16 == at, L1, jnp.where(k16 == at + 1, L2, L3))
            return jnp.where((k16 >= at) & (k16 < at + 3), terms, (k16 < 6).astype(BF16))

        lam_a, lam_b = rows16(0), -rows16(3) + 2 * (k16 < 3).astype(BF16)
        et = etot.reshape(T // CHUNK, 8, 128)[:, 0, :2 * H].reshape(T // CHUNK, 2, G, HPG).transpose(1, 2, 0, 3)
        etot5 = jnp.pad(jnp.broadcast_to(et[..., None], et.shape + (128,)),
                        ((0, 0), (0, 0), (0, 0), (0, 8 - HPG), (0, 0)))
        dsk = jnp.pad(jnp.repeat(d_skip[l], HEADDIM, axis=1)[:, None, :], ((0, 0), (0, 7), (0, 0)))
        scan_ops = (dtb, ein, dte, lam_a, lam_b, etot5)
        xbc = _conv_fwd(proj, dinner // conv_tc, xbc_w, W["conv8"], vec(conv_b[l]), n_ctx, conv_tc, "conv_fwd")
        if l + 1 < depth:
            y2, states, gathered[l + 1][0] = _ssd_fwd(xbc, *scan_ops, dsk, G, n_ctx, ssd_gpb, "ssd_fwd_gather",
                                                      gather_exchange(l + 1, 0))
        else:
            y2, states = _ssd_fwd(xbc, *scan_ops, dsk, G, n_ctx, ssd_gpb, "ssd_fwd")
        yn = _gnorm_fwd(y2, proj, vec(ssd_norm_w[l]), gs, tr_mid, "gnorm_fwd")
        pm = _pool_apply(proj, (6 * D) // pg, BF16, n_ctx, pg, False, "pool_fwd")
        pms = _pool_mix_fwd(pm, W["pool"], vec(pool_scale[l]), tr_big, "pool_mix_fwd")
        o_ssd = _mm(yn, W["ssd"], "nn", F32, "mm_ssd_out")
        o_pool = _mm(pms, W["po"], "nn", F32, "mm_pool_out")
        mg = _merge_fwd(proj, 7, o_ssd, o_pool, tr_mid, "merge_fwd")
        mo = _mm(mg, W["out"], "nn", F32, "mm_out")
        x1, h2 = _norm_mod(xcur, vec(g_ffn[l]), m6, 3, 4, n_ctx_tiles, tr, "resid_norm_mod", resid=(mo, 2))
        gu = _mm(h2, W["guT"], "nt", F32, "mm_gate_up")
        act = _swiglu_fwd(gu, tr_mid, "swiglu_fwd")
        f = _mm(act, W["down"], "nn", F32, "mm_down")
        saved.append(dict(W=W, m6=m6, x0=xcur, h=h, proj=proj, scan_ops=scan_ops, xbc=xbc, y2=y2,
                          states=states, yn=yn, pm=pm, pms=pms, o_ssd=o_ssd, o_pool=o_pool, mg=mg, mo=mo, x1=x1,
                          h2=h2, gu=gu, act=act, f=f))
        xcur = _resid(x1, f, m6, 5, n_ctx_tiles, tr, "resid")

    loss_blk, dx, dgf = _loss_head(xcur, tgt, vec(g_final), n_ctx_tiles, tr, "loss_head")
    loss = lax.psum(loss_blk[0, 0], MESH_AXES)

    big_rows = {}
    small_g = {n: [None] * depth for n in small}
    d_c_ctx = jnp.zeros((D,), F32)
    late = ["w_in", "w_ada", "conv_w"]
    rs_groups = [[[n for n in big if n not in late], late[:1], late[1:]] if l == 0 else [big] for l in range(depth)]
    rlays = [[_Layout(D, [(n, l, shard[(n, l)].shape[0]) for n in grp]) for grp in rs_groups[l]]
             for l in range(depth)]
    my_c = lax.axis_index("c").astype(jnp.int32).reshape(1)

    def reduce_pair(l, gi):
        lay = rlays[l][gi]
        gparts = [jnp.pad(big_rows[(n, l)].astype(BF16), ((0, 0), (0, pr - rows), (0, 0)))
                  for n, _, rows, pr, _ in lay.pieces]
        gparts.append(jnp.zeros((N_DEV, lay.rows - sum(p[3] for p in lay.pieces), D), BF16))
        gbuf = jnp.concatenate(gparts, axis=1)
        gbuf = gbuf.reshape(2, 2, 2, lay.rows, D).transpose(2, 0, 1, 3, 4).reshape(2, 4, lay.rows, D)
        return _pair_add(gbuf, _pair_exchange(gbuf, "rs_pair_exchange"), my_c, "rs_pair_add")

    slots = {}
    pending = None
    for l in reversed(range(depth)):
        S = saved[l]
        W, m6, proj = S["W"], S["m6"], S["proj"]
        df, dga2 = _resid_bwd(dx, S["f"], m6, 5, n_ctx_tiles, tr, "resid_bwd")
        dact = _mm(df, W["down"], "nt", F32, "mm_down_dx")
        g_down = _mm(S["act"], df, "tn", BF16,"mm_down_dw")
        dgu = _swiglu_bwd(S["gu"], dact, tr_mid, "swiglu_bwd")
        dh2 = _mm(dgu, W["guT"], "nn", F32, "mm_gate_up_dx")
        g_guT = _mm(dgu, S["h2"], "tn", BF16,"mm_gate_up_dw")
        dx1, st2 = _norm_mod_bwd(S["x1"], dh2, dx, vec(g_ffn[l]), m6, 4, n_ctx_tiles, tr, "norm_mod_bwd")
        dmo, dga1 = _resid_bwd(dx1, S["mo"], m6, 2, n_ctx_tiles, tr, "resid_bwd")
        dmg = _mm(dmo, W["out"], "nt", F32, "mm_out_dx")
        g_out = _mm(S["mg"], dmo, "tn", BF16,"mm_out_dw")
        do_ssd, do_pool, dgl = _merge_bwd(proj, 7, S["o_ssd"], S["o_pool"], dmg, tr_mid, "merge_bwd")
        dyn = _mm(do_ssd, W["ssd"], "nt", F32, "mm_ssd_out_dx")
        g_ssd = _mm(S["yn"], do_ssd, "tn", BF16,"mm_ssd_out_dw")
        dpms = _mm(do_pool, W["po"], "nt", F32, "mm_pool_out_dx")
        g_po = _mm(S["pms"], do_pool, "tn", BF16,"mm_pool_out_dw")
        dpm, g_pool, dps = _pool_mix_bwd(S["pm"], dpms, W["pool"], vec(pool_scale[l]), tr_big, "pool_mix_bwd")
        dup = _pool_apply(dpm, 0, BF16, n_ctx, pg, True, "pool_bwd")
        dy, dz, dnw = _gnorm_bwd(S["y2"], proj, vec(ssd_norm_w[l]), dyn, gs, tr, "gnorm_bwd")
        big_rows[("w_ssd_out", l)] = g_ssd.reshape(N_DEV, -1, D)
        big_rows[("pool_w", l)] = g_pool.reshape(len(POOL_WINDOWS), N_DEV, pg // N_DEV, pg).transpose(1, 0, 2, 3) \
            .reshape(N_DEV, -1, D)
        big_rows[("w_pool_out", l)] = g_po.reshape(N_DEV, -1, D)
        big_rows[("w_out", l)] = g_out.reshape(N_DEV, -1, D)
        big_rows[("w_gate_up", l)] = g_guT.reshape(N_DEV, -1, D)
        big_rows[("w_down", l)] = g_down.reshape(N_DEV, -1, D)
        riding = [] if pending is None else [pending]
        if len(rs_groups[l]) > 1:
            riding.append(((l, 0), reduce_pair(l, 0)))
        if not riding:
            dxs2, db2, dc2, ddt4, dlc4, dlr4 = _ssd_bwd(S["xbc"], dy, S["states"], *S["scan_ops"], G, n_ctx,
                                                        ssd_gpb_bwd, "ssd_bwd")
        else:
            red = jnp.concatenate([r for _, r in riding], axis=1)
            exch = _Exchange(red, jax.ShapeDtypeStruct(red.shape, BF16), _CHIP_SEMS, _chip_exchange_start,
                             _chip_exchange_finish)
            dxs2, db2, dc2, ddt4, dlc4, dlr4, got = _ssd_bwd(
                S["xbc"], dy, S["states"], *S["scan_ops"], G, n_ctx, ssd_gpb_bwd, "ssd_bwd_exchange", exch)
            at = 0
            for key, r in riding:
                slots[key] = got[:, at:at + r.shape[1]]
                at += r.shape[1]
        dskv = _pad_rows(jnp.repeat(d_skip[l], HEADDIM, axis=1), 8)
        cb = vec(conv_b[l])
        dxbc_x, dconv_x = _conv_bwd(proj, dinner // conv_tc, dxs2, 0, W["conv8"], cb, n_ctx, conv_tc, "conv_bwd_x",
                                    skip=(dy, dskv))
        dxbc_b, dconv_b = _conv_bwd(proj, 2 * dinner // conv_tc, db2, dinner // conv_tc, W["conv8"], cb, n_ctx,
                                    conv_tc, "conv_bwd_bc")
        dxbc_c, dconv_c = _conv_bwd(proj, (2 * dinner + GN) // conv_tc, dc2, (dinner + GN) // conv_tc, W["conv8"], cb,
                                    n_ctx, conv_tc, "conv_bwd_bc")
        dconv = jnp.concatenate([dconv_x, dconv_b, dconv_c], axis=1)
        ddt_raw, dtst = _dt_bwd(proj, dt_blk, pad128(dt_bias[l]), pad128(a_log[l]), from4(ddt4),
                                from4(dlc4 + dlr4.transpose(0, 1, 3, 2)), H, tr, "dt_bwd")
        dproj = jnp.concatenate([dz, dxbc_x, dxbc_b, dxbc_c, dup, dgl, ddt_raw,
                                 jnp.zeros((T, NP - 9 * D - 128), BF16)], axis=1)
        g_inT_new = _mm(dproj, S["h"], "tn", BF16,"mm_in_dw")
        g_inT = jnp.concatenate([g_inT_new[:6 * D], g_inT_new[9 * D:9 * D + 2 * H], g_inT_new[6 * D:9 * D]], axis=0)
        big_rows[("w_in", l)] = g_inT.reshape(N_DEV, -1, D)
        if len(rs_groups[l]) == 3:
            red = reduce_pair(l, 1)
            dh, slots[(l, 1)] = _mm(dproj, W["inT"], "nn", F32, "mm_in_dx_exchange", exch=_Exchange(
                red, jax.ShapeDtypeStruct(red.shape, BF16), _CHIP_SEMS, _chip_exchange_start, _chip_exchange_finish))
        else:
            dh = _mm(dproj, W["inT"], "nn", F32, "mm_in_dx")
        dx0, st1 = _norm_mod_bwd(S["x0"], dh, dx1, vec(g_mix[l]), m6, 1, n_ctx_tiles, tr, "norm_mod_bwd")
        dm6 = _pad_rows(jnp.concatenate([st1[0:2], st1[2:4], dga1[0:2], st2[0:2], st2[2:4], dga2[0:2]], axis=1), 8)
        dsil = _mm(dm6, W["adaT"], "nn", F32, "mm_ada_dx")
        sil_b, dcc, dbada = _ada_bwd_small(cc8, dsil, dm6, "ada_bwd_small")
        g_adaT = _mm(dm6, sil_b, "tn", BF16,"mm_ada_dw")
        d_c_ctx = d_c_ctx + dcc[1]
        dx = dx0

        big_rows[("w_ada", l)] = g_adaT.reshape(N_DEV, -1, D)
        big_rows[("conv_w", l)] = jnp.pad(
            dconv[:CONV_K].reshape(CONV_K, N_DEV, xbc_w // N_DEV).transpose(1, 0, 2),
            ((0, 0), (0, 16 - CONV_K), (0, 0))).reshape(N_DEV, -1, D)
        small_g["b_ada"][l] = dbada[0]
        small_g["g_mix"][l] = st1[4]
        small_g["conv_b"][l] = dconv[CONV_K]
        small_g["dt_bias"][l] = dtst[0, :2 * H].reshape(2, H)
        small_g["a_log"][l] = dtst[1, :2 * H].reshape(2, H)
        dsk_h = dconv_x[CONV_K + 1].reshape(H, HEADDIM).sum(axis=-1)
        small_g["d_skip"][l] = jnp.stack([dsk_h, dsk_h])
        small_g["ssd_norm_w"][l] = dnw[0]
        small_g["pool_scale"][l] = dps[0]
        small_g["g_ffn"][l] = st2[4]

        last_gi = len(rs_groups[l]) - 1
        pending = ((l, last_gi), reduce_pair(l, last_gi))
    slots[pending[0]] = _chip_exchange(pending[1], "rs_chip_exchange")
    g_local = {key: _sum_slots(s, "rs_chip_add") for key, s in slots.items()}
    grad_x = dx[n_ctx:][None]

    def local_grad(name):
        outs = []
        for l in range(depth):
            gi = [name in grp for grp in rs_groups[l]].index(True)
            _, _, rows, _, off = rlays[l][gi].find(name, l)
            piece = g_local[(l, gi)][off:off + rows]
            if name in ("w_ada", "w_in", "w_gate_up"):
                piece = piece.T
            elif name == "conv_w":
                piece = piece.reshape(16, -1)[:CONV_K]
            elif name == "pool_w":
                piece = piece.reshape(weights[name].shape[1:])
            outs.append(piece)
        return jnp.stack(outs)

    grads = {n: local_grad(n) for n in big}

    small_full = {"c_ctx": d_c_ctx, "g_final": dgf[0]}
    for n in small:
        if n not in small_full:
            small_full[n] = jnp.stack(small_g[n])

    def pack_small(tree):
        flat = jnp.concatenate([tree[n].reshape(-1).astype(F32) for n in small])
        rows = _round_up(-(-flat.shape[0] // D), 8)
        return jnp.pad(flat, (0, rows * D - flat.shape[0])).reshape(rows, D)

    def unpack_small(buf):
        flat, out, off = buf.reshape(-1), {}, 0
        for n in small:
            sz = weights[n].size
            out[n] = flat[off:off + sz].reshape(weights[n].shape)
            off += sz
        return out

    g_small = _sum_slots(_all_gather(pack_small(small_full), "gather_small_grads"), "sum_small_grads")
    grads.update(unpack_small(g_small))

    delta, new_m, new_v = {}, {}, {}
    for n in big:
        shp = weights[n].shape
        d_, m_, v_ = _adamw(weights[n].reshape(-1, shp[-1]), grads[n].reshape(-1, shp[-1]),
                            moms_m[n].reshape(-1, shp[-1]), moms_v[n].reshape(-1, shp[-1]), "adamw_" + n)
        delta[n], new_m[n], new_v[n] = d_.reshape(shp), m_.reshape(shp), v_.reshape(shp)
    d_, m_, v_ = _adamw(pack_small(weights), g_small, pack_small(moms_m), pack_small(moms_v), "adamw_small")
    delta.update(unpack_small(d_))
    new_m.update(unpack_small(m_))
    new_v.update(unpack_small(v_))

    return (loss, grad_x, *[grads[n] for n in order], *[delta[n] for n in order],
            *[new_m[n] for n in order], *[new_v[n] for n in order])
```

```python
import functools

import jax
import jax.numpy as jnp
from jax import lax
from jax.experimental import pallas as pl
from jax.experimental.pallas import tpu as pltpu

F32 = jnp.float32
BF16 = jnp.bfloat16
N_DEV = 8
EPS = 1e-6
GRID_W = 64
POOL_WINDOWS = (2, 4, 8, 16)
HEADDIM = 64
STATE = 128
CHUNK = 128
HPG = 4
CONV_K = 5
ADAM_LR, ADAM_B1, ADAM_B2, ADAM_EPS, ADAM_WD, ADAM_STEP = 0.001, 0.9, 0.999, 1e-08, 0.01, 10
NEG_BIG = -1e30
MESH_AXES = ("x", "y", "c")
ANY = pl.BlockSpec(memory_space=pl.ANY)


def _tile(n, cands):
    for t in cands:
        if n % t == 0:
            return t
    return n


def _round_up(n, m):
    return -(-n // m) * m


def _silu(x):
    return x * jax.nn.sigmoid(x)


def _dsilu(x):
    s = jax.nn.sigmoid(x)
    return s * (1.0 + x * (1.0 - s))


def _cparams(sem):
    return pltpu.CompilerParams(dimension_semantics=sem, vmem_limit_bytes=56 * 1024 * 1024)


def _mm(a, b, mode, out_dtype, name, exch=None):
    if mode == "tn":
        K, M = a.shape
        N = b.shape[1]
        tm = _tile(M, (512, 256, 128))
        tn = _tile(N, (1024, 512, 256, 128))
        tk = K if K <= 4608 else _tile(K, (1088, 544, 512, 256, 128))
        a_spec = pl.BlockSpec((tk, tm), lambda i, j, k: (k, i))
        b_spec = pl.BlockSpec((tk, tn), lambda i, j, k: (k, j))
        dims = (((0,), (0,)), ((), ()))
    else:
        M, K = a.shape
        N = b.shape[0] if mode == "nt" else b.shape[1]
        tm = _tile(M, (1088, 544, 512, 256, 128))
        tk = K if K <= 4096 else max(t for t in range(128, 2817, 128) if K % t == 0)
        tn = _tile(N, (512, 256, 128)) if tk == K else _tile(N, (1024, 512, 256, 128))
        a_spec = pl.BlockSpec((tm, tk), lambda i, j, k: (i, k))
        if mode == "nt":
            b_spec = pl.BlockSpec((tn, tk), lambda i, j, k: (j, k))
            dims = (((1,), (1,)), ((), ()))
        else:
            b_spec = pl.BlockSpec((tk, tn), lambda i, j, k: (k, j))
            dims = (((1,), (0,)), ((), ()))
    nk = K // tk

    def body(a_ref, b_ref, o_ref, *acc):
        if nk == 1:
            o_ref[...] = lax.dot_general(a_ref[...].astype(BF16), b_ref[...].astype(BF16), dims,
                                         preferred_element_type=F32).astype(o_ref.dtype)
            return
        k = pl.program_id(2)

        @pl.when(k == 0)
        def _():
            acc[0][...] = jnp.zeros_like(acc[0])

        acc[0][...] += lax.dot_general(a_ref[...].astype(BF16), b_ref[...].astype(BF16), dims,
                                       preferred_element_type=F32)

        @pl.when(k == nk - 1)
        def _():
            o_ref[...] = acc[0][...].astype(o_ref.dtype)

    call = dict(name=name, grid=(M // tm, N // tn, nk), in_specs=[a_spec, b_spec],
                scratch_shapes=[pltpu.VMEM((tm, tn), F32)] if nk > 1 else [])
    o_spec, o_sds = pl.BlockSpec((tm, tn), lambda i, j, k: (i, j)), jax.ShapeDtypeStruct((M, N), out_dtype)
    if exch is not None:
        return _call_with_exchange(body, exch, out_specs=[o_spec], out_shape=[o_sds], operands=(a, b), **call)
    return pl.pallas_call(body, out_specs=o_spec, out_shape=o_sds,
                          compiler_params=_cparams(("parallel", "parallel", "arbitrary")), **call)(a, b)


def _ada_fwd(cc8, w_adaT, b_ada, name):
    D = cc8.shape[1]
    N = w_adaT.shape[0]
    tn = _tile(N, (512, 256, 128))

    def body(c_ref, w_ref, b_ref, o_ref):
        a = _silu(c_ref[...]).astype(BF16)
        o_ref[...] = lax.dot_general(a, w_ref[...], (((1,), (1,)), ((), ())),
                                     preferred_element_type=F32) + b_ref[...]

    return pl.pallas_call(
        body, name=name, grid=(N // tn,),
        in_specs=[pl.BlockSpec((8, D), lambda j: (0, 0)), pl.BlockSpec((tn, D), lambda j: (j, 0)),
                  pl.BlockSpec((1, tn), lambda j: (0, j))],
        out_specs=pl.BlockSpec((8, tn), lambda j: (0, j)),
        out_shape=jax.ShapeDtypeStruct((8, N), F32),
        compiler_params=_cparams(("parallel",)),
    )(cc8, w_adaT, b_ada)


def _ada_bwd_small(cc8, dsil, dm6, name):
    D = cc8.shape[1]
    N = dm6.shape[1]

    def body(c_ref, ds_ref, dm_ref, sil_ref, dc_ref, db_ref):
        c = c_ref[...]
        sil_ref[...] = _silu(c).astype(BF16)
        dc_ref[...] = ds_ref[...] * _dsilu(c)
        dm = dm_ref[...]
        row = lax.broadcasted_iota(jnp.int32, dm.shape, 0)
        db_ref[...] = jnp.where(row == 0, jnp.sum(dm, axis=0, keepdims=True), 0.0)

    return pl.pallas_call(
        body, name=name, grid=(1,),
        in_specs=[pl.BlockSpec((8, D), lambda i: (0, 0)), pl.BlockSpec((8, D), lambda i: (0, 0)),
                  pl.BlockSpec((8, N), lambda i: (0, 0))],
        out_specs=[pl.BlockSpec((8, D), lambda i: (0, 0)), pl.BlockSpec((8, D), lambda i: (0, 0)),
                   pl.BlockSpec((8, N), lambda i: (0, 0))],
        out_shape=[jax.ShapeDtypeStruct((8, D), BF16), jax.ShapeDtypeStruct((8, D), F32),
                   jax.ShapeDtypeStruct((8, N), F32)],
        compiler_params=_cparams(("arbitrary",)),
    )(cc8, dsil, dm6)


def _seg_pick(m_ref, is_ctx):
    return jnp.where(is_ctx, m_ref[1:2, :], m_ref[0:1, :])


def _norm_mod(x, g, m6, sh_idx, sc_idx, n_ctx_tiles, tr, name, resid=None):
    T, D = x.shape
    row = pl.BlockSpec((tr, D), lambda i: (i, 0))
    vec = pl.BlockSpec((1, D), lambda i: (0, 0))

    def mcol(idx):
        return pl.BlockSpec((8, D), lambda i: (0, idx))

    def body(*refs):
        if resid is None:
            x_ref, g_ref, sh_ref, sc_ref, h_ref = refs
            xv = x_ref[...]
        else:
            x_ref, f_ref, ga_ref, g_ref, sh_ref, sc_ref, xo_ref, h_ref = refs
        is_ctx = pl.program_id(0) < n_ctx_tiles
        if resid is not None:
            xv = x_ref[...] + _seg_pick(ga_ref, is_ctx) * f_ref[...]
            xo_ref[...] = xv
        rstd = lax.rsqrt(jnp.mean(xv * xv, axis=-1, keepdims=True) + EPS)
        hn = xv * rstd * g_ref[...]
        h_ref[...] = (hn * (1.0 + _seg_pick(sc_ref, is_ctx)) + _seg_pick(sh_ref, is_ctx)).astype(BF16)

    if resid is None:
        ins, in_specs = [x, g, m6, m6], [row, vec, mcol(sh_idx), mcol(sc_idx)]
        out_specs, out_shape = row, jax.ShapeDtypeStruct((T, D), BF16)
    else:
        f, ga_idx = resid
        ins = [x, f, m6, g, m6, m6]
        in_specs = [row, row, mcol(ga_idx), vec, mcol(sh_idx), mcol(sc_idx)]
        out_specs = [row, row]
        out_shape = [jax.ShapeDtypeStruct((T, D), F32), jax.ShapeDtypeStruct((T, D), BF16)]
    return pl.pallas_call(body, name=name, grid=(T // tr,), in_specs=in_specs, out_specs=out_specs,
                          out_shape=out_shape, compiler_params=_cparams(("parallel",)))(*ins)


def _resid(x, f, m6, ga_idx, n_ctx_tiles, tr, name):
    T, D = x.shape
    row = pl.BlockSpec((tr, D), lambda i: (i, 0))

    def body(x_ref, f_ref, ga_ref, o_ref):
        is_ctx = pl.program_id(0) < n_ctx_tiles
        o_ref[...] = x_ref[...] + _seg_pick(ga_ref, is_ctx) * f_ref[...]

    return pl.pallas_call(body, name=name, grid=(T // tr,),
                          in_specs=[row, row, pl.BlockSpec((8, D), lambda i: (0, ga_idx))], out_specs=row,
                          out_shape=jax.ShapeDtypeStruct((T, D), F32),
                          compiler_params=_cparams(("parallel",)))(x, f, m6)


def _resid_bwd(dx, f, m6, ga_idx, n_ctx_tiles, tr, name):
    T, D = dx.shape
    row = pl.BlockSpec((tr, D), lambda i: (i, 0))
    acc = pl.BlockSpec((8, D), lambda i: (0, 0))

    def body(dx_ref, f_ref, ga_ref, df_ref, dga_ref):
        i = pl.program_id(0)
        is_ctx = i < n_ctx_tiles

        @pl.when(i == 0)
        def _():
            dga_ref[...] = jnp.zeros_like(dga_ref)

        dxv = dx_ref[...]
        df_ref[...] = (_seg_pick(ga_ref, is_ctx) * dxv).astype(BF16)
        s = jnp.sum(dxv * f_ref[...], axis=0, keepdims=True)
        r = lax.broadcasted_iota(jnp.int32, (8, D), 0)
        dga_ref[...] += jnp.where(r == jnp.where(is_ctx, 1, 0), s, 0.0)

    return pl.pallas_call(body, name=name, grid=(T // tr,),
                          in_specs=[row, row, pl.BlockSpec((8, D), lambda i: (0, ga_idx))],
                          out_specs=[row, acc],
                          out_shape=[jax.ShapeDtypeStruct((T, D), BF16), jax.ShapeDtypeStruct((8, D), F32)],
                          compiler_params=_cparams(("arbitrary",)))(dx, f, m6)


def _norm_mod_bwd(x, dh, dxres, g, m6, sc_idx, n_ctx_tiles, tr, name):
    T, D = x.shape
    row = pl.BlockSpec((tr, D), lambda i: (i, 0))
    acc = pl.BlockSpec((8, D), lambda i: (0, 0))

    def body(x_ref, dh_ref, dr_ref, g_ref, sc_ref, dx_ref, st_ref):
        i = pl.program_id(0)
        is_ctx = i < n_ctx_tiles

        @pl.when(i == 0)
        def _():
            st_ref[...] = jnp.zeros_like(st_ref)

        xv, dh_v, gv = x_ref[...], dh_ref[...], g_ref[...]
        sc1 = 1.0 + _seg_pick(sc_ref, is_ctx)
        rstd = lax.rsqrt(jnp.mean(xv * xv, axis=-1, keepdims=True) + EPS)
        xhat = xv * rstd
        dxhat = dh_v * sc1 * gv
        dx_ref[...] = dr_ref[...] + rstd * (dxhat - xhat * jnp.mean(dxhat * xhat, axis=-1, keepdims=True))
        dsh = jnp.sum(dh_v, axis=0, keepdims=True)
        dsc = jnp.sum(dh_v * xhat * gv, axis=0, keepdims=True)
        dg = jnp.sum(dh_v * sc1 * xhat, axis=0, keepdims=True)
        r = lax.broadcasted_iota(jnp.int32, (8, D), 0)
        seg = jnp.where(is_ctx, 1, 0)
        st_ref[...] += (jnp.where(r == seg, dsh, 0.0) + jnp.where(r == 2 + seg, dsc, 0.0)
                        + jnp.where(r == 4, dg, 0.0))

    return pl.pallas_call(body, name=name, grid=(T // tr,),
                          in_specs=[row, row, row, pl.BlockSpec((1, D), lambda i: (0, 0)),
                                    pl.BlockSpec((8, D), lambda i: (0, sc_idx))],
                          out_specs=[row, acc],
                          out_shape=[jax.ShapeDtypeStruct((T, D), F32), jax.ShapeDtypeStruct((8, D), F32)],
                          compiler_params=_cparams(("arbitrary",)))(x, dh, dxres, g, m6)


def _loss_head(x, tgt, g, n_ctx_tiles, tr, name):
    T, D = x.shape
    row = pl.BlockSpec((tr, D), lambda i: (i, 0))

    def body(x_ref, t_ref, g_ref, l_ref, dx_ref, dg_ref):
        i = pl.program_id(0)

        @pl.when(i == 0)
        def _():
            l_ref[...] = jnp.zeros_like(l_ref)
            dg_ref[...] = jnp.zeros_like(dg_ref)

        @pl.when(i < n_ctx_tiles)
        def _():
            dx_ref[...] = jnp.zeros_like(dx_ref)

        @pl.when(i >= n_ctx_tiles)
        def _():
            xv, gv = x_ref[...], g_ref[...]
            rstd = lax.rsqrt(jnp.mean(xv * xv, axis=-1, keepdims=True) + EPS)
            xhat = xv * rstd
            e = xhat * gv - t_ref[...]
            l_ref[...] += 0.5 * jnp.sum(jnp.mean(e * e, axis=-1, keepdims=True), axis=0, keepdims=True)
            dy = e * (1.0 / D)
            dxhat = dy * gv
            dx_ref[...] = rstd * (dxhat - xhat * jnp.mean(dxhat * xhat, axis=-1, keepdims=True))
            r = lax.broadcasted_iota(jnp.int32, (8, D), 0)
            dg_ref[...] += jnp.where(r == 0, jnp.sum(dy * xhat, axis=0, keepdims=True), 0.0)

    return pl.pallas_call(
        body, name=name, grid=(T // tr,),
        in_specs=[row, pl.BlockSpec((tr, D), lambda i: (jnp.maximum(i - n_ctx_tiles, 0), 0)),
                  pl.BlockSpec((1, D), lambda i: (0, 0))],
        out_specs=[pl.BlockSpec((8, 128), lambda i: (0, 0)), row, pl.BlockSpec((8, D), lambda i: (0, 0))],
        out_shape=[jax.ShapeDtypeStruct((8, 128), F32), jax.ShapeDtypeStruct((T, D), F32),
                   jax.ShapeDtypeStruct((8, D), F32)],
        compiler_params=_cparams(("arbitrary",)))(x, tgt, g)


def _seq_masks(T, n_ctx, width):
    row = lax.broadcasted_iota(jnp.int32, (T, width), 0)
    in_ctx = row < n_ctx
    return jnp.where(in_ctx, row, row - n_ctx), jnp.where(in_ctx, n_ctx, T - n_ctx)


def _shift_rows(u, off, t_loc, seg_len):
    T = u.shape[0]
    if off == 0:
        return u
    v = pltpu.roll(u, (-off) % T, 0)
    ok = (t_loc + off >= 0) & (t_loc + off < seg_len)
    return jnp.where(ok, v, 0.0)


def _conv_fwd(proj, col0_blk, ncol, conv_w8, conv_b, n_ctx, tc, name):
    T = proj.shape[0]

    def body(u_ref, w_ref, b_ref, o_ref):
        u = u_ref[...]
        t_loc, seg_len = _seq_masks(T, n_ctx, tc)
        acc = jnp.broadcast_to(b_ref[...], u.shape)
        for i in range(CONV_K):
            acc = acc + w_ref[i:i + 1, :] * _shift_rows(u, i - CONV_K // 2, t_loc, seg_len)
        o_ref[...] = _silu(acc)

    return pl.pallas_call(
        body, name=name, grid=(ncol // tc,),
        in_specs=[pl.BlockSpec((T, tc), lambda j: (0, col0_blk + j)), pl.BlockSpec((8, tc), lambda j: (0, j)),
                  pl.BlockSpec((1, tc), lambda j: (0, j))],
        out_specs=pl.BlockSpec((T, tc), lambda j: (0, j)),
        out_shape=jax.ShapeDtypeStruct((T, ncol), F32),
        compiler_params=_cparams(("parallel",)))(proj, conv_w8, conv_b)


def _conv_bwd(proj, col0_blk, d2, w_blk0, conv_w8, conv_b, n_ctx, tc, name, skip=None):
    T, ncol = d2.shape[1], d2.shape[2]

    def body(u_ref, d_ref, w_ref, b_ref, *rest):
        if skip is None:
            du_ref, dw_ref = rest
        else:
            dy_ref, k_ref, du_ref, dw_ref = rest
        u = u_ref[...]
        t_loc, seg_len = _seq_masks(T, n_ctx, tc)
        pre = jnp.broadcast_to(b_ref[...], u.shape)
        for i in range(CONV_K):
            pre = pre + w_ref[i:i + 1, :] * _shift_rows(u, i - CONV_K // 2, t_loc, seg_len)
        r = lax.broadcasted_iota(jnp.int32, (8, tc), 0)
        dact = d_ref[0] + d_ref[1]
        dw = jnp.zeros((8, tc), F32)
        if skip is not None:
            dyv = dy_ref[...]
            dact = dact + (k_ref[0:1, :] + k_ref[1:2, :]) * dyv
            dw = jnp.where(r == CONV_K + 1, jnp.sum(dyv * _silu(pre), axis=0, keepdims=True), 0.0)
        dpre = dact * _dsilu(pre)
        du = jnp.zeros_like(u)
        dw = dw + jnp.where(r == CONV_K, jnp.sum(dpre, axis=0, keepdims=True), 0.0)
        for i in range(CONV_K):
            off = i - CONV_K // 2
            du = du + w_ref[i:i + 1, :] * _shift_rows(dpre, -off, t_loc, seg_len)
            dw = dw + jnp.where(r == i, jnp.sum(dpre * _shift_rows(u, off, t_loc, seg_len), axis=0, keepdims=True),
                                0.0)
        du_ref[...] = du.astype(BF16)
        dw_ref[...] = dw

    col = pl.BlockSpec((T, tc), lambda j: (0, j))
    in_specs = [pl.BlockSpec((T, tc), lambda j: (0, col0_blk + j)), pl.BlockSpec((2, T, tc), lambda j: (0, 0, j)),
                pl.BlockSpec((8, tc), lambda j: (0, w_blk0 + j)), pl.BlockSpec((1, tc), lambda j: (0, w_blk0 + j))]
    operands = [proj, d2, conv_w8, conv_b]
    if skip is not None:
        in_specs += [col, pl.BlockSpec((8, tc), lambda j: (0, j))]
        operands += list(skip)
    return pl.pallas_call(
        body, name=name, grid=(ncol // tc,), in_specs=in_specs,
        out_specs=[col, pl.BlockSpec((8, tc), lambda j: (0, j))],
        out_shape=[jax.ShapeDtypeStruct((T, ncol), BF16), jax.ShapeDtypeStruct((8, ncol), F32)],
        compiler_params=_cparams(("parallel",)))(*operands)


def _pool_core(u, half, t_loc, seg_len, transpose):
    tr = u.shape[0]

    def shift(v, s):
        w = pltpu.roll(v, s % tr, 0)
        ok = (t_loc - s >= 0) & (t_loc - s < seg_len)
        return jnp.where(ok, w, 0.0)

    cnt = (jnp.minimum(t_loc, half) + jnp.minimum(seg_len - t_loc, half)).astype(F32)
    q = u / cnt if transpose else u
    back, ahead, h = q, q, 1
    while h < half:
        back = back + shift(back, h)
        ahead = ahead + shift(ahead, -h)
        h *= 2
    if transpose:
        tot = back + shift(ahead, -1)
        return tot - u
    tot = shift(back, 1) + ahead
    return tot / cnt - u


def _pool_apply(src, col0_blk, out_dtype, n_ctx, pg, transpose, name):
    T = src.shape[0]

    def body(u_ref, o_ref):
        gi = pl.program_id(0)
        row = lax.broadcasted_iota(jnp.int32, (T, pg), 0)
        seg_len = jnp.where(row < n_ctx, n_ctx, GRID_W)
        t_loc = row & (seg_len - 1)
        u = u_ref[...].astype(F32)
        for k_idx, k in enumerate(POOL_WINDOWS):
            @pl.when(gi == k_idx)
            def _(k=k):
                o_ref[...] = _pool_core(u, k // 2, t_loc, seg_len, transpose).astype(o_ref.dtype)

    return pl.pallas_call(
        body, name=name, grid=(len(POOL_WINDOWS),),
        in_specs=[pl.BlockSpec((T, pg), lambda gi: (0, col0_blk + gi))],
        out_specs=pl.BlockSpec((T, pg), lambda gi: (0, gi)),
        out_shape=jax.ShapeDtypeStruct((T, pg * len(POOL_WINDOWS)), out_dtype),
        compiler_params=_cparams(("parallel",)))(src)


def _pool_mix_fwd(pm, pool_w, pool_scale, tr, name):
    T, W = pm.shape
    ng, pg = pool_w.shape[0], pool_w.shape[1]

    def body(p_ref, w_ref, s_ref, o_ref):
        o_ref[...] = (jnp.dot(p_ref[...], w_ref[...], preferred_element_type=F32) * s_ref[...]).astype(BF16)

    return pl.pallas_call(
        body, name=name, grid=(T // tr, ng),
        in_specs=[pl.BlockSpec((tr, pg), lambda i, g: (i, g)), pl.BlockSpec((None, pg, pg), lambda i, g: (g, 0, 0)),
                  pl.BlockSpec((1, pg), lambda i, g: (0, g))],
        out_specs=pl.BlockSpec((tr, pg), lambda i, g: (i, g)),
        out_shape=jax.ShapeDtypeStruct((T, W), BF16),
        compiler_params=_cparams(("parallel", "parallel")))(pm, pool_w, pool_scale)


def _pool_mix_bwd(pm, dpms, pool_w, pool_scale, tr, name):
    T, W = pm.shape
    ng, pg = pool_w.shape[0], pool_w.shape[1]

    def body(p_ref, d_ref, w_ref, s_ref, dp_ref, dw_ref, ds_ref):
        i = pl.program_id(1)

        @pl.when(i == 0)
        def _():
            dw_ref[...] = jnp.zeros_like(dw_ref)
            ds_ref[...] = jnp.zeros_like(ds_ref)

        p, w = p_ref[...], w_ref[...]
        d = d_ref[...].astype(F32)
        pmix = jnp.dot(p, w, preferred_element_type=F32)
        r = lax.broadcasted_iota(jnp.int32, (8, pg), 0)
        ds_ref[...] += jnp.where(r == 0, jnp.sum(d * pmix, axis=0, keepdims=True), 0.0)
        dmix = (d * s_ref[...]).astype(BF16)
        dp_ref[...] = lax.dot_general(dmix, w, (((1,), (1,)), ((), ())), preferred_element_type=F32)
        dw_ref[...] += lax.dot_general(p, dmix, (((0,), (0,)), ((), ())), preferred_element_type=F32)

    return pl.pallas_call(
        body, name=name, grid=(ng, T // tr),
        in_specs=[pl.BlockSpec((tr, pg), lambda g, i: (i, g)), pl.BlockSpec((tr, pg), lambda g, i: (i, g)),
                  pl.BlockSpec((None, pg, pg), lambda g, i: (g, 0, 0)), pl.BlockSpec((1, pg), lambda g, i: (0, g))],
        out_specs=[pl.BlockSpec((tr, pg), lambda g, i: (i, g)), pl.BlockSpec((None, pg, pg), lambda g, i: (g, 0, 0)),
                   pl.BlockSpec((8, pg), lambda g, i: (0, g))],
        out_shape=[jax.ShapeDtypeStruct((T, W), F32), jax.ShapeDtypeStruct((ng, pg, pg), F32),
                   jax.ShapeDtypeStruct((8, W), F32)],
        compiler_params=_cparams(("parallel", "arbitrary")))(pm, dpms, pool_w, pool_scale)


def _chunk_cumsum(v, upper):
    Q = v.shape[0]
    ii = lax.broadcasted_iota(jnp.int32, (Q, Q), 0)
    jj = lax.broadcasted_iota(jnp.int32, (Q, Q), 1)
    tri = ((jj >= ii) if upper else (jj <= ii)).astype(BF16)
    h1 = v.astype(BF16)
    r1 = v - h1.astype(F32)
    h2 = r1.astype(BF16)
    h3 = (r1 - h2.astype(F32)).astype(BF16)
    return (jnp.dot(tri, h1, preferred_element_type=F32) + jnp.dot(tri, h2, preferred_element_type=F32)
            + jnp.dot(tri, h3, preferred_element_type=F32))


def _split3(v):
    h1 = v.astype(BF16)
    r1 = v - h1.astype(F32)
    h2 = r1.astype(BF16)
    return h1, h2, (r1 - h2.astype(F32)).astype(BF16)


def _dt_prep(proj, dt_blk, bias, a_log, expand, n_heads, name):
    T = proj.shape[0]
    Wd = expand.shape[1]
    Q = CHUNK
    cps = 2 if (T // Q) % 2 == 0 else 1
    row = pl.BlockSpec((cps * Q, 128), lambda i: (i, 0))
    wide = pl.BlockSpec((cps * Q, Wd), lambda i: (i, 0))

    def body(r_ref, b_ref, al_ref, e_ref, l1_ref, l2_ref, l3_ref, dtb_ref, ein_ref, dte_ref, etot_ref):
        xv = r_ref[...] + b_ref[...]
        dt = jnp.maximum(xv, 0.0) + jnp.log(1.0 + jnp.exp(-jnp.abs(xv)))
        a_all = -jnp.exp(al_ref[...]) * dt
        fwd_col = lax.broadcasted_iota(jnp.int32, (Q, 128), 1) < n_heads
        lams, rests = [], []
        for c in range(cps):
            a = a_all[c * Q:(c + 1) * Q]
            lam_c = jnp.where(fwd_col, _chunk_cumsum(a, False), _chunk_cumsum(a, True))
            tot_c = jnp.where(fwd_col[0:1], lam_c[Q - 1:Q], lam_c[0:1])
            etot_ref[8 * c:8 * (c + 1), :] = jnp.broadcast_to(jnp.exp(tot_c), (8, 128))
            lams.append(lam_c)
            rests.append(tot_c - lam_c)
        lam = lams[0] if cps == 1 else jnp.concatenate(lams, axis=0)
        rest = rests[0] if cps == 1 else jnp.concatenate(rests, axis=0)
        l1_ref[...], l2_ref[...], l3_ref[...] = _split3(lam)
        ex = e_ref[...]

        def rep(v):
            p1, p2, p3 = _split3(v)
            return (jnp.dot(p1, ex, preferred_element_type=F32) + jnp.dot(p2, ex, preferred_element_type=F32)
                    + jnp.dot(p3, ex, preferred_element_type=F32))

        dtb_ref[...] = rep(dt)
        ein_ref[...] = rep(jnp.exp(lam))
        dte_ref[...] = rep(jnp.exp(rest))

    vec = pl.BlockSpec((1, 128), lambda i: (0, 0))
    return pl.pallas_call(
        body, name=name, grid=(T // (cps * Q),),
        in_specs=[pl.BlockSpec((cps * Q, 128), lambda i: (i, dt_blk)), vec, vec,
                  pl.BlockSpec((128, Wd), lambda i: (0, 0))],
        out_specs=[row, row, row, wide, wide, wide, pl.BlockSpec((8 * cps, 128), lambda i: (i, 0))],
        out_shape=[jax.ShapeDtypeStruct((T, 128), BF16)] * 3 + [jax.ShapeDtypeStruct((T, Wd), F32)] * 3
        + [jax.ShapeDtypeStruct((T // Q * 8, 128), F32)],
        compiler_params=_cparams(("parallel",)))(proj, bias, a_log, expand)


def _dt_bwd(proj, dt_blk, bias, a_log, ddt, dlam, n_heads, tr, name):
    T = proj.shape[0]
    row = pl.BlockSpec((tr, 128), lambda i: (i, 0))
    vec = pl.BlockSpec((1, 128), lambda i: (0, 0))

    def body(r_ref, b_ref, al_ref, ddt_ref, dl_ref, o_ref, st_ref):
        @pl.when(pl.program_id(0) == 0)
        def _():
            st_ref[...] = jnp.zeros_like(st_ref)

        xv = r_ref[...] + b_ref[...]
        dt = jnp.maximum(xv, 0.0) + jnp.log(1.0 + jnp.exp(-jnp.abs(xv)))
        a_neg = -jnp.exp(al_ref[...])
        col = lax.broadcasted_iota(jnp.int32, (CHUNK, 128), 1)
        dl = dl_ref[...]
        parts = []
        for k in range(tr // CHUNK):
            dk = dl[k * CHUNK:(k + 1) * CHUNK]
            parts.append(jnp.where(col < n_heads, _chunk_cumsum(dk, True), _chunk_cumsum(dk, False)))
        dav = jnp.concatenate(parts, axis=0)
        draw = (ddt_ref[...] + dav * a_neg) * jax.nn.sigmoid(xv)
        o_ref[...] = draw.astype(BF16)
        r = lax.broadcasted_iota(jnp.int32, (8, 128), 0)
        st_ref[...] += (jnp.where(r == 0, jnp.sum(draw, axis=0, keepdims=True), 0.0)
                        + jnp.where(r == 1, jnp.sum(dav * dt, axis=0, keepdims=True) * a_neg, 0.0))

    return pl.pallas_call(
        body, name=name, grid=(T // tr,),
        in_specs=[pl.BlockSpec((tr, 128), lambda i: (i, dt_blk)), vec, vec, row, row],
        out_specs=[row, pl.BlockSpec((8, 128), lambda i: (0, 0))],
        out_shape=[jax.ShapeDtypeStruct((T, 128), BF16), jax.ShapeDtypeStruct((8, 128), F32)],
        compiler_params=_cparams(("arbitrary",)))(proj, bias, a_log, ddt, dlam)


def _scan_chunk(d, pos, nc_ctx, nc):
    rev = jnp.where(pos < nc_ctx, nc_ctx - 1 - pos, nc - 1 - (pos - nc_ctx))
    return jnp.where(d == 0, pos, rev)


def _chunk_mask(d):
    ii = lax.broadcasted_iota(jnp.int32, (CHUNK, CHUNK), 0)
    jj = lax.broadcasted_iota(jnp.int32, (CHUNK, CHUNK), 1)
    return (ii - jj) * jnp.where(d == 0, 1, -1) >= 0


def _ssd_specs(T, G, n_ctx, gpb, chunk_of):
    R, P, N, Q = HPG, HEADDIM, STATE, CHUNK
    H = G * R
    nc, nc_ctx = T // Q, n_ctx // Q
    xw, bw = gpb * R * P, gpb * N
    b_blk0 = (H * P) // bw
    c_blk0 = b_blk0 + G // gpb

    def ch(d, s):
        return chunk_of(d, s, nc_ctx, nc)

    return dict(
        x=pl.BlockSpec((Q, xw), lambda d, g, s: (ch(d, s), g)),
        b=pl.BlockSpec((Q, bw), lambda d, g, s: (ch(d, s), b_blk0 + g)),
        c=pl.BlockSpec((Q, bw), lambda d, g, s: (ch(d, s), c_blk0 + g)),
        col=pl.BlockSpec((None, gpb, Q, R), lambda d, g, s: (d, g, ch(d, s), 0)),
        row=pl.BlockSpec((None, gpb, R, Q), lambda d, g, s: (d, g, 0, ch(d, s))),
        rep=pl.BlockSpec((Q, xw), lambda d, g, s: (ch(d, s), d * (G // gpb) + g)),
        lam_a=pl.BlockSpec((None, gpb * R, 16, Q), lambda d, g, s: (d, g, 0, ch(d, s))),
        lam_b=pl.BlockSpec((None, gpb * R, 16, Q), lambda d, g, s: (d, g, 0, ch(d, s))),
        etot=pl.BlockSpec((None, gpb, None, 8, 128), lambda d, g, s: (d, g, ch(d, s), 0, 0)),
        dsk=pl.BlockSpec((None, 8, xw), lambda d, g, s: (d, 0, g)),
        xd=pl.BlockSpec((None, Q, xw), lambda d, g, s: (d, ch(d, s), g)),
        bd=pl.BlockSpec((None, Q, bw), lambda d, g, s: (d, ch(d, s), g)),
        st=pl.BlockSpec((None, None, gpb * R // 2, 2 * P, N), lambda d, g, s: (d, ch(d, s), g, 0, 0)),
    )


class _Exchange:
    def __init__(self, operand, out_sds, sems, start, finish):
        self.operand, self.out_sds, self.sems, self.start, self.finish = operand, out_sds, sems, start, finish


def _call_with_exchange(body, exch, *, name, grid, in_specs, out_specs, out_shape, scratch_shapes, operands):
    if exch is None:
        return pl.pallas_call(body, name=name, grid=grid, in_specs=in_specs, out_specs=out_specs,
                              out_shape=out_shape, scratch_shapes=scratch_shapes,
                              compiler_params=_cparams(("arbitrary",) * len(grid)))(*operands)
    n_in, n_out, n_scr = len(in_specs), len(out_specs), len(scratch_shapes)

    def fused(*refs):
        ins, c_in = refs[:n_in], refs[n_in]
        outs, c_out = refs[n_in + 1:n_in + 1 + n_out], refs[n_in + 1 + n_out]
        scr = refs[n_in + 2 + n_out:n_in + 2 + n_out + n_scr]
        sems = refs[n_in + 2 + n_out + n_scr:]
        ids = [pl.program_id(a) for a in range(len(grid))]
        first = functools.reduce(lambda p, q: p & q, [i == 0 for i in ids])
        last = functools.reduce(lambda p, q: p & q, [i == n - 1 for i, n in zip(ids, grid)])

        @pl.when(first)
        def _():
            exch.start(c_in, c_out, *sems)

        body(*ins, *outs, *scr)

        @pl.when(last)
        def _():
            exch.finish(c_in, c_out, *sems)

    return pl.pallas_call(fused, name=name, grid=grid, in_specs=list(in_specs) + [ANY],
                          out_specs=list(out_specs) + [ANY], out_shape=list(out_shape) + [exch.out_sds],
                          scratch_shapes=list(scratch_shapes) + list(exch.sems),
                          compiler_params=_cparams(("arbitrary",) * len(grid)))(*operands, exch.operand)


def _ssd_fwd(xbc, dtb, ein, dte, lam_a, lam_b, etot, dsk, G, n_ctx, gpb, name, exch=None):
    T = xbc.shape[0]
    R, P, N, Q = HPG, HEADDIM, STATE, CHUNK
    H = G * R
    nc = T // Q
    sp = _ssd_specs(T, G, n_ctx, gpb, _scan_chunk)

    def body(x_ref, b_ref, c_ref, dt_ref, ein_ref, dte_ref, la_ref, lb_ref, et_ref, dsk_ref, y_ref, st_ref, S):
        d, s = pl.program_id(0), pl.program_id(2)

        @pl.when(s == 0)
        def _():
            S[...] = jnp.zeros_like(S)

        mask = _chunk_mask(d)
        head0 = lax.broadcasted_iota(jnp.int32, (Q, 2 * P), 1) < P
        rows0 = lax.broadcasted_iota(jnp.int32, (2 * P, N), 0) < P
        for gg in range(gpb):
            Bm = b_ref[:, gg * N:(gg + 1) * N].astype(BF16)
            Cm = c_ref[:, gg * N:(gg + 1) * N].astype(BF16)
            Gm = lax.dot_general(Cm, Bm, (((1,), (1,)), ((), ())), preferred_element_type=F32)
            for k in range(R // 2):
                pk = gg * (R // 2) + k
                sl = slice(pk * 2 * P, (pk + 1) * 2 * P)
                xp = x_ref[:, sl]
                xc = xp * dt_ref[:, sl]
                s_in = S[pk]
                y = lax.dot_general(Cm, s_in.astype(BF16), (((1,), (1,)), ((), ())),
                                    preferred_element_type=F32) * ein_ref[:, sl] + dsk_ref[0:1, sl] * xp
                for j in range(2):
                    hr = 2 * pk + j
                    diff = lax.dot_general(la_ref[hr], lb_ref[hr], (((0,), (0,)), ((), ())), preferred_element_type=F32)
                    ldec = jnp.exp(jnp.where(mask, diff, NEG_BIG))
                    xc_j = (jnp.where(head0, xc, 0.0) if j == 0 else jnp.where(head0, 0.0, xc)).astype(BF16)
                    y = y + jnp.dot((Gm * ldec).astype(BF16), xc_j, preferred_element_type=F32)
                y_ref[:, sl] = y
                st_ref[pk] = s_in
                e_all = jnp.where(rows0, et_ref[gg, 2 * k:2 * k + 1, :], et_ref[gg, 2 * k + 1:2 * k + 2, :])
                xd = (xc * dte_ref[:, sl]).astype(BF16)
                S[pk] = e_all * s_in + lax.dot_general(xd, Bm, (((0,), (0,)), ((), ())),
                                                       preferred_element_type=F32)

    return _call_with_exchange(
        body, exch, name=name, grid=(2, G // gpb, nc),
        in_specs=[sp["x"], sp["b"], sp["c"], sp["rep"], sp["rep"], sp["rep"], sp["lam_a"], sp["lam_b"], sp["etot"],
                  sp["dsk"]],
        out_specs=[sp["xd"], sp["st"]],
        out_shape=[jax.ShapeDtypeStruct((2, T, H * P), F32), jax.ShapeDtypeStruct((2, nc, H // 2, 2 * P, N), F32)],
        scratch_shapes=[pltpu.VMEM((gpb * R // 2, 2 * P, N), F32)],
        operands=(xbc, xbc, xbc, dtb, ein, dte, lam_a, lam_b, etot, dsk))


def _ssd_bwd(xbc, dy, states, dtb, ein, dte, lam_a, lam_b, etot, G, n_ctx, gpb, name, exch=None):
    T = xbc.shape[0]
    R, P, N, Q = HPG, HEADDIM, STATE, CHUNK
    H = G * R
    nc = T // Q
    sp = _ssd_specs(T, G, n_ctx, gpb, lambda d, s, nc_ctx, n: _scan_chunk(d, n - 1 - s, nc_ctx, n))

    def body(x_ref, b_ref, c_ref, dy_ref, st_ref, dt_ref, ein_ref, dte_ref, la_ref, lb_ref, et_ref,
             dx_ref, db_ref, dc_ref, ddt_ref, dlc_ref, dlr_ref, dS):
        d, s = pl.program_id(0), pl.program_id(2)

        @pl.when(s == 0)
        def _():
            dS[...] = jnp.zeros_like(dS)

        mask = _chunk_mask(d)
        ri = lax.broadcasted_iota(jnp.int32, (Q, 1), 0)
        is_last = ri == jnp.where(d == 0, Q - 1, 0)
        head0 = lax.broadcasted_iota(jnp.int32, (Q, 2 * P), 1) < P
        rows0 = lax.broadcasted_iota(jnp.int32, (2 * P, N), 0) < P

        def total(v):
            return jnp.sum(jnp.sum(v, axis=1, keepdims=True), axis=0, keepdims=True)

        for gg in range(gpb):
            Bm = b_ref[:, gg * N:(gg + 1) * N].astype(BF16)
            Cm = c_ref[:, gg * N:(gg + 1) * N].astype(BF16)
            Gm = lax.dot_general(Cm, Bm, (((1,), (1,)), ((), ())), preferred_element_type=F32)
            dG = jnp.zeros((Q, Q), F32)
            dB = jnp.zeros((Q, N), F32)
            dC = jnp.zeros((Q, N), F32)
            for k in range(R // 2):
                pk = gg * (R // 2) + k
                sl = slice(pk * 2 * P, (pk + 1) * 2 * P)
                e_in = ein_ref[:, sl]
                dte = dte_ref[:, sl]
                e_all = jnp.where(rows0, et_ref[gg, 2 * k:2 * k + 1, :], et_ref[gg, 2 * k + 1:2 * k + 2, :])
                xp = x_ref[:, sl]
                dtp = dt_ref[:, sl]
                xc = xp * dtp
                xc_b = xc.astype(BF16)
                dyp = dy_ref[:, sl]
                s_in = st_ref[pk]
                s_in_b = s_in.astype(BF16)
                ds_out = dS[pk]
                ds_out_b = ds_out.astype(BF16)
                y_int = lax.dot_general(Cm, s_in_b, (((1,), (1,)), ((), ())), preferred_element_type=F32) * e_in
                b_ds = lax.dot_general(Bm, ds_out_b, (((1,), (1,)), ((), ())), preferred_element_type=F32)
                dxc = dte * b_ds
                u = xc * dxc
                v = dyp * y_int - u
                sse = ds_out * s_in * e_all
                for j in range(2):
                    hr, r = 2 * pk + j, 2 * k + j

                    def pick(a, m0=head0, j=j):
                        return jnp.where(m0, a, 0.0) if j == 0 else jnp.where(m0, 0.0, a)

                    diff = lax.dot_general(la_ref[hr], lb_ref[hr], (((0,), (0,)), ((), ())), preferred_element_type=F32)
                    ldec = jnp.exp(jnp.where(mask, diff, NEG_BIG))
                    dy_j = pick(dyp).astype(BF16)
                    dM = lax.dot_general(dy_j, xc_b, (((1,), (1,)), ((), ())), preferred_element_type=F32)
                    dMl = dM * ldec
                    Wm = dMl * Gm
                    dlam_c = jnp.sum(Wm, axis=1, keepdims=True) + jnp.sum(pick(v), axis=1, keepdims=True)
                    last = total(pick(sse, rows0)) + total(pick(u))
                    dlc_ref[gg, :, r:r + 1] = dlam_c + jnp.where(is_last, last, 0.0)
                    dlr_ref[gg, r:r + 1, :] = -jnp.sum(Wm, axis=0, keepdims=True)
                    dxc = dxc + lax.dot_general((Gm * ldec).astype(BF16), dy_j, (((0,), (0,)), ((), ())),
                                                preferred_element_type=F32)
                    dG = dG + dMl
                dx_ref[:, sl] = dxc * dtp
                t = dxc * xp
                ddt_ref[gg, :, 2 * k:2 * k + 1] = jnp.sum(jnp.where(head0, t, 0.0), axis=1, keepdims=True)
                ddt_ref[gg, :, 2 * k + 1:2 * k + 2] = jnp.sum(jnp.where(head0, 0.0, t), axis=1, keepdims=True)
                edy_b = (e_in * dyp).astype(BF16)
                dC = dC + jnp.dot(edy_b, s_in_b, preferred_element_type=F32)
                dB = dB + jnp.dot((dte * xc).astype(BF16), ds_out_b, preferred_element_type=F32)
                dS[pk] = e_all * ds_out + lax.dot_general(edy_b, Cm, (((0,), (0,)), ((), ())),
                                                          preferred_element_type=F32)
            dG_b = dG.astype(BF16)
            dc_ref[:, gg * N:(gg + 1) * N] = dC + jnp.dot(dG_b, Bm, preferred_element_type=F32)
            db_ref[:, gg * N:(gg + 1) * N] = dB + lax.dot_general(dG_b, Cm, (((0,), (0,)), ((), ())),
                                                                  preferred_element_type=F32)

    return _call_with_exchange(
        body, exch, name=name, grid=(2, G // gpb, nc),
        in_specs=[sp["x"], sp["b"], sp["c"], sp["x"], sp["st"], sp["rep"], sp["rep"], sp["rep"], sp["lam_a"],
                  sp["lam_b"], sp["etot"]],
        out_specs=[sp["xd"], sp["bd"], sp["bd"], sp["col"], sp["col"], sp["row"]],
        out_shape=[jax.ShapeDtypeStruct((2, T, H * P), F32), jax.ShapeDtypeStruct((2, T, G * N), F32),
                   jax.ShapeDtypeStruct((2, T, G * N), F32), jax.ShapeDtypeStruct((2, G, T, R), F32),
                   jax.ShapeDtypeStruct((2, G, T, R), F32), jax.ShapeDtypeStruct((2, G, R, T), F32)],
        scratch_shapes=[pltpu.VMEM((gpb * R // 2, 2 * P, N), F32)],
        operands=(xbc, xbc, xbc, dy, states, dtb, ein, dte, lam_a, lam_b, etot))


def _gnorm_fwd(y2, proj, w, gs, tr, name):
    T, HP = y2.shape[1], y2.shape[2]

    def body(y_ref, z_ref, w_ref, o_ref):
        yz = (y_ref[0] + y_ref[1]) * _silu(z_ref[...])
        for g in range(HP // gs):
            v = yz[:, g * gs:(g + 1) * gs]
            rstd = lax.rsqrt(jnp.mean(v * v, axis=-1, keepdims=True) + EPS)
            o_ref[:, g * gs:(g + 1) * gs] = (v * rstd * w_ref[:, g * gs:(g + 1) * gs]).astype(BF16)

    return pl.pallas_call(
        body, name=name, grid=(T // tr,),
        in_specs=[pl.BlockSpec((2, tr, HP), lambda i: (0, i, 0)), pl.BlockSpec((tr, HP), lambda i: (i, 0)),
                  pl.BlockSpec((1, HP), lambda i: (0, 0))],
        out_specs=pl.BlockSpec((tr, HP), lambda i: (i, 0)),
        out_shape=jax.ShapeDtypeStruct((T, HP), BF16),
        compiler_params=_cparams(("parallel",)))(y2, proj, w)


def _gnorm_bwd(y2, proj, w, dyn, gs, tr, name):
    T, HP = y2.shape[1], y2.shape[2]
    row = pl.BlockSpec((tr, HP), lambda i: (i, 0))

    def body(y_ref, z_ref, w_ref, d_ref, dy_ref, dz_ref, dw_ref):
        @pl.when(pl.program_id(0) == 0)
        def _():
            dw_ref[...] = jnp.zeros_like(dw_ref)

        yv = y_ref[0] + y_ref[1]
        zv = z_ref[...]
        sz = _silu(zv)
        yz = yv * sz
        dv = d_ref[...]
        r = lax.broadcasted_iota(jnp.int32, (8, gs), 0)
        for g in range(HP // gs):
            sl = slice(g * gs, (g + 1) * gs)
            v = yz[:, sl]
            rstd = lax.rsqrt(jnp.mean(v * v, axis=-1, keepdims=True) + EPS)
            xhat = v * rstd
            dyn_g = dv[:, sl]
            dhat = dyn_g * w_ref[:, sl]
            dyz = rstd * (dhat - xhat * jnp.mean(dhat * xhat, axis=-1, keepdims=True))
            dy_ref[:, sl] = dyz * sz[:, sl]
            dz_ref[:, sl] = (dyz * yv[:, sl] * _dsilu(zv[:, sl])).astype(BF16)
            dw_ref[:, sl] += jnp.where(r == 0, jnp.sum(dyn_g * xhat, axis=0, keepdims=True), 0.0)

    return pl.pallas_call(
        body, name=name, grid=(T // tr,),
        in_specs=[pl.BlockSpec((2, tr, HP), lambda i: (0, i, 0)), row, pl.BlockSpec((1, HP), lambda i: (0, 0)), row],
        out_specs=[row, row, pl.BlockSpec((8, HP), lambda i: (0, 0))],
        out_shape=[jax.ShapeDtypeStruct((T, HP), F32), jax.ShapeDtypeStruct((T, HP), BF16),
                   jax.ShapeDtypeStruct((8, HP), F32)],
        compiler_params=_cparams(("arbitrary",)))(y2, proj, w, dyn)


def _merge_fwd(proj, g1_blk, o_ssd, o_pool, tr, name):
    T, D = o_ssd.shape
    row = pl.BlockSpec((tr, D), lambda i: (i, 0))

    def body(g1_ref, g2_ref, a_ref, b_ref, o_ref):
        o_ref[...] = (jax.nn.sigmoid(g1_ref[...]) * a_ref[...]
                      + jax.nn.sigmoid(g2_ref[...]) * b_ref[...]).astype(BF16)

    return pl.pallas_call(
        body, name=name, grid=(T // tr,),
        in_specs=[pl.BlockSpec((tr, D), lambda i: (i, g1_blk)), pl.BlockSpec((tr, D), lambda i: (i, g1_blk + 1)),
                  row, row],
        out_specs=row, out_shape=jax.ShapeDtypeStruct((T, D), BF16),
        compiler_params=_cparams(("parallel",)))(proj, proj, o_ssd, o_pool)


def _merge_bwd(proj, g1_blk, o_ssd, o_pool, dmg, tr, name):
    T, D = o_ssd.shape
    row = pl.BlockSpec((tr, D), lambda i: (i, 0))

    def body(g1_ref, g2_ref, a_ref, b_ref, d_ref, da_ref, db_ref, dg_ref):
        s1, s2 = jax.nn.sigmoid(g1_ref[...]), jax.nn.sigmoid(g2_ref[...])
        dv = d_ref[...]
        da_ref[...] = (s1 * dv).astype(BF16)
        db_ref[...] = (s2 * dv).astype(BF16)
        dg_ref[:, :D] = (dv * a_ref[...] * s1 * (1.0 - s1)).astype(BF16)
        dg_ref[:, D:] = (dv * b_ref[...] * s2 * (1.0 - s2)).astype(BF16)

    return pl.pallas_call(
        body, name=name, grid=(T // tr,),
        in_specs=[pl.BlockSpec((tr, D), lambda i: (i, g1_blk)), pl.BlockSpec((tr, D), lambda i: (i, g1_blk + 1)),
                  row, row, row],
        out_specs=[row, row, pl.BlockSpec((tr, 2 * D), lambda i: (i, 0))],
        out_shape=[jax.ShapeDtypeStruct((T, D), BF16), jax.ShapeDtypeStruct((T, D), BF16),
                   jax.ShapeDtypeStruct((T, 2 * D), BF16)],
        compiler_params=_cparams(("parallel",)))(proj, proj, o_ssd, o_pool, dmg)


def _swiglu_fwd(gu, tr, name):
    T, F2 = gu.shape
    F = F2 // 2
    tc = _tile(F, (1408, 768, 512, 256, 128))
    nb = F // tc

    def body(a_ref, b_ref, o_ref):
        o_ref[...] = (_silu(a_ref[...]) * b_ref[...]).astype(BF16)

    return pl.pallas_call(
        body, name=name, grid=(T // tr, nb),
        in_specs=[pl.BlockSpec((tr, tc), lambda i, j: (i, j)), pl.BlockSpec((tr, tc), lambda i, j: (i, nb + j))],
        out_specs=pl.BlockSpec((tr, tc), lambda i, j: (i, j)),
        out_shape=jax.ShapeDtypeStruct((T, F), BF16),
        compiler_params=_cparams(("parallel", "parallel")))(gu, gu)


def _swiglu_bwd(gu, dact, tr, name):
    T, F2 = gu.shape
    F = F2 // 2
    tc = _tile(F, (1408, 768, 512, 256, 128))
    nb = F // tc

    def body(a_ref, b_ref, d_ref, o_ref):
        is_a = pl.program_id(1) < nb
        av, bv, dv = a_ref[...], b_ref[...], d_ref[...]
        o_ref[...] = jnp.where(is_a, dv * bv * _dsilu(av), dv * _silu(av)).astype(BF16)

    return pl.pallas_call(
        body, name=name, grid=(T // tr, 2 * nb),
        in_specs=[pl.BlockSpec((tr, tc), lambda i, j: (i, j % nb)), pl.BlockSpec((tr, tc), lambda i, j: (i, nb + j % nb)),
                  pl.BlockSpec((tr, tc), lambda i, j: (i, j % nb))],
        out_specs=pl.BlockSpec((tr, tc), lambda i, j: (i, j)),
        out_shape=jax.ShapeDtypeStruct((T, F2), BF16),
        compiler_params=_cparams(("parallel", "parallel")))(gu, gu, dact)


def _adamw(w, g, m, v, name):
    Rr, C = w.shape
    tr = _tile(Rr, (256, 128, 64, 32, 16, 8))
    row = pl.BlockSpec((tr, C), lambda i: (i, 0))

    def body(w_ref, g_ref, m_ref, v_ref, d_ref, mo_ref, vo_ref):
        gv = g_ref[...]
        mn = ADAM_B1 * m_ref[...] + (1.0 - ADAM_B1) * gv
        vn = ADAM_B2 * v_ref[...] + (1.0 - ADAM_B2) * (gv * gv)
        m_hat = mn / (1.0 - ADAM_B1 ** ADAM_STEP)
        v_hat = vn / (1.0 - ADAM_B2 ** ADAM_STEP)
        d_ref[...] = -ADAM_LR * (m_hat / (jnp.sqrt(v_hat) + ADAM_EPS) + ADAM_WD * w_ref[...])
        mo_ref[...] = mn
        vo_ref[...] = vn

    sds = jax.ShapeDtypeStruct((Rr, C), F32)
    return pl.pallas_call(body, name=name, grid=(Rr // tr,), in_specs=[row] * 4, out_specs=[row] * 3,
                          out_shape=[sds] * 3, compiler_params=_cparams(("parallel",)))(w, g, m, v)


def _sum_slots(x, name):
    n, Rr, C = x.shape
    tr = _tile(Rr, (512, 256, 128, 64, 32, 16, 8))

    def body(x_ref, o_ref):
        acc = x_ref[0].astype(F32)
        for k in range(1, n):
            acc = acc + x_ref[k].astype(F32)
        o_ref[...] = acc

    return pl.pallas_call(body, name=name, grid=(Rr // tr,),
                          in_specs=[pl.BlockSpec((n, tr, C), lambda i: (0, i, 0))],
                          out_specs=pl.BlockSpec((tr, C), lambda i: (i, 0)),
                          out_shape=jax.ShapeDtypeStruct((Rr, C), F32),
                          compiler_params=_cparams(("parallel",)))(x)


def _place():
    return lax.axis_index("x"), lax.axis_index("y"), lax.axis_index("c")


def _all_gather(x, name):
    Rr, C = x.shape

    def body(x_ref, out_ref, send_sems, recv_sems, local_sem):
        _gather_start(x_ref, out_ref, send_sems, recv_sems, local_sem)
        _gather_finish(x_ref, out_ref, send_sems, recv_sems, local_sem)

    return pl.pallas_call(
        body, name=name, in_specs=[ANY], out_specs=ANY,
        out_shape=jax.ShapeDtypeStruct((N_DEV, Rr, C), x.dtype), scratch_shapes=_GATHER_SEMS,
    )(x)


_GATHER_SEMS = [pltpu.SemaphoreType.DMA((7,)), pltpu.SemaphoreType.DMA((7,)), pltpu.SemaphoreType.DMA]


def _gather_copies(x_ref, out_ref, send_sems, recv_sems, local_sem):
    mx, my, mc = _place()
    me, sibling = (mx, my, mc), (mx, my, 1 - mc)
    chips = [(1 - mx, my), (mx, 1 - my), (1 - mx, 1 - my)]

    def slot(px, py, pc):
        return out_ref.at[4 * px + 2 * py + pc]

    def copy(k, block, to, src=None):
        return pltpu.make_async_remote_copy(
            src_ref=slot(*block) if src is None else src, dst_ref=slot(*block),
            send_sem=send_sems.at[k], recv_sem=recv_sems.at[k],
            device_id=to, device_id_type=pl.DeviceIdType.MESH)

    return dict(
        mine=pltpu.make_async_copy(x_ref, slot(*me), local_sem),
        first=[copy(0, me, sibling, src=x_ref)] + [copy(1 + j, me, (*chip, mc), src=x_ref)
                                                   for j, chip in enumerate(chips)],
        passed=[copy(4 + j, (*chip, mc), sibling) for j, chip in enumerate(chips)],
        from_chips=[copy(1 + j, (*chip, mc), me) for j, chip in enumerate(chips)],
        from_sibling=[copy(0, sibling, me)] + [copy(4 + j, (*chip, 1 - mc), me) for j, chip in enumerate(chips)],
    )


def _gather_start(*refs):
    cps = _gather_copies(*refs)
    cps["mine"].start()
    for cp in cps["first"]:
        cp.start()


def _gather_finish(*refs):
    cps = _gather_copies(*refs)
    for j in range(3):
        cps["from_chips"][j].wait_recv()
        cps["passed"][j].start()
    for cp in cps["from_sibling"]:
        cp.wait_recv()
    for cp in cps["first"] + cps["passed"]:
        cp.wait_send()
    cps["mine"].wait()


def _pair_exchange(buf, name):
    def body(b_ref, got_ref, send_sems, recv_sems):
        _pair_start(b_ref, got_ref, send_sems, recv_sems)
        _pair_finish(b_ref, got_ref, send_sems, recv_sems)

    return pl.pallas_call(
        body, name=name, in_specs=[ANY], out_specs=ANY,
        out_shape=jax.ShapeDtypeStruct(buf.shape[1:], buf.dtype), scratch_shapes=_PAIR_SEMS,
    )(buf)


_PAIR_PARTS = 4
_PAIR_SEMS = [pltpu.SemaphoreType.DMA((4 * _PAIR_PARTS,)), pltpu.SemaphoreType.DMA((4 * _PAIR_PARTS,))]


def _pair_copies(b_ref, got_ref, send_sems, recv_sems):
    mx, my, mc = _place()
    n, Rr = got_ref.shape[0], got_ref.shape[1]
    pr = Rr // _PAIR_PARTS
    return [pltpu.make_async_remote_copy(
        src_ref=b_ref.at[1 - mc, q, pl.ds(p * pr, pr)], dst_ref=got_ref.at[q, pl.ds(p * pr, pr)],
        send_sem=send_sems.at[q * _PAIR_PARTS + p], recv_sem=recv_sems.at[q * _PAIR_PARTS + p],
        device_id=(mx, my, 1 - mc), device_id_type=pl.DeviceIdType.MESH)
        for q in range(n) for p in range(_PAIR_PARTS)]


def _pair_start(*refs):
    for cp in _pair_copies(*refs):
        cp.start()


def _pair_finish(*refs):
    for cp in _pair_copies(*refs):
        cp.wait()


def _pair_add(buf, got, my_c, name):
    _, n, Rr, C = buf.shape
    tr = _tile(Rr, (512, 256, 128, 64, 32, 16))

    def body(c_ref, b_ref, g_ref, o_ref):
        o_ref[...] = (b_ref[...].astype(F32) + g_ref[...].astype(F32)).astype(BF16)

    return pl.pallas_call(
        body, name=name,
        grid_spec=pltpu.PrefetchScalarGridSpec(
            num_scalar_prefetch=1, grid=(n, Rr // tr),
            in_specs=[pl.BlockSpec((None, None, tr, C), lambda q, i, c: (c[0], q, i, 0)),
                      pl.BlockSpec((None, tr, C), lambda q, i, c: (q, i, 0))],
            out_specs=pl.BlockSpec((None, tr, C), lambda q, i, c: (q, i, 0))),
        out_shape=jax.ShapeDtypeStruct((n, Rr, C), BF16),
        compiler_params=_cparams(("parallel", "parallel")))(my_c, buf, got)


def _chip_exchange(red, name):
    def body(r_ref, out_ref, send_sems, recv_sems, local_sem):
        _chip_exchange_start(r_ref, out_ref, send_sems, recv_sems, local_sem)
        _chip_exchange_finish(r_ref, out_ref, send_sems, recv_sems, local_sem)

    return pl.pallas_call(
        body, name=name, in_specs=[ANY], out_specs=ANY,
        out_shape=jax.ShapeDtypeStruct(red.shape, red.dtype), scratch_shapes=_CHIP_SEMS,
    )(red)


_CHIP_SEMS = [pltpu.SemaphoreType.DMA((3,)), pltpu.SemaphoreType.DMA((3,)), pltpu.SemaphoreType.DMA]


def _chip_exchange_copies(r_ref, out_ref, send_sems, recv_sems, local_sem):
    mx, my, mc = _place()
    chips = [(1 - mx, my), (mx, 1 - my), (1 - mx, 1 - my)]

    def copy(k, src_slot, dst_slot, to):
        return pltpu.make_async_remote_copy(
            src_ref=r_ref.at[src_slot], dst_ref=out_ref.at[dst_slot],
            send_sem=send_sems.at[k], recv_sem=recv_sems.at[k],
            device_id=(*to, mc), device_id_type=pl.DeviceIdType.MESH)

    return dict(
        mine=pltpu.make_async_copy(r_ref.at[2 * mx + my], out_ref.at[2 * mx + my], local_sem),
        sends=[copy(k, 2 * px + py, 2 * mx + my, (px, py)) for k, (px, py) in enumerate(chips)],
        recvs=[copy(k, 2 * px + py, 2 * px + py, (px, py)) for k, (px, py) in enumerate(chips)],
    )


def _chip_exchange_start(*refs):
    cps = _chip_exchange_copies(*refs)
    cps["mine"].start()
    for cp in cps["sends"]:
        cp.start()


def _chip_exchange_finish(*refs):
    cps = _chip_exchange_copies(*refs)
    for cp in cps["recvs"]:
        cp.wait_recv()
    for cp in cps["sends"]:
        cp.wait_send()
    cps["mine"].wait()


def _pad_rows(a, rows):
    return jnp.pad(a, ((0, rows - a.shape[0]), (0, 0)))


class _Layout:
    def __init__(self, D, shards):
        self.D = D
        self.pieces = []
        off = 0
        for name, layer, rows in shards:
            pr = _round_up(rows, 16)
            self.pieces.append((name, layer, rows, pr, off))
            off += pr
        self.rows = _round_up(off, 256)

    def find(self, name, layer):
        for p in self.pieces:
            if p[0] == name and p[1] == layer:
                return p
        raise KeyError(name)


def kernel(x, c, ctx, c_ctx, w_ada, b_ada, g_mix, w_in, conv_w, conv_b, dt_bias, a_log, d_skip, ssd_norm_w, w_ssd_out, pool_w, pool_scale, w_pool_out, w_out, g_ffn, w_gate_up, w_down, g_final, loss_target, m_c_ctx, m_w_ada, m_b_ada, m_g_mix, m_w_in, m_conv_w, m_conv_b, m_dt_bias, m_a_log, m_d_skip, m_ssd_norm_w, m_w_ssd_out, m_pool_w, m_pool_scale, m_w_pool_out, m_w_out, m_g_ffn, m_w_gate_up, m_w_down, m_g_final, v_c_ctx, v_w_ada, v_b_ada, v_g_mix, v_w_in, v_conv_w, v_conv_b, v_dt_bias, v_a_log, v_d_skip, v_ssd_norm_w, v_w_ssd_out, v_pool_w, v_pool_scale, v_w_pool_out, v_w_out, v_g_ffn, v_w_gate_up, v_w_down, v_g_final):
    weights = dict(c_ctx=c_ctx, w_ada=w_ada, b_ada=b_ada, g_mix=g_mix, w_in=w_in, conv_w=conv_w, conv_b=conv_b,
                   dt_bias=dt_bias, a_log=a_log, d_skip=d_skip, ssd_norm_w=ssd_norm_w, w_ssd_out=w_ssd_out,
                   pool_w=pool_w, pool_scale=pool_scale, w_pool_out=w_pool_out, w_out=w_out, g_ffn=g_ffn,
                   w_gate_up=w_gate_up, w_down=w_down, g_final=g_final)
    moms_m = dict(c_ctx=m_c_ctx, w_ada=m_w_ada, b_ada=m_b_ada, g_mix=m_g_mix, w_in=m_w_in, conv_w=m_conv_w,
                  conv_b=m_conv_b, dt_bias=m_dt_bias, a_log=m_a_log, d_skip=m_d_skip, ssd_norm_w=m_ssd_norm_w,
                  w_ssd_out=m_w_ssd_out, pool_w=m_pool_w, pool_scale=m_pool_scale, w_pool_out=m_w_pool_out,
                  w_out=m_w_out, g_ffn=m_g_ffn, w_gate_up=m_w_gate_up, w_down=m_w_down, g_final=m_g_final)
    moms_v = dict(c_ctx=v_c_ctx, w_ada=v_w_ada, b_ada=v_b_ada, g_mix=v_g_mix, w_in=v_w_in, conv_w=v_conv_w,
                  conv_b=v_conv_b, dt_bias=v_dt_bias, a_log=v_a_log, d_skip=v_d_skip, ssd_norm_w=v_ssd_norm_w,
                  w_ssd_out=v_w_ssd_out, pool_w=v_pool_w, pool_scale=v_pool_scale, w_pool_out=v_w_pool_out,
                  w_out=v_w_out, g_ffn=v_g_ffn, w_gate_up=v_w_gate_up, w_down=v_w_down, g_final=v_g_final)
    order = ["c_ctx", "w_ada", "b_ada", "g_mix", "w_in", "conv_w", "conv_b", "dt_bias", "a_log", "d_skip",
             "ssd_norm_w", "w_ssd_out", "pool_w", "pool_scale", "w_pool_out", "w_out", "g_ffn", "w_gate_up",
             "w_down", "g_final"]
    big = ["w_ada", "w_in", "conv_w", "w_ssd_out", "pool_w", "w_pool_out", "w_out", "w_gate_up", "w_down"]
    small = [n for n in order if n not in big]

    depth = w_in.shape[0]
    L, D = x.shape[1], x.shape[2]
    n_ctx = ctx.shape[1]
    T = n_ctx + L
    in_cols = w_in.shape[2] * N_DEV
    xbc_w = conv_w.shape[2] * N_DEV
    dinner = ssd_norm_w.shape[1]
    H = dt_bias.shape[2]
    G = H // HPG
    GN = G * STATE
    assert xbc_w == dinner + 2 * GN and dinner == H * HEADDIM
    assert in_cols == dinner + xbc_w + 2 * H + D + 2 * D
    assert dinner == 2 * D and GN == D and 2 * H <= 128
    F = w_down.shape[1] * N_DEV
    pg = pool_w.shape[3]
    tr = n_ctx
    assert L % tr == 0 and tr % GRID_W == 0 and tr % CHUNK == 0 and L % CHUNK == 0
    n_ctx_tiles = 1
    tr_big = _tile(T, (1088, 544, 512, 256, 128))
    tr_mid = _tile(T, (544, 512, 256, 128))
    NP =_round_up(9 * D + 128, 512)
    gs = dinner // G
    off_xbc, off_dt, off_pool = dinner, dinner + xbc_w, dinner + xbc_w + 2 * H
    off_gate = off_pool + D

    def shard_rows(name, l):
        w = weights[name][l]
        if name in ("w_ada", "w_in", "w_gate_up"):
            return w.T
        if name == "conv_w":
            w8 = _pad_rows(w, 8)
            hi = w8.astype(BF16)
            lo = (w8 - hi.astype(F32)).astype(BF16)
            return jnp.concatenate([hi, lo], axis=0).reshape(-1, D)
        if name == "pool_w":
            return w.reshape(-1, D)
        return w

    first_needed = ["w_ada", "w_in"]
    shard = {(n, l): shard_rows(n, l) for l in range(depth) for n in big}
    gather_groups = [[first_needed, [n for n in big if n not in first_needed]] for l in range(depth)]
    glays = [[_Layout(D, [(n, l, shard[(n, l)].shape[0]) for n in grp]) for grp in gather_groups[l]]
             for l in range(depth)]

    def packed(l, gi):
        lay = glays[l][gi]
        rows = jnp.concatenate([_pad_rows(shard[(n, l)].astype(BF16), lay.find(n, l)[3])
                                for n in gather_groups[l][gi]], axis=0)
        return _pad_rows(rows, lay.rows)

    def gather_exchange(l, gi):
        return _Exchange(packed(l, gi), jax.ShapeDtypeStruct((N_DEV, glays[l][gi].rows, D), BF16), _GATHER_SEMS,
                         _gather_start, _gather_finish)

    gathered = [[None] * len(g) for g in gather_groups]
    gathered[0][0] = _all_gather(packed(0, 0), "gather_weights")

    def full(name, l):
        gi = [name in grp for grp in gather_groups[l]].index(True)
        _, _, rows, _, off = glays[l][gi].find(name, l)
        return gathered[l][gi][:, off:off + rows, :]

    def w_inT_new(l):
        w = full("w_in", l).reshape(in_cols, D)
        parts = [w[:off_xbc], w[off_xbc:off_dt], w[off_pool:off_gate], w[off_gate:], w[off_dt:off_pool]]
        return _pad_rows(jnp.concatenate(parts, axis=0), NP)

    xs0 = jnp.concatenate([ctx[0], x[0]], axis=0)
    cc8 = _pad_rows(jnp.concatenate([c, c_ctx[None, :]], axis=0), 8)
    tgt = loss_target[0]

    def vec(a):
        return a.reshape(1, -1)

    def pad128(a):
        return jnp.pad(a.reshape(1, -1), ((0, 0), (0, 128 - 2 * H)))

    expand = (jnp.arange(128)[:, None] == jnp.arange(2 * H * HEADDIM)[None, :] // HEADDIM).astype(BF16)

    def from4(arr):
        return jnp.pad(arr.transpose(2, 0, 1, 3).reshape(T, 2 * H), ((0, 0), (0, 128 - 2 * H)))

    dt_blk = (9 * D) // 128
    conv_tc = 128
    ssd_gpb, ssd_gpb_bwd = min(G, 8), 2
    saved = []
    xcur = xs0
    for l in range(depth):
        W = dict(adaT=full("w_ada", l).reshape(6 * D, D), inT=w_inT_new(l))
        m6 = _ada_fwd(cc8, W["adaT"], vec(b_ada[l]), "ada_fwd")
        h = _norm_mod(xcur, vec(g_mix[l]), m6, 0, 1, n_ctx_tiles, tr, "norm_mod")
        if len(gather_groups[l]) > 1:
            proj, gathered[l][1] = _mm(h, W["inT"], "nt", F32, "mm_in_gather", exch=gather_exchange(l, 1))
        else:
            proj = _mm(h, W["inT"], "nt", F32, "mm_in")
        W.update(
            ssd=full("w_ssd_out", l).reshape(dinner, D), po=full("w_pool_out", l).reshape(D, D),
            out=full("w_out", l).reshape(D, D), guT=full("w_gate_up", l).reshape(2 * F, D),
            down=full("w_down", l).reshape(F, D),
            pool=full("pool_w", l).reshape(N_DEV, len(POOL_WINDOWS), pg // N_DEV, pg).transpose(1, 0, 2, 3)
            .reshape(len(POOL_WINDOWS), pg, pg),
        )
        cw = full("conv_w", l).reshape(N_DEV, 16, xbc_w // N_DEV).astype(F32)
        W["conv8"] = (cw[:, :8] + cw[:, 8:]).transpose(1, 0, 2).reshape(8, xbc_w)
        l1, l2, l3, dtb, ein, dte, etot = _dt_prep(proj, dt_blk, pad128(dt_bias[l]), pad128(a_log[l]), expand, H,
                                                   "dt_prep")
        L1, L2, L3 = (v[:, :2 * H].T.reshape(2, H, 1, T) for v in (l1, l2, l3))
        k16 = jnp.arange(16).reshape(1, 1, 16, 1)

        def rows16(at):
            terms = jnp.where(k16 == at, L1, jnp.where(k16 == at + 1, L2, L3))
            return jnp.where((k16 >= at) & (k16 < at + 3), terms, (k16 < 6).astype(BF16))

        lam_a, lam_b = rows16(0), -rows16(3) + 2 * (k16 < 3).astype(BF16)
        et = etot.reshape(T // CHUNK, 8, 128)[:, 0, :2 * H].reshape(T // CHUNK, 2, G, HPG).transpose(1, 2, 0, 3)
        etot5 = jnp.pad(jnp.broadcast_to(et[..., None], et.shape + (128,)),
                        ((0, 0), (0, 0), (0, 0), (0, 8 - HPG), (0, 0)))
        dsk = jnp.pad(jnp.repeat(d_skip[l], HEADDIM, axis=1)[:, None, :], ((0, 0), (0, 7), (0, 0)))
        scan_ops = (dtb, ein, dte, lam_a, lam_b, etot5)
        xbc = _conv_fwd(proj, dinner // conv_tc, xbc_w, W["conv8"], vec(conv_b[l]), n_ctx, conv_tc, "conv_fwd")
        if l + 1 < depth:
            y2, states, gathered[l + 1][0] = _ssd_fwd(xbc, *scan_ops, dsk, G, n_ctx, ssd_gpb, "ssd_fwd_gather",
                                                      gather_exchange(l + 1, 0))
        else:
            y2, states = _ssd_fwd(xbc, *scan_ops, dsk, G, n_ctx, ssd_gpb, "ssd_fwd")
        yn = _gnorm_fwd(y2, proj, vec(ssd_norm_w[l]), gs, tr_mid, "gnorm_fwd")
        pm = _pool_apply(proj, (6 * D) // pg, BF16, n_ctx, pg, False, "pool_fwd")
        pms = _pool_mix_fwd(pm, W["pool"], vec(pool_scale[l]), tr_big, "pool_mix_fwd")
        o_ssd = _mm(yn, W["ssd"], "nn", F32, "mm_ssd_out")
        o_pool = _mm(pms, W["po"], "nn", F32, "mm_pool_out")
        mg = _merge_fwd(proj, 7, o_ssd, o_pool, tr_mid, "merge_fwd")
        mo = _mm(mg, W["out"], "nn", F32, "mm_out")
        x1, h2 = _norm_mod(xcur, vec(g_ffn[l]), m6, 3, 4, n_ctx_tiles, tr, "resid_norm_mod", resid=(mo, 2))
        gu = _mm(h2, W["guT"], "nt", F32, "mm_gate_up")
        act = _swiglu_fwd(gu, tr_mid, "swiglu_fwd")
        f = _mm(act, W["down"], "nn", F32, "mm_down")
        saved.append(dict(W=W, m6=m6, x0=xcur, h=h, proj=proj, scan_ops=scan_ops, xbc=xbc, y2=y2,
                          states=states, yn=yn, pm=pm, pms=pms, o_ssd=o_ssd, o_pool=o_pool, mg=mg, mo=mo, x1=x1,
                          h2=h2, gu=gu, act=act, f=f))
        xcur = _resid(x1, f, m6, 5, n_ctx_tiles, tr, "resid")

    loss_blk, dx, dgf = _loss_head(xcur, tgt, vec(g_final), n_ctx_tiles, tr, "loss_head")
    loss = lax.psum(loss_blk[0, 0], MESH_AXES)

    big_rows = {}
    small_g = {n: [None] * depth for n in small}
    d_c_ctx = jnp.zeros((D,), F32)
    late = ["w_in", "w_ada", "conv_w"]
    rs_groups = [[[n for n in big if n not in late], late[:1], late[1:]] if l == 0 else [big] for l in range(depth)]
    rlays = [[_Layout(D, [(n, l, shard[(n, l)].shape[0]) for n in grp]) for grp in rs_groups[l]]
             for l in range(depth)]
    my_c = lax.axis_index("c").astype(jnp.int32).reshape(1)

    def pack_group(l, gi):
        lay = rlays[l][gi]
        gparts = [jnp.pad(big_rows[(n, l)].astype(BF16), ((0, 0), (0, pr - rows), (0, 0)))
                  for n, _, rows, pr, _ in lay.pieces]
        gparts.append(jnp.zeros((N_DEV, lay.rows - sum(p[3] for p in lay.pieces), D), BF16))
        gbuf = jnp.concatenate(gparts, axis=1)
        return gbuf.reshape(2, 2, 2, lay.rows, D).transpose(2, 0, 1, 3, 4).reshape(2, 4, lay.rows, D)

    def reduce_pair(l, gi):
        gbuf = pack_group(l, gi)
        return _pair_add(gbuf, _pair_exchange(gbuf, "rs_pair_exchange"), my_c, "rs_pair_add")

    slots = {}
    pending = None
    pair_waiting = None
    for l in reversed(range(depth)):
        S = saved[l]
        W, m6, proj = S["W"], S["m6"], S["proj"]
        df, dga2 = _resid_bwd(dx, S["f"], m6, 5, n_ctx_tiles, tr, "resid_bwd")
        if pair_waiting is None:
            dact = _mm(df, W["down"], "nt", F32, "mm_down_dx")
        else:
            key, gbuf = pair_waiting
            dact, got = _mm(df, W["down"], "nt", F32, "mm_down_dx_exchange", exch=_Exchange(
                gbuf, jax.ShapeDtypeStruct(gbuf.shape[1:], BF16), _PAIR_SEMS, _pair_start, _pair_finish))
            pending, pair_waiting = (key, _pair_add(gbuf, got, my_c, "rs_pair_add")), None
        g_down = _mm(S["act"], df, "tn", BF16,"mm_down_dw")
        dgu = _swiglu_bwd(S["gu"], dact, tr_mid, "swiglu_bwd")
        dh2 = _mm(dgu, W["guT"], "nn", F32, "mm_gate_up_dx")
        g_guT = _mm(dgu, S["h2"], "tn", BF16,"mm_gate_up_dw")
        dx1, st2 = _norm_mod_bwd(S["x1"], dh2, dx, vec(g_ffn[l]), m6, 4, n_ctx_tiles, tr, "norm_mod_bwd")
        dmo, dga1 = _resid_bwd(dx1, S["mo"], m6, 2, n_ctx_tiles, tr, "resid_bwd")
        dmg = _mm(dmo, W["out"], "nt", F32, "mm_out_dx")
        g_out = _mm(S["mg"], dmo, "tn", BF16,"mm_out_dw")
        do_ssd, do_pool, dgl = _merge_bwd(proj, 7, S["o_ssd"], S["o_pool"], dmg, tr_mid, "merge_bwd")
        dyn = _mm(do_ssd, W["ssd"], "nt", F32, "mm_ssd_out_dx")
        g_ssd = _mm(S["yn"], do_ssd, "tn", BF16,"mm_ssd_out_dw")
        dpms = _mm(do_pool, W["po"], "nt", F32, "mm_pool_out_dx")
        g_po = _mm(S["pms"], do_pool, "tn", BF16,"mm_pool_out_dw")
        dpm, g_pool, dps = _pool_mix_bwd(S["pm"], dpms, W["pool"], vec(pool_scale[l]), tr_big, "pool_mix_bwd")
        dup = _pool_apply(dpm, 0, BF16, n_ctx, pg, True, "pool_bwd")
        dy, dz, dnw = _gnorm_bwd(S["y2"], proj, vec(ssd_norm_w[l]), dyn, gs, tr, "gnorm_bwd")
        big_rows[("w_ssd_out", l)] = g_ssd.reshape(N_DEV, -1, D)
        big_rows[("pool_w", l)] = g_pool.reshape(len(POOL_WINDOWS), N_DEV, pg // N_DEV, pg).transpose(1, 0, 2, 3) \
            .reshape(N_DEV, -1, D)
        big_rows[("w_pool_out", l)] = g_po.reshape(N_DEV, -1, D)
        big_rows[("w_out", l)] = g_out.reshape(N_DEV, -1, D)
        big_rows[("w_gate_up", l)] = g_guT.reshape(N_DEV, -1, D)
        big_rows[("w_down", l)] = g_down.reshape(N_DEV, -1, D)
        riding = [] if pending is None else [pending]
        if len(rs_groups[l]) > 1:
            riding.append(((l, 0), reduce_pair(l, 0)))
        if not riding:
            dxs2, db2, dc2, ddt4, dlc4, dlr4 = _ssd_bwd(S["xbc"], dy, S["states"], *S["scan_ops"], G, n_ctx,
                                                        ssd_gpb_bwd, "ssd_bwd")
        else:
            red = jnp.concatenate([r for _, r in riding], axis=1)
            exch = _Exchange(red, jax.ShapeDtypeStruct(red.shape, BF16), _CHIP_SEMS, _chip_exchange_start,
                             _chip_exchange_finish)
            dxs2, db2, dc2, ddt4, dlc4, dlr4, got = _ssd_bwd(
                S["xbc"], dy, S["states"], *S["scan_ops"], G, n_ctx, ssd_gpb_bwd, "ssd_bwd_exchange", exch)
            at = 0
            for key, r in riding:
                slots[key] = got[:, at:at + r.shape[1]]
                at += r.shape[1]
        dskv = _pad_rows(jnp.repeat(d_skip[l], HEADDIM, axis=1), 8)
        cb = vec(conv_b[l])
        dxbc_x, dconv_x = _conv_bwd(proj, dinner // conv_tc, dxs2, 0, W["conv8"], cb, n_ctx, conv_tc, "conv_bwd_x",
                                    skip=(dy, dskv))
        dxbc_b, dconv_b = _conv_bwd(proj, 2 * dinner // conv_tc, db2, dinner // conv_tc, W["conv8"], cb, n_ctx,
                                    conv_tc, "conv_bwd_bc")
        dxbc_c, dconv_c = _conv_bwd(proj, (2 * dinner + GN) // conv_tc, dc2, (dinner + GN) // conv_tc, W["conv8"], cb,
                                    n_ctx, conv_tc, "conv_bwd_bc")
        dconv = jnp.concatenate([dconv_x, dconv_b, dconv_c], axis=1)
        ddt_raw, dtst = _dt_bwd(proj, dt_blk, pad128(dt_bias[l]), pad128(a_log[l]), from4(ddt4),
                                from4(dlc4 + dlr4.transpose(0, 1, 3, 2)), H, tr, "dt_bwd")
        dproj = jnp.concatenate([dz, dxbc_x, dxbc_b, dxbc_c, dup, dgl, ddt_raw,
                                 jnp.zeros((T, NP - 9 * D - 128), BF16)], axis=1)
        g_inT_new = _mm(dproj, S["h"], "tn", BF16,"mm_in_dw")
        g_inT = jnp.concatenate([g_inT_new[:6 * D], g_inT_new[9 * D:9 * D + 2 * H], g_inT_new[6 * D:9 * D]], axis=0)
        big_rows[("w_in", l)] = g_inT.reshape(N_DEV, -1, D)
        if len(rs_groups[l]) == 3:
            red = reduce_pair(l, 1)
            dh, slots[(l, 1)] = _mm(dproj, W["inT"], "nn", F32, "mm_in_dx_exchange", exch=_Exchange(
                red, jax.ShapeDtypeStruct(red.shape, BF16), _CHIP_SEMS, _chip_exchange_start, _chip_exchange_finish))
        else:
            dh = _mm(dproj, W["inT"], "nn", F32, "mm_in_dx")
        dx0, st1 = _norm_mod_bwd(S["x0"], dh, dx1, vec(g_mix[l]), m6, 1, n_ctx_tiles, tr, "norm_mod_bwd")
        dm6 = _pad_rows(jnp.concatenate([st1[0:2], st1[2:4], dga1[0:2], st2[0:2], st2[2:4], dga2[0:2]], axis=1), 8)
        dsil = _mm(dm6, W["adaT"], "nn", F32, "mm_ada_dx")
        sil_b, dcc, dbada = _ada_bwd_small(cc8, dsil, dm6, "ada_bwd_small")
        g_adaT = _mm(dm6, sil_b, "tn", BF16,"mm_ada_dw")
        d_c_ctx = d_c_ctx + dcc[1]
        dx = dx0

        big_rows[("w_ada", l)] = g_adaT.reshape(N_DEV, -1, D)
        big_rows[("conv_w", l)] = jnp.pad(
            dconv[:CONV_K].reshape(CONV_K, N_DEV, xbc_w // N_DEV).transpose(1, 0, 2),
            ((0, 0), (0, 16 - CONV_K), (0, 0))).reshape(N_DEV, -1, D)
        small_g["b_ada"][l] = dbada[0]
        small_g["g_mix"][l] = st1[4]
        small_g["conv_b"][l] = dconv[CONV_K]
        small_g["dt_bias"][l] = dtst[0, :2 * H].reshape(2, H)
        small_g["a_log"][l] = dtst[1, :2 * H].reshape(2, H)
        dsk_h = dconv_x[CONV_K + 1].reshape(H, HEADDIM).sum(axis=-1)
        small_g["d_skip"][l] = jnp.stack([dsk_h, dsk_h])
        small_g["ssd_norm_w"][l] = dnw[0]
        small_g["pool_scale"][l] = dps[0]
        small_g["g_ffn"][l] = st2[4]

        last_gi = len(rs_groups[l]) - 1
        if l > 0:
            pair_waiting = ((l, last_gi), pack_group(l, last_gi))
        else:
            pending = ((l, last_gi), reduce_pair(l, last_gi))
    slots[pending[0]] = _chip_exchange(pending[1], "rs_chip_exchange")
    g_local = {key: _sum_slots(s, "rs_chip_add") for key, s in slots.items()}
    grad_x = dx[n_ctx:][None]

    def local_grad(name):
        outs = []
        for l in range(depth):
            gi = [name in grp for grp in rs_groups[l]].index(True)
            _, _, rows, _, off = rlays[l][gi].find(name, l)
            piece = g_local[(l, gi)][off:off + rows]
            if name in ("w_ada", "w_in", "w_gate_up"):
                piece = piece.T
            elif name == "conv_w":
                piece = piece.reshape(16, -1)[:CONV_K]
            elif name == "pool_w":
                piece = piece.reshape(weights[name].shape[1:])
            outs.append(piece)
        return jnp.stack(outs)

    grads = {n: local_grad(n) for n in big}

    small_full = {"c_ctx": d_c_ctx, "g_final": dgf[0]}
    for n in small:
        if n not in small_full:
            small_full[n] = jnp.stack(small_g[n])

    def pack_small(tree):
        flat = jnp.concatenate([tree[n].reshape(-1).astype(F32) for n in small])
        rows = _round_up(-(-flat.shape[0] // D), 8)
        return jnp.pad(flat, (0, rows * D - flat.shape[0])).reshape(rows, D)

    def unpack_small(buf):
        flat, out, off = buf.reshape(-1), {}, 0
        for n in small:
            sz = weights[n].size
            out[n] = flat[off:off + sz].reshape(weights[n].shape)
            off += sz
        return out

    g_small = _sum_slots(_all_gather(pack_small(small_full), "gather_small_grads"), "sum_small_grads")
    grads.update(unpack_small(g_small))

    delta, new_m, new_v = {}, {}, {}
    for n in big:
        shp = weights[n].shape
        d_, m_, v_ = _adamw(weights[n].reshape(-1, shp[-1]), grads[n].reshape(-1, shp[-1]),
                            moms_m[n].reshape(-1, shp[-1]), moms_v[n].reshape(-1, shp[-1]), "adamw_" + n)
        delta[n], new_m[n], new_v[n] = d_.reshape(shp), m_.reshape(shp), v_.reshape(shp)
    d_, m_, v_ = _adamw(pack_small(weights), g_small, pack_small(moms_m), pack_small(moms_v), "adamw_small")
    delta.update(unpack_small(d_))
    new_m.update(unpack_small(m_))
    new_v.update(unpack_small(v_))

    return (loss, grad_x, *[grads[n] for n in order], *[delta[n] for n in order],
            *[new_m[n] for n in order], *[new_v[n] for n in order])
```

```python
import functools

import jax
import jax.numpy as jnp
from jax import lax
from jax.experimental import pallas as pl
from jax.experimental.pallas import tpu as pltpu

F32 = jnp.float32
BF16 = jnp.bfloat16
N_DEV = 8
EPS = 1e-6
GRID_W = 64
POOL_WINDOWS = (2, 4, 8, 16)
HEADDIM = 64
STATE = 128
CHUNK = 128
HPG = 4
CONV_K = 5
ADAM_LR, ADAM_B1, ADAM_B2, ADAM_EPS, ADAM_WD, ADAM_STEP = 0.001, 0.9, 0.999, 1e-08, 0.01, 10
NEG_BIG = -1e30
MESH_AXES = ("x", "y", "c")
ANY = pl.BlockSpec(memory_space=pl.ANY)


def _tile(n, cands):
    for t in cands:
        if n % t == 0:
            return t
    return n


def _round_up(n, m):
    return -(-n // m) * m


def _silu(x):
    return x * jax.nn.sigmoid(x)


def _dsilu(x):
    s = jax.nn.sigmoid(x)
    return s * (1.0 + x * (1.0 - s))


def _cparams(sem):
    return pltpu.CompilerParams(dimension_semantics=sem, vmem_limit_bytes=56 * 1024 * 1024)


def _mm(a, b, mode, out_dtype, name, exch=None):
    if mode == "tn":
        K, M = a.shape
        N = b.shape[1]
        tm = _tile(M, (512, 256, 128))
        tn = _tile(N, (1024, 512, 256, 128))
        tk = K if K <= 4608 else _tile(K, (1088, 544, 512, 256, 128))
        a_spec = pl.BlockSpec((tk, tm), lambda i, j, k: (k, i))
        b_spec = pl.BlockSpec((tk, tn), lambda i, j, k: (k, j))
        dims = (((0,), (0,)), ((), ()))
    else:
        M, K = a.shape
        N = b.shape[0] if mode == "nt" else b.shape[1]
        tm = _tile(M, (1088, 544, 512, 256, 128))
        tk = K if K <= 4096 else max(t for t in range(128, 2817, 128) if K % t == 0)
        tn = _tile(N, (512, 256, 128)) if tk == K else _tile(N, (1024, 512, 256, 128))
        a_spec = pl.BlockSpec((tm, tk), lambda i, j, k: (i, k))
        if mode == "nt":
            b_spec = pl.BlockSpec((tn, tk), lambda i, j, k: (j, k))
            dims = (((1,), (1,)), ((), ()))
        else:
            b_spec = pl.BlockSpec((tk, tn), lambda i, j, k: (k, j))
            dims = (((1,), (0,)), ((), ()))
    nk = K // tk

    def body(a_ref, b_ref, o_ref, *acc):
        if nk == 1:
            o_ref[...] = lax.dot_general(a_ref[...].astype(BF16), b_ref[...].astype(BF16), dims,
                                         preferred_element_type=F32).astype(o_ref.dtype)
            return
        k = pl.program_id(2)

        @pl.when(k == 0)
        def _():
            acc[0][...] = jnp.zeros_like(acc[0])

        acc[0][...] += lax.dot_general(a_ref[...].astype(BF16), b_ref[...].astype(BF16), dims,
                                       preferred_element_type=F32)

        @pl.when(k == nk - 1)
        def _():
            o_ref[...] = acc[0][...].astype(o_ref.dtype)

    call = dict(name=name, grid=(M // tm, N // tn, nk), in_specs=[a_spec, b_spec],
                scratch_shapes=[pltpu.VMEM((tm, tn), F32)] if nk > 1 else [])
    o_spec, o_sds = pl.BlockSpec((tm, tn), lambda i, j, k: (i, j)), jax.ShapeDtypeStruct((M, N), out_dtype)
    if exch is not None:
        return _call_with_exchange(body, exch, out_specs=[o_spec], out_shape=[o_sds], operands=(a, b), **call)
    return pl.pallas_call(body, out_specs=o_spec, out_shape=o_sds,
                          compiler_params=_cparams(("parallel", "parallel", "arbitrary")), **call)(a, b)


def _ada_fwd(cc8, w_adaT, b_ada, name):
    D = cc8.shape[1]
    N = w_adaT.shape[0]
    tn = _tile(N, (512, 256, 128))

    def body(c_ref, w_ref, b_ref, o_ref):
        a = _silu(c_ref[...]).astype(BF16)
        o_ref[...] = lax.dot_general(a, w_ref[...], (((1,), (1,)), ((), ())),
                                     preferred_element_type=F32) + b_ref[...]

    return pl.pallas_call(
        body, name=name, grid=(N // tn,),
        in_specs=[pl.BlockSpec((8, D), lambda j: (0, 0)), pl.BlockSpec((tn, D), lambda j: (j, 0)),
                  pl.BlockSpec((1, tn), lambda j: (0, j))],
        out_specs=pl.BlockSpec((8, tn), lambda j: (0, j)),
        out_shape=jax.ShapeDtypeStruct((8, N), F32),
        compiler_params=_cparams(("parallel",)),
    )(cc8, w_adaT, b_ada)


def _ada_bwd_small(cc8, dsil, dm6, name):
    D = cc8.shape[1]
    N = dm6.shape[1]

    def body(c_ref, ds_ref, dm_ref, sil_ref, dc_ref, db_ref):
        c = c_ref[...]
        sil_ref[...] = _silu(c).astype(BF16)
        dc_ref[...] = ds_ref[...] * _dsilu(c)
        dm = dm_ref[...]
        row = lax.broadcasted_iota(jnp.int32, dm.shape, 0)
        db_ref[...] = jnp.where(row == 0, jnp.sum(dm, axis=0, keepdims=True), 0.0)

    return pl.pallas_call(
        body, name=name, grid=(1,),
        in_specs=[pl.BlockSpec((8, D), lambda i: (0, 0)), pl.BlockSpec((8, D), lambda i: (0, 0)),
                  pl.BlockSpec((8, N), lambda i: (0, 0))],
        out_specs=[pl.BlockSpec((8, D), lambda i: (0, 0)), pl.BlockSpec((8, D), lambda i: (0, 0)),
                   pl.BlockSpec((8, N), lambda i: (0, 0))],
        out_shape=[jax.ShapeDtypeStruct((8, D), BF16), jax.ShapeDtypeStruct((8, D), F32),
                   jax.ShapeDtypeStruct((8, N), F32)],
        compiler_params=_cparams(("arbitrary",)),
    )(cc8, dsil, dm6)


def _seg_pick(m_ref, is_ctx):
    return jnp.where(is_ctx, m_ref[1:2, :], m_ref[0:1, :])


def _norm_mod(x, g, m6, sh_idx, sc_idx, n_ctx_tiles, tr, name, resid=None):
    T, D = x.shape
    row = pl.BlockSpec((tr, D), lambda i: (i, 0))
    vec = pl.BlockSpec((1, D), lambda i: (0, 0))

    def mcol(idx):
        return pl.BlockSpec((8, D), lambda i: (0, idx))

    def body(*refs):
        if resid is None:
            x_ref, g_ref, sh_ref, sc_ref, h_ref = refs
            xv = x_ref[...]
        else:
            x_ref, f_ref, ga_ref, g_ref, sh_ref, sc_ref, xo_ref, h_ref = refs
        is_ctx = pl.program_id(0) < n_ctx_tiles
        if resid is not None:
            xv = x_ref[...] + _seg_pick(ga_ref, is_ctx) * f_ref[...]
            xo_ref[...] = xv
        rstd = lax.rsqrt(jnp.mean(xv * xv, axis=-1, keepdims=True) + EPS)
        hn = xv * rstd * g_ref[...]
        h_ref[...] = (hn * (1.0 + _seg_pick(sc_ref, is_ctx)) + _seg_pick(sh_ref, is_ctx)).astype(BF16)

    if resid is None:
        ins, in_specs = [x, g, m6, m6], [row, vec, mcol(sh_idx), mcol(sc_idx)]
        out_specs, out_shape = row, jax.ShapeDtypeStruct((T, D), BF16)
    else:
        f, ga_idx = resid
        ins = [x, f, m6, g, m6, m6]
        in_specs = [row, row, mcol(ga_idx), vec, mcol(sh_idx), mcol(sc_idx)]
        out_specs = [row, row]
        out_shape = [jax.ShapeDtypeStruct((T, D), F32), jax.ShapeDtypeStruct((T, D), BF16)]
    return pl.pallas_call(body, name=name, grid=(T // tr,), in_specs=in_specs, out_specs=out_specs,
                          out_shape=out_shape, compiler_params=_cparams(("parallel",)))(*ins)


def _resid(x, f, m6, ga_idx, n_ctx_tiles, tr, name):
    T, D = x.shape
    row = pl.BlockSpec((tr, D), lambda i: (i, 0))

    def body(x_ref, f_ref, ga_ref, o_ref):
        is_ctx = pl.program_id(0) < n_ctx_tiles
        o_ref[...] = x_ref[...] + _seg_pick(ga_ref, is_ctx) * f_ref[...]

    return pl.pallas_call(body, name=name, grid=(T // tr,),
                          in_specs=[row, row, pl.BlockSpec((8, D), lambda i: (0, ga_idx))], out_specs=row,
                          out_shape=jax.ShapeDtypeStruct((T, D), F32),
                          compiler_params=_cparams(("parallel",)))(x, f, m6)


def _resid_bwd(dx, f, m6, ga_idx, n_ctx_tiles, tr, name):
    T, D = dx.shape
    row = pl.BlockSpec((tr, D), lambda i: (i, 0))
    acc = pl.BlockSpec((8, D), lambda i: (0, 0))

    def body(dx_ref, f_ref, ga_ref, df_ref, dga_ref):
        i = pl.program_id(0)
        is_ctx = i < n_ctx_tiles

        @pl.when(i == 0)
        def _():
            dga_ref[...] = jnp.zeros_like(dga_ref)

        dxv = dx_ref[...]
        df_ref[...] = (_seg_pick(ga_ref, is_ctx) * dxv).astype(BF16)
        s = jnp.sum(dxv * f_ref[...], axis=0, keepdims=True)
        r = lax.broadcasted_iota(jnp.int32, (8, D), 0)
        dga_ref[...] += jnp.where(r == jnp.where(is_ctx, 1, 0), s, 0.0)

    return pl.pallas_call(body, name=name, grid=(T // tr,),
                          in_specs=[row, row, pl.BlockSpec((8, D), lambda i: (0, ga_idx))],
                          out_specs=[row, acc],
                          out_shape=[jax.ShapeDtypeStruct((T, D), BF16), jax.ShapeDtypeStruct((8, D), F32)],
                          compiler_params=_cparams(("arbitrary",)))(dx, f, m6)


def _norm_mod_bwd(x, dh, dxres, g, m6, sc_idx, n_ctx_tiles, tr, name):
    T, D = x.shape
    row = pl.BlockSpec((tr, D), lambda i: (i, 0))
    acc = pl.BlockSpec((8, D), lambda i: (0, 0))

    def body(x_ref, dh_ref, dr_ref, g_ref, sc_ref, dx_ref, st_ref):
        i = pl.program_id(0)
        is_ctx = i < n_ctx_tiles

        @pl.when(i == 0)
        def _():
            st_ref[...] = jnp.zeros_like(st_ref)

        xv, dh_v, gv = x_ref[...], dh_ref[...], g_ref[...]
        sc1 = 1.0 + _seg_pick(sc_ref, is_ctx)
        rstd = lax.rsqrt(jnp.mean(xv * xv, axis=-1, keepdims=True) + EPS)
        xhat = xv * rstd
        dxhat = dh_v * sc1 * gv
        dx_ref[...] = dr_ref[...] + rstd * (dxhat - xhat * jnp.mean(dxhat * xhat, axis=-1, keepdims=True))
        dsh = jnp.sum(dh_v, axis=0, keepdims=True)
        dsc = jnp.sum(dh_v * xhat * gv, axis=0, keepdims=True)
        dg = jnp.sum(dh_v * sc1 * xhat, axis=0, keepdims=True)
        r = lax.broadcasted_iota(jnp.int32, (8, D), 0)
        seg = jnp.where(is_ctx, 1, 0)
        st_ref[...] += (jnp.where(r == seg, dsh, 0.0) + jnp.where(r == 2 + seg, dsc, 0.0)
                        + jnp.where(r == 4, dg, 0.0))

    return pl.pallas_call(body, name=name, grid=(T // tr,),
                          in_specs=[row, row, row, pl.BlockSpec((1, D), lambda i: (0, 0)),
                                    pl.BlockSpec((8, D), lambda i: (0, sc_idx))],
                          out_specs=[row, acc],
                          out_shape=[jax.ShapeDtypeStruct((T, D), F32), jax.ShapeDtypeStruct((8, D), F32)],
                          compiler_params=_cparams(("arbitrary",)))(x, dh, dxres, g, m6)


def _loss_head(x, tgt, g, n_ctx_tiles, tr, name):
    T, D = x.shape
    row = pl.BlockSpec((tr, D), lambda i: (i, 0))

    def body(x_ref, t_ref, g_ref, l_ref, dx_ref, dg_ref):
        i = pl.program_id(0)

        @pl.when(i == 0)
        def _():
            l_ref[...] = jnp.zeros_like(l_ref)
            dg_ref[...] = jnp.zeros_like(dg_ref)

        @pl.when(i < n_ctx_tiles)
        def _():
            dx_ref[...] = jnp.zeros_like(dx_ref)

        @pl.when(i >= n_ctx_tiles)
        def _():
            xv, gv = x_ref[...], g_ref[...]
            rstd = lax.rsqrt(jnp.mean(xv * xv, axis=-1, keepdims=True) + EPS)
            xhat = xv * rstd
            e = xhat * gv - t_ref[...]
            l_ref[...] += 0.5 * jnp.sum(jnp.mean(e * e, axis=-1, keepdims=True), axis=0, keepdims=True)
            dy = e * (1.0 / D)
            dxhat = dy * gv
            dx_ref[...] = rstd * (dxhat - xhat * jnp.mean(dxhat * xhat, axis=-1, keepdims=True))
            r = lax.broadcasted_iota(jnp.int32, (8, D), 0)
            dg_ref[...] += jnp.where(r == 0, jnp.sum(dy * xhat, axis=0, keepdims=True), 0.0)

    return pl.pallas_call(
        body, name=name, grid=(T // tr,),
        in_specs=[row, pl.BlockSpec((tr, D), lambda i: (jnp.maximum(i - n_ctx_tiles, 0), 0)),
                  pl.BlockSpec((1, D), lambda i: (0, 0))],
        out_specs=[pl.BlockSpec((8, 128), lambda i: (0, 0)), row, pl.BlockSpec((8, D), lambda i: (0, 0))],
        out_shape=[jax.ShapeDtypeStruct((8, 128), F32), jax.ShapeDtypeStruct((T, D), F32),
                   jax.ShapeDtypeStruct((8, D), F32)],
        compiler_params=_cparams(("arbitrary",)))(x, tgt, g)


def _seq_masks(T, n_ctx, width):
    row = lax.broadcasted_iota(jnp.int32, (T, width), 0)
    in_ctx = row < n_ctx
    return jnp.where(in_ctx, row, row - n_ctx), jnp.where(in_ctx, n_ctx, T - n_ctx)


def _shift_rows(u, off, t_loc, seg_len):
    T = u.shape[0]
    if off == 0:
        return u
    v = pltpu.roll(u, (-off) % T, 0)
    ok = (t_loc + off >= 0) & (t_loc + off < seg_len)
    return jnp.where(ok, v, 0.0)


def _conv_fwd(proj, col0_blk, ncol, conv_w8, conv_b, n_ctx, tc, name):
    T = proj.shape[0]

    def body(u_ref, w_ref, b_ref, o_ref):
        u = u_ref[...]
        t_loc, seg_len = _seq_masks(T, n_ctx, tc)
        acc = jnp.broadcast_to(b_ref[...], u.shape)
        for i in range(CONV_K):
            acc = acc + w_ref[i:i + 1, :] * _shift_rows(u, i - CONV_K // 2, t_loc, seg_len)
        o_ref[...] = _silu(acc)

    return pl.pallas_call(
        body, name=name, grid=(ncol // tc,),
        in_specs=[pl.BlockSpec((T, tc), lambda j: (0, col0_blk + j)), pl.BlockSpec((8, tc), lambda j: (0, j)),
                  pl.BlockSpec((1, tc), lambda j: (0, j))],
        out_specs=pl.BlockSpec((T, tc), lambda j: (0, j)),
        out_shape=jax.ShapeDtypeStruct((T, ncol), F32),
        compiler_params=_cparams(("parallel",)))(proj, conv_w8, conv_b)


def _conv_bwd(proj, col0_blk, d2, w_blk0, conv_w8, conv_b, n_ctx, tc, name, skip=None):
    T, ncol = d2.shape[1], d2.shape[2]

    def body(u_ref, d_ref, w_ref, b_ref, *rest):
        if skip is None:
            du_ref, dw_ref = rest
        else:
            dy_ref, k_ref, du_ref, dw_ref = rest
        u = u_ref[...]
        t_loc, seg_len = _seq_masks(T, n_ctx, tc)
        pre = jnp.broadcast_to(b_ref[...], u.shape)
        for i in range(CONV_K):
            pre = pre + w_ref[i:i + 1, :] * _shift_rows(u, i - CONV_K // 2, t_loc, seg_len)
        r = lax.broadcasted_iota(jnp.int32, (8, tc), 0)
        dact = d_ref[0] + d_ref[1]
        dw = jnp.zeros((8, tc), F32)
        if skip is not None:
            dyv = dy_ref[...]
            dact = dact + (k_ref[0:1, :] + k_ref[1:2, :]) * dyv
            dw = jnp.where(r == CONV_K + 1, jnp.sum(dyv * _silu(pre), axis=0, keepdims=True), 0.0)
        dpre = dact * _dsilu(pre)
        du = jnp.zeros_like(u)
        dw = dw + jnp.where(r == CONV_K, jnp.sum(dpre, axis=0, keepdims=True), 0.0)
        for i in range(CONV_K):
            off = i - CONV_K // 2
            du = du + w_ref[i:i + 1, :] * _shift_rows(dpre, -off, t_loc, seg_len)
            dw = dw + jnp.where(r == i, jnp.sum(dpre * _shift_rows(u, off, t_loc, seg_len), axis=0, keepdims=True),
                                0.0)
        du_ref[...] = du.astype(BF16)
        dw_ref[...] = dw

    col = pl.BlockSpec((T, tc), lambda j: (0, j))
    in_specs = [pl.BlockSpec((T, tc), lambda j: (0, col0_blk + j)), pl.BlockSpec((2, T, tc), lambda j: (0, 0, j)),
                pl.BlockSpec((8, tc), lambda j: (0, w_blk0 + j)), pl.BlockSpec((1, tc), lambda j: (0, w_blk0 + j))]
    operands = [proj, d2, conv_w8, conv_b]
    if skip is not None:
        in_specs += [col, pl.BlockSpec((8, tc), lambda j: (0, j))]
        operands += list(skip)
    return pl.pallas_call(
        body, name=name, grid=(ncol // tc,), in_specs=in_specs,
        out_specs=[col, pl.BlockSpec((8, tc), lambda j: (0, j))],
        out_shape=[jax.ShapeDtypeStruct((T, ncol), BF16), jax.ShapeDtypeStruct((8, ncol), F32)],
        compiler_params=_cparams(("parallel",)))(*operands)


def _pool_core(u, half, t_loc, seg_len, transpose):
    tr = u.shape[0]

    def shift(v, s):
        w = pltpu.roll(v, s % tr, 0)
        ok = (t_loc - s >= 0) & (t_loc - s < seg_len)
        return jnp.where(ok, w, 0.0)

    cnt = (jnp.minimum(t_loc, half) + jnp.minimum(seg_len - t_loc, half)).astype(F32)
    q = u / cnt if transpose else u
    back, ahead, h = q, q, 1
    while h < half:
        back = back + shift(back, h)
        ahead = ahead + shift(ahead, -h)
        h *= 2
    if transpose:
        tot = back + shift(ahead, -1)
        return tot - u
    tot = shift(back, 1) + ahead
    return tot / cnt - u


def _pool_apply(src, col0_blk, out_dtype, n_ctx, pg, transpose, name):
    T = src.shape[0]

    def body(u_ref, o_ref):
        gi = pl.program_id(0)
        row = lax.broadcasted_iota(jnp.int32, (T, pg), 0)
        seg_len = jnp.where(row < n_ctx, n_ctx, GRID_W)
        t_loc = row & (seg_len - 1)
        u = u_ref[...].astype(F32)
        for k_idx, k in enumerate(POOL_WINDOWS):
            @pl.when(gi == k_idx)
            def _(k=k):
                o_ref[...] = _pool_core(u, k // 2, t_loc, seg_len, transpose).astype(o_ref.dtype)

    return pl.pallas_call(
        body, name=name, grid=(len(POOL_WINDOWS),),
        in_specs=[pl.BlockSpec((T, pg), lambda gi: (0, col0_blk + gi))],
        out_specs=pl.BlockSpec((T, pg), lambda gi: (0, gi)),
        out_shape=jax.ShapeDtypeStruct((T, pg * len(POOL_WINDOWS)), out_dtype),
        compiler_params=_cparams(("parallel",)))(src)


def _pool_mix_fwd(pm, pool_w, pool_scale, tr, name):
    T, W = pm.shape
    ng, pg = pool_w.shape[0], pool_w.shape[1]

    def body(p_ref, w_ref, s_ref, o_ref):
        o_ref[...] = (jnp.dot(p_ref[...], w_ref[...], preferred_element_type=F32) * s_ref[...]).astype(BF16)

    return pl.pallas_call(
        body, name=name, grid=(T // tr, ng),
        in_specs=[pl.BlockSpec((tr, pg), lambda i, g: (i, g)), pl.BlockSpec((None, pg, pg), lambda i, g: (g, 0, 0)),
                  pl.BlockSpec((1, pg), lambda i, g: (0, g))],
        out_specs=pl.BlockSpec((tr, pg), lambda i, g: (i, g)),
        out_shape=jax.ShapeDtypeStruct((T, W), BF16),
        compiler_params=_cparams(("parallel", "parallel")))(pm, pool_w, pool_scale)


def _pool_mix_bwd(pm, dpms, pool_w, pool_scale, tr, name):
    T, W = pm.shape
    ng, pg = pool_w.shape[0], pool_w.shape[1]

    def body(p_ref, d_ref, w_ref, s_ref, dp_ref, dw_ref, ds_ref):
        i = pl.program_id(1)

        @pl.when(i == 0)
        def _():
            dw_ref[...] = jnp.zeros_like(dw_ref)
            ds_ref[...] = jnp.zeros_like(ds_ref)

        p, w = p_ref[...], w_ref[...]
        d = d_ref[...].astype(F32)
        pmix = jnp.dot(p, w, preferred_element_type=F32)
        r = lax.broadcasted_iota(jnp.int32, (8, pg), 0)
        ds_ref[...] += jnp.where(r == 0, jnp.sum(d * pmix, axis=0, keepdims=True), 0.0)
        dmix = (d * s_ref[...]).astype(BF16)
        dp_ref[...] = lax.dot_general(dmix, w, (((1,), (1,)), ((), ())), preferred_element_type=F32)
        dw_ref[...] += lax.dot_general(p, dmix, (((0,), (0,)), ((), ())), preferred_element_type=F32)

    return pl.pallas_call(
        body, name=name, grid=(ng, T // tr),
        in_specs=[pl.BlockSpec((tr, pg), lambda g, i: (i, g)), pl.BlockSpec((tr, pg), lambda g, i: (i, g)),
                  pl.BlockSpec((None, pg, pg), lambda g, i: (g, 0, 0)), pl.BlockSpec((1, pg), lambda g, i: (0, g))],
        out_specs=[pl.BlockSpec((tr, pg), lambda g, i: (i, g)), pl.BlockSpec((None, pg, pg), lambda g, i: (g, 0, 0)),
                   pl.BlockSpec((8, pg), lambda g, i: (0, g))],
        out_shape=[jax.ShapeDtypeStruct((T, W), F32), jax.ShapeDtypeStruct((ng, pg, pg), F32),
                   jax.ShapeDtypeStruct((8, W), F32)],
        compiler_params=_cparams(("parallel", "arbitrary")))(pm, dpms, pool_w, pool_scale)


def _chunk_cumsum(v, upper):
    Q = v.shape[0]
    ii = lax.broadcasted_iota(jnp.int32, (Q, Q), 0)
    jj = lax.broadcasted_iota(jnp.int32, (Q, Q), 1)
    tri = ((jj >= ii) if upper else (jj <= ii)).astype(BF16)
    h1 = v.astype(BF16)
    r1 = v - h1.astype(F32)
    h2 = r1.astype(BF16)
    h3 = (r1 - h2.astype(F32)).astype(BF16)
    return (jnp.dot(tri, h1, preferred_element_type=F32) + jnp.dot(tri, h2, preferred_element_type=F32)
            + jnp.dot(tri, h3, preferred_element_type=F32))


def _split3(v):
    h1 = v.astype(BF16)
    r1 = v - h1.astype(F32)
    h2 = r1.astype(BF16)
    return h1, h2, (r1 - h2.astype(F32)).astype(BF16)


def _dt_prep(proj, dt_blk, bias, a_log, expand, n_heads, name):
    T = proj.shape[0]
    Wd = expand.shape[1]
    Q = CHUNK
    cps = 2 if (T // Q) % 2 == 0 else 1
    row = pl.BlockSpec((cps * Q, 128), lambda i: (i, 0))
    wide = pl.BlockSpec((cps * Q, Wd), lambda i: (i, 0))

    def body(r_ref, b_ref, al_ref, e_ref, l1_ref, l2_ref, l3_ref, dtb_ref, ein_ref, dte_ref, etot_ref):
        xv = r_ref[...] + b_ref[...]
        dt = jnp.maximum(xv, 0.0) + jnp.log(1.0 + jnp.exp(-jnp.abs(xv)))
        a_all = -jnp.exp(al_ref[...]) * dt
        fwd_col = lax.broadcasted_iota(jnp.int32, (Q, 128), 1) < n_heads
        lams, rests = [], []
        for c in range(cps):
            a = a_all[c * Q:(c + 1) * Q]
            lam_c = jnp.where(fwd_col, _chunk_cumsum(a, False), _chunk_cumsum(a, True))
            tot_c = jnp.where(fwd_col[0:1], lam_c[Q - 1:Q], lam_c[0:1])
            etot_ref[8 * c:8 * (c + 1), :] = jnp.broadcast_to(jnp.exp(tot_c), (8, 128))
            lams.append(lam_c)
            rests.append(tot_c - lam_c)
        lam = lams[0] if cps == 1 else jnp.concatenate(lams, axis=0)
        rest = rests[0] if cps == 1 else jnp.concatenate(rests, axis=0)
        l1_ref[...], l2_ref[...], l3_ref[...] = _split3(lam)
        ex = e_ref[...]

        def rep(v):
            p1, p2, p3 = _split3(v)
            return (jnp.dot(p1, ex, preferred_element_type=F32) + jnp.dot(p2, ex, preferred_element_type=F32)
                    + jnp.dot(p3, ex, preferred_element_type=F32))

        dtb_ref[...] = rep(dt)
        ein_ref[...] = rep(jnp.exp(lam))
        dte_ref[...] = rep(jnp.exp(rest))

    vec = pl.BlockSpec((1, 128), lambda i: (0, 0))
    return pl.pallas_call(
        body, name=name, grid=(T // (cps * Q),),
        in_specs=[pl.BlockSpec((cps * Q, 128), lambda i: (i, dt_blk)), vec, vec,
                  pl.BlockSpec((128, Wd), lambda i: (0, 0))],
        out_specs=[row, row, row, wide, wide, wide, pl.BlockSpec((8 * cps, 128), lambda i: (i, 0))],
        out_shape=[jax.ShapeDtypeStruct((T, 128), BF16)] * 3 + [jax.ShapeDtypeStruct((T, Wd), F32)] * 3
        + [jax.ShapeDtypeStruct((T // Q * 8, 128), F32)],
        compiler_params=_cparams(("parallel",)))(proj, bias, a_log, expand)


def _dt_bwd(proj, dt_blk, bias, a_log, ddt, dlam, n_heads, tr, name):
    T = proj.shape[0]
    row = pl.BlockSpec((tr, 128), lambda i: (i, 0))
    vec = pl.BlockSpec((1, 128), lambda i: (0, 0))

    def body(r_ref, b_ref, al_ref, ddt_ref, dl_ref, o_ref, st_ref):
        @pl.when(pl.program_id(0) == 0)
        def _():
            st_ref[...] = jnp.zeros_like(st_ref)

        xv = r_ref[...] + b_ref[...]
        dt = jnp.maximum(xv, 0.0) + jnp.log(1.0 + jnp.exp(-jnp.abs(xv)))
        a_neg = -jnp.exp(al_ref[...])
        col = lax.broadcasted_iota(jnp.int32, (CHUNK, 128), 1)
        dl = dl_ref[...]
        parts = []
        for k in range(tr // CHUNK):
            dk = dl[k * CHUNK:(k + 1) * CHUNK]
            parts.append(jnp.where(col < n_heads, _chunk_cumsum(dk, True), _chunk_cumsum(dk, False)))
        dav = jnp.concatenate(parts, axis=0)
        draw = (ddt_ref[...] + dav * a_neg) * jax.nn.sigmoid(xv)
        o_ref[...] = draw.astype(BF16)
        r = lax.broadcasted_iota(jnp.int32, (8, 128), 0)
        st_ref[...] += (jnp.where(r == 0, jnp.sum(draw, axis=0, keepdims=True), 0.0)
                        + jnp.where(r == 1, jnp.sum(dav * dt, axis=0, keepdims=True) * a_neg, 0.0))

    return pl.pallas_call(
        body, name=name, grid=(T // tr,),
        in_specs=[pl.BlockSpec((tr, 128), lambda i: (i, dt_blk)), vec, vec, row, row],
        out_specs=[row, pl.BlockSpec((8, 128), lambda i: (0, 0))],
        out_shape=[jax.ShapeDtypeStruct((T, 128), BF16), jax.ShapeDtypeStruct((8, 128), F32)],
        compiler_params=_cparams(("arbitrary",)))(proj, bias, a_log, ddt, dlam)


def _scan_chunk(d, pos, nc_ctx, nc):
    rev = jnp.where(pos < nc_ctx, nc_ctx - 1 - pos, nc - 1 - (pos - nc_ctx))
    return jnp.where(d == 0, pos, rev)


def _chunk_mask(d):
    ii = lax.broadcasted_iota(jnp.int32, (CHUNK, CHUNK), 0)
    jj = lax.broadcasted_iota(jnp.int32, (CHUNK, CHUNK), 1)
    return (ii - jj) * jnp.where(d == 0, 1, -1) >= 0


def _ssd_specs(T, G, n_ctx, gpb, chunk_of):
    R, P, N, Q = HPG, HEADDIM, STATE, CHUNK
    H = G * R
    nc, nc_ctx = T // Q, n_ctx // Q
    xw, bw = gpb * R * P, gpb * N
    b_blk0 = (H * P) // bw
    c_blk0 = b_blk0 + G // gpb

    def ch(d, s):
        return chunk_of(d, s, nc_ctx, nc)

    return dict(
        x=pl.BlockSpec((Q, xw), lambda d, g, s: (ch(d, s), g)),
        b=pl.BlockSpec((Q, bw), lambda d, g, s: (ch(d, s), b_blk0 + g)),
        c=pl.BlockSpec((Q, bw), lambda d, g, s: (ch(d, s), c_blk0 + g)),
        col=pl.BlockSpec((None, gpb, Q, R), lambda d, g, s: (d, g, ch(d, s), 0)),
        row=pl.BlockSpec((None, gpb, R, Q), lambda d, g, s: (d, g, 0, ch(d, s))),
        rep=pl.BlockSpec((Q, xw), lambda d, g, s: (ch(d, s), d * (G // gpb) + g)),
        lam_a=pl.BlockSpec((None, gpb * R, 16, Q), lambda d, g, s: (d, g, 0, ch(d, s))),
        lam_b=pl.BlockSpec((None, gpb * R, 16, Q), lambda d, g, s: (d, g, 0, ch(d, s))),
        etot=pl.BlockSpec((None, gpb, None, 8, 128), lambda d, g, s: (d, g, ch(d, s), 0, 0)),
        dsk=pl.BlockSpec((None, 8, xw), lambda d, g, s: (d, 0, g)),
        xd=pl.BlockSpec((None, Q, xw), lambda d, g, s: (d, ch(d, s), g)),
        bd=pl.BlockSpec((None, Q, bw), lambda d, g, s: (d, ch(d, s), g)),
        st=pl.BlockSpec((None, None, gpb * R // 2, 2 * P, N), lambda d, g, s: (d, ch(d, s), g, 0, 0)),
    )


class _Exchange:
    def __init__(self, operand, out_sds, sems, start, finish):
        self.operand, self.out_sds, self.sems, self.start, self.finish = operand, out_sds, sems, start, finish


def _call_with_exchange(body, exch, *, name, grid, in_specs, out_specs, out_shape, scratch_shapes, operands):
    if exch is None:
        return pl.pallas_call(body, name=name, grid=grid, in_specs=in_specs, out_specs=out_specs,
                              out_shape=out_shape, scratch_shapes=scratch_shapes,
                              compiler_params=_cparams(("arbitrary",) * len(grid)))(*operands)
    n_in, n_out, n_scr = len(in_specs), len(out_specs), len(scratch_shapes)

    def fused(*refs):
        ins, c_in = refs[:n_in], refs[n_in]
        outs, c_out = refs[n_in + 1:n_in + 1 + n_out], refs[n_in + 1 + n_out]
        scr = refs[n_in + 2 + n_out:n_in + 2 + n_out + n_scr]
        sems = refs[n_in + 2 + n_out + n_scr:]
        ids = [pl.program_id(a) for a in range(len(grid))]
        first = functools.reduce(lambda p, q: p & q, [i == 0 for i in ids])
        last = functools.reduce(lambda p, q: p & q, [i == n - 1 for i, n in zip(ids, grid)])

        @pl.when(first)
        def _():
            exch.start(c_in, c_out, *sems)

        body(*ins, *outs, *scr)

        @pl.when(last)
        def _():
            exch.finish(c_in, c_out, *sems)

    return pl.pallas_call(fused, name=name, grid=grid, in_specs=list(in_specs) + [ANY],
                          out_specs=list(out_specs) + [ANY], out_shape=list(out_shape) + [exch.out_sds],
                          scratch_shapes=list(scratch_shapes) + list(exch.sems),
                          compiler_params=_cparams(("arbitrary",) * len(grid)))(*operands, exch.operand)


def _ssd_fwd(xbc, dtb, ein, dte, lam_a, lam_b, etot, dsk, G, n_ctx, gpb, name, exch=None):
    T = xbc.shape[0]
    R, P, N, Q = HPG, HEADDIM, STATE, CHUNK
    H = G * R
    nc = T // Q
    sp = _ssd_specs(T, G, n_ctx, gpb, _scan_chunk)

    def body(x_ref, b_ref, c_ref, dt_ref, ein_ref, dte_ref, la_ref, lb_ref, et_ref, dsk_ref, y_ref, st_ref, S):
        d, s = pl.program_id(0), pl.program_id(2)

        @pl.when(s == 0)
        def _():
            S[...] = jnp.zeros_like(S)

        mask = _chunk_mask(d)
        head0 = lax.broadcasted_iota(jnp.int32, (Q, 2 * P), 1) < P
        rows0 = lax.broadcasted_iota(jnp.int32, (2 * P, N), 0) < P
        for gg in range(gpb):
            Bm = b_ref[:, gg * N:(gg + 1) * N].astype(BF16)
            Cm = c_ref[:, gg * N:(gg + 1) * N].astype(BF16)
            Gm = lax.dot_general(Cm, Bm, (((1,), (1,)), ((), ())), preferred_element_type=F32)
            for k in range(R // 2):
                pk = gg * (R // 2) + k
                sl = slice(pk * 2 * P, (pk + 1) * 2 * P)
                xp = x_ref[:, sl]
                xc = xp * dt_ref[:, sl]
                s_in = S[pk]
                y = lax.dot_general(Cm, s_in.astype(BF16), (((1,), (1,)), ((), ())),
                                    preferred_element_type=F32) * ein_ref[:, sl] + dsk_ref[0:1, sl] * xp
                for j in range(2):
                    hr = 2 * pk + j
                    diff = lax.dot_general(la_ref[hr], lb_ref[hr], (((0,), (0,)), ((), ())), preferred_element_type=F32)
                    ldec = jnp.exp(jnp.where(mask, diff, NEG_BIG))
                    xc_j = (jnp.where(head0, xc, 0.0) if j == 0 else jnp.where(head0, 0.0, xc)).astype(BF16)
                    y = y + jnp.dot((Gm * ldec).astype(BF16), xc_j, preferred_element_type=F32)
                y_ref[:, sl] = y
                st_ref[pk] = s_in
                e_all = jnp.where(rows0, et_ref[gg, 2 * k:2 * k + 1, :], et_ref[gg, 2 * k + 1:2 * k + 2, :])
                xd = (xc * dte_ref[:, sl]).astype(BF16)
                S[pk] = e_all * s_in + lax.dot_general(xd, Bm, (((0,), (0,)), ((), ())),
                                                       preferred_element_type=F32)

    return _call_with_exchange(
        body, exch, name=name, grid=(2, G // gpb, nc),
        in_specs=[sp["x"], sp["b"], sp["c"], sp["rep"], sp["rep"], sp["rep"], sp["lam_a"], sp["lam_b"], sp["etot"],
                  sp["dsk"]],
        out_specs=[sp["xd"], sp["st"]],
        out_shape=[jax.ShapeDtypeStruct((2, T, H * P), F32), jax.ShapeDtypeStruct((2, nc, H // 2, 2 * P, N), F32)],
        scratch_shapes=[pltpu.VMEM((gpb * R // 2, 2 * P, N), F32)],
        operands=(xbc, xbc, xbc, dtb, ein, dte, lam_a, lam_b, etot, dsk))


def _ssd_bwd(xbc, dy, states, dtb, ein, dte, lam_a, lam_b, etot, G, n_ctx, gpb, name, exch=None):
    T = xbc.shape[0]
    R, P, N, Q = HPG, HEADDIM, STATE, CHUNK
    H = G * R
    nc = T // Q
    sp = _ssd_specs(T, G, n_ctx, gpb, lambda d, s, nc_ctx, n: _scan_chunk(d, n - 1 - s, nc_ctx, n))

    def body(x_ref, b_ref, c_ref, dy_ref, st_ref, dt_ref, ein_ref, dte_ref, la_ref, lb_ref, et_ref,
             dx_ref, db_ref, dc_ref, ddt_ref, dlc_ref, dlr_ref, dS):
        d, s = pl.program_id(0), pl.program_id(2)

        @pl.when(s == 0)
        def _():
            dS[...] = jnp.zeros_like(dS)

        mask = _chunk_mask(d)
        ri = lax.broadcasted_iota(jnp.int32, (Q, 1), 0)
        is_last = ri == jnp.where(d == 0, Q - 1, 0)
        head0 = lax.broadcasted_iota(jnp.int32, (Q, 2 * P), 1) < P
        rows0 = lax.broadcasted_iota(jnp.int32, (2 * P, N), 0) < P

        def total(v):
            return jnp.sum(jnp.sum(v, axis=1, keepdims=True), axis=0, keepdims=True)

        for gg in range(gpb):
            Bm = b_ref[:, gg * N:(gg + 1) * N].astype(BF16)
            Cm = c_ref[:, gg * N:(gg + 1) * N].astype(BF16)
            Gm = lax.dot_general(Cm, Bm, (((1,), (1,)), ((), ())), preferred_element_type=F32)
            dG = jnp.zeros((Q, Q), F32)
            dB = jnp.zeros((Q, N), F32)
            dC = jnp.zeros((Q, N), F32)
            for k in range(R // 2):
                pk = gg * (R // 2) + k
                sl = slice(pk * 2 * P, (pk + 1) * 2 * P)
                e_in = ein_ref[:, sl]
                dte = dte_ref[:, sl]
                e_all = jnp.where(rows0, et_ref[gg, 2 * k:2 * k + 1, :], et_ref[gg, 2 * k + 1:2 * k + 2, :])
                xp = x_ref[:, sl]
                dtp = dt_ref[:, sl]
                xc = xp * dtp
                xc_b = xc.astype(BF16)
                dyp = dy_ref[:, sl]
                s_in = st_ref[pk]
                s_in_b = s_in.astype(BF16)
                ds_out = dS[pk]
                ds_out_b = ds_out.astype(BF16)
                y_int = lax.dot_general(Cm, s_in_b, (((1,), (1,)), ((), ())), preferred_element_type=F32) * e_in
                b_ds = lax.dot_general(Bm, ds_out_b, (((1,), (1,)), ((), ())), preferred_element_type=F32)
                dxc = dte * b_ds
                u = xc * dxc
                v = dyp * y_int - u
                sse = ds_out * s_in * e_all
                for j in range(2):
                    hr, r = 2 * pk + j, 2 * k + j

                    def pick(a, m0=head0, j=j):
                        return jnp.where(m0, a, 0.0) if j == 0 else jnp.where(m0, 0.0, a)

                    diff = lax.dot_general(la_ref[hr], lb_ref[hr], (((0,), (0,)), ((), ())), preferred_element_type=F32)
                    ldec = jnp.exp(jnp.where(mask, diff, NEG_BIG))
                    dy_j = pick(dyp).astype(BF16)
                    dM = lax.dot_general(dy_j, xc_b, (((1,), (1,)), ((), ())), preferred_element_type=F32)
                    dMl = dM * ldec
                    Wm = dMl * Gm
                    dlam_c = jnp.sum(Wm, axis=1, keepdims=True) + jnp.sum(pick(v), axis=1, keepdims=True)
                    last = total(pick(sse, rows0)) + total(pick(u))
                    dlc_ref[gg, :, r:r + 1] = dlam_c + jnp.where(is_last, last, 0.0)
                    dlr_ref[gg, r:r + 1, :] = -jnp.sum(Wm, axis=0, keepdims=True)
                    dxc = dxc + lax.dot_general((Gm * ldec).astype(BF16), dy_j, (((0,), (0,)), ((), ())),
                                                preferred_element_type=F32)
                    dG = dG + dMl
                dx_ref[:, sl] = dxc * dtp
                t = dxc * xp
                ddt_ref[gg, :, 2 * k:2 * k + 1] = jnp.sum(jnp.where(head0, t, 0.0), axis=1, keepdims=True)
                ddt_ref[gg, :, 2 * k + 1:2 * k + 2] = jnp.sum(jnp.where(head0, 0.0, t), axis=1, keepdims=True)
                edy_b = (e_in * dyp).astype(BF16)
                dC = dC + jnp.dot(edy_b, s_in_b, preferred_element_type=F32)
                dB = dB + jnp.dot((dte * xc).astype(BF16), ds_out_b, preferred_element_type=F32)
                dS[pk] = e_all * ds_out + lax.dot_general(edy_b, Cm, (((0,), (0,)), ((), ())),
                                                          preferred_element_type=F32)
            dG_b = dG.astype(BF16)
            dc_ref[:, gg * N:(gg + 1) * N] = dC + jnp.dot(dG_b, Bm, preferred_element_type=F32)
            db_ref[:, gg * N:(gg + 1) * N] = dB + lax.dot_general(dG_b, Cm, (((0,), (0,)), ((), ())),
                                                                  preferred_element_type=F32)

    return _call_with_exchange(
        body, exch, name=name, grid=(2, G // gpb, nc),
        in_specs=[sp["x"], sp["b"], sp["c"], sp["x"], sp["st"], sp["rep"], sp["rep"], sp["rep"], sp["lam_a"],
                  sp["lam_b"], sp["etot"]],
        out_specs=[sp["xd"], sp["bd"], sp["bd"], sp["col"], sp["col"], sp["row"]],
        out_shape=[jax.ShapeDtypeStruct((2, T, H * P), F32), jax.ShapeDtypeStruct((2, T, G * N), F32),
                   jax.ShapeDtypeStruct((2, T, G * N), F32), jax.ShapeDtypeStruct((2, G, T, R), F32),
                   jax.ShapeDtypeStruct((2, G, T, R), F32), jax.ShapeDtypeStruct((2, G, R, T), F32)],
        scratch_shapes=[pltpu.VMEM((gpb * R // 2, 2 * P, N), F32)],
        operands=(xbc, xbc, xbc, dy, states, dtb, ein, dte, lam_a, lam_b, etot))


def _gnorm_fwd(y2, proj, w, gs, tr, name):
    T, HP = y2.shape[1], y2.shape[2]

    def body(y_ref, z_ref, w_ref, o_ref):
        yz = (y_ref[0] + y_ref[1]) * _silu(z_ref[...])
        for g in range(HP // gs):
            v = yz[:, g * gs:(g + 1) * gs]
            rstd = lax.rsqrt(jnp.mean(v * v, axis=-1, keepdims=True) + EPS)
            o_ref[:, g * gs:(g + 1) * gs] = (v * rstd * w_ref[:, g * gs:(g + 1) * gs]).astype(BF16)

    return pl.pallas_call(
        body, name=name, grid=(T // tr,),
        in_specs=[pl.BlockSpec((2, tr, HP), lambda i: (0, i, 0)), pl.BlockSpec((tr, HP), lambda i: (i, 0)),
                  pl.BlockSpec((1, HP), lambda i: (0, 0))],
        out_specs=pl.BlockSpec((tr, HP), lambda i: (i, 0)),
        out_shape=jax.ShapeDtypeStruct((T, HP), BF16),
        compiler_params=_cparams(("parallel",)))(y2, proj, w)


def _gnorm_bwd(y2, proj, w, dyn, gs, tr, name):
    T, HP = y2.shape[1], y2.shape[2]
    row = pl.BlockSpec((tr, HP), lambda i: (i, 0))

    def body(y_ref, z_ref, w_ref, d_ref, dy_ref, dz_ref, dw_ref):
        @pl.when(pl.program_id(0) == 0)
        def _():
            dw_ref[...] = jnp.zeros_like(dw_ref)

        yv = y_ref[0] + y_ref[1]
        zv = z_ref[...]
        sz = _silu(zv)
        yz = yv * sz
        dv = d_ref[...]
        r = lax.broadcasted_iota(jnp.int32, (8, gs), 0)
        for g in range(HP // gs):
            sl = slice(g * gs, (g + 1) * gs)
            v = yz[:, sl]
            rstd = lax.rsqrt(jnp.mean(v * v, axis=-1, keepdims=True) + EPS)
            xhat = v * rstd
            dyn_g = dv[:, sl]
            dhat = dyn_g * w_ref[:, sl]
            dyz = rstd * (dhat - xhat * jnp.mean(dhat * xhat, axis=-1, keepdims=True))
            dy_ref[:, sl] = dyz * sz[:, sl]
            dz_ref[:, sl] = (dyz * yv[:, sl] * _dsilu(zv[:, sl])).astype(BF16)
            dw_ref[:, sl] += jnp.where(r == 0, jnp.sum(dyn_g * xhat, axis=0, keepdims=True), 0.0)

    return pl.pallas_call(
        body, name=name, grid=(T // tr,),
        in_specs=[pl.BlockSpec((2, tr, HP), lambda i: (0, i, 0)), row, pl.BlockSpec((1, HP), lambda i: (0, 0)), row],
        out_specs=[row, row, pl.BlockSpec((8, HP), lambda i: (0, 0))],
        out_shape=[jax.ShapeDtypeStruct((T, HP), F32), jax.ShapeDtypeStruct((T, HP), BF16),
                   jax.ShapeDtypeStruct((8, HP), F32)],
        compiler_params=_cparams(("arbitrary",)))(y2, proj, w, dyn)


def _merge_fwd(proj, g1_blk, o_ssd, o_pool, tr, name):
    T, D = o_ssd.shape
    row = pl.BlockSpec((tr, D), lambda i: (i, 0))

    def body(g1_ref, g2_ref, a_ref, b_ref, o_ref):
        o_ref[...] = (jax.nn.sigmoid(g1_ref[...]) * a_ref[...]
                      + jax.nn.sigmoid(g2_ref[...]) * b_ref[...]).astype(BF16)

    return pl.pallas_call(
        body, name=name, grid=(T // tr,),
        in_specs=[pl.BlockSpec((tr, D), lambda i: (i, g1_blk)), pl.BlockSpec((tr, D), lambda i: (i, g1_blk + 1)),
                  row, row],
        out_specs=row, out_shape=jax.ShapeDtypeStruct((T, D), BF16),
        compiler_params=_cparams(("parallel",)))(proj, proj, o_ssd, o_pool)


def _merge_bwd(proj, g1_blk, o_ssd, o_pool, dmg, tr, name):
    T, D = o_ssd.shape
    row = pl.BlockSpec((tr, D), lambda i: (i, 0))

    def body(g1_ref, g2_ref, a_ref, b_ref, d_ref, da_ref, db_ref, dg_ref):
        s1, s2 = jax.nn.sigmoid(g1_ref[...]), jax.nn.sigmoid(g2_ref[...])
        dv = d_ref[...]
        da_ref[...] = (s1 * dv).astype(BF16)
        db_ref[...] = (s2 * dv).astype(BF16)
        dg_ref[:, :D] = (dv * a_ref[...] * s1 * (1.0 - s1)).astype(BF16)
        dg_ref[:, D:] = (dv * b_ref[...] * s2 * (1.0 - s2)).astype(BF16)

    return pl.pallas_call(
        body, name=name, grid=(T // tr,),
        in_specs=[pl.BlockSpec((tr, D), lambda i: (i, g1_blk)), pl.BlockSpec((tr, D), lambda i: (i, g1_blk + 1)),
                  row, row, row],
        out_specs=[row, row, pl.BlockSpec((tr, 2 * D), lambda i: (i, 0))],
        out_shape=[jax.ShapeDtypeStruct((T, D), BF16), jax.ShapeDtypeStruct((T, D), BF16),
                   jax.ShapeDtypeStruct((T, 2 * D), BF16)],
        compiler_params=_cparams(("parallel",)))(proj, proj, o_ssd, o_pool, dmg)


def _swiglu_fwd(gu, tr, name):
    T, F2 = gu.shape
    F = F2 // 2
    tc = _tile(F, (1408, 768, 512, 256, 128))
    nb = F // tc

    def body(a_ref, b_ref, o_ref):
        o_ref[...] = (_silu(a_ref[...]) * b_ref[...]).astype(BF16)

    return pl.pallas_call(
        body, name=name, grid=(T // tr, nb),
        in_specs=[pl.BlockSpec((tr, tc), lambda i, j: (i, j)), pl.BlockSpec((tr, tc), lambda i, j: (i, nb + j))],
        out_specs=pl.BlockSpec((tr, tc), lambda i, j: (i, j)),
        out_shape=jax.ShapeDtypeStruct((T, F), BF16),
        compiler_params=_cparams(("parallel", "parallel")))(gu, gu)


def _swiglu_bwd(gu, dact, tr, name):
    T, F2 = gu.shape
    F = F2 // 2
    tc = _tile(F, (1408, 768, 512, 256, 128))
    nb = F // tc

    def body(a_ref, b_ref, d_ref, o_ref):
        is_a = pl.program_id(1) < nb
        av, bv, dv = a_ref[...], b_ref[...], d_ref[...]
        o_ref[...] = jnp.where(is_a, dv * bv * _dsilu(av), dv * _silu(av)).astype(BF16)

    return pl.pallas_call(
        body, name=name, grid=(T // tr, 2 * nb),
        in_specs=[pl.BlockSpec((tr, tc), lambda i, j: (i, j % nb)), pl.BlockSpec((tr, tc), lambda i, j: (i, nb + j % nb)),
                  pl.BlockSpec((tr, tc), lambda i, j: (i, j % nb))],
        out_specs=pl.BlockSpec((tr, tc), lambda i, j: (i, j)),
        out_shape=jax.ShapeDtypeStruct((T, F2), BF16),
        compiler_params=_cparams(("parallel", "parallel")))(gu, gu, dact)


def _adamw(w, g, m, v, name):
    Rr, C = w.shape
    tr = _tile(Rr, (256, 128, 64, 32, 16, 8))
    row = pl.BlockSpec((tr, C), lambda i: (i, 0))

    def body(w_ref, g_ref, m_ref, v_ref, d_ref, mo_ref, vo_ref):
        gv = g_ref[...]
        mn = ADAM_B1 * m_ref[...] + (1.0 - ADAM_B1) * gv
        vn = ADAM_B2 * v_ref[...] + (1.0 - ADAM_B2) * (gv * gv)
        m_hat = mn / (1.0 - ADAM_B1 ** ADAM_STEP)
        v_hat = vn / (1.0 - ADAM_B2 ** ADAM_STEP)
        d_ref[...] = -ADAM_LR * (m_hat / (jnp.sqrt(v_hat) + ADAM_EPS) + ADAM_WD * w_ref[...])
        mo_ref[...] = mn
        vo_ref[...] = vn

    sds = jax.ShapeDtypeStruct((Rr, C), F32)
    return pl.pallas_call(body, name=name, grid=(Rr // tr,), in_specs=[row] * 4, out_specs=[row] * 3,
                          out_shape=[sds] * 3, compiler_params=_cparams(("parallel",)))(w, g, m, v)


def _sum_slots(x, name):
    n, Rr, C = x.shape
    tr = _tile(Rr, (512, 256, 128, 64, 32, 16, 8))

    def body(x_ref, o_ref):
        acc = x_ref[0].astype(F32)
        for k in range(1, n):
            acc = acc + x_ref[k].astype(F32)
        o_ref[...] = acc

    return pl.pallas_call(body, name=name, grid=(Rr // tr,),
                          in_specs=[pl.BlockSpec((n, tr, C), lambda i: (0, i, 0))],
                          out_specs=pl.BlockSpec((tr, C), lambda i: (i, 0)),
                          out_shape=jax.ShapeDtypeStruct((Rr, C), F32),
                          compiler_params=_cparams(("parallel",)))(x)


def _place():
    return lax.axis_index("x"), lax.axis_index("y"), lax.axis_index("c")


def _all_gather(x, name):
    Rr, C = x.shape

    def body(x_ref, out_ref, send_sems, recv_sems, local_sem):
        _gather_start(x_ref, out_ref, send_sems, recv_sems, local_sem)
        _gather_finish(x_ref, out_ref, send_sems, recv_sems, local_sem)

    return pl.pallas_call(
        body, name=name, in_specs=[ANY], out_specs=ANY,
        out_shape=jax.ShapeDtypeStruct((N_DEV, Rr, C), x.dtype), scratch_shapes=_GATHER_SEMS,
    )(x)


_GATHER_SEMS = [pltpu.SemaphoreType.DMA((7,)), pltpu.SemaphoreType.DMA((7,)), pltpu.SemaphoreType.DMA]


def _gather_copies(x_ref, out_ref, send_sems, recv_sems, local_sem):
    mx, my, mc = _place()
    me, sibling = (mx, my, mc), (mx, my, 1 - mc)
    chips = [(1 - mx, my), (mx, 1 - my), (1 - mx, 1 - my)]

    def slot(px, py, pc):
        return out_ref.at[4 * px + 2 * py + pc]

    def copy(k, block, to, src=None):
        return pltpu.make_async_remote_copy(
            src_ref=slot(*block) if src is None else src, dst_ref=slot(*block),
            send_sem=send_sems.at[k], recv_sem=recv_sems.at[k],
            device_id=to, device_id_type=pl.DeviceIdType.MESH)

    return dict(
        mine=pltpu.make_async_copy(x_ref, slot(*me), local_sem),
        first=[copy(0, me, sibling, src=x_ref)] + [copy(1 + j, me, (*chip, mc), src=x_ref)
                                                   for j, chip in enumerate(chips)],
        passed=[copy(4 + j, (*chip, mc), sibling) for j, chip in enumerate(chips)],
        from_chips=[copy(1 + j, (*chip, mc), me) for j, chip in enumerate(chips)],
        from_sibling=[copy(0, sibling, me)] + [copy(4 + j, (*chip, 1 - mc), me) for j, chip in enumerate(chips)],
    )


def _gather_start(*refs):
    cps = _gather_copies(*refs)
    cps["mine"].start()
    for cp in cps["first"]:
        cp.start()


def _gather_finish(*refs):
    cps = _gather_copies(*refs)
    for j in range(3):
        cps["from_chips"][j].wait_recv()
        cps["passed"][j].start()
    for cp in cps["from_sibling"]:
        cp.wait_recv()
    for cp in cps["first"] + cps["passed"]:
        cp.wait_send()
    cps["mine"].wait()


def _pair_exchange(buf, name):
    def body(b_ref, got_ref, send_sems, recv_sems):
        _pair_start(b_ref, got_ref, send_sems, recv_sems)
        _pair_finish(b_ref, got_ref, send_sems, recv_sems)

    return pl.pallas_call(
        body, name=name, in_specs=[ANY], out_specs=ANY,
        out_shape=jax.ShapeDtypeStruct(buf.shape[1:], buf.dtype), scratch_shapes=_PAIR_SEMS,
    )(buf)


_PAIR_PARTS = 4
_PAIR_SEMS = [pltpu.SemaphoreType.DMA((4 * _PAIR_PARTS,)), pltpu.SemaphoreType.DMA((4 * _PAIR_PARTS,))]


def _pair_copies(b_ref, got_ref, send_sems, recv_sems):
    mx, my, mc = _place()
    n, Rr = got_ref.shape[0], got_ref.shape[1]
    pr = Rr // _PAIR_PARTS
    return [pltpu.make_async_remote_copy(
        src_ref=b_ref.at[1 - mc, q, pl.ds(p * pr, pr)], dst_ref=got_ref.at[q, pl.ds(p * pr, pr)],
        send_sem=send_sems.at[q * _PAIR_PARTS + p], recv_sem=recv_sems.at[q * _PAIR_PARTS + p],
        device_id=(mx, my, 1 - mc), device_id_type=pl.DeviceIdType.MESH)
        for q in range(n) for p in range(_PAIR_PARTS)]


def _pair_start(*refs):
    for cp in _pair_copies(*refs):
        cp.start()


def _pair_finish(*refs):
    for cp in _pair_copies(*refs):
        cp.wait()


def _pair_add(buf, got, my_c, name):
    _, n, Rr, C = buf.shape
    tr = _tile(Rr, (512, 256, 128, 64, 32, 16))

    def body(c_ref, b_ref, g_ref, o_ref):
        o_ref[...] = (b_ref[...].astype(F32) + g_ref[...].astype(F32)).astype(BF16)

    return pl.pallas_call(
        body, name=name,
        grid_spec=pltpu.PrefetchScalarGridSpec(
            num_scalar_prefetch=1, grid=(n, Rr // tr),
            in_specs=[pl.BlockSpec((None, None, tr, C), lambda q, i, c: (c[0], q, i, 0)),
                      pl.BlockSpec((None, tr, C), lambda q, i, c: (q, i, 0))],
            out_specs=pl.BlockSpec((None, tr, C), lambda q, i, c: (q, i, 0))),
        out_shape=jax.ShapeDtypeStruct((n, Rr, C), BF16),
        compiler_params=_cparams(("parallel", "parallel")))(my_c, buf, got)


def _chip_exchange(red, name):
    def body(r_ref, out_ref, send_sems, recv_sems, local_sem):
        _chip_exchange_start(r_ref, out_ref, send_sems, recv_sems, local_sem)
        _chip_exchange_finish(r_ref, out_ref, send_sems, recv_sems, local_sem)

    return pl.pallas_call(
        body, name=name, in_specs=[ANY], out_specs=ANY,
        out_shape=jax.ShapeDtypeStruct(red.shape, red.dtype), scratch_shapes=_CHIP_SEMS,
    )(red)


_CHIP_SEMS = [pltpu.SemaphoreType.DMA((3,)), pltpu.SemaphoreType.DMA((3,)), pltpu.SemaphoreType.DMA]


def _chip_exchange_copies(r_ref, out_ref, send_sems, recv_sems, local_sem):
    mx, my, mc = _place()
    chips = [(1 - mx, my), (mx, 1 - my), (1 - mx, 1 - my)]

    def copy(k, src_slot, dst_slot, to):
        return pltpu.make_async_remote_copy(
            src_ref=r_ref.at[src_slot], dst_ref=out_ref.at[dst_slot],
            send_sem=send_sems.at[k], recv_sem=recv_sems.at[k],
            device_id=(*to, mc), device_id_type=pl.DeviceIdType.MESH)

    return dict(
        mine=pltpu.make_async_copy(r_ref.at[2 * mx + my], out_ref.at[2 * mx + my], local_sem),
        sends=[copy(k, 2 * px + py, 2 * mx + my, (px, py)) for k, (px, py) in enumerate(chips)],
        recvs=[copy(k, 2 * px + py, 2 * px + py, (px, py)) for k, (px, py) in enumerate(chips)],
    )


def _chip_exchange_start(*refs):
    cps = _chip_exchange_copies(*refs)
    cps["mine"].start()
    for cp in cps["sends"]:
        cp.start()


def _chip_exchange_finish(*refs):
    cps = _chip_exchange_copies(*refs)
    for cp in cps["recvs"]:
        cp.wait_recv()
    for cp in cps["sends"]:
        cp.wait_send()
    cps["mine"].wait()


def _pad_rows(a, rows):
    return jnp.pad(a, ((0, rows - a.shape[0]), (0, 0)))


class _Layout:
    def __init__(self, D, shards):
        self.D = D
        self.pieces = []
        off = 0
        for name, layer, rows in shards:
            pr = _round_up(rows, 16)
            self.pieces.append((name, layer, rows, pr, off))
            off += pr
        self.rows = _round_up(off, 256)

    def find(self, name, layer):
        for p in self.pieces:
            if p[0] == name and p[1] == layer:
                return p
        raise KeyError(name)


def kernel(x, c, ctx, c_ctx, w_ada, b_ada, g_mix, w_in, conv_w, conv_b, dt_bias, a_log, d_skip, ssd_norm_w, w_ssd_out, pool_w, pool_scale, w_pool_out, w_out, g_ffn, w_gate_up, w_down, g_final, loss_target, m_c_ctx, m_w_ada, m_b_ada, m_g_mix, m_w_in, m_conv_w, m_conv_b, m_dt_bias, m_a_log, m_d_skip, m_ssd_norm_w, m_w_ssd_out, m_pool_w, m_pool_scale, m_w_pool_out, m_w_out, m_g_ffn, m_w_gate_up, m_w_down, m_g_final, v_c_ctx, v_w_ada, v_b_ada, v_g_mix, v_w_in, v_conv_w, v_conv_b, v_dt_bias, v_a_log, v_d_skip, v_ssd_norm_w, v_w_ssd_out, v_pool_w, v_pool_scale, v_w_pool_out, v_w_out, v_g_ffn, v_w_gate_up, v_w_down, v_g_final):
    weights = dict(c_ctx=c_ctx, w_ada=w_ada, b_ada=b_ada, g_mix=g_mix, w_in=w_in, conv_w=conv_w, conv_b=conv_b,
                   dt_bias=dt_bias, a_log=a_log, d_skip=d_skip, ssd_norm_w=ssd_norm_w, w_ssd_out=w_ssd_out,
                   pool_w=pool_w, pool_scale=pool_scale, w_pool_out=w_pool_out, w_out=w_out, g_ffn=g_ffn,
                   w_gate_up=w_gate_up, w_down=w_down, g_final=g_final)
    moms_m = dict(c_ctx=m_c_ctx, w_ada=m_w_ada, b_ada=m_b_ada, g_mix=m_g_mix, w_in=m_w_in, conv_w=m_conv_w,
                  conv_b=m_conv_b, dt_bias=m_dt_bias, a_log=m_a_log, d_skip=m_d_skip, ssd_norm_w=m_ssd_norm_w,
                  w_ssd_out=m_w_ssd_out, pool_w=m_pool_w, pool_scale=m_pool_scale, w_pool_out=m_w_pool_out,
                  w_out=m_w_out, g_ffn=m_g_ffn, w_gate_up=m_w_gate_up, w_down=m_w_down, g_final=m_g_final)
    moms_v = dict(c_ctx=v_c_ctx, w_ada=v_w_ada, b_ada=v_b_ada, g_mix=v_g_mix, w_in=v_w_in, conv_w=v_conv_w,
                  conv_b=v_conv_b, dt_bias=v_dt_bias, a_log=v_a_log, d_skip=v_d_skip, ssd_norm_w=v_ssd_norm_w,
                  w_ssd_out=v_w_ssd_out, pool_w=v_pool_w, pool_scale=v_pool_scale, w_pool_out=v_w_pool_out,
                  w_out=v_w_out, g_ffn=v_g_ffn, w_gate_up=v_w_gate_up, w_down=v_w_down, g_final=v_g_final)
    order = ["c_ctx", "w_ada", "b_ada", "g_mix", "w_in", "conv_w", "conv_b", "dt_bias", "a_log", "d_skip",
             "ssd_norm_w", "w_ssd_out", "pool_w", "pool_scale", "w_pool_out", "w_out", "g_ffn", "w_gate_up",
             "w_down", "g_final"]
    big = ["w_ada", "w_in", "conv_w", "w_ssd_out", "pool_w", "w_pool_out", "w_out", "w_gate_up", "w_down"]
    small = [n for n in order if n not in big]

    depth = w_in.shape[0]
    L, D = x.shape[1], x.shape[2]
    n_ctx = ctx.shape[1]
    T = n_ctx + L
    in_cols = w_in.shape[2] * N_DEV
    xbc_w = conv_w.shape[2] * N_DEV
    dinner = ssd_norm_w.shape[1]
    H = dt_bias.shape[2]
    G = H // HPG
    GN = G * STATE
    assert xbc_w == dinner + 2 * GN and dinner == H * HEADDIM
    assert in_cols == dinner + xbc_w + 2 * H + D + 2 * D
    assert dinner == 2 * D and GN == D and 2 * H <= 128
    F = w_down.shape[1] * N_DEV
    pg = pool_w.shape[3]
    tr = n_ctx
    assert L % tr == 0 and tr % GRID_W == 0 and tr % CHUNK == 0 and L % CHUNK == 0
    n_ctx_tiles = 1
    tr_big = _tile(T, (1088, 544, 512, 256, 128))
    tr_mid = _tile(T, (544, 512, 256, 128))
    NP =_round_up(9 * D + 128, 512)
    gs = dinner // G
    off_xbc, off_dt, off_pool = dinner, dinner + xbc_w, dinner + xbc_w + 2 * H
    off_gate = off_pool + D

    def shard_rows(name, l):
        w = weights[name][l]
        if name in ("w_ada", "w_in", "w_gate_up"):
            return w.T
        if name == "conv_w":
            w8 = _pad_rows(w, 8)
            hi = w8.astype(BF16)
            lo = (w8 - hi.astype(F32)).astype(BF16)
            return jnp.concatenate([hi, lo], axis=0).reshape(-1, D)
        if name == "pool_w":
            return w.reshape(-1, D)
        return w

    first_needed = ["w_ada", "w_in"]
    shard = {(n, l): shard_rows(n, l) for l in range(depth) for n in big}
    gather_groups = [[first_needed, [n for n in big if n not in first_needed]] for l in range(depth)]
    glays = [[_Layout(D, [(n, l, shard[(n, l)].shape[0]) for n in grp]) for grp in gather_groups[l]]
             for l in range(depth)]

    def packed(l, gi):
        lay = glays[l][gi]
        rows = jnp.concatenate([_pad_rows(shard[(n, l)].astype(BF16), lay.find(n, l)[3])
                                for n in gather_groups[l][gi]], axis=0)
        return _pad_rows(rows, lay.rows)

    def gather_exchange(l, gi):
        return _Exchange(packed(l, gi), jax.ShapeDtypeStruct((N_DEV, glays[l][gi].rows, D), BF16), _GATHER_SEMS,
                         _gather_start, _gather_finish)

    gathered = [[None] * len(g) for g in gather_groups]
    gathered[0][0] = _all_gather(packed(0, 0), "gather_weights")

    def full(name, l):
        gi = [name in grp for grp in gather_groups[l]].index(True)
        _, _, rows, _, off = glays[l][gi].find(name, l)
        return gathered[l][gi][:, off:off + rows, :]

    def w_inT_new(l):
        w = full("w_in", l).reshape(in_cols, D)
        parts = [w[:off_xbc], w[off_xbc:off_dt], w[off_pool:off_gate], w[off_gate:], w[off_dt:off_pool]]
        return _pad_rows(jnp.concatenate(parts, axis=0), NP)

    xs0 = jnp.concatenate([ctx[0], x[0]], axis=0)
    cc8 = _pad_rows(jnp.concatenate([c, c_ctx[None, :]], axis=0), 8)
    tgt = loss_target[0]

    def vec(a):
        return a.reshape(1, -1)

    def pad128(a):
        return jnp.pad(a.reshape(1, -1), ((0, 0), (0, 128 - 2 * H)))

    expand = (jnp.arange(128)[:, None] == jnp.arange(2 * H * HEADDIM)[None, :] // HEADDIM).astype(BF16)

    def from4(arr):
        return jnp.pad(arr.transpose(2, 0, 1, 3).reshape(T, 2 * H), ((0, 0), (0, 128 - 2 * H)))

    dt_blk = (9 * D) // 128
    conv_tc = 128
    ssd_gpb, ssd_gpb_bwd = min(G, 8), 2
    saved = []
    xcur = xs0
    for l in range(depth):
        W = dict(adaT=full("w_ada", l).reshape(6 * D, D), inT=w_inT_new(l))
        m6 = _ada_fwd(cc8, W["adaT"], vec(b_ada[l]), "ada_fwd")
        h = _norm_mod(xcur, vec(g_mix[l]), m6, 0, 1, n_ctx_tiles, tr, "norm_mod")
        if len(gather_groups[l]) > 1:
            proj, gathered[l][1] = _mm(h, W["inT"], "nt", F32, "mm_in_gather", exch=gather_exchange(l, 1))
        else:
            proj = _mm(h, W["inT"], "nt", F32, "mm_in")
        W.update(
            ssd=full("w_ssd_out", l).reshape(dinner, D), po=full("w_pool_out", l).reshape(D, D),
            out=full("w_out", l).reshape(D, D), guT=full("w_gate_up", l).reshape(2 * F, D),
            down=full("w_down", l).reshape(F, D),
            pool=full("pool_w", l).reshape(N_DEV, len(POOL_WINDOWS), pg // N_DEV, pg).transpose(1, 0, 2, 3)
            .reshape(len(POOL_WINDOWS), pg, pg),
        )
        cw = full("conv_w", l).reshape(N_DEV, 16, xbc_w // N_DEV).astype(F32)
        W["conv8"] = (cw[:, :8] + cw[:, 8:]).transpose(1, 0, 2).reshape(8, xbc_w)
        l1, l2, l3, dtb, ein, dte, etot = _dt_prep(proj, dt_blk, pad128(dt_bias[l]), pad128(a_log[l]), expand, H,
                                                   "dt_prep")
        L1, L2, L3 = (v[:, :2 * H].T.reshape(2, H, 1, T) for v in (l1, l2, l3))
        k16 = jnp.arange(16).reshape(1, 1, 16, 1)

        def rows16(at):
            terms = jnp.where(k16 == at, L1, jnp.where(k16 == at + 1, L2, L3))
            return jnp.where((k16 >= at) & (k16 < at + 3), terms, (k16 < 6).astype(BF16))

        lam_a, lam_b = rows16(0), -rows16(3) + 2 * (k16 < 3).astype(BF16)
        et = etot.reshape(T // CHUNK, 8, 128)[:, 0, :2 * H].reshape(T // CHUNK, 2, G, HPG).transpose(1, 2, 0, 3)
        etot5 = jnp.pad(jnp.broadcast_to(et[..., None], et.shape + (128,)),
                        ((0, 0), (0, 0), (0, 0), (0, 8 - HPG), (0, 0)))
        dsk = jnp.pad(jnp.repeat(d_skip[l], HEADDIM, axis=1)[:, None, :], ((0, 0), (0, 7), (0, 0)))
        scan_ops = (dtb, ein, dte, lam_a, lam_b, etot5)
        xbc = _conv_fwd(proj, dinner // conv_tc, xbc_w, W["conv8"], vec(conv_b[l]), n_ctx, conv_tc, "conv_fwd")
        if l + 1 < depth:
            y2, states, gathered[l + 1][0] = _ssd_fwd(xbc, *scan_ops, dsk, G, n_ctx, ssd_gpb, "ssd_fwd_gather",
                                                      gather_exchange(l + 1, 0))
        else:
            y2, states = _ssd_fwd(xbc, *scan_ops, dsk, G, n_ctx, ssd_gpb, "ssd_fwd")
        yn = _gnorm_fwd(y2, proj, vec(ssd_norm_w[l]), gs, tr_mid, "gnorm_fwd")
        pm = _pool_apply(proj, (6 * D) // pg, BF16, n_ctx, pg, False, "pool_fwd")
        pms = _pool_mix_fwd(pm, W["pool"], vec(pool_scale[l]), tr_big, "pool_mix_fwd")
        o_ssd = _mm(yn, W["ssd"], "nn", F32, "mm_ssd_out")
        o_pool = _mm(pms, W["po"], "nn", F32, "mm_pool_out")
        mg = _merge_fwd(proj, 7, o_ssd, o_pool, tr_mid, "merge_fwd")
        mo = _mm(mg, W["out"], "nn", F32, "mm_out")
        x1, h2 = _norm_mod(xcur, vec(g_ffn[l]), m6, 3, 4, n_ctx_tiles, tr, "resid_norm_mod", resid=(mo, 2))
        gu = _mm(h2, W["guT"], "nt", F32, "mm_gate_up")
        act = _swiglu_fwd(gu, tr_mid, "swiglu_fwd")
        f = _mm(act, W["down"], "nn", F32, "mm_down")
        saved.append(dict(W=W, m6=m6, x0=xcur, h=h, proj=proj, scan_ops=scan_ops, xbc=xbc, y2=y2,
                          states=states, yn=yn, pm=pm, pms=pms, o_ssd=o_ssd, o_pool=o_pool, mg=mg, mo=mo, x1=x1,
                          h2=h2, gu=gu, act=act, f=f))
        xcur = _resid(x1, f, m6, 5, n_ctx_tiles, tr, "resid")

    loss_blk, dx, dgf = _loss_head(xcur, tgt, vec(g_final), n_ctx_tiles, tr, "loss_head")
    loss = lax.psum(loss_blk[0, 0], MESH_AXES)

    big_rows = {}
    small_g = {n: [None] * depth for n in small}
    d_c_ctx = jnp.zeros((D,), F32)
    late = ["w_in", "conv_w", "w_ada"]
    rs_groups = [[[n for n in big if n not in late], late[:2], late[2:]] if l == 0 else [big] for l in range(depth)]
    rlays = [[_Layout(D, [(n, l, shard[(n, l)].shape[0]) for n in grp]) for grp in rs_groups[l]]
             for l in range(depth)]
    my_c = lax.axis_index("c").astype(jnp.int32).reshape(1)

    def pack_group(l, gi):
        lay = rlays[l][gi]
        gparts = [jnp.pad(big_rows[(n, l)].astype(BF16), ((0, 0), (0, pr - rows), (0, 0)))
                  for n, _, rows, pr, _ in lay.pieces]
        gparts.append(jnp.zeros((N_DEV, lay.rows - sum(p[3] for p in lay.pieces), D), BF16))
        gbuf = jnp.concatenate(gparts, axis=1)
        return gbuf.reshape(2, 2, 2, lay.rows, D).transpose(2, 0, 1, 3, 4).reshape(2, 4, lay.rows, D)

    def reduce_pair(l, gi):
        gbuf = pack_group(l, gi)
        return _pair_add(gbuf, _pair_exchange(gbuf, "rs_pair_exchange"), my_c, "rs_pair_add")

    slots = {}
    pending = None
    pair_waiting = None
    for l in reversed(range(depth)):
        S = saved[l]
        W, m6, proj = S["W"], S["m6"], S["proj"]
        df, dga2 = _resid_bwd(dx, S["f"], m6, 5, n_ctx_tiles, tr, "resid_bwd")
        if pair_waiting is None:
            dact = _mm(df, W["down"], "nt", F32, "mm_down_dx")
        else:
            key, gbuf = pair_waiting
            dact, got = _mm(df, W["down"], "nt", F32, "mm_down_dx_exchange", exch=_Exchange(
                gbuf, jax.ShapeDtypeStruct(gbuf.shape[1:], BF16), _PAIR_SEMS, _pair_start, _pair_finish))
            pending, pair_waiting = (key, _pair_add(gbuf, got, my_c, "rs_pair_add")), None
        g_down = _mm(S["act"], df, "tn", BF16,"mm_down_dw")
        dgu = _swiglu_bwd(S["gu"], dact, tr_mid, "swiglu_bwd")
        dh2 = _mm(dgu, W["guT"], "nn", F32, "mm_gate_up_dx")
        g_guT = _mm(dgu, S["h2"], "tn", BF16,"mm_gate_up_dw")
        dx1, st2 = _norm_mod_bwd(S["x1"], dh2, dx, vec(g_ffn[l]), m6, 4, n_ctx_tiles, tr, "norm_mod_bwd")
        dmo, dga1 = _resid_bwd(dx1, S["mo"], m6, 2, n_ctx_tiles, tr, "resid_bwd")
        dmg = _mm(dmo, W["out"], "nt", F32, "mm_out_dx")
        g_out = _mm(S["mg"], dmo, "tn", BF16,"mm_out_dw")
        do_ssd, do_pool, dgl = _merge_bwd(proj, 7, S["o_ssd"], S["o_pool"], dmg, tr_mid, "merge_bwd")
        dyn = _mm(do_ssd, W["ssd"], "nt", F32, "mm_ssd_out_dx")
        g_ssd = _mm(S["yn"], do_ssd, "tn", BF16,"mm_ssd_out_dw")
        dpms = _mm(do_pool, W["po"], "nt", F32, "mm_pool_out_dx")
        g_po = _mm(S["pms"], do_pool, "tn", BF16,"mm_pool_out_dw")
        dpm, g_pool, dps = _pool_mix_bwd(S["pm"], dpms, W["pool"], vec(pool_scale[l]), tr_big, "pool_mix_bwd")
        dup = _pool_apply(dpm, 0, BF16, n_ctx, pg, True, "pool_bwd")
        dy, dz, dnw = _gnorm_bwd(S["y2"], proj, vec(ssd_norm_w[l]), dyn, gs, tr, "gnorm_bwd")
        big_rows[("w_ssd_out", l)] = g_ssd.reshape(N_DEV, -1, D)
        big_rows[("pool_w", l)] = g_pool.reshape(len(POOL_WINDOWS), N_DEV, pg // N_DEV, pg).transpose(1, 0, 2, 3) \
            .reshape(N_DEV, -1, D)
        big_rows[("w_pool_out", l)] = g_po.reshape(N_DEV, -1, D)
        big_rows[("w_out", l)] = g_out.reshape(N_DEV, -1, D)
        big_rows[("w_gate_up", l)] = g_guT.reshape(N_DEV, -1, D)
        big_rows[("w_down", l)] = g_down.reshape(N_DEV, -1, D)
        riding = [] if pending is None else [pending]
        if len(rs_groups[l]) > 1:
            riding.append(((l, 0), reduce_pair(l, 0)))
        if not riding:
            dxs2, db2, dc2, ddt4, dlc4, dlr4 = _ssd_bwd(S["xbc"], dy, S["states"], *S["scan_ops"], G, n_ctx,
                                                        ssd_gpb_bwd, "ssd_bwd")
        else:
            red = jnp.concatenate([r for _, r in riding], axis=1)
            exch = _Exchange(red, jax.ShapeDtypeStruct(red.shape, BF16), _CHIP_SEMS, _chip_exchange_start,
                             _chip_exchange_finish)
            dxs2, db2, dc2, ddt4, dlc4, dlr4, got = _ssd_bwd(
                S["xbc"], dy, S["states"], *S["scan_ops"], G, n_ctx, ssd_gpb_bwd, "ssd_bwd_exchange", exch)
            at = 0
            for key, r in riding:
                slots[key] = got[:, at:at + r.shape[1]]
                at += r.shape[1]
        dskv = _pad_rows(jnp.repeat(d_skip[l], HEADDIM, axis=1), 8)
        cb = vec(conv_b[l])
        dxbc_x, dconv_x = _conv_bwd(proj, dinner // conv_tc, dxs2, 0, W["conv8"], cb, n_ctx, conv_tc, "conv_bwd_x",
                                    skip=(dy, dskv))
        dxbc_b, dconv_b = _conv_bwd(proj, 2 * dinner // conv_tc, db2, dinner // conv_tc, W["conv8"], cb, n_ctx,
                                    conv_tc, "conv_bwd_bc")
        dxbc_c, dconv_c = _conv_bwd(proj, (2 * dinner + GN) // conv_tc, dc2, (dinner + GN) // conv_tc, W["conv8"], cb,
                                    n_ctx, conv_tc, "conv_bwd_bc")
        dconv = jnp.concatenate([dconv_x, dconv_b, dconv_c], axis=1)
        ddt_raw, dtst = _dt_bwd(proj, dt_blk, pad128(dt_bias[l]), pad128(a_log[l]), from4(ddt4),
                                from4(dlc4 + dlr4.transpose(0, 1, 3, 2)), H, tr, "dt_bwd")
        dproj = jnp.concatenate([dz, dxbc_x, dxbc_b, dxbc_c, dup, dgl, ddt_raw,
                                 jnp.zeros((T, NP - 9 * D - 128), BF16)], axis=1)
        g_inT_new = _mm(dproj, S["h"], "tn", BF16,"mm_in_dw")
        g_inT = jnp.concatenate([g_inT_new[:6 * D], g_inT_new[9 * D:9 * D + 2 * H], g_inT_new[6 * D:9 * D]], axis=0)
        big_rows[("w_in", l)] = g_inT.reshape(N_DEV, -1, D)
        big_rows[("conv_w", l)] = jnp.pad(
            dconv[:CONV_K].reshape(CONV_K, N_DEV, xbc_w // N_DEV).transpose(1, 0, 2),
            ((0, 0), (0, 16 - CONV_K), (0, 0))).reshape(N_DEV, -1, D)
        if len(rs_groups[l]) == 3:
            red = reduce_pair(l, 1)
            dh, slots[(l, 1)] = _mm(dproj, W["inT"], "nn", F32, "mm_in_dx_exchange", exch=_Exchange(
                red, jax.ShapeDtypeStruct(red.shape, BF16), _CHIP_SEMS, _chip_exchange_start, _chip_exchange_finish))
        else:
            dh = _mm(dproj, W["inT"], "nn", F32, "mm_in_dx")
        dx0, st1 = _norm_mod_bwd(S["x0"], dh, dx1, vec(g_mix[l]), m6, 1, n_ctx_tiles, tr, "norm_mod_bwd")
        dm6 = _pad_rows(jnp.concatenate([st1[0:2], st1[2:4], dga1[0:2], st2[0:2], st2[2:4], dga2[0:2]], axis=1), 8)
        dsil = _mm(dm6, W["adaT"], "nn", F32, "mm_ada_dx")
        sil_b, dcc, dbada = _ada_bwd_small(cc8, dsil, dm6, "ada_bwd_small")
        g_adaT = _mm(dm6, sil_b, "tn", BF16,"mm_ada_dw")
        d_c_ctx = d_c_ctx + dcc[1]
        dx = dx0

        big_rows[("w_ada", l)] = g_adaT.reshape(N_DEV, -1, D)
        small_g["b_ada"][l] = dbada[0]
        small_g["g_mix"][l] = st1[4]
        small_g["conv_b"][l] = dconv[CONV_K]
        small_g["dt_bias"][l] = dtst[0, :2 * H].reshape(2, H)
        small_g["a_log"][l] = dtst[1, :2 * H].reshape(2, H)
        dsk_h = dconv_x[CONV_K + 1].reshape(H, HEADDIM).sum(axis=-1)
        small_g["d_skip"][l] = jnp.stack([dsk_h, dsk_h])
        small_g["ssd_norm_w"][l] = dnw[0]
        small_g["pool_scale"][l] = dps[0]
        small_g["g_ffn"][l] = st2[4]

        last_gi = len(rs_groups[l]) - 1
        if l > 0:
            pair_waiting = ((l, last_gi), pack_group(l, last_gi))
        else:
            pending = ((l, last_gi), reduce_pair(l, last_gi))
    slots[pending[0]] = _chip_exchange(pending[1], "rs_chip_exchange")
    g_local = {key: _sum_slots(s, "rs_chip_add") for key, s in slots.items()}
    grad_x = dx[n_ctx:][None]

    def local_grad(name):
        outs = []
        for l in range(depth):
            gi = [name in grp for grp in rs_groups[l]].index(True)
            _, _, rows, _, off = rlays[l][gi].find(name, l)
            piece = g_local[(l, gi)][off:off + rows]
            if name in ("w_ada", "w_in", "w_gate_up"):
                piece = piece.T
            elif name == "conv_w":
                piece = piece.reshape(16, -1)[:CONV_K]
            elif name == "pool_w":
                piece = piece.reshape(weights[name].shape[1:])
            outs.append(piece)
        return jnp.stack(outs)

    grads = {n: local_grad(n) for n in big}

    small_full = {"c_ctx": d_c_ctx, "g_final": dgf[0]}
    for n in small:
        if n not in small_full:
            small_full[n] = jnp.stack(small_g[n])

    def pack_small(tree):
        flat = jnp.concatenate([tree[n].reshape(-1).astype(F32) for n in small])
        rows = _round_up(-(-flat.shape[0] // D), 8)
        return jnp.pad(flat, (0, rows * D - flat.shape[0])).reshape(rows, D)

    def unpack_small(buf):
        flat, out, off = buf.reshape(-1), {}, 0
        for n in small:
            sz = weights[n].size
            out[n] = flat[off:off + sz].reshape(weights[n].shape)
            off += sz
        return out

    g_small = _sum_slots(_all_gather(pack_small(small_full), "gather_small_grads"), "sum_small_grads")
    grads.update(unpack_small(g_small))

    delta, new_m, new_v = {}, {}, {}
    for n in big:
        shp = weights[n].shape
        d_, m_, v_ = _adamw(weights[n].reshape(-1, shp[-1]), grads[n].reshape(-1, shp[-1]),
                            moms_m[n].reshape(-1, shp[-1]), moms_v[n].reshape(-1, shp[-1]), "adamw_" + n)
        delta[n], new_m[n], new_v[n] = d_.reshape(shp), m_.reshape(shp), v_.reshape(shp)
    d_, m_, v_ = _adamw(pack_small(weights), g_small, pack_small(moms_m), pack_small(moms_v), "adamw_small")
    delta.update(unpack_small(d_))
    new_m.update(unpack_small(m_))
    new_v.update(unpack_small(v_))

    return (loss, grad_x, *[grads[n] for n in order], *[delta[n] for n in order],
            *[new_m[n] for n in order], *[new_v[n] for n in order])
```

```python
import functools

import jax
import jax.numpy as jnp
from jax import lax
from jax.experimental import pallas as pl
from jax.experimental.pallas import tpu as pltpu

F32 = jnp.float32
BF16 = jnp.bfloat16
N_DEV = 8
EPS = 1e-6
GRID_W = 64
POOL_WINDOWS = (2, 4, 8, 16)
HEADDIM = 64
STATE = 128
CHUNK = 128
HPG = 4
CONV_K = 5
ADAM_LR, ADAM_B1, ADAM_B2, ADAM_EPS, ADAM_WD, ADAM_STEP = 0.001, 0.9, 0.999, 1e-08, 0.01, 10
NEG_BIG = -1e30
MESH_AXES = ("x", "y", "c")
ANY = pl.BlockSpec(memory_space=pl.ANY)


def _tile(n, cands):
    for t in cands:
        if n % t == 0:
            return t
    return n


def _round_up(n, m):
    return -(-n // m) * m


def _silu(x):
    return x * jax.nn.sigmoid(x)


def _dsilu(x):
    s = jax.nn.sigmoid(x)
    return s * (1.0 + x * (1.0 - s))


def _cparams(sem):
    return pltpu.CompilerParams(dimension_semantics=sem, vmem_limit_bytes=56 * 1024 * 1024)


def _mm(a, b, mode, out_dtype, name, exch=None):
    if mode == "tn":
        K, M = a.shape
        N = b.shape[1]
        tm = _tile(M, (512, 256, 128))
        tn = _tile(N, (1024, 512, 256, 128))
        tk = K if K <= 4608 else _tile(K, (1088, 544, 512, 256, 128))
        a_spec = pl.BlockSpec((tk, tm), lambda i, j, k: (k, i))
        b_spec = pl.BlockSpec((tk, tn), lambda i, j, k: (k, j))
        dims = (((0,), (0,)), ((), ()))
    else:
        M, K = a.shape
        N = b.shape[0] if mode == "nt" else b.shape[1]
        tm = _tile(M, (1088, 544, 512, 256, 128))
        tk = K if K <= 4096 else max(t for t in range(128, 2817, 128) if K % t == 0)
        tn = _tile(N, (512, 256, 128)) if tk == K and N > 1024 else _tile(N, (1024, 512, 256, 128))
        a_spec = pl.BlockSpec((tm, tk), lambda i, j, k: (i, k))
        if mode == "nt":
            b_spec = pl.BlockSpec((tn, tk), lambda i, j, k: (j, k))
            dims = (((1,), (1,)), ((), ()))
        else:
            b_spec = pl.BlockSpec((tk, tn), lambda i, j, k: (k, j))
            dims = (((1,), (0,)), ((), ()))
    nk = K // tk

    def body(a_ref, b_ref, o_ref, *acc):
        if nk == 1:
            o_ref[...] = lax.dot_general(a_ref[...].astype(BF16), b_ref[...].astype(BF16), dims,
                                         preferred_element_type=F32).astype(o_ref.dtype)
            return
        k = pl.program_id(2)

        @pl.when(k == 0)
        def _():
            acc[0][...] = jnp.zeros_like(acc[0])

        acc[0][...] += lax.dot_general(a_ref[...].astype(BF16), b_ref[...].astype(BF16), dims,
                                       preferred_element_type=F32)

        @pl.when(k == nk - 1)
        def _():
            o_ref[...] = acc[0][...].astype(o_ref.dtype)

    call = dict(name=name, grid=(M // tm, N // tn, nk), in_specs=[a_spec, b_spec],
                scratch_shapes=[pltpu.VMEM((tm, tn), F32)] if nk > 1 else [])
    o_spec, o_sds = pl.BlockSpec((tm, tn), lambda i, j, k: (i, j)), jax.ShapeDtypeStruct((M, N), out_dtype)
    if exch is not None:
        return _call_with_exchange(body, exch, out_specs=[o_spec], out_shape=[o_sds], operands=(a, b), **call)
    return pl.pallas_call(body, out_specs=o_spec, out_shape=o_sds,
                          compiler_params=_cparams(("parallel", "parallel", "arbitrary")), **call)(a, b)


def _ada_fwd(cc8, w_adaT, b_ada, name):
    D = cc8.shape[1]
    N = w_adaT.shape[0]
    tn = _tile(N, (512, 256, 128))

    def body(c_ref, w_ref, b_ref, o_ref):
        a = _silu(c_ref[...]).astype(BF16)
        o_ref[...] = lax.dot_general(a, w_ref[...], (((1,), (1,)), ((), ())),
                                     preferred_element_type=F32) + b_ref[...]

    return pl.pallas_call(
        body, name=name, grid=(N // tn,),
        in_specs=[pl.BlockSpec((8, D), lambda j: (0, 0)), pl.BlockSpec((tn, D), lambda j: (j, 0)),
                  pl.BlockSpec((1, tn), lambda j: (0, j))],
        out_specs=pl.BlockSpec((8, tn), lambda j: (0, j)),
        out_shape=jax.ShapeDtypeStruct((8, N), F32),
        compiler_params=_cparams(("parallel",)),
    )(cc8, w_adaT, b_ada)


def _ada_bwd_small(cc8, dsil, dm6, name):
    D = cc8.shape[1]
    N = dm6.shape[1]

    def body(c_ref, ds_ref, dm_ref, sil_ref, dc_ref, db_ref):
        c = c_ref[...]
        sil_ref[...] = _silu(c).astype(BF16)
        dc_ref[...] = ds_ref[...] * _dsilu(c)
        dm = dm_ref[...]
        row = lax.broadcasted_iota(jnp.int32, dm.shape, 0)
        db_ref[...] = jnp.where(row == 0, jnp.sum(dm, axis=0, keepdims=True), 0.0)

    return pl.pallas_call(
        body, name=name, grid=(1,),
        in_specs=[pl.BlockSpec((8, D), lambda i: (0, 0)), pl.BlockSpec((8, D), lambda i: (0, 0)),
                  pl.BlockSpec((8, N), lambda i: (0, 0))],
        out_specs=[pl.BlockSpec((8, D), lambda i: (0, 0)), pl.BlockSpec((8, D), lambda i: (0, 0)),
                   pl.BlockSpec((8, N), lambda i: (0, 0))],
        out_shape=[jax.ShapeDtypeStruct((8, D), BF16), jax.ShapeDtypeStruct((8, D), F32),
                   jax.ShapeDtypeStruct((8, N), F32)],
        compiler_params=_cparams(("arbitrary",)),
    )(cc8, dsil, dm6)


def _seg_pick(m_ref, is_ctx):
    return jnp.where(is_ctx, m_ref[1:2, :], m_ref[0:1, :])


def _norm_mod(x, g, m6, sh_idx, sc_idx, n_ctx_tiles, tr, name, resid=None):
    T, D = x.shape
    row = pl.BlockSpec((tr, D), lambda i: (i, 0))
    vec = pl.BlockSpec((1, D), lambda i: (0, 0))

    def mcol(idx):
        return pl.BlockSpec((8, D), lambda i: (0, idx))

    def body(*refs):
        if resid is None:
            x_ref, g_ref, sh_ref, sc_ref, h_ref = refs
            xv = x_ref[...]
        else:
            x_ref, f_ref, ga_ref, g_ref, sh_ref, sc_ref, xo_ref, h_ref = refs
        is_ctx = pl.program_id(0) < n_ctx_tiles
        if resid is not None:
            xv = x_ref[...] + _seg_pick(ga_ref, is_ctx) * f_ref[...]
            xo_ref[...] = xv
        rstd = lax.rsqrt(jnp.mean(xv * xv, axis=-1, keepdims=True) + EPS)
        hn = xv * rstd * g_ref[...]
        h_ref[...] = (hn * (1.0 + _seg_pick(sc_ref, is_ctx)) + _seg_pick(sh_ref, is_ctx)).astype(BF16)

    if resid is None:
        ins, in_specs = [x, g, m6, m6], [row, vec, mcol(sh_idx), mcol(sc_idx)]
        out_specs, out_shape = row, jax.ShapeDtypeStruct((T, D), BF16)
    else:
        f, ga_idx = resid
        ins = [x, f, m6, g, m6, m6]
        in_specs = [row, row, mcol(ga_idx), vec, mcol(sh_idx), mcol(sc_idx)]
        out_specs = [row, row]
        out_shape = [jax.ShapeDtypeStruct((T, D), F32), jax.ShapeDtypeStruct((T, D), BF16)]
    return pl.pallas_call(body, name=name, grid=(T // tr,), in_specs=in_specs, out_specs=out_specs,
                          out_shape=out_shape, compiler_params=_cparams(("parallel",)))(*ins)


def _resid(x, f, m6, ga_idx, n_ctx_tiles, tr, name):
    T, D = x.shape
    row = pl.BlockSpec((tr, D), lambda i: (i, 0))

    def body(x_ref, f_ref, ga_ref, o_ref):
        is_ctx = pl.program_id(0) < n_ctx_tiles
        o_ref[...] = x_ref[...] + _seg_pick(ga_ref, is_ctx) * f_ref[...]

    return pl.pallas_call(body, name=name, grid=(T // tr,),
                          in_specs=[row, row, pl.BlockSpec((8, D), lambda i: (0, ga_idx))], out_specs=row,
                          out_shape=jax.ShapeDtypeStruct((T, D), F32),
                          compiler_params=_cparams(("parallel",)))(x, f, m6)


def _resid_bwd(dx, f, m6, ga_idx, n_ctx_tiles, tr, name):
    T, D = dx.shape
    row = pl.BlockSpec((tr, D), lambda i: (i, 0))
    acc = pl.BlockSpec((8, D), lambda i: (0, 0))

    def body(dx_ref, f_ref, ga_ref, df_ref, dga_ref):
        i = pl.program_id(0)
        is_ctx = i < n_ctx_tiles

        @pl.when(i == 0)
        def _():
            dga_ref[...] = jnp.zeros_like(dga_ref)

        dxv = dx_ref[...]
        df_ref[...] = (_seg_pick(ga_ref, is_ctx) * dxv).astype(BF16)
        s = jnp.sum(dxv * f_ref[...], axis=0, keepdims=True)
        r = lax.broadcasted_iota(jnp.int32, (8, D), 0)
        dga_ref[...] += jnp.where(r == jnp.where(is_ctx, 1, 0), s, 0.0)

    return pl.pallas_call(body, name=name, grid=(T // tr,),
                          in_specs=[row, row, pl.BlockSpec((8, D), lambda i: (0, ga_idx))],
                          out_specs=[row, acc],
                          out_shape=[jax.ShapeDtypeStruct((T, D), BF16), jax.ShapeDtypeStruct((8, D), F32)],
                          compiler_params=_cparams(("arbitrary",)))(dx, f, m6)


def _norm_mod_bwd(x, dh, dxres, g, m6, sc_idx, n_ctx_tiles, tr, name):
    T, D = x.shape
    row = pl.BlockSpec((tr, D), lambda i: (i, 0))
    acc = pl.BlockSpec((8, D), lambda i: (0, 0))

    def body(x_ref, dh_ref, dr_ref, g_ref, sc_ref, dx_ref, st_ref):
        i = pl.program_id(0)
        is_ctx = i < n_ctx_tiles

        @pl.when(i == 0)
        def _():
            st_ref[...] = jnp.zeros_like(st_ref)

        xv, dh_v, gv = x_ref[...], dh_ref[...], g_ref[...]
        sc1 = 1.0 + _seg_pick(sc_ref, is_ctx)
        rstd = lax.rsqrt(jnp.mean(xv * xv, axis=-1, keepdims=True) + EPS)
        xhat = xv * rstd
        dxhat = dh_v * sc1 * gv
        dx_ref[...] = dr_ref[...] + rstd * (dxhat - xhat * jnp.mean(dxhat * xhat, axis=-1, keepdims=True))
        dsh = jnp.sum(dh_v, axis=0, keepdims=True)
        dsc = jnp.sum(dh_v * xhat * gv, axis=0, keepdims=True)
        dg = jnp.sum(dh_v * sc1 * xhat, axis=0, keepdims=True)
        r = lax.broadcasted_iota(jnp.int32, (8, D), 0)
        seg = jnp.where(is_ctx, 1, 0)
        st_ref[...] += (jnp.where(r == seg, dsh, 0.0) + jnp.where(r == 2 + seg, dsc, 0.0)
                        + jnp.where(r == 4, dg, 0.0))

    return pl.pallas_call(body, name=name, grid=(T // tr,),
                          in_specs=[row, row, row, pl.BlockSpec((1, D), lambda i: (0, 0)),
                                    pl.BlockSpec((8, D), lambda i: (0, sc_idx))],
                          out_specs=[row, acc],
                          out_shape=[jax.ShapeDtypeStruct((T, D), F32), jax.ShapeDtypeStruct((8, D), F32)],
                          compiler_params=_cparams(("arbitrary",)))(x, dh, dxres, g, m6)


def _loss_head(x, tgt, g, n_ctx_tiles, tr, name):
    T, D = x.shape
    row = pl.BlockSpec((tr, D), lambda i: (i, 0))

    def body(x_ref, t_ref, g_ref, l_ref, dx_ref, dg_ref):
        i = pl.program_id(0)

        @pl.when(i == 0)
        def _():
            l_ref[...] = jnp.zeros_like(l_ref)
            dg_ref[...] = jnp.zeros_like(dg_ref)

        @pl.when(i < n_ctx_tiles)
        def _():
            dx_ref[...] = jnp.zeros_like(dx_ref)

        @pl.when(i >= n_ctx_tiles)
        def _():
            xv, gv = x_ref[...], g_ref[...]
            rstd = lax.rsqrt(jnp.mean(xv * xv, axis=-1, keepdims=True) + EPS)
            xhat = xv * rstd
            e = xhat * gv - t_ref[...]
            l_ref[...] += 0.5 * jnp.sum(jnp.mean(e * e, axis=-1, keepdims=True), axis=0, keepdims=True)
            dy = e * (1.0 / D)
            dxhat = dy * gv
            dx_ref[...] = rstd * (dxhat - xhat * jnp.mean(dxhat * xhat, axis=-1, keepdims=True))
            r = lax.broadcasted_iota(jnp.int32, (8, D), 0)
            dg_ref[...] += jnp.where(r == 0, jnp.sum(dy * xhat, axis=0, keepdims=True), 0.0)

    return pl.pallas_call(
        body, name=name, grid=(T // tr,),
        in_specs=[row, pl.BlockSpec((tr, D), lambda i: (jnp.maximum(i - n_ctx_tiles, 0), 0)),
                  pl.BlockSpec((1, D), lambda i: (0, 0))],
        out_specs=[pl.BlockSpec((8, 128), lambda i: (0, 0)), row, pl.BlockSpec((8, D), lambda i: (0, 0))],
        out_shape=[jax.ShapeDtypeStruct((8, 128), F32), jax.ShapeDtypeStruct((T, D), F32),
                   jax.ShapeDtypeStruct((8, D), F32)],
        compiler_params=_cparams(("arbitrary",)))(x, tgt, g)


def _seq_masks(T, n_ctx, width):
    row = lax.broadcasted_iota(jnp.int32, (T, width), 0)
    in_ctx = row < n_ctx
    return jnp.where(in_ctx, row, row - n_ctx), jnp.where(in_ctx, n_ctx, T - n_ctx)


def _shift_rows(u, off, t_loc, seg_len):
    T = u.shape[0]
    if off == 0:
        return u
    v = pltpu.roll(u, (-off) % T, 0)
    ok = (t_loc + off >= 0) & (t_loc + off < seg_len)
    return jnp.where(ok, v, 0.0)


def _conv_fwd(proj, col0_blk, ncol, conv_w8, conv_b, n_ctx, tc, name):
    T = proj.shape[0]

    def body(u_ref, w_ref, b_ref, o_ref):
        u = u_ref[...]
        t_loc, seg_len = _seq_masks(T, n_ctx, tc)
        acc = jnp.broadcast_to(b_ref[...], u.shape)
        for i in range(CONV_K):
            acc = acc + w_ref[i:i + 1, :] * _shift_rows(u, i - CONV_K // 2, t_loc, seg_len)
        o_ref[...] = _silu(acc)

    return pl.pallas_call(
        body, name=name, grid=(ncol // tc,),
        in_specs=[pl.BlockSpec((T, tc), lambda j: (0, col0_blk + j)), pl.BlockSpec((8, tc), lambda j: (0, j)),
                  pl.BlockSpec((1, tc), lambda j: (0, j))],
        out_specs=pl.BlockSpec((T, tc), lambda j: (0, j)),
        out_shape=jax.ShapeDtypeStruct((T, ncol), F32),
        compiler_params=_cparams(("parallel",)))(proj, conv_w8, conv_b)


def _conv_bwd(proj, col0_blk, d2, w_blk0, conv_w8, conv_b, n_ctx, tc, name, skip=None):
    T, ncol = d2.shape[1], d2.shape[2]

    def body(u_ref, d_ref, w_ref, b_ref, *rest):
        if skip is None:
            du_ref, dw_ref = rest
        else:
            dy_ref, k_ref, du_ref, dw_ref = rest
        u = u_ref[...]
        t_loc, seg_len = _seq_masks(T, n_ctx, tc)
        pre = jnp.broadcast_to(b_ref[...], u.shape)
        for i in range(CONV_K):
            pre = pre + w_ref[i:i + 1, :] * _shift_rows(u, i - CONV_K // 2, t_loc, seg_len)
        r = lax.broadcasted_iota(jnp.int32, (8, tc), 0)
        dact = d_ref[0] + d_ref[1]
        dw = jnp.zeros((8, tc), F32)
        if skip is not None:
            dyv = dy_ref[...]
            dact = dact + (k_ref[0:1, :] + k_ref[1:2, :]) * dyv
            dw = jnp.where(r == CONV_K + 1, jnp.sum(dyv * _silu(pre), axis=0, keepdims=True), 0.0)
        dpre = dact * _dsilu(pre)
        du = jnp.zeros_like(u)
        dw = dw + jnp.where(r == CONV_K, jnp.sum(dpre, axis=0, keepdims=True), 0.0)
        for i in range(CONV_K):
            off = i - CONV_K // 2
            du = du + w_ref[i:i + 1, :] * _shift_rows(dpre, -off, t_loc, seg_len)
            dw = dw + jnp.where(r == i, jnp.sum(dpre * _shift_rows(u, off, t_loc, seg_len), axis=0, keepdims=True),
                                0.0)
        du_ref[...] = du.astype(BF16)
        dw_ref[...] = dw

    col = pl.BlockSpec((T, tc), lambda j: (0, j))
    in_specs = [pl.BlockSpec((T, tc), lambda j: (0, col0_blk + j)), pl.BlockSpec((2, T, tc), lambda j: (0, 0, j)),
                pl.BlockSpec((8, tc), lambda j: (0, w_blk0 + j)), pl.BlockSpec((1, tc), lambda j: (0, w_blk0 + j))]
    operands = [proj, d2, conv_w8, conv_b]
    if skip is not None:
        in_specs += [col, pl.BlockSpec((8, tc), lambda j: (0, j))]
        operands += list(skip)
    return pl.pallas_call(
        body, name=name, grid=(ncol // tc,), in_specs=in_specs,
        out_specs=[col, pl.BlockSpec((8, tc), lambda j: (0, j))],
        out_shape=[jax.ShapeDtypeStruct((T, ncol), BF16), jax.ShapeDtypeStruct((8, ncol), F32)],
        compiler_params=_cparams(("parallel",)))(*operands)


def _pool_core(u, half, t_loc, seg_len, transpose):
    tr = u.shape[0]

    def shift(v, s):
        w = pltpu.roll(v, s % tr, 0)
        ok = (t_loc - s >= 0) & (t_loc - s < seg_len)
        return jnp.where(ok, w, 0.0)

    cnt = (jnp.minimum(t_loc, half) + jnp.minimum(seg_len - t_loc, half)).astype(F32)
    q = u / cnt if transpose else u
    back, ahead, h = q, q, 1
    while h < half:
        back = back + shift(back, h)
        ahead = ahead + shift(ahead, -h)
        h *= 2
    if transpose:
        tot = back + shift(ahead, -1)
        return tot - u
    tot = shift(back, 1) + ahead
    return tot / cnt - u


def _pool_apply(src, col0_blk, out_dtype, n_ctx, pg, transpose, name):
    T = src.shape[0]

    def body(u_ref, o_ref):
        gi = pl.program_id(0)
        row = lax.broadcasted_iota(jnp.int32, (T, pg), 0)
        seg_len = jnp.where(row < n_ctx, n_ctx, GRID_W)
        t_loc = row & (seg_len - 1)
        u = u_ref[...].astype(F32)
        for k_idx, k in enumerate(POOL_WINDOWS):
            @pl.when(gi == k_idx)
            def _(k=k):
                o_ref[...] = _pool_core(u, k // 2, t_loc, seg_len, transpose).astype(o_ref.dtype)

    return pl.pallas_call(
        body, name=name, grid=(len(POOL_WINDOWS),),
        in_specs=[pl.BlockSpec((T, pg), lambda gi: (0, col0_blk + gi))],
        out_specs=pl.BlockSpec((T, pg), lambda gi: (0, gi)),
        out_shape=jax.ShapeDtypeStruct((T, pg * len(POOL_WINDOWS)), out_dtype),
        compiler_params=_cparams(("parallel",)))(src)


def _pool_mix_fwd(pm, pool_w, pool_scale, tr, name):
    T, W = pm.shape
    ng, pg = pool_w.shape[0], pool_w.shape[1]

    def body(p_ref, w_ref, s_ref, o_ref):
        o_ref[...] = (jnp.dot(p_ref[...], w_ref[...], preferred_element_type=F32) * s_ref[...]).astype(BF16)

    return pl.pallas_call(
        body, name=name, grid=(T // tr, ng),
        in_specs=[pl.BlockSpec((tr, pg), lambda i, g: (i, g)), pl.BlockSpec((None, pg, pg), lambda i, g: (g, 0, 0)),
                  pl.BlockSpec((1, pg), lambda i, g: (0, g))],
        out_specs=pl.BlockSpec((tr, pg), lambda i, g: (i, g)),
        out_shape=jax.ShapeDtypeStruct((T, W), BF16),
        compiler_params=_cparams(("parallel", "parallel")))(pm, pool_w, pool_scale)


def _pool_mix_bwd(pm, dpms, pool_w, pool_scale, tr, name):
    T, W = pm.shape
    ng, pg = pool_w.shape[0], pool_w.shape[1]

    def body(p_ref, d_ref, w_ref, s_ref, dp_ref, dw_ref, ds_ref):
        i = pl.program_id(1)

        @pl.when(i == 0)
        def _():
            dw_ref[...] = jnp.zeros_like(dw_ref)
            ds_ref[...] = jnp.zeros_like(ds_ref)

        p, w = p_ref[...], w_ref[...]
        d = d_ref[...].astype(F32)
        pmix = jnp.dot(p, w, preferred_element_type=F32)
        r = lax.broadcasted_iota(jnp.int32, (8, pg), 0)
        ds_ref[...] += jnp.where(r == 0, jnp.sum(d * pmix, axis=0, keepdims=True), 0.0)
        dmix = (d * s_ref[...]).astype(BF16)
        dp_ref[...] = lax.dot_general(dmix, w, (((1,), (1,)), ((), ())), preferred_element_type=F32)
        dw_ref[...] += lax.dot_general(p, dmix, (((0,), (0,)), ((), ())), preferred_element_type=F32)

    return pl.pallas_call(
        body, name=name, grid=(ng, T // tr),
        in_specs=[pl.BlockSpec((tr, pg), lambda g, i: (i, g)), pl.BlockSpec((tr, pg), lambda g, i: (i, g)),
                  pl.BlockSpec((None, pg, pg), lambda g, i: (g, 0, 0)), pl.BlockSpec((1, pg), lambda g, i: (0, g))],
        out_specs=[pl.BlockSpec((tr, pg), lambda g, i: (i, g)), pl.BlockSpec((None, pg, pg), lambda g, i: (g, 0, 0)),
                   pl.BlockSpec((8, pg), lambda g, i: (0, g))],
        out_shape=[jax.ShapeDtypeStruct((T, W), F32), jax.ShapeDtypeStruct((ng, pg, pg), F32),
                   jax.ShapeDtypeStruct((8, W), F32)],
        compiler_params=_cparams(("parallel", "arbitrary")))(pm, dpms, pool_w, pool_scale)


def _chunk_cumsum(v, upper):
    Q = v.shape[0]
    ii = lax.broadcasted_iota(jnp.int32, (Q, Q), 0)
    jj = lax.broadcasted_iota(jnp.int32, (Q, Q), 1)
    tri = ((jj >= ii) if upper else (jj <= ii)).astype(BF16)
    h1 = v.astype(BF16)
    r1 = v - h1.astype(F32)
    h2 = r1.astype(BF16)
    h3 = (r1 - h2.astype(F32)).astype(BF16)
    return (jnp.dot(tri, h1, preferred_element_type=F32) + jnp.dot(tri, h2, preferred_element_type=F32)
            + jnp.dot(tri, h3, preferred_element_type=F32))


def _split3(v):
    h1 = v.astype(BF16)
    r1 = v - h1.astype(F32)
    h2 = r1.astype(BF16)
    return h1, h2, (r1 - h2.astype(F32)).astype(BF16)


def _dt_prep(proj, dt_blk, bias, a_log, expand, n_heads, name):
    T = proj.shape[0]
    Wd = expand.shape[1]
    Q = CHUNK
    cps = 2 if (T // Q) % 2 == 0 else 1
    row = pl.BlockSpec((cps * Q, 128), lambda i: (i, 0))
    wide = pl.BlockSpec((cps * Q, Wd), lambda i: (i, 0))

    def body(r_ref, b_ref, al_ref, e_ref, l1_ref, l2_ref, l3_ref, dtb_ref, ein_ref, dte_ref, etot_ref):
        xv = r_ref[...] + b_ref[...]
        dt = jnp.maximum(xv, 0.0) + jnp.log(1.0 + jnp.exp(-jnp.abs(xv)))
        a_all = -jnp.exp(al_ref[...]) * dt
        fwd_col = lax.broadcasted_iota(jnp.int32, (Q, 128), 1) < n_heads
        lams, rests = [], []
        for c in range(cps):
            a = a_all[c * Q:(c + 1) * Q]
            lam_c = jnp.where(fwd_col, _chunk_cumsum(a, False), _chunk_cumsum(a, True))
            tot_c = jnp.where(fwd_col[0:1], lam_c[Q - 1:Q], lam_c[0:1])
            etot_ref[8 * c:8 * (c + 1), :] = jnp.broadcast_to(jnp.exp(tot_c), (8, 128))
            lams.append(lam_c)
            rests.append(tot_c - lam_c)
        lam = lams[0] if cps == 1 else jnp.concatenate(lams, axis=0)
        rest = rests[0] if cps == 1 else jnp.concatenate(rests, axis=0)
        l1_ref[...], l2_ref[...], l3_ref[...] = _split3(lam)
        ex = e_ref[...]

        def rep(v):
            p1, p2, p3 = _split3(v)
            return (jnp.dot(p1, ex, preferred_element_type=F32) + jnp.dot(p2, ex, preferred_element_type=F32)
                    + jnp.dot(p3, ex, preferred_element_type=F32))

        dtb_ref[...] = rep(dt)
        ein_ref[...] = rep(jnp.exp(lam))
        dte_ref[...] = rep(jnp.exp(rest))

    vec = pl.BlockSpec((1, 128), lambda i: (0, 0))
    return pl.pallas_call(
        body, name=name, grid=(T // (cps * Q),),
        in_specs=[pl.BlockSpec((cps * Q, 128), lambda i: (i, dt_blk)), vec, vec,
                  pl.BlockSpec((128, Wd), lambda i: (0, 0))],
        out_specs=[row, row, row, wide, wide, wide, pl.BlockSpec((8 * cps, 128), lambda i: (i, 0))],
        out_shape=[jax.ShapeDtypeStruct((T, 128), BF16)] * 3 + [jax.ShapeDtypeStruct((T, Wd), F32)] * 3
        + [jax.ShapeDtypeStruct((T // Q * 8, 128), F32)],
        compiler_params=_cparams(("parallel",)))(proj, bias, a_log, expand)


def _dt_bwd(proj, dt_blk, bias, a_log, ddt, dlam, n_heads, tr, name):
    T = proj.shape[0]
    row = pl.BlockSpec((tr, 128), lambda i: (i, 0))
    vec = pl.BlockSpec((1, 128), lambda i: (0, 0))

    def body(r_ref, b_ref, al_ref, ddt_ref, dl_ref, o_ref, st_ref):
        @pl.when(pl.program_id(0) == 0)
        def _():
            st_ref[...] = jnp.zeros_like(st_ref)

        xv = r_ref[...] + b_ref[...]
        dt = jnp.maximum(xv, 0.0) + jnp.log(1.0 + jnp.exp(-jnp.abs(xv)))
        a_neg = -jnp.exp(al_ref[...])
        col = lax.broadcasted_iota(jnp.int32, (CHUNK, 128), 1)
        dl = dl_ref[...]
        parts = []
        for k in range(tr // CHUNK):
            dk = dl[k * CHUNK:(k + 1) * CHUNK]
            parts.append(jnp.where(col < n_heads, _chunk_cumsum(dk, True), _chunk_cumsum(dk, False)))
        dav = jnp.concatenate(parts, axis=0)
        draw = (ddt_ref[...] + dav * a_neg) * jax.nn.sigmoid(xv)
        o_ref[...] = draw.astype(BF16)
        r = lax.broadcasted_iota(jnp.int32, (8, 128), 0)
        st_ref[...] += (jnp.where(r == 0, jnp.sum(draw, axis=0, keepdims=True), 0.0)
                        + jnp.where(r == 1, jnp.sum(dav * dt, axis=0, keepdims=True) * a_neg, 0.0))

    return pl.pallas_call(
        body, name=name, grid=(T // tr,),
        in_specs=[pl.BlockSpec((tr, 128), lambda i: (i, dt_blk)), vec, vec, row, row],
        out_specs=[row, pl.BlockSpec((8, 128), lambda i: (0, 0))],
        out_shape=[jax.ShapeDtypeStruct((T, 128), BF16), jax.ShapeDtypeStruct((8, 128), F32)],
        compiler_params=_cparams(("arbitrary",)))(proj, bias, a_log, ddt, dlam)


def _scan_chunk(d, pos, nc_ctx, nc):
    rev = jnp.where(pos < nc_ctx, nc_ctx - 1 - pos, nc - 1 - (pos - nc_ctx))
    return jnp.where(d == 0, pos, rev)


def _chunk_mask(d):
    ii = lax.broadcasted_iota(jnp.int32, (CHUNK, CHUNK), 0)
    jj = lax.broadcasted_iota(jnp.int32, (CHUNK, CHUNK), 1)
    return (ii - jj) * jnp.where(d == 0, 1, -1) >= 0


def _ssd_specs(T, G, n_ctx, gpb, chunk_of):
    R, P, N, Q = HPG, HEADDIM, STATE, CHUNK
    H = G * R
    nc, nc_ctx = T // Q, n_ctx // Q
    xw, bw = gpb * R * P, gpb * N
    b_blk0 = (H * P) // bw
    c_blk0 = b_blk0 + G // gpb

    def ch(d, s):
        return chunk_of(d, s, nc_ctx, nc)

    return dict(
        x=pl.BlockSpec((Q, xw), lambda d, g, s: (ch(d, s), g)),
        b=pl.BlockSpec((Q, bw), lambda d, g, s: (ch(d, s), b_blk0 + g)),
        c=pl.BlockSpec((Q, bw), lambda d, g, s: (ch(d, s), c_blk0 + g)),
        col=pl.BlockSpec((None, gpb, Q, R), lambda d, g, s: (d, g, ch(d, s), 0)),
        row=pl.BlockSpec((None, gpb, R, Q), lambda d, g, s: (d, g, 0, ch(d, s))),
        rep=pl.BlockSpec((Q, xw), lambda d, g, s: (ch(d, s), d * (G // gpb) + g)),
        lam_a=pl.BlockSpec((None, gpb * R, 16, Q), lambda d, g, s: (d, g, 0, ch(d, s))),
        lam_b=pl.BlockSpec((None, gpb * R, 16, Q), lambda d, g, s: (d, g, 0, ch(d, s))),
        etot=pl.BlockSpec((None, gpb, None, 8, 128), lambda d, g, s: (d, g, ch(d, s), 0, 0)),
        dsk=pl.BlockSpec((None, 8, xw), lambda d, g, s: (d, 0, g)),
        xd=pl.BlockSpec((None, Q, xw), lambda d, g, s: (d, ch(d, s), g)),
        bd=pl.BlockSpec((None, Q, bw), lambda d, g, s: (d, ch(d, s), g)),
        st=pl.BlockSpec((None, None, gpb * R // 2, 2 * P, N), lambda d, g, s: (d, ch(d, s), g, 0, 0)),
    )


class _Exchange:
    def __init__(self, operand, out_sds, sems, start, finish):
        self.operand, self.out_sds, self.sems, self.start, self.finish = operand, out_sds, sems, start, finish


def _call_with_exchange(body, exch, *, name, grid, in_specs, out_specs, out_shape, scratch_shapes, operands):
    if exch is None:
        return pl.pallas_call(body, name=name, grid=grid, in_specs=in_specs, out_specs=out_specs,
                              out_shape=out_shape, scratch_shapes=scratch_shapes,
                              compiler_params=_cparams(("arbitrary",) * len(grid)))(*operands)
    n_in, n_out, n_scr = len(in_specs), len(out_specs), len(scratch_shapes)

    def fused(*refs):
        ins, c_in = refs[:n_in], refs[n_in]
        outs, c_out = refs[n_in + 1:n_in + 1 + n_out], refs[n_in + 1 + n_out]
        scr = refs[n_in + 2 + n_out:n_in + 2 + n_out + n_scr]
        sems = refs[n_in + 2 + n_out + n_scr:]
        ids = [pl.program_id(a) for a in range(len(grid))]
        first = functools.reduce(lambda p, q: p & q, [i == 0 for i in ids])
        last = functools.reduce(lambda p, q: p & q, [i == n - 1 for i, n in zip(ids, grid)])

        @pl.when(first)
        def _():
            exch.start(c_in, c_out, *sems)

        body(*ins, *outs, *scr)

        @pl.when(last)
        def _():
            exch.finish(c_in, c_out, *sems)

    return pl.pallas_call(fused, name=name, grid=grid, in_specs=list(in_specs) + [ANY],
                          out_specs=list(out_specs) + [ANY], out_shape=list(out_shape) + [exch.out_sds],
                          scratch_shapes=list(scratch_shapes) + list(exch.sems),
                          compiler_params=_cparams(("arbitrary",) * len(grid)))(*operands, exch.operand)


def _ssd_fwd(xbc, dtb, ein, dte, lam_a, lam_b, etot, dsk, G, n_ctx, gpb, name, exch=None):
    T = xbc.shape[0]
    R, P, N, Q = HPG, HEADDIM, STATE, CHUNK
    H = G * R
    nc = T // Q
    sp = _ssd_specs(T, G, n_ctx, gpb, _scan_chunk)

    def body(x_ref, b_ref, c_ref, dt_ref, ein_ref, dte_ref, la_ref, lb_ref, et_ref, dsk_ref, y_ref, st_ref, S):
        d, s = pl.program_id(0), pl.program_id(2)

        @pl.when(s == 0)
        def _():
            S[...] = jnp.zeros_like(S)

        mask = _chunk_mask(d)
        head0 = lax.broadcasted_iota(jnp.int32, (Q, 2 * P), 1) < P
        rows0 = lax.broadcasted_iota(jnp.int32, (2 * P, N), 0) < P
        for gg in range(gpb):
            Bm = b_ref[:, gg * N:(gg + 1) * N].astype(BF16)
            Cm = c_ref[:, gg * N:(gg + 1) * N].astype(BF16)
            Gm = lax.dot_general(Cm, Bm, (((1,), (1,)), ((), ())), preferred_element_type=F32)
            for k in range(R // 2):
                pk = gg * (R // 2) + k
                sl = slice(pk * 2 * P, (pk + 1) * 2 * P)
                xp = x_ref[:, sl]
                xc = xp * dt_ref[:, sl]
                s_in = S[pk]
                y = lax.dot_general(Cm, s_in.astype(BF16), (((1,), (1,)), ((), ())),
                                    preferred_element_type=F32) * ein_ref[:, sl] + dsk_ref[0:1, sl] * xp
                for j in range(2):
                    hr = 2 * pk + j
                    diff = lax.dot_general(la_ref[hr], lb_ref[hr], (((0,), (0,)), ((), ())), preferred_element_type=F32)
                    ldec = jnp.exp(jnp.where(mask, diff, NEG_BIG))
                    xc_j = (jnp.where(head0, xc, 0.0) if j == 0 else jnp.where(head0, 0.0, xc)).astype(BF16)
                    y = y + jnp.dot((Gm * ldec).astype(BF16), xc_j, preferred_element_type=F32)
                y_ref[:, sl] = y
                st_ref[pk] = s_in
                e_all = jnp.where(rows0, et_ref[gg, 2 * k:2 * k + 1, :], et_ref[gg, 2 * k + 1:2 * k + 2, :])
                xd = (xc * dte_ref[:, sl]).astype(BF16)
                S[pk] = e_all * s_in + lax.dot_general(xd, Bm, (((0,), (0,)), ((), ())),
                                                       preferred_element_type=F32)

    return _call_with_exchange(
        body, exch, name=name, grid=(2, G // gpb, nc),
        in_specs=[sp["x"], sp["b"], sp["c"], sp["rep"], sp["rep"], sp["rep"], sp["lam_a"], sp["lam_b"], sp["etot"],
                  sp["dsk"]],
        out_specs=[sp["xd"], sp["st"]],
        out_shape=[jax.ShapeDtypeStruct((2, T, H * P), F32), jax.ShapeDtypeStruct((2, nc, H // 2, 2 * P, N), F32)],
        scratch_shapes=[pltpu.VMEM((gpb * R // 2, 2 * P, N), F32)],
        operands=(xbc, xbc, xbc, dtb, ein, dte, lam_a, lam_b, etot, dsk))


def _ssd_bwd(xbc, dy, states, dtb, ein, dte, lam_a, lam_b, etot, G, n_ctx, gpb, name, exch=None):
    T = xbc.shape[0]
    R, P, N, Q = HPG, HEADDIM, STATE, CHUNK
    H = G * R
    nc = T // Q
    sp = _ssd_specs(T, G, n_ctx, gpb, lambda d, s, nc_ctx, n: _scan_chunk(d, n - 1 - s, nc_ctx, n))

    def body(x_ref, b_ref, c_ref, dy_ref, st_ref, dt_ref, ein_ref, dte_ref, la_ref, lb_ref, et_ref,
             dx_ref, db_ref, dc_ref, ddt_ref, dlc_ref, dlr_ref, dS):
        d, s = pl.program_id(0), pl.program_id(2)

        @pl.when(s == 0)
        def _():
            dS[...] = jnp.zeros_like(dS)

        mask = _chunk_mask(d)
        ri = lax.broadcasted_iota(jnp.int32, (Q, 1), 0)
        is_last = ri == jnp.where(d == 0, Q - 1, 0)
        head0 = lax.broadcasted_iota(jnp.int32, (Q, 2 * P), 1) < P
        rows0 = lax.broadcasted_iota(jnp.int32, (2 * P, N), 0) < P

        def total(v):
            return jnp.sum(jnp.sum(v, axis=1, keepdims=True), axis=0, keepdims=True)

        for gg in range(gpb):
            Bm = b_ref[:, gg * N:(gg + 1) * N].astype(BF16)
            Cm = c_ref[:, gg * N:(gg + 1) * N].astype(BF16)
            Gm = lax.dot_general(Cm, Bm, (((1,), (1,)), ((), ())), preferred_element_type=F32)
            dG = jnp.zeros((Q, Q), F32)
            dB = jnp.zeros((Q, N), F32)
            dC = jnp.zeros((Q, N), F32)
            for k in range(R // 2):
                pk = gg * (R // 2) + k
                sl = slice(pk * 2 * P, (pk + 1) * 2 * P)
                e_in = ein_ref[:, sl]
                dte = dte_ref[:, sl]
                e_all = jnp.where(rows0, et_ref[gg, 2 * k:2 * k + 1, :], et_ref[gg, 2 * k + 1:2 * k + 2, :])
                xp = x_ref[:, sl]
                dtp = dt_ref[:, sl]
                xc = xp * dtp
                xc_b = xc.astype(BF16)
                dyp = dy_ref[:, sl]
                s_in = st_ref[pk]
                s_in_b = s_in.astype(BF16)
                ds_out = dS[pk]
                ds_out_b = ds_out.astype(BF16)
                y_int = lax.dot_general(Cm, s_in_b, (((1,), (1,)), ((), ())), preferred_element_type=F32) * e_in
                b_ds = lax.dot_general(Bm, ds_out_b, (((1,), (1,)), ((), ())), preferred_element_type=F32)
                dxc = dte * b_ds
                u = xc * dxc
                v = dyp * y_int - u
                sse = ds_out * s_in * e_all
                for j in range(2):
                    hr, r = 2 * pk + j, 2 * k + j

                    def pick(a, m0=head0, j=j):
                        return jnp.where(m0, a, 0.0) if j == 0 else jnp.where(m0, 0.0, a)

                    diff = lax.dot_general(la_ref[hr], lb_ref[hr], (((0,), (0,)), ((), ())), preferred_element_type=F32)
                    ldec = jnp.exp(jnp.where(mask, diff, NEG_BIG))
                    dy_j = pick(dyp).astype(BF16)
                    dM = lax.dot_general(dy_j, xc_b, (((1,), (1,)), ((), ())), preferred_element_type=F32)
                    dMl = dM * ldec
                    Wm = dMl * Gm
                    dlam_c = jnp.sum(Wm, axis=1, keepdims=True) + jnp.sum(pick(v), axis=1, keepdims=True)
                    last = total(pick(sse, rows0)) + total(pick(u))
                    dlc_ref[gg, :, r:r + 1] = dlam_c + jnp.where(is_last, last, 0.0)
                    dlr_ref[gg, r:r + 1, :] = -jnp.sum(Wm, axis=0, keepdims=True)
                    dxc = dxc + lax.dot_general((Gm * ldec).astype(BF16), dy_j, (((0,), (0,)), ((), ())),
                                                preferred_element_type=F32)
                    dG = dG + dMl
                dx_ref[:, sl] = dxc * dtp
                t = dxc * xp
                ddt_ref[gg, :, 2 * k:2 * k + 1] = jnp.sum(jnp.where(head0, t, 0.0), axis=1, keepdims=True)
                ddt_ref[gg, :, 2 * k + 1:2 * k + 2] = jnp.sum(jnp.where(head0, 0.0, t), axis=1, keepdims=True)
                edy_b = (e_in * dyp).astype(BF16)
                dC = dC + jnp.dot(edy_b, s_in_b, preferred_element_type=F32)
                dB = dB + jnp.dot((dte * xc).astype(BF16), ds_out_b, preferred_element_type=F32)
                dS[pk] = e_all * ds_out + lax.dot_general(edy_b, Cm, (((0,), (0,)), ((), ())),
                                                          preferred_element_type=F32)
            dG_b = dG.astype(BF16)
            dc_ref[:, gg * N:(gg + 1) * N] = dC + jnp.dot(dG_b, Bm, preferred_element_type=F32)
            db_ref[:, gg * N:(gg + 1) * N] = dB + lax.dot_general(dG_b, Cm, (((0,), (0,)), ((), ())),
                                                                  preferred_element_type=F32)

    return _call_with_exchange(
        body, exch, name=name, grid=(2, G // gpb, nc),
        in_specs=[sp["x"], sp["b"], sp["c"], sp["x"], sp["st"], sp["rep"], sp["rep"], sp["rep"], sp["lam_a"],
                  sp["lam_b"], sp["etot"]],
        out_specs=[sp["xd"], sp["bd"], sp["bd"], sp["col"], sp["col"], sp["row"]],
        out_shape=[jax.ShapeDtypeStruct((2, T, H * P), F32), jax.ShapeDtypeStruct((2, T, G * N), F32),
                   jax.ShapeDtypeStruct((2, T, G * N), F32), jax.ShapeDtypeStruct((2, G, T, R), F32),
                   jax.ShapeDtypeStruct((2, G, T, R), F32), jax.ShapeDtypeStruct((2, G, R, T), F32)],
        scratch_shapes=[pltpu.VMEM((gpb * R // 2, 2 * P, N), F32)],
        operands=(xbc, xbc, xbc, dy, states, dtb, ein, dte, lam_a, lam_b, etot))


def _gnorm_fwd(y2, proj, w, gs, tr, name):
    T, HP = y2.shape[1], y2.shape[2]

    def body(y_ref, z_ref, w_ref, o_ref):
        yz = (y_ref[0] + y_ref[1]) * _silu(z_ref[...])
        for g in range(HP // gs):
            v = yz[:, g * gs:(g + 1) * gs]
            rstd = lax.rsqrt(jnp.mean(v * v, axis=-1, keepdims=True) + EPS)
            o_ref[:, g * gs:(g + 1) * gs] = (v * rstd * w_ref[:, g * gs:(g + 1) * gs]).astype(BF16)

    return pl.pallas_call(
        body, name=name, grid=(T // tr,),
        in_specs=[pl.BlockSpec((2, tr, HP), lambda i: (0, i, 0)), pl.BlockSpec((tr, HP), lambda i: (i, 0)),
                  pl.BlockSpec((1, HP), lambda i: (0, 0))],
        out_specs=pl.BlockSpec((tr, HP), lambda i: (i, 0)),
        out_shape=jax.ShapeDtypeStruct((T, HP), BF16),
        compiler_params=_cparams(("parallel",)))(y2, proj, w)


def _gnorm_bwd(y2, proj, w, dyn, gs, tr, name):
    T, HP = y2.shape[1], y2.shape[2]
    row = pl.BlockSpec((tr, HP), lambda i: (i, 0))

    def body(y_ref, z_ref, w_ref, d_ref, dy_ref, dz_ref, dw_ref):
        @pl.when(pl.program_id(0) == 0)
        def _():
            dw_ref[...] = jnp.zeros_like(dw_ref)

        yv = y_ref[0] + y_ref[1]
        zv = z_ref[...]
        sz = _silu(zv)
        yz = yv * sz
        dv = d_ref[...]
        r = lax.broadcasted_iota(jnp.int32, (8, gs), 0)
        for g in range(HP // gs):
            sl = slice(g * gs, (g + 1) * gs)
            v = yz[:, sl]
            rstd = lax.rsqrt(jnp.mean(v * v, axis=-1, keepdims=True) + EPS)
            xhat = v * rstd
            dyn_g = dv[:, sl]
            dhat = dyn_g * w_ref[:, sl]
            dyz = rstd * (dhat - xhat * jnp.mean(dhat * xhat, axis=-1, keepdims=True))
            dy_ref[:, sl] = dyz * sz[:, sl]
            dz_ref[:, sl] = (dyz * yv[:, sl] * _dsilu(zv[:, sl])).astype(BF16)
            dw_ref[:, sl] += jnp.where(r == 0, jnp.sum(dyn_g * xhat, axis=0, keepdims=True), 0.0)

    return pl.pallas_call(
        body, name=name, grid=(T // tr,),
        in_specs=[pl.BlockSpec((2, tr, HP), lambda i: (0, i, 0)), row, pl.BlockSpec((1, HP), lambda i: (0, 0)), row],
        out_specs=[row, row, pl.BlockSpec((8, HP), lambda i: (0, 0))],
        out_shape=[jax.ShapeDtypeStruct((T, HP), F32), jax.ShapeDtypeStruct((T, HP), BF16),
                   jax.ShapeDtypeStruct((8, HP), F32)],
        compiler_params=_cparams(("arbitrary",)))(y2, proj, w, dyn)


def _merge_fwd(proj, g1_blk, o_ssd, o_pool, tr, name):
    T, D = o_ssd.shape
    row = pl.BlockSpec((tr, D), lambda i: (i, 0))

    def body(g1_ref, g2_ref, a_ref, b_ref, o_ref):
        o_ref[...] = (jax.nn.sigmoid(g1_ref[...]) * a_ref[...]
                      + jax.nn.sigmoid(g2_ref[...]) * b_ref[...]).astype(BF16)

    return pl.pallas_call(
        body, name=name, grid=(T // tr,),
        in_specs=[pl.BlockSpec((tr, D), lambda i: (i, g1_blk)), pl.BlockSpec((tr, D), lambda i: (i, g1_blk + 1)),
                  row, row],
        out_specs=row, out_shape=jax.ShapeDtypeStruct((T, D), BF16),
        compiler_params=_cparams(("parallel",)))(proj, proj, o_ssd, o_pool)


def _merge_bwd(proj, g1_blk, o_ssd, o_pool, dmg, tr, name):
    T, D = o_ssd.shape
    row = pl.BlockSpec((tr, D), lambda i: (i, 0))

    def body(g1_ref, g2_ref, a_ref, b_ref, d_ref, da_ref, db_ref, dg_ref):
        s1, s2 = jax.nn.sigmoid(g1_ref[...]), jax.nn.sigmoid(g2_ref[...])
        dv = d_ref[...]
        da_ref[...] = (s1 * dv).astype(BF16)
        db_ref[...] = (s2 * dv).astype(BF16)
        dg_ref[:, :D] = (dv * a_ref[...] * s1 * (1.0 - s1)).astype(BF16)
        dg_ref[:, D:] = (dv * b_ref[...] * s2 * (1.0 - s2)).astype(BF16)

    return pl.pallas_call(
        body, name=name, grid=(T // tr,),
        in_specs=[pl.BlockSpec((tr, D), lambda i: (i, g1_blk)), pl.BlockSpec((tr, D), lambda i: (i, g1_blk + 1)),
                  row, row, row],
        out_specs=[row, row, pl.BlockSpec((tr, 2 * D), lambda i: (i, 0))],
        out_shape=[jax.ShapeDtypeStruct((T, D), BF16), jax.ShapeDtypeStruct((T, D), BF16),
                   jax.ShapeDtypeStruct((T, 2 * D), BF16)],
        compiler_params=_cparams(("parallel",)))(proj, proj, o_ssd, o_pool, dmg)


def _swiglu_fwd(gu, tr, name):
    T, F2 = gu.shape
    F = F2 // 2
    tc = _tile(F, (1408, 768, 512, 256, 128))
    nb = F // tc

    def body(a_ref, b_ref, o_ref):
        o_ref[...] = (_silu(a_ref[...]) * b_ref[...]).astype(BF16)

    return pl.pallas_call(
        body, name=name, grid=(T // tr, nb),
        in_specs=[pl.BlockSpec((tr, tc), lambda i, j: (i, j)), pl.BlockSpec((tr, tc), lambda i, j: (i, nb + j))],
        out_specs=pl.BlockSpec((tr, tc), lambda i, j: (i, j)),
        out_shape=jax.ShapeDtypeStruct((T, F), BF16),
        compiler_params=_cparams(("parallel", "parallel")))(gu, gu)


def _swiglu_bwd(gu, dact, tr, name):
    T, F2 = gu.shape
    F = F2 // 2
    tc = _tile(F, (1408, 768, 512, 256, 128))
    nb = F // tc

    def body(a_ref, b_ref, d_ref, o_ref):
        is_a = pl.program_id(1) < nb
        av, bv, dv = a_ref[...], b_ref[...], d_ref[...]
        o_ref[...] = jnp.where(is_a, dv * bv * _dsilu(av), dv * _silu(av)).astype(BF16)

    return pl.pallas_call(
        body, name=name, grid=(T // tr, 2 * nb),
        in_specs=[pl.BlockSpec((tr, tc), lambda i, j: (i, j % nb)), pl.BlockSpec((tr, tc), lambda i, j: (i, nb + j % nb)),
                  pl.BlockSpec((tr, tc), lambda i, j: (i, j % nb))],
        out_specs=pl.BlockSpec((tr, tc), lambda i, j: (i, j)),
        out_shape=jax.ShapeDtypeStruct((T, F2), BF16),
        compiler_params=_cparams(("parallel", "parallel")))(gu, gu, dact)


def _adamw(w, g, m, v, name):
    Rr, C = w.shape
    tr = _tile(Rr, (256, 128, 64, 32, 16, 8))
    row = pl.BlockSpec((tr, C), lambda i: (i, 0))

    def body(w_ref, g_ref, m_ref, v_ref, d_ref, mo_ref, vo_ref):
        gv = g_ref[...]
        mn = ADAM_B1 * m_ref[...] + (1.0 - ADAM_B1) * gv
        vn = ADAM_B2 * v_ref[...] + (1.0 - ADAM_B2) * (gv * gv)
        m_hat = mn / (1.0 - ADAM_B1 ** ADAM_STEP)
        v_hat = vn / (1.0 - ADAM_B2 ** ADAM_STEP)
        d_ref[...] = -ADAM_LR * (m_hat / (jnp.sqrt(v_hat) + ADAM_EPS) + ADAM_WD * w_ref[...])
        mo_ref[...] = mn
        vo_ref[...] = vn

    sds = jax.ShapeDtypeStruct((Rr, C), F32)
    return pl.pallas_call(body, name=name, grid=(Rr // tr,), in_specs=[row] * 4, out_specs=[row] * 3,
                          out_shape=[sds] * 3, compiler_params=_cparams(("parallel",)))(w, g, m, v)


def _sum_slots(x, name):
    n, Rr, C = x.shape
    tr = _tile(Rr, (512, 256, 128, 64, 32, 16, 8))

    def body(x_ref, o_ref):
        acc = x_ref[0].astype(F32)
        for k in range(1, n):
            acc = acc + x_ref[k].astype(F32)
        o_ref[...] = acc

    return pl.pallas_call(body, name=name, grid=(Rr // tr,),
                          in_specs=[pl.BlockSpec((n, tr, C), lambda i: (0, i, 0))],
                          out_specs=pl.BlockSpec((tr, C), lambda i: (i, 0)),
                          out_shape=jax.ShapeDtypeStruct((Rr, C), F32),
                          compiler_params=_cparams(("parallel",)))(x)


def _place():
    return lax.axis_index("x"), lax.axis_index("y"), lax.axis_index("c")


def _all_gather(x, name):
    Rr, C = x.shape

    def body(x_ref, out_ref, send_sems, recv_sems, local_sem):
        _gather_start(x_ref, out_ref, send_sems, recv_sems, local_sem)
        _gather_finish(x_ref, out_ref, send_sems, recv_sems, local_sem)

    return pl.pallas_call(
        body, name=name, in_specs=[ANY], out_specs=ANY,
        out_shape=jax.ShapeDtypeStruct((N_DEV, Rr, C), x.dtype), scratch_shapes=_GATHER_SEMS,
    )(x)


_GATHER_SEMS = [pltpu.SemaphoreType.DMA((7,)), pltpu.SemaphoreType.DMA((7,)), pltpu.SemaphoreType.DMA]


def _gather_copies(x_ref, out_ref, send_sems, recv_sems, local_sem):
    mx, my, mc = _place()
    me, sibling = (mx, my, mc), (mx, my, 1 - mc)
    chips = [(1 - mx, my), (mx, 1 - my), (1 - mx, 1 - my)]

    def slot(px, py, pc):
        return out_ref.at[4 * px + 2 * py + pc]

    def copy(k, block, to, src=None):
        return pltpu.make_async_remote_copy(
            src_ref=slot(*block) if src is None else src, dst_ref=slot(*block),
            send_sem=send_sems.at[k], recv_sem=recv_sems.at[k],
            device_id=to, device_id_type=pl.DeviceIdType.MESH)

    return dict(
        mine=pltpu.make_async_copy(x_ref, slot(*me), local_sem),
        first=[copy(0, me, sibling, src=x_ref)] + [copy(1 + j, me, (*chip, mc), src=x_ref)
                                                   for j, chip in enumerate(chips)],
        passed=[copy(4 + j, (*chip, mc), sibling) for j, chip in enumerate(chips)],
        from_chips=[copy(1 + j, (*chip, mc), me) for j, chip in enumerate(chips)],
        from_sibling=[copy(0, sibling, me)] + [copy(4 + j, (*chip, 1 - mc), me) for j, chip in enumerate(chips)],
    )


def _gather_start(*refs):
    cps = _gather_copies(*refs)
    cps["mine"].start()
    for cp in cps["first"]:
        cp.start()


def _gather_finish(*refs):
    cps = _gather_copies(*refs)
    for j in range(3):
        cps["from_chips"][j].wait_recv()
        cps["passed"][j].start()
    for cp in cps["from_sibling"]:
        cp.wait_recv()
    for cp in cps["first"] + cps["passed"]:
        cp.wait_send()
    cps["mine"].wait()


def _pair_exchange(buf, name):
    def body(b_ref, got_ref, send_sems, recv_sems):
        _pair_start(b_ref, got_ref, send_sems, recv_sems)
        _pair_finish(b_ref, got_ref, send_sems, recv_sems)

    return pl.pallas_call(
        body, name=name, in_specs=[ANY], out_specs=ANY,
        out_shape=jax.ShapeDtypeStruct(buf.shape[1:], buf.dtype), scratch_shapes=_PAIR_SEMS,
    )(buf)


_PAIR_PARTS = 4
_PAIR_SEMS = [pltpu.SemaphoreType.DMA((4 * _PAIR_PARTS,)), pltpu.SemaphoreType.DMA((4 * _PAIR_PARTS,))]


def _pair_copies(b_ref, got_ref, send_sems, recv_sems):
    mx, my, mc = _place()
    n, Rr = got_ref.shape[0], got_ref.shape[1]
    pr = Rr // _PAIR_PARTS
    return [pltpu.make_async_remote_copy(
        src_ref=b_ref.at[1 - mc, q, pl.ds(p * pr, pr)], dst_ref=got_ref.at[q, pl.ds(p * pr, pr)],
        send_sem=send_sems.at[q * _PAIR_PARTS + p], recv_sem=recv_sems.at[q * _PAIR_PARTS + p],
        device_id=(mx, my, 1 - mc), device_id_type=pl.DeviceIdType.MESH)
        for q in range(n) for p in range(_PAIR_PARTS)]


def _pair_start(*refs):
    for cp in _pair_copies(*refs):
        cp.start()


def _pair_finish(*refs):
    for cp in _pair_copies(*refs):
        cp.wait()


def _pair_add(buf, got, my_c, name):
    _, n, Rr, C = buf.shape
    tr = _tile(Rr, (512, 256, 128, 64, 32, 16))

    def body(c_ref, b_ref, g_ref, o_ref):
        o_ref[...] = (b_ref[...].astype(F32) + g_ref[...].astype(F32)).astype(BF16)

    return pl.pallas_call(
        body, name=name,
        grid_spec=pltpu.PrefetchScalarGridSpec(
            num_scalar_prefetch=1, grid=(n, Rr // tr),
            in_specs=[pl.BlockSpec((None, None, tr, C), lambda q, i, c: (c[0], q, i, 0)),
                      pl.BlockSpec((None, tr, C), lambda q, i, c: (q, i, 0))],
            out_specs=pl.BlockSpec((None, tr, C), lambda q, i, c: (q, i, 0))),
        out_shape=jax.ShapeDtypeStruct((n, Rr, C), BF16),
        compiler_params=_cparams(("parallel", "parallel")))(my_c, buf, got)


def _chip_exchange(red, name):
    def body(r_ref, out_ref, send_sems, recv_sems, local_sem):
        _chip_exchange_start(r_ref, out_ref, send_sems, recv_sems, local_sem)
        _chip_exchange_finish(r_ref, out_ref, send_sems, recv_sems, local_sem)

    return pl.pallas_call(
        body, name=name, in_specs=[ANY], out_specs=ANY,
        out_shape=jax.ShapeDtypeStruct(red.shape, red.dtype), scratch_shapes=_CHIP_SEMS,
    )(red)


_CHIP_SEMS = [pltpu.SemaphoreType.DMA((3,)), pltpu.SemaphoreType.DMA((3,)), pltpu.SemaphoreType.DMA]


def _chip_exchange_copies(r_ref, out_ref, send_sems, recv_sems, local_sem):
    mx, my, mc = _place()
    chips = [(1 - mx, my), (mx, 1 - my), (1 - mx, 1 - my)]

    def copy(k, src_slot, dst_slot, to):
        return pltpu.make_async_remote_copy(
            src_ref=r_ref.at[src_slot], dst_ref=out_ref.at[dst_slot],
            send_sem=send_sems.at[k], recv_sem=recv_sems.at[k],
            device_id=(*to, mc), device_id_type=pl.DeviceIdType.MESH)

    return dict(
        mine=pltpu.make_async_copy(r_ref.at[2 * mx + my], out_ref.at[2 * mx + my], local_sem),
        sends=[copy(k, 2 * px + py, 2 * mx + my, (px, py)) for k, (px, py) in enumerate(chips)],
        recvs=[copy(k, 2 * px + py, 2 * px + py, (px, py)) for k, (px, py) in enumerate(chips)],
    )


def _chip_exchange_start(*refs):
    cps = _chip_exchange_copies(*refs)
    cps["mine"].start()
    for cp in cps["sends"]:
        cp.start()


def _chip_exchange_finish(*refs):
    cps = _chip_exchange_copies(*refs)
    for cp in cps["recvs"]:
        cp.wait_recv()
    for cp in cps["sends"]:
        cp.wait_send()
    cps["mine"].wait()


def _pad_rows(a, rows):
    return jnp.pad(a, ((0, rows - a.shape[0]), (0, 0)))


class _Layout:
    def __init__(self, D, shards):
        self.D = D
        self.pieces = []
        off = 0
        for name, layer, rows in shards:
            pr = _round_up(rows, 16)
            self.pieces.append((name, layer, rows, pr, off))
            off += pr
        self.rows = _round_up(off, 256)

    def find(self, name, layer):
        for p in self.pieces:
            if p[0] == name and p[1] == layer:
                return p
        raise KeyError(name)


def kernel(x, c, ctx, c_ctx, w_ada, b_ada, g_mix, w_in, conv_w, conv_b, dt_bias, a_log, d_skip, ssd_norm_w, w_ssd_out, pool_w, pool_scale, w_pool_out, w_out, g_ffn, w_gate_up, w_down, g_final, loss_target, m_c_ctx, m_w_ada, m_b_ada, m_g_mix, m_w_in, m_conv_w, m_conv_b, m_dt_bias, m_a_log, m_d_skip, m_ssd_norm_w, m_w_ssd_out, m_pool_w, m_pool_scale, m_w_pool_out, m_w_out, m_g_ffn, m_w_gate_up, m_w_down, m_g_final, v_c_ctx, v_w_ada, v_b_ada, v_g_mix, v_w_in, v_conv_w, v_conv_b, v_dt_bias, v_a_log, v_d_skip, v_ssd_norm_w, v_w_ssd_out, v_pool_w, v_pool_scale, v_w_pool_out, v_w_out, v_g_ffn, v_w_gate_up, v_w_down, v_g_final):
    weights = dict(c_ctx=c_ctx, w_ada=w_ada, b_ada=b_ada, g_mix=g_mix, w_in=w_in, conv_w=conv_w, conv_b=conv_b,
                   dt_bias=dt_bias, a_log=a_log, d_skip=d_skip, ssd_norm_w=ssd_norm_w, w_ssd_out=w_ssd_out,
                   pool_w=pool_w, pool_scale=pool_scale, w_pool_out=w_pool_out, w_out=w_out, g_ffn=g_ffn,
                   w_gate_up=w_gate_up, w_down=w_down, g_final=g_final)
    moms_m = dict(c_ctx=m_c_ctx, w_ada=m_w_ada, b_ada=m_b_ada, g_mix=m_g_mix, w_in=m_w_in, conv_w=m_conv_w,
                  conv_b=m_conv_b, dt_bias=m_dt_bias, a_log=m_a_log, d_skip=m_d_skip, ssd_norm_w=m_ssd_norm_w,
                  w_ssd_out=m_w_ssd_out, pool_w=m_pool_w, pool_scale=m_pool_scale, w_pool_out=m_w_pool_out,
                  w_out=m_w_out, g_ffn=m_g_ffn, w_gate_up=m_w_gate_up, w_down=m_w_down, g_final=m_g_final)
    moms_v = dict(c_ctx=v_c_ctx, w_ada=v_w_ada, b_ada=v_b_ada, g_mix=v_g_mix, w_in=v_w_in, conv_w=v_conv_w,
                  conv_b=v_conv_b, dt_bias=v_dt_bias, a_log=v_a_log, d_skip=v_d_skip, ssd_norm_w=v_ssd_norm_w,
                  w_ssd_out=v_w_ssd_out, pool_w=v_pool_w, pool_scale=v_pool_scale, w_pool_out=v_w_pool_out,
                  w_out=v_w_out, g_ffn=v_g_ffn, w_gate_up=v_w_gate_up, w_down=v_w_down, g_final=v_g_final)
    order = ["c_ctx", "w_ada", "b_ada", "g_mix", "w_in", "conv_w", "conv_b", "dt_bias", "a_log", "d_skip",
             "ssd_norm_w", "w_ssd_out", "pool_w", "pool_scale", "w_pool_out", "w_out", "g_ffn", "w_gate_up",
             "w_down", "g_final"]
    big = ["w_ada", "w_in", "conv_w", "w_ssd_out", "pool_w", "w_pool_out", "w_out", "w_gate_up", "w_down"]
    small = [n for n in order if n not in big]

    depth = w_in.shape[0]
    L, D = x.shape[1], x.shape[2]
    n_ctx = ctx.shape[1]
    T = n_ctx + L
    in_cols = w_in.shape[2] * N_DEV
    xbc_w = conv_w.shape[2] * N_DEV
    dinner = ssd_norm_w.shape[1]
    H = dt_bias.shape[2]
    G = H // HPG
    GN = G * STATE
    assert xbc_w == dinner + 2 * GN and dinner == H * HEADDIM
    assert in_cols == dinner + xbc_w + 2 * H + D + 2 * D
    assert dinner == 2 * D and GN == D and 2 * H <= 128
    F = w_down.shape[1] * N_DEV
    pg = pool_w.shape[3]
    tr = n_ctx
    assert L % tr == 0 and tr % GRID_W == 0 and tr % CHUNK == 0 and L % CHUNK == 0
    n_ctx_tiles = 1
    tr_big = _tile(T, (1088, 544, 512, 256, 128))
    tr_mid = _tile(T, (544, 512, 256, 128))
    NP =_round_up(9 * D + 128, 512)
    gs = dinner // G
    off_xbc, off_dt, off_pool = dinner, dinner + xbc_w, dinner + xbc_w + 2 * H
    off_gate = off_pool + D

    def shard_rows(name, l):
        w = weights[name][l]
        if name in ("w_ada", "w_in", "w_gate_up"):
            return w.T
        if name == "conv_w":
            w8 = _pad_rows(w, 8)
            hi = w8.astype(BF16)
            lo = (w8 - hi.astype(F32)).astype(BF16)
            return jnp.concatenate([hi, lo], axis=0).reshape(-1, D)
        if name == "pool_w":
            return w.reshape(-1, D)
        return w

    first_needed = ["w_ada", "w_in"]
    shard = {(n, l): shard_rows(n, l) for l in range(depth) for n in big}
    gather_groups = [[first_needed, [n for n in big if n not in first_needed]] for l in range(depth)]
    glays = [[_Layout(D, [(n, l, shard[(n, l)].shape[0]) for n in grp]) for grp in gather_groups[l]]
             for l in range(depth)]

    def packed(l, gi):
        lay = glays[l][gi]
        rows = jnp.concatenate([_pad_rows(shard[(n, l)].astype(BF16), lay.find(n, l)[3])
                                for n in gather_groups[l][gi]], axis=0)
        return _pad_rows(rows, lay.rows)

    def gather_exchange(l, gi):
        return _Exchange(packed(l, gi), jax.ShapeDtypeStruct((N_DEV, glays[l][gi].rows, D), BF16), _GATHER_SEMS,
                         _gather_start, _gather_finish)

    gathered = [[None] * len(g) for g in gather_groups]
    gathered[0][0] = _all_gather(packed(0, 0), "gather_weights")

    def full(name, l):
        gi = [name in grp for grp in gather_groups[l]].index(True)
        _, _, rows, _, off = glays[l][gi].find(name, l)
        return gathered[l][gi][:, off:off + rows, :]

    def w_inT_new(l):
        w = full("w_in", l).reshape(in_cols, D)
        parts = [w[:off_xbc], w[off_xbc:off_dt], w[off_pool:off_gate], w[off_gate:], w[off_dt:off_pool]]
        return _pad_rows(jnp.concatenate(parts, axis=0), NP)

    xs0 = jnp.concatenate([ctx[0], x[0]], axis=0)
    cc8 = _pad_rows(jnp.concatenate([c, c_ctx[None, :]], axis=0), 8)
    tgt = loss_target[0]

    def vec(a):
        return a.reshape(1, -1)

    def pad128(a):
        return jnp.pad(a.reshape(1, -1), ((0, 0), (0, 128 - 2 * H)))

    expand = (jnp.arange(128)[:, None] == jnp.arange(2 * H * HEADDIM)[None, :] // HEADDIM).astype(BF16)

    def from4(arr):
        return jnp.pad(arr.transpose(2, 0, 1, 3).reshape(T, 2 * H), ((0, 0), (0, 128 - 2 * H)))

    dt_blk = (9 * D) // 128
    conv_tc = 128
    ssd_gpb, ssd_gpb_bwd = min(G, 8), 2
    saved = []
    xcur = xs0
    for l in range(depth):
        W = dict(adaT=full("w_ada", l).reshape(6 * D, D), inT=w_inT_new(l))
        m6 = _ada_fwd(cc8, W["adaT"], vec(b_ada[l]), "ada_fwd")
        h = _norm_mod(xcur, vec(g_mix[l]), m6, 0, 1, n_ctx_tiles, tr, "norm_mod")
        if len(gather_groups[l]) > 1:
            proj, gathered[l][1] = _mm(h, W["inT"], "nt", F32, "mm_in_gather", exch=gather_exchange(l, 1))
        else:
            proj = _mm(h, W["inT"], "nt", F32, "mm_in")
        W.update(
            ssd=full("w_ssd_out", l).reshape(dinner, D), po=full("w_pool_out", l).reshape(D, D),
            out=full("w_out", l).reshape(D, D), guT=full("w_gate_up", l).reshape(2 * F, D),
            down=full("w_down", l).reshape(F, D),
            pool=full("pool_w", l).reshape(N_DEV, len(POOL_WINDOWS), pg // N_DEV, pg).transpose(1, 0, 2, 3)
            .reshape(len(POOL_WINDOWS), pg, pg),
        )
        cw = full("conv_w", l).reshape(N_DEV, 16, xbc_w // N_DEV).astype(F32)
        W["conv8"] = (cw[:, :8] + cw[:, 8:]).transpose(1, 0, 2).reshape(8, xbc_w)
        l1, l2, l3, dtb, ein, dte, etot = _dt_prep(proj, dt_blk, pad128(dt_bias[l]), pad128(a_log[l]), expand, H,
                                                   "dt_prep")
        L1, L2, L3 = (v[:, :2 * H].T.reshape(2, H, 1, T) for v in (l1, l2, l3))
        k16 = jnp.arange(16).reshape(1, 1, 16, 1)

        def rows16(at):
            terms = jnp.where(k16 == at, L1, jnp.where(k16 == at + 1, L2, L3))
            return jnp.where((k16 >= at) & (k16 < at + 3), terms, (k16 < 6).astype(BF16))

        lam_a, lam_b = rows16(0), -rows16(3) + 2 * (k16 < 3).astype(BF16)
        et = etot.reshape(T // CHUNK, 8, 128)[:, 0, :2 * H].reshape(T // CHUNK, 2, G, HPG).transpose(1, 2, 0, 3)
        etot5 = jnp.pad(jnp.broadcast_to(et[..., None], et.shape + (128,)),
                        ((0, 0), (0, 0), (0, 0), (0, 8 - HPG), (0, 0)))
        dsk = jnp.pad(jnp.repeat(d_skip[l], HEADDIM, axis=1)[:, None, :], ((0, 0), (0, 7), (0, 0)))
        scan_ops = (dtb, ein, dte, lam_a, lam_b, etot5)
        xbc = _conv_fwd(proj, dinner // conv_tc, xbc_w, W["conv8"], vec(conv_b[l]), n_ctx, conv_tc, "conv_fwd")
        if l + 1 < depth:
            y2, states, gathered[l + 1][0] = _ssd_fwd(xbc, *scan_ops, dsk, G, n_ctx, ssd_gpb, "ssd_fwd_gather",
                                                      gather_exchange(l + 1, 0))
        else:
            y2, states = _ssd_fwd(xbc, *scan_ops, dsk, G, n_ctx, ssd_gpb, "ssd_fwd")
        yn = _gnorm_fwd(y2, proj, vec(ssd_norm_w[l]), gs, tr_mid, "gnorm_fwd")
        pm = _pool_apply(proj, (6 * D) // pg, BF16, n_ctx, pg, False, "pool_fwd")
        pms = _pool_mix_fwd(pm, W["pool"], vec(pool_scale[l]), tr_big, "pool_mix_fwd")
        o_ssd = _mm(yn, W["ssd"], "nn", F32, "mm_ssd_out")
        o_pool = _mm(pms, W["po"], "nn", F32, "mm_pool_out")
        mg = _merge_fwd(proj, 7, o_ssd, o_pool, tr_mid, "merge_fwd")
        mo = _mm(mg, W["out"], "nn", F32, "mm_out")
        x1, h2 = _norm_mod(xcur, vec(g_ffn[l]), m6, 3, 4, n_ctx_tiles, tr, "resid_norm_mod", resid=(mo, 2))
        gu = _mm(h2, W["guT"], "nt", F32, "mm_gate_up")
        act = _swiglu_fwd(gu, tr_mid, "swiglu_fwd")
        f = _mm(act, W["down"], "nn", F32, "mm_down")
        saved.append(dict(W=W, m6=m6, x0=xcur, h=h, proj=proj, scan_ops=scan_ops, xbc=xbc, y2=y2,
                          states=states, yn=yn, pm=pm, pms=pms, o_ssd=o_ssd, o_pool=o_pool, mg=mg, mo=mo, x1=x1,
                          h2=h2, gu=gu, act=act, f=f))
        xcur = _resid(x1, f, m6, 5, n_ctx_tiles, tr, "resid")

    loss_blk, dx, dgf = _loss_head(xcur, tgt, vec(g_final), n_ctx_tiles, tr, "loss_head")
    loss = lax.psum(loss_blk[0, 0], MESH_AXES)

    big_rows = {}
    small_g = {n: [None] * depth for n in small}
    d_c_ctx = jnp.zeros((D,), F32)
    late = ["w_in", "conv_w", "w_ada"]
    rs_groups = [[[n for n in big if n not in late], late[:2], late[2:]] if l == 0 else [big] for l in range(depth)]
    rlays = [[_Layout(D, [(n, l, shard[(n, l)].shape[0]) for n in grp]) for grp in rs_groups[l]]
             for l in range(depth)]
    my_c = lax.axis_index("c").astype(jnp.int32).reshape(1)

    def pack_group(l, gi):
        lay = rlays[l][gi]
        gparts = [jnp.pad(big_rows[(n, l)].astype(BF16), ((0, 0), (0, pr - rows), (0, 0)))
                  for n, _, rows, pr, _ in lay.pieces]
        gparts.append(jnp.zeros((N_DEV, lay.rows - sum(p[3] for p in lay.pieces), D), BF16))
        gbuf = jnp.concatenate(gparts, axis=1)
        return gbuf.reshape(2, 2, 2, lay.rows, D).transpose(2, 0, 1, 3, 4).reshape(2, 4, lay.rows, D)

    def reduce_pair(l, gi):
        gbuf = pack_group(l, gi)
        return _pair_add(gbuf, _pair_exchange(gbuf, "rs_pair_exchange"), my_c, "rs_pair_add")

    slots = {}
    pending = None
    pair_waiting = None
    for l in reversed(range(depth)):
        S = saved[l]
        W, m6, proj = S["W"], S["m6"], S["proj"]
        df, dga2 = _resid_bwd(dx, S["f"], m6, 5, n_ctx_tiles, tr, "resid_bwd")
        if pair_waiting is None:
            dact = _mm(df, W["down"], "nt", F32, "mm_down_dx")
        else:
            key, gbuf = pair_waiting
            dact, got = _mm(df, W["down"], "nt", F32, "mm_down_dx_exchange", exch=_Exchange(
                gbuf, jax.ShapeDtypeStruct(gbuf.shape[1:], BF16), _PAIR_SEMS, _pair_start, _pair_finish))
            pending, pair_waiting = (key, _pair_add(gbuf, got, my_c, "rs_pair_add")), None
        g_down = _mm(S["act"], df, "tn", BF16,"mm_down_dw")
        dgu = _swiglu_bwd(S["gu"], dact, tr_mid, "swiglu_bwd")
        dh2 = _mm(dgu, W["guT"], "nn", F32, "mm_gate_up_dx")
        g_guT = _mm(dgu, S["h2"], "tn", BF16,"mm_gate_up_dw")
        dx1, st2 = _norm_mod_bwd(S["x1"], dh2, dx, vec(g_ffn[l]), m6, 4, n_ctx_tiles, tr, "norm_mod_bwd")
        dmo, dga1 = _resid_bwd(dx1, S["mo"], m6, 2, n_ctx_tiles, tr, "resid_bwd")
        dmg = _mm(dmo, W["out"], "nt", F32, "mm_out_dx")
        g_out = _mm(S["mg"], dmo, "tn", BF16,"mm_out_dw")
        do_ssd, do_pool, dgl = _merge_bwd(proj, 7, S["o_ssd"], S["o_pool"], dmg, tr_mid, "merge_bwd")
        dyn = _mm(do_ssd, W["ssd"], "nt", F32, "mm_ssd_out_dx")
        g_ssd = _mm(S["yn"], do_ssd, "tn", BF16,"mm_ssd_out_dw")
        dpms = _mm(do_pool, W["po"], "nt", F32, "mm_pool_out_dx")
        g_po = _mm(S["pms"], do_pool, "tn", BF16,"mm_pool_out_dw")
        dpm, g_pool, dps = _pool_mix_bwd(S["pm"], dpms, W["pool"], vec(pool_scale[l]), tr_big, "pool_mix_bwd")
        dup = _pool_apply(dpm, 0, BF16, n_ctx, pg, True, "pool_bwd")
        dy, dz, dnw = _gnorm_bwd(S["y2"], proj, vec(ssd_norm_w[l]), dyn, gs, tr, "gnorm_bwd")
        big_rows[("w_ssd_out", l)] = g_ssd.reshape(N_DEV, -1, D)
        big_rows[("pool_w", l)] = g_pool.reshape(len(POOL_WINDOWS), N_DEV, pg // N_DEV, pg).transpose(1, 0, 2, 3) \
            .reshape(N_DEV, -1, D)
        big_rows[("w_pool_out", l)] = g_po.reshape(N_DEV, -1, D)
        big_rows[("w_out", l)] = g_out.reshape(N_DEV, -1, D)
        big_rows[("w_gate_up", l)] = g_guT.reshape(N_DEV, -1, D)
        big_rows[("w_down", l)] = g_down.reshape(N_DEV, -1, D)
        riding = [] if pending is None else [pending]
        if len(rs_groups[l]) > 1:
            riding.append(((l, 0), reduce_pair(l, 0)))
        if not riding:
            dxs2, db2, dc2, ddt4, dlc4, dlr4 = _ssd_bwd(S["xbc"], dy, S["states"], *S["scan_ops"], G, n_ctx,
                                                        ssd_gpb_bwd, "ssd_bwd")
        else:
            red = jnp.concatenate([r for _, r in riding], axis=1)
            exch = _Exchange(red, jax.ShapeDtypeStruct(red.shape, BF16), _CHIP_SEMS, _chip_exchange_start,
                             _chip_exchange_finish)
            dxs2, db2, dc2, ddt4, dlc4, dlr4, got = _ssd_bwd(
                S["xbc"], dy, S["states"], *S["scan_ops"], G, n_ctx, ssd_gpb_bwd, "ssd_bwd_exchange", exch)
            at = 0
            for key, r in riding:
                slots[key] = got[:, at:at + r.shape[1]]
                at += r.shape[1]
        dskv = _pad_rows(jnp.repeat(d_skip[l], HEADDIM, axis=1), 8)
        cb = vec(conv_b[l])
        dxbc_x, dconv_x = _conv_bwd(proj, dinner // conv_tc, dxs2, 0, W["conv8"], cb, n_ctx, conv_tc, "conv_bwd_x",
                                    skip=(dy, dskv))
        dxbc_b, dconv_b = _conv_bwd(proj, 2 * dinner // conv_tc, db2, dinner // conv_tc, W["conv8"], cb, n_ctx,
                                    conv_tc, "conv_bwd_bc")
        dxbc_c, dconv_c = _conv_bwd(proj, (2 * dinner + GN) // conv_tc, dc2, (dinner + GN) // conv_tc, W["conv8"], cb,
                                    n_ctx, conv_tc, "conv_bwd_bc")
        dconv = jnp.concatenate([dconv_x, dconv_b, dconv_c], axis=1)
        ddt_raw, dtst = _dt_bwd(proj, dt_blk, pad128(dt_bias[l]), pad128(a_log[l]), from4(ddt4),
                                from4(dlc4 + dlr4.transpose(0, 1, 3, 2)), H, tr, "dt_bwd")
        dproj = jnp.concatenate([dz, dxbc_x, dxbc_b, dxbc_c, dup, dgl, ddt_raw,
                                 jnp.zeros((T, NP - 9 * D - 128), BF16)], axis=1)
        g_inT_new = _mm(dproj, S["h"], "tn", BF16,"mm_in_dw")
        g_inT = jnp.concatenate([g_inT_new[:6 * D], g_inT_new[9 * D:9 * D + 2 * H], g_inT_new[6 * D:9 * D]], axis=0)
        big_rows[("w_in", l)] = g_inT.reshape(N_DEV, -1, D)
        big_rows[("conv_w", l)] = jnp.pad(
            dconv[:CONV_K].reshape(CONV_K, N_DEV, xbc_w // N_DEV).transpose(1, 0, 2),
            ((0, 0), (0, 16 - CONV_K), (0, 0))).reshape(N_DEV, -1, D)
        if len(rs_groups[l]) == 3:
            red = reduce_pair(l, 1)
            dh, slots[(l, 1)] = _mm(dproj, W["inT"], "nn", F32, "mm_in_dx_exchange", exch=_Exchange(
                red, jax.ShapeDtypeStruct(red.shape, BF16), _CHIP_SEMS, _chip_exchange_start, _chip_exchange_finish))
        else:
            dh = _mm(dproj, W["inT"], "nn", F32, "mm_in_dx")
        dx0, st1 = _norm_mod_bwd(S["x0"], dh, dx1, vec(g_mix[l]), m6, 1, n_ctx_tiles, tr, "norm_mod_bwd")
        dm6 = _pad_rows(jnp.concatenate([st1[0:2], st1[2:4], dga1[0:2], st2[0:2], st2[2:4], dga2[0:2]], axis=1), 8)
        dsil = _mm(dm6, W["adaT"], "nn", F32, "mm_ada_dx")
        sil_b, dcc, dbada = _ada_bwd_small(cc8, dsil, dm6, "ada_bwd_small")
        g_adaT = _mm(dm6, sil_b, "tn", BF16,"mm_ada_dw")
        d_c_ctx = d_c_ctx + dcc[1]
        dx = dx0

        big_rows[("w_ada", l)] = g_adaT.reshape(N_DEV, -1, D)
        small_g["b_ada"][l] = dbada[0]
        small_g["g_mix"][l] = st1[4]
        small_g["conv_b"][l] = dconv[CONV_K]
        small_g["dt_bias"][l] = dtst[0, :2 * H].reshape(2, H)
        small_g["a_log"][l] = dtst[1, :2 * H].reshape(2, H)
        dsk_h = dconv_x[CONV_K + 1].reshape(H, HEADDIM).sum(axis=-1)
        small_g["d_skip"][l] = jnp.stack([dsk_h, dsk_h])
        small_g["ssd_norm_w"][l] = dnw[0]
        small_g["pool_scale"][l] = dps[0]
        small_g["g_ffn"][l] = st2[4]

        last_gi = len(rs_groups[l]) - 1
        if l > 0:
            pair_waiting = ((l, last_gi), pack_group(l, last_gi))
        else:
            pending = ((l, last_gi), reduce_pair(l, last_gi))
    slots[pending[0]] = _chip_exchange(pending[1], "rs_chip_exchange")
    g_local = {key: _sum_slots(s, "rs_chip_add") for key, s in slots.items()}
    grad_x = dx[n_ctx:][None]

    def local_grad(name):
        outs = []
        for l in range(depth):
            gi = [name in grp for grp in rs_groups[l]].index(True)
            _, _, rows, _, off = rlays[l][gi].find(name, l)
            piece = g_local[(l, gi)][off:off + rows]
            if name in ("w_ada", "w_in", "w_gate_up"):
                piece = piece.T
            elif name == "conv_w":
                piece = piece.reshape(16, -1)[:CONV_K]
            elif name == "pool_w":
                piece = piece.reshape(weights[name].shape[1:])
            outs.append(piece)
        return jnp.stack(outs)

    grads = {n: local_grad(n) for n in big}

    small_full = {"c_ctx": d_c_ctx, "g_final": dgf[0]}
    for n in small:
        if n not in small_full:
            small_full[n] = jnp.stack(small_g[n])

    def pack_small(tree):
        flat = jnp.concatenate([tree[n].reshape(-1).astype(F32) for n in small])
        rows = _round_up(-(-flat.shape[0] // D), 8)
        return jnp.pad(flat, (0, rows * D - flat.shape[0])).reshape(rows, D)

    def unpack_small(buf):
        flat, out, off = buf.reshape(-1), {}, 0
        for n in small:
            sz = weights[n].size
            out[n] = flat[off:off + sz].reshape(weights[n].shape)
            off += sz
        return out

    g_small = _sum_slots(_all_gather(pack_small(small_full), "gather_small_grads"), "sum_small_grads")
    grads.update(unpack_small(g_small))

    delta, new_m, new_v = {}, {}, {}
    for n in big:
        shp = weights[n].shape
        d_, m_, v_ = _adamw(weights[n].reshape(-1, shp[-1]), grads[n].reshape(-1, shp[-1]),
                            moms_m[n].reshape(-1, shp[-1]), moms_v[n].reshape(-1, shp[-1]), "adamw_" + n)
        delta[n], new_m[n], new_v[n] = d_.reshape(shp), m_.reshape(shp), v_.reshape(shp)
    d_, m_, v_ = _adamw(pack_small(weights), g_small, pack_small(moms_m), pack_small(moms_v), "adamw_small")
    delta.update(unpack_small(d_))
    new_m.update(unpack_small(m_))
    new_v.update(unpack_small(v_))

    return (loss, grad_x, *[grads[n] for n in order], *[delta[n] for n in order],
            *[new_m[n] for n in order], *[new_v[n] for n in order])
```
